```python
import jax, jax.numpy as jnp
from jax import lax
import numpy as np

D_MODEL = 1024
BATCH = 8
SEQ = 2048
DEPTH = 1
DEC_BATCH = 128
DEC_SEQ = 1
PAST_LEN = 16384
PAGE_SIZE = 128

HEAD_DIM = 64
N_Q_HEADS = 8
N_KV_HEADS = 2
GQA_GROUP = N_Q_HEADS // N_KV_HEADS
WINDOW = 128
ATTN_BLOCK = WINDOW
ROPE_THETA = 10000.0
ML_HEADS = 4
ML_DIM = 128
ML_CHUNK = 128
D_FF = 4 * D_MODEL
D_PLE = 256
EPS = 1e-6

ATTN_Q = N_Q_HEADS * HEAD_DIM
ATTN_KV = N_KV_HEADS * HEAD_DIM
ML_W = ML_HEADS * ML_DIM
SPLITS = (ATTN_Q, ATTN_KV, ATTN_KV, ML_W, ML_W, ML_W, ML_W, ML_HEADS, ML_HEADS, D_MODEL, D_MODEL)
D_IN = ATTN_Q + 2 * ATTN_KV + 4 * ML_W + 2 * ML_HEADS + 2 * D_MODEL
F_GATE_OFF = ATTN_Q + 2 * ATTN_KV + 4 * ML_W + ML_HEADS

kernel_name = 'hybrid_swa_sink_mlstm_decoder_step'


def rmsnorm(x, g):
    xf = x.astype(jnp.float32)
    r = lax.rsqrt(jnp.mean(xf * xf, axis=-1, keepdims=True) + EPS)
    return (xf * r).astype(x.dtype) * g


def rope(x, pos):
    half = HEAD_DIM // 2
    inv = 1.0 / (ROPE_THETA ** (jnp.arange(half, dtype=jnp.float32) / half))
    ang = pos.astype(jnp.float32)[:, None] * inv[None, :]
    cos = jnp.cos(ang)[:, None, :]
    sin = jnp.sin(ang)[:, None, :]
    xf = x.astype(jnp.float32)
    x1, x2 = xf[..., :half], xf[..., half:]
    return jnp.concatenate([x1 * cos - x2 * sin, x2 * cos + x1 * sin], axis=-1).astype(x.dtype)


def sink_softmax(s, sinks):
    sink_col = jnp.broadcast_to(sinks.astype(jnp.float32), s.shape[:-1] + (1,))
    return jax.nn.softmax(jnp.concatenate([s, sink_col], axis=-1), axis=-1)[..., :-1]


def swa_banded(q, k, v, sinks):
    B, S = q.shape[:2]
    L = ATTN_BLOCK
    nb = S // L
    qb = q.reshape(B, nb, L, N_KV_HEADS, GQA_GROUP, HEAD_DIM).astype(jnp.float32)
    kb = k.reshape(B, nb, L, N_KV_HEADS, HEAD_DIM)
    vb = v.reshape(B, nb, L, N_KV_HEADS, HEAD_DIM)
    pad = ((0, 0), (1, 0), (0, 0), (0, 0), (0, 0))
    kk = jnp.concatenate([jnp.pad(kb, pad)[:, :-1], kb], axis=2).astype(jnp.float32)
    vv = jnp.concatenate([jnp.pad(vb, pad)[:, :-1], vb], axis=2)
    qpos = jnp.arange(S).reshape(nb, L)
    kpos = jnp.concatenate([qpos - L, qpos], axis=1)
    diff = qpos[:, :, None] - kpos[:, None, :]
    valid = (diff >= 0) & (diff <= WINDOW) & (kpos[:, None, :] >= 0)
    s = jnp.einsum('bnqhgd,bnkhd->bnhgqk', qb, kk) * (HEAD_DIM ** -0.5)
    s = jnp.where(valid[None, :, None, None], s, -jnp.inf)
    p = sink_softmax(s, sinks.reshape(N_KV_HEADS, GQA_GROUP, 1, 1))
    o = jnp.einsum('bnhgqk,bnkhd->bnqhgd', p.astype(v.dtype), vv)
    return o.reshape(B, S, ATTN_Q)


def swa_decode(q, k_new, v_new, cache_k, cache_v, sinks):
    B, T = q.shape[:2]
    kk = jnp.concatenate([cache_k, k_new], axis=1)
    vv = jnp.concatenate([cache_v, v_new], axis=1)
    qpos = PAST_LEN + jnp.arange(T)
    kpos = PAST_LEN - WINDOW + jnp.arange(WINDOW + T)
    diff = qpos[:, None] - kpos[None, :]
    valid = (diff >= 0) & (diff <= WINDOW)
    qg = q.reshape(B, T, N_KV_HEADS, GQA_GROUP, HEAD_DIM).astype(jnp.float32)
    s = jnp.einsum('bqhgd,bkhd->bhgqk', qg, kk.astype(jnp.float32)) * (HEAD_DIM ** -0.5)
    s = jnp.where(valid, s, -jnp.inf)
    p = sink_softmax(s, sinks.reshape(N_KV_HEADS, GQA_GROUP, 1, 1))
    o = jnp.einsum('bhgqk,bkhd->bqhgd', p.astype(vv.dtype), vv).reshape(B, T, ATTN_Q)
    return o, kk[:, T:], vv[:, T:]


def _mlstm_chunk_step(carry, inp):
    C, n, m = carry
    q, k, v, ig, lf = inp
    L = q.shape[2]
    b = jnp.cumsum(lf, axis=-1)
    causal = jnp.tril(jnp.ones((L, L), dtype=bool))
    dlog = jnp.where(causal, b[..., :, None] - b[..., None, :] + ig[..., None, :], -jnp.inf)
    a = b + m[..., None]
    m_t = jnp.maximum(a, jnp.max(dlog, axis=-1))
    aw = jnp.exp(a - m_t)
    sw = jnp.einsum('bhtd,bhsd->bhts', q, k) * jnp.exp(dlog - m_t[..., None])
    num = aw[..., None] * jnp.einsum('bhtd,bhde->bhte', q, C) + jnp.einsum('bhts,bhse->bhte', sw, v)
    den = aw * jnp.einsum('bhtd,bhd->bht', q, n) + jnp.sum(sw, axis=-1)
    h = num / jnp.maximum(jnp.abs(den), jnp.exp(-m_t))[..., None]
    b_end = b[..., -1]
    g = b_end[..., None] - b + ig
    m_new = jnp.maximum(b_end + m, jnp.max(g, axis=-1))
    decay = jnp.exp(b_end + m - m_new)
    w = jnp.exp(g - m_new[..., None])
    C_new = decay[..., None, None] * C + jnp.einsum('bhs,bhsd,bhse->bhde', w, k, v)
    n_new = decay[..., None] * n + jnp.einsum('bhs,bhsd->bhd', w, k)
    return (C_new, n_new, m_new), h


def mlstm_chunkwise(q, k, v, ig, lf, C0, n0, m0, chunk):
    B, H, T, _ = q.shape
    nc = T // chunk

    def to_chunks(t):
        return jnp.moveaxis(t.reshape((B, H, nc, chunk) + t.shape[3:]), 2, 0)

    (C, n, m), hs = lax.scan(_mlstm_chunk_step, (C0, n0, m0),
                             (to_chunks(q), to_chunks(k), to_chunks(v), to_chunks(ig), to_chunks(lf)))
    return jnp.moveaxis(hs, 0, 2).reshape(B, H, T, ML_DIM), (C, n, m)


def _layer(h, pl, pos, kv_past, ml_state, norm_mix, w_in, b_in, attn_sinks, w_attn_up, w_ml_up, w_o,
           norm_mlp, w_ff1, w_ff2, norm_ple, w_ple_gate, w_ple_proj):
    B, T, _ = h.shape
    xn = rmsnorm(h, norm_mix)
    z = xn @ w_in + b_in
    aq, ak, av, mq, mk, mv, mo, mi, mf, ga, gm = jnp.split(z, np.cumsum(SPLITS)[:-1].tolist(), axis=-1)
    aq = rope(aq.reshape(B, T, N_Q_HEADS, HEAD_DIM), pos)
    ak = rope(ak.reshape(B, T, N_KV_HEADS, HEAD_DIM), pos)
    av = av.reshape(B, T, N_KV_HEADS, HEAD_DIM)
    if kv_past is None:
        a_out = swa_banded(aq, ak, av, attn_sinks)
        k_state, v_state = ak[:, T - WINDOW:], av[:, T - WINDOW:]
    else:
        a_out, k_state, v_state = swa_decode(aq, ak, av, kv_past[0], kv_past[1], attn_sinks)
    def heads(t):
        return t.reshape(B, T, ML_HEADS, ML_DIM).transpose(0, 2, 1, 3).astype(jnp.float32)
    q_m = heads(mq)
    k_m = heads(mk) * (ML_DIM ** -0.5)
    v_m = heads(mv)
    ig = mi.astype(jnp.float32).transpose(0, 2, 1)
    lf = jax.nn.log_sigmoid(mf.astype(jnp.float32)).transpose(0, 2, 1)
    if ml_state is None:
        C0 = jnp.zeros((B, ML_HEADS, ML_DIM, ML_DIM), jnp.float32)
        n0 = jnp.zeros((B, ML_HEADS, ML_DIM), jnp.float32)
        m0 = jnp.zeros((B, ML_HEADS), jnp.float32)
        chunk = ML_CHUNK
    else:
        C0, n0, m0 = (s.astype(jnp.float32) for s in ml_state)
        chunk = T
    h_m, (C, n, m) = mlstm_chunkwise(q_m, k_m, v_m, ig, lf, C0, n0, m0, chunk)
    h_m = h_m.transpose(0, 2, 1, 3).reshape(B, T, ML_W).astype(h.dtype) * jax.nn.sigmoid(mo)
    mix = jax.nn.sigmoid(ga) * (a_out @ w_attn_up) + jax.nn.sigmoid(gm) * (h_m @ w_ml_up)
    h = h + mix @ w_o
    u = jnp.square(jax.nn.relu(rmsnorm(h, norm_mlp) @ w_ff1))
    h = h + u @ w_ff2
    h = h + (pl @ w_ple_proj) * jax.nn.sigmoid(rmsnorm(h, norm_ple) @ w_ple_gate)
    dt = h.dtype
    return h, (k_state, v_state, C.astype(dt), n.astype(dt), m.astype(dt))


def setup_inputs(seed: int = 0) -> dict:
    key = jax.random.key(seed)
    ks = iter(jax.random.split(key, 32))

    def nrm(shape, scale):
        return jax.random.normal(next(ks), shape, jnp.float32) * scale

    f_bias = jnp.zeros((D_IN,), jnp.float32).at[F_GATE_OFF:F_GATE_OFF + ML_HEADS].set(
        jnp.linspace(3.0, 6.0, ML_HEADS, dtype=jnp.float32))
    return {
        'x_prompt': nrm((BATCH, SEQ, D_MODEL), 1.0),
        'x_sample': nrm((DEC_BATCH, DEC_SEQ, D_MODEL), 1.0),
        'cache_k': nrm((DEPTH, DEC_BATCH, WINDOW, N_KV_HEADS, HEAD_DIM), 1.0),
        'cache_v': nrm((DEPTH, DEC_BATCH, WINDOW, N_KV_HEADS, HEAD_DIM), 1.0),
        'state_C': nrm((DEPTH, DEC_BATCH, ML_HEADS, ML_DIM, ML_DIM), 1.0),
        'state_n': nrm((DEPTH, DEC_BATCH, ML_HEADS, ML_DIM), 1.0),
        'state_m': nrm((DEPTH, DEC_BATCH, ML_HEADS), 0.5),
        'p_prompt': nrm((DEPTH, BATCH, SEQ, D_PLE), 1.0),
        'p_sample': nrm((DEPTH, DEC_BATCH, DEC_SEQ, D_PLE), 1.0),
        'norm_mix': 1.0 + nrm((DEPTH, D_MODEL), 0.01),
        'w_in': nrm((DEPTH, D_MODEL, D_IN), D_MODEL ** -0.5),
        'b_in': nrm((DEPTH, D_IN), 0.01) + f_bias,
        'attn_sinks': nrm((DEPTH, N_Q_HEADS), 0.5),
        'w_attn_up': nrm((DEPTH, ATTN_Q, D_MODEL), ATTN_Q ** -0.5),
        'w_ml_up': nrm((DEPTH, ML_W, D_MODEL), ML_W ** -0.5),
        'w_o': nrm((DEPTH, D_MODEL, D_MODEL), D_MODEL ** -0.5),
        'norm_mlp': 1.0 + nrm((DEPTH, D_MODEL), 0.01),
        'w_ff1': nrm((DEPTH, D_MODEL, D_FF), D_MODEL ** -0.5),
        'w_ff2': nrm((DEPTH, D_FF, D_MODEL), D_FF ** -0.5),
        'norm_ple': 1.0 + nrm((DEPTH, D_MODEL), 0.01),
        'w_ple_gate': nrm((DEPTH, D_MODEL, D_MODEL), D_MODEL ** -0.5),
        'w_ple_proj': nrm((DEPTH, D_PLE, D_MODEL), D_PLE ** -0.5),
        'norm_final': 1.0 + nrm((D_MODEL,), 0.01),
    }


def reference(x_prompt, x_sample, cache_k, cache_v, state_C, state_n, state_m, p_prompt, p_sample,
              norm_mix, w_in, b_in, attn_sinks, w_attn_up, w_ml_up, w_o, norm_mlp, w_ff1, w_ff2,
              norm_ple, w_ple_gate, w_ple_proj, norm_final):
    pos_p = jnp.arange(x_prompt.shape[1], dtype=jnp.int32)
    pos_s = PAST_LEN + jnp.arange(x_sample.shape[1], dtype=jnp.int32)
    hp, hs = x_prompt, x_sample
    sp, ss = [], []
    for l in range(DEPTH):
        w = (norm_mix[l], w_in[l], b_in[l], attn_sinks[l], w_attn_up[l], w_ml_up[l], w_o[l],
             norm_mlp[l], w_ff1[l], w_ff2[l], norm_ple[l], w_ple_gate[l], w_ple_proj[l])
        hp, st_p = _layer(hp, p_prompt[l], pos_p, None, None, *w)
        hs, st_s = _layer(hs, p_sample[l], pos_s, (cache_k[l], cache_v[l]),
                          (state_C[l], state_n[l], state_m[l]), *w)
        sp.append(st_p)
        ss.append(st_s)
    y_prompt = rmsnorm(hp, norm_final)
    y_sample = rmsnorm(hs, norm_final)
    k_p, v_p, C_p, n_p, m_p = (jnp.stack([s[i] for s in sp]) for i in range(5))
    k_s, v_s, C_s, n_s, m_s = (jnp.stack([s[i] for s in ss]) for i in range(5))
    return (y_prompt, y_sample, k_p, v_p, C_p, n_p, m_p, k_s, v_s, C_s, n_s, m_s)
```

```python
import functools

import numpy as np
import jax
import jax.numpy as jnp
from jax import lax
from jax.experimental import pallas as pl
from jax.experimental.pallas import tpu as pltpu

F32 = jnp.float32
BF16 = jnp.bfloat16

D_MODEL = 1024
HEAD_DIM = 64
N_Q_HEADS = 8
N_KV_HEADS = 2
GQA_GROUP = N_Q_HEADS // N_KV_HEADS
WINDOW = 128
ROPE_THETA = 10000.0
ML_HEADS = 4
ML_DIM = 128
ML_CHUNK = 128
D_FF = 4 * D_MODEL
D_PLE = 256
EPS = 1e-6
PAST_LEN = 16384

ATTN_Q = N_Q_HEADS * HEAD_DIM
ATTN_KV = N_KV_HEADS * HEAD_DIM
ML_W = ML_HEADS * ML_DIM
LANES = 128

OFF_AQ = 0
OFF_AK = OFF_AQ + ATTN_Q
OFF_AV = OFF_AK + ATTN_KV
OFF_MQ = OFF_AV + ATTN_KV
OFF_MK = OFF_MQ + ML_W
OFF_MV = OFF_MK + ML_W
OFF_MO = OFF_MV + ML_W
OFF_MI = OFF_MO + ML_W
OFF_MF = OFF_MI + ML_HEADS
OFF_GA = OFF_MF + ML_HEADS
OFF_GM = OFF_GA + D_MODEL
D_IN = OFF_GM + D_MODEL
N_PROJ = OFF_MI + LANES

VMEM_LIMIT = 60 * 1024 * 1024


def _rms(x, g):
    r = lax.rsqrt(jnp.mean(x * x, axis=-1, keepdims=True) + EPS)
    return (x * r) * g


def _mm(a, b):
    return jnp.dot(a, b, preferred_element_type=F32)


def _mm_nt(a, b):
    return lax.dot_general(a, b, (((1,), (1,)), ((), ())), preferred_element_type=F32)


def _mm_tn(a, b):
    return lax.dot_general(a, b, (((0,), (0,)), ((), ())), preferred_element_type=F32)


def _log_sigmoid(x):
    return jnp.minimum(x, 0.0) - jnp.log1p(jnp.exp(-jnp.abs(x)))


def _in_proj_kernel(x_ref, g_ref, w_ref, b_ref, cos_ref, sin_ref,
                    q_ref, k_ref, v_ref, mq_ref, mk_ref, mv_ref, og_ref, gif_ref):
    xn = _rms(x_ref[...], g_ref[...]).astype(BF16)

    def proj(lo, hi):
        return _mm(xn, w_ref[:, lo:hi]) + b_ref[:, lo:hi]

    cos = cos_ref[...]
    sin = sin_ref[...]
    lane = lax.broadcasted_iota(jnp.int32, (1, LANES), 1)
    first_half = (lane % HEAD_DIM) < (HEAD_DIM // 2)

    def rope(z):
        partner = jnp.where(first_half, pltpu.roll(z, LANES - HEAD_DIM // 2, 1), pltpu.roll(z, HEAD_DIM // 2, 1))
        return z * cos + partner * sin

    zq = proj(OFF_AQ, OFF_AK)
    for c in range(ATTN_Q // LANES):
        sl = slice(c * LANES, (c + 1) * LANES)
        q_ref[:, sl] = (rope(zq[:, sl]) * (HEAD_DIM ** -0.5)).astype(q_ref.dtype)
    zkv = proj(OFF_AK, OFF_MQ)
    k_ref[...] = rope(zkv[:, :ATTN_KV])
    v_ref[...] = zkv[:, ATTN_KV:]
    mq_ref[...] = proj(OFF_MQ, OFF_MK).astype(mq_ref.dtype)
    mk_ref[...] = (proj(OFF_MK, OFF_MV) * (ML_DIM ** -0.5)).astype(mk_ref.dtype)
    mv_ref[...] = proj(OFF_MV, OFF_MO).astype(mv_ref.dtype)
    og_ref[...] = jax.nn.sigmoid(proj(OFF_MO, OFF_MI))
    zg = proj(OFF_MI, N_PROJ)
    gif_ref[...] = jnp.where(lane < ML_HEADS, zg, jnp.where(lane < 2 * ML_HEADS, _log_sigmoid(zg), 0.0))


def _in_proj(x, g, w, b, cos, sin, tm, m_dtype):
    rows = x.shape[0]
    n_pos_blocks = cos.shape[0] // tm
    row = lambda i: (i, 0)
    const = lambda i: (0, 0)
    pos = lambda i: (i % n_pos_blocks, 0)
    out_shape = (
        jax.ShapeDtypeStruct((rows, ATTN_Q), BF16),
        jax.ShapeDtypeStruct((rows, ATTN_KV), F32),
        jax.ShapeDtypeStruct((rows, ATTN_KV), F32),
        jax.ShapeDtypeStruct((rows, ML_W), m_dtype),
        jax.ShapeDtypeStruct((rows, ML_W), m_dtype),
        jax.ShapeDtypeStruct((rows, ML_W), m_dtype),
        jax.ShapeDtypeStruct((rows, ML_W), F32),
        jax.ShapeDtypeStruct((rows, LANES), F32),
    )
    return pl.pallas_call(
        _in_proj_kernel,
        grid=(rows // tm,),
        in_specs=[
            pl.BlockSpec((tm, D_MODEL), row),
            pl.BlockSpec((1, D_MODEL), const),
            pl.BlockSpec((D_MODEL, N_PROJ), const),
            pl.BlockSpec((1, N_PROJ), const),
            pl.BlockSpec((tm, LANES), pos),
            pl.BlockSpec((tm, LANES), pos),
        ],
        out_specs=tuple(pl.BlockSpec((tm, s.shape[1]), row) for s in out_shape),
        out_shape=out_shape,
        compiler_params=pltpu.CompilerParams(dimension_semantics=("parallel",), vmem_limit_bytes=VMEM_LIMIT),
        name="in_proj",
    )(x, g, w, b, cos, sin)


def _attn_kernel(sink_ref, q_ref, kp_ref, kc_ref, vp_ref, vc_ref, o_ref):
    j = pl.program_id(1)
    L = WINDOW
    t = lax.broadcasted_iota(jnp.int32, (L, L), 0)
    i = lax.broadcasted_iota(jnp.int32, (L, L), 1)
    mask_c = i <= t
    mask_p = (i >= t) & (j > 0)
    kp = kp_ref[...].astype(BF16)
    kc = kc_ref[...].astype(BF16)
    vp = vp_ref[...].astype(BF16)
    vc = vc_ref[...].astype(BF16)
    outs = []
    for hd in range(N_Q_HEADS):
        kv = hd // GQA_GROUP
        ks = slice(kv * HEAD_DIM, (kv + 1) * HEAD_DIM)
        qh = q_ref[:, hd * HEAD_DIM:(hd + 1) * HEAD_DIM]
        sp = jnp.where(mask_p, _mm_nt(qh, kp[:, ks]), -jnp.inf)
        sc = jnp.where(mask_c, _mm_nt(qh, kc[:, ks]), -jnp.inf)
        sink = sink_ref[hd]
        m = jnp.maximum(jnp.maximum(jnp.max(sp, axis=1, keepdims=True), jnp.max(sc, axis=1, keepdims=True)), sink)
        pp = jnp.exp(sp - m)
        pc = jnp.exp(sc - m)
        den = jnp.sum(pp, axis=1, keepdims=True) + jnp.sum(pc, axis=1, keepdims=True) + jnp.exp(sink - m)
        o = _mm(pp.astype(BF16), vp[:, ks]) + _mm(pc.astype(BF16), vc[:, ks])
        outs.append(o / den)
    o_ref[...] = jnp.concatenate(outs, axis=1).astype(o_ref.dtype)


def _attn_prompt(sinks, q, k, v, batch, seq):
    nb = seq // WINDOW
    cur = lambda b, j: (b * nb + j, 0)
    prev = lambda b, j: (b * nb + jnp.maximum(j - 1, 0), 0)
    return pl.pallas_call(
        _attn_kernel,
        grid=(batch, nb),
        in_specs=[
            pl.BlockSpec(memory_space=pltpu.SMEM),
            pl.BlockSpec((WINDOW, ATTN_Q), cur),
            pl.BlockSpec((WINDOW, ATTN_KV), prev),
            pl.BlockSpec((WINDOW, ATTN_KV), cur),
            pl.BlockSpec((WINDOW, ATTN_KV), prev),
            pl.BlockSpec((WINDOW, ATTN_KV), cur),
        ],
        out_specs=pl.BlockSpec((WINDOW, ATTN_Q), cur),
        out_shape=jax.ShapeDtypeStruct((batch * seq, ATTN_Q), BF16),
        compiler_params=pltpu.CompilerParams(dimension_semantics=("parallel", "parallel")),
        name="attn_prompt",
    )(sinks, q, k, k, v, v)


def _mlstm_kernel(q_ref, k_ref, v_ref, og_ref, gif_ref, h_ref, cst_ref, mst_ref, c_scr, m_scr):
    c = pl.program_id(1)
    L = ML_CHUNK

    @pl.when(c == 0)
    def _():
        c_scr[...] = jnp.zeros_like(c_scr)
        m_scr[...] = jnp.zeros_like(m_scr)

    G = gif_ref[...]
    row = lax.broadcasted_iota(jnp.int32, (L, L), 0)
    col = lax.broadcasted_iota(jnp.int32, (L, L), 1)
    causal = col <= row
    cum = jnp.dot(causal.astype(F32), G, precision=lax.Precision.HIGHEST, preferred_element_type=F32)
    A = jnp.where(col < ML_HEADS, G, cum)
    AT = A.T
    ones_col = (col == 0).astype(BF16)
    for h in range(ML_HEADS):
        hs = slice(h * ML_DIM, (h + 1) * ML_DIM)
        ig_c = A[:, h:h + 1]
        b_c = A[:, ML_HEADS + h:ML_HEADS + h + 1]
        ig_r = AT[h:h + 1, :]
        b_r = AT[ML_HEADS + h:ML_HEADS + h + 1, :]
        m_prev = m_scr[h:h + 1, 0:1]
        dlog = jnp.where(causal, b_c - b_r + ig_r, -jnp.inf)
        a = b_c + m_prev
        m_t = jnp.maximum(a, jnp.max(dlog, axis=1, keepdims=True))
        aw = jnp.exp(a - m_t)
        qh = q_ref[:, hs]
        kh = k_ref[:, hs]
        vh = v_ref[:, hs]
        sw = _mm_nt(qh, kh) * jnp.exp(dlog - m_t)
        state = c_scr[h]
        qc = _mm(qh, state.astype(BF16))
        num = aw * qc[:, :ML_DIM] + _mm(sw.astype(BF16), vh)
        den = aw * qc[:, ML_DIM:ML_DIM + 1] + jnp.sum(sw, axis=1, keepdims=True)
        hh = num / jnp.maximum(jnp.abs(den), jnp.exp(-m_t))
        h_ref[:, hs] = (hh * og_ref[:, hs]).astype(h_ref.dtype)
        b_end = b_c[L - 1:L, :]
        g = b_end - b_c + ig_c
        m_new = jnp.maximum(b_end + m_prev, jnp.max(g, axis=0, keepdims=True))
        decay = jnp.exp(b_end + m_prev - m_new)
        w = jnp.exp(g - m_new)
        wk = (w * kh.astype(F32)).astype(BF16)
        v_aug = jnp.concatenate([vh, ones_col], axis=1)
        c_scr[h] = decay * state + _mm_tn(wk, v_aug)
        m_scr[h:h + 1, :] = jnp.broadcast_to(m_new, (1, LANES))

    @pl.when(c == pl.num_programs(1) - 1)
    def _():
        cst_ref[0] = c_scr[...]
        mst_ref[0] = m_scr[...]


def _mlstm_prompt(mq, mk, mv, og, gif, batch, seq):
    nc = seq // ML_CHUNK
    blk = lambda b, c: (b * nc + c, 0)
    return pl.pallas_call(
        _mlstm_kernel,
        grid=(batch, nc),
        in_specs=[
            pl.BlockSpec((ML_CHUNK, ML_W), blk),
            pl.BlockSpec((ML_CHUNK, ML_W), blk),
            pl.BlockSpec((ML_CHUNK, ML_W), blk),
            pl.BlockSpec((ML_CHUNK, ML_W), blk),
            pl.BlockSpec((ML_CHUNK, LANES), blk),
        ],
        out_specs=(
            pl.BlockSpec((ML_CHUNK, ML_W), blk),
            pl.BlockSpec((1, ML_HEADS, ML_DIM, 2 * ML_DIM), lambda b, c: (b, 0, 0, 0)),
            pl.BlockSpec((1, 8, LANES), lambda b, c: (b, 0, 0)),
        ),
        out_shape=(
            jax.ShapeDtypeStruct((batch * seq, ML_W), BF16),
            jax.ShapeDtypeStruct((batch, ML_HEADS, ML_DIM, 2 * ML_DIM), F32),
            jax.ShapeDtypeStruct((batch, 8, LANES), F32),
        ),
        scratch_shapes=[pltpu.VMEM((ML_HEADS, ML_DIM, 2 * ML_DIM), F32), pltpu.VMEM((8, LANES), F32)],
        compiler_params=pltpu.CompilerParams(dimension_semantics=("parallel", "arbitrary")),
        name="mlstm_prompt",
    )(mq, mk, mv, og, gif)


def _out_kernel(x_ref, a_ref, hm_ref, p_ref, gmix_ref, wg_ref, bg_ref, wau_ref, wmu_ref, wo_ref,
                gmlp_ref, wff1_ref, wff2_ref, gple_ref, wpg_ref, wpp_ref, gfin_ref, y_ref):
    x = x_ref[...]
    xn = _rms(x, gmix_ref[...]).astype(BF16)
    gates = jax.nn.sigmoid(_mm(xn, wg_ref[...]) + bg_ref[...])
    mix = gates[:, :D_MODEL] * _mm(a_ref[...], wau_ref[...]) + gates[:, D_MODEL:] * _mm(hm_ref[...], wmu_ref[...])
    h = x + _mm(mix.astype(BF16), wo_ref[...])
    u = jnp.square(jnp.maximum(_mm(_rms(h, gmlp_ref[...]).astype(BF16), wff1_ref[...]), 0.0))
    h = h + _mm(u.astype(BF16), wff2_ref[...])
    gate = jax.nn.sigmoid(_mm(_rms(h, gple_ref[...]).astype(BF16), wpg_ref[...]))
    h = h + _mm(p_ref[...].astype(BF16), wpp_ref[...]) * gate
    y_ref[...] = _rms(h, gfin_ref[...])


def _out_proj(x, a, hm, p, weights, tm):
    rows = x.shape[0]
    row = lambda i: (i, 0)
    const = lambda i: (0, 0)
    w_specs = [pl.BlockSpec(w.shape, const, pipeline_mode=pl.Buffered(1)) for w in weights]
    return pl.pallas_call(
        _out_kernel,
        grid=(rows // tm,),
        in_specs=[
            pl.BlockSpec((tm, D_MODEL), row),
            pl.BlockSpec((tm, ATTN_Q), row),
            pl.BlockSpec((tm, ML_W), row),
            pl.BlockSpec((tm, D_PLE), row),
        ] + w_specs,
        out_specs=pl.BlockSpec((tm, D_MODEL), row),
        out_shape=jax.ShapeDtypeStruct((rows, D_MODEL), F32),
        compiler_params=pltpu.CompilerParams(dimension_semantics=("parallel",), vmem_limit_bytes=VMEM_LIMIT),
        name="out_proj",
    )(x, a, hm, p, *weights)


def _attn_decode_kernel(sink_ref, qm_ref, kn_ref, vn_ref, ck_ref, cv_ref, o_ref, ko_ref, vo_ref):
    qm = qm_ref[...]
    kn = kn_ref[...]
    vn = vn_ref[...]
    ck = ck_ref[...]
    cv = cv_ref[...]
    sink = sink_ref[...]
    s = jnp.einsum('bhc,bkc->bhk', qm, ck.astype(BF16), preferred_element_type=F32)
    s_new = jnp.sum(qm.astype(F32) * kn[:, None, :], axis=2, keepdims=True)
    m = jnp.maximum(jnp.maximum(jnp.max(s, axis=2, keepdims=True), s_new), sink)
    p = jnp.exp(s - m)
    p_new = jnp.exp(s_new - m)
    den = jnp.sum(p, axis=2, keepdims=True) + p_new + jnp.exp(sink - m)
    o = jnp.einsum('bhk,bkc->bhc', p.astype(BF16), cv.astype(BF16), preferred_element_type=F32)
    o_ref[...] = (o + p_new * vn[:, None, :]) / den
    ko_ref[:, 0:WINDOW - 1, :] = ck[:, 1:WINDOW, :]
    ko_ref[:, WINDOW - 1:WINDOW, :] = kn[:, None, :]
    vo_ref[:, 0:WINDOW - 1, :] = cv[:, 1:WINDOW, :]
    vo_ref[:, WINDOW - 1:WINDOW, :] = vn[:, None, :]


def _attn_decode(sinks_col, qmat, k_new, v_new, cache_k, cache_v, bt):
    nbatch = qmat.shape[0]
    b3 = lambda i: (i, 0, 0)
    b2 = lambda i: (i, 0)
    return pl.pallas_call(
        _attn_decode_kernel,
        grid=(nbatch // bt,),
        in_specs=[
            pl.BlockSpec((N_Q_HEADS, 1), lambda i: (0, 0)),
            pl.BlockSpec((bt, N_Q_HEADS, ATTN_KV), b3),
            pl.BlockSpec((bt, ATTN_KV), b2),
            pl.BlockSpec((bt, ATTN_KV), b2),
            pl.BlockSpec((bt, WINDOW, ATTN_KV), b3),
            pl.BlockSpec((bt, WINDOW, ATTN_KV), b3),
        ],
        out_specs=(
            pl.BlockSpec((bt, N_Q_HEADS, ATTN_KV), b3),
            pl.BlockSpec((bt, WINDOW, ATTN_KV), b3),
            pl.BlockSpec((bt, WINDOW, ATTN_KV), b3),
        ),
        out_shape=(
            jax.ShapeDtypeStruct((nbatch, N_Q_HEADS, ATTN_KV), F32),
            jax.ShapeDtypeStruct((nbatch, WINDOW, ATTN_KV), F32),
            jax.ShapeDtypeStruct((nbatch, WINDOW, ATTN_KV), F32),
        ),
        compiler_params=pltpu.CompilerParams(dimension_semantics=("parallel",)),
        name="attn_decode",
    )(sinks_col, qmat, k_new, v_new, cache_k, cache_v)


def _mlstm_decode_kernel(q_ref, k_ref, v_ref, og_ref, gif_ref, c_ref, n_ref, m_ref, h_ref, co_ref, no_ref, mo_ref):
    bt = q_ref.shape[0]
    gif = gif_ref[...]
    ig = gif[:, 0:ML_HEADS]
    lf = gif[:, ML_HEADS:2 * ML_HEADS]
    m_prev = m_ref[...]
    a = lf + m_prev
    m_t = jnp.maximum(a, ig)
    aw = jnp.exp(a - m_t)
    e_i = jnp.exp(ig - m_t)
    floor = jnp.exp(-m_t)
    m_new = jnp.maximum(a, ig)
    decay = jnp.exp(a - m_new)
    w = jnp.exp(ig - m_new)
    mo_ref[...] = m_new
    pad = jnp.zeros((LANES - bt, ML_DIM), F32)
    for h in range(ML_HEADS):
        hs = slice(h * ML_DIM, (h + 1) * ML_DIM)
        q = q_ref[:, hs]
        k = k_ref[:, hs]
        v = v_ref[:, hs]
        n = n_ref[:, h, :]
        qk = jnp.sum(q * k, axis=1, keepdims=True)
        qn = jnp.sum(q * n, axis=1, keepdims=True)
        sw = qk * e_i[:, h:h + 1]
        den = aw[:, h:h + 1] * qn + sw
        inv = 1.0 / jnp.maximum(jnp.abs(den), floor[:, h:h + 1])
        no_ref[:, h, :] = decay[:, h:h + 1] * n + w[:, h:h + 1] * k
        qT = jnp.concatenate([q, pad], axis=0).T
        kT = jnp.concatenate([w[:, h:h + 1] * k, pad], axis=0).T
        for b in range(bt):
            C = c_ref[b, h]
            qC = jnp.sum(qT[:, b:b + 1] * C, axis=0, keepdims=True)
            num = aw[b:b + 1, h:h + 1] * qC + sw[b:b + 1, :] * v[b:b + 1, :]
            h_ref[b:b + 1, hs] = (num * inv[b:b + 1, :] * og_ref[b:b + 1, hs]).astype(h_ref.dtype)
            co_ref[b, h] = decay[b:b + 1, h:h + 1] * C + kT[:, b:b + 1] * v[b:b + 1, :]


def _mlstm_decode(mq, mk, mv, og, gif, state_c, state_n, state_m, bt):
    nbatch = mq.shape[0]
    b2 = lambda i: (i, 0)
    b3 = lambda i: (i, 0, 0)
    b4 = lambda i: (i, 0, 0, 0)
    return pl.pallas_call(
        _mlstm_decode_kernel,
        grid=(nbatch // bt,),
        in_specs=[
            pl.BlockSpec((bt, ML_W), b2),
            pl.BlockSpec((bt, ML_W), b2),
            pl.BlockSpec((bt, ML_W), b2),
            pl.BlockSpec((bt, ML_W), b2),
            pl.BlockSpec((bt, LANES), b2),
            pl.BlockSpec((bt, ML_HEADS, ML_DIM, ML_DIM), b4),
            pl.BlockSpec((bt, ML_HEADS, ML_DIM), b3),
            pl.BlockSpec((bt, ML_HEADS), b2),
        ],
        out_specs=(
            pl.BlockSpec((bt, ML_W), b2),
            pl.BlockSpec((bt, ML_HEADS, ML_DIM, ML_DIM), b4),
            pl.BlockSpec((bt, ML_HEADS, ML_DIM), b3),
            pl.BlockSpec((bt, ML_HEADS), b2),
        ),
        out_shape=(
            jax.ShapeDtypeStruct((nbatch, ML_W), BF16),
            jax.ShapeDtypeStruct((nbatch, ML_HEADS, ML_DIM, ML_DIM), F32),
            jax.ShapeDtypeStruct((nbatch, ML_HEADS, ML_DIM), F32),
            jax.ShapeDtypeStruct((nbatch, ML_HEADS), F32),
        ),
        compiler_params=pltpu.CompilerParams(dimension_semantics=("parallel",)),
        name="mlstm_decode",
    )(mq, mk, mv, og, gif, state_c, state_n, state_m)


def _rope_tables(pos):
    half = HEAD_DIM // 2
    inv = 1.0 / (ROPE_THETA ** (jnp.arange(half, dtype=F32) / half))
    ang = pos.astype(F32)[:, None] * inv[None, :]
    cos = jnp.cos(ang)
    sin = jnp.sin(ang)
    cos_t = jnp.concatenate([cos, cos, cos, cos], axis=1)
    sin_t = jnp.concatenate([-sin, sin, -sin, sin], axis=1)
    return cos_t, sin_t


def kernel(x_prompt, x_sample, cache_k, cache_v, state_C, state_n, state_m, p_prompt, p_sample, norm_mix, w_in, b_in, attn_sinks, w_attn_up, w_ml_up, w_o, norm_mlp, w_ff1, w_ff2, norm_ple, w_ple_gate, w_ple_proj, norm_final):
    assert w_in.shape[0] == 1, "single layer"
    batch, seq, _ = x_prompt.shape
    nbatch = x_sample.shape[0]
    assert x_sample.shape[1] == 1

    w_in0 = w_in[0]
    b_in0 = b_in[0]
    w_proj = jnp.concatenate([w_in0[:, :OFF_GA], jnp.zeros((D_MODEL, N_PROJ - OFF_GA), F32)], axis=1).astype(BF16)
    b_proj = jnp.concatenate([b_in0[:OFF_GA], jnp.zeros((N_PROJ - OFF_GA,), F32)])[None, :]
    w_gate = w_in0[:, OFF_GA:].astype(BF16)
    b_gate = b_in0[OFF_GA:][None, :]
    g_mix = norm_mix[0][None, :]
    out_weights = (
        g_mix, w_gate, b_gate, w_attn_up[0].astype(BF16), w_ml_up[0].astype(BF16), w_o[0].astype(BF16),
        norm_mlp[0][None, :], w_ff1[0].astype(BF16), w_ff2[0].astype(BF16),
        norm_ple[0][None, :], w_ple_gate[0].astype(BF16), w_ple_proj[0].astype(BF16), norm_final[None, :],
    )
    sinks = attn_sinks[0]

    cos_p, sin_p = _rope_tables(jnp.arange(seq, dtype=jnp.int32))
    xp = x_prompt.reshape(batch * seq, D_MODEL)
    q, k, v, mq, mk, mv, og, gif = _in_proj(xp, g_mix, w_proj, b_proj, cos_p, sin_p, tm=512, m_dtype=BF16)
    a_out = _attn_prompt(sinks, q, k, v, batch, seq)
    h_m, cst, mst = _mlstm_prompt(mq, mk, mv, og, gif, batch, seq)
    y_prompt = _out_proj(xp, a_out, h_m, p_prompt[0].reshape(batch * seq, D_PLE), out_weights, tm=256)
    y_prompt = y_prompt.reshape(batch, seq, D_MODEL)
    k_prompt = k.reshape(batch, seq, N_KV_HEADS, HEAD_DIM)[:, seq - WINDOW:][None]
    v_prompt = v.reshape(batch, seq, N_KV_HEADS, HEAD_DIM)[:, seq - WINDOW:][None]
    c_prompt = cst[..., :ML_DIM][None]
    n_prompt = cst[..., ML_DIM][None]
    m_prompt = mst[:, :ML_HEADS, 0][None]

    cos_s, sin_s = _rope_tables(jnp.full((nbatch,), PAST_LEN, dtype=jnp.int32))
    xs = x_sample.reshape(nbatch, D_MODEL)
    qs, ks, vs, mqs, mks, mvs, ogs, gifs = _in_proj(xs, g_mix, w_proj, b_proj, cos_s, sin_s, tm=nbatch, m_dtype=F32)
    q4 = qs.reshape(nbatch, N_KV_HEADS, GQA_GROUP, 1, HEAD_DIM)
    sel = jnp.eye(N_KV_HEADS, dtype=BF16)[None, :, None, :, None]
    qmat = (q4 * sel).reshape(nbatch, N_Q_HEADS, ATTN_KV)
    o_full, k_s, v_s = _attn_decode(sinks[:, None], qmat, ks, vs,
                                    cache_k[0].reshape(nbatch, WINDOW, ATTN_KV),
                                    cache_v[0].reshape(nbatch, WINDOW, ATTN_KV), bt=8)
    o5 = o_full.reshape(nbatch, N_KV_HEADS, GQA_GROUP, N_KV_HEADS, HEAD_DIM)
    a_s = jnp.stack([o5[:, kv, :, kv, :] for kv in range(N_KV_HEADS)], axis=1).reshape(nbatch, ATTN_Q).astype(BF16)
    h_s, c_s, n_s, m_s = _mlstm_decode(mqs, mks, mvs, ogs, gifs, state_C[0], state_n[0], state_m[0], bt=8)
    y_sample = _out_proj(xs, a_s, h_s, p_sample[0].reshape(nbatch, D_PLE), out_weights, tm=nbatch)
    y_sample = y_sample.reshape(nbatch, 1, D_MODEL)
    k_sample = k_s.reshape(nbatch, WINDOW, N_KV_HEADS, HEAD_DIM)[None]
    v_sample = v_s.reshape(nbatch, WINDOW, N_KV_HEADS, HEAD_DIM)[None]

    return (y_prompt, y_sample, k_prompt, v_prompt, c_prompt, n_prompt, m_prompt,
            k_sample, v_sample, c_s[None], n_s[None], m_s[None])
```

```python
import functools

import numpy as np
import jax
import jax.numpy as jnp
from jax import lax
from jax.experimental import pallas as pl
from jax.experimental.pallas import tpu as pltpu

F32 = jnp.float32
BF16 = jnp.bfloat16

D_MODEL = 1024
HEAD_DIM = 64
N_Q_HEADS = 8
N_KV_HEADS = 2
GQA_GROUP = N_Q_HEADS // N_KV_HEADS
WINDOW = 128
ROPE_THETA = 10000.0
ML_HEADS = 4
ML_DIM = 128
ML_CHUNK = 128
D_FF = 4 * D_MODEL
D_PLE = 256
EPS = 1e-6
PAST_LEN = 16384

ATTN_Q = N_Q_HEADS * HEAD_DIM
ATTN_KV = N_KV_HEADS * HEAD_DIM
ML_W = ML_HEADS * ML_DIM
LANES = 128

OFF_AQ = 0
OFF_AK = OFF_AQ + ATTN_Q
OFF_AV = OFF_AK + ATTN_KV
OFF_MQ = OFF_AV + ATTN_KV
OFF_MK = OFF_MQ + ML_W
OFF_MV = OFF_MK + ML_W
OFF_MO = OFF_MV + ML_W
OFF_MI = OFF_MO + ML_W
OFF_MF = OFF_MI + ML_HEADS
OFF_GA = OFF_MF + ML_HEADS
OFF_GM = OFF_GA + D_MODEL
D_IN = OFF_GM + D_MODEL
N_PROJ = OFF_MI + LANES

VMEM_LIMIT = 60 * 1024 * 1024


def _rms(x, g):
    r = lax.rsqrt(jnp.mean(x * x, axis=-1, keepdims=True) + EPS)
    return (x * r) * g


def _mm(a, b):
    return jnp.dot(a, b, preferred_element_type=F32)


def _mm_nt(a, b):
    return lax.dot_general(a, b, (((1,), (1,)), ((), ())), preferred_element_type=F32)


def _mm_tn(a, b):
    return lax.dot_general(a, b, (((0,), (0,)), ((), ())), preferred_element_type=F32)


def _log_sigmoid(x):
    return jnp.minimum(x, 0.0) - jnp.log1p(jnp.exp(-jnp.abs(x)))


def _in_proj_kernel(x_ref, g_ref, w_ref, b_ref, cos_ref, sin_ref,
                    q_ref, kt_ref, v2_ref, kl_ref, vl_ref, mq_ref, mk_ref, mv_ref, og_ref, gif_ref):
    xn = _rms(x_ref[...], g_ref[...]).astype(BF16)

    def proj(lo, hi):
        return _mm(xn, w_ref[:, lo:hi]) + b_ref[:, lo:hi]

    cos = cos_ref[...]
    sin = sin_ref[...]
    lane = lax.broadcasted_iota(jnp.int32, (1, LANES), 1)
    first_half = (lane % HEAD_DIM) < (HEAD_DIM // 2)

    def rope(z):
        partner = jnp.where(first_half, pltpu.roll(z, LANES - HEAD_DIM // 2, 1), pltpu.roll(z, HEAD_DIM // 2, 1))
        return z * cos + partner * sin

    zq = proj(OFF_AQ, OFF_AK)
    for c in range(ATTN_Q // LANES):
        sl = slice(c * LANES, (c + 1) * LANES)
        q_ref[:, sl] = (rope(zq[:, sl]) * (HEAD_DIM ** -0.5)).astype(q_ref.dtype)
    zkv = proj(OFF_AK, OFF_MQ)
    k = rope(zkv[:, :ATTN_KV])
    v = zkv[:, ATTN_KV:]
    kt_ref[...] = k.T.astype(kt_ref.dtype)
    v2_ref[:, :ATTN_KV] = v.astype(v2_ref.dtype)
    v2_ref[:, ATTN_KV:] = pltpu.roll(v, HEAD_DIM, 1).astype(v2_ref.dtype)
    tm = k.shape[0]
    kl_ref[...] = k[tm - WINDOW:, :]
    vl_ref[...] = v[tm - WINDOW:, :]
    mq_ref[...] = proj(OFF_MQ, OFF_MK).astype(mq_ref.dtype)
    mk_ref[...] = (proj(OFF_MK, OFF_MV) * (ML_DIM ** -0.5)).astype(mk_ref.dtype)
    mv_ref[...] = proj(OFF_MV, OFF_MO).astype(mv_ref.dtype)
    og_ref[...] = jax.nn.sigmoid(proj(OFF_MO, OFF_MI))
    zg = proj(OFF_MI, N_PROJ)
    gif_ref[...] = jnp.where(lane < ML_HEADS, zg, jnp.where(lane < 2 * ML_HEADS, _log_sigmoid(zg), 0.0))


def _in_proj(x, g, w, b, cos, sin, tm, m_dtype):
    rows = x.shape[0]
    seq = cos.shape[0]
    tps = seq // tm
    nseq = rows // seq
    row = lambda i: (i, 0)
    const = lambda i: (0, 0)
    pos = lambda i: (i % tps, 0)
    last = lambda i: (i // tps, 0)
    row_out = lambda n, dt: (jax.ShapeDtypeStruct((rows, n), dt), pl.BlockSpec((tm, n), row))
    outs = (
        row_out(ATTN_Q, BF16),
        (jax.ShapeDtypeStruct((nseq * ATTN_KV, seq), BF16), pl.BlockSpec((ATTN_KV, tm), lambda i: (i // tps, i % tps))),
        row_out(2 * ATTN_KV, BF16),
        (jax.ShapeDtypeStruct((nseq * WINDOW, ATTN_KV), F32), pl.BlockSpec((WINDOW, ATTN_KV), last)),
        (jax.ShapeDtypeStruct((nseq * WINDOW, ATTN_KV), F32), pl.BlockSpec((WINDOW, ATTN_KV), last)),
        row_out(ML_W, m_dtype),
        row_out(ML_W, m_dtype),
        row_out(ML_W, m_dtype),
        row_out(ML_W, F32),
        row_out(LANES, F32),
    )
    return pl.pallas_call(
        _in_proj_kernel,
        grid=(rows // tm,),
        in_specs=[
            pl.BlockSpec((tm, D_MODEL), row),
            pl.BlockSpec((1, D_MODEL), const),
            pl.BlockSpec((D_MODEL, N_PROJ), const),
            pl.BlockSpec((1, N_PROJ), const),
            pl.BlockSpec((tm, LANES), pos),
            pl.BlockSpec((tm, LANES), pos),
        ],
        out_specs=tuple(o[1] for o in outs),
        out_shape=tuple(o[0] for o in outs),
        compiler_params=pltpu.CompilerParams(dimension_semantics=("arbitrary",), vmem_limit_bytes=VMEM_LIMIT),
        name="in_proj",
    )(x, g, w, b, cos, sin)


def _attn_kernel(sink_ref, q_ref, ktp_ref, ktc_ref, vp_ref, vc_ref, o_ref):
    j = pl.program_id(1)
    L = WINDOW
    nsub = q_ref.shape[0] // L
    t = lax.broadcasted_iota(jnp.int32, (L, 4 * L), 0)
    cidx = lax.broadcasted_iota(jnp.int32, (L, 4 * L), 1)
    i = cidx % L
    is_prev = (cidx // L) % 2 == 0
    allowed = (is_prev & (i >= t)) | (~is_prev & (i <= t))
    bias = jnp.where(allowed, 0.0, -jnp.inf)
    bias_first = jnp.where(is_prev & (j == 0), -jnp.inf, bias)
    low = lax.broadcasted_iota(jnp.int32, (1, LANES), 1) < HEAD_DIM
    zeros_t = jnp.zeros((HEAD_DIM, 2 * L), BF16)
    for r in range(nsub):
        rs = slice(r * L, (r + 1) * L)
        if r == 0:
            ktp, vp = ktp_ref[...], vp_ref[...]
        else:
            ktp, vp = ktc_ref[:, (r - 1) * L:r * L], vc_ref[(r - 1) * L:r * L, :]
        kt = jnp.concatenate([ktp, ktc_ref[:, rs]], axis=1)
        v2 = jnp.concatenate([vp, vc_ref[rs, :]], axis=0)
        for kv in range(N_KV_HEADS):
            kth = kt[kv * HEAD_DIM:(kv + 1) * HEAD_DIM, :]
            k_op = jnp.concatenate([jnp.concatenate([kth, zeros_t], axis=1),
                                    jnp.concatenate([zeros_t, kth], axis=1)], axis=0)
            v_plain, v_swapped = v2[:, :ATTN_KV], v2[:, ATTN_KV:]
            v_lo, v_hi = (v_plain, v_swapped) if kv == 0 else (v_swapped, v_plain)
            v_op = jnp.concatenate([jnp.where(low, v_lo, jnp.zeros_like(v_lo)),
                                    jnp.where(low, jnp.zeros_like(v_hi), v_hi)], axis=0)
            for c in range(kv * 2, kv * 2 + 2):
                cs = slice(c * LANES, (c + 1) * LANES)
                s = _mm(q_ref[rs, cs], k_op) + (bias_first if r == 0 else bias)
                ps, dens = [], []
                for half in range(2):
                    sh = s[:, half * 2 * L:(half + 1) * 2 * L]
                    sink = sink_ref[2 * c + half]
                    m = jnp.maximum(jnp.max(sh, axis=1, keepdims=True), sink)
                    p = jnp.exp(sh - m)
                    ps.append(p)
                    dens.append(jnp.sum(p, axis=1, keepdims=True) + jnp.exp(sink - m))
                o = _mm(jnp.concatenate(ps, axis=1).astype(BF16), v_op)
                o_ref[rs, cs] = (o / jnp.where(low, dens[0], dens[1])).astype(o_ref.dtype)


def _attn_prompt(sinks, q, kt, v2, batch, seq, tq):
    nb = seq // WINDOW
    nt = seq // tq
    sub = tq // WINDOW
    return pl.pallas_call(
        _attn_kernel,
        grid=(batch, nt),
        in_specs=[
            pl.BlockSpec(memory_space=pltpu.SMEM),
            pl.BlockSpec((tq, ATTN_Q), lambda b, j: (b * nt + j, 0)),
            pl.BlockSpec((ATTN_KV, WINDOW), lambda b, j: (b, jnp.maximum(j * sub - 1, 0))),
            pl.BlockSpec((ATTN_KV, tq), lambda b, j: (b, j)),
            pl.BlockSpec((WINDOW, 2 * ATTN_KV), lambda b, j: (b * nb + jnp.maximum(j * sub - 1, 0), 0)),
            pl.BlockSpec((tq, 2 * ATTN_KV), lambda b, j: (b * nt + j, 0)),
        ],
        out_specs=pl.BlockSpec((tq, ATTN_Q), lambda b, j: (b * nt + j, 0)),
        out_shape=jax.ShapeDtypeStruct((batch * seq, ATTN_Q), BF16),
        compiler_params=pltpu.CompilerParams(dimension_semantics=("parallel", "parallel")),
        name="attn_prompt",
    )(sinks, q, kt, kt, v2, v2)


def _mlstm_kernel(q_ref, k_ref, v_ref, og_ref, gif_ref, h_ref, cst_ref, mst_ref, c_scr, m_scr):
    c = pl.program_id(1)
    L = ML_CHUNK

    @pl.when(c == 0)
    def _():
        c_scr[...] = jnp.zeros_like(c_scr)
        m_scr[...] = jnp.zeros_like(m_scr)

    G = gif_ref[...]
    row = lax.broadcasted_iota(jnp.int32, (L, L), 0)
    col = lax.broadcasted_iota(jnp.int32, (L, L), 1)
    causal = col <= row
    cum = jnp.dot(causal.astype(F32), G, precision=lax.Precision.HIGHEST, preferred_element_type=F32)
    A = jnp.where(col < ML_HEADS, G, cum)
    AT = A.T
    ones_col = (col == 0).astype(BF16)
    for h in range(ML_HEADS):
        hs = slice(h * ML_DIM, (h + 1) * ML_DIM)
        ig_c = A[:, h:h + 1]
        b_c = A[:, ML_HEADS + h:ML_HEADS + h + 1]
        ig_r = AT[h:h + 1, :]
        b_r = AT[ML_HEADS + h:ML_HEADS + h + 1, :]
        m_prev = m_scr[h:h + 1, 0:1]
        dlog = jnp.where(causal, b_c - b_r + ig_r, -jnp.inf)
        a = b_c + m_prev
        m_t = jnp.maximum(a, jnp.max(dlog, axis=1, keepdims=True))
        aw = jnp.exp(a - m_t)
        qh = q_ref[:, hs]
        kh = k_ref[:, hs]
        vh = v_ref[:, hs]
        sw = _mm_nt(qh, kh) * jnp.exp(dlog - m_t)
        state = c_scr[h]
        qc = _mm(qh, state.astype(BF16))
        num = aw * qc[:, :ML_DIM] + _mm(sw.astype(BF16), vh)
        den = aw * qc[:, ML_DIM:ML_DIM + 1] + jnp.sum(sw, axis=1, keepdims=True)
        hh = num / jnp.maximum(jnp.abs(den), jnp.exp(-m_t))
        h_ref[:, hs] = (hh * og_ref[:, hs]).astype(h_ref.dtype)
        b_end = b_c[L - 1:L, :]
        g = b_end - b_c + ig_c
        m_new = jnp.maximum(b_end + m_prev, jnp.max(g, axis=0, keepdims=True))
        decay = jnp.exp(b_end + m_prev - m_new)
        w = jnp.exp(g - m_new)
        wk = (w * kh.astype(F32)).astype(BF16)
        v_aug = jnp.concatenate([vh, ones_col], axis=1)
        c_scr[h] = decay * state + _mm_tn(wk, v_aug)
        m_scr[h:h + 1, :] = jnp.broadcast_to(m_new, (1, LANES))

    @pl.when(c == pl.num_programs(1) - 1)
    def _():
        cst_ref[0] = c_scr[...]
        mst_ref[0] = m_scr[...]


def _mlstm_prompt(mq, mk, mv, og, gif, batch, seq):
    nc = seq // ML_CHUNK
    blk = lambda b, c: (b * nc + c, 0)
    return pl.pallas_call(
        _mlstm_kernel,
        grid=(batch, nc),
        in_specs=[
            pl.BlockSpec((ML_CHUNK, ML_W), blk),
            pl.BlockSpec((ML_CHUNK, ML_W), blk),
            pl.BlockSpec((ML_CHUNK, ML_W), blk),
            pl.BlockSpec((ML_CHUNK, ML_W), blk),
            pl.BlockSpec((ML_CHUNK, LANES), blk),
        ],
        out_specs=(
            pl.BlockSpec((ML_CHUNK, ML_W), blk),
            pl.BlockSpec((1, ML_HEADS, ML_DIM, 2 * ML_DIM), lambda b, c: (b, 0, 0, 0)),
            pl.BlockSpec((1, 8, LANES), lambda b, c: (b, 0, 0)),
        ),
        out_shape=(
            jax.ShapeDtypeStruct((batch * seq, ML_W), BF16),
            jax.ShapeDtypeStruct((batch, ML_HEADS, ML_DIM, 2 * ML_DIM), F32),
            jax.ShapeDtypeStruct((batch, 8, LANES), F32),
        ),
        scratch_shapes=[pltpu.VMEM((ML_HEADS, ML_DIM, 2 * ML_DIM), F32), pltpu.VMEM((8, LANES), F32)],
        compiler_params=pltpu.CompilerParams(dimension_semantics=("parallel", "arbitrary")),
        name="mlstm_prompt",
    )(mq, mk, mv, og, gif)


def _out_kernel(x_ref, a_ref, hm_ref, p_ref, gmix_ref, wg_ref, bg_ref, wau_ref, wmu_ref, wo_ref,
                gmlp_ref, wff1_ref, wff2_ref, gple_ref, wpg_ref, wpp_ref, gfin_ref, y_ref):
    x = x_ref[...]
    xn = _rms(x, gmix_ref[...]).astype(BF16)
    gates = jax.nn.sigmoid(_mm(xn, wg_ref[...]) + bg_ref[...])
    mix = gates[:, :D_MODEL] * _mm(a_ref[...], wau_ref[...]) + gates[:, D_MODEL:] * _mm(hm_ref[...], wmu_ref[...])
    h = x + _mm(mix.astype(BF16), wo_ref[...])
    u = jnp.square(jnp.maximum(_mm(_rms(h, gmlp_ref[...]).astype(BF16), wff1_ref[...]), 0.0))
    h = h + _mm(u.astype(BF16), wff2_ref[...])
    gate = jax.nn.sigmoid(_mm(_rms(h, gple_ref[...]).astype(BF16), wpg_ref[...]))
    h = h + _mm(p_ref[...].astype(BF16), wpp_ref[...]) * gate
    y_ref[...] = _rms(h, gfin_ref[...])


def _out_proj(x, a, hm, p, weights, tm):
    rows = x.shape[0]
    row = lambda i: (i, 0)
    const = lambda i: (0, 0)
    w_specs = [pl.BlockSpec(w.shape, const, pipeline_mode=pl.Buffered(1)) for w in weights]
    return pl.pallas_call(
        _out_kernel,
        grid=(rows // tm,),
        in_specs=[
            pl.BlockSpec((tm, D_MODEL), row),
            pl.BlockSpec((tm, ATTN_Q), row),
            pl.BlockSpec((tm, ML_W), row),
            pl.BlockSpec((tm, D_PLE), row),
        ] + w_specs,
        out_specs=pl.BlockSpec((tm, D_MODEL), row),
        out_shape=jax.ShapeDtypeStruct((rows, D_MODEL), F32),
        compiler_params=pltpu.CompilerParams(dimension_semantics=("parallel",), vmem_limit_bytes=VMEM_LIMIT),
        name="out_proj",
    )(x, a, hm, p, *weights)


def _attn_decode_kernel(sink_ref, qm_ref, kn_ref, vn_ref, ck_ref, cv_ref, o_ref, ko_ref, vo_ref):
    qm = qm_ref[...]
    kn = kn_ref[...]
    vn = vn_ref[...]
    ck = ck_ref[...]
    cv = cv_ref[...]
    sink = sink_ref[...]
    s = jnp.einsum('bhc,bkc->bhk', qm, ck.astype(BF16), preferred_element_type=F32)
    s_new = jnp.sum(qm.astype(F32) * kn[:, None, :], axis=2, keepdims=True)
    m = jnp.maximum(jnp.maximum(jnp.max(s, axis=2, keepdims=True), s_new), sink)
    p = jnp.exp(s - m)
    p_new = jnp.exp(s_new - m)
    den = jnp.sum(p, axis=2, keepdims=True) + p_new + jnp.exp(sink - m)
    o = jnp.einsum('bhk,bkc->bhc', p.astype(BF16), cv.astype(BF16), preferred_element_type=F32)
    o_ref[...] = (o + p_new * vn[:, None, :]) / den
    ko_ref[:, 0:WINDOW - 1, :] = ck[:, 1:WINDOW, :]
    ko_ref[:, WINDOW - 1:WINDOW, :] = kn[:, None, :]
    vo_ref[:, 0:WINDOW - 1, :] = cv[:, 1:WINDOW, :]
    vo_ref[:, WINDOW - 1:WINDOW, :] = vn[:, None, :]


def _attn_decode(sinks_col, qmat, k_new, v_new, cache_k, cache_v, bt):
    nbatch = qmat.shape[0]
    b3 = lambda i: (i, 0, 0)
    b2 = lambda i: (i, 0)
    return pl.pallas_call(
        _attn_decode_kernel,
        grid=(nbatch // bt,),
        in_specs=[
            pl.BlockSpec((N_Q_HEADS, 1), lambda i: (0, 0)),
            pl.BlockSpec((bt, N_Q_HEADS, ATTN_KV), b3),
            pl.BlockSpec((bt, ATTN_KV), b2),
            pl.BlockSpec((bt, ATTN_KV), b2),
            pl.BlockSpec((bt, WINDOW, ATTN_KV), b3),
            pl.BlockSpec((bt, WINDOW, ATTN_KV), b3),
        ],
        out_specs=(
            pl.BlockSpec((bt, N_Q_HEADS, ATTN_KV), b3),
            pl.BlockSpec((bt, WINDOW, ATTN_KV), b3),
            pl.BlockSpec((bt, WINDOW, ATTN_KV), b3),
        ),
        out_shape=(
            jax.ShapeDtypeStruct((nbatch, N_Q_HEADS, ATTN_KV), F32),
            jax.ShapeDtypeStruct((nbatch, WINDOW, ATTN_KV), F32),
            jax.ShapeDtypeStruct((nbatch, WINDOW, ATTN_KV), F32),
        ),
        compiler_params=pltpu.CompilerParams(dimension_semantics=("parallel",)),
        name="attn_decode",
    )(sinks_col, qmat, k_new, v_new, cache_k, cache_v)


def _mlstm_decode_kernel(q_ref, k_ref, v_ref, og_ref, gif_ref, c_ref, n_ref, m_ref, h_ref, co_ref, no_ref, mo_ref):
    bt = q_ref.shape[0]
    gif = gif_ref[...]
    ig = gif[:, 0:ML_HEADS]
    lf = gif[:, ML_HEADS:2 * ML_HEADS]
    m_prev = m_ref[...]
    a = lf + m_prev
    m_t = jnp.maximum(a, ig)
    aw = jnp.exp(a - m_t)
    e_i = jnp.exp(ig - m_t)
    floor = jnp.exp(-m_t)
    m_new = jnp.maximum(a, ig)
    decay = jnp.exp(a - m_new)
    w = jnp.exp(ig - m_new)
    mo_ref[...] = m_new
    pad = jnp.zeros((LANES - bt, ML_DIM), F32)
    for h in range(ML_HEADS):
        hs = slice(h * ML_DIM, (h + 1) * ML_DIM)
        q = q_ref[:, hs]
        k = k_ref[:, hs]
        v = v_ref[:, hs]
        n = n_ref[:, h, :]
        qk = jnp.sum(q * k, axis=1, keepdims=True)
        qn = jnp.sum(q * n, axis=1, keepdims=True)
        sw = qk * e_i[:, h:h + 1]
        den = aw[:, h:h + 1] * qn + sw
        inv = 1.0 / jnp.maximum(jnp.abs(den), floor[:, h:h + 1])
        no_ref[:, h, :] = decay[:, h:h + 1] * n + w[:, h:h + 1] * k
        qT = jnp.concatenate([q, pad], axis=0).T
        kT = jnp.concatenate([w[:, h:h + 1] * k, pad], axis=0).T
        for b in range(bt):
            C = c_ref[b, h]
            qC = jnp.sum(qT[:, b:b + 1] * C, axis=0, keepdims=True)
            num = aw[b:b + 1, h:h + 1] * qC + sw[b:b + 1, :] * v[b:b + 1, :]
            h_ref[b:b + 1, hs] = (num * inv[b:b + 1, :] * og_ref[b:b + 1, hs]).astype(h_ref.dtype)
            co_ref[b, h] = decay[b:b + 1, h:h + 1] * C + kT[:, b:b + 1] * v[b:b + 1, :]


def _mlstm_decode(mq, mk, mv, og, gif, state_c, state_n, state_m, bt):
    nbatch = mq.shape[0]
    b2 = lambda i: (i, 0)
    b3 = lambda i: (i, 0, 0)
    b4 = lambda i: (i, 0, 0, 0)
    return pl.pallas_call(
        _mlstm_decode_kernel,
        grid=(nbatch // bt,),
        in_specs=[
            pl.BlockSpec((bt, ML_W), b2),
            pl.BlockSpec((bt, ML_W), b2),
            pl.BlockSpec((bt, ML_W), b2),
            pl.BlockSpec((bt, ML_W), b2),
            pl.BlockSpec((bt, LANES), b2),
            pl.BlockSpec((bt, ML_HEADS, ML_DIM, ML_DIM), b4),
            pl.BlockSpec((bt, ML_HEADS, ML_DIM), b3),
            pl.BlockSpec((bt, ML_HEADS), b2),
        ],
        out_specs=(
            pl.BlockSpec((bt, ML_W), b2),
            pl.BlockSpec((bt, ML_HEADS, ML_DIM, ML_DIM), b4),
            pl.BlockSpec((bt, ML_HEADS, ML_DIM), b3),
            pl.BlockSpec((bt, ML_HEADS), b2),
        ),
        out_shape=(
            jax.ShapeDtypeStruct((nbatch, ML_W), BF16),
            jax.ShapeDtypeStruct((nbatch, ML_HEADS, ML_DIM, ML_DIM), F32),
            jax.ShapeDtypeStruct((nbatch, ML_HEADS, ML_DIM), F32),
            jax.ShapeDtypeStruct((nbatch, ML_HEADS), F32),
        ),
        compiler_params=pltpu.CompilerParams(dimension_semantics=("parallel",)),
        name="mlstm_decode",
    )(mq, mk, mv, og, gif, state_c, state_n, state_m)


def _rope_tables(pos):
    half = HEAD_DIM // 2
    inv = 1.0 / (ROPE_THETA ** (jnp.arange(half, dtype=F32) / half))
    ang = pos.astype(F32)[:, None] * inv[None, :]
    cos = jnp.cos(ang)
    sin = jnp.sin(ang)
    cos_t = jnp.concatenate([cos, cos, cos, cos], axis=1)
    sin_t = jnp.concatenate([-sin, sin, -sin, sin], axis=1)
    return cos_t, sin_t


def kernel(x_prompt, x_sample, cache_k, cache_v, state_C, state_n, state_m, p_prompt, p_sample, norm_mix, w_in, b_in, attn_sinks, w_attn_up, w_ml_up, w_o, norm_mlp, w_ff1, w_ff2, norm_ple, w_ple_gate, w_ple_proj, norm_final):
    assert w_in.shape[0] == 1, "single layer"
    batch, seq, _ = x_prompt.shape
    nbatch = x_sample.shape[0]
    assert x_sample.shape[1] == 1

    w_in0 = w_in[0]
    b_in0 = b_in[0]
    w_proj = jnp.concatenate([w_in0[:, :OFF_GA], jnp.zeros((D_MODEL, N_PROJ - OFF_GA), F32)], axis=1).astype(BF16)
    b_proj = jnp.concatenate([b_in0[:OFF_GA], jnp.zeros((N_PROJ - OFF_GA,), F32)])[None, :]
    w_gate = w_in0[:, OFF_GA:].astype(BF16)
    b_gate = b_in0[OFF_GA:][None, :]
    g_mix = norm_mix[0][None, :]
    out_weights = (
        g_mix, w_gate, b_gate, w_attn_up[0].astype(BF16), w_ml_up[0].astype(BF16), w_o[0].astype(BF16),
        norm_mlp[0][None, :], w_ff1[0].astype(BF16), w_ff2[0].astype(BF16),
        norm_ple[0][None, :], w_ple_gate[0].astype(BF16), w_ple_proj[0].astype(BF16), norm_final[None, :],
    )
    sinks = attn_sinks[0]

    cos_p, sin_p = _rope_tables(jnp.arange(seq, dtype=jnp.int32))
    xp = x_prompt.reshape(batch * seq, D_MODEL)
    q, kt, v2, k_last, v_last, mq, mk, mv, og, gif = _in_proj(xp, g_mix, w_proj, b_proj, cos_p, sin_p, tm=512, m_dtype=BF16)
    a_out = _attn_prompt(sinks, q, kt, v2, batch, seq, tq=512)
    h_m, cst, mst = _mlstm_prompt(mq, mk, mv, og, gif, batch, seq)
    y_prompt = _out_proj(xp, a_out, h_m, p_prompt[0].reshape(batch * seq, D_PLE), out_weights, tm=256)
    y_prompt = y_prompt.reshape(batch, seq, D_MODEL)
    k_prompt = k_last.reshape(1, batch, WINDOW, N_KV_HEADS, HEAD_DIM)
    v_prompt = v_last.reshape(1, batch, WINDOW, N_KV_HEADS, HEAD_DIM)
    c_prompt = cst[..., :ML_DIM][None]
    n_prompt = cst[..., ML_DIM][None]
    m_prompt = mst[:, :ML_HEADS, 0][None]

    cos_s, sin_s = _rope_tables(jnp.full((nbatch,), PAST_LEN, dtype=jnp.int32))
    xs = x_sample.reshape(nbatch, D_MODEL)
    qs, _, _, ks, vs, mqs, mks, mvs, ogs, gifs = _in_proj(xs, g_mix, w_proj, b_proj, cos_s, sin_s, tm=nbatch, m_dtype=F32)
    q4 = qs.reshape(nbatch, N_KV_HEADS, GQA_GROUP, 1, HEAD_DIM)
    sel = jnp.eye(N_KV_HEADS, dtype=BF16)[None, :, None, :, None]
    qmat = (q4 * sel).reshape(nbatch, N_Q_HEADS, ATTN_KV)
    o_full, k_s, v_s = _attn_decode(sinks[:, None], qmat, ks, vs,
                                    cache_k[0].reshape(nbatch, WINDOW, ATTN_KV),
                                    cache_v[0].reshape(nbatch, WINDOW, ATTN_KV), bt=8)
    o5 = o_full.reshape(nbatch, N_KV_HEADS, GQA_GROUP, N_KV_HEADS, HEAD_DIM)
    a_s = jnp.stack([o5[:, kv, :, kv, :] for kv in range(N_KV_HEADS)], axis=1).reshape(nbatch, ATTN_Q).astype(BF16)
    h_s, c_s, n_s, m_s = _mlstm_decode(mqs, mks, mvs, ogs, gifs, state_C[0], state_n[0], state_m[0], bt=8)
    y_sample = _out_proj(xs, a_s, h_s, p_sample[0].reshape(nbatch, D_PLE), out_weights, tm=nbatch)
    y_sample = y_sample.reshape(nbatch, 1, D_MODEL)
    k_sample = k_s.reshape(nbatch, WINDOW, N_KV_HEADS, HEAD_DIM)[None]
    v_sample = v_s.reshape(nbatch, WINDOW, N_KV_HEADS, HEAD_DIM)[None]

    return (y_prompt, y_sample, k_prompt, v_prompt, c_prompt, n_prompt, m_prompt,
            k_sample, v_sample, c_s[None], n_s[None], m_s[None])
```

```python
import functools

import numpy as np
import jax
import jax.numpy as jnp
from jax import lax
from jax.experimental import pallas as pl
from jax.experimental.pallas import tpu as pltpu

F32 = jnp.float32
BF16 = jnp.bfloat16

D_MODEL = 1024
HEAD_DIM = 64
N_Q_HEADS = 8
N_KV_HEADS = 2
GQA_GROUP = N_Q_HEADS // N_KV_HEADS
WINDOW = 128
ROPE_THETA = 10000.0
ML_HEADS = 4
ML_DIM = 128
ML_CHUNK = 128
D_FF = 4 * D_MODEL
D_PLE = 256
EPS = 1e-6
PAST_LEN = 16384

ATTN_Q = N_Q_HEADS * HEAD_DIM
ATTN_KV = N_KV_HEADS * HEAD_DIM
ML_W = ML_HEADS * ML_DIM
LANES = 128

OFF_AQ = 0
OFF_AK = OFF_AQ + ATTN_Q
OFF_AV = OFF_AK + ATTN_KV
OFF_MQ = OFF_AV + ATTN_KV
OFF_MK = OFF_MQ + ML_W
OFF_MV = OFF_MK + ML_W
OFF_MO = OFF_MV + ML_W
OFF_MI = OFF_MO + ML_W
OFF_MF = OFF_MI + ML_HEADS
OFF_GA = OFF_MF + ML_HEADS
OFF_GM = OFF_GA + D_MODEL
D_IN = OFF_GM + D_MODEL
N_PROJ = OFF_MI + LANES

VMEM_LIMIT = 60 * 1024 * 1024

def _rms(x, g):
    r = lax.rsqrt(jnp.mean(x * x, axis=-1, keepdims=True) + EPS)
    return (x * r) * g


def _mm(a, b):
    return jnp.dot(a, b, preferred_element_type=F32)


def _mm_nt(a, b):
    return lax.dot_general(a, b, (((1,), (1,)), ((), ())), preferred_element_type=F32)


def _mm_tn(a, b):
    return lax.dot_general(a, b, (((0,), (0,)), ((), ())), preferred_element_type=F32)


def _log_sigmoid(x):
    return jnp.minimum(x, 0.0) - jnp.log1p(jnp.exp(-jnp.abs(x)))


def _in_proj_kernel(x_ref, g_ref, w_ref, b_ref, cos_ref, sin_ref,
                    q_ref, k_ref, v_ref, kl_ref, vl_ref, mq_ref, mk_ref, mv_ref, og_ref, gif_ref, *, feature_major):
    xn = _rms(x_ref[...], g_ref[...]).astype(BF16)

    def proj(lo, hi):
        return _mm(xn, w_ref[:, lo:hi]) + b_ref[:, lo:hi]

    cos = cos_ref[...]
    sin = sin_ref[...]
    lane = lax.broadcasted_iota(jnp.int32, (1, LANES), 1)
    first_half = (lane % HEAD_DIM) < (HEAD_DIM // 2)

    def rope(z):
        partner = jnp.where(first_half, pltpu.roll(z, LANES - HEAD_DIM // 2, 1), pltpu.roll(z, HEAD_DIM // 2, 1))
        return z * cos + partner * sin

    zq = proj(OFF_AQ, OFF_AK)
    for c in range(ATTN_Q // LANES):
        sl = slice(c * LANES, (c + 1) * LANES)
        qc = rope(zq[:, sl]) * (HEAD_DIM ** -0.5)
        if feature_major:
            q_ref[sl, :] = qc.T.astype(q_ref.dtype)
        else:
            q_ref[:, sl] = qc.astype(q_ref.dtype)
    zkv = proj(OFF_AK, OFF_MQ)
    k = rope(zkv[:, :ATTN_KV])
    v = zkv[:, ATTN_KV:]
    k_ref[...] = k.astype(k_ref.dtype)
    v_ref[...] = (v.T if feature_major else v).astype(v_ref.dtype)
    tm = k.shape[0]
    kl_ref[...] = k[tm - WINDOW:, :]
    vl_ref[...] = v[tm - WINDOW:, :]
    mq_ref[...] = proj(OFF_MQ, OFF_MK).astype(mq_ref.dtype)
    mk_ref[...] = (proj(OFF_MK, OFF_MV) * (ML_DIM ** -0.5)).astype(mk_ref.dtype)
    mv_ref[...] = proj(OFF_MV, OFF_MO).astype(mv_ref.dtype)
    og_ref[...] = jax.nn.sigmoid(proj(OFF_MO, OFF_MI))
    zg = proj(OFF_MI, N_PROJ)
    gif_ref[...] = jnp.where(lane < ML_HEADS, zg, jnp.where(lane < 2 * ML_HEADS, _log_sigmoid(zg), 0.0))


def _in_proj(x, g, w, b, cos, sin, tm, m_dtype, feature_major):
    rows = x.shape[0]
    seq = cos.shape[0]
    tps = seq // tm
    nseq = rows // seq
    row = lambda i: (i, 0)
    const = lambda i: (0, 0)
    pos = lambda i: (i % tps, 0)
    last = lambda i: (i // tps, 0)
    row_out = lambda n, dt: (jax.ShapeDtypeStruct((rows, n), dt), pl.BlockSpec((tm, n), row))
    col_out = lambda n, dt: (jax.ShapeDtypeStruct((n, rows), dt), pl.BlockSpec((n, tm), lambda i: (0, i)))
    att_out = col_out if feature_major else row_out
    outs = (
        att_out(ATTN_Q, BF16),
        row_out(ATTN_KV, BF16),
        att_out(ATTN_KV, BF16),
        (jax.ShapeDtypeStruct((nseq * WINDOW, ATTN_KV), F32), pl.BlockSpec((WINDOW, ATTN_KV), last)),
        (jax.ShapeDtypeStruct((nseq * WINDOW, ATTN_KV), F32), pl.BlockSpec((WINDOW, ATTN_KV), last)),
        row_out(ML_W, m_dtype),
        row_out(ML_W, m_dtype),
        row_out(ML_W, m_dtype),
        row_out(ML_W, F32),
        row_out(LANES, F32),
    )
    return pl.pallas_call(
        functools.partial(_in_proj_kernel, feature_major=feature_major),
        grid=(rows // tm,),
        in_specs=[
            pl.BlockSpec((tm, D_MODEL), row),
            pl.BlockSpec((1, D_MODEL), const),
            pl.BlockSpec((D_MODEL, N_PROJ), const),
            pl.BlockSpec((1, N_PROJ), const),
            pl.BlockSpec((tm, LANES), pos),
            pl.BlockSpec((tm, LANES), pos),
        ],
        out_specs=tuple(o[1] for o in outs),
        out_shape=tuple(o[0] for o in outs),
        compiler_params=pltpu.CompilerParams(dimension_semantics=("arbitrary",), vmem_limit_bytes=VMEM_LIMIT),
        name="in_proj",
    )(x, g, w, b, cos, sin)


def _attn_kernel(sink_ref, qt_ref, kp_ref, kc_ref, vtp_ref, vtc_ref, o_ref):
    j = pl.program_id(1)
    L = WINDOW
    nsub = o_ref.shape[0] // L
    key = lax.broadcasted_iota(jnp.int32, (2 * L, L), 0)
    t = lax.broadcasted_iota(jnp.int32, (2 * L, L), 1)
    is_prev = key < L
    allowed = (is_prev & (key >= t)) | (~is_prev & (key - L <= t))
    bias = jnp.where(allowed, 0.0, -jnp.inf)
    bias_first = jnp.where(is_prev & (j == 0), -jnp.inf, bias)
    zeros_q = jnp.zeros((HEAD_DIM, L), BF16)
    for r in range(nsub):
        ts = slice(r * L, (r + 1) * L)
        k_prev = kp_ref[...] if r == 0 else kc_ref[(r - 1) * L:r * L, :]
        vt_prev = vtp_ref[...] if r == 0 else vtc_ref[:, (r - 1) * L:r * L]
        k2 = jnp.concatenate([k_prev, kc_ref[ts, :]], axis=0)
        vt2 = jnp.concatenate([vt_prev, vtc_ref[:, ts]], axis=1)
        b_r = bias_first if r == 0 else bias
        heads = range(N_Q_HEADS)
        q_op = [jnp.concatenate([qt_ref[hd * HEAD_DIM:(hd + 1) * HEAD_DIM, ts], zeros_q][::1 if hd < GQA_GROUP else -1],
                                axis=0) for hd in heads]
        s = [_mm(k2, q_op[hd]) + b_r for hd in heads]
        m = [jnp.maximum(jnp.max(s[hd], axis=0, keepdims=True), sink_ref[hd]) for hd in heads]
        p = [jnp.exp(s[hd] - m[hd]) for hd in heads]
        den = [jnp.sum(p[hd], axis=0, keepdims=True) + jnp.exp(sink_ref[hd] - m[hd]) for hd in heads]
        o = [_mm(vt2[(hd // GQA_GROUP) * HEAD_DIM:(hd // GQA_GROUP + 1) * HEAD_DIM, :], p[hd].astype(BF16)) / den[hd]
             for hd in heads]
        o_ref[ts, :] = jnp.concatenate(o, axis=0).T.astype(o_ref.dtype)


def _attn_prompt(sinks, qt, k, vt, batch, seq, tq):
    nb = seq // WINDOW
    nt = seq // tq
    sub = tq // WINDOW
    prev_blk = lambda b, j: b * nb + jnp.maximum(j * sub - 1, 0)
    return pl.pallas_call(
        _attn_kernel,
        grid=(batch, nt),
        in_specs=[
            pl.BlockSpec(memory_space=pltpu.SMEM),
            pl.BlockSpec((ATTN_Q, tq), lambda b, j: (0, b * nt + j)),
            pl.BlockSpec((WINDOW, ATTN_KV), lambda b, j: (prev_blk(b, j), 0)),
            pl.BlockSpec((tq, ATTN_KV), lambda b, j: (b * nt + j, 0)),
            pl.BlockSpec((ATTN_KV, WINDOW), lambda b, j: (0, prev_blk(b, j))),
            pl.BlockSpec((ATTN_KV, tq), lambda b, j: (0, b * nt + j)),
        ],
        out_specs=pl.BlockSpec((tq, ATTN_Q), lambda b, j: (b * nt + j, 0)),
        out_shape=jax.ShapeDtypeStruct((batch * seq, ATTN_Q), BF16),
        compiler_params=pltpu.CompilerParams(dimension_semantics=("parallel", "parallel")),
        name="attn_prompt",
    )(sinks, qt, k, k, vt, vt)


def _mlstm_kernel(q_ref, k_ref, v_ref, og_ref, gif_ref, h_ref, cst_ref, mst_ref, c_scr, m_scr):
    c = pl.program_id(1)
    L = ML_CHUNK
    bb = q_ref.shape[0]

    @pl.when(c == 0)
    def _():
        c_scr[...] = jnp.zeros_like(c_scr)
        m_scr[...] = jnp.zeros_like(m_scr)

    row = lax.broadcasted_iota(jnp.int32, (L, L), 0)
    col = lax.broadcasted_iota(jnp.int32, (L, L), 1)
    causal = col <= row
    tri = causal.astype(F32)
    ones_col = (col == 0).astype(BF16)
    chains = [(b, h) for b in range(bb) for h in range(ML_HEADS)]
    hsl = lambda h: slice(h * ML_DIM, (h + 1) * ML_DIM)
    A, AT = {}, {}
    for b in range(bb):
        G = gif_ref[b]
        cum = jnp.dot(tri, G, precision=lax.Precision.HIGHEST, preferred_element_type=F32)
        A[b] = jnp.where(col < ML_HEADS, G, cum)
    for b in range(bb):
        AT[b] = A[b].T
    ig_c = {(b, h): A[b][:, h:h + 1] for b, h in chains}
    b_c = {(b, h): A[b][:, ML_HEADS + h:ML_HEADS + h + 1] for b, h in chains}
    m_prev = {(b, h): m_scr[b, h:h + 1, 0:1] for b, h in chains}
    dlog = {(b, h): jnp.where(causal, b_c[b, h] - AT[b][ML_HEADS + h:ML_HEADS + h + 1, :] + AT[b][h:h + 1, :], -jnp.inf)
            for b, h in chains}
    qk = {(b, h): _mm_nt(q_ref[b, :, hsl(h)], k_ref[b, :, hsl(h)]) for b, h in chains}
    state = {(b, h): c_scr[b, h] for b, h in chains}
    qc = {(b, h): _mm(q_ref[b, :, hsl(h)], state[b, h].astype(BF16)) for b, h in chains}
    a = {ch: b_c[ch] + m_prev[ch] for ch in chains}
    m_t = {ch: jnp.maximum(a[ch], jnp.max(dlog[ch], axis=1, keepdims=True)) for ch in chains}
    sw = {ch: qk[ch] * jnp.exp(dlog[ch] - m_t[ch]) for ch in chains}
    sv = {(b, h): _mm(sw[b, h].astype(BF16), v_ref[b, :, hsl(h)]) for b, h in chains}
    for b, h in chains:
        ch = (b, h)
        aw = jnp.exp(a[ch] - m_t[ch])
        num = aw * qc[ch][:, :ML_DIM] + sv[ch]
        den = aw * qc[ch][:, ML_DIM:ML_DIM + 1] + jnp.sum(sw[ch], axis=1, keepdims=True)
        hh = num / jnp.maximum(jnp.abs(den), jnp.exp(-m_t[ch]))
        h_ref[b, :, hsl(h)] = (hh * og_ref[b, :, hsl(h)]).astype(h_ref.dtype)
    for b, h in chains:
        ch = (b, h)
        b_end = b_c[ch][L - 1:L, :]
        g = b_end - b_c[ch] + ig_c[ch]
        m_new = jnp.maximum(b_end + m_prev[ch], jnp.max(g, axis=0, keepdims=True))
        decay = jnp.exp(b_end + m_prev[ch] - m_new)
        w = jnp.exp(g - m_new)
        wk = (w * k_ref[b, :, hsl(h)].astype(F32)).astype(BF16)
        v_aug = jnp.concatenate([v_ref[b, :, hsl(h)], ones_col], axis=1)
        c_scr[b, h] = decay * state[ch] + _mm_tn(wk, v_aug)
        m_scr[b, h:h + 1, :] = jnp.broadcast_to(m_new, (1, LANES))

    @pl.when(c == pl.num_programs(1) - 1)
    def _():
        cst_ref[...] = c_scr[...]
        mst_ref[...] = m_scr[...]


def _mlstm_prompt(mq, mk, mv, og, gif, batch, seq, bb):
    nc = seq // ML_CHUNK
    blk = lambda g, c: (g, c, 0)
    r3 = lambda t: t.reshape(batch, seq, t.shape[-1])
    h, cst, mst = pl.pallas_call(
        _mlstm_kernel,
        grid=(batch // bb, nc),
        in_specs=[
            pl.BlockSpec((bb, ML_CHUNK, ML_W), blk),
            pl.BlockSpec((bb, ML_CHUNK, ML_W), blk),
            pl.BlockSpec((bb, ML_CHUNK, ML_W), blk),
            pl.BlockSpec((bb, ML_CHUNK, ML_W), blk),
            pl.BlockSpec((bb, ML_CHUNK, LANES), blk),
        ],
        out_specs=(
            pl.BlockSpec((bb, ML_CHUNK, ML_W), blk),
            pl.BlockSpec((bb, ML_HEADS, ML_DIM, 2 * ML_DIM), lambda g, c: (g, 0, 0, 0)),
            pl.BlockSpec((bb, 8, LANES), lambda g, c: (g, 0, 0)),
        ),
        out_shape=(
            jax.ShapeDtypeStruct((batch, seq, ML_W), BF16),
            jax.ShapeDtypeStruct((batch, ML_HEADS, ML_DIM, 2 * ML_DIM), F32),
            jax.ShapeDtypeStruct((batch, 8, LANES), F32),
        ),
        scratch_shapes=[pltpu.VMEM((bb, ML_HEADS, ML_DIM, 2 * ML_DIM), F32), pltpu.VMEM((bb, 8, LANES), F32)],
        compiler_params=pltpu.CompilerParams(dimension_semantics=("parallel", "arbitrary")),
        name="mlstm_prompt",
    )(r3(mq), r3(mk), r3(mv), r3(og), r3(gif))
    return h.reshape(batch * seq, ML_W), cst, mst


def _out_kernel(x_ref, a_ref, hm_ref, p_ref, gmix_ref, wg_ref, bg_ref, wau_ref, wmu_ref, wo_ref,
                gmlp_ref, wff1_ref, wff2_ref, gple_ref, wpg_ref, wpp_ref, gfin_ref, y_ref):
    x = x_ref[...]
    xn = _rms(x, gmix_ref[...]).astype(BF16)
    gates = jax.nn.sigmoid(_mm(xn, wg_ref[...]) + bg_ref[...])
    mix = gates[:, :D_MODEL] * _mm(a_ref[...], wau_ref[...]) + gates[:, D_MODEL:] * _mm(hm_ref[...], wmu_ref[...])
    h = x + _mm(mix.astype(BF16), wo_ref[...])
    u = jnp.square(jnp.maximum(_mm(_rms(h, gmlp_ref[...]).astype(BF16), wff1_ref[...]), 0.0))
    h = h + _mm(u.astype(BF16), wff2_ref[...])
    gate = jax.nn.sigmoid(_mm(_rms(h, gple_ref[...]).astype(BF16), wpg_ref[...]))
    h = h + _mm(p_ref[...].astype(BF16), wpp_ref[...]) * gate
    y_ref[...] = _rms(h, gfin_ref[...])


def _out_proj(x, a, hm, p, weights, tm):
    rows = x.shape[0]
    row = lambda i: (i, 0)
    const = lambda i: (0, 0)
    w_specs = [pl.BlockSpec(w.shape, const, pipeline_mode=pl.Buffered(1)) for w in weights]
    return pl.pallas_call(
        _out_kernel,
        grid=(rows // tm,),
        in_specs=[
            pl.BlockSpec((tm, D_MODEL), row),
            pl.BlockSpec((tm, ATTN_Q), row),
            pl.BlockSpec((tm, ML_W), row),
            pl.BlockSpec((tm, D_PLE), row),
        ] + w_specs,
        out_specs=pl.BlockSpec((tm, D_MODEL), row),
        out_shape=jax.ShapeDtypeStruct((rows, D_MODEL), F32),
        compiler_params=pltpu.CompilerParams(dimension_semantics=("parallel",), vmem_limit_bytes=VMEM_LIMIT),
        name="out_proj",
    )(x, a, hm, p, *weights)


def _attn_decode_kernel(sink_ref, qm_ref, kn_ref, vn_ref, ck_ref, cv_ref, o_ref, ko_ref, vo_ref):
    qm = qm_ref[...]
    kn = kn_ref[...]
    vn = vn_ref[...]
    ck = ck_ref[...]
    cv = cv_ref[...]
    sink = sink_ref[...]
    s = jnp.einsum('bhc,bkc->bhk', qm, ck.astype(BF16), preferred_element_type=F32)
    s_new = jnp.sum(qm.astype(F32) * kn[:, None, :], axis=2, keepdims=True)
    m = jnp.maximum(jnp.maximum(jnp.max(s, axis=2, keepdims=True), s_new), sink)
    p = jnp.exp(s - m)
    p_new = jnp.exp(s_new - m)
    den = jnp.sum(p, axis=2, keepdims=True) + p_new + jnp.exp(sink - m)
    o = jnp.einsum('bhk,bkc->bhc', p.astype(BF16), cv.astype(BF16), preferred_element_type=F32)
    o_ref[...] = (o + p_new * vn[:, None, :]) / den
    ko_ref[:, 0:WINDOW - 1, :] = ck[:, 1:WINDOW, :]
    ko_ref[:, WINDOW - 1:WINDOW, :] = kn[:, None, :]
    vo_ref[:, 0:WINDOW - 1, :] = cv[:, 1:WINDOW, :]
    vo_ref[:, WINDOW - 1:WINDOW, :] = vn[:, None, :]


def _attn_decode(sinks_col, qmat, k_new, v_new, cache_k, cache_v, bt):
    nbatch = qmat.shape[0]
    b3 = lambda i: (i, 0, 0)
    b2 = lambda i: (i, 0)
    return pl.pallas_call(
        _attn_decode_kernel,
        grid=(nbatch // bt,),
        in_specs=[
            pl.BlockSpec((N_Q_HEADS, 1), lambda i: (0, 0)),
            pl.BlockSpec((bt, N_Q_HEADS, ATTN_KV), b3),
            pl.BlockSpec((bt, ATTN_KV), b2),
            pl.BlockSpec((bt, ATTN_KV), b2),
            pl.BlockSpec((bt, WINDOW, ATTN_KV), b3),
            pl.BlockSpec((bt, WINDOW, ATTN_KV), b3),
        ],
        out_specs=(
            pl.BlockSpec((bt, N_Q_HEADS, ATTN_KV), b3),
            pl.BlockSpec((bt, WINDOW, ATTN_KV), b3),
            pl.BlockSpec((bt, WINDOW, ATTN_KV), b3),
        ),
        out_shape=(
            jax.ShapeDtypeStruct((nbatch, N_Q_HEADS, ATTN_KV), F32),
            jax.ShapeDtypeStruct((nbatch, WINDOW, ATTN_KV), F32),
            jax.ShapeDtypeStruct((nbatch, WINDOW, ATTN_KV), F32),
        ),
        compiler_params=pltpu.CompilerParams(dimension_semantics=("parallel",)),
        name="attn_decode",
    )(sinks_col, qmat, k_new, v_new, cache_k, cache_v)


def _mlstm_decode_kernel(q_ref, k_ref, v_ref, og_ref, gif_ref, c_ref, n_ref, m_ref, h_ref, co_ref, no_ref, mo_ref):
    bt = q_ref.shape[0]
    gif = gif_ref[...]
    ig = gif[:, 0:ML_HEADS]
    lf = gif[:, ML_HEADS:2 * ML_HEADS]
    m_prev = m_ref[...]
    a = lf + m_prev
    m_t = jnp.maximum(a, ig)
    aw = jnp.exp(a - m_t)
    e_i = jnp.exp(ig - m_t)
    floor = jnp.exp(-m_t)
    m_new = jnp.maximum(a, ig)
    decay = jnp.exp(a - m_new)
    w = jnp.exp(ig - m_new)
    mo_ref[...] = m_new
    pad = jnp.zeros((LANES - bt, ML_DIM), F32)
    for h in range(ML_HEADS):
        hs = slice(h * ML_DIM, (h + 1) * ML_DIM)
        q = q_ref[:, hs]
        k = k_ref[:, hs]
        v = v_ref[:, hs]
        n = n_ref[:, h, :]
        qk = jnp.sum(q * k, axis=1, keepdims=True)
        qn = jnp.sum(q * n, axis=1, keepdims=True)
        sw = qk * e_i[:, h:h + 1]
        den = aw[:, h:h + 1] * qn + sw
        inv = 1.0 / jnp.maximum(jnp.abs(den), floor[:, h:h + 1])
        no_ref[:, h, :] = decay[:, h:h + 1] * n + w[:, h:h + 1] * k
        qT = jnp.concatenate([q, pad], axis=0).T
        kT = jnp.concatenate([w[:, h:h + 1] * k, pad], axis=0).T
        for b in range(bt):
            C = c_ref[b, h]
            qC = jnp.sum(qT[:, b:b + 1] * C, axis=0, keepdims=True)
            num = aw[b:b + 1, h:h + 1] * qC + sw[b:b + 1, :] * v[b:b + 1, :]
            h_ref[b:b + 1, hs] = (num * inv[b:b + 1, :] * og_ref[b:b + 1, hs]).astype(h_ref.dtype)
            co_ref[b, h] = decay[b:b + 1, h:h + 1] * C + kT[:, b:b + 1] * v[b:b + 1, :]


def _mlstm_decode(mq, mk, mv, og, gif, state_c, state_n, state_m, bt):
    nbatch = mq.shape[0]
    b2 = lambda i: (i, 0)
    b3 = lambda i: (i, 0, 0)
    b4 = lambda i: (i, 0, 0, 0)
    return pl.pallas_call(
        _mlstm_decode_kernel,
        grid=(nbatch // bt,),
        in_specs=[
            pl.BlockSpec((bt, ML_W), b2),
            pl.BlockSpec((bt, ML_W), b2),
            pl.BlockSpec((bt, ML_W), b2),
            pl.BlockSpec((bt, ML_W), b2),
            pl.BlockSpec((bt, LANES), b2),
            pl.BlockSpec((bt, ML_HEADS, ML_DIM, ML_DIM), b4),
            pl.BlockSpec((bt, ML_HEADS, ML_DIM), b3),
            pl.BlockSpec((bt, ML_HEADS), b2),
        ],
        out_specs=(
            pl.BlockSpec((bt, ML_W), b2),
            pl.BlockSpec((bt, ML_HEADS, ML_DIM, ML_DIM), b4),
            pl.BlockSpec((bt, ML_HEADS, ML_DIM), b3),
            pl.BlockSpec((bt, ML_HEADS), b2),
        ),
        out_shape=(
            jax.ShapeDtypeStruct((nbatch, ML_W), BF16),
            jax.ShapeDtypeStruct((nbatch, ML_HEADS, ML_DIM, ML_DIM), F32),
            jax.ShapeDtypeStruct((nbatch, ML_HEADS, ML_DIM), F32),
            jax.ShapeDtypeStruct((nbatch, ML_HEADS), F32),
        ),
        compiler_params=pltpu.CompilerParams(dimension_semantics=("parallel",)),
        name="mlstm_decode",
    )(mq, mk, mv, og, gif, state_c, state_n, state_m)


def _rope_tables(pos):
    half = HEAD_DIM // 2
    inv = 1.0 / (ROPE_THETA ** (jnp.arange(half, dtype=F32) / half))
    ang = pos.astype(F32)[:, None] * inv[None, :]
    cos = jnp.cos(ang)
    sin = jnp.sin(ang)
    cos_t = jnp.concatenate([cos, cos, cos, cos], axis=1)
    sin_t = jnp.concatenate([-sin, sin, -sin, sin], axis=1)
    return cos_t, sin_t


def kernel(x_prompt, x_sample, cache_k, cache_v, state_C, state_n, state_m, p_prompt, p_sample, norm_mix, w_in, b_in, attn_sinks, w_attn_up, w_ml_up, w_o, norm_mlp, w_ff1, w_ff2, norm_ple, w_ple_gate, w_ple_proj, norm_final):
    assert w_in.shape[0] == 1, "single layer"
    batch, seq, _ = x_prompt.shape
    nbatch = x_sample.shape[0]
    assert x_sample.shape[1] == 1

    w_in0 = w_in[0]
    b_in0 = b_in[0]
    w_proj = jnp.concatenate([w_in0[:, :OFF_GA], jnp.zeros((D_MODEL, N_PROJ - OFF_GA), F32)], axis=1).astype(BF16)
    b_proj = jnp.concatenate([b_in0[:OFF_GA], jnp.zeros((N_PROJ - OFF_GA,), F32)])[None, :]
    w_gate = w_in0[:, OFF_GA:].astype(BF16)
    b_gate = b_in0[OFF_GA:][None, :]
    g_mix = norm_mix[0][None, :]
    out_weights = (
        g_mix, w_gate, b_gate, w_attn_up[0].astype(BF16), w_ml_up[0].astype(BF16), w_o[0].astype(BF16),
        norm_mlp[0][None, :], w_ff1[0].astype(BF16), w_ff2[0].astype(BF16),
        norm_ple[0][None, :], w_ple_gate[0].astype(BF16), w_ple_proj[0].astype(BF16), norm_final[None, :],
    )
    sinks = attn_sinks[0]

    cos_p, sin_p = _rope_tables(jnp.arange(seq, dtype=jnp.int32))
    xp = x_prompt.reshape(batch * seq, D_MODEL)
    qt, k, vt, k_last, v_last, mq, mk, mv, og, gif = _in_proj(xp, g_mix, w_proj, b_proj, cos_p, sin_p, tm=512,
                                                              m_dtype=BF16, feature_major=True)
    a_out = _attn_prompt(sinks, qt, k, vt, batch, seq, tq=512)
    h_m, cst, mst = _mlstm_prompt(mq, mk, mv, og, gif, batch, seq, bb=2)
    y_prompt = _out_proj(xp, a_out, h_m, p_prompt[0].reshape(batch * seq, D_PLE), out_weights, tm=256)
    y_prompt = y_prompt.reshape(batch, seq, D_MODEL)
    k_prompt = k_last.reshape(1, batch, WINDOW, N_KV_HEADS, HEAD_DIM)
    v_prompt = v_last.reshape(1, batch, WINDOW, N_KV_HEADS, HEAD_DIM)
    c_prompt = cst[..., :ML_DIM][None]
    n_prompt = cst[..., ML_DIM][None]
    m_prompt = mst[:, :ML_HEADS, 0][None]

    cos_s, sin_s = _rope_tables(jnp.full((nbatch,), PAST_LEN, dtype=jnp.int32))
    xs = x_sample.reshape(nbatch, D_MODEL)
    qs, _, _, ks, vs, mqs, mks, mvs, ogs, gifs = _in_proj(xs, g_mix, w_proj, b_proj, cos_s, sin_s, tm=nbatch,
                                                          m_dtype=F32, feature_major=False)
    q4 = qs.reshape(nbatch, N_KV_HEADS, GQA_GROUP, 1, HEAD_DIM)
    sel = jnp.eye(N_KV_HEADS, dtype=BF16)[None, :, None, :, None]
    qmat = (q4 * sel).reshape(nbatch, N_Q_HEADS, ATTN_KV)
    o_full, k_s, v_s = _attn_decode(sinks[:, None], qmat, ks, vs,
                                    cache_k[0].reshape(nbatch, WINDOW, ATTN_KV),
                                    cache_v[0].reshape(nbatch, WINDOW, ATTN_KV), bt=8)
    o5 = o_full.reshape(nbatch, N_KV_HEADS, GQA_GROUP, N_KV_HEADS, HEAD_DIM)
    a_s = jnp.stack([o5[:, kv, :, kv, :] for kv in range(N_KV_HEADS)], axis=1).reshape(nbatch, ATTN_Q).astype(BF16)
    h_s, c_s, n_s, m_s = _mlstm_decode(mqs, mks, mvs, ogs, gifs, state_C[0], state_n[0], state_m[0], bt=8)
    y_sample = _out_proj(xs, a_s, h_s, p_sample[0].reshape(nbatch, D_PLE), out_weights, tm=nbatch)
    y_sample = y_sample.reshape(nbatch, 1, D_MODEL)
    k_sample = k_s.reshape(nbatch, WINDOW, N_KV_HEADS, HEAD_DIM)[None]
    v_sample = v_s.reshape(nbatch, WINDOW, N_KV_HEADS, HEAD_DIM)[None]

    return (y_prompt, y_sample, k_prompt, v_prompt, c_prompt, n_prompt, m_prompt,
            k_sample, v_sample, c_s[None], n_s[None], m_s[None])
```

```python
import functools

import numpy as np
import jax
import jax.numpy as jnp
from jax import lax
from jax.experimental import pallas as pl
from jax.experimental.pallas import tpu as pltpu

F32 = jnp.float32
BF16 = jnp.bfloat16

D_MODEL = 1024
HEAD_DIM = 64
N_Q_HEADS = 8
N_KV_HEADS = 2
GQA_GROUP = N_Q_HEADS // N_KV_HEADS
WINDOW = 128
ROPE_THETA = 10000.0
ML_HEADS = 4
ML_DIM = 128
ML_CHUNK = 128
D_FF = 4 * D_MODEL
D_PLE = 256
EPS = 1e-6
PAST_LEN = 16384

ATTN_Q = N_Q_HEADS * HEAD_DIM
ATTN_KV = N_KV_HEADS * HEAD_DIM
ML_W = ML_HEADS * ML_DIM
LANES = 128

OFF_AQ = 0
OFF_AK = OFF_AQ + ATTN_Q
OFF_AV = OFF_AK + ATTN_KV
OFF_MQ = OFF_AV + ATTN_KV
OFF_MK = OFF_MQ + ML_W
OFF_MV = OFF_MK + ML_W
OFF_MO = OFF_MV + ML_W
OFF_MI = OFF_MO + ML_W
OFF_MF = OFF_MI + ML_HEADS
OFF_GA = OFF_MF + ML_HEADS
OFF_GM = OFF_GA + D_MODEL
D_IN = OFF_GM + D_MODEL
N_PROJ = OFF_MI + LANES

VMEM_LIMIT = 60 * 1024 * 1024

def _rms(x, g):
    r = lax.rsqrt(jnp.mean(x * x, axis=-1, keepdims=True) + EPS)
    return (x * r) * g


def _mm(a, b):
    return jnp.dot(a, b, preferred_element_type=F32)


def _mm_nt(a, b):
    return lax.dot_general(a, b, (((1,), (1,)), ((), ())), preferred_element_type=F32)


def _mm_tn(a, b):
    return lax.dot_general(a, b, (((0,), (0,)), ((), ())), preferred_element_type=F32)


def _log_sigmoid(x):
    return jnp.minimum(x, 0.0) - jnp.log1p(jnp.exp(-jnp.abs(x)))


def _in_proj_kernel(x_ref, g_ref, w_ref, b_ref, cos_ref, sin_ref,
                    q_ref, k_ref, v_ref, kl_ref, vl_ref, mq_ref, mk_ref, mv_ref, og_ref, gif_ref, gt_ref,
                    *, feature_major):
    xn = _rms(x_ref[...], g_ref[...]).astype(BF16)

    def proj(lo, hi):
        return _mm(xn, w_ref[:, lo:hi]) + b_ref[:, lo:hi]

    cos = cos_ref[...]
    sin = sin_ref[...]
    lane = lax.broadcasted_iota(jnp.int32, (1, LANES), 1)
    first_half = (lane % HEAD_DIM) < (HEAD_DIM // 2)

    def rope(z):
        partner = jnp.where(first_half, pltpu.roll(z, LANES - HEAD_DIM // 2, 1), pltpu.roll(z, HEAD_DIM // 2, 1))
        return z * cos + partner * sin

    zq = proj(OFF_AQ, OFF_AK)
    for c in range(ATTN_Q // LANES):
        sl = slice(c * LANES, (c + 1) * LANES)
        qc = rope(zq[:, sl]) * (HEAD_DIM ** -0.5)
        if feature_major:
            q_ref[sl, :] = qc.T.astype(q_ref.dtype)
        else:
            q_ref[:, sl] = qc.astype(q_ref.dtype)
    zkv = proj(OFF_AK, OFF_MQ)
    k = rope(zkv[:, :ATTN_KV])
    v = zkv[:, ATTN_KV:]
    k_ref[...] = k.astype(k_ref.dtype)
    v_ref[...] = (v.T if feature_major else v).astype(v_ref.dtype)
    tm = k.shape[0]
    kl_ref[...] = k[tm - WINDOW:, :]
    vl_ref[...] = v[tm - WINDOW:, :]
    zmq = proj(OFF_MQ, OFF_MK)
    zmv = proj(OFF_MV, OFF_MO)
    if feature_major:
        for h in range(ML_HEADS):
            hs = slice(h * ML_DIM, (h + 1) * ML_DIM)
            mq_ref[0, hs, :] = zmq[:, hs].T.astype(mq_ref.dtype)
            mv_ref[0, hs, :] = zmv[:, hs].T.astype(mv_ref.dtype)
    else:
        mq_ref[...] = zmq.astype(mq_ref.dtype)
        mv_ref[...] = zmv.astype(mv_ref.dtype)
    mk_ref[...] = (proj(OFF_MK, OFF_MV) * (ML_DIM ** -0.5)).astype(mk_ref.dtype)
    og_ref[...] = jax.nn.sigmoid(proj(OFF_MO, OFF_MI))
    zg = proj(OFF_MI, N_PROJ)
    gif = jnp.where(lane < ML_HEADS, zg, jnp.where(lane < 2 * ML_HEADS, _log_sigmoid(zg), 0.0))
    if feature_major:
        L = ML_CHUNK
        tri = (lax.broadcasted_iota(jnp.int32, (L, L), 1) <= lax.broadcasted_iota(jnp.int32, (L, L), 0)).astype(F32)
        for c in range(tm // L):
            cs = slice(c * L, (c + 1) * L)
            G = gif[cs, :]
            cum = jnp.dot(tri, G, precision=lax.Precision.HIGHEST, preferred_element_type=F32)
            A = jnp.where(lane < ML_HEADS, G, cum)
            diff = pltpu.roll(pltpu.roll(A, ML_HEADS, 1) - A, ML_HEADS, 1)
            A = jnp.where(lane < 2 * ML_HEADS, A, jnp.where(lane < 3 * ML_HEADS, diff, 0.0))
            gif_ref[cs, :] = A
            gt_ref[0, :, cs] = A.T[0:8, :]
    else:
        gif_ref[...] = gif
        gt_ref[0] = jnp.zeros(gt_ref.shape[1:], F32)


def _in_proj(x, g, w, b, cos, sin, tm, m_dtype, feature_major):
    rows = x.shape[0]
    seq = cos.shape[0]
    tps = seq // tm
    nseq = rows // seq
    row = lambda i: (i, 0)
    const = lambda i: (0, 0)
    pos = lambda i: (i % tps, 0)
    last = lambda i: (i // tps, 0)
    row_out = lambda n, dt: (jax.ShapeDtypeStruct((rows, n), dt), pl.BlockSpec((tm, n), row))
    col_out = lambda n, dt: (jax.ShapeDtypeStruct((n, rows), dt), pl.BlockSpec((n, tm), lambda i: (0, i)))
    att_out = col_out if feature_major else row_out
    seq_out = lambda n, dt: (jax.ShapeDtypeStruct((nseq, n, seq), dt),
                             pl.BlockSpec((1, n, tm), lambda i: (i // tps, 0, i % tps)))
    ml_out = seq_out if feature_major else row_out
    outs = (
        att_out(ATTN_Q, BF16),
        row_out(ATTN_KV, BF16),
        att_out(ATTN_KV, BF16),
        (jax.ShapeDtypeStruct((nseq * WINDOW, ATTN_KV), F32), pl.BlockSpec((WINDOW, ATTN_KV), last)),
        (jax.ShapeDtypeStruct((nseq * WINDOW, ATTN_KV), F32), pl.BlockSpec((WINDOW, ATTN_KV), last)),
        ml_out(ML_W, m_dtype),
        row_out(ML_W, m_dtype),
        ml_out(ML_W, m_dtype),
        row_out(ML_W, F32),
        row_out(LANES, F32),
        seq_out(8, F32),
    )
    return pl.pallas_call(
        functools.partial(_in_proj_kernel, feature_major=feature_major),
        grid=(rows // tm,),
        in_specs=[
            pl.BlockSpec((tm, D_MODEL), row),
            pl.BlockSpec((1, D_MODEL), const),
            pl.BlockSpec((D_MODEL, N_PROJ), const),
            pl.BlockSpec((1, N_PROJ), const),
            pl.BlockSpec((tm, LANES), pos),
            pl.BlockSpec((tm, LANES), pos),
        ],
        out_specs=tuple(o[1] for o in outs),
        out_shape=tuple(o[0] for o in outs),
        compiler_params=pltpu.CompilerParams(dimension_semantics=("arbitrary",), vmem_limit_bytes=VMEM_LIMIT),
        name="in_proj",
    )(x, g, w, b, cos, sin)


def _attn_kernel(sink_ref, qt_ref, kp_ref, kc_ref, vtp_ref, vtc_ref, o_ref):
    j = pl.program_id(1)
    L = WINDOW
    nsub = o_ref.shape[0] // L
    key = lax.broadcasted_iota(jnp.int32, (2 * L, L), 0)
    t = lax.broadcasted_iota(jnp.int32, (2 * L, L), 1)
    is_prev = key < L
    allowed = (is_prev & (key >= t)) | (~is_prev & (key - L <= t))
    bias = jnp.where(allowed, 0.0, -jnp.inf)
    bias_first = jnp.where(is_prev & (j == 0), -jnp.inf, bias)
    zeros_q = jnp.zeros((HEAD_DIM, L), BF16)
    for r in range(nsub):
        ts = slice(r * L, (r + 1) * L)
        k_prev = kp_ref[...] if r == 0 else kc_ref[(r - 1) * L:r * L, :]
        vt_prev = vtp_ref[...] if r == 0 else vtc_ref[:, (r - 1) * L:r * L]
        k2 = jnp.concatenate([k_prev, kc_ref[ts, :]], axis=0)
        vt2 = jnp.concatenate([vt_prev, vtc_ref[:, ts]], axis=1)
        b_r = bias_first if r == 0 else bias
        heads = range(N_Q_HEADS)
        q_op = [jnp.concatenate([qt_ref[hd * HEAD_DIM:(hd + 1) * HEAD_DIM, ts], zeros_q][::1 if hd < GQA_GROUP else -1],
                                axis=0) for hd in heads]
        s = [_mm(k2, q_op[hd]) + b_r for hd in heads]
        m = [jnp.maximum(jnp.max(s[hd], axis=0, keepdims=True), sink_ref[hd]) for hd in heads]
        p = [jnp.exp(s[hd] - m[hd]) for hd in heads]
        den = [jnp.sum(p[hd], axis=0, keepdims=True) + jnp.exp(sink_ref[hd] - m[hd]) for hd in heads]
        o = [_mm(vt2[(hd // GQA_GROUP) * HEAD_DIM:(hd // GQA_GROUP + 1) * HEAD_DIM, :], p[hd].astype(BF16)) / den[hd]
             for hd in heads]
        o_ref[ts, :] = jnp.concatenate(o, axis=0).T.astype(o_ref.dtype)


def _attn_prompt(sinks, qt, k, vt, batch, seq, tq):
    nb = seq // WINDOW
    nt = seq // tq
    sub = tq // WINDOW
    prev_blk = lambda b, j: b * nb + jnp.maximum(j * sub - 1, 0)
    return pl.pallas_call(
        _attn_kernel,
        grid=(batch, nt),
        in_specs=[
            pl.BlockSpec(memory_space=pltpu.SMEM),
            pl.BlockSpec((ATTN_Q, tq), lambda b, j: (0, b * nt + j)),
            pl.BlockSpec((WINDOW, ATTN_KV), lambda b, j: (prev_blk(b, j), 0)),
            pl.BlockSpec((tq, ATTN_KV), lambda b, j: (b * nt + j, 0)),
            pl.BlockSpec((ATTN_KV, WINDOW), lambda b, j: (0, prev_blk(b, j))),
            pl.BlockSpec((ATTN_KV, tq), lambda b, j: (0, b * nt + j)),
        ],
        out_specs=pl.BlockSpec((tq, ATTN_Q), lambda b, j: (b * nt + j, 0)),
        out_shape=jax.ShapeDtypeStruct((batch * seq, ATTN_Q), BF16),
        compiler_params=pltpu.CompilerParams(dimension_semantics=("parallel", "parallel")),
        name="attn_prompt",
    )(sinks, qt, k, k, vt, vt)


ML_STATE_ROWS = ML_DIM + 8


def _mlstm_kernel(qt_ref, k_ref, vt_ref, og_ref, gif_ref, gt_ref, h_ref, cst_ref, nst_ref, mst_ref, c_scr, m_scr):
    c = pl.program_id(1)
    L = ML_CHUNK
    bb = k_ref.shape[0]

    @pl.when(c == 0)
    def _():
        c_scr[...] = jnp.zeros_like(c_scr)
        m_scr[...] = jnp.zeros_like(m_scr)

    s_idx = lax.broadcasted_iota(jnp.int32, (L, L), 0)
    t_idx = lax.broadcasted_iota(jnp.int32, (L, L), 1)
    causal = s_idx <= t_idx
    chains = [(b, h) for b in range(bb) for h in range(ML_HEADS)]
    hsl = lambda h: slice(h * ML_DIM, (h + 1) * ML_DIM)
    qt = {(b, h): qt_ref[b, hsl(h), :] for b, h in chains}
    kk = {(b, h): k_ref[b, :, hsl(h)] for b, h in chains}
    vt = {(b, h): vt_ref[b, hsl(h), :] for b, h in chains}
    ig_r = {(b, h): gt_ref[b, h:h + 1, :] for b, h in chains}
    b_r = {(b, h): gt_ref[b, ML_HEADS + h:ML_HEADS + h + 1, :] for b, h in chains}
    u_c = {(b, h): gif_ref[b, :, 2 * ML_HEADS + h:2 * ML_HEADS + h + 1] for b, h in chains}
    m_prev = {(b, h): m_scr[b, h:h + 1, :] for b, h in chains}
    state = {(b, h): c_scr[b, h] for b, h in chains}
    dlog = {ch: jnp.where(causal, b_r[ch] + u_c[ch], -jnp.inf) for ch in chains}
    qk = {ch: _mm(kk[ch], qt[ch]) for ch in chains}
    inter = {ch: _mm(state[ch].astype(BF16), qt[ch]) for ch in chains}
    a = {ch: b_r[ch] + m_prev[ch] for ch in chains}
    m_t = {ch: jnp.maximum(a[ch], jnp.max(dlog[ch], axis=0, keepdims=True)) for ch in chains}
    sw = {ch: qk[ch] * jnp.exp(dlog[ch] - m_t[ch]) for ch in chains}
    sv = {ch: _mm(vt[ch], sw[ch].astype(BF16)) for ch in chains}
    for b, h in chains:
        ch = (b, h)
        aw = jnp.exp(a[ch] - m_t[ch])
        num = aw * inter[ch][:ML_DIM] + sv[ch]
        den = aw * inter[ch][ML_DIM:ML_DIM + 1] + jnp.sum(sw[ch], axis=0, keepdims=True)
        hh = num / jnp.maximum(jnp.abs(den), jnp.exp(-m_t[ch]))
        h_ref[b, :, hsl(h)] = (hh.T * og_ref[b, :, hsl(h)]).astype(h_ref.dtype)
    pad = jnp.zeros((ML_STATE_ROWS - ML_DIM - 1, L), F32)
    for b, h in chains:
        ch = (b, h)
        b_end = b_r[ch][:, L - 1:L]
        g = b_end - b_r[ch] + ig_r[ch]
        m_new = jnp.maximum(b_end + m_prev[ch], jnp.max(g, axis=1, keepdims=True))
        decay = jnp.exp(b_end + m_prev[ch] - m_new)
        w = jnp.exp(g - m_new)
        v_aug = jnp.concatenate([vt[ch].astype(F32) * w, w, pad], axis=0).astype(BF16)
        c_scr[b, h] = decay * state[ch] + _mm(v_aug, kk[ch])
        m_scr[b, h:h + 1, :] = m_new

    @pl.when(c == pl.num_programs(1) - 1)
    def _():
        nst_ref[...] = jnp.zeros_like(nst_ref)
        for b, h in chains:
            final = c_scr[b, h]
            cst_ref[b, h] = final[:ML_DIM].T
            nst_ref[b, h:h + 1, :] = final[ML_DIM:ML_DIM + 1]
        mst_ref[...] = m_scr[...]


def _mlstm_prompt(mqt, mk, mvt, og, gif, gt, batch, seq, bb):
    nc = seq // ML_CHUNK
    row_blk = lambda g, c: (g, c, 0)
    col_blk = lambda g, c: (g, 0, c)
    per_seq3 = lambda g, c: (g, 0, 0)
    r3 = lambda t: t.reshape(batch, seq, t.shape[-1])
    h, cst, nst, mst = pl.pallas_call(
        _mlstm_kernel,
        grid=(batch // bb, nc),
        in_specs=[
            pl.BlockSpec((bb, ML_W, ML_CHUNK), col_blk),
            pl.BlockSpec((bb, ML_CHUNK, ML_W), row_blk),
            pl.BlockSpec((bb, ML_W, ML_CHUNK), col_blk),
            pl.BlockSpec((bb, ML_CHUNK, ML_W), row_blk),
            pl.BlockSpec((bb, ML_CHUNK, LANES), row_blk),
            pl.BlockSpec((bb, 8, ML_CHUNK), col_blk),
        ],
        out_specs=(
            pl.BlockSpec((bb, ML_CHUNK, ML_W), row_blk),
            pl.BlockSpec((bb, ML_HEADS, ML_DIM, ML_DIM), lambda g, c: (g, 0, 0, 0)),
            pl.BlockSpec((bb, 8, LANES), per_seq3),
            pl.BlockSpec((bb, 8, LANES), per_seq3),
        ),
        out_shape=(
            jax.ShapeDtypeStruct((batch, seq, ML_W), BF16),
            jax.ShapeDtypeStruct((batch, ML_HEADS, ML_DIM, ML_DIM), F32),
            jax.ShapeDtypeStruct((batch, 8, LANES), F32),
            jax.ShapeDtypeStruct((batch, 8, LANES), F32),
        ),
        scratch_shapes=[pltpu.VMEM((bb, ML_HEADS, ML_STATE_ROWS, ML_DIM), F32), pltpu.VMEM((bb, 8, LANES), F32)],
        compiler_params=pltpu.CompilerParams(dimension_semantics=("parallel", "arbitrary")),
        name="mlstm_prompt",
    )(mqt, r3(mk), mvt, r3(og), r3(gif), gt)
    return h.reshape(batch * seq, ML_W), cst, nst, mst


def _out_kernel(x_ref, a_ref, hm_ref, p_ref, gmix_ref, wg_ref, bg_ref, wau_ref, wmu_ref, wo_ref,
                gmlp_ref, wff1_ref, wff2_ref, gple_ref, wpg_ref, wpp_ref, gfin_ref, y_ref):
    x = x_ref[...]
    xn = _rms(x, gmix_ref[...]).astype(BF16)
    gates = jax.nn.sigmoid(_mm(xn, wg_ref[...]) + bg_ref[...])
    mix = gates[:, :D_MODEL] * _mm(a_ref[...], wau_ref[...]) + gates[:, D_MODEL:] * _mm(hm_ref[...], wmu_ref[...])
    h = x + _mm(mix.astype(BF16), wo_ref[...])
    u = jnp.square(jnp.maximum(_mm(_rms(h, gmlp_ref[...]).astype(BF16), wff1_ref[...]), 0.0))
    h = h + _mm(u.astype(BF16), wff2_ref[...])
    gate = jax.nn.sigmoid(_mm(_rms(h, gple_ref[...]).astype(BF16), wpg_ref[...]))
    h = h + _mm(p_ref[...].astype(BF16), wpp_ref[...]) * gate
    y_ref[...] = _rms(h, gfin_ref[...])


def _out_proj(x, a, hm, p, weights, tm):
    rows = x.shape[0]
    row = lambda i: (i, 0)
    const = lambda i: (0, 0)
    w_specs = [pl.BlockSpec(w.shape, const, pipeline_mode=pl.Buffered(1)) for w in weights]
    return pl.pallas_call(
        _out_kernel,
        grid=(rows // tm,),
        in_specs=[
            pl.BlockSpec((tm, D_MODEL), row),
            pl.BlockSpec((tm, ATTN_Q), row),
            pl.BlockSpec((tm, ML_W), row),
            pl.BlockSpec((tm, D_PLE), row),
        ] + w_specs,
        out_specs=pl.BlockSpec((tm, D_MODEL), row),
        out_shape=jax.ShapeDtypeStruct((rows, D_MODEL), F32),
        compiler_params=pltpu.CompilerParams(dimension_semantics=("parallel",), vmem_limit_bytes=VMEM_LIMIT),
        name="out_proj",
    )(x, a, hm, p, *weights)


def _attn_decode_kernel(sink_ref, qm_ref, kn_ref, vn_ref, ck_ref, cv_ref, o_ref, ko_ref, vo_ref):
    qm = qm_ref[...]
    kn = kn_ref[...]
    vn = vn_ref[...]
    ck = ck_ref[...]
    cv = cv_ref[...]
    sink = sink_ref[...]
    s = jnp.einsum('bhc,bkc->bhk', qm, ck.astype(BF16), preferred_element_type=F32)
    s_new = jnp.sum(qm.astype(F32) * kn[:, None, :], axis=2, keepdims=True)
    m = jnp.maximum(jnp.maximum(jnp.max(s, axis=2, keepdims=True), s_new), sink)
    p = jnp.exp(s - m)
    p_new = jnp.exp(s_new - m)
    den = jnp.sum(p, axis=2, keepdims=True) + p_new + jnp.exp(sink - m)
    o = jnp.einsum('bhk,bkc->bhc', p.astype(BF16), cv.astype(BF16), preferred_element_type=F32)
    o_ref[...] = (o + p_new * vn[:, None, :]) / den
    ko_ref[:, 0:WINDOW - 1, :] = ck[:, 1:WINDOW, :]
    ko_ref[:, WINDOW - 1:WINDOW, :] = kn[:, None, :]
    vo_ref[:, 0:WINDOW - 1, :] = cv[:, 1:WINDOW, :]
    vo_ref[:, WINDOW - 1:WINDOW, :] = vn[:, None, :]


def _attn_decode(sinks_col, qmat, k_new, v_new, cache_k, cache_v, bt):
    nbatch = qmat.shape[0]
    b3 = lambda i: (i, 0, 0)
    b2 = lambda i: (i, 0)
    return pl.pallas_call(
        _attn_decode_kernel,
        grid=(nbatch // bt,),
        in_specs=[
            pl.BlockSpec((N_Q_HEADS, 1), lambda i: (0, 0)),
            pl.BlockSpec((bt, N_Q_HEADS, ATTN_KV), b3),
            pl.BlockSpec((bt, ATTN_KV), b2),
            pl.BlockSpec((bt, ATTN_KV), b2),
            pl.BlockSpec((bt, WINDOW, ATTN_KV), b3),
            pl.BlockSpec((bt, WINDOW, ATTN_KV), b3),
        ],
        out_specs=(
            pl.BlockSpec((bt, N_Q_HEADS, ATTN_KV), b3),
            pl.BlockSpec((bt, WINDOW, ATTN_KV), b3),
            pl.BlockSpec((bt, WINDOW, ATTN_KV), b3),
        ),
        out_shape=(
            jax.ShapeDtypeStruct((nbatch, N_Q_HEADS, ATTN_KV), F32),
            jax.ShapeDtypeStruct((nbatch, WINDOW, ATTN_KV), F32),
            jax.ShapeDtypeStruct((nbatch, WINDOW, ATTN_KV), F32),
        ),
        compiler_params=pltpu.CompilerParams(dimension_semantics=("parallel",)),
        name="attn_decode",
    )(sinks_col, qmat, k_new, v_new, cache_k, cache_v)


def _mlstm_decode_kernel(q_ref, k_ref, v_ref, og_ref, gif_ref, c_ref, n_ref, m_ref, h_ref, co_ref, no_ref, mo_ref):
    bt = q_ref.shape[0]
    gif = gif_ref[...]
    ig = gif[:, 0:ML_HEADS]
    lf = gif[:, ML_HEADS:2 * ML_HEADS]
    m_prev = m_ref[...]
    a = lf + m_prev
    m_t = jnp.maximum(a, ig)
    aw = jnp.exp(a - m_t)
    e_i = jnp.exp(ig - m_t)
    floor = jnp.exp(-m_t)
    m_new = jnp.maximum(a, ig)
    decay = jnp.exp(a - m_new)
    w = jnp.exp(ig - m_new)
    mo_ref[...] = m_new
    pad = jnp.zeros((LANES - bt, ML_DIM), F32)
    for h in range(ML_HEADS):
        hs = slice(h * ML_DIM, (h + 1) * ML_DIM)
        q = q_ref[:, hs]
        k = k_ref[:, hs]
        v = v_ref[:, hs]
        n = n_ref[:, h, :]
        qk = jnp.sum(q * k, axis=1, keepdims=True)
        qn = jnp.sum(q * n, axis=1, keepdims=True)
        sw = qk * e_i[:, h:h + 1]
        den = aw[:, h:h + 1] * qn + sw
        inv = 1.0 / jnp.maximum(jnp.abs(den), floor[:, h:h + 1])
        no_ref[:, h, :] = decay[:, h:h + 1] * n + w[:, h:h + 1] * k
        qT = jnp.concatenate([q, pad], axis=0).T
        kT = jnp.concatenate([w[:, h:h + 1] * k, pad], axis=0).T
        for b in range(bt):
            C = c_ref[b, h]
            qC = jnp.sum(qT[:, b:b + 1] * C, axis=0, keepdims=True)
            num = aw[b:b + 1, h:h + 1] * qC + sw[b:b + 1, :] * v[b:b + 1, :]
            h_ref[b:b + 1, hs] = (num * inv[b:b + 1, :] * og_ref[b:b + 1, hs]).astype(h_ref.dtype)
            co_ref[b, h] = decay[b:b + 1, h:h + 1] * C + kT[:, b:b + 1] * v[b:b + 1, :]


def _mlstm_decode(mq, mk, mv, og, gif, state_c, state_n, state_m, bt):
    nbatch = mq.shape[0]
    b2 = lambda i: (i, 0)
    b3 = lambda i: (i, 0, 0)
    b4 = lambda i: (i, 0, 0, 0)
    return pl.pallas_call(
        _mlstm_decode_kernel,
        grid=(nbatch // bt,),
        in_specs=[
            pl.BlockSpec((bt, ML_W), b2),
            pl.BlockSpec((bt, ML_W), b2),
            pl.BlockSpec((bt, ML_W), b2),
            pl.BlockSpec((bt, ML_W), b2),
            pl.BlockSpec((bt, LANES), b2),
            pl.BlockSpec((bt, ML_HEADS, ML_DIM, ML_DIM), b4),
            pl.BlockSpec((bt, ML_HEADS, ML_DIM), b3),
            pl.BlockSpec((bt, ML_HEADS), b2),
        ],
        out_specs=(
            pl.BlockSpec((bt, ML_W), b2),
            pl.BlockSpec((bt, ML_HEADS, ML_DIM, ML_DIM), b4),
            pl.BlockSpec((bt, ML_HEADS, ML_DIM), b3),
            pl.BlockSpec((bt, ML_HEADS), b2),
        ),
        out_shape=(
            jax.ShapeDtypeStruct((nbatch, ML_W), BF16),
            jax.ShapeDtypeStruct((nbatch, ML_HEADS, ML_DIM, ML_DIM), F32),
            jax.ShapeDtypeStruct((nbatch, ML_HEADS, ML_DIM), F32),
            jax.ShapeDtypeStruct((nbatch, ML_HEADS), F32),
        ),
        compiler_params=pltpu.CompilerParams(dimension_semantics=("parallel",)),
        name="mlstm_decode",
    )(mq, mk, mv, og, gif, state_c, state_n, state_m)


def _rope_tables(pos):
    half = HEAD_DIM // 2
    inv = 1.0 / (ROPE_THETA ** (jnp.arange(half, dtype=F32) / half))
    ang = pos.astype(F32)[:, None] * inv[None, :]
    cos = jnp.cos(ang)
    sin = jnp.sin(ang)
    cos_t = jnp.concatenate([cos, cos, cos, cos], axis=1)
    sin_t = jnp.concatenate([-sin, sin, -sin, sin], axis=1)
    return cos_t, sin_t


def kernel(x_prompt, x_sample, cache_k, cache_v, state_C, state_n, state_m, p_prompt, p_sample, norm_mix, w_in, b_in, attn_sinks, w_attn_up, w_ml_up, w_o, norm_mlp, w_ff1, w_ff2, norm_ple, w_ple_gate, w_ple_proj, norm_final):
    assert w_in.shape[0] == 1, "single layer"
    batch, seq, _ = x_prompt.shape
    nbatch = x_sample.shape[0]
    assert x_sample.shape[1] == 1

    w_in0 = w_in[0]
    b_in0 = b_in[0]
    w_proj = jnp.concatenate([w_in0[:, :OFF_GA], jnp.zeros((D_MODEL, N_PROJ - OFF_GA), F32)], axis=1).astype(BF16)
    b_proj = jnp.concatenate([b_in0[:OFF_GA], jnp.zeros((N_PROJ - OFF_GA,), F32)])[None, :]
    w_gate = w_in0[:, OFF_GA:].astype(BF16)
    b_gate = b_in0[OFF_GA:][None, :]
    g_mix = norm_mix[0][None, :]
    out_weights = (
        g_mix, w_gate, b_gate, w_attn_up[0].astype(BF16), w_ml_up[0].astype(BF16), w_o[0].astype(BF16),
        norm_mlp[0][None, :], w_ff1[0].astype(BF16), w_ff2[0].astype(BF16),
        norm_ple[0][None, :], w_ple_gate[0].astype(BF16), w_ple_proj[0].astype(BF16), norm_final[None, :],
    )
    sinks = attn_sinks[0]

    cos_p, sin_p = _rope_tables(jnp.arange(seq, dtype=jnp.int32))
    xp = x_prompt.reshape(batch * seq, D_MODEL)
    qt, k, vt, k_last, v_last, mqt, mk, mvt, og, gif, gt = _in_proj(xp, g_mix, w_proj, b_proj, cos_p, sin_p, tm=512,
                                                                    m_dtype=BF16, feature_major=True)
    a_out = _attn_prompt(sinks, qt, k, vt, batch, seq, tq=512)
    h_m, cst, nst, mst = _mlstm_prompt(mqt, mk, mvt, og, gif, gt, batch, seq, bb=2)
    y_prompt = _out_proj(xp, a_out, h_m, p_prompt[0].reshape(batch * seq, D_PLE), out_weights, tm=256)
    y_prompt = y_prompt.reshape(batch, seq, D_MODEL)
    k_prompt = k_last.reshape(1, batch, WINDOW, N_KV_HEADS, HEAD_DIM)
    v_prompt = v_last.reshape(1, batch, WINDOW, N_KV_HEADS, HEAD_DIM)
    c_prompt = cst[None]
    n_prompt = nst[:, :ML_HEADS, :][None]
    m_prompt = mst[:, :ML_HEADS, 0][None]

    cos_s, sin_s = _rope_tables(jnp.full((nbatch,), PAST_LEN, dtype=jnp.int32))
    xs = x_sample.reshape(nbatch, D_MODEL)
    qs, _, _, ks, vs, mqs, mks, mvs, ogs, gifs, _ = _in_proj(xs, g_mix, w_proj, b_proj, cos_s, sin_s, tm=nbatch,
                                                          m_dtype=F32, feature_major=False)
    q4 = qs.reshape(nbatch, N_KV_HEADS, GQA_GROUP, 1, HEAD_DIM)
    sel = jnp.eye(N_KV_HEADS, dtype=BF16)[None, :, None, :, None]
    qmat = (q4 * sel).reshape(nbatch, N_Q_HEADS, ATTN_KV)
    o_full, k_s, v_s = _attn_decode(sinks[:, None], qmat, ks, vs,
                                    cache_k[0].reshape(nbatch, WINDOW, ATTN_KV),
                                    cache_v[0].reshape(nbatch, WINDOW, ATTN_KV), bt=8)
    o5 = o_full.reshape(nbatch, N_KV_HEADS, GQA_GROUP, N_KV_HEADS, HEAD_DIM)
    a_s = jnp.stack([o5[:, kv, :, kv, :] for kv in range(N_KV_HEADS)], axis=1).reshape(nbatch, ATTN_Q).astype(BF16)
    h_s, c_s, n_s, m_s = _mlstm_decode(mqs, mks, mvs, ogs, gifs, state_C[0], state_n[0], state_m[0], bt=8)
    y_sample = _out_proj(xs, a_s, h_s, p_sample[0].reshape(nbatch, D_PLE), out_weights, tm=nbatch)
    y_sample = y_sample.reshape(nbatch, 1, D_MODEL)
    k_sample = k_s.reshape(nbatch, WINDOW, N_KV_HEADS, HEAD_DIM)[None]
    v_sample = v_s.reshape(nbatch, WINDOW, N_KV_HEADS, HEAD_DIM)[None]

    return (y_prompt, y_sample, k_prompt, v_prompt, c_prompt, n_prompt, m_prompt,
            k_sample, v_sample, c_s[None], n_s[None], m_s[None])
```

```python
import functools

import numpy as np
import jax
import jax.numpy as jnp
from jax import lax
from jax.experimental import pallas as pl
from jax.experimental.pallas import tpu as pltpu

F32 = jnp.float32
BF16 = jnp.bfloat16

D_MODEL = 1024
HEAD_DIM = 64
N_Q_HEADS = 8
N_KV_HEADS = 2
GQA_GROUP = N_Q_HEADS // N_KV_HEADS
WINDOW = 128
ROPE_THETA = 10000.0
ML_HEADS = 4
ML_DIM = 128
ML_CHUNK = 128
D_FF = 4 * D_MODEL
D_PLE = 256
EPS = 1e-6
PAST_LEN = 16384

ATTN_Q = N_Q_HEADS * HEAD_DIM
ATTN_KV = N_KV_HEADS * HEAD_DIM
ML_W = ML_HEADS * ML_DIM
LANES = 128

OFF_AQ = 0
OFF_AK = OFF_AQ + ATTN_Q
OFF_AV = OFF_AK + ATTN_KV
OFF_MQ = OFF_AV + ATTN_KV
OFF_MK = OFF_MQ + ML_W
OFF_MV = OFF_MK + ML_W
OFF_MO = OFF_MV + ML_W
OFF_MI = OFF_MO + ML_W
OFF_MF = OFF_MI + ML_HEADS
OFF_GA = OFF_MF + ML_HEADS
OFF_GM = OFF_GA + D_MODEL
D_IN = OFF_GM + D_MODEL
N_PROJ = OFF_MI + LANES

VMEM_LIMIT = 60 * 1024 * 1024

def _rms(x, g):
    r = lax.rsqrt(jnp.mean(x * x, axis=-1, keepdims=True) + EPS)
    return (x * r) * g


def _mm(a, b):
    return jnp.dot(a, b, preferred_element_type=F32)


def _mm_nt(a, b):
    return lax.dot_general(a, b, (((1,), (1,)), ((), ())), preferred_element_type=F32)


def _mm_tn(a, b):
    return lax.dot_general(a, b, (((0,), (0,)), ((), ())), preferred_element_type=F32)


def _log_sigmoid(x):
    return jnp.minimum(x, 0.0) - jnp.log1p(jnp.exp(-jnp.abs(x)))


WEIGHT_ROW_SPLIT = 8


def _weight_cast_kernel(*refs):
    n_plain = (len(refs) - 3) // 2
    w_in_ref, plain_in = refs[0], refs[1:1 + n_plain]
    w_proj_ref, w_gate_ref, plain_out = refs[1 + n_plain], refs[2 + n_plain], refs[3 + n_plain:]
    w_proj_ref[...] = w_in_ref[:, :N_PROJ].astype(BF16)
    w_gate_ref[...] = w_in_ref[:, OFF_GA:D_IN].astype(BF16)
    for src, dst in zip(plain_in, plain_out):
        dst[...] = src[...].astype(BF16)


def _weight_cast(w_in0, plain):
    split = WEIGHT_ROW_SPLIT
    rows_blk = lambda w: pl.BlockSpec((w.shape[0] // split, w.shape[1]), lambda i: (i, 0))
    out_shapes = (jax.ShapeDtypeStruct((D_MODEL, N_PROJ), BF16), jax.ShapeDtypeStruct((D_MODEL, 2 * D_MODEL), BF16),
                  *(jax.ShapeDtypeStruct(w.shape, BF16) for w in plain))
    return pl.pallas_call(
        _weight_cast_kernel,
        grid=(split,),
        in_specs=[rows_blk(w_in0)] + [rows_blk(w) for w in plain],
        out_specs=tuple(rows_blk(s) for s in out_shapes),
        out_shape=out_shapes,
        compiler_params=pltpu.CompilerParams(dimension_semantics=("parallel",), vmem_limit_bytes=VMEM_LIMIT),
        name="weight_cast",
    )(w_in0, *plain)


def _in_proj_kernel(x_ref, g_ref, w_ref, b_ref, cos_ref, sin_ref,
                    q_ref, k_ref, v_ref, kl_ref, vl_ref, mq_ref, mk_ref, mv_ref, og_ref, gif_ref, gt_ref,
                    *, feature_major):
    xn = _rms(x_ref[...], g_ref[...]).astype(BF16)

    def proj(lo, hi):
        return _mm(xn, w_ref[:, lo:hi]) + b_ref[:, lo:hi]

    cos = cos_ref[...]
    sin = sin_ref[...]
    lane = lax.broadcasted_iota(jnp.int32, (1, LANES), 1)
    first_half = (lane % HEAD_DIM) < (HEAD_DIM // 2)

    def rope(z):
        partner = jnp.where(first_half, pltpu.roll(z, LANES - HEAD_DIM // 2, 1), pltpu.roll(z, HEAD_DIM // 2, 1))
        return z * cos + partner * sin

    zq = proj(OFF_AQ, OFF_AK)
    for c in range(ATTN_Q // LANES):
        sl = slice(c * LANES, (c + 1) * LANES)
        qc = rope(zq[:, sl]) * (HEAD_DIM ** -0.5)
        if feature_major:
            q_ref[sl, :] = qc.T.astype(q_ref.dtype)
        else:
            q_ref[:, sl] = qc.astype(q_ref.dtype)
    zkv = proj(OFF_AK, OFF_MQ)
    k = rope(zkv[:, :ATTN_KV])
    v = zkv[:, ATTN_KV:]
    k_ref[...] = k.astype(k_ref.dtype)
    v_ref[...] = (v.T if feature_major else v).astype(v_ref.dtype)
    tm = k.shape[0]
    kl_ref[...] = k[tm - WINDOW:, :]
    vl_ref[...] = v[tm - WINDOW:, :]
    zmq = proj(OFF_MQ, OFF_MK)
    zmv = proj(OFF_MV, OFF_MO)
    if feature_major:
        for h in range(ML_HEADS):
            hs = slice(h * ML_DIM, (h + 1) * ML_DIM)
            mq_ref[0, hs, :] = zmq[:, hs].T.astype(mq_ref.dtype)
            mv_ref[0, hs, :] = zmv[:, hs].T.astype(mv_ref.dtype)
    else:
        mq_ref[...] = zmq.astype(mq_ref.dtype)
        mv_ref[...] = zmv.astype(mv_ref.dtype)
    mk_ref[...] = (proj(OFF_MK, OFF_MV) * (ML_DIM ** -0.5)).astype(mk_ref.dtype)
    og_ref[...] = jax.nn.sigmoid(proj(OFF_MO, OFF_MI))
    zg = proj(OFF_MI, N_PROJ)
    gif = jnp.where(lane < ML_HEADS, zg, jnp.where(lane < 2 * ML_HEADS, _log_sigmoid(zg), 0.0))
    if feature_major:
        L = ML_CHUNK
        tri = (lax.broadcasted_iota(jnp.int32, (L, L), 1) <= lax.broadcasted_iota(jnp.int32, (L, L), 0)).astype(F32)
        for c in range(tm // L):
            cs = slice(c * L, (c + 1) * L)
            G = gif[cs, :]
            cum = jnp.dot(tri, G, precision=lax.Precision.HIGHEST, preferred_element_type=F32)
            A = jnp.where(lane < ML_HEADS, G, cum)
            diff = pltpu.roll(pltpu.roll(A, ML_HEADS, 1) - A, ML_HEADS, 1)
            A = jnp.where(lane < 2 * ML_HEADS, A, jnp.where(lane < 3 * ML_HEADS, diff, 0.0))
            gif_ref[cs, :] = A
            gt_ref[0, :, cs] = A.T[0:8, :]
    else:
        gif_ref[...] = gif
        gt_ref[0] = jnp.zeros(gt_ref.shape[1:], F32)


def _in_proj(x, g, w, b, cos, sin, tm, m_dtype, feature_major):
    rows = x.shape[0]
    seq = cos.shape[0]
    tps = seq // tm
    nseq = rows // seq
    row = lambda i: (i, 0)
    const = lambda i: (0, 0)
    pos = lambda i: (i % tps, 0)
    last = lambda i: (i // tps, 0)
    row_out = lambda n, dt: (jax.ShapeDtypeStruct((rows, n), dt), pl.BlockSpec((tm, n), row))
    col_out = lambda n, dt: (jax.ShapeDtypeStruct((n, rows), dt), pl.BlockSpec((n, tm), lambda i: (0, i)))
    att_out = col_out if feature_major else row_out
    seq_out = lambda n, dt: (jax.ShapeDtypeStruct((nseq, n, seq), dt),
                             pl.BlockSpec((1, n, tm), lambda i: (i // tps, 0, i % tps)))
    ml_out = seq_out if feature_major else row_out
    outs = (
        att_out(ATTN_Q, BF16),
        row_out(ATTN_KV, BF16),
        att_out(ATTN_KV, BF16),
        (jax.ShapeDtypeStruct((nseq * WINDOW, ATTN_KV), F32), pl.BlockSpec((WINDOW, ATTN_KV), last)),
        (jax.ShapeDtypeStruct((nseq * WINDOW, ATTN_KV), F32), pl.BlockSpec((WINDOW, ATTN_KV), last)),
        ml_out(ML_W, m_dtype),
        row_out(ML_W, m_dtype),
        ml_out(ML_W, m_dtype),
        row_out(ML_W, F32),
        row_out(LANES, F32),
        seq_out(8, F32),
    )
    return pl.pallas_call(
        functools.partial(_in_proj_kernel, feature_major=feature_major),
        grid=(rows // tm,),
        in_specs=[
            pl.BlockSpec((tm, D_MODEL), row),
            pl.BlockSpec((1, D_MODEL), const),
            pl.BlockSpec((D_MODEL, N_PROJ), const),
            pl.BlockSpec((1, N_PROJ), const),
            pl.BlockSpec((tm, LANES), pos),
            pl.BlockSpec((tm, LANES), pos),
        ],
        out_specs=tuple(o[1] for o in outs),
        out_shape=tuple(o[0] for o in outs),
        compiler_params=pltpu.CompilerParams(dimension_semantics=("arbitrary",), vmem_limit_bytes=VMEM_LIMIT),
        name="in_proj",
    )(x, g, w, b, cos, sin)


def _attn_kernel(sink_ref, qt_ref, kp_ref, kc_ref, vtp_ref, vtc_ref, o_ref):
    j = pl.program_id(1)
    L = WINDOW
    nsub = o_ref.shape[0] // L
    key = lax.broadcasted_iota(jnp.int32, (2 * L, L), 0)
    t = lax.broadcasted_iota(jnp.int32, (2 * L, L), 1)
    is_prev = key < L
    allowed = (is_prev & (key >= t)) | (~is_prev & (key - L <= t))
    bias = jnp.where(allowed, 0.0, -jnp.inf)
    bias_first = jnp.where(is_prev & (j == 0), -jnp.inf, bias)
    zeros_q = jnp.zeros((HEAD_DIM, L), BF16)
    for r in range(nsub):
        ts = slice(r * L, (r + 1) * L)
        k_prev = kp_ref[...] if r == 0 else kc_ref[(r - 1) * L:r * L, :]
        vt_prev = vtp_ref[...] if r == 0 else vtc_ref[:, (r - 1) * L:r * L]
        k2 = jnp.concatenate([k_prev, kc_ref[ts, :]], axis=0)
        vt2 = jnp.concatenate([vt_prev, vtc_ref[:, ts]], axis=1)
        b_r = bias_first if r == 0 else bias
        heads = range(N_Q_HEADS)
        q_op = [jnp.concatenate([qt_ref[hd * HEAD_DIM:(hd + 1) * HEAD_DIM, ts], zeros_q][::1 if hd < GQA_GROUP else -1],
                                axis=0) for hd in heads]
        s = [_mm(k2, q_op[hd]) + b_r for hd in heads]
        m = [jnp.maximum(jnp.max(s[hd], axis=0, keepdims=True), sink_ref[hd]) for hd in heads]
        p = [jnp.exp(s[hd] - m[hd]) for hd in heads]
        den = [jnp.sum(p[hd], axis=0, keepdims=True) + jnp.exp(sink_ref[hd] - m[hd]) for hd in heads]
        o = [_mm(vt2[(hd // GQA_GROUP) * HEAD_DIM:(hd // GQA_GROUP + 1) * HEAD_DIM, :], p[hd].astype(BF16)) / den[hd]
             for hd in heads]
        o_ref[ts, :] = jnp.concatenate(o, axis=0).T.astype(o_ref.dtype)


def _attn_prompt(sinks, qt, k, vt, batch, seq, tq):
    nb = seq // WINDOW
    nt = seq // tq
    sub = tq // WINDOW
    prev_blk = lambda b, j: b * nb + jnp.maximum(j * sub - 1, 0)
    return pl.pallas_call(
        _attn_kernel,
        grid=(batch, nt),
        in_specs=[
            pl.BlockSpec(memory_space=pltpu.SMEM),
            pl.BlockSpec((ATTN_Q, tq), lambda b, j: (0, b * nt + j)),
            pl.BlockSpec((WINDOW, ATTN_KV), lambda b, j: (prev_blk(b, j), 0)),
            pl.BlockSpec((tq, ATTN_KV), lambda b, j: (b * nt + j, 0)),
            pl.BlockSpec((ATTN_KV, WINDOW), lambda b, j: (0, prev_blk(b, j))),
            pl.BlockSpec((ATTN_KV, tq), lambda b, j: (0, b * nt + j)),
        ],
        out_specs=pl.BlockSpec((tq, ATTN_Q), lambda b, j: (b * nt + j, 0)),
        out_shape=jax.ShapeDtypeStruct((batch * seq, ATTN_Q), BF16),
        compiler_params=pltpu.CompilerParams(dimension_semantics=("parallel", "parallel")),
        name="attn_prompt",
    )(sinks, qt, k, k, vt, vt)


ML_STATE_ROWS = ML_DIM + 8


def _mlstm_kernel(qt_ref, k_ref, vt_ref, og_ref, gif_ref, gt_ref, h_ref, cst_ref, nst_ref, mst_ref, c_scr, m_scr):
    c = pl.program_id(1)
    L = ML_CHUNK
    bb = k_ref.shape[0]

    @pl.when(c == 0)
    def _():
        c_scr[...] = jnp.zeros_like(c_scr)
        m_scr[...] = jnp.zeros_like(m_scr)

    s_idx = lax.broadcasted_iota(jnp.int32, (L, L), 0)
    t_idx = lax.broadcasted_iota(jnp.int32, (L, L), 1)
    causal = s_idx <= t_idx
    chains = [(b, h) for b in range(bb) for h in range(ML_HEADS)]
    hsl = lambda h: slice(h * ML_DIM, (h + 1) * ML_DIM)
    qt = {(b, h): qt_ref[b, hsl(h), :] for b, h in chains}
    kk = {(b, h): k_ref[b, :, hsl(h)] for b, h in chains}
    vt = {(b, h): vt_ref[b, hsl(h), :] for b, h in chains}
    ig_r = {(b, h): gt_ref[b, h:h + 1, :] for b, h in chains}
    b_r = {(b, h): gt_ref[b, ML_HEADS + h:ML_HEADS + h + 1, :] for b, h in chains}
    u_c = {(b, h): gif_ref[b, :, 2 * ML_HEADS + h:2 * ML_HEADS + h + 1] for b, h in chains}
    m_prev = {(b, h): m_scr[b, h:h + 1, :] for b, h in chains}
    state = {(b, h): c_scr[b, h] for b, h in chains}
    dlog = {ch: jnp.where(causal, b_r[ch] + u_c[ch], -jnp.inf) for ch in chains}
    qk = {ch: _mm(kk[ch], qt[ch]) for ch in chains}
    inter = {ch: _mm(state[ch].astype(BF16), qt[ch]) for ch in chains}
    a = {ch: b_r[ch] + m_prev[ch] for ch in chains}
    m_t = {ch: jnp.maximum(a[ch], jnp.max(dlog[ch], axis=0, keepdims=True)) for ch in chains}
    sw = {ch: qk[ch] * jnp.exp(dlog[ch] - m_t[ch]) for ch in chains}
    sv = {ch: _mm(vt[ch], sw[ch].astype(BF16)) for ch in chains}
    for b, h in chains:
        ch = (b, h)
        aw = jnp.exp(a[ch] - m_t[ch])
        num = aw * inter[ch][:ML_DIM] + sv[ch]
        den = aw * inter[ch][ML_DIM:ML_DIM + 1] + jnp.sum(sw[ch], axis=0, keepdims=True)
        hh = num / jnp.maximum(jnp.abs(den), jnp.exp(-m_t[ch]))
        h_ref[b, :, hsl(h)] = (hh.T * og_ref[b, :, hsl(h)]).astype(h_ref.dtype)
    pad = jnp.zeros((ML_STATE_ROWS - ML_DIM - 1, L), F32)
    for b, h in chains:
        ch = (b, h)
        b_end = b_r[ch][:, L - 1:L]
        g = b_end - b_r[ch] + ig_r[ch]
        m_new = jnp.maximum(b_end + m_prev[ch], jnp.max(g, axis=1, keepdims=True))
        decay = jnp.exp(b_end + m_prev[ch] - m_new)
        w = jnp.exp(g - m_new)
        v_aug = jnp.concatenate([vt[ch].astype(F32) * w, w, pad], axis=0).astype(BF16)
        c_scr[b, h] = decay * state[ch] + _mm(v_aug, kk[ch])
        m_scr[b, h:h + 1, :] = m_new

    @pl.when(c == pl.num_programs(1) - 1)
    def _():
        nst_ref[...] = jnp.zeros_like(nst_ref)
        for b, h in chains:
            final = c_scr[b, h]
            cst_ref[b, h] = final[:ML_DIM].T
            nst_ref[b, h:h + 1, :] = final[ML_DIM:ML_DIM + 1]
        mst_ref[...] = m_scr[...]


def _mlstm_prompt(mqt, mk, mvt, og, gif, gt, batch, seq, bb):
    nc = seq // ML_CHUNK
    row_blk = lambda g, c: (g, c, 0)
    col_blk = lambda g, c: (g, 0, c)
    per_seq3 = lambda g, c: (g, 0, 0)
    r3 = lambda t: t.reshape(batch, seq, t.shape[-1])
    h, cst, nst, mst = pl.pallas_call(
        _mlstm_kernel,
        grid=(batch // bb, nc),
        in_specs=[
            pl.BlockSpec((bb, ML_W, ML_CHUNK), col_blk),
            pl.BlockSpec((bb, ML_CHUNK, ML_W), row_blk),
            pl.BlockSpec((bb, ML_W, ML_CHUNK), col_blk),
            pl.BlockSpec((bb, ML_CHUNK, ML_W), row_blk),
            pl.BlockSpec((bb, ML_CHUNK, LANES), row_blk),
            pl.BlockSpec((bb, 8, ML_CHUNK), col_blk),
        ],
        out_specs=(
            pl.BlockSpec((bb, ML_CHUNK, ML_W), row_blk),
            pl.BlockSpec((bb, ML_HEADS, ML_DIM, ML_DIM), lambda g, c: (g, 0, 0, 0)),
            pl.BlockSpec((bb, 8, LANES), per_seq3),
            pl.BlockSpec((bb, 8, LANES), per_seq3),
        ),
        out_shape=(
            jax.ShapeDtypeStruct((batch, seq, ML_W), BF16),
            jax.ShapeDtypeStruct((batch, ML_HEADS, ML_DIM, ML_DIM), F32),
            jax.ShapeDtypeStruct((batch, 8, LANES), F32),
            jax.ShapeDtypeStruct((batch, 8, LANES), F32),
        ),
        scratch_shapes=[pltpu.VMEM((bb, ML_HEADS, ML_STATE_ROWS, ML_DIM), F32), pltpu.VMEM((bb, 8, LANES), F32)],
        compiler_params=pltpu.CompilerParams(dimension_semantics=("parallel", "arbitrary")),
        name="mlstm_prompt",
    )(mqt, r3(mk), mvt, r3(og), r3(gif), gt)
    return h.reshape(batch * seq, ML_W), cst, nst, mst


def _out_kernel(x_ref, a_ref, hm_ref, p_ref, gmix_ref, wg_ref, bg_ref, wau_ref, wmu_ref, wo_ref,
                gmlp_ref, wff1_ref, wff2_ref, gple_ref, wpg_ref, wpp_ref, gfin_ref, y_ref):
    x = x_ref[...]
    xn = _rms(x, gmix_ref[...]).astype(BF16)
    gates = jax.nn.sigmoid(_mm(xn, wg_ref[...]) + bg_ref[...])
    mix = gates[:, :D_MODEL] * _mm(a_ref[...], wau_ref[...]) + gates[:, D_MODEL:] * _mm(hm_ref[...], wmu_ref[...])
    h = x + _mm(mix.astype(BF16), wo_ref[...])
    u = jnp.square(jnp.maximum(_mm(_rms(h, gmlp_ref[...]).astype(BF16), wff1_ref[...]), 0.0))
    h = h + _mm(u.astype(BF16), wff2_ref[...])
    gate = jax.nn.sigmoid(_mm(_rms(h, gple_ref[...]).astype(BF16), wpg_ref[...]))
    h = h + _mm(p_ref[...].astype(BF16), wpp_ref[...]) * gate
    y_ref[...] = _rms(h, gfin_ref[...])


def _out_proj(x, a, hm, p, weights, tm):
    rows = x.shape[0]
    row = lambda i: (i, 0)
    const = lambda i: (0, 0)
    w_specs = [pl.BlockSpec(w.shape, const, pipeline_mode=pl.Buffered(1)) for w in weights]
    return pl.pallas_call(
        _out_kernel,
        grid=(rows // tm,),
        in_specs=[
            pl.BlockSpec((tm, D_MODEL), row),
            pl.BlockSpec((tm, ATTN_Q), row),
            pl.BlockSpec((tm, ML_W), row),
            pl.BlockSpec((tm, D_PLE), row),
        ] + w_specs,
        out_specs=pl.BlockSpec((tm, D_MODEL), row),
        out_shape=jax.ShapeDtypeStruct((rows, D_MODEL), F32),
        compiler_params=pltpu.CompilerParams(dimension_semantics=("parallel",), vmem_limit_bytes=VMEM_LIMIT),
        name="out_proj",
    )(x, a, hm, p, *weights)


def _attn_decode_kernel(sink_ref, qm_ref, kn_ref, vn_ref, ck_ref, cv_ref, o_ref, ko_ref, vo_ref):
    qm = qm_ref[...]
    kn = kn_ref[...]
    vn = vn_ref[...]
    ck = ck_ref[...]
    cv = cv_ref[...]
    sink = sink_ref[...]
    s = jnp.einsum('bhc,bkc->bhk', qm, ck.astype(BF16), preferred_element_type=F32)
    s_new = jnp.sum(qm.astype(F32) * kn[:, None, :], axis=2, keepdims=True)
    m = jnp.maximum(jnp.maximum(jnp.max(s, axis=2, keepdims=True), s_new), sink)
    p = jnp.exp(s - m)
    p_new = jnp.exp(s_new - m)
    den = jnp.sum(p, axis=2, keepdims=True) + p_new + jnp.exp(sink - m)
    o = jnp.einsum('bhk,bkc->bhc', p.astype(BF16), cv.astype(BF16), preferred_element_type=F32)
    o_ref[...] = (o + p_new * vn[:, None, :]) / den
    ko_ref[:, 0:WINDOW - 1, :] = ck[:, 1:WINDOW, :]
    ko_ref[:, WINDOW - 1:WINDOW, :] = kn[:, None, :]
    vo_ref[:, 0:WINDOW - 1, :] = cv[:, 1:WINDOW, :]
    vo_ref[:, WINDOW - 1:WINDOW, :] = vn[:, None, :]


def _attn_decode(sinks_col, qmat, k_new, v_new, cache_k, cache_v, bt):
    nbatch = qmat.shape[0]
    b3 = lambda i: (i, 0, 0)
    b2 = lambda i: (i, 0)
    return pl.pallas_call(
        _attn_decode_kernel,
        grid=(nbatch // bt,),
        in_specs=[
            pl.BlockSpec((N_Q_HEADS, 1), lambda i: (0, 0)),
            pl.BlockSpec((bt, N_Q_HEADS, ATTN_KV), b3),
            pl.BlockSpec((bt, ATTN_KV), b2),
            pl.BlockSpec((bt, ATTN_KV), b2),
            pl.BlockSpec((bt, WINDOW, ATTN_KV), b3),
            pl.BlockSpec((bt, WINDOW, ATTN_KV), b3),
        ],
        out_specs=(
            pl.BlockSpec((bt, N_Q_HEADS, ATTN_KV), b3),
            pl.BlockSpec((bt, WINDOW, ATTN_KV), b3),
            pl.BlockSpec((bt, WINDOW, ATTN_KV), b3),
        ),
        out_shape=(
            jax.ShapeDtypeStruct((nbatch, N_Q_HEADS, ATTN_KV), F32),
            jax.ShapeDtypeStruct((nbatch, WINDOW, ATTN_KV), F32),
            jax.ShapeDtypeStruct((nbatch, WINDOW, ATTN_KV), F32),
        ),
        compiler_params=pltpu.CompilerParams(dimension_semantics=("parallel",)),
        name="attn_decode",
    )(sinks_col, qmat, k_new, v_new, cache_k, cache_v)


def _mlstm_decode_kernel(q_ref, k_ref, v_ref, og_ref, gif_ref, c_ref, n_ref, m_ref, h_ref, co_ref, no_ref, mo_ref):
    bt = q_ref.shape[0]
    gif = gif_ref[...]
    ig = gif[:, 0:ML_HEADS]
    lf = gif[:, ML_HEADS:2 * ML_HEADS]
    m_prev = m_ref[...]
    a = lf + m_prev
    m_t = jnp.maximum(a, ig)
    aw = jnp.exp(a - m_t)
    e_i = jnp.exp(ig - m_t)
    floor = jnp.exp(-m_t)
    m_new = jnp.maximum(a, ig)
    decay = jnp.exp(a - m_new)
    w = jnp.exp(ig - m_new)
    mo_ref[...] = m_new
    pad = jnp.zeros((LANES - bt, ML_DIM), F32)
    for h in range(ML_HEADS):
        hs = slice(h * ML_DIM, (h + 1) * ML_DIM)
        q = q_ref[:, hs]
        k = k_ref[:, hs]
        v = v_ref[:, hs]
        n = n_ref[:, h, :]
        qk = jnp.sum(q * k, axis=1, keepdims=True)
        qn = jnp.sum(q * n, axis=1, keepdims=True)
        sw = qk * e_i[:, h:h + 1]
        den = aw[:, h:h + 1] * qn + sw
        inv = 1.0 / jnp.maximum(jnp.abs(den), floor[:, h:h + 1])
        no_ref[:, h, :] = decay[:, h:h + 1] * n + w[:, h:h + 1] * k
        qT = jnp.concatenate([q, pad], axis=0).T
        kT = jnp.concatenate([w[:, h:h + 1] * k, pad], axis=0).T
        for b in range(bt):
            C = c_ref[b, h]
            qC = jnp.sum(qT[:, b:b + 1] * C, axis=0, keepdims=True)
            num = aw[b:b + 1, h:h + 1] * qC + sw[b:b + 1, :] * v[b:b + 1, :]
            h_ref[b:b + 1, hs] = (num * inv[b:b + 1, :] * og_ref[b:b + 1, hs]).astype(h_ref.dtype)
            co_ref[b, h] = decay[b:b + 1, h:h + 1] * C + kT[:, b:b + 1] * v[b:b + 1, :]


def _mlstm_decode(mq, mk, mv, og, gif, state_c, state_n, state_m, bt):
    nbatch = mq.shape[0]
    b2 = lambda i: (i, 0)
    b3 = lambda i: (i, 0, 0)
    b4 = lambda i: (i, 0, 0, 0)
    return pl.pallas_call(
        _mlstm_decode_kernel,
        grid=(nbatch // bt,),
        in_specs=[
            pl.BlockSpec((bt, ML_W), b2),
            pl.BlockSpec((bt, ML_W), b2),
            pl.BlockSpec((bt, ML_W), b2),
            pl.BlockSpec((bt, ML_W), b2),
            pl.BlockSpec((bt, LANES), b2),
            pl.BlockSpec((bt, ML_HEADS, ML_DIM, ML_DIM), b4),
            pl.BlockSpec((bt, ML_HEADS, ML_DIM), b3),
            pl.BlockSpec((bt, ML_HEADS), b2),
        ],
        out_specs=(
            pl.BlockSpec((bt, ML_W), b2),
            pl.BlockSpec((bt, ML_HEADS, ML_DIM, ML_DIM), b4),
            pl.BlockSpec((bt, ML_HEADS, ML_DIM), b3),
            pl.BlockSpec((bt, ML_HEADS), b2),
        ),
        out_shape=(
            jax.ShapeDtypeStruct((nbatch, ML_W), BF16),
            jax.ShapeDtypeStruct((nbatch, ML_HEADS, ML_DIM, ML_DIM), F32),
            jax.ShapeDtypeStruct((nbatch, ML_HEADS, ML_DIM), F32),
            jax.ShapeDtypeStruct((nbatch, ML_HEADS), F32),
        ),
        compiler_params=pltpu.CompilerParams(dimension_semantics=("parallel",)),
        name="mlstm_decode",
    )(mq, mk, mv, og, gif, state_c, state_n, state_m)


def _rope_tables(pos):
    half = HEAD_DIM // 2
    inv = 1.0 / (ROPE_THETA ** (jnp.arange(half, dtype=F32) / half))
    ang = pos.astype(F32)[:, None] * inv[None, :]
    cos = jnp.cos(ang)
    sin = jnp.sin(ang)
    cos_t = jnp.concatenate([cos, cos, cos, cos], axis=1)
    sin_t = jnp.concatenate([-sin, sin, -sin, sin], axis=1)
    return cos_t, sin_t


def kernel(x_prompt, x_sample, cache_k, cache_v, state_C, state_n, state_m, p_prompt, p_sample, norm_mix, w_in, b_in, attn_sinks, w_attn_up, w_ml_up, w_o, norm_mlp, w_ff1, w_ff2, norm_ple, w_ple_gate, w_ple_proj, norm_final):
    assert w_in.shape[0] == 1, "single layer"
    batch, seq, _ = x_prompt.shape
    nbatch = x_sample.shape[0]
    assert x_sample.shape[1] == 1

    w_in0 = w_in[0]
    b_in0 = b_in[0]
    w_proj, w_gate, w_au, w_mu, w_out, w_f1, w_f2, w_pg, w_pp = _weight_cast(
        w_in0, (w_attn_up[0], w_ml_up[0], w_o[0], w_ff1[0], w_ff2[0], w_ple_gate[0], w_ple_proj[0]))
    b_proj = b_in0[:N_PROJ][None, :]
    b_gate = b_in0[OFF_GA:][None, :]
    g_mix = norm_mix[0][None, :]
    out_weights = (
        g_mix, w_gate, b_gate, w_au, w_mu, w_out,
        norm_mlp[0][None, :], w_f1, w_f2,
        norm_ple[0][None, :], w_pg, w_pp, norm_final[None, :],
    )
    sinks = attn_sinks[0]

    cos_p, sin_p = _rope_tables(jnp.arange(seq, dtype=jnp.int32))
    xp = x_prompt.reshape(batch * seq, D_MODEL)
    qt, k, vt, k_last, v_last, mqt, mk, mvt, og, gif, gt = _in_proj(xp, g_mix, w_proj, b_proj, cos_p, sin_p, tm=512,
                                                                    m_dtype=BF16, feature_major=True)
    a_out = _attn_prompt(sinks, qt, k, vt, batch, seq, tq=512)
    h_m, cst, nst, mst = _mlstm_prompt(mqt, mk, mvt, og, gif, gt, batch, seq, bb=2)
    y_prompt = _out_proj(xp, a_out, h_m, p_prompt[0].reshape(batch * seq, D_PLE), out_weights, tm=256)
    y_prompt = y_prompt.reshape(batch, seq, D_MODEL)
    k_prompt = k_last.reshape(1, batch, WINDOW, N_KV_HEADS, HEAD_DIM)
    v_prompt = v_last.reshape(1, batch, WINDOW, N_KV_HEADS, HEAD_DIM)
    c_prompt = cst[None]
    n_prompt = nst[:, :ML_HEADS, :][None]
    m_prompt = mst[:, :ML_HEADS, 0][None]

    cos_s, sin_s = _rope_tables(jnp.full((nbatch,), PAST_LEN, dtype=jnp.int32))
    xs = x_sample.reshape(nbatch, D_MODEL)
    qs, _, _, ks, vs, mqs, mks, mvs, ogs, gifs, _ = _in_proj(xs, g_mix, w_proj, b_proj, cos_s, sin_s, tm=nbatch,
                                                          m_dtype=F32, feature_major=False)
    q4 = qs.reshape(nbatch, N_KV_HEADS, GQA_GROUP, 1, HEAD_DIM)
    sel = jnp.eye(N_KV_HEADS, dtype=BF16)[None, :, None, :, None]
    qmat = (q4 * sel).reshape(nbatch, N_Q_HEADS, ATTN_KV)
    o_full, k_s, v_s = _attn_decode(sinks[:, None], qmat, ks, vs,
                                    cache_k[0].reshape(nbatch, WINDOW, ATTN_KV),
                                    cache_v[0].reshape(nbatch, WINDOW, ATTN_KV), bt=8)
    o5 = o_full.reshape(nbatch, N_KV_HEADS, GQA_GROUP, N_KV_HEADS, HEAD_DIM)
    a_s = jnp.stack([o5[:, kv, :, kv, :] for kv in range(N_KV_HEADS)], axis=1).reshape(nbatch, ATTN_Q).astype(BF16)
    h_s, c_s, n_s, m_s = _mlstm_decode(mqs, mks, mvs, ogs, gifs, state_C[0], state_n[0], state_m[0], bt=8)
    y_sample = _out_proj(xs, a_s, h_s, p_sample[0].reshape(nbatch, D_PLE), out_weights, tm=nbatch)
    y_sample = y_sample.reshape(nbatch, 1, D_MODEL)
    k_sample = k_s.reshape(nbatch, WINDOW, N_KV_HEADS, HEAD_DIM)[None]
    v_sample = v_s.reshape(nbatch, WINDOW, N_KV_HEADS, HEAD_DIM)[None]

    return (y_prompt, y_sample, k_prompt, v_prompt, c_prompt, n_prompt, m_prompt,
            k_sample, v_sample, c_s[None], n_s[None], m_s[None])
```

```python
import functools

import numpy as np
import jax
import jax.numpy as jnp
from jax import lax
from jax.experimental import pallas as pl
from jax.experimental.pallas import tpu as pltpu

F32 = jnp.float32
BF16 = jnp.bfloat16

D_MODEL = 1024
HEAD_DIM = 64
N_Q_HEADS = 8
N_KV_HEADS = 2
GQA_GROUP = N_Q_HEADS // N_KV_HEADS
WINDOW = 128
ROPE_THETA = 10000.0
ML_HEADS = 4
ML_DIM = 128
ML_CHUNK = 128
D_FF = 4 * D_MODEL
D_PLE = 256
EPS = 1e-6
PAST_LEN = 16384

ATTN_Q = N_Q_HEADS * HEAD_DIM
ATTN_KV = N_KV_HEADS * HEAD_DIM
ML_W = ML_HEADS * ML_DIM
LANES = 128

OFF_AQ = 0
OFF_AK = OFF_AQ + ATTN_Q
OFF_AV = OFF_AK + ATTN_KV
OFF_MQ = OFF_AV + ATTN_KV
OFF_MK = OFF_MQ + ML_W
OFF_MV = OFF_MK + ML_W
OFF_MO = OFF_MV + ML_W
OFF_MI = OFF_MO + ML_W
OFF_MF = OFF_MI + ML_HEADS
OFF_GA = OFF_MF + ML_HEADS
OFF_GM = OFF_GA + D_MODEL
D_IN = OFF_GM + D_MODEL
N_PROJ = OFF_MI + LANES

VMEM_LIMIT = 60 * 1024 * 1024

def _rms(x, g):
    r = lax.rsqrt(jnp.mean(x * x, axis=-1, keepdims=True) + EPS)
    return (x * r) * g


def _mm(a, b):
    return jnp.dot(a, b, preferred_element_type=F32)


def _mm_nt(a, b):
    return lax.dot_general(a, b, (((1,), (1,)), ((), ())), preferred_element_type=F32)


def _mm_tn(a, b):
    return lax.dot_general(a, b, (((0,), (0,)), ((), ())), preferred_element_type=F32)


def _log_sigmoid(x):
    return jnp.minimum(x, 0.0) - jnp.log1p(jnp.exp(-jnp.abs(x)))


WEIGHT_ROW_SPLIT = 8


def _weight_cast_kernel(*refs):
    n_plain = (len(refs) - 3) // 2
    w_in_ref, plain_in = refs[0], refs[1:1 + n_plain]
    w_proj_ref, w_gate_ref, plain_out = refs[1 + n_plain], refs[2 + n_plain], refs[3 + n_plain:]
    w_proj_ref[...] = w_in_ref[:, :N_PROJ].astype(BF16)
    w_gate_ref[...] = w_in_ref[:, OFF_GA:D_IN].astype(BF16)
    for src, dst in zip(plain_in, plain_out):
        dst[...] = src[...].astype(BF16)


def _weight_cast(w_in0, plain):
    split = WEIGHT_ROW_SPLIT
    rows_blk = lambda w: pl.BlockSpec((w.shape[0] // split, w.shape[1]), lambda i: (i, 0))
    out_shapes = (jax.ShapeDtypeStruct((D_MODEL, N_PROJ), BF16), jax.ShapeDtypeStruct((D_MODEL, 2 * D_MODEL), BF16),
                  *(jax.ShapeDtypeStruct(w.shape, BF16) for w in plain))
    return pl.pallas_call(
        _weight_cast_kernel,
        grid=(split,),
        in_specs=[rows_blk(w_in0)] + [rows_blk(w) for w in plain],
        out_specs=tuple(rows_blk(s) for s in out_shapes),
        out_shape=out_shapes,
        compiler_params=pltpu.CompilerParams(dimension_semantics=("parallel",), vmem_limit_bytes=VMEM_LIMIT),
        name="weight_cast",
    )(w_in0, *plain)


def _in_proj_kernel(x_ref, g_ref, w_ref, b_ref, cos_ref, sin_ref,
                    q_ref, k_ref, v_ref, kl_ref, vl_ref, mq_ref, mk_ref, mv_ref, og_ref, gif_ref, gt_ref,
                    *, feature_major):
    xn = _rms(x_ref[...], g_ref[...]).astype(BF16)

    def proj(lo, hi):
        return _mm(xn, w_ref[:, lo:hi]) + b_ref[:, lo:hi]

    cos = cos_ref[...]
    sin = sin_ref[...]
    lane = lax.broadcasted_iota(jnp.int32, (1, LANES), 1)
    first_half = (lane % HEAD_DIM) < (HEAD_DIM // 2)

    def rope(z):
        partner = jnp.where(first_half, pltpu.roll(z, LANES - HEAD_DIM // 2, 1), pltpu.roll(z, HEAD_DIM // 2, 1))
        return z * cos + partner * sin

    zq = proj(OFF_AQ, OFF_AK)
    for c in range(ATTN_Q // LANES):
        sl = slice(c * LANES, (c + 1) * LANES)
        qc = rope(zq[:, sl]) * (HEAD_DIM ** -0.5)
        if feature_major:
            q_ref[sl, :] = qc.T.astype(q_ref.dtype)
        else:
            q_ref[:, sl] = qc.astype(q_ref.dtype)
    zkv = proj(OFF_AK, OFF_MQ)
    k = rope(zkv[:, :ATTN_KV])
    v = zkv[:, ATTN_KV:]
    k_ref[...] = k.astype(k_ref.dtype)
    v_ref[...] = (v.T if feature_major else v).astype(v_ref.dtype)
    tm = k.shape[0]
    kl_ref[...] = k[tm - WINDOW:, :]
    vl_ref[...] = v[tm - WINDOW:, :]
    zmq = proj(OFF_MQ, OFF_MK)
    zmv = proj(OFF_MV, OFF_MO)
    if feature_major:
        for h in range(ML_HEADS):
            hs = slice(h * ML_DIM, (h + 1) * ML_DIM)
            mq_ref[0, hs, :] = zmq[:, hs].T.astype(mq_ref.dtype)
            mv_ref[0, hs, :] = zmv[:, hs].T.astype(mv_ref.dtype)
    else:
        mq_ref[...] = zmq.astype(mq_ref.dtype)
        mv_ref[...] = zmv.astype(mv_ref.dtype)
    mk_ref[...] = (proj(OFF_MK, OFF_MV) * (ML_DIM ** -0.5)).astype(mk_ref.dtype)
    og_ref[...] = jax.nn.sigmoid(proj(OFF_MO, OFF_MI))
    zg = proj(OFF_MI, N_PROJ)
    gif = jnp.where(lane < ML_HEADS, zg, jnp.where(lane < 2 * ML_HEADS, _log_sigmoid(zg), 0.0))
    if feature_major:
        L = ML_CHUNK
        tri = (lax.broadcasted_iota(jnp.int32, (L, L), 1) <= lax.broadcasted_iota(jnp.int32, (L, L), 0)).astype(F32)
        for c in range(tm // L):
            cs = slice(c * L, (c + 1) * L)
            G = gif[cs, :]
            cum = jnp.dot(tri, G, precision=lax.Precision.HIGHEST, preferred_element_type=F32)
            A = jnp.where(lane < ML_HEADS, G, cum)
            diff = pltpu.roll(pltpu.roll(A, ML_HEADS, 1) - A, ML_HEADS, 1)
            A = jnp.where(lane < 2 * ML_HEADS, A, jnp.where(lane < 3 * ML_HEADS, diff, 0.0))
            gif_ref[cs, :] = A
            gt_ref[0, :, cs] = A.T[0:8, :]
    else:
        gif_ref[...] = gif
        gt_ref[0] = jnp.zeros(gt_ref.shape[1:], F32)


def _in_proj(x, g, w, b, cos, sin, tm, m_dtype, feature_major):
    rows = x.shape[0]
    seq = cos.shape[0]
    tps = seq // tm
    nseq = rows // seq
    row = lambda i: (i, 0)
    const = lambda i: (0, 0)
    pos = lambda i: (i % tps, 0)
    last = lambda i: (i // tps, 0)
    row_out = lambda n, dt: (jax.ShapeDtypeStruct((rows, n), dt), pl.BlockSpec((tm, n), row))
    col_out = lambda n, dt: (jax.ShapeDtypeStruct((n, rows), dt), pl.BlockSpec((n, tm), lambda i: (0, i)))
    att_out = col_out if feature_major else row_out
    seq_out = lambda n, dt: (jax.ShapeDtypeStruct((nseq, n, seq), dt),
                             pl.BlockSpec((1, n, tm), lambda i: (i // tps, 0, i % tps)))
    ml_out = seq_out if feature_major else row_out
    outs = (
        att_out(ATTN_Q, BF16),
        row_out(ATTN_KV, BF16),
        att_out(ATTN_KV, BF16),
        (jax.ShapeDtypeStruct((nseq * WINDOW, ATTN_KV), F32), pl.BlockSpec((WINDOW, ATTN_KV), last)),
        (jax.ShapeDtypeStruct((nseq * WINDOW, ATTN_KV), F32), pl.BlockSpec((WINDOW, ATTN_KV), last)),
        ml_out(ML_W, m_dtype),
        row_out(ML_W, m_dtype),
        ml_out(ML_W, m_dtype),
        row_out(ML_W, F32),
        row_out(LANES, F32),
        seq_out(8, F32),
    )
    return pl.pallas_call(
        functools.partial(_in_proj_kernel, feature_major=feature_major),
        grid=(rows // tm,),
        in_specs=[
            pl.BlockSpec((tm, D_MODEL), row),
            pl.BlockSpec((1, D_MODEL), const),
            pl.BlockSpec((D_MODEL, N_PROJ), const),
            pl.BlockSpec((1, N_PROJ), const),
            pl.BlockSpec((tm, LANES), pos),
            pl.BlockSpec((tm, LANES), pos),
        ],
        out_specs=tuple(o[1] for o in outs),
        out_shape=tuple(o[0] for o in outs),
        compiler_params=pltpu.CompilerParams(dimension_semantics=("arbitrary",), vmem_limit_bytes=VMEM_LIMIT),
        name="in_proj",
    )(x, g, w, b, cos, sin)


def _attn_kernel(sink_ref, qt_ref, kp_ref, kc_ref, vtp_ref, vtc_ref, o_ref):
    j = pl.program_id(1)
    L = WINDOW
    nsub = o_ref.shape[0] // L
    key = lax.broadcasted_iota(jnp.int32, (2 * L, L), 0)
    t = lax.broadcasted_iota(jnp.int32, (2 * L, L), 1)
    is_prev = key < L
    allowed = (is_prev & (key >= t)) | (~is_prev & (key - L <= t))
    bias = jnp.where(allowed, 0.0, -jnp.inf)
    bias_first = jnp.where(is_prev & (j == 0), -jnp.inf, bias)
    zeros_q = jnp.zeros((HEAD_DIM, L), BF16)
    for r in range(nsub):
        ts = slice(r * L, (r + 1) * L)
        k_prev = kp_ref[...] if r == 0 else kc_ref[(r - 1) * L:r * L, :]
        vt_prev = vtp_ref[...] if r == 0 else vtc_ref[:, (r - 1) * L:r * L]
        k2 = jnp.concatenate([k_prev, kc_ref[ts, :]], axis=0)
        vt2 = jnp.concatenate([vt_prev, vtc_ref[:, ts]], axis=1)
        b_r = bias_first if r == 0 else bias
        heads = range(N_Q_HEADS)
        q_op = [jnp.concatenate([qt_ref[hd * HEAD_DIM:(hd + 1) * HEAD_DIM, ts], zeros_q][::1 if hd < GQA_GROUP else -1],
                                axis=0) for hd in heads]
        s = [_mm(k2, q_op[hd]) + b_r for hd in heads]
        m = [jnp.maximum(jnp.max(s[hd], axis=0, keepdims=True), sink_ref[hd]) for hd in heads]
        p = [jnp.exp(s[hd] - m[hd]) for hd in heads]
        den = [jnp.sum(p[hd], axis=0, keepdims=True) + jnp.exp(sink_ref[hd] - m[hd]) for hd in heads]
        o = [_mm(vt2[(hd // GQA_GROUP) * HEAD_DIM:(hd // GQA_GROUP + 1) * HEAD_DIM, :], p[hd].astype(BF16)) / den[hd]
             for hd in heads]
        o_ref[ts, :] = jnp.concatenate(o, axis=0).T.astype(o_ref.dtype)


def _attn_prompt(sinks, qt, k, vt, batch, seq, tq):
    nb = seq // WINDOW
    nt = seq // tq
    sub = tq // WINDOW
    prev_blk = lambda b, j: b * nb + jnp.maximum(j * sub - 1, 0)
    return pl.pallas_call(
        _attn_kernel,
        grid=(batch, nt),
        in_specs=[
            pl.BlockSpec(memory_space=pltpu.SMEM),
            pl.BlockSpec((ATTN_Q, tq), lambda b, j: (0, b * nt + j)),
            pl.BlockSpec((WINDOW, ATTN_KV), lambda b, j: (prev_blk(b, j), 0)),
            pl.BlockSpec((tq, ATTN_KV), lambda b, j: (b * nt + j, 0)),
            pl.BlockSpec((ATTN_KV, WINDOW), lambda b, j: (0, prev_blk(b, j))),
            pl.BlockSpec((ATTN_KV, tq), lambda b, j: (0, b * nt + j)),
        ],
        out_specs=pl.BlockSpec((tq, ATTN_Q), lambda b, j: (b * nt + j, 0)),
        out_shape=jax.ShapeDtypeStruct((batch * seq, ATTN_Q), BF16),
        compiler_params=pltpu.CompilerParams(dimension_semantics=("parallel", "parallel")),
        name="attn_prompt",
    )(sinks, qt, k, k, vt, vt)


ML_STATE_ROWS = ML_DIM + 8


def _mlstm_kernel(qt_ref, k_ref, vt_ref, og_ref, gif_ref, gt_ref, h_ref, cst_ref, nst_ref, mst_ref, c_scr, m_scr):
    c = pl.program_id(1)
    L = ML_CHUNK
    bb = k_ref.shape[0]

    @pl.when(c == 0)
    def _():
        c_scr[...] = jnp.zeros_like(c_scr)
        m_scr[...] = jnp.zeros_like(m_scr)

    s_idx = lax.broadcasted_iota(jnp.int32, (L, L), 0)
    t_idx = lax.broadcasted_iota(jnp.int32, (L, L), 1)
    causal = s_idx <= t_idx
    chains = [(b, h) for b in range(bb) for h in range(ML_HEADS)]
    hsl = lambda h: slice(h * ML_DIM, (h + 1) * ML_DIM)
    qt = {(b, h): qt_ref[b, hsl(h), :] for b, h in chains}
    kk = {(b, h): k_ref[b, :, hsl(h)] for b, h in chains}
    vt = {(b, h): vt_ref[b, hsl(h), :] for b, h in chains}
    ig_r = {(b, h): gt_ref[b, h:h + 1, :] for b, h in chains}
    b_r = {(b, h): gt_ref[b, ML_HEADS + h:ML_HEADS + h + 1, :] for b, h in chains}
    u_c = {(b, h): gif_ref[b, :, 2 * ML_HEADS + h:2 * ML_HEADS + h + 1] for b, h in chains}
    m_prev = {(b, h): m_scr[b, h:h + 1, :] for b, h in chains}
    state = {(b, h): c_scr[b, h] for b, h in chains}
    dlog = {ch: jnp.where(causal, b_r[ch] + u_c[ch], -jnp.inf) for ch in chains}
    qk = {ch: _mm(kk[ch], qt[ch]) for ch in chains}
    inter = {ch: _mm(state[ch].astype(BF16), qt[ch]) for ch in chains}
    a = {ch: b_r[ch] + m_prev[ch] for ch in chains}
    m_t = {ch: jnp.maximum(a[ch], jnp.max(dlog[ch], axis=0, keepdims=True)) for ch in chains}
    sw = {ch: qk[ch] * jnp.exp(dlog[ch] - m_t[ch]) for ch in chains}
    sv = {ch: _mm(vt[ch], sw[ch].astype(BF16)) for ch in chains}
    for b, h in chains:
        ch = (b, h)
        aw = jnp.exp(a[ch] - m_t[ch])
        num = aw * inter[ch][:ML_DIM] + sv[ch]
        den = aw * inter[ch][ML_DIM:ML_DIM + 1] + jnp.sum(sw[ch], axis=0, keepdims=True)
        hh = num / jnp.maximum(jnp.abs(den), jnp.exp(-m_t[ch]))
        h_ref[b, :, hsl(h)] = (hh.T * og_ref[b, :, hsl(h)]).astype(h_ref.dtype)
    pad = jnp.zeros((ML_STATE_ROWS - ML_DIM - 1, L), F32)
    for b, h in chains:
        ch = (b, h)
        b_end = b_r[ch][:, L - 1:L]
        g = b_end - b_r[ch] + ig_r[ch]
        m_new = jnp.maximum(b_end + m_prev[ch], jnp.max(g, axis=1, keepdims=True))
        decay = jnp.exp(b_end + m_prev[ch] - m_new)
        w = jnp.exp(g - m_new)
        v_aug = jnp.concatenate([vt[ch].astype(F32) * w, w, pad], axis=0).astype(BF16)
        c_scr[b, h] = decay * state[ch] + _mm(v_aug, kk[ch])
        m_scr[b, h:h + 1, :] = m_new

    @pl.when(c == pl.num_programs(1) - 1)
    def _():
        nst_ref[...] = jnp.zeros_like(nst_ref)
        for b, h in chains:
            final = c_scr[b, h]
            cst_ref[b, h] = final[:ML_DIM].T
            nst_ref[b, h:h + 1, :] = final[ML_DIM:ML_DIM + 1]
        mst_ref[...] = m_scr[...]


def _mlstm_prompt(mqt, mk, mvt, og, gif, gt, batch, seq, bb):
    nc = seq // ML_CHUNK
    row_blk = lambda g, c: (g, c, 0)
    col_blk = lambda g, c: (g, 0, c)
    per_seq3 = lambda g, c: (g, 0, 0)
    r3 = lambda t: t.reshape(batch, seq, t.shape[-1])
    h, cst, nst, mst = pl.pallas_call(
        _mlstm_kernel,
        grid=(batch // bb, nc),
        in_specs=[
            pl.BlockSpec((bb, ML_W, ML_CHUNK), col_blk),
            pl.BlockSpec((bb, ML_CHUNK, ML_W), row_blk),
            pl.BlockSpec((bb, ML_W, ML_CHUNK), col_blk),
            pl.BlockSpec((bb, ML_CHUNK, ML_W), row_blk),
            pl.BlockSpec((bb, ML_CHUNK, LANES), row_blk),
            pl.BlockSpec((bb, 8, ML_CHUNK), col_blk),
        ],
        out_specs=(
            pl.BlockSpec((bb, ML_CHUNK, ML_W), row_blk),
            pl.BlockSpec((bb, ML_HEADS, ML_DIM, ML_DIM), lambda g, c: (g, 0, 0, 0)),
            pl.BlockSpec((bb, 8, LANES), per_seq3),
            pl.BlockSpec((bb, 8, LANES), per_seq3),
        ),
        out_shape=(
            jax.ShapeDtypeStruct((batch, seq, ML_W), BF16),
            jax.ShapeDtypeStruct((batch, ML_HEADS, ML_DIM, ML_DIM), F32),
            jax.ShapeDtypeStruct((batch, 8, LANES), F32),
            jax.ShapeDtypeStruct((batch, 8, LANES), F32),
        ),
        scratch_shapes=[pltpu.VMEM((bb, ML_HEADS, ML_STATE_ROWS, ML_DIM), F32), pltpu.VMEM((bb, 8, LANES), F32)],
        compiler_params=pltpu.CompilerParams(dimension_semantics=("parallel", "arbitrary")),
        name="mlstm_prompt",
    )(mqt, r3(mk), mvt, r3(og), r3(gif), gt)
    return h.reshape(batch * seq, ML_W), cst, nst, mst


def _out_kernel(x_ref, a_ref, hm_ref, p_ref, gmix_ref, wg_ref, bg_ref, wau_ref, wmu_ref, wo_ref,
                gmlp_ref, wff1_ref, wff2_ref, gple_ref, wpg_ref, wpp_ref, gfin_ref, y_ref):
    x = x_ref[...]
    xn = _rms(x, gmix_ref[...]).astype(BF16)
    gates = jax.nn.sigmoid(_mm(xn, wg_ref[...]) + bg_ref[...])
    mix = gates[:, :D_MODEL] * _mm(a_ref[...], wau_ref[...]) + gates[:, D_MODEL:] * _mm(hm_ref[...], wmu_ref[...])
    h = x + _mm(mix.astype(BF16), wo_ref[...])
    u = jnp.square(jnp.maximum(_mm(_rms(h, gmlp_ref[...]).astype(BF16), wff1_ref[...]), 0.0))
    h = h + _mm(u.astype(BF16), wff2_ref[...])
    gate = jax.nn.sigmoid(_mm(_rms(h, gple_ref[...]).astype(BF16), wpg_ref[...]))
    h = h + _mm(p_ref[...].astype(BF16), wpp_ref[...]) * gate
    y_ref[...] = _rms(h, gfin_ref[...])


def _out_proj(x, a, hm, p, weights, tm):
    rows = x.shape[0]
    row = lambda i: (i, 0)
    const = lambda i: (0, 0)
    w_specs = [pl.BlockSpec(w.shape, const, pipeline_mode=pl.Buffered(1)) for w in weights]
    return pl.pallas_call(
        _out_kernel,
        grid=(rows // tm,),
        in_specs=[
            pl.BlockSpec((tm, D_MODEL), row),
            pl.BlockSpec((tm, ATTN_Q), row),
            pl.BlockSpec((tm, ML_W), row),
            pl.BlockSpec((tm, D_PLE), row),
        ] + w_specs,
        out_specs=pl.BlockSpec((tm, D_MODEL), row),
        out_shape=jax.ShapeDtypeStruct((rows, D_MODEL), F32),
        compiler_params=pltpu.CompilerParams(dimension_semantics=("parallel",), vmem_limit_bytes=VMEM_LIMIT),
        name="out_proj",
    )(x, a, hm, p, *weights)


def _attn_decode_kernel(sink_ref, qm_ref, kn_ref, vn_ref, ck_ref, cv_ref, o_ref, ko_ref, vo_ref):
    bt = qm_ref.shape[0]
    qm = qm_ref[...]
    kn = kn_ref[...]
    vn = vn_ref[...]
    ck = ck_ref[...]
    cv = cv_ref[...]
    sink = sink_ref[...]
    s = jnp.einsum('bhc,bck->bhk', qm, ck.astype(BF16), preferred_element_type=F32)
    s_new = jnp.sum(qm.astype(F32) * kn[:, None, :], axis=2, keepdims=True)
    m = jnp.maximum(jnp.maximum(jnp.max(s, axis=2, keepdims=True), s_new), sink)
    p = jnp.exp(s - m)
    p_new = jnp.exp(s_new - m)
    den = jnp.sum(p, axis=2, keepdims=True) + p_new + jnp.exp(sink - m)
    o = jnp.einsum('bhk,bck->bhc', p.astype(BF16), cv.astype(BF16), preferred_element_type=F32)
    o_ref[...] = (o + p_new * vn[:, None, :]) / den
    pad = jnp.zeros((LANES - bt, ATTN_KV), F32)
    kn_t = jnp.concatenate([kn, pad], axis=0).T
    vn_t = jnp.concatenate([vn, pad], axis=0).T
    newest = lax.broadcasted_iota(jnp.int32, (ATTN_KV, WINDOW), 1) == WINDOW - 1
    for b in range(bt):
        ko_ref[b] = jnp.where(newest, kn_t[:, b:b + 1], pltpu.roll(ck[b], WINDOW - 1, 1))
        vo_ref[b] = jnp.where(newest, vn_t[:, b:b + 1], pltpu.roll(cv[b], WINDOW - 1, 1))


def _attn_decode(sinks_col, qmat, k_new, v_new, cache_k, cache_v, bt):
    nbatch = qmat.shape[0]
    b3 = lambda i: (i, 0, 0)
    b2 = lambda i: (i, 0)
    return pl.pallas_call(
        _attn_decode_kernel,
        grid=(nbatch // bt,),
        in_specs=[
            pl.BlockSpec((N_Q_HEADS, 1), lambda i: (0, 0)),
            pl.BlockSpec((bt, N_Q_HEADS, ATTN_KV), b3),
            pl.BlockSpec((bt, ATTN_KV), b2),
            pl.BlockSpec((bt, ATTN_KV), b2),
            pl.BlockSpec((bt, WINDOW, ATTN_KV), b3),
            pl.BlockSpec((bt, WINDOW, ATTN_KV), b3),
        ],
        out_specs=(
            pl.BlockSpec((bt, N_Q_HEADS, ATTN_KV), b3),
            pl.BlockSpec((bt, WINDOW, ATTN_KV), b3),
            pl.BlockSpec((bt, WINDOW, ATTN_KV), b3),
        ),
        out_shape=(
            jax.ShapeDtypeStruct((nbatch, N_Q_HEADS, ATTN_KV), F32),
            jax.ShapeDtypeStruct((nbatch, WINDOW, ATTN_KV), F32),
            jax.ShapeDtypeStruct((nbatch, WINDOW, ATTN_KV), F32),
        ),
        compiler_params=pltpu.CompilerParams(dimension_semantics=("parallel",)),
        name="attn_decode",
    )(sinks_col, qmat, k_new, v_new, cache_k, cache_v)


def _mlstm_decode_kernel(q_ref, k_ref, v_ref, og_ref, gif_ref, c_ref, n_ref, m_ref, h_ref, co_ref, no_ref, mo_ref):
    bt = q_ref.shape[0]
    gif = gif_ref[...]
    ig = gif[:, 0:ML_HEADS]
    lf = gif[:, ML_HEADS:2 * ML_HEADS]
    m_prev = m_ref[...]
    a = lf + m_prev
    m_t = jnp.maximum(a, ig)
    aw = jnp.exp(a - m_t)
    e_i = jnp.exp(ig - m_t)
    floor = jnp.exp(-m_t)
    m_new = jnp.maximum(a, ig)
    decay = jnp.exp(a - m_new)
    w = jnp.exp(ig - m_new)
    mo_ref[...] = m_new
    pad = jnp.zeros((LANES - bt, ML_DIM), F32)
    lane = lax.broadcasted_iota(jnp.int32, (ML_DIM, LANES), 1)
    for h in range(ML_HEADS):
        hs = slice(h * ML_DIM, (h + 1) * ML_DIM)
        q = q_ref[:, hs]
        k = k_ref[:, hs]
        v = v_ref[:, hs]
        n = n_ref[:, h, :]
        qk = jnp.sum(q * k, axis=1, keepdims=True)
        qn = jnp.sum(q * n, axis=1, keepdims=True)
        sw = qk * e_i[:, h:h + 1]
        den = aw[:, h:h + 1] * qn + sw
        inv = 1.0 / jnp.maximum(jnp.abs(den), floor[:, h:h + 1])
        no_ref[:, h, :] = decay[:, h:h + 1] * n + w[:, h:h + 1] * k
        q_b = q.astype(BF16)
        wk_t = jnp.concatenate([w[:, h:h + 1] * k, pad], axis=0).T.astype(BF16)
        v_rows = jnp.concatenate([v, pad], axis=0).astype(BF16)
        nums = []
        for b in range(bt):
            C = c_ref[b, h]
            q_c = _mm(q_b, C.astype(BF16))[b:b + 1, :]
            nums.append(aw[b:b + 1, h:h + 1] * q_c)
            outer = _mm(jnp.where(lane == b, wk_t, jnp.zeros_like(wk_t)), v_rows)
            co_ref[b, h] = decay[b:b + 1, h:h + 1] * C + outer
        num = jnp.concatenate(nums, axis=0) + sw * v
        h_ref[:, hs] = (num * inv * og_ref[:, hs]).astype(h_ref.dtype)


def _mlstm_decode(mq, mk, mv, og, gif, state_c, state_n, state_m, bt):
    nbatch = mq.shape[0]
    b2 = lambda i: (i, 0)
    b3 = lambda i: (i, 0, 0)
    b4 = lambda i: (i, 0, 0, 0)
    return pl.pallas_call(
        _mlstm_decode_kernel,
        grid=(nbatch // bt,),
        in_specs=[
            pl.BlockSpec((bt, ML_W), b2),
            pl.BlockSpec((bt, ML_W), b2),
            pl.BlockSpec((bt, ML_W), b2),
            pl.BlockSpec((bt, ML_W), b2),
            pl.BlockSpec((bt, LANES), b2),
            pl.BlockSpec((bt, ML_HEADS, ML_DIM, ML_DIM), b4),
            pl.BlockSpec((bt, ML_HEADS, ML_DIM), b3),
            pl.BlockSpec((bt, ML_HEADS), b2),
        ],
        out_specs=(
            pl.BlockSpec((bt, ML_W), b2),
            pl.BlockSpec((bt, ML_HEADS, ML_DIM, ML_DIM), b4),
            pl.BlockSpec((bt, ML_HEADS, ML_DIM), b3),
            pl.BlockSpec((bt, ML_HEADS), b2),
        ),
        out_shape=(
            jax.ShapeDtypeStruct((nbatch, ML_W), BF16),
            jax.ShapeDtypeStruct((nbatch, ML_HEADS, ML_DIM, ML_DIM), F32),
            jax.ShapeDtypeStruct((nbatch, ML_HEADS, ML_DIM), F32),
            jax.ShapeDtypeStruct((nbatch, ML_HEADS), F32),
        ),
        compiler_params=pltpu.CompilerParams(dimension_semantics=("parallel",)),
        name="mlstm_decode",
    )(mq, mk, mv, og, gif, state_c, state_n, state_m)


def _rope_tables(pos):
    half = HEAD_DIM // 2
    inv = 1.0 / (ROPE_THETA ** (jnp.arange(half, dtype=F32) / half))
    ang = pos.astype(F32)[:, None] * inv[None, :]
    cos = jnp.cos(ang)
    sin = jnp.sin(ang)
    cos_t = jnp.concatenate([cos, cos, cos, cos], axis=1)
    sin_t = jnp.concatenate([-sin, sin, -sin, sin], axis=1)
    return cos_t, sin_t


def kernel(x_prompt, x_sample, cache_k, cache_v, state_C, state_n, state_m, p_prompt, p_sample, norm_mix, w_in, b_in, attn_sinks, w_attn_up, w_ml_up, w_o, norm_mlp, w_ff1, w_ff2, norm_ple, w_ple_gate, w_ple_proj, norm_final):
    assert w_in.shape[0] == 1, "single layer"
    batch, seq, _ = x_prompt.shape
    nbatch = x_sample.shape[0]
    assert x_sample.shape[1] == 1

    w_in0 = w_in[0]
    b_in0 = b_in[0]
    w_proj, w_gate, w_au, w_mu, w_out, w_f1, w_f2, w_pg, w_pp = _weight_cast(
        w_in0, (w_attn_up[0], w_ml_up[0], w_o[0], w_ff1[0], w_ff2[0], w_ple_gate[0], w_ple_proj[0]))
    b_proj = b_in0[:N_PROJ][None, :]
    b_gate = b_in0[OFF_GA:][None, :]
    g_mix = norm_mix[0][None, :]
    out_weights = (
        g_mix, w_gate, b_gate, w_au, w_mu, w_out,
        norm_mlp[0][None, :], w_f1, w_f2,
        norm_ple[0][None, :], w_pg, w_pp, norm_final[None, :],
    )
    sinks = attn_sinks[0]

    cos_p, sin_p = _rope_tables(jnp.arange(seq, dtype=jnp.int32))
    xp = x_prompt.reshape(batch * seq, D_MODEL)
    qt, k, vt, k_last, v_last, mqt, mk, mvt, og, gif, gt = _in_proj(xp, g_mix, w_proj, b_proj, cos_p, sin_p, tm=512,
                                                                    m_dtype=BF16, feature_major=True)
    a_out = _attn_prompt(sinks, qt, k, vt, batch, seq, tq=512)
    h_m, cst, nst, mst = _mlstm_prompt(mqt, mk, mvt, og, gif, gt, batch, seq, bb=2)
    y_prompt = _out_proj(xp, a_out, h_m, p_prompt[0].reshape(batch * seq, D_PLE), out_weights, tm=256)
    y_prompt = y_prompt.reshape(batch, seq, D_MODEL)
    k_prompt = k_last.reshape(1, batch, WINDOW, N_KV_HEADS, HEAD_DIM)
    v_prompt = v_last.reshape(1, batch, WINDOW, N_KV_HEADS, HEAD_DIM)
    c_prompt = cst[None]
    n_prompt = nst[:, :ML_HEADS, :][None]
    m_prompt = mst[:, :ML_HEADS, 0][None]

    cos_s, sin_s = _rope_tables(jnp.full((nbatch,), PAST_LEN, dtype=jnp.int32))
    xs = x_sample.reshape(nbatch, D_MODEL)
    qs, _, _, ks, vs, mqs, mks, mvs, ogs, gifs, _ = _in_proj(xs, g_mix, w_proj, b_proj, cos_s, sin_s, tm=nbatch,
                                                          m_dtype=F32, feature_major=False)
    q4 = qs.reshape(nbatch, N_KV_HEADS, GQA_GROUP, 1, HEAD_DIM)
    sel = jnp.eye(N_KV_HEADS, dtype=BF16)[None, :, None, :, None]
    qmat = (q4 * sel).reshape(nbatch, N_Q_HEADS, ATTN_KV)
    to_fp = lambda c: c[0].transpose(0, 2, 3, 1).reshape(nbatch, ATTN_KV, WINDOW)
    from_fp = lambda c: c.reshape(nbatch, N_KV_HEADS, HEAD_DIM, WINDOW).transpose(0, 3, 1, 2)[None]
    o_full, k_s, v_s = _attn_decode(sinks[:, None], qmat, ks, vs, to_fp(cache_k), to_fp(cache_v), bt=8)
    o5 = o_full.reshape(nbatch, N_KV_HEADS, GQA_GROUP, N_KV_HEADS, HEAD_DIM)
    a_s = jnp.stack([o5[:, kv, :, kv, :] for kv in range(N_KV_HEADS)], axis=1).reshape(nbatch, ATTN_Q).astype(BF16)
    h_s, c_s, n_s, m_s = _mlstm_decode(mqs, mks, mvs, ogs, gifs, state_C[0], state_n[0], state_m[0], bt=8)
    y_sample = _out_proj(xs, a_s, h_s, p_sample[0].reshape(nbatch, D_PLE), out_weights, tm=nbatch)
    y_sample = y_sample.reshape(nbatch, 1, D_MODEL)
    k_sample = from_fp(k_s)
    v_sample = from_fp(v_s)

    return (y_prompt, y_sample, k_prompt, v_prompt, c_prompt, n_prompt, m_prompt,
            k_sample, v_sample, c_s[None], n_s[None], m_s[None])
```

```python
import functools

import numpy as np
import jax
import jax.numpy as jnp
from jax import lax
from jax.experimental import pallas as pl
from jax.experimental.pallas import tpu as pltpu

F32 = jnp.float32
BF16 = jnp.bfloat16

D_MODEL = 1024
HEAD_DIM = 64
N_Q_HEADS = 8
N_KV_HEADS = 2
GQA_GROUP = N_Q_HEADS // N_KV_HEADS
WINDOW = 128
ROPE_THETA = 10000.0
ML_HEADS = 4
ML_DIM = 128
ML_CHUNK = 128
D_FF = 4 * D_MODEL
D_PLE = 256
EPS = 1e-6
PAST_LEN = 16384

ATTN_Q = N_Q_HEADS * HEAD_DIM
ATTN_KV = N_KV_HEADS * HEAD_DIM
ML_W = ML_HEADS * ML_DIM
LANES = 128

OFF_AQ = 0
OFF_AK = OFF_AQ + ATTN_Q
OFF_AV = OFF_AK + ATTN_KV
OFF_MQ = OFF_AV + ATTN_KV
OFF_MK = OFF_MQ + ML_W
OFF_MV = OFF_MK + ML_W
OFF_MO = OFF_MV + ML_W
OFF_MI = OFF_MO + ML_W
OFF_MF = OFF_MI + ML_HEADS
OFF_GA = OFF_MF + ML_HEADS
OFF_GM = OFF_GA + D_MODEL
D_IN = OFF_GM + D_MODEL
N_PROJ = OFF_MI + LANES

VMEM_LIMIT = 60 * 1024 * 1024

def _rms(x, g):
    r = lax.rsqrt(jnp.mean(x * x, axis=-1, keepdims=True) + EPS)
    return (x * r) * g


def _mm(a, b):
    return jnp.dot(a, b, preferred_element_type=F32)


def _mm_nt(a, b):
    return lax.dot_general(a, b, (((1,), (1,)), ((), ())), preferred_element_type=F32)


def _mm_tn(a, b):
    return lax.dot_general(a, b, (((0,), (0,)), ((), ())), preferred_element_type=F32)


def _log_sigmoid(x):
    return jnp.minimum(x, 0.0) - jnp.log1p(jnp.exp(-jnp.abs(x)))


WEIGHT_ROW_SPLIT = 8


def _weight_cast_kernel(*refs):
    n_plain = (len(refs) - 3) // 2
    w_in_t_ref, plain_in = refs[0], refs[1:1 + n_plain]
    w_proj_ref, w_gate_ref, plain_out = refs[1 + n_plain], refs[2 + n_plain], refs[3 + n_plain:]
    for g in range(N_PROJ // LANES):
        w_proj_ref[:, g * LANES:(g + 1) * LANES] = w_in_t_ref[g * LANES:(g + 1) * LANES, :].T.astype(BF16)
    for g in range(2 * D_MODEL // LANES):
        w_gate_ref[:, g * LANES:(g + 1) * LANES] = w_in_t_ref[OFF_GA + g * LANES:OFF_GA + (g + 1) * LANES, :].T.astype(BF16)
    for src, dst in zip(plain_in, plain_out):
        dst[...] = src[...].astype(BF16)


def _weight_cast(w_in_t, plain):
    split = WEIGHT_ROW_SPLIT
    assert D_MODEL // split == LANES
    rows_blk = lambda w: pl.BlockSpec((w.shape[0] // split, w.shape[1]), lambda i: (i, 0))
    out_shapes = (jax.ShapeDtypeStruct((D_MODEL, N_PROJ), BF16), jax.ShapeDtypeStruct((D_MODEL, 2 * D_MODEL), BF16),
                  *(jax.ShapeDtypeStruct(w.shape, BF16) for w in plain))
    return pl.pallas_call(
        _weight_cast_kernel,
        grid=(split,),
        in_specs=[pl.BlockSpec((D_IN, LANES), lambda i: (0, i))] + [rows_blk(w) for w in plain],
        out_specs=tuple(rows_blk(s) for s in out_shapes),
        out_shape=out_shapes,
        compiler_params=pltpu.CompilerParams(dimension_semantics=("parallel",), vmem_limit_bytes=VMEM_LIMIT),
        name="weight_cast",
    )(w_in_t, *plain)


def _in_proj_kernel(x_ref, g_ref, w_ref, b_ref, cos_ref, sin_ref,
                    q_ref, k_ref, v_ref, kl_ref, vl_ref, mq_ref, mk_ref, mv_ref, og_ref, gif_ref, gt_ref,
                    *, feature_major):
    xn = _rms(x_ref[...], g_ref[...]).astype(BF16)

    def proj(lo, hi):
        return _mm(xn, w_ref[:, lo:hi]) + b_ref[:, lo:hi]

    cos = cos_ref[...]
    sin = sin_ref[...]
    lane = lax.broadcasted_iota(jnp.int32, (1, LANES), 1)
    first_half = (lane % HEAD_DIM) < (HEAD_DIM // 2)

    def rope(z):
        partner = jnp.where(first_half, pltpu.roll(z, LANES - HEAD_DIM // 2, 1), pltpu.roll(z, HEAD_DIM // 2, 1))
        return z * cos + partner * sin

    zq = proj(OFF_AQ, OFF_AK)
    for c in range(ATTN_Q // LANES):
        sl = slice(c * LANES, (c + 1) * LANES)
        qc = rope(zq[:, sl]) * (HEAD_DIM ** -0.5)
        if feature_major:
            q_ref[sl, :] = qc.T.astype(q_ref.dtype)
        else:
            q_ref[:, sl] = qc.astype(q_ref.dtype)
    zkv = proj(OFF_AK, OFF_MQ)
    k = rope(zkv[:, :ATTN_KV])
    v = zkv[:, ATTN_KV:]
    k_ref[...] = k.astype(k_ref.dtype)
    v_ref[...] = (v.T if feature_major else v).astype(v_ref.dtype)
    tm = k.shape[0]
    kl_ref[...] = k[tm - WINDOW:, :]
    vl_ref[...] = v[tm - WINDOW:, :]
    zmq = proj(OFF_MQ, OFF_MK)
    zmv = proj(OFF_MV, OFF_MO)
    if feature_major:
        for h in range(ML_HEADS):
            hs = slice(h * ML_DIM, (h + 1) * ML_DIM)
            mq_ref[0, hs, :] = zmq[:, hs].T.astype(mq_ref.dtype)
            mv_ref[0, hs, :] = zmv[:, hs].T.astype(mv_ref.dtype)
    else:
        mq_ref[...] = zmq.astype(mq_ref.dtype)
        mv_ref[...] = zmv.astype(mv_ref.dtype)
    mk_ref[...] = (proj(OFF_MK, OFF_MV) * (ML_DIM ** -0.5)).astype(mk_ref.dtype)
    og_ref[...] = jax.nn.sigmoid(proj(OFF_MO, OFF_MI))
    zg = proj(OFF_MI, N_PROJ)
    gif = jnp.where(lane < ML_HEADS, zg, jnp.where(lane < 2 * ML_HEADS, _log_sigmoid(zg), 0.0))
    if feature_major:
        L = ML_CHUNK
        tri = (lax.broadcasted_iota(jnp.int32, (L, L), 1) <= lax.broadcasted_iota(jnp.int32, (L, L), 0)).astype(F32)
        for c in range(tm // L):
            cs = slice(c * L, (c + 1) * L)
            G = gif[cs, :]
            cum = jnp.dot(tri, G, precision=lax.Precision.HIGHEST, preferred_element_type=F32)
            A = jnp.where(lane < ML_HEADS, G, cum)
            diff = pltpu.roll(pltpu.roll(A, ML_HEADS, 1) - A, ML_HEADS, 1)
            A = jnp.where(lane < 2 * ML_HEADS, A, jnp.where(lane < 3 * ML_HEADS, diff, 0.0))
            gif_ref[cs, :] = A
            gt_ref[0, :, cs] = A.T[0:8, :]
    else:
        gif_ref[...] = gif
        gt_ref[0] = jnp.zeros(gt_ref.shape[1:], F32)


def _in_proj(x, g, w, b, cos, sin, tm, m_dtype, feature_major):
    rows = x.shape[0]
    seq = cos.shape[0]
    tps = seq // tm
    nseq = rows // seq
    row = lambda i: (i, 0)
    const = lambda i: (0, 0)
    pos = lambda i: (i % tps, 0)
    last = lambda i: (i // tps, 0)
    row_out = lambda n, dt: (jax.ShapeDtypeStruct((rows, n), dt), pl.BlockSpec((tm, n), row))
    col_out = lambda n, dt: (jax.ShapeDtypeStruct((n, rows), dt), pl.BlockSpec((n, tm), lambda i: (0, i)))
    att_out = col_out if feature_major else row_out
    seq_out = lambda n, dt: (jax.ShapeDtypeStruct((nseq, n, seq), dt),
                             pl.BlockSpec((1, n, tm), lambda i: (i // tps, 0, i % tps)))
    ml_out = seq_out if feature_major else row_out
    outs = (
        att_out(ATTN_Q, BF16),
        row_out(ATTN_KV, BF16),
        att_out(ATTN_KV, BF16),
        (jax.ShapeDtypeStruct((nseq * WINDOW, ATTN_KV), F32), pl.BlockSpec((WINDOW, ATTN_KV), last)),
        (jax.ShapeDtypeStruct((nseq * WINDOW, ATTN_KV), F32), pl.BlockSpec((WINDOW, ATTN_KV), last)),
        ml_out(ML_W, m_dtype),
        row_out(ML_W, m_dtype),
        ml_out(ML_W, m_dtype),
        row_out(ML_W, F32),
        row_out(LANES, F32),
        seq_out(8, F32),
    )
    return pl.pallas_call(
        functools.partial(_in_proj_kernel, feature_major=feature_major),
        grid=(rows // tm,),
        in_specs=[
            pl.BlockSpec((tm, D_MODEL), row),
            pl.BlockSpec((1, D_MODEL), const),
            pl.BlockSpec((D_MODEL, N_PROJ), const),
            pl.BlockSpec((1, N_PROJ), const),
            pl.BlockSpec((tm, LANES), pos),
            pl.BlockSpec((tm, LANES), pos),
        ],
        out_specs=tuple(o[1] for o in outs),
        out_shape=tuple(o[0] for o in outs),
        compiler_params=pltpu.CompilerParams(dimension_semantics=("arbitrary",), vmem_limit_bytes=VMEM_LIMIT),
        name="in_proj",
    )(x, g, w, b, cos, sin)


def _attn_kernel(sink_ref, qt_ref, kp_ref, kc_ref, vtp_ref, vtc_ref, o_ref):
    j = pl.program_id(1)
    L = WINDOW
    nsub = o_ref.shape[0] // L
    key = lax.broadcasted_iota(jnp.int32, (2 * L, L), 0)
    t = lax.broadcasted_iota(jnp.int32, (2 * L, L), 1)
    is_prev = key < L
    allowed = (is_prev & (key >= t)) | (~is_prev & (key - L <= t))
    bias = jnp.where(allowed, 0.0, -jnp.inf)
    bias_first = jnp.where(is_prev & (j == 0), -jnp.inf, bias)
    zeros_q = jnp.zeros((HEAD_DIM, L), BF16)
    for r in range(nsub):
        ts = slice(r * L, (r + 1) * L)
        k_prev = kp_ref[...] if r == 0 else kc_ref[(r - 1) * L:r * L, :]
        vt_prev = vtp_ref[...] if r == 0 else vtc_ref[:, (r - 1) * L:r * L]
        k2 = jnp.concatenate([k_prev, kc_ref[ts, :]], axis=0)
        vt2 = jnp.concatenate([vt_prev, vtc_ref[:, ts]], axis=1)
        b_r = bias_first if r == 0 else bias
        heads = range(N_Q_HEADS)
        q_op = [jnp.concatenate([qt_ref[hd * HEAD_DIM:(hd + 1) * HEAD_DIM, ts], zeros_q][::1 if hd < GQA_GROUP else -1],
                                axis=0) for hd in heads]
        s = [_mm(k2, q_op[hd]) + b_r for hd in heads]
        m = [jnp.maximum(jnp.max(s[hd], axis=0, keepdims=True), sink_ref[hd]) for hd in heads]
        p = [jnp.exp(s[hd] - m[hd]) for hd in heads]
        den = [jnp.sum(p[hd], axis=0, keepdims=True) + jnp.exp(sink_ref[hd] - m[hd]) for hd in heads]
        o = [_mm(vt2[(hd // GQA_GROUP) * HEAD_DIM:(hd // GQA_GROUP + 1) * HEAD_DIM, :], p[hd].astype(BF16)) / den[hd]
             for hd in heads]
        o_ref[ts, :] = jnp.concatenate(o, axis=0).T.astype(o_ref.dtype)


def _attn_prompt(sinks, qt, k, vt, batch, seq, tq):
    nb = seq // WINDOW
    nt = seq // tq
    sub = tq // WINDOW
    prev_blk = lambda b, j: b * nb + jnp.maximum(j * sub - 1, 0)
    return pl.pallas_call(
        _attn_kernel,
        grid=(batch, nt),
        in_specs=[
            pl.BlockSpec(memory_space=pltpu.SMEM),
            pl.BlockSpec((ATTN_Q, tq), lambda b, j: (0, b * nt + j)),
            pl.BlockSpec((WINDOW, ATTN_KV), lambda b, j: (prev_blk(b, j), 0)),
            pl.BlockSpec((tq, ATTN_KV), lambda b, j: (b * nt + j, 0)),
            pl.BlockSpec((ATTN_KV, WINDOW), lambda b, j: (0, prev_blk(b, j))),
            pl.BlockSpec((ATTN_KV, tq), lambda b, j: (0, b * nt + j)),
        ],
        out_specs=pl.BlockSpec((tq, ATTN_Q), lambda b, j: (b * nt + j, 0)),
        out_shape=jax.ShapeDtypeStruct((batch * seq, ATTN_Q), BF16),
        compiler_params=pltpu.CompilerParams(dimension_semantics=("parallel", "parallel")),
        name="attn_prompt",
    )(sinks, qt, k, k, vt, vt)


ML_STATE_ROWS = ML_DIM + 8


def _mlstm_kernel(qt_ref, k_ref, vt_ref, og_ref, gif_ref, gt_ref, h_ref, cst_ref, nst_ref, mst_ref, c_scr, m_scr):
    c = pl.program_id(1)
    L = ML_CHUNK
    bb = k_ref.shape[0]

    @pl.when(c == 0)
    def _():
        c_scr[...] = jnp.zeros_like(c_scr)
        m_scr[...] = jnp.zeros_like(m_scr)

    s_idx = lax.broadcasted_iota(jnp.int32, (L, L), 0)
    t_idx = lax.broadcasted_iota(jnp.int32, (L, L), 1)
    causal = s_idx <= t_idx
    chains = [(b, h) for b in range(bb) for h in range(ML_HEADS)]
    hsl = lambda h: slice(h * ML_DIM, (h + 1) * ML_DIM)
    qt = {(b, h): qt_ref[b, hsl(h), :] for b, h in chains}
    kk = {(b, h): k_ref[b, :, hsl(h)] for b, h in chains}
    vt = {(b, h): vt_ref[b, hsl(h), :] for b, h in chains}
    ig_r = {(b, h): gt_ref[b, h:h + 1, :] for b, h in chains}
    b_r = {(b, h): gt_ref[b, ML_HEADS + h:ML_HEADS + h + 1, :] for b, h in chains}
    u_c = {(b, h): gif_ref[b, :, 2 * ML_HEADS + h:2 * ML_HEADS + h + 1] for b, h in chains}
    m_prev = {(b, h): m_scr[b, h:h + 1, :] for b, h in chains}
    state = {(b, h): c_scr[b, h] for b, h in chains}
    dlog = {ch: jnp.where(causal, b_r[ch] + u_c[ch], -jnp.inf) for ch in chains}
    qk = {ch: _mm(kk[ch], qt[ch]) for ch in chains}
    inter = {ch: _mm(state[ch].astype(BF16), qt[ch]) for ch in chains}
    a = {ch: b_r[ch] + m_prev[ch] for ch in chains}
    m_t = {ch: jnp.maximum(a[ch], jnp.max(dlog[ch], axis=0, keepdims=True)) for ch in chains}
    sw = {ch: qk[ch] * jnp.exp(dlog[ch] - m_t[ch]) for ch in chains}
    sv = {ch: _mm(vt[ch], sw[ch].astype(BF16)) for ch in chains}
    for b, h in chains:
        ch = (b, h)
        aw = jnp.exp(a[ch] - m_t[ch])
        num = aw * inter[ch][:ML_DIM] + sv[ch]
        den = aw * inter[ch][ML_DIM:ML_DIM + 1] + jnp.sum(sw[ch], axis=0, keepdims=True)
        hh = num / jnp.maximum(jnp.abs(den), jnp.exp(-m_t[ch]))
        h_ref[b, :, hsl(h)] = (hh.T * og_ref[b, :, hsl(h)]).astype(h_ref.dtype)
    pad = jnp.zeros((ML_STATE_ROWS - ML_DIM - 1, L), F32)
    for b, h in chains:
        ch = (b, h)
        b_end = b_r[ch][:, L - 1:L]
        g = b_end - b_r[ch] + ig_r[ch]
        m_new = jnp.maximum(b_end + m_prev[ch], jnp.max(g, axis=1, keepdims=True))
        decay = jnp.exp(b_end + m_prev[ch] - m_new)
        w = jnp.exp(g - m_new)
        v_aug = jnp.concatenate([vt[ch].astype(F32) * w, w, pad], axis=0).astype(BF16)
        c_scr[b, h] = decay * state[ch] + _mm(v_aug, kk[ch])
        m_scr[b, h:h + 1, :] = m_new

    @pl.when(c == pl.num_programs(1) - 1)
    def _():
        nst_ref[...] = jnp.zeros_like(nst_ref)
        for b, h in chains:
            final = c_scr[b, h]
            cst_ref[b, h] = final[:ML_DIM].T
            nst_ref[b, h:h + 1, :] = final[ML_DIM:ML_DIM + 1]
        mst_ref[...] = m_scr[...]


def _mlstm_prompt(mqt, mk, mvt, og, gif, gt, batch, seq, bb):
    nc = seq // ML_CHUNK
    row_blk = lambda g, c: (g, c, 0)
    col_blk = lambda g, c: (g, 0, c)
    per_seq3 = lambda g, c: (g, 0, 0)
    r3 = lambda t: t.reshape(batch, seq, t.shape[-1])
    h, cst, nst, mst = pl.pallas_call(
        _mlstm_kernel,
        grid=(batch // bb, nc),
        in_specs=[
            pl.BlockSpec((bb, ML_W, ML_CHUNK), col_blk),
            pl.BlockSpec((bb, ML_CHUNK, ML_W), row_blk),
            pl.BlockSpec((bb, ML_W, ML_CHUNK), col_blk),
            pl.BlockSpec((bb, ML_CHUNK, ML_W), row_blk),
            pl.BlockSpec((bb, ML_CHUNK, LANES), row_blk),
            pl.BlockSpec((bb, 8, ML_CHUNK), col_blk),
        ],
        out_specs=(
            pl.BlockSpec((bb, ML_CHUNK, ML_W), row_blk),
            pl.BlockSpec((bb, ML_HEADS, ML_DIM, ML_DIM), lambda g, c: (g, 0, 0, 0)),
            pl.BlockSpec((bb, 8, LANES), per_seq3),
            pl.BlockSpec((bb, 8, LANES), per_seq3),
        ),
        out_shape=(
            jax.ShapeDtypeStruct((batch, seq, ML_W), BF16),
            jax.ShapeDtypeStruct((batch, ML_HEADS, ML_DIM, ML_DIM), F32),
            jax.ShapeDtypeStruct((batch, 8, LANES), F32),
            jax.ShapeDtypeStruct((batch, 8, LANES), F32),
        ),
        scratch_shapes=[pltpu.VMEM((bb, ML_HEADS, ML_STATE_ROWS, ML_DIM), F32), pltpu.VMEM((bb, 8, LANES), F32)],
        compiler_params=pltpu.CompilerParams(dimension_semantics=("parallel", "arbitrary")),
        name="mlstm_prompt",
    )(mqt, r3(mk), mvt, r3(og), r3(gif), gt)
    return h.reshape(batch * seq, ML_W), cst, nst, mst


def _out_kernel(x_ref, a_ref, hm_ref, p_ref, gmix_ref, wg_ref, bg_ref, wau_ref, wmu_ref, wo_ref,
                gmlp_ref, wff1_ref, wff2_ref, gple_ref, wpg_ref, wpp_ref, gfin_ref, y_ref):
    x = x_ref[...]
    xn = _rms(x, gmix_ref[...]).astype(BF16)
    gates = jax.nn.sigmoid(_mm(xn, wg_ref[...]) + bg_ref[...])
    mix = gates[:, :D_MODEL] * _mm(a_ref[...], wau_ref[...]) + gates[:, D_MODEL:] * _mm(hm_ref[...], wmu_ref[...])
    h = x + _mm(mix.astype(BF16), wo_ref[...])
    u = jnp.square(jnp.maximum(_mm(_rms(h, gmlp_ref[...]).astype(BF16), wff1_ref[...]), 0.0))
    h = h + _mm(u.astype(BF16), wff2_ref[...])
    gate = jax.nn.sigmoid(_mm(_rms(h, gple_ref[...]).astype(BF16), wpg_ref[...]))
    h = h + _mm(p_ref[...].astype(BF16), wpp_ref[...]) * gate
    y_ref[...] = _rms(h, gfin_ref[...])


def _out_proj(x, a, hm, p, weights, tm):
    rows = x.shape[0]
    row = lambda i: (i, 0)
    const = lambda i: (0, 0)
    w_specs = [pl.BlockSpec(w.shape, const, pipeline_mode=pl.Buffered(1)) for w in weights]
    return pl.pallas_call(
        _out_kernel,
        grid=(rows // tm,),
        in_specs=[
            pl.BlockSpec((tm, D_MODEL), row),
            pl.BlockSpec((tm, ATTN_Q), row),
            pl.BlockSpec((tm, ML_W), row),
            pl.BlockSpec((tm, D_PLE), row),
        ] + w_specs,
        out_specs=pl.BlockSpec((tm, D_MODEL), row),
        out_shape=jax.ShapeDtypeStruct((rows, D_MODEL), F32),
        compiler_params=pltpu.CompilerParams(dimension_semantics=("parallel",), vmem_limit_bytes=VMEM_LIMIT),
        name="out_proj",
    )(x, a, hm, p, *weights)


def _attn_decode_kernel(sink_ref, qm_ref, kn_ref, vn_ref, ck_ref, cv_ref, o_ref, ko_ref, vo_ref):
    bt = qm_ref.shape[0]
    qm = qm_ref[...]
    kn = kn_ref[...]
    vn = vn_ref[...]
    ck = ck_ref[...]
    cv = cv_ref[...]
    sink = sink_ref[...]
    s = jnp.einsum('bhc,bck->bhk', qm, ck.astype(BF16), preferred_element_type=F32)
    s_new = jnp.sum(qm.astype(F32) * kn[:, None, :], axis=2, keepdims=True)
    m = jnp.maximum(jnp.maximum(jnp.max(s, axis=2, keepdims=True), s_new), sink)
    p = jnp.exp(s - m)
    p_new = jnp.exp(s_new - m)
    den = jnp.sum(p, axis=2, keepdims=True) + p_new + jnp.exp(sink - m)
    o = jnp.einsum('bhk,bck->bhc', p.astype(BF16), cv.astype(BF16), preferred_element_type=F32)
    o_ref[...] = (o + p_new * vn[:, None, :]) / den
    pad = jnp.zeros((LANES - bt, ATTN_KV), F32)
    kn_t = jnp.concatenate([kn, pad], axis=0).T
    vn_t = jnp.concatenate([vn, pad], axis=0).T
    newest = lax.broadcasted_iota(jnp.int32, (ATTN_KV, WINDOW), 1) == WINDOW - 1
    for b in range(bt):
        ko_ref[b] = jnp.where(newest, kn_t[:, b:b + 1], pltpu.roll(ck[b], WINDOW - 1, 1))
        vo_ref[b] = jnp.where(newest, vn_t[:, b:b + 1], pltpu.roll(cv[b], WINDOW - 1, 1))


def _attn_decode(sinks_col, qmat, k_new, v_new, cache_k, cache_v, bt):
    nbatch = qmat.shape[0]
    b3 = lambda i: (i, 0, 0)
    b2 = lambda i: (i, 0)
    return pl.pallas_call(
        _attn_decode_kernel,
        grid=(nbatch // bt,),
        in_specs=[
            pl.BlockSpec((N_Q_HEADS, 1), lambda i: (0, 0)),
            pl.BlockSpec((bt, N_Q_HEADS, ATTN_KV), b3),
            pl.BlockSpec((bt, ATTN_KV), b2),
            pl.BlockSpec((bt, ATTN_KV), b2),
            pl.BlockSpec((bt, WINDOW, ATTN_KV), b3),
            pl.BlockSpec((bt, WINDOW, ATTN_KV), b3),
        ],
        out_specs=(
            pl.BlockSpec((bt, N_Q_HEADS, ATTN_KV), b3),
            pl.BlockSpec((bt, WINDOW, ATTN_KV), b3),
            pl.BlockSpec((bt, WINDOW, ATTN_KV), b3),
        ),
        out_shape=(
            jax.ShapeDtypeStruct((nbatch, N_Q_HEADS, ATTN_KV), F32),
            jax.ShapeDtypeStruct((nbatch, WINDOW, ATTN_KV), F32),
            jax.ShapeDtypeStruct((nbatch, WINDOW, ATTN_KV), F32),
        ),
        compiler_params=pltpu.CompilerParams(dimension_semantics=("parallel",)),
        name="attn_decode",
    )(sinks_col, qmat, k_new, v_new, cache_k, cache_v)


def _mlstm_decode_kernel(q_ref, k_ref, v_ref, og_ref, gif_ref, c_ref, n_ref, m_ref, h_ref, co_ref, no_ref, mo_ref):
    bt = q_ref.shape[0]
    gif = gif_ref[...]
    ig = gif[:, 0:ML_HEADS]
    lf = gif[:, ML_HEADS:2 * ML_HEADS]
    m_prev = m_ref[...]
    a = lf + m_prev
    m_t = jnp.maximum(a, ig)
    aw = jnp.exp(a - m_t)
    e_i = jnp.exp(ig - m_t)
    floor = jnp.exp(-m_t)
    m_new = jnp.maximum(a, ig)
    decay = jnp.exp(a - m_new)
    w = jnp.exp(ig - m_new)
    mo_ref[...] = m_new
    pad = jnp.zeros((LANES - bt, ML_DIM), F32)
    lane = lax.broadcasted_iota(jnp.int32, (ML_DIM, LANES), 1)
    for h in range(ML_HEADS):
        hs = slice(h * ML_DIM, (h + 1) * ML_DIM)
        q = q_ref[:, hs]
        k = k_ref[:, hs]
        v = v_ref[:, hs]
        n = n_ref[:, h, :]
        qk = jnp.sum(q * k, axis=1, keepdims=True)
        qn = jnp.sum(q * n, axis=1, keepdims=True)
        sw = qk * e_i[:, h:h + 1]
        den = aw[:, h:h + 1] * qn + sw
        inv = 1.0 / jnp.maximum(jnp.abs(den), floor[:, h:h + 1])
        no_ref[:, h, :] = decay[:, h:h + 1] * n + w[:, h:h + 1] * k
        q_b = q.astype(BF16)
        wk_t = jnp.concatenate([w[:, h:h + 1] * k, pad], axis=0).T.astype(BF16)
        v_rows = jnp.concatenate([v, pad], axis=0).astype(BF16)
        nums = []
        for b in range(bt):
            C = c_ref[b, h]
            q_c = _mm(q_b, C.astype(BF16))[b:b + 1, :]
            nums.append(aw[b:b + 1, h:h + 1] * q_c)
            outer = _mm(jnp.where(lane == b, wk_t, jnp.zeros_like(wk_t)), v_rows)
            co_ref[b, h] = decay[b:b + 1, h:h + 1] * C + outer
        num = jnp.concatenate(nums, axis=0) + sw * v
        h_ref[:, hs] = (num * inv * og_ref[:, hs]).astype(h_ref.dtype)


def _mlstm_decode(mq, mk, mv, og, gif, state_c, state_n, state_m, bt):
    nbatch = mq.shape[0]
    b2 = lambda i: (i, 0)
    b3 = lambda i: (i, 0, 0)
    b4 = lambda i: (i, 0, 0, 0)
    return pl.pallas_call(
        _mlstm_decode_kernel,
        grid=(nbatch // bt,),
        in_specs=[
            pl.BlockSpec((bt, ML_W), b2),
            pl.BlockSpec((bt, ML_W), b2),
            pl.BlockSpec((bt, ML_W), b2),
            pl.BlockSpec((bt, ML_W), b2),
            pl.BlockSpec((bt, LANES), b2),
            pl.BlockSpec((bt, ML_HEADS, ML_DIM, ML_DIM), b4),
            pl.BlockSpec((bt, ML_HEADS, ML_DIM), b3),
            pl.BlockSpec((bt, ML_HEADS), b2),
        ],
        out_specs=(
            pl.BlockSpec((bt, ML_W), b2),
            pl.BlockSpec((bt, ML_HEADS, ML_DIM, ML_DIM), b4),
            pl.BlockSpec((bt, ML_HEADS, ML_DIM), b3),
            pl.BlockSpec((bt, ML_HEADS), b2),
        ),
        out_shape=(
            jax.ShapeDtypeStruct((nbatch, ML_W), BF16),
            jax.ShapeDtypeStruct((nbatch, ML_HEADS, ML_DIM, ML_DIM), F32),
            jax.ShapeDtypeStruct((nbatch, ML_HEADS, ML_DIM), F32),
            jax.ShapeDtypeStruct((nbatch, ML_HEADS), F32),
        ),
        compiler_params=pltpu.CompilerParams(dimension_semantics=("parallel",)),
        name="mlstm_decode",
    )(mq, mk, mv, og, gif, state_c, state_n, state_m)


def _rope_tables(pos):
    half = HEAD_DIM // 2
    inv = 1.0 / (ROPE_THETA ** (jnp.arange(half, dtype=F32) / half))
    ang = pos.astype(F32)[:, None] * inv[None, :]
    cos = jnp.cos(ang)
    sin = jnp.sin(ang)
    cos_t = jnp.concatenate([cos, cos, cos, cos], axis=1)
    sin_t = jnp.concatenate([-sin, sin, -sin, sin], axis=1)
    return cos_t, sin_t


def kernel(x_prompt, x_sample, cache_k, cache_v, state_C, state_n, state_m, p_prompt, p_sample, norm_mix, w_in, b_in, attn_sinks, w_attn_up, w_ml_up, w_o, norm_mlp, w_ff1, w_ff2, norm_ple, w_ple_gate, w_ple_proj, norm_final):
    assert w_in.shape[0] == 1, "single layer"
    batch, seq, _ = x_prompt.shape
    nbatch = x_sample.shape[0]
    assert x_sample.shape[1] == 1

    w_in0 = w_in[0]
    b_in0 = b_in[0]
    w_proj, w_gate, w_au, w_mu, w_out, w_f1, w_f2, w_pg, w_pp = _weight_cast(
        w_in0.T, (w_attn_up[0], w_ml_up[0], w_o[0], w_ff1[0], w_ff2[0], w_ple_gate[0], w_ple_proj[0]))
    b_proj = b_in0[:N_PROJ][None, :]
    b_gate = b_in0[OFF_GA:][None, :]
    g_mix = norm_mix[0][None, :]
    out_weights = (
        g_mix, w_gate, b_gate, w_au, w_mu, w_out,
        norm_mlp[0][None, :], w_f1, w_f2,
        norm_ple[0][None, :], w_pg, w_pp, norm_final[None, :],
    )
    sinks = attn_sinks[0]

    cos_p, sin_p = _rope_tables(jnp.arange(seq, dtype=jnp.int32))
    xp = x_prompt.reshape(batch * seq, D_MODEL)
    qt, k, vt, k_last, v_last, mqt, mk, mvt, og, gif, gt = _in_proj(xp, g_mix, w_proj, b_proj, cos_p, sin_p, tm=512,
                                                                    m_dtype=BF16, feature_major=True)
    a_out = _attn_prompt(sinks, qt, k, vt, batch, seq, tq=512)
    h_m, cst, nst, mst = _mlstm_prompt(mqt, mk, mvt, og, gif, gt, batch, seq, bb=2)
    y_prompt = _out_proj(xp, a_out, h_m, p_prompt[0].reshape(batch * seq, D_PLE), out_weights, tm=256)
    y_prompt = y_prompt.reshape(batch, seq, D_MODEL)
    k_prompt = k_last.reshape(1, batch, WINDOW, N_KV_HEADS, HEAD_DIM)
    v_prompt = v_last.reshape(1, batch, WINDOW, N_KV_HEADS, HEAD_DIM)
    c_prompt = cst[None]
    n_prompt = nst[:, :ML_HEADS, :][None]
    m_prompt = mst[:, :ML_HEADS, 0][None]

    cos_s, sin_s = _rope_tables(jnp.full((nbatch,), PAST_LEN, dtype=jnp.int32))
    xs = x_sample.reshape(nbatch, D_MODEL)
    qs, _, _, ks, vs, mqs, mks, mvs, ogs, gifs, _ = _in_proj(xs, g_mix, w_proj, b_proj, cos_s, sin_s, tm=nbatch,
                                                          m_dtype=F32, feature_major=False)
    q4 = qs.reshape(nbatch, N_KV_HEADS, GQA_GROUP, 1, HEAD_DIM)
    sel = jnp.eye(N_KV_HEADS, dtype=BF16)[None, :, None, :, None]
    qmat = (q4 * sel).reshape(nbatch, N_Q_HEADS, ATTN_KV)
    to_fp = lambda c: c[0].transpose(0, 2, 3, 1).reshape(nbatch, ATTN_KV, WINDOW)
    from_fp = lambda c: c.reshape(nbatch, N_KV_HEADS, HEAD_DIM, WINDOW).transpose(0, 3, 1, 2)[None]
    o_full, k_s, v_s = _attn_decode(sinks[:, None], qmat, ks, vs, to_fp(cache_k), to_fp(cache_v), bt=8)
    o5 = o_full.reshape(nbatch, N_KV_HEADS, GQA_GROUP, N_KV_HEADS, HEAD_DIM)
    a_s = jnp.stack([o5[:, kv, :, kv, :] for kv in range(N_KV_HEADS)], axis=1).reshape(nbatch, ATTN_Q).astype(BF16)
    h_s, c_s, n_s, m_s = _mlstm_decode(mqs, mks, mvs, ogs, gifs, state_C[0], state_n[0], state_m[0], bt=8)
    y_sample = _out_proj(xs, a_s, h_s, p_sample[0].reshape(nbatch, D_PLE), out_weights, tm=nbatch)
    y_sample = y_sample.reshape(nbatch, 1, D_MODEL)
    k_sample = from_fp(k_s)
    v_sample = from_fp(v_s)

    return (y_prompt, y_sample, k_prompt, v_prompt, c_prompt, n_prompt, m_prompt,
            k_sample, v_sample, c_s[None], n_s[None], m_s[None])
```

```python
import functools

import numpy as np
import jax
import jax.numpy as jnp
from jax import lax
from jax.experimental import pallas as pl
from jax.experimental.pallas import tpu as pltpu

F32 = jnp.float32
BF16 = jnp.bfloat16

D_MODEL = 1024
HEAD_DIM = 64
N_Q_HEADS = 8
N_KV_HEADS = 2
GQA_GROUP = N_Q_HEADS // N_KV_HEADS
WINDOW = 128
ROPE_THETA = 10000.0
ML_HEADS = 4
ML_DIM = 128
ML_CHUNK = 128
D_FF = 4 * D_MODEL
D_PLE = 256
EPS = 1e-6
PAST_LEN = 16384

ATTN_Q = N_Q_HEADS * HEAD_DIM
ATTN_KV = N_KV_HEADS * HEAD_DIM
ML_W = ML_HEADS * ML_DIM
LANES = 128

OFF_AQ = 0
OFF_AK = OFF_AQ + ATTN_Q
OFF_AV = OFF_AK + ATTN_KV
OFF_MQ = OFF_AV + ATTN_KV
OFF_MK = OFF_MQ + ML_W
OFF_MV = OFF_MK + ML_W
OFF_MO = OFF_MV + ML_W
OFF_MI = OFF_MO + ML_W
OFF_MF = OFF_MI + ML_HEADS
OFF_GA = OFF_MF + ML_HEADS
OFF_GM = OFF_GA + D_MODEL
D_IN = OFF_GM + D_MODEL
N_PROJ = OFF_MI + LANES

VMEM_LIMIT = 60 * 1024 * 1024

def _rms(x, g):
    r = lax.rsqrt(jnp.mean(x * x, axis=-1, keepdims=True) + EPS)
    return (x * r) * g


def _mm(a, b):
    return jnp.dot(a, b, preferred_element_type=F32)


def _mm_nt(a, b):
    return lax.dot_general(a, b, (((1,), (1,)), ((), ())), preferred_element_type=F32)


def _mm_tn(a, b):
    return lax.dot_general(a, b, (((0,), (0,)), ((), ())), preferred_element_type=F32)


def _log_sigmoid(x):
    return jnp.minimum(x, 0.0) - jnp.log1p(jnp.exp(-jnp.abs(x)))


WEIGHT_ROW_SPLIT = 8


def _weight_cast_kernel(*refs):
    n_plain = (len(refs) - 3) // 2
    w_in_t_ref, plain_in = refs[0], refs[1:1 + n_plain]
    w_proj_ref, w_gate_ref, plain_out = refs[1 + n_plain], refs[2 + n_plain], refs[3 + n_plain:]
    for g in range(N_PROJ // LANES):
        w_proj_ref[:, g * LANES:(g + 1) * LANES] = w_in_t_ref[g * LANES:(g + 1) * LANES, :].T.astype(BF16)
    for g in range(2 * D_MODEL // LANES):
        w_gate_ref[:, g * LANES:(g + 1) * LANES] = w_in_t_ref[OFF_GA + g * LANES:OFF_GA + (g + 1) * LANES, :].T.astype(BF16)
    for src, dst in zip(plain_in, plain_out):
        dst[...] = src[...].astype(BF16)


def _weight_cast(w_in_t, plain):
    split = WEIGHT_ROW_SPLIT
    assert D_MODEL // split == LANES
    rows_blk = lambda w: pl.BlockSpec((w.shape[0] // split, w.shape[1]), lambda i: (i, 0))
    out_shapes = (jax.ShapeDtypeStruct((D_MODEL, N_PROJ), BF16), jax.ShapeDtypeStruct((D_MODEL, 2 * D_MODEL), BF16),
                  *(jax.ShapeDtypeStruct(w.shape, BF16) for w in plain))
    return pl.pallas_call(
        _weight_cast_kernel,
        grid=(split,),
        in_specs=[pl.BlockSpec((D_IN, LANES), lambda i: (0, i))] + [rows_blk(w) for w in plain],
        out_specs=tuple(rows_blk(s) for s in out_shapes),
        out_shape=out_shapes,
        compiler_params=pltpu.CompilerParams(dimension_semantics=("parallel",), vmem_limit_bytes=VMEM_LIMIT),
        name="weight_cast",
    )(w_in_t, *plain)


def _in_proj_kernel(x_ref, g_ref, w_ref, b_ref, cos_ref, sin_ref,
                    q_ref, k_ref, v_ref, kl_ref, vl_ref, mq_ref, mk_ref, mv_ref, og_ref, gif_ref, gt_ref,
                    *, feature_major):
    xn = _rms(x_ref[...], g_ref[...]).astype(BF16)

    def proj(lo, hi):
        return _mm(xn, w_ref[:, lo:hi]) + b_ref[:, lo:hi]

    cos = cos_ref[...]
    sin = sin_ref[...]
    lane = lax.broadcasted_iota(jnp.int32, (1, LANES), 1)
    first_half = (lane % HEAD_DIM) < (HEAD_DIM // 2)

    def rope(z):
        partner = jnp.where(first_half, pltpu.roll(z, LANES - HEAD_DIM // 2, 1), pltpu.roll(z, HEAD_DIM // 2, 1))
        return z * cos + partner * sin

    zq = proj(OFF_AQ, OFF_AK)
    for c in range(ATTN_Q // LANES):
        sl = slice(c * LANES, (c + 1) * LANES)
        qc = rope(zq[:, sl]) * (HEAD_DIM ** -0.5)
        if feature_major:
            q_ref[sl, :] = qc.T.astype(q_ref.dtype)
        else:
            q_ref[:, sl] = qc.astype(q_ref.dtype)
    zkv = proj(OFF_AK, OFF_MQ)
    k = rope(zkv[:, :ATTN_KV])
    v = zkv[:, ATTN_KV:]
    k_ref[...] = k.astype(k_ref.dtype)
    v_ref[...] = (v.T if feature_major else v).astype(v_ref.dtype)
    tm = k.shape[0]
    kl_ref[...] = k[tm - WINDOW:, :]
    vl_ref[...] = v[tm - WINDOW:, :]
    zmq = proj(OFF_MQ, OFF_MK)
    zmv = proj(OFF_MV, OFF_MO)
    if feature_major:
        for h in range(ML_HEADS):
            hs = slice(h * ML_DIM, (h + 1) * ML_DIM)
            mq_ref[0, hs, :] = zmq[:, hs].T.astype(mq_ref.dtype)
            mv_ref[0, hs, :] = zmv[:, hs].T.astype(mv_ref.dtype)
    else:
        mq_ref[...] = zmq.astype(mq_ref.dtype)
        mv_ref[...] = zmv.astype(mv_ref.dtype)
    mk_ref[...] = (proj(OFF_MK, OFF_MV) * (ML_DIM ** -0.5)).astype(mk_ref.dtype)
    og_ref[...] = jax.nn.sigmoid(proj(OFF_MO, OFF_MI))
    zg = proj(OFF_MI, N_PROJ)
    gif = jnp.where(lane < ML_HEADS, zg, jnp.where(lane < 2 * ML_HEADS, _log_sigmoid(zg), 0.0))
    if feature_major:
        L = ML_CHUNK
        tri = (lax.broadcasted_iota(jnp.int32, (L, L), 1) <= lax.broadcasted_iota(jnp.int32, (L, L), 0)).astype(F32)
        for c in range(tm // L):
            cs = slice(c * L, (c + 1) * L)
            G = gif[cs, :]
            cum = jnp.dot(tri, G, precision=lax.Precision.HIGHEST, preferred_element_type=F32)
            A = jnp.where(lane < ML_HEADS, G, cum)
            diff = pltpu.roll(pltpu.roll(A, ML_HEADS, 1) - A, ML_HEADS, 1)
            A = jnp.where(lane < 2 * ML_HEADS, A, jnp.where(lane < 3 * ML_HEADS, diff, 0.0))
            gif_ref[cs, :] = A
            gt_ref[0, :, cs] = A.T[0:8, :]
    else:
        gif_ref[...] = gif
        gt_ref[0] = jnp.zeros(gt_ref.shape[1:], F32)


def _in_proj(x, g, w, b, cos, sin, tm, m_dtype, feature_major):
    rows = x.shape[0]
    seq = cos.shape[0]
    tps = seq // tm
    nseq = rows // seq
    row = lambda i: (i, 0)
    const = lambda i: (0, 0)
    pos = lambda i: (i % tps, 0)
    last = lambda i: (i // tps, 0)
    row_out = lambda n, dt: (jax.ShapeDtypeStruct((rows, n), dt), pl.BlockSpec((tm, n), row))
    col_out = lambda n, dt: (jax.ShapeDtypeStruct((n, rows), dt), pl.BlockSpec((n, tm), lambda i: (0, i)))
    att_out = col_out if feature_major else row_out
    seq_out = lambda n, dt: (jax.ShapeDtypeStruct((nseq, n, seq), dt),
                             pl.BlockSpec((1, n, tm), lambda i: (i // tps, 0, i % tps)))
    ml_out = seq_out if feature_major else row_out
    outs = (
        att_out(ATTN_Q, BF16),
        row_out(ATTN_KV, BF16),
        att_out(ATTN_KV, BF16),
        (jax.ShapeDtypeStruct((nseq * WINDOW, ATTN_KV), F32), pl.BlockSpec((WINDOW, ATTN_KV), last)),
        (jax.ShapeDtypeStruct((nseq * WINDOW, ATTN_KV), F32), pl.BlockSpec((WINDOW, ATTN_KV), last)),
        ml_out(ML_W, m_dtype),
        row_out(ML_W, m_dtype),
        ml_out(ML_W, m_dtype),
        row_out(ML_W, F32),
        row_out(LANES, F32),
        seq_out(8, F32),
    )
    return pl.pallas_call(
        functools.partial(_in_proj_kernel, feature_major=feature_major),
        grid=(rows // tm,),
        in_specs=[
            pl.BlockSpec((tm, D_MODEL), row),
            pl.BlockSpec((1, D_MODEL), const),
            pl.BlockSpec((D_MODEL, N_PROJ), const),
            pl.BlockSpec((1, N_PROJ), const),
            pl.BlockSpec((tm, LANES), pos),
            pl.BlockSpec((tm, LANES), pos),
        ],
        out_specs=tuple(o[1] for o in outs),
        out_shape=tuple(o[0] for o in outs),
        compiler_params=pltpu.CompilerParams(dimension_semantics=("arbitrary",), vmem_limit_bytes=VMEM_LIMIT),
        name="in_proj",
    )(x, g, w, b, cos, sin)


def _attn_stages(sink_ref, qt_ref, kp_ref, kc_ref, vtp_ref, vtc_ref, first_block, emit):
    L = WINDOW
    nsub = kc_ref.shape[0] // L
    key = lax.broadcasted_iota(jnp.int32, (2 * L, L), 0)
    t = lax.broadcasted_iota(jnp.int32, (2 * L, L), 1)
    is_prev = key < L
    allowed = (is_prev & (key >= t)) | (~is_prev & (key - L <= t))
    bias = jnp.where(allowed, 0.0, -jnp.inf)
    bias_first = jnp.where(is_prev & first_block, -jnp.inf, bias)
    zeros_q = jnp.zeros((HEAD_DIM, L), BF16)
    heads = range(N_Q_HEADS)
    for r in range(nsub):
        ts = slice(r * L, (r + 1) * L)
        k_prev = kp_ref[...] if r == 0 else kc_ref[(r - 1) * L:r * L, :]
        vt_prev = vtp_ref[...] if r == 0 else vtc_ref[:, (r - 1) * L:r * L]
        k2 = jnp.concatenate([k_prev, kc_ref[ts, :]], axis=0)
        vt2 = jnp.concatenate([vt_prev, vtc_ref[:, ts]], axis=1)
        b_r = bias_first if r == 0 else bias
        q_op = [jnp.concatenate([qt_ref[hd * HEAD_DIM:(hd + 1) * HEAD_DIM, ts], zeros_q][::1 if hd < GQA_GROUP else -1],
                                axis=0) for hd in heads]
        s = [_mm(k2, q_op[hd]) + b_r for hd in heads]
        yield
        m = [jnp.maximum(jnp.max(s[hd], axis=0, keepdims=True), sink_ref[hd]) for hd in heads]
        yield
        p = [jnp.exp(s[hd] - m[hd]) for hd in heads]
        yield
        den = [jnp.sum(p[hd], axis=0, keepdims=True) + jnp.exp(sink_ref[hd] - m[hd]) for hd in heads]
        o = [_mm(vt2[(hd // GQA_GROUP) * HEAD_DIM:(hd // GQA_GROUP + 1) * HEAD_DIM, :], p[hd].astype(BF16)) / den[hd]
             for hd in heads]
        yield
        emit(r, jnp.concatenate(o, axis=0).T)
        yield


ML_STATE_ROWS = ML_DIM + 8


def _mlstm_stages(chains, emit):
    L = ML_CHUNK
    s_idx = lax.broadcasted_iota(jnp.int32, (L, L), 0)
    t_idx = lax.broadcasted_iota(jnp.int32, (L, L), 1)
    causal = s_idx <= t_idx
    n = range(len(chains))
    dlog = [jnp.where(causal, ch["b_r"] + ch["u_c"], -jnp.inf) for ch in chains]
    yield
    qk = [_mm(ch["k"], ch["qt"]) for ch in chains]
    inter = [_mm(ch["state"].astype(BF16), ch["qt"]) for ch in chains]
    yield
    a = [ch["b_r"] + ch["m_prev"] for ch in chains]
    m_t = [jnp.maximum(a[i], jnp.max(dlog[i], axis=0, keepdims=True)) for i in n]
    yield
    sw = [qk[i] * jnp.exp(dlog[i] - m_t[i]) for i in n]
    yield
    sv = [_mm(chains[i]["vt"], sw[i].astype(BF16)) for i in n]
    yield
    hs = []
    for i in n:
        aw = jnp.exp(a[i] - m_t[i])
        num = aw * inter[i][:ML_DIM] + sv[i]
        den = aw * inter[i][ML_DIM:ML_DIM + 1] + jnp.sum(sw[i], axis=0, keepdims=True)
        hh = num / jnp.maximum(jnp.abs(den), jnp.exp(-m_t[i]))
        hs.append(hh.T * chains[i]["og"])
    yield
    pad = jnp.zeros((ML_STATE_ROWS - ML_DIM - 1, L), F32)
    new_state, new_m = [], []
    for i in n:
        ch = chains[i]
        b_end = ch["b_r"][:, L - 1:L]
        g = b_end - ch["b_r"] + ch["ig_r"]
        m_new = jnp.maximum(b_end + ch["m_prev"], jnp.max(g, axis=1, keepdims=True))
        decay = jnp.exp(b_end + ch["m_prev"] - m_new)
        w = jnp.exp(g - m_new)
        v_aug = jnp.concatenate([ch["vt"].astype(F32) * w, w, pad], axis=0).astype(BF16)
        new_state.append(decay * ch["state"] + _mm(v_aug, ch["k"]))
        new_m.append(m_new)
    emit(hs, new_state, new_m)
    yield


DENSE_COLS = 512


def _dense_stages(x, a, hm, p, gmix_ref, wg_ref, bg_ref, wau_ref, wmu_ref, wo_ref,
                  gmlp_ref, wff1_ref, wff2_ref, gple_ref, wpg_ref, wpp_ref, gfin_ref, emit):
    nc = DENSE_COLS
    pieces = lambda n: [slice(c, c + nc) for c in range(0, n, nc)]
    xn = _rms(x, gmix_ref[...]).astype(BF16)
    gates = []
    for cs in pieces(2 * D_MODEL):
        gates.append(jax.nn.sigmoid(_mm(xn, wg_ref[:, cs]) + bg_ref[:, cs]))
        yield
    half = D_MODEL // nc
    mix = []
    for j, cs in enumerate(pieces(D_MODEL)):
        mix.append((gates[j] * _mm(a, wau_ref[:, cs]) + gates[half + j] * _mm(hm, wmu_ref[:, cs])).astype(BF16))
        yield
    mix = jnp.concatenate(mix, axis=1)
    h = []
    for cs in pieces(D_MODEL):
        h.append(x[:, cs] + _mm(mix, wo_ref[:, cs]))
        yield
    h = jnp.concatenate(h, axis=1)
    hn = _rms(h, gmlp_ref[...]).astype(BF16)
    u = []
    for cs in pieces(D_FF):
        u.append(jnp.square(jnp.maximum(_mm(hn, wff1_ref[:, cs]), 0.0)).astype(BF16))
        yield
    u = jnp.concatenate(u, axis=1)
    h2 = []
    for cs in pieces(D_MODEL):
        h2.append(h[:, cs] + _mm(u, wff2_ref[:, cs]))
        yield
    h = jnp.concatenate(h2, axis=1)
    hn = _rms(h, gple_ref[...]).astype(BF16)
    pb = p.astype(BF16)
    h3 = []
    for cs in pieces(D_MODEL):
        h3.append(h[:, cs] + _mm(pb, wpp_ref[:, cs]) * jax.nn.sigmoid(_mm(hn, wpg_ref[:, cs])))
        yield
    emit(_rms(jnp.concatenate(h3, axis=1), gfin_ref[...]))
    yield


def _weave(*streams):
    live = list(streams)
    while live:
        for g in list(live):
            try:
                next(g)
            except StopIteration:
                live.remove(g)


def _out_kernel(x_ref, a_ref, hm_ref, p_ref, *rest):
    weights, y_ref = rest[:-1], rest[-1]

    def emit(y):
        y_ref[...] = y

    _weave(_dense_stages(x_ref[...], a_ref[...], hm_ref[...], p_ref[...], *weights, emit=emit))


def _out_proj(x, a, hm, p, weights, tm):
    rows = x.shape[0]
    row = lambda i: (i, 0)
    const = lambda i: (0, 0)
    w_specs = [pl.BlockSpec(w.shape, const, pipeline_mode=pl.Buffered(1)) for w in weights]
    return pl.pallas_call(
        _out_kernel,
        grid=(rows // tm,),
        in_specs=[
            pl.BlockSpec((tm, D_MODEL), row),
            pl.BlockSpec((tm, ATTN_Q), row),
            pl.BlockSpec((tm, ML_W), row),
            pl.BlockSpec((tm, D_PLE), row),
        ] + w_specs,
        out_specs=pl.BlockSpec((tm, D_MODEL), row),
        out_shape=jax.ShapeDtypeStruct((rows, D_MODEL), F32),
        compiler_params=pltpu.CompilerParams(dimension_semantics=("parallel",), vmem_limit_bytes=VMEM_LIMIT),
        name="out_proj",
    )(x, a, hm, p, *weights)


N_FUSED_WEIGHTS = 13


def _prompt_kernel(sink_ref, x_ref, p_ref, qt_ref, kp_ref, kc_ref, vtp_ref, vtc_ref,
                   mqt_ref, mk_ref, mvt_ref, og_ref, gif_ref, gt_ref, *rest, tiles_per_seq):
    weights = rest[:N_FUSED_WEIGHTS]
    y_ref, cst_ref, nst_ref, mst_ref, a_scr, h_scr, c_scr, m_scr = rest[N_FUSED_WEIGHTS:]
    i = pl.program_id(0)
    n_tiles = pl.num_programs(0) - 1
    tile_in_seq = jnp.minimum(i, n_tiles - 1) % tiles_per_seq
    slot = i % 2
    L = ML_CHUNK
    tm = kc_ref.shape[0]

    @pl.when(i == 0)
    def _():
        a_scr[...] = jnp.zeros_like(a_scr)
        h_scr[...] = jnp.zeros_like(h_scr)

    @pl.when(tile_in_seq == 0)
    def _():
        c_scr[...] = jnp.zeros_like(c_scr)
        m_scr[...] = jnp.zeros_like(m_scr)

    def branches():
        def emit_attn(r, blk):
            a_scr[slot, r * WINDOW:(r + 1) * WINDOW, :] = blk.astype(a_scr.dtype)

        yield from _attn_stages(sink_ref, qt_ref, kp_ref, kc_ref, vtp_ref, vtc_ref, tile_in_seq == 0, emit_attn)
        hsl = lambda h: slice(h * ML_DIM, (h + 1) * ML_DIM)
        carry = dict(state=[c_scr[h] for h in range(ML_HEADS)], m=[m_scr[h:h + 1, :] for h in range(ML_HEADS)])
        for c in range(tm // L):
            cs = slice(c * L, (c + 1) * L)

            def emit_ml(hs, new_state, new_m, cs=cs):
                for h in range(ML_HEADS):
                    h_scr[slot, cs, hsl(h)] = hs[h].astype(h_scr.dtype)
                carry["state"], carry["m"] = new_state, new_m

            chains = [dict(qt=mqt_ref[0, hsl(h), cs], k=mk_ref[cs, hsl(h)], vt=mvt_ref[0, hsl(h), cs],
                           og=og_ref[cs, hsl(h)], ig_r=gt_ref[0, h:h + 1, cs],
                           b_r=gt_ref[0, ML_HEADS + h:ML_HEADS + h + 1, cs],
                           u_c=gif_ref[cs, 2 * ML_HEADS + h:2 * ML_HEADS + h + 1],
                           state=carry["state"][h], m_prev=carry["m"][h]) for h in range(ML_HEADS)]
            yield from _mlstm_stages(chains, emit_ml)
        for h in range(ML_HEADS):
            c_scr[h] = carry["state"][h]
            m_scr[h:h + 1, :] = carry["m"][h]

    def emit_y(y):
        y_ref[...] = y

    _weave(branches(), _dense_stages(x_ref[...], a_scr[1 - slot], h_scr[1 - slot], p_ref[...], *weights, emit=emit_y))

    @pl.when((tile_in_seq == tiles_per_seq - 1) & (i < n_tiles))
    def _():
        nst_ref[...] = jnp.zeros_like(nst_ref)
        for h in range(ML_HEADS):
            final = c_scr[h]
            cst_ref[0, h] = final[:ML_DIM].T
            nst_ref[0, h:h + 1, :] = final[ML_DIM:ML_DIM + 1]
        mst_ref[0] = m_scr[...]


def _prompt_fused(sinks, x, p, qt, k, vt, mqt, mk, mvt, og, gif, gt, weights, batch, seq, tm):
    assert len(weights) == N_FUSED_WEIGHTS
    rows = batch * seq
    n_tiles = rows // tm
    tps = seq // tm
    sub = tm // WINDOW
    ta = lambda i: jnp.minimum(i, n_tiles - 1)
    tb = lambda i: jnp.maximum(i - 1, 0)
    prev_blk = lambda i: jnp.maximum(ta(i) * sub - 1, 0)
    const = lambda i: (0, 0)
    seq_blk = lambda i: (ta(i) // tps, 0, ta(i) % tps)
    per_seq = lambda i: (ta(i) // tps, 0, 0)
    w_specs = [pl.BlockSpec(w.shape, const, pipeline_mode=pl.Buffered(1)) for w in weights]
    return pl.pallas_call(
        functools.partial(_prompt_kernel, tiles_per_seq=tps),
        grid=(n_tiles + 1,),
        in_specs=[
            pl.BlockSpec(memory_space=pltpu.SMEM),
            pl.BlockSpec((tm, D_MODEL), lambda i: (tb(i), 0)),
            pl.BlockSpec((tm, D_PLE), lambda i: (tb(i), 0)),
            pl.BlockSpec((ATTN_Q, tm), lambda i: (0, ta(i))),
            pl.BlockSpec((WINDOW, ATTN_KV), lambda i: (prev_blk(i), 0)),
            pl.BlockSpec((tm, ATTN_KV), lambda i: (ta(i), 0)),
            pl.BlockSpec((ATTN_KV, WINDOW), lambda i: (0, prev_blk(i))),
            pl.BlockSpec((ATTN_KV, tm), lambda i: (0, ta(i))),
            pl.BlockSpec((1, ML_W, tm), seq_blk),
            pl.BlockSpec((tm, ML_W), lambda i: (ta(i), 0)),
            pl.BlockSpec((1, ML_W, tm), seq_blk),
            pl.BlockSpec((tm, ML_W), lambda i: (ta(i), 0)),
            pl.BlockSpec((tm, LANES), lambda i: (ta(i), 0)),
            pl.BlockSpec((1, 8, tm), seq_blk),
        ] + w_specs,
        out_specs=(
            pl.BlockSpec((tm, D_MODEL), lambda i: (tb(i), 0)),
            pl.BlockSpec((1, ML_HEADS, ML_DIM, ML_DIM), lambda i: (ta(i) // tps, 0, 0, 0)),
            pl.BlockSpec((1, 8, LANES), per_seq),
            pl.BlockSpec((1, 8, LANES), per_seq),
        ),
        out_shape=(
            jax.ShapeDtypeStruct((rows, D_MODEL), F32),
            jax.ShapeDtypeStruct((batch, ML_HEADS, ML_DIM, ML_DIM), F32),
            jax.ShapeDtypeStruct((batch, 8, LANES), F32),
            jax.ShapeDtypeStruct((batch, 8, LANES), F32),
        ),
        scratch_shapes=[pltpu.VMEM((2, tm, ATTN_Q), BF16), pltpu.VMEM((2, tm, ML_W), BF16),
                        pltpu.VMEM((ML_HEADS, ML_STATE_ROWS, ML_DIM), F32), pltpu.VMEM((8, LANES), F32)],
        compiler_params=pltpu.CompilerParams(dimension_semantics=("arbitrary",), vmem_limit_bytes=VMEM_LIMIT),
        name="prompt_fused",
    )(sinks, x, p, qt, k, k, vt, vt, mqt, mk, mvt, og, gif, gt, *weights)


def _attn_decode_kernel(sink_ref, qm_ref, kn_ref, vn_ref, ck_ref, cv_ref, o_ref, ko_ref, vo_ref):
    bt = qm_ref.shape[0]
    qm = qm_ref[...]
    kn = kn_ref[...]
    vn = vn_ref[...]
    ck = ck_ref[...]
    cv = cv_ref[...]
    sink = sink_ref[...]
    s = jnp.einsum('bhc,bck->bhk', qm, ck.astype(BF16), preferred_element_type=F32)
    s_new = jnp.sum(qm.astype(F32) * kn[:, None, :], axis=2, keepdims=True)
    m = jnp.maximum(jnp.maximum(jnp.max(s, axis=2, keepdims=True), s_new), sink)
    p = jnp.exp(s - m)
    p_new = jnp.exp(s_new - m)
    den = jnp.sum(p, axis=2, keepdims=True) + p_new + jnp.exp(sink - m)
    o = jnp.einsum('bhk,bck->bhc', p.astype(BF16), cv.astype(BF16), preferred_element_type=F32)
    o_ref[...] = (o + p_new * vn[:, None, :]) / den
    pad = jnp.zeros((LANES - bt, ATTN_KV), F32)
    kn_t = jnp.concatenate([kn, pad], axis=0).T
    vn_t = jnp.concatenate([vn, pad], axis=0).T
    newest = lax.broadcasted_iota(jnp.int32, (ATTN_KV, WINDOW), 1) == WINDOW - 1
    for b in range(bt):
        ko_ref[b] = jnp.where(newest, kn_t[:, b:b + 1], pltpu.roll(ck[b], WINDOW - 1, 1))
        vo_ref[b] = jnp.where(newest, vn_t[:, b:b + 1], pltpu.roll(cv[b], WINDOW - 1, 1))


def _attn_decode(sinks_col, qmat, k_new, v_new, cache_k, cache_v, bt):
    nbatch = qmat.shape[0]
    b3 = lambda i: (i, 0, 0)
    b2 = lambda i: (i, 0)
    return pl.pallas_call(
        _attn_decode_kernel,
        grid=(nbatch // bt,),
        in_specs=[
            pl.BlockSpec((N_Q_HEADS, 1), lambda i: (0, 0)),
            pl.BlockSpec((bt, N_Q_HEADS, ATTN_KV), b3),
            pl.BlockSpec((bt, ATTN_KV), b2),
            pl.BlockSpec((bt, ATTN_KV), b2),
            pl.BlockSpec((bt, WINDOW, ATTN_KV), b3),
            pl.BlockSpec((bt, WINDOW, ATTN_KV), b3),
        ],
        out_specs=(
            pl.BlockSpec((bt, N_Q_HEADS, ATTN_KV), b3),
            pl.BlockSpec((bt, WINDOW, ATTN_KV), b3),
            pl.BlockSpec((bt, WINDOW, ATTN_KV), b3),
        ),
        out_shape=(
            jax.ShapeDtypeStruct((nbatch, N_Q_HEADS, ATTN_KV), F32),
            jax.ShapeDtypeStruct((nbatch, WINDOW, ATTN_KV), F32),
            jax.ShapeDtypeStruct((nbatch, WINDOW, ATTN_KV), F32),
        ),
        compiler_params=pltpu.CompilerParams(dimension_semantics=("parallel",)),
        name="attn_decode",
    )(sinks_col, qmat, k_new, v_new, cache_k, cache_v)


def _mlstm_decode_kernel(q_ref, k_ref, v_ref, og_ref, gif_ref, c_ref, n_ref, m_ref, h_ref, co_ref, no_ref, mo_ref):
    bt = q_ref.shape[0]
    gif = gif_ref[...]
    ig = gif[:, 0:ML_HEADS]
    lf = gif[:, ML_HEADS:2 * ML_HEADS]
    m_prev = m_ref[...]
    a = lf + m_prev
    m_t = jnp.maximum(a, ig)
    aw = jnp.exp(a - m_t)
    e_i = jnp.exp(ig - m_t)
    floor = jnp.exp(-m_t)
    m_new = jnp.maximum(a, ig)
    decay = jnp.exp(a - m_new)
    w = jnp.exp(ig - m_new)
    mo_ref[...] = m_new
    pad = jnp.zeros((LANES - bt, ML_DIM), F32)
    lane = lax.broadcasted_iota(jnp.int32, (ML_DIM, LANES), 1)
    for h in range(ML_HEADS):
        hs = slice(h * ML_DIM, (h + 1) * ML_DIM)
        q = q_ref[:, hs]
        k = k_ref[:, hs]
        v = v_ref[:, hs]
        n = n_ref[:, h, :]
        qk = jnp.sum(q * k, axis=1, keepdims=True)
        qn = jnp.sum(q * n, axis=1, keepdims=True)
        sw = qk * e_i[:, h:h + 1]
        den = aw[:, h:h + 1] * qn + sw
        inv = 1.0 / jnp.maximum(jnp.abs(den), floor[:, h:h + 1])
        no_ref[:, h, :] = decay[:, h:h + 1] * n + w[:, h:h + 1] * k
        q_b = q.astype(BF16)
        wk_t = jnp.concatenate([w[:, h:h + 1] * k, pad], axis=0).T.astype(BF16)
        v_rows = jnp.concatenate([v, pad], axis=0).astype(BF16)
        nums = []
        for b in range(bt):
            C = c_ref[b, h]
            q_c = _mm(q_b, C.astype(BF16))[b:b + 1, :]
            nums.append(aw[b:b + 1, h:h + 1] * q_c)
            outer = _mm(jnp.where(lane == b, wk_t, jnp.zeros_like(wk_t)), v_rows)
            co_ref[b, h] = decay[b:b + 1, h:h + 1] * C + outer
        num = jnp.concatenate(nums, axis=0) + sw * v
        h_ref[:, hs] = (num * inv * og_ref[:, hs]).astype(h_ref.dtype)


def _mlstm_decode(mq, mk, mv, og, gif, state_c, state_n, state_m, bt):
    nbatch = mq.shape[0]
    b2 = lambda i: (i, 0)
    b3 = lambda i: (i, 0, 0)
    b4 = lambda i: (i, 0, 0, 0)
    return pl.pallas_call(
        _mlstm_decode_kernel,
        grid=(nbatch // bt,),
        in_specs=[
            pl.BlockSpec((bt, ML_W), b2),
            pl.BlockSpec((bt, ML_W), b2),
            pl.BlockSpec((bt, ML_W), b2),
            pl.BlockSpec((bt, ML_W), b2),
            pl.BlockSpec((bt, LANES), b2),
            pl.BlockSpec((bt, ML_HEADS, ML_DIM, ML_DIM), b4),
            pl.BlockSpec((bt, ML_HEADS, ML_DIM), b3),
            pl.BlockSpec((bt, ML_HEADS), b2),
        ],
        out_specs=(
            pl.BlockSpec((bt, ML_W), b2),
            pl.BlockSpec((bt, ML_HEADS, ML_DIM, ML_DIM), b4),
            pl.BlockSpec((bt, ML_HEADS, ML_DIM), b3),
            pl.BlockSpec((bt, ML_HEADS), b2),
        ),
        out_shape=(
            jax.ShapeDtypeStruct((nbatch, ML_W), BF16),
            jax.ShapeDtypeStruct((nbatch, ML_HEADS, ML_DIM, ML_DIM), F32),
            jax.ShapeDtypeStruct((nbatch, ML_HEADS, ML_DIM), F32),
            jax.ShapeDtypeStruct((nbatch, ML_HEADS), F32),
        ),
        compiler_params=pltpu.CompilerParams(dimension_semantics=("parallel",)),
        name="mlstm_decode",
    )(mq, mk, mv, og, gif, state_c, state_n, state_m)


def _rope_tables(pos):
    half = HEAD_DIM // 2
    inv = 1.0 / (ROPE_THETA ** (jnp.arange(half, dtype=F32) / half))
    ang = pos.astype(F32)[:, None] * inv[None, :]
    cos = jnp.cos(ang)
    sin = jnp.sin(ang)
    cos_t = jnp.concatenate([cos, cos, cos, cos], axis=1)
    sin_t = jnp.concatenate([-sin, sin, -sin, sin], axis=1)
    return cos_t, sin_t


def kernel(x_prompt, x_sample, cache_k, cache_v, state_C, state_n, state_m, p_prompt, p_sample, norm_mix, w_in, b_in, attn_sinks, w_attn_up, w_ml_up, w_o, norm_mlp, w_ff1, w_ff2, norm_ple, w_ple_gate, w_ple_proj, norm_final):
    assert w_in.shape[0] == 1, "single layer"
    batch, seq, _ = x_prompt.shape
    nbatch = x_sample.shape[0]
    assert x_sample.shape[1] == 1

    w_in0 = w_in[0]
    b_in0 = b_in[0]
    w_proj, w_gate, w_au, w_mu, w_out, w_f1, w_f2, w_pg, w_pp = _weight_cast(
        w_in0.T, (w_attn_up[0], w_ml_up[0], w_o[0], w_ff1[0], w_ff2[0], w_ple_gate[0], w_ple_proj[0]))
    b_proj = b_in0[:N_PROJ][None, :]
    b_gate = b_in0[OFF_GA:][None, :]
    g_mix = norm_mix[0][None, :]
    out_weights = (
        g_mix, w_gate, b_gate, w_au, w_mu, w_out,
        norm_mlp[0][None, :], w_f1, w_f2,
        norm_ple[0][None, :], w_pg, w_pp, norm_final[None, :],
    )
    sinks = attn_sinks[0]

    cos_p, sin_p = _rope_tables(jnp.arange(seq, dtype=jnp.int32))
    xp = x_prompt.reshape(batch * seq, D_MODEL)
    qt, k, vt, k_last, v_last, mqt, mk, mvt, og, gif, gt = _in_proj(xp, g_mix, w_proj, b_proj, cos_p, sin_p, tm=512,
                                                                    m_dtype=BF16, feature_major=True)
    y_prompt, cst, nst, mst = _prompt_fused(sinks, xp, p_prompt[0].reshape(batch * seq, D_PLE), qt, k, vt,
                                            mqt, mk, mvt, og, gif, gt, out_weights, batch, seq, tm=256)
    y_prompt = y_prompt.reshape(batch, seq, D_MODEL)
    k_prompt = k_last.reshape(1, batch, WINDOW, N_KV_HEADS, HEAD_DIM)
    v_prompt = v_last.reshape(1, batch, WINDOW, N_KV_HEADS, HEAD_DIM)
    c_prompt = cst[None]
    n_prompt = nst[:, :ML_HEADS, :][None]
    m_prompt = mst[:, :ML_HEADS, 0][None]

    cos_s, sin_s = _rope_tables(jnp.full((nbatch,), PAST_LEN, dtype=jnp.int32))
    xs = x_sample.reshape(nbatch, D_MODEL)
    qs, _, _, ks, vs, mqs, mks, mvs, ogs, gifs, _ = _in_proj(xs, g_mix, w_proj, b_proj, cos_s, sin_s, tm=nbatch,
                                                          m_dtype=F32, feature_major=False)
    q4 = qs.reshape(nbatch, N_KV_HEADS, GQA_GROUP, 1, HEAD_DIM)
    sel = jnp.eye(N_KV_HEADS, dtype=BF16)[None, :, None, :, None]
    qmat = (q4 * sel).reshape(nbatch, N_Q_HEADS, ATTN_KV)
    to_fp = lambda c: c[0].transpose(0, 2, 3, 1).reshape(nbatch, ATTN_KV, WINDOW)
    from_fp = lambda c: c.reshape(nbatch, N_KV_HEADS, HEAD_DIM, WINDOW).transpose(0, 3, 1, 2)[None]
    o_full, k_s, v_s = _attn_decode(sinks[:, None], qmat, ks, vs, to_fp(cache_k), to_fp(cache_v), bt=8)
    o5 = o_full.reshape(nbatch, N_KV_HEADS, GQA_GROUP, N_KV_HEADS, HEAD_DIM)
    a_s = jnp.stack([o5[:, kv, :, kv, :] for kv in range(N_KV_HEADS)], axis=1).reshape(nbatch, ATTN_Q).astype(BF16)
    h_s, c_s, n_s, m_s = _mlstm_decode(mqs, mks, mvs, ogs, gifs, state_C[0], state_n[0], state_m[0], bt=8)
    y_sample = _out_proj(xs, a_s, h_s, p_sample[0].reshape(nbatch, D_PLE), out_weights, tm=nbatch)
    y_sample = y_sample.reshape(nbatch, 1, D_MODEL)
    k_sample = from_fp(k_s)
    v_sample = from_fp(v_s)

    return (y_prompt, y_sample, k_prompt, v_prompt, c_prompt, n_prompt, m_prompt,
            k_sample, v_sample, c_s[None], n_s[None], m_s[None])
```

```python
import functools

import numpy as np
import jax
import jax.numpy as jnp
from jax import lax
from jax.experimental import pallas as pl
from jax.experimental.pallas import tpu as pltpu

F32 = jnp.float32
BF16 = jnp.bfloat16

D_MODEL = 1024
HEAD_DIM = 64
N_Q_HEADS = 8
N_KV_HEADS = 2
GQA_GROUP = N_Q_HEADS // N_KV_HEADS
WINDOW = 128
ROPE_THETA = 10000.0
ML_HEADS = 4
ML_DIM = 128
ML_CHUNK = 128
D_FF = 4 * D_MODEL
D_PLE = 256
EPS = 1e-6
PAST_LEN = 16384

ATTN_Q = N_Q_HEADS * HEAD_DIM
ATTN_KV = N_KV_HEADS * HEAD_DIM
ML_W = ML_HEADS * ML_DIM
LANES = 128

OFF_AQ = 0
OFF_AK = OFF_AQ + ATTN_Q
OFF_AV = OFF_AK + ATTN_KV
OFF_MQ = OFF_AV + ATTN_KV
OFF_MK = OFF_MQ + ML_W
OFF_MV = OFF_MK + ML_W
OFF_MO = OFF_MV + ML_W
OFF_MI = OFF_MO + ML_W
OFF_MF = OFF_MI + ML_HEADS
OFF_GA = OFF_MF + ML_HEADS
OFF_GM = OFF_GA + D_MODEL
D_IN = OFF_GM + D_MODEL
N_PROJ = OFF_MI + LANES

VMEM_LIMIT = 60 * 1024 * 1024

def _rms(x, g):
    r = lax.rsqrt(jnp.mean(x * x, axis=-1, keepdims=True) + EPS)
    return (x * r) * g


def _mm(a, b):
    return jnp.dot(a, b, preferred_element_type=F32)


def _mm_nt(a, b):
    return lax.dot_general(a, b, (((1,), (1,)), ((), ())), preferred_element_type=F32)


def _mm_tn(a, b):
    return lax.dot_general(a, b, (((0,), (0,)), ((), ())), preferred_element_type=F32)


def _log_sigmoid(x):
    return jnp.minimum(x, 0.0) - jnp.log1p(jnp.exp(-jnp.abs(x)))


WEIGHT_ROW_SPLIT = 8


def _weight_cast_kernel(*refs):
    n_plain = (len(refs) - 3) // 2
    w_in_t_ref, plain_in = refs[0], refs[1:1 + n_plain]
    w_proj_ref, w_gate_ref, plain_out = refs[1 + n_plain], refs[2 + n_plain], refs[3 + n_plain:]
    for g in range(N_PROJ // LANES):
        w_proj_ref[:, g * LANES:(g + 1) * LANES] = w_in_t_ref[g * LANES:(g + 1) * LANES, :].T.astype(BF16)
    for g in range(2 * D_MODEL // LANES):
        w_gate_ref[:, g * LANES:(g + 1) * LANES] = w_in_t_ref[OFF_GA + g * LANES:OFF_GA + (g + 1) * LANES, :].T.astype(BF16)
    for src, dst in zip(plain_in, plain_out):
        dst[...] = src[...].astype(BF16)


def _weight_cast(w_in_t, plain):
    split = WEIGHT_ROW_SPLIT
    assert D_MODEL // split == LANES
    rows_blk = lambda w: pl.BlockSpec((w.shape[0] // split, w.shape[1]), lambda i: (i, 0))
    out_shapes = (jax.ShapeDtypeStruct((D_MODEL, N_PROJ), BF16), jax.ShapeDtypeStruct((D_MODEL, 2 * D_MODEL), BF16),
                  *(jax.ShapeDtypeStruct(w.shape, BF16) for w in plain))
    return pl.pallas_call(
        _weight_cast_kernel,
        grid=(split,),
        in_specs=[pl.BlockSpec((D_IN, LANES), lambda i: (0, i))] + [rows_blk(w) for w in plain],
        out_specs=tuple(rows_blk(s) for s in out_shapes),
        out_shape=out_shapes,
        compiler_params=pltpu.CompilerParams(dimension_semantics=("parallel",), vmem_limit_bytes=VMEM_LIMIT),
        name="weight_cast",
    )(w_in_t, *plain)


def _in_proj_kernel(x_ref, g_ref, w_ref, b_ref, cos_ref, sin_ref,
                    q_ref, k_ref, v_ref, kl_ref, vl_ref, mq_ref, mk_ref, mv_ref, og_ref, gif_ref, gt_ref,
                    *, feature_major):
    xn = _rms(x_ref[...], g_ref[...]).astype(BF16)

    def proj(lo, hi):
        return _mm(xn, w_ref[:, lo:hi]) + b_ref[:, lo:hi]

    cos = cos_ref[...]
    sin = sin_ref[...]
    lane = lax.broadcasted_iota(jnp.int32, (1, LANES), 1)
    first_half = (lane % HEAD_DIM) < (HEAD_DIM // 2)

    def rope(z):
        partner = jnp.where(first_half, pltpu.roll(z, LANES - HEAD_DIM // 2, 1), pltpu.roll(z, HEAD_DIM // 2, 1))
        return z * cos + partner * sin

    zq = proj(OFF_AQ, OFF_AK)
    for c in range(ATTN_Q // LANES):
        sl = slice(c * LANES, (c + 1) * LANES)
        qc = rope(zq[:, sl]) * (HEAD_DIM ** -0.5)
        if feature_major:
            q_ref[sl, :] = qc.T.astype(q_ref.dtype)
        else:
            q_ref[:, sl] = qc.astype(q_ref.dtype)
    zkv = proj(OFF_AK, OFF_MQ)
    k = rope(zkv[:, :ATTN_KV])
    v = zkv[:, ATTN_KV:]
    k_ref[...] = k.astype(k_ref.dtype)
    v_ref[...] = (v.T if feature_major else v).astype(v_ref.dtype)
    tm = k.shape[0]
    kl_ref[...] = k[tm - WINDOW:, :]
    vl_ref[...] = v[tm - WINDOW:, :]
    zmq = proj(OFF_MQ, OFF_MK)
    zmv = proj(OFF_MV, OFF_MO)
    if feature_major:
        for h in range(ML_HEADS):
            hs = slice(h * ML_DIM, (h + 1) * ML_DIM)
            mq_ref[0, hs, :] = zmq[:, hs].T.astype(mq_ref.dtype)
            mv_ref[0, hs, :] = zmv[:, hs].T.astype(mv_ref.dtype)
    else:
        mq_ref[...] = zmq.astype(mq_ref.dtype)
        mv_ref[...] = zmv.astype(mv_ref.dtype)
    mk_ref[...] = (proj(OFF_MK, OFF_MV) * (ML_DIM ** -0.5)).astype(mk_ref.dtype)
    og_ref[...] = jax.nn.sigmoid(proj(OFF_MO, OFF_MI))
    zg = proj(OFF_MI, N_PROJ)
    gif = jnp.where(lane < ML_HEADS, zg, jnp.where(lane < 2 * ML_HEADS, _log_sigmoid(zg), 0.0))
    if feature_major:
        L = ML_CHUNK
        tri = (lax.broadcasted_iota(jnp.int32, (L, L), 1) <= lax.broadcasted_iota(jnp.int32, (L, L), 0)).astype(F32)
        for c in range(tm // L):
            cs = slice(c * L, (c + 1) * L)
            G = gif[cs, :]
            cum = jnp.dot(tri, G, precision=lax.Precision.HIGHEST, preferred_element_type=F32)
            A = jnp.where(lane < ML_HEADS, G, cum)
            diff = pltpu.roll(pltpu.roll(A, ML_HEADS, 1) - A, ML_HEADS, 1)
            A = jnp.where(lane < 2 * ML_HEADS, A, jnp.where(lane < 3 * ML_HEADS, diff, 0.0))
            gif_ref[cs, :] = A
            gt_ref[0, :, cs] = A.T[0:8, :]
    else:
        gif_ref[...] = gif
        gt_ref[0] = jnp.zeros(gt_ref.shape[1:], F32)


def _in_proj(x, g, w, b, cos, sin, tm, m_dtype, feature_major):
    rows = x.shape[0]
    seq = cos.shape[0]
    tps = seq // tm
    nseq = rows // seq
    row = lambda i: (i, 0)
    const = lambda i: (0, 0)
    pos = lambda i: (i % tps, 0)
    last = lambda i: (i // tps, 0)
    row_out = lambda n, dt: (jax.ShapeDtypeStruct((rows, n), dt), pl.BlockSpec((tm, n), row))
    col_out = lambda n, dt: (jax.ShapeDtypeStruct((n, rows), dt), pl.BlockSpec((n, tm), lambda i: (0, i)))
    att_out = col_out if feature_major else row_out
    seq_out = lambda n, dt: (jax.ShapeDtypeStruct((nseq, n, seq), dt),
                             pl.BlockSpec((1, n, tm), lambda i: (i // tps, 0, i % tps)))
    ml_out = seq_out if feature_major else row_out
    outs = (
        att_out(ATTN_Q, BF16),
        row_out(ATTN_KV, BF16),
        att_out(ATTN_KV, BF16),
        (jax.ShapeDtypeStruct((nseq * WINDOW, ATTN_KV), F32), pl.BlockSpec((WINDOW, ATTN_KV), last)),
        (jax.ShapeDtypeStruct((nseq * WINDOW, ATTN_KV), F32), pl.BlockSpec((WINDOW, ATTN_KV), last)),
        ml_out(ML_W, m_dtype),
        row_out(ML_W, m_dtype),
        ml_out(ML_W, m_dtype),
        row_out(ML_W, F32),
        row_out(LANES, F32),
        seq_out(8, F32),
    )
    return pl.pallas_call(
        functools.partial(_in_proj_kernel, feature_major=feature_major),
        grid=(rows // tm,),
        in_specs=[
            pl.BlockSpec((tm, D_MODEL), row),
            pl.BlockSpec((1, D_MODEL), const),
            pl.BlockSpec((D_MODEL, N_PROJ), const),
            pl.BlockSpec((1, N_PROJ), const),
            pl.BlockSpec((tm, LANES), pos),
            pl.BlockSpec((tm, LANES), pos),
        ],
        out_specs=tuple(o[1] for o in outs),
        out_shape=tuple(o[0] for o in outs),
        compiler_params=pltpu.CompilerParams(dimension_semantics=("arbitrary",), vmem_limit_bytes=VMEM_LIMIT),
        name="in_proj",
    )(x, g, w, b, cos, sin)


def _attn_stages(sink_ref, qt_ref, kp_ref, kc_ref, vtp_ref, vtc_ref, first_block, emit):
    L = WINDOW
    nsub = kc_ref.shape[0] // L
    key = lax.broadcasted_iota(jnp.int32, (2 * L, L), 0)
    t = lax.broadcasted_iota(jnp.int32, (2 * L, L), 1)
    is_prev = key < L
    allowed = (is_prev & (key >= t)) | (~is_prev & (key - L <= t))
    bias = jnp.where(allowed, 0.0, -jnp.inf)
    bias_first = jnp.where(is_prev & first_block, -jnp.inf, bias)
    zeros_q = jnp.zeros((HEAD_DIM, L), BF16)
    heads = range(N_Q_HEADS)
    for r in range(nsub):
        ts = slice(r * L, (r + 1) * L)
        k_prev = kp_ref[...] if r == 0 else kc_ref[(r - 1) * L:r * L, :]
        vt_prev = vtp_ref[...] if r == 0 else vtc_ref[:, (r - 1) * L:r * L]
        k2 = jnp.concatenate([k_prev, kc_ref[ts, :]], axis=0)
        vt2 = jnp.concatenate([vt_prev, vtc_ref[:, ts]], axis=1)
        b_r = bias_first if r == 0 else bias
        q_op = [jnp.concatenate([qt_ref[hd * HEAD_DIM:(hd + 1) * HEAD_DIM, ts], zeros_q][::1 if hd < GQA_GROUP else -1],
                                axis=0) for hd in heads]
        s = [_mm(k2, q_op[hd]) + b_r for hd in heads]
        yield
        m = [jnp.maximum(jnp.max(s[hd], axis=0, keepdims=True), sink_ref[hd]) for hd in heads]
        yield
        p = [jnp.exp(s[hd] - m[hd]) for hd in heads]
        yield
        den = [jnp.sum(p[hd], axis=0, keepdims=True) + jnp.exp(sink_ref[hd] - m[hd]) for hd in heads]
        o = [_mm(vt2[(hd // GQA_GROUP) * HEAD_DIM:(hd // GQA_GROUP + 1) * HEAD_DIM, :], p[hd].astype(BF16)) / den[hd]
             for hd in heads]
        yield
        emit(r, jnp.concatenate(o, axis=0).T)
        yield


ML_STATE_ROWS = ML_DIM + 8


def _mlstm_stages(chains, emit):
    L = ML_CHUNK
    s_idx = lax.broadcasted_iota(jnp.int32, (L, L), 0)
    t_idx = lax.broadcasted_iota(jnp.int32, (L, L), 1)
    causal = s_idx <= t_idx
    n = range(len(chains))
    dlog = [jnp.where(causal, ch["b_r"] + ch["u_c"], -jnp.inf) for ch in chains]
    yield
    qk = [_mm(ch["k"], ch["qt"]) for ch in chains]
    inter = [_mm(ch["state"].astype(BF16), ch["qt"]) for ch in chains]
    yield
    a = [ch["b_r"] + ch["m_prev"] for ch in chains]
    m_t = [jnp.maximum(a[i], jnp.max(dlog[i], axis=0, keepdims=True)) for i in n]
    yield
    sw = [qk[i] * jnp.exp(dlog[i] - m_t[i]) for i in n]
    yield
    sv = [_mm(chains[i]["vt"], sw[i].astype(BF16)) for i in n]
    yield
    hs = []
    for i in n:
        aw = jnp.exp(a[i] - m_t[i])
        num = aw * inter[i][:ML_DIM] + sv[i]
        den = aw * inter[i][ML_DIM:ML_DIM + 1] + jnp.sum(sw[i], axis=0, keepdims=True)
        hh = num / jnp.maximum(jnp.abs(den), jnp.exp(-m_t[i]))
        hs.append(hh.T * chains[i]["og"])
    yield
    pad = jnp.zeros((ML_STATE_ROWS - ML_DIM - 1, L), F32)
    new_state, new_m = [], []
    for i in n:
        ch = chains[i]
        b_end = ch["b_r"][:, L - 1:L]
        g = b_end - ch["b_r"] + ch["ig_r"]
        m_new = jnp.maximum(b_end + ch["m_prev"], jnp.max(g, axis=1, keepdims=True))
        decay = jnp.exp(b_end + ch["m_prev"] - m_new)
        w = jnp.exp(g - m_new)
        v_aug = jnp.concatenate([ch["vt"].astype(F32) * w, w, pad], axis=0).astype(BF16)
        new_state.append(decay * ch["state"] + _mm(v_aug, ch["k"]))
        new_m.append(m_new)
    emit(hs, new_state, new_m)
    yield


DENSE_COLS = 512


def _dense_stages(x, a, hm, p, gmix_ref, wg_ref, bg_ref, wau_ref, wmu_ref, wo_ref,
                  gmlp_ref, wff1_ref, wff2_ref, gple_ref, wpg_ref, wpp_ref, gfin_ref, emit):
    nc = DENSE_COLS
    pieces = lambda n: [slice(c, c + nc) for c in range(0, n, nc)]
    xn = _rms(x, gmix_ref[...]).astype(BF16)
    gates = []
    for cs in pieces(2 * D_MODEL):
        gates.append(jax.nn.sigmoid(_mm(xn, wg_ref[:, cs]) + bg_ref[:, cs]))
        yield
    half = D_MODEL // nc
    mix = []
    for j, cs in enumerate(pieces(D_MODEL)):
        mix.append((gates[j] * _mm(a, wau_ref[:, cs]) + gates[half + j] * _mm(hm, wmu_ref[:, cs])).astype(BF16))
        yield
    mix = jnp.concatenate(mix, axis=1)
    h = []
    for cs in pieces(D_MODEL):
        h.append(x[:, cs] + _mm(mix, wo_ref[:, cs]))
        yield
    h = jnp.concatenate(h, axis=1)
    hn = _rms(h, gmlp_ref[...]).astype(BF16)
    u = []
    for cs in pieces(D_FF):
        u.append(jnp.square(jnp.maximum(_mm(hn, wff1_ref[:, cs]), 0.0)).astype(BF16))
        yield
    u = jnp.concatenate(u, axis=1)
    h2 = []
    for cs in pieces(D_MODEL):
        h2.append(h[:, cs] + _mm(u, wff2_ref[:, cs]))
        yield
    h = jnp.concatenate(h2, axis=1)
    hn = _rms(h, gple_ref[...]).astype(BF16)
    pb = p.astype(BF16)
    h3 = []
    for cs in pieces(D_MODEL):
        h3.append(h[:, cs] + _mm(pb, wpp_ref[:, cs]) * jax.nn.sigmoid(_mm(hn, wpg_ref[:, cs])))
        yield
    emit(_rms(jnp.concatenate(h3, axis=1), gfin_ref[...]))
    yield


def _weave(*streams, delays=None):
    delays = delays or [0] * len(streams)
    done = set()
    rnd = 0
    while len(done) < len(set(map(id, streams))):
        for g, d in zip(streams, delays):
            if rnd >= d and id(g) not in done:
                try:
                    next(g)
                except StopIteration:
                    done.add(id(g))
        rnd += 1


def _out_kernel(x_ref, a_ref, hm_ref, p_ref, *rest):
    weights, y_ref = rest[:-1], rest[-1]

    def emit(y):
        y_ref[...] = y

    _weave(_dense_stages(x_ref[...], a_ref[...], hm_ref[...], p_ref[...], *weights, emit=emit))


def _out_proj(x, a, hm, p, weights, tm):
    rows = x.shape[0]
    row = lambda i: (i, 0)
    const = lambda i: (0, 0)
    w_specs = [pl.BlockSpec(w.shape, const, pipeline_mode=pl.Buffered(1)) for w in weights]
    return pl.pallas_call(
        _out_kernel,
        grid=(rows // tm,),
        in_specs=[
            pl.BlockSpec((tm, D_MODEL), row),
            pl.BlockSpec((tm, ATTN_Q), row),
            pl.BlockSpec((tm, ML_W), row),
            pl.BlockSpec((tm, D_PLE), row),
        ] + w_specs,
        out_specs=pl.BlockSpec((tm, D_MODEL), row),
        out_shape=jax.ShapeDtypeStruct((rows, D_MODEL), F32),
        compiler_params=pltpu.CompilerParams(dimension_semantics=("parallel",), vmem_limit_bytes=VMEM_LIMIT),
        name="out_proj",
    )(x, a, hm, p, *weights)


N_FUSED_WEIGHTS = 13
DENSE_STREAMS = 2
DENSE_STAGGER = 11


def _prompt_kernel(sink_ref, x_ref, p_ref, qt_ref, kp_ref, kc_ref, vtp_ref, vtc_ref,
                   mqt_ref, mk_ref, mvt_ref, og_ref, gif_ref, gt_ref, *rest, tiles_per_seq):
    weights = rest[:N_FUSED_WEIGHTS]
    y_ref, cst_ref, nst_ref, mst_ref, a_scr, h_scr, c_scr, m_scr = rest[N_FUSED_WEIGHTS:]
    i = pl.program_id(0)
    n_tiles = pl.num_programs(0) - 1
    tile_in_seq = jnp.minimum(i, n_tiles - 1) % tiles_per_seq
    slot = i % 2
    L = ML_CHUNK
    tm = kc_ref.shape[0]

    @pl.when(i == 0)
    def _():
        a_scr[...] = jnp.zeros_like(a_scr)
        h_scr[...] = jnp.zeros_like(h_scr)

    @pl.when(tile_in_seq == 0)
    def _():
        c_scr[...] = jnp.zeros_like(c_scr)
        m_scr[...] = jnp.zeros_like(m_scr)

    def branches():
        def emit_attn(r, blk):
            a_scr[slot, r * WINDOW:(r + 1) * WINDOW, :] = blk.astype(a_scr.dtype)

        yield from _attn_stages(sink_ref, qt_ref, kp_ref, kc_ref, vtp_ref, vtc_ref, tile_in_seq == 0, emit_attn)
        hsl = lambda h: slice(h * ML_DIM, (h + 1) * ML_DIM)
        carry = dict(state=[c_scr[h] for h in range(ML_HEADS)], m=[m_scr[h:h + 1, :] for h in range(ML_HEADS)])
        for c in range(tm // L):
            cs = slice(c * L, (c + 1) * L)

            def emit_ml(hs, new_state, new_m, cs=cs):
                for h in range(ML_HEADS):
                    h_scr[slot, cs, hsl(h)] = hs[h].astype(h_scr.dtype)
                carry["state"], carry["m"] = new_state, new_m

            chains = [dict(qt=mqt_ref[0, hsl(h), cs], k=mk_ref[cs, hsl(h)], vt=mvt_ref[0, hsl(h), cs],
                           og=og_ref[cs, hsl(h)], ig_r=gt_ref[0, h:h + 1, cs],
                           b_r=gt_ref[0, ML_HEADS + h:ML_HEADS + h + 1, cs],
                           u_c=gif_ref[cs, 2 * ML_HEADS + h:2 * ML_HEADS + h + 1],
                           state=carry["state"][h], m_prev=carry["m"][h]) for h in range(ML_HEADS)]
            yield from _mlstm_stages(chains, emit_ml)
        for h in range(ML_HEADS):
            c_scr[h] = carry["state"][h]
            m_scr[h:h + 1, :] = carry["m"][h]

    def dense(rows):
        def emit_y(y):
            y_ref[rows, :] = y

        return _dense_stages(x_ref[rows, :], a_scr[1 - slot, rows, :], h_scr[1 - slot, rows, :], p_ref[rows, :],
                             *weights, emit=emit_y)

    sr = tm // DENSE_STREAMS
    br = branches()
    streams = [dense(slice(k * sr, (k + 1) * sr)) for k in range(DENSE_STREAMS)]
    _weave(br, streams[0], *([br] + streams[1:] if DENSE_STREAMS > 1 else []),
           delays=[0, 0] + ([0] + [DENSE_STAGGER * k for k in range(1, DENSE_STREAMS)] if DENSE_STREAMS > 1 else []))

    @pl.when((tile_in_seq == tiles_per_seq - 1) & (i < n_tiles))
    def _():
        nst_ref[...] = jnp.zeros_like(nst_ref)
        for h in range(ML_HEADS):
            final = c_scr[h]
            cst_ref[0, h] = final[:ML_DIM].T
            nst_ref[0, h:h + 1, :] = final[ML_DIM:ML_DIM + 1]
        mst_ref[0] = m_scr[...]


def _prompt_fused(sinks, x, p, qt, k, vt, mqt, mk, mvt, og, gif, gt, weights, batch, seq, tm):
    assert len(weights) == N_FUSED_WEIGHTS
    rows = batch * seq
    n_tiles = rows // tm
    tps = seq // tm
    sub = tm // WINDOW
    ta = lambda i: jnp.minimum(i, n_tiles - 1)
    tb = lambda i: jnp.maximum(i - 1, 0)
    prev_blk = lambda i: jnp.maximum(ta(i) * sub - 1, 0)
    const = lambda i: (0, 0)
    seq_blk = lambda i: (ta(i) // tps, 0, ta(i) % tps)
    per_seq = lambda i: (ta(i) // tps, 0, 0)
    w_specs = [pl.BlockSpec(w.shape, const, pipeline_mode=pl.Buffered(1)) for w in weights]
    return pl.pallas_call(
        functools.partial(_prompt_kernel, tiles_per_seq=tps),
        grid=(n_tiles + 1,),
        in_specs=[
            pl.BlockSpec(memory_space=pltpu.SMEM),
            pl.BlockSpec((tm, D_MODEL), lambda i: (tb(i), 0)),
            pl.BlockSpec((tm, D_PLE), lambda i: (tb(i), 0)),
            pl.BlockSpec((ATTN_Q, tm), lambda i: (0, ta(i))),
            pl.BlockSpec((WINDOW, ATTN_KV), lambda i: (prev_blk(i), 0)),
            pl.BlockSpec((tm, ATTN_KV), lambda i: (ta(i), 0)),
            pl.BlockSpec((ATTN_KV, WINDOW), lambda i: (0, prev_blk(i))),
            pl.BlockSpec((ATTN_KV, tm), lambda i: (0, ta(i))),
            pl.BlockSpec((1, ML_W, tm), seq_blk),
            pl.BlockSpec((tm, ML_W), lambda i: (ta(i), 0)),
            pl.BlockSpec((1, ML_W, tm), seq_blk),
            pl.BlockSpec((tm, ML_W), lambda i: (ta(i), 0)),
            pl.BlockSpec((tm, LANES), lambda i: (ta(i), 0)),
            pl.BlockSpec((1, 8, tm), seq_blk),
        ] + w_specs,
        out_specs=(
            pl.BlockSpec((tm, D_MODEL), lambda i: (tb(i), 0)),
            pl.BlockSpec((1, ML_HEADS, ML_DIM, ML_DIM), lambda i: (ta(i) // tps, 0, 0, 0)),
            pl.BlockSpec((1, 8, LANES), per_seq),
            pl.BlockSpec((1, 8, LANES), per_seq),
        ),
        out_shape=(
            jax.ShapeDtypeStruct((rows, D_MODEL), F32),
            jax.ShapeDtypeStruct((batch, ML_HEADS, ML_DIM, ML_DIM), F32),
            jax.ShapeDtypeStruct((batch, 8, LANES), F32),
            jax.ShapeDtypeStruct((batch, 8, LANES), F32),
        ),
        scratch_shapes=[pltpu.VMEM((2, tm, ATTN_Q), BF16), pltpu.VMEM((2, tm, ML_W), BF16),
                        pltpu.VMEM((ML_HEADS, ML_STATE_ROWS, ML_DIM), F32), pltpu.VMEM((8, LANES), F32)],
        compiler_params=pltpu.CompilerParams(dimension_semantics=("arbitrary",), vmem_limit_bytes=VMEM_LIMIT),
        name="prompt_fused",
    )(sinks, x, p, qt, k, k, vt, vt, mqt, mk, mvt, og, gif, gt, *weights)


def _attn_decode_kernel(sink_ref, qm_ref, kn_ref, vn_ref, ck_ref, cv_ref, o_ref, ko_ref, vo_ref):
    bt = qm_ref.shape[0]
    qm = qm_ref[...]
    kn = kn_ref[...]
    vn = vn_ref[...]
    ck = ck_ref[...]
    cv = cv_ref[...]
    sink = sink_ref[...]
    s = jnp.einsum('bhc,bck->bhk', qm, ck.astype(BF16), preferred_element_type=F32)
    s_new = jnp.sum(qm.astype(F32) * kn[:, None, :], axis=2, keepdims=True)
    m = jnp.maximum(jnp.maximum(jnp.max(s, axis=2, keepdims=True), s_new), sink)
    p = jnp.exp(s - m)
    p_new = jnp.exp(s_new - m)
    den = jnp.sum(p, axis=2, keepdims=True) + p_new + jnp.exp(sink - m)
    o = jnp.einsum('bhk,bck->bhc', p.astype(BF16), cv.astype(BF16), preferred_element_type=F32)
    o_ref[...] = (o + p_new * vn[:, None, :]) / den
    pad = jnp.zeros((LANES - bt, ATTN_KV), F32)
    kn_t = jnp.concatenate([kn, pad], axis=0).T
    vn_t = jnp.concatenate([vn, pad], axis=0).T
    newest = lax.broadcasted_iota(jnp.int32, (ATTN_KV, WINDOW), 1) == WINDOW - 1
    for b in range(bt):
        ko_ref[b] = jnp.where(newest, kn_t[:, b:b + 1], pltpu.roll(ck[b], WINDOW - 1, 1))
        vo_ref[b] = jnp.where(newest, vn_t[:, b:b + 1], pltpu.roll(cv[b], WINDOW - 1, 1))


def _attn_decode(sinks_col, qmat, k_new, v_new, cache_k, cache_v, bt):
    nbatch = qmat.shape[0]
    b3 = lambda i: (i, 0, 0)
    b2 = lambda i: (i, 0)
    return pl.pallas_call(
        _attn_decode_kernel,
        grid=(nbatch // bt,),
        in_specs=[
            pl.BlockSpec((N_Q_HEADS, 1), lambda i: (0, 0)),
            pl.BlockSpec((bt, N_Q_HEADS, ATTN_KV), b3),
            pl.BlockSpec((bt, ATTN_KV), b2),
            pl.BlockSpec((bt, ATTN_KV), b2),
            pl.BlockSpec((bt, WINDOW, ATTN_KV), b3),
            pl.BlockSpec((bt, WINDOW, ATTN_KV), b3),
        ],
        out_specs=(
            pl.BlockSpec((bt, N_Q_HEADS, ATTN_KV), b3),
            pl.BlockSpec((bt, WINDOW, ATTN_KV), b3),
            pl.BlockSpec((bt, WINDOW, ATTN_KV), b3),
        ),
        out_shape=(
            jax.ShapeDtypeStruct((nbatch, N_Q_HEADS, ATTN_KV), F32),
            jax.ShapeDtypeStruct((nbatch, WINDOW, ATTN_KV), F32),
            jax.ShapeDtypeStruct((nbatch, WINDOW, ATTN_KV), F32),
        ),
        compiler_params=pltpu.CompilerParams(dimension_semantics=("parallel",)),
        name="attn_decode",
    )(sinks_col, qmat, k_new, v_new, cache_k, cache_v)


def _mlstm_decode_kernel(q_ref, k_ref, v_ref, og_ref, gif_ref, c_ref, n_ref, m_ref, h_ref, co_ref, no_ref, mo_ref):
    bt = q_ref.shape[0]
    gif = gif_ref[...]
    ig = gif[:, 0:ML_HEADS]
    lf = gif[:, ML_HEADS:2 * ML_HEADS]
    m_prev = m_ref[...]
    a = lf + m_prev
    m_t = jnp.maximum(a, ig)
    aw = jnp.exp(a - m_t)
    e_i = jnp.exp(ig - m_t)
    floor = jnp.exp(-m_t)
    m_new = jnp.maximum(a, ig)
    decay = jnp.exp(a - m_new)
    w = jnp.exp(ig - m_new)
    mo_ref[...] = m_new
    pad = jnp.zeros((LANES - bt, ML_DIM), F32)
    lane = lax.broadcasted_iota(jnp.int32, (ML_DIM, LANES), 1)
    for h in range(ML_HEADS):
        hs = slice(h * ML_DIM, (h + 1) * ML_DIM)
        q = q_ref[:, hs]
        k = k_ref[:, hs]
        v = v_ref[:, hs]
        n = n_ref[:, h, :]
        qk = jnp.sum(q * k, axis=1, keepdims=True)
        qn = jnp.sum(q * n, axis=1, keepdims=True)
        sw = qk * e_i[:, h:h + 1]
        den = aw[:, h:h + 1] * qn + sw
        inv = 1.0 / jnp.maximum(jnp.abs(den), floor[:, h:h + 1])
        no_ref[:, h, :] = decay[:, h:h + 1] * n + w[:, h:h + 1] * k
        q_b = q.astype(BF16)
        wk_t = jnp.concatenate([w[:, h:h + 1] * k, pad], axis=0).T.astype(BF16)
        v_rows = jnp.concatenate([v, pad], axis=0).astype(BF16)
        nums = []
        for b in range(bt):
            C = c_ref[b, h]
            q_c = _mm(q_b, C.astype(BF16))[b:b + 1, :]
            nums.append(aw[b:b + 1, h:h + 1] * q_c)
            outer = _mm(jnp.where(lane == b, wk_t, jnp.zeros_like(wk_t)), v_rows)
            co_ref[b, h] = decay[b:b + 1, h:h + 1] * C + outer
        num = jnp.concatenate(nums, axis=0) + sw * v
        h_ref[:, hs] = (num * inv * og_ref[:, hs]).astype(h_ref.dtype)


def _mlstm_decode(mq, mk, mv, og, gif, state_c, state_n, state_m, bt):
    nbatch = mq.shape[0]
    b2 = lambda i: (i, 0)
    b3 = lambda i: (i, 0, 0)
    b4 = lambda i: (i, 0, 0, 0)
    return pl.pallas_call(
        _mlstm_decode_kernel,
        grid=(nbatch // bt,),
        in_specs=[
            pl.BlockSpec((bt, ML_W), b2),
            pl.BlockSpec((bt, ML_W), b2),
            pl.BlockSpec((bt, ML_W), b2),
            pl.BlockSpec((bt, ML_W), b2),
            pl.BlockSpec((bt, LANES), b2),
            pl.BlockSpec((bt, ML_HEADS, ML_DIM, ML_DIM), b4),
            pl.BlockSpec((bt, ML_HEADS, ML_DIM), b3),
            pl.BlockSpec((bt, ML_HEADS), b2),
        ],
        out_specs=(
            pl.BlockSpec((bt, ML_W), b2),
            pl.BlockSpec((bt, ML_HEADS, ML_DIM, ML_DIM), b4),
            pl.BlockSpec((bt, ML_HEADS, ML_DIM), b3),
            pl.BlockSpec((bt, ML_HEADS), b2),
        ),
        out_shape=(
            jax.ShapeDtypeStruct((nbatch, ML_W), BF16),
            jax.ShapeDtypeStruct((nbatch, ML_HEADS, ML_DIM, ML_DIM), F32),
            jax.ShapeDtypeStruct((nbatch, ML_HEADS, ML_DIM), F32),
            jax.ShapeDtypeStruct((nbatch, ML_HEADS), F32),
        ),
        compiler_params=pltpu.CompilerParams(dimension_semantics=("parallel",)),
        name="mlstm_decode",
    )(mq, mk, mv, og, gif, state_c, state_n, state_m)


def _rope_tables(pos):
    half = HEAD_DIM // 2
    inv = 1.0 / (ROPE_THETA ** (jnp.arange(half, dtype=F32) / half))
    ang = pos.astype(F32)[:, None] * inv[None, :]
    cos = jnp.cos(ang)
    sin = jnp.sin(ang)
    cos_t = jnp.concatenate([cos, cos, cos, cos], axis=1)
    sin_t = jnp.concatenate([-sin, sin, -sin, sin], axis=1)
    return cos_t, sin_t


def kernel(x_prompt, x_sample, cache_k, cache_v, state_C, state_n, state_m, p_prompt, p_sample, norm_mix, w_in, b_in, attn_sinks, w_attn_up, w_ml_up, w_o, norm_mlp, w_ff1, w_ff2, norm_ple, w_ple_gate, w_ple_proj, norm_final):
    assert w_in.shape[0] == 1, "single layer"
    batch, seq, _ = x_prompt.shape
    nbatch = x_sample.shape[0]
    assert x_sample.shape[1] == 1

    w_in0 = w_in[0]
    b_in0 = b_in[0]
    w_proj, w_gate, w_au, w_mu, w_out, w_f1, w_f2, w_pg, w_pp = _weight_cast(
        w_in0.T, (w_attn_up[0], w_ml_up[0], w_o[0], w_ff1[0], w_ff2[0], w_ple_gate[0], w_ple_proj[0]))
    b_proj = b_in0[:N_PROJ][None, :]
    b_gate = b_in0[OFF_GA:][None, :]
    g_mix = norm_mix[0][None, :]
    out_weights = (
        g_mix, w_gate, b_gate, w_au, w_mu, w_out,
        norm_mlp[0][None, :], w_f1, w_f2,
        norm_ple[0][None, :], w_pg, w_pp, norm_final[None, :],
    )
    sinks = attn_sinks[0]

    cos_p, sin_p = _rope_tables(jnp.arange(seq, dtype=jnp.int32))
    xp = x_prompt.reshape(batch * seq, D_MODEL)
    qt, k, vt, k_last, v_last, mqt, mk, mvt, og, gif, gt = _in_proj(xp, g_mix, w_proj, b_proj, cos_p, sin_p, tm=512,
                                                                    m_dtype=BF16, feature_major=True)
    y_prompt, cst, nst, mst = _prompt_fused(sinks, xp, p_prompt[0].reshape(batch * seq, D_PLE), qt, k, vt,
                                            mqt, mk, mvt, og, gif, gt, out_weights, batch, seq, tm=512)
    y_prompt = y_prompt.reshape(batch, seq, D_MODEL)
    k_prompt = k_last.reshape(1, batch, WINDOW, N_KV_HEADS, HEAD_DIM)
    v_prompt = v_last.reshape(1, batch, WINDOW, N_KV_HEADS, HEAD_DIM)
    c_prompt = cst[None]
    n_prompt = nst[:, :ML_HEADS, :][None]
    m_prompt = mst[:, :ML_HEADS, 0][None]

    cos_s, sin_s = _rope_tables(jnp.full((nbatch,), PAST_LEN, dtype=jnp.int32))
    xs = x_sample.reshape(nbatch, D_MODEL)
    qs, _, _, ks, vs, mqs, mks, mvs, ogs, gifs, _ = _in_proj(xs, g_mix, w_proj, b_proj, cos_s, sin_s, tm=nbatch,
                                                          m_dtype=F32, feature_major=False)
    q4 = qs.reshape(nbatch, N_KV_HEADS, GQA_GROUP, 1, HEAD_DIM)
    sel = jnp.eye(N_KV_HEADS, dtype=BF16)[None, :, None, :, None]
    qmat = (q4 * sel).reshape(nbatch, N_Q_HEADS, ATTN_KV)
    to_fp = lambda c: c[0].transpose(0, 2, 3, 1).reshape(nbatch, ATTN_KV, WINDOW)
    from_fp = lambda c: c.reshape(nbatch, N_KV_HEADS, HEAD_DIM, WINDOW).transpose(0, 3, 1, 2)[None]
    o_full, k_s, v_s = _attn_decode(sinks[:, None], qmat, ks, vs, to_fp(cache_k), to_fp(cache_v), bt=8)
    o5 = o_full.reshape(nbatch, N_KV_HEADS, GQA_GROUP, N_KV_HEADS, HEAD_DIM)
    a_s = jnp.stack([o5[:, kv, :, kv, :] for kv in range(N_KV_HEADS)], axis=1).reshape(nbatch, ATTN_Q).astype(BF16)
    h_s, c_s, n_s, m_s = _mlstm_decode(mqs, mks, mvs, ogs, gifs, state_C[0], state_n[0], state_m[0], bt=8)
    y_sample = _out_proj(xs, a_s, h_s, p_sample[0].reshape(nbatch, D_PLE), out_weights, tm=nbatch)
    y_sample = y_sample.reshape(nbatch, 1, D_MODEL)
    k_sample = from_fp(k_s)
    v_sample = from_fp(v_s)

    return (y_prompt, y_sample, k_prompt, v_prompt, c_prompt, n_prompt, m_prompt,
            k_sample, v_sample, c_s[None], n_s[None], m_s[None])
```

```python
import functools

import numpy as np
import jax
import jax.numpy as jnp
from jax import lax
from jax.experimental import pallas as pl
from jax.experimental.pallas import tpu as pltpu

F32 = jnp.float32
BF16 = jnp.bfloat16

D_MODEL = 1024
HEAD_DIM = 64
N_Q_HEADS = 8
N_KV_HEADS = 2
GQA_GROUP = N_Q_HEADS // N_KV_HEADS
WINDOW = 128
ROPE_THETA = 10000.0
ML_HEADS = 4
ML_DIM = 128
ML_CHUNK = 128
D_FF = 4 * D_MODEL
D_PLE = 256
EPS = 1e-6
PAST_LEN = 16384

ATTN_Q = N_Q_HEADS * HEAD_DIM
ATTN_KV = N_KV_HEADS * HEAD_DIM
ML_W = ML_HEADS * ML_DIM
LANES = 128

OFF_AQ = 0
OFF_AK = OFF_AQ + ATTN_Q
OFF_AV = OFF_AK + ATTN_KV
OFF_MQ = OFF_AV + ATTN_KV
OFF_MK = OFF_MQ + ML_W
OFF_MV = OFF_MK + ML_W
OFF_MO = OFF_MV + ML_W
OFF_MI = OFF_MO + ML_W
OFF_MF = OFF_MI + ML_HEADS
OFF_GA = OFF_MF + ML_HEADS
OFF_GM = OFF_GA + D_MODEL
D_IN = OFF_GM + D_MODEL
N_PROJ = OFF_MI + LANES

VMEM_LIMIT = 60 * 1024 * 1024

def _rms(x, g):
    r = lax.rsqrt(jnp.mean(x * x, axis=-1, keepdims=True) + EPS)
    return (x * r) * g


def _mm(a, b):
    return jnp.dot(a, b, preferred_element_type=F32)


def _mm_nt(a, b):
    return lax.dot_general(a, b, (((1,), (1,)), ((), ())), preferred_element_type=F32)


def _mm_tn(a, b):
    return lax.dot_general(a, b, (((0,), (0,)), ((), ())), preferred_element_type=F32)


def _log_sigmoid(x):
    return jnp.minimum(x, 0.0) - jnp.log1p(jnp.exp(-jnp.abs(x)))


def _weave(*streams):
    live = list(streams)
    while live:
        for g in list(live):
            try:
                next(g)
            except StopIteration:
                live.remove(g)


WEIGHT_ROW_SPLIT = 8


def _weight_cast_kernel(*refs):
    n_plain = (len(refs) - 3) // 2
    w_in_t_ref, plain_in = refs[0], refs[1:1 + n_plain]
    w_proj_ref, w_gate_ref, plain_out = refs[1 + n_plain], refs[2 + n_plain], refs[3 + n_plain:]
    for g in range(N_PROJ // LANES):
        w_proj_ref[:, g * LANES:(g + 1) * LANES] = w_in_t_ref[g * LANES:(g + 1) * LANES, :].T.astype(BF16)
    for g in range(2 * D_MODEL // LANES):
        w_gate_ref[:, g * LANES:(g + 1) * LANES] = w_in_t_ref[OFF_GA + g * LANES:OFF_GA + (g + 1) * LANES, :].T.astype(BF16)
    for src, dst in zip(plain_in, plain_out):
        dst[...] = src[...].astype(BF16)


def _weight_cast(w_in_t, plain):
    split = WEIGHT_ROW_SPLIT
    assert D_MODEL // split == LANES
    rows_blk = lambda w: pl.BlockSpec((w.shape[0] // split, w.shape[1]), lambda i: (i, 0))
    out_shapes = (jax.ShapeDtypeStruct((D_MODEL, N_PROJ), BF16), jax.ShapeDtypeStruct((D_MODEL, 2 * D_MODEL), BF16),
                  *(jax.ShapeDtypeStruct(w.shape, BF16) for w in plain))
    return pl.pallas_call(
        _weight_cast_kernel,
        grid=(split,),
        in_specs=[pl.BlockSpec((D_IN, LANES), lambda i: (0, i))] + [rows_blk(w) for w in plain],
        out_specs=tuple(rows_blk(s) for s in out_shapes),
        out_shape=out_shapes,
        compiler_params=pltpu.CompilerParams(dimension_semantics=("parallel",), vmem_limit_bytes=VMEM_LIMIT),
        name="weight_cast",
    )(w_in_t, *plain)


def _in_proj_stages(x, g_ref, w_ref, b_ref, cos_ref, sin_ref,
                    q_ref, k_ref, v_ref, kl_ref, vl_ref, mq_ref, mk_ref, mv_ref, og_ref, gif_ref, gt_ref,
                    *, feature_major):
    xn = _rms(x, g_ref[...]).astype(BF16)
    tm = x.shape[0]

    def proj(lo, hi):
        return _mm(xn, w_ref[:, lo:hi]) + b_ref[:, lo:hi]

    cos = cos_ref[...]
    sin = sin_ref[...]
    lane = lax.broadcasted_iota(jnp.int32, (1, LANES), 1)
    first_half = (lane % HEAD_DIM) < (HEAD_DIM // 2)

    def rope(z):
        partner = jnp.where(first_half, pltpu.roll(z, LANES - HEAD_DIM // 2, 1), pltpu.roll(z, HEAD_DIM // 2, 1))
        return z * cos + partner * sin

    zq = proj(OFF_AQ, OFF_AK)
    for c in range(ATTN_Q // LANES):
        sl = slice(c * LANES, (c + 1) * LANES)
        qc = rope(zq[:, sl]) * (HEAD_DIM ** -0.5)
        if feature_major:
            q_ref[sl, :] = qc.T.astype(q_ref.dtype)
        else:
            q_ref[:, sl] = qc.astype(q_ref.dtype)
    yield
    zkv = proj(OFF_AK, OFF_MQ)
    k = rope(zkv[:, :ATTN_KV])
    v = zkv[:, ATTN_KV:]
    k_ref[...] = k.astype(k_ref.dtype)
    v_ref[...] = (v.T if feature_major else v).astype(v_ref.dtype)
    kl_ref[...] = k[tm - WINDOW:, :]
    vl_ref[...] = v[tm - WINDOW:, :]
    yield
    for z_off, dst in ((OFF_MQ, mq_ref), (OFF_MV, mv_ref)):
        z = proj(z_off, z_off + ML_W)
        if feature_major:
            for h in range(ML_HEADS):
                hs = slice(h * ML_DIM, (h + 1) * ML_DIM)
                dst[0, hs, :] = z[:, hs].T.astype(dst.dtype)
        else:
            dst[...] = z.astype(dst.dtype)
        yield
    mk_ref[...] = (proj(OFF_MK, OFF_MV) * (ML_DIM ** -0.5)).astype(mk_ref.dtype)
    yield
    og_ref[...] = jax.nn.sigmoid(proj(OFF_MO, OFF_MI))
    yield
    zg = proj(OFF_MI, N_PROJ)
    gif = jnp.where(lane < ML_HEADS, zg, jnp.where(lane < 2 * ML_HEADS, _log_sigmoid(zg), 0.0))
    if feature_major:
        L = ML_CHUNK
        tri = (lax.broadcasted_iota(jnp.int32, (L, L), 1) <= lax.broadcasted_iota(jnp.int32, (L, L), 0)).astype(F32)
        for c in range(tm // L):
            cs = slice(c * L, (c + 1) * L)
            G = gif[cs, :]
            cum = jnp.dot(tri, G, precision=lax.Precision.HIGHEST, preferred_element_type=F32)
            A = jnp.where(lane < ML_HEADS, G, cum)
            diff = pltpu.roll(pltpu.roll(A, ML_HEADS, 1) - A, ML_HEADS, 1)
            A = jnp.where(lane < 2 * ML_HEADS, A, jnp.where(lane < 3 * ML_HEADS, diff, 0.0))
            gif_ref[cs, :] = A
            gt_ref[0, :, cs] = A.T[0:8, :]
    else:
        gif_ref[...] = gif
        gt_ref[0] = jnp.zeros(gt_ref.shape[1:], F32)
    yield


def _in_proj_kernel(x_ref, *refs, feature_major):
    _weave(_in_proj_stages(x_ref[...], *refs, feature_major=feature_major))


def _in_proj(x, g, w, b, cos, sin, tm, m_dtype, feature_major):
    rows = x.shape[0]
    seq = cos.shape[0]
    tps = seq // tm
    nseq = rows // seq
    row = lambda i: (i, 0)
    const = lambda i: (0, 0)
    pos = lambda i: (i % tps, 0)
    last = lambda i: (i // tps, 0)
    row_out = lambda n, dt: (jax.ShapeDtypeStruct((rows, n), dt), pl.BlockSpec((tm, n), row))
    col_out = lambda n, dt: (jax.ShapeDtypeStruct((n, rows), dt), pl.BlockSpec((n, tm), lambda i: (0, i)))
    att_out = col_out if feature_major else row_out
    seq_out = lambda n, dt: (jax.ShapeDtypeStruct((nseq, n, seq), dt),
                             pl.BlockSpec((1, n, tm), lambda i: (i // tps, 0, i % tps)))
    ml_out = seq_out if feature_major else row_out
    outs = (
        att_out(ATTN_Q, BF16),
        row_out(ATTN_KV, BF16),
        att_out(ATTN_KV, BF16),
        (jax.ShapeDtypeStruct((nseq * WINDOW, ATTN_KV), F32), pl.BlockSpec((WINDOW, ATTN_KV), last)),
        (jax.ShapeDtypeStruct((nseq * WINDOW, ATTN_KV), F32), pl.BlockSpec((WINDOW, ATTN_KV), last)),
        ml_out(ML_W, m_dtype),
        row_out(ML_W, m_dtype),
        ml_out(ML_W, m_dtype),
        row_out(ML_W, F32),
        row_out(LANES, F32),
        seq_out(8, F32),
    )
    return pl.pallas_call(
        functools.partial(_in_proj_kernel, feature_major=feature_major),
        grid=(rows // tm,),
        in_specs=[
            pl.BlockSpec((tm, D_MODEL), row),
            pl.BlockSpec((1, D_MODEL), const),
            pl.BlockSpec((D_MODEL, N_PROJ), const),
            pl.BlockSpec((1, N_PROJ), const),
            pl.BlockSpec((tm, LANES), pos),
            pl.BlockSpec((tm, LANES), pos),
        ],
        out_specs=tuple(o[1] for o in outs),
        out_shape=tuple(o[0] for o in outs),
        compiler_params=pltpu.CompilerParams(dimension_semantics=("arbitrary",), vmem_limit_bytes=VMEM_LIMIT),
        name="in_proj",
    )(x, g, w, b, cos, sin)


def _attn_stages(sink_ref, qt_ref, kp_ref, kc_ref, vtp_ref, vtc_ref, first_block, emit):
    L = WINDOW
    nsub = kc_ref.shape[0] // L
    key = lax.broadcasted_iota(jnp.int32, (2 * L, L), 0)
    t = lax.broadcasted_iota(jnp.int32, (2 * L, L), 1)
    is_prev = key < L
    allowed = (is_prev & (key >= t)) | (~is_prev & (key - L <= t))
    bias = jnp.where(allowed, 0.0, -jnp.inf)
    bias_first = jnp.where(is_prev & first_block, -jnp.inf, bias)
    zeros_q = jnp.zeros((HEAD_DIM, L), BF16)
    heads = range(N_Q_HEADS)
    for r in range(nsub):
        ts = slice(r * L, (r + 1) * L)
        k_prev = kp_ref[...] if r == 0 else kc_ref[(r - 1) * L:r * L, :]
        vt_prev = vtp_ref[...] if r == 0 else vtc_ref[:, (r - 1) * L:r * L]
        k2 = jnp.concatenate([k_prev, kc_ref[ts, :]], axis=0)
        vt2 = jnp.concatenate([vt_prev, vtc_ref[:, ts]], axis=1)
        b_r = bias_first if r == 0 else bias
        q_op = [jnp.concatenate([qt_ref[hd * HEAD_DIM:(hd + 1) * HEAD_DIM, ts], zeros_q][::1 if hd < GQA_GROUP else -1],
                                axis=0) for hd in heads]
        s = [_mm(k2, q_op[hd]) + b_r for hd in heads]
        yield
        m = [jnp.maximum(jnp.max(s[hd], axis=0, keepdims=True), sink_ref[hd]) for hd in heads]
        yield
        p = [jnp.exp(s[hd] - m[hd]) for hd in heads]
        yield
        den = [jnp.sum(p[hd], axis=0, keepdims=True) + jnp.exp(sink_ref[hd] - m[hd]) for hd in heads]
        o = [_mm(vt2[(hd // GQA_GROUP) * HEAD_DIM:(hd // GQA_GROUP + 1) * HEAD_DIM, :], p[hd].astype(BF16)) / den[hd]
             for hd in heads]
        yield
        emit(r, jnp.concatenate(o, axis=0).T)
        yield


ML_STATE_ROWS = ML_DIM + 8


def _mlstm_stages(chains, emit):
    L = ML_CHUNK
    s_idx = lax.broadcasted_iota(jnp.int32, (L, L), 0)
    t_idx = lax.broadcasted_iota(jnp.int32, (L, L), 1)
    causal = s_idx <= t_idx
    n = range(len(chains))
    dlog = [jnp.where(causal, ch["b_r"] + ch["u_c"], -jnp.inf) for ch in chains]
    yield
    qk = [_mm(ch["k"], ch["qt"]) for ch in chains]
    inter = [_mm(ch["state"].astype(BF16), ch["qt"]) for ch in chains]
    yield
    a = [ch["b_r"] + ch["m_prev"] for ch in chains]
    m_t = [jnp.maximum(a[i], jnp.max(dlog[i], axis=0, keepdims=True)) for i in n]
    yield
    sw = [qk[i] * jnp.exp(dlog[i] - m_t[i]) for i in n]
    yield
    sv = [_mm(chains[i]["vt"], sw[i].astype(BF16)) for i in n]
    yield
    hs = []
    for i in n:
        aw = jnp.exp(a[i] - m_t[i])
        num = aw * inter[i][:ML_DIM] + sv[i]
        den = aw * inter[i][ML_DIM:ML_DIM + 1] + jnp.sum(sw[i], axis=0, keepdims=True)
        hh = num / jnp.maximum(jnp.abs(den), jnp.exp(-m_t[i]))
        hs.append(hh.T * chains[i]["og"])
    yield
    pad = jnp.zeros((ML_STATE_ROWS - ML_DIM - 1, L), F32)
    new_state, new_m = [], []
    for i in n:
        ch = chains[i]
        b_end = ch["b_r"][:, L - 1:L]
        g = b_end - ch["b_r"] + ch["ig_r"]
        m_new = jnp.maximum(b_end + ch["m_prev"], jnp.max(g, axis=1, keepdims=True))
        decay = jnp.exp(b_end + ch["m_prev"] - m_new)
        w = jnp.exp(g - m_new)
        v_aug = jnp.concatenate([ch["vt"].astype(F32) * w, w, pad], axis=0).astype(BF16)
        new_state.append(decay * ch["state"] + _mm(v_aug, ch["k"]))
        new_m.append(m_new)
    emit(hs, new_state, new_m)
    yield


DENSE_COLS = 512


def _dense_stages(x, a, hm, p, gmix_ref, wg_ref, bg_ref, wau_ref, wmu_ref, wo_ref,
                  gmlp_ref, wff1_ref, wff2_ref, gple_ref, wpg_ref, wpp_ref, gfin_ref, emit):
    nc = DENSE_COLS
    pieces = lambda n: [slice(c, c + nc) for c in range(0, n, nc)]
    xn = _rms(x, gmix_ref[...]).astype(BF16)
    gates = []
    for cs in pieces(2 * D_MODEL):
        gates.append(jax.nn.sigmoid(_mm(xn, wg_ref[:, cs]) + bg_ref[:, cs]))
        yield
    half = D_MODEL // nc
    mix = []
    for j, cs in enumerate(pieces(D_MODEL)):
        mix.append((gates[j] * _mm(a, wau_ref[:, cs]) + gates[half + j] * _mm(hm, wmu_ref[:, cs])).astype(BF16))
        yield
    mix = jnp.concatenate(mix, axis=1)
    h = []
    for cs in pieces(D_MODEL):
        h.append(x[:, cs] + _mm(mix, wo_ref[:, cs]))
        yield
    h = jnp.concatenate(h, axis=1)
    hn = _rms(h, gmlp_ref[...]).astype(BF16)
    u = []
    for cs in pieces(D_FF):
        u.append(jnp.square(jnp.maximum(_mm(hn, wff1_ref[:, cs]), 0.0)).astype(BF16))
        yield
    u = jnp.concatenate(u, axis=1)
    h2 = []
    for cs in pieces(D_MODEL):
        h2.append(h[:, cs] + _mm(u, wff2_ref[:, cs]))
        yield
    h = jnp.concatenate(h2, axis=1)
    hn = _rms(h, gple_ref[...]).astype(BF16)
    pb = p.astype(BF16)
    h3 = []
    for cs in pieces(D_MODEL):
        h3.append(h[:, cs] + _mm(pb, wpp_ref[:, cs]) * jax.nn.sigmoid(_mm(hn, wpg_ref[:, cs])))
        yield
    emit(_rms(jnp.concatenate(h3, axis=1), gfin_ref[...]))
    yield


def _out_kernel(x_ref, a_ref, hm_ref, p_ref, *rest):
    weights, y_ref = rest[:-1], rest[-1]

    def emit(y):
        y_ref[...] = y

    _weave(_dense_stages(x_ref[...], a_ref[...], hm_ref[...], p_ref[...], *weights, emit=emit))


def _out_proj(x, a, hm, p, weights, tm):
    rows = x.shape[0]
    row = lambda i: (i, 0)
    const = lambda i: (0, 0)
    w_specs = [pl.BlockSpec(w.shape, const, pipeline_mode=pl.Buffered(1)) for w in weights]
    return pl.pallas_call(
        _out_kernel,
        grid=(rows // tm,),
        in_specs=[
            pl.BlockSpec((tm, D_MODEL), row),
            pl.BlockSpec((tm, ATTN_Q), row),
            pl.BlockSpec((tm, ML_W), row),
            pl.BlockSpec((tm, D_PLE), row),
        ] + w_specs,
        out_specs=pl.BlockSpec((tm, D_MODEL), row),
        out_shape=jax.ShapeDtypeStruct((rows, D_MODEL), F32),
        compiler_params=pltpu.CompilerParams(dimension_semantics=("parallel",), vmem_limit_bytes=VMEM_LIMIT),
        name="out_proj",
    )(x, a, hm, p, *weights)


N_FUSED_WEIGHTS = 13
N_PROJ_SLOTS = 9


def _prompt_kernel(sink_ref, x0_ref, cos_ref, sin_ref, x2_ref, p_ref, gmix_ref, wp_ref, bp_ref, *rest, tiles_per_seq):
    weights = rest[:N_FUSED_WEIGHTS]
    y_ref, kl_ref, vl_ref, cst_ref, nst_ref, mst_ref = rest[N_FUSED_WEIGHTS:N_FUSED_WEIGHTS + 6]
    scr = rest[N_FUSED_WEIGHTS + 6:]
    proj_scr, (kprev_scr, vtprev_scr, a_scr, h_scr, c_scr, m_scr) = scr[:N_PROJ_SLOTS], scr[N_PROJ_SLOTS:]
    i = pl.program_id(0)
    n_tiles = pl.num_programs(0) - 2
    t1 = jnp.clip(i - 1, 0, n_tiles - 1)
    tile_in_seq = t1 % tiles_per_seq
    s0, s1 = i % 2, (i + 1) % 2
    L = ML_CHUNK
    tm = x0_ref.shape[0]

    @pl.when(i == 0)
    def _():
        for r in scr:
            r[...] = jnp.zeros_like(r)

    @pl.when(tile_in_seq == 0)
    def _():
        c_scr[...] = jnp.zeros_like(c_scr)
        m_scr[...] = jnp.zeros_like(m_scr)

    def projection():
        outs = [r.at[s0] for r in proj_scr]
        yield from _in_proj_stages(x0_ref[...], gmix_ref, wp_ref, bp_ref, cos_ref, sin_ref,
                                   outs[0], outs[1], outs[2], kl_ref, vl_ref, *outs[3:], feature_major=True)

    def branches():
        qt_ref, kc_ref, vtc_ref, mqt_ref, mk_ref, mvt_ref, og_ref, gif_ref, gt_ref = [r.at[s1] for r in proj_scr]
        slot = s1

        def emit_attn(r, blk):
            a_scr[slot, r * WINDOW:(r + 1) * WINDOW, :] = blk.astype(a_scr.dtype)

        yield from _attn_stages(sink_ref, qt_ref, kprev_scr, kc_ref, vtprev_scr, vtc_ref, tile_in_seq == 0, emit_attn)
        kprev_scr[...] = kc_ref[tm - WINDOW:, :]
        vtprev_scr[...] = vtc_ref[:, tm - WINDOW:]
        hsl = lambda h: slice(h * ML_DIM, (h + 1) * ML_DIM)
        carry = dict(state=[c_scr[h] for h in range(ML_HEADS)], m=[m_scr[h:h + 1, :] for h in range(ML_HEADS)])
        for c in range(tm // L):
            cs = slice(c * L, (c + 1) * L)

            def emit_ml(hs, new_state, new_m, cs=cs):
                for h in range(ML_HEADS):
                    h_scr[slot, cs, hsl(h)] = hs[h].astype(h_scr.dtype)
                carry["state"], carry["m"] = new_state, new_m

            chains = [dict(qt=mqt_ref[0, hsl(h), cs], k=mk_ref[cs, hsl(h)], vt=mvt_ref[0, hsl(h), cs],
                           og=og_ref[cs, hsl(h)], ig_r=gt_ref[0, h:h + 1, cs],
                           b_r=gt_ref[0, ML_HEADS + h:ML_HEADS + h + 1, cs],
                           u_c=gif_ref[cs, 2 * ML_HEADS + h:2 * ML_HEADS + h + 1],
                           state=carry["state"][h], m_prev=carry["m"][h]) for h in range(ML_HEADS)]
            yield from _mlstm_stages(chains, emit_ml)
        for h in range(ML_HEADS):
            c_scr[h] = carry["state"][h]
            m_scr[h:h + 1, :] = carry["m"][h]

    def emit_y(y):
        y_ref[...] = y

    dense = _dense_stages(x2_ref[...], a_scr[s0], h_scr[s0], p_ref[...], *weights, emit=emit_y)
    _weave(branches(), dense, projection())

    @pl.when((tile_in_seq == tiles_per_seq - 1) & (i >= 1) & (i <= n_tiles))
    def _():
        nst_ref[...] = jnp.zeros_like(nst_ref)
        for h in range(ML_HEADS):
            final = c_scr[h]
            cst_ref[0, h] = final[:ML_DIM].T
            nst_ref[0, h:h + 1, :] = final[ML_DIM:ML_DIM + 1]
        mst_ref[0] = m_scr[...]


def _prompt_fused(sinks, x, p, cos, sin, g_mix, w_proj, b_proj, weights, batch, seq, tm):
    assert len(weights) == N_FUSED_WEIGHTS
    rows = batch * seq
    n_tiles = rows // tm
    tps = seq // tm
    t0 = lambda i: jnp.minimum(i, n_tiles - 1)
    t1 = lambda i: jnp.clip(i - 1, 0, n_tiles - 1)
    t2 = lambda i: jnp.maximum(i - 2, 0)
    const = lambda i: (0, 0)
    single = lambda a: pl.BlockSpec(a.shape, const, pipeline_mode=pl.Buffered(1))
    slot = lambda shape, dt: pltpu.VMEM((2,) + shape, dt)
    return pl.pallas_call(
        functools.partial(_prompt_kernel, tiles_per_seq=tps),
        grid=(n_tiles + 2,),
        in_specs=[
            pl.BlockSpec(memory_space=pltpu.SMEM),
            pl.BlockSpec((tm, D_MODEL), lambda i: (t0(i), 0)),
            pl.BlockSpec((tm, LANES), lambda i: (t0(i) % tps, 0)),
            pl.BlockSpec((tm, LANES), lambda i: (t0(i) % tps, 0)),
            pl.BlockSpec((tm, D_MODEL), lambda i: (t2(i), 0)),
            pl.BlockSpec((tm, D_PLE), lambda i: (t2(i), 0)),
            single(g_mix), single(w_proj), single(b_proj),
        ] + [single(w) for w in weights],
        out_specs=(
            pl.BlockSpec((tm, D_MODEL), lambda i: (t2(i), 0)),
            pl.BlockSpec((WINDOW, ATTN_KV), lambda i: (t0(i) // tps, 0)),
            pl.BlockSpec((WINDOW, ATTN_KV), lambda i: (t0(i) // tps, 0)),
            pl.BlockSpec((1, ML_HEADS, ML_DIM, ML_DIM), lambda i: (t1(i) // tps, 0, 0, 0)),
            pl.BlockSpec((1, 8, LANES), lambda i: (t1(i) // tps, 0, 0)),
            pl.BlockSpec((1, 8, LANES), lambda i: (t1(i) // tps, 0, 0)),
        ),
        out_shape=(
            jax.ShapeDtypeStruct((rows, D_MODEL), F32),
            jax.ShapeDtypeStruct((batch * WINDOW, ATTN_KV), F32),
            jax.ShapeDtypeStruct((batch * WINDOW, ATTN_KV), F32),
            jax.ShapeDtypeStruct((batch, ML_HEADS, ML_DIM, ML_DIM), F32),
            jax.ShapeDtypeStruct((batch, 8, LANES), F32),
            jax.ShapeDtypeStruct((batch, 8, LANES), F32),
        ),
        scratch_shapes=[
            slot((ATTN_Q, tm), BF16), slot((tm, ATTN_KV), BF16), slot((ATTN_KV, tm), BF16),
            slot((1, ML_W, tm), BF16), slot((tm, ML_W), BF16), slot((1, ML_W, tm), BF16),
            slot((tm, ML_W), F32), slot((tm, LANES), F32), slot((1, 8, tm), F32),
            pltpu.VMEM((WINDOW, ATTN_KV), BF16), pltpu.VMEM((ATTN_KV, WINDOW), BF16),
            slot((tm, ATTN_Q), BF16), slot((tm, ML_W), BF16),
            pltpu.VMEM((ML_HEADS, ML_STATE_ROWS, ML_DIM), F32), pltpu.VMEM((8, LANES), F32),
        ],
        compiler_params=pltpu.CompilerParams(dimension_semantics=("arbitrary",), vmem_limit_bytes=VMEM_LIMIT),
        name="prompt_fused",
    )(sinks, x, cos, sin, x, p, g_mix, w_proj, b_proj, *weights)


def _attn_decode_kernel(sink_ref, qm_ref, kn_ref, vn_ref, ck_ref, cv_ref, o_ref, ko_ref, vo_ref):
    bt = qm_ref.shape[0]
    qm = qm_ref[...]
    kn = kn_ref[...]
    vn = vn_ref[...]
    ck = ck_ref[...]
    cv = cv_ref[...]
    sink = sink_ref[...]
    s = jnp.einsum('bhc,bck->bhk', qm, ck.astype(BF16), preferred_element_type=F32)
    s_new = jnp.sum(qm.astype(F32) * kn[:, None, :], axis=2, keepdims=True)
    m = jnp.maximum(jnp.maximum(jnp.max(s, axis=2, keepdims=True), s_new), sink)
    p = jnp.exp(s - m)
    p_new = jnp.exp(s_new - m)
    den = jnp.sum(p, axis=2, keepdims=True) + p_new + jnp.exp(sink - m)
    o = jnp.einsum('bhk,bck->bhc', p.astype(BF16), cv.astype(BF16), preferred_element_type=F32)
    o_ref[...] = (o + p_new * vn[:, None, :]) / den
    pad = jnp.zeros((LANES - bt, ATTN_KV), F32)
    kn_t = jnp.concatenate([kn, pad], axis=0).T
    vn_t = jnp.concatenate([vn, pad], axis=0).T
    newest = lax.broadcasted_iota(jnp.int32, (ATTN_KV, WINDOW), 1) == WINDOW - 1
    for b in range(bt):
        ko_ref[b] = jnp.where(newest, kn_t[:, b:b + 1], pltpu.roll(ck[b], WINDOW - 1, 1))
        vo_ref[b] = jnp.where(newest, vn_t[:, b:b + 1], pltpu.roll(cv[b], WINDOW - 1, 1))


def _attn_decode(sinks_col, qmat, k_new, v_new, cache_k, cache_v, bt):
    nbatch = qmat.shape[0]
    b3 = lambda i: (i, 0, 0)
    b2 = lambda i: (i, 0)
    return pl.pallas_call(
        _attn_decode_kernel,
        grid=(nbatch // bt,),
        in_specs=[
            pl.BlockSpec((N_Q_HEADS, 1), lambda i: (0, 0)),
            pl.BlockSpec((bt, N_Q_HEADS, ATTN_KV), b3),
            pl.BlockSpec((bt, ATTN_KV), b2),
            pl.BlockSpec((bt, ATTN_KV), b2),
            pl.BlockSpec((bt, WINDOW, ATTN_KV), b3),
            pl.BlockSpec((bt, WINDOW, ATTN_KV), b3),
        ],
        out_specs=(
            pl.BlockSpec((bt, N_Q_HEADS, ATTN_KV), b3),
            pl.BlockSpec((bt, WINDOW, ATTN_KV), b3),
            pl.BlockSpec((bt, WINDOW, ATTN_KV), b3),
        ),
        out_shape=(
            jax.ShapeDtypeStruct((nbatch, N_Q_HEADS, ATTN_KV), F32),
            jax.ShapeDtypeStruct((nbatch, WINDOW, ATTN_KV), F32),
            jax.ShapeDtypeStruct((nbatch, WINDOW, ATTN_KV), F32),
        ),
        compiler_params=pltpu.CompilerParams(dimension_semantics=("parallel",)),
        name="attn_decode",
    )(sinks_col, qmat, k_new, v_new, cache_k, cache_v)


def _mlstm_decode_kernel(q_ref, k_ref, v_ref, og_ref, gif_ref, c_ref, n_ref, m_ref, h_ref, co_ref, no_ref, mo_ref):
    bt = q_ref.shape[0]
    gif = gif_ref[...]
    ig = gif[:, 0:ML_HEADS]
    lf = gif[:, ML_HEADS:2 * ML_HEADS]
    m_prev = m_ref[...]
    a = lf + m_prev
    m_t = jnp.maximum(a, ig)
    aw = jnp.exp(a - m_t)
    e_i = jnp.exp(ig - m_t)
    floor = jnp.exp(-m_t)
    m_new = jnp.maximum(a, ig)
    decay = jnp.exp(a - m_new)
    w = jnp.exp(ig - m_new)
    mo_ref[...] = m_new
    pad = jnp.zeros((LANES - bt, ML_DIM), F32)
    lane = lax.broadcasted_iota(jnp.int32, (ML_DIM, LANES), 1)
    for h in range(ML_HEADS):
        hs = slice(h * ML_DIM, (h + 1) * ML_DIM)
        q = q_ref[:, hs]
        k = k_ref[:, hs]
        v = v_ref[:, hs]
        n = n_ref[:, h, :]
        qk = jnp.sum(q * k, axis=1, keepdims=True)
        qn = jnp.sum(q * n, axis=1, keepdims=True)
        sw = qk * e_i[:, h:h + 1]
        den = aw[:, h:h + 1] * qn + sw
        inv = 1.0 / jnp.maximum(jnp.abs(den), floor[:, h:h + 1])
        no_ref[:, h, :] = decay[:, h:h + 1] * n + w[:, h:h + 1] * k
        q_b = q.astype(BF16)
        wk_t = jnp.concatenate([w[:, h:h + 1] * k, pad], axis=0).T.astype(BF16)
        v_rows = jnp.concatenate([v, pad], axis=0).astype(BF16)
        nums = []
        for b in range(bt):
            C = c_ref[b, h]
            q_c = _mm(q_b, C.astype(BF16))[b:b + 1, :]
            nums.append(aw[b:b + 1, h:h + 1] * q_c)
            outer = _mm(jnp.where(lane == b, wk_t, jnp.zeros_like(wk_t)), v_rows)
            co_ref[b, h] = decay[b:b + 1, h:h + 1] * C + outer
        num = jnp.concatenate(nums, axis=0) + sw * v
        h_ref[:, hs] = (num * inv * og_ref[:, hs]).astype(h_ref.dtype)


def _mlstm_decode(mq, mk, mv, og, gif, state_c, state_n, state_m, bt):
    nbatch = mq.shape[0]
    b2 = lambda i: (i, 0)
    b3 = lambda i: (i, 0, 0)
    b4 = lambda i: (i, 0, 0, 0)
    return pl.pallas_call(
        _mlstm_decode_kernel,
        grid=(nbatch // bt,),
        in_specs=[
            pl.BlockSpec((bt, ML_W), b2),
            pl.BlockSpec((bt, ML_W), b2),
            pl.BlockSpec((bt, ML_W), b2),
            pl.BlockSpec((bt, ML_W), b2),
            pl.BlockSpec((bt, LANES), b2),
            pl.BlockSpec((bt, ML_HEADS, ML_DIM, ML_DIM), b4),
            pl.BlockSpec((bt, ML_HEADS, ML_DIM), b3),
            pl.BlockSpec((bt, ML_HEADS), b2),
        ],
        out_specs=(
            pl.BlockSpec((bt, ML_W), b2),
            pl.BlockSpec((bt, ML_HEADS, ML_DIM, ML_DIM), b4),
            pl.BlockSpec((bt, ML_HEADS, ML_DIM), b3),
            pl.BlockSpec((bt, ML_HEADS), b2),
        ),
        out_shape=(
            jax.ShapeDtypeStruct((nbatch, ML_W), BF16),
            jax.ShapeDtypeStruct((nbatch, ML_HEADS, ML_DIM, ML_DIM), F32),
            jax.ShapeDtypeStruct((nbatch, ML_HEADS, ML_DIM), F32),
            jax.ShapeDtypeStruct((nbatch, ML_HEADS), F32),
        ),
        compiler_params=pltpu.CompilerParams(dimension_semantics=("parallel",)),
        name="mlstm_decode",
    )(mq, mk, mv, og, gif, state_c, state_n, state_m)


def _rope_tables(pos):
    half = HEAD_DIM // 2
    inv = 1.0 / (ROPE_THETA ** (jnp.arange(half, dtype=F32) / half))
    ang = pos.astype(F32)[:, None] * inv[None, :]
    cos = jnp.cos(ang)
    sin = jnp.sin(ang)
    cos_t = jnp.concatenate([cos, cos, cos, cos], axis=1)
    sin_t = jnp.concatenate([-sin, sin, -sin, sin], axis=1)
    return cos_t, sin_t


def kernel(x_prompt, x_sample, cache_k, cache_v, state_C, state_n, state_m, p_prompt, p_sample, norm_mix, w_in, b_in, attn_sinks, w_attn_up, w_ml_up, w_o, norm_mlp, w_ff1, w_ff2, norm_ple, w_ple_gate, w_ple_proj, norm_final):
    assert w_in.shape[0] == 1, "single layer"
    batch, seq, _ = x_prompt.shape
    nbatch = x_sample.shape[0]
    assert x_sample.shape[1] == 1

    w_in0 = w_in[0]
    b_in0 = b_in[0]
    w_proj, w_gate, w_au, w_mu, w_out, w_f1, w_f2, w_pg, w_pp = _weight_cast(
        w_in0.T, (w_attn_up[0], w_ml_up[0], w_o[0], w_ff1[0], w_ff2[0], w_ple_gate[0], w_ple_proj[0]))
    b_proj = b_in0[:N_PROJ][None, :]
    b_gate = b_in0[OFF_GA:][None, :]
    g_mix = norm_mix[0][None, :]
    out_weights = (
        g_mix, w_gate, b_gate, w_au, w_mu, w_out,
        norm_mlp[0][None, :], w_f1, w_f2,
        norm_ple[0][None, :], w_pg, w_pp, norm_final[None, :],
    )
    sinks = attn_sinks[0]

    cos_p, sin_p = _rope_tables(jnp.arange(seq, dtype=jnp.int32))
    xp = x_prompt.reshape(batch * seq, D_MODEL)
    y_prompt, k_last, v_last, cst, nst, mst = _prompt_fused(
        sinks, xp, p_prompt[0].reshape(batch * seq, D_PLE), cos_p, sin_p, g_mix, w_proj, b_proj, out_weights,
        batch, seq, tm=256)
    y_prompt = y_prompt.reshape(batch, seq, D_MODEL)
    k_prompt = k_last.reshape(1, batch, WINDOW, N_KV_HEADS, HEAD_DIM)
    v_prompt = v_last.reshape(1, batch, WINDOW, N_KV_HEADS, HEAD_DIM)
    c_prompt = cst[None]
    n_prompt = nst[:, :ML_HEADS, :][None]
    m_prompt = mst[:, :ML_HEADS, 0][None]

    cos_s, sin_s = _rope_tables(jnp.full((nbatch,), PAST_LEN, dtype=jnp.int32))
    xs = x_sample.reshape(nbatch, D_MODEL)
    qs, _, _, ks, vs, mqs, mks, mvs, ogs, gifs, _ = _in_proj(xs, g_mix, w_proj, b_proj, cos_s, sin_s, tm=nbatch,
                                                          m_dtype=F32, feature_major=False)
    q4 = qs.reshape(nbatch, N_KV_HEADS, GQA_GROUP, 1, HEAD_DIM)
    sel = jnp.eye(N_KV_HEADS, dtype=BF16)[None, :, None, :, None]
    qmat = (q4 * sel).reshape(nbatch, N_Q_HEADS, ATTN_KV)
    to_fp = lambda c: c[0].transpose(0, 2, 3, 1).reshape(nbatch, ATTN_KV, WINDOW)
    from_fp = lambda c: c.reshape(nbatch, N_KV_HEADS, HEAD_DIM, WINDOW).transpose(0, 3, 1, 2)[None]
    o_full, k_s, v_s = _attn_decode(sinks[:, None], qmat, ks, vs, to_fp(cache_k), to_fp(cache_v), bt=8)
    o5 = o_full.reshape(nbatch, N_KV_HEADS, GQA_GROUP, N_KV_HEADS, HEAD_DIM)
    a_s = jnp.stack([o5[:, kv, :, kv, :] for kv in range(N_KV_HEADS)], axis=1).reshape(nbatch, ATTN_Q).astype(BF16)
    h_s, c_s, n_s, m_s = _mlstm_decode(mqs, mks, mvs, ogs, gifs, state_C[0], state_n[0], state_m[0], bt=8)
    y_sample = _out_proj(xs, a_s, h_s, p_sample[0].reshape(nbatch, D_PLE), out_weights, tm=nbatch)
    y_sample = y_sample.reshape(nbatch, 1, D_MODEL)
    k_sample = from_fp(k_s)
    v_sample = from_fp(v_s)

    return (y_prompt, y_sample, k_prompt, v_prompt, c_prompt, n_prompt, m_prompt,
            k_sample, v_sample, c_s[None], n_s[None], m_s[None])
```

```python
import functools

import numpy as np
import jax
import jax.numpy as jnp
from jax import lax
from jax.experimental import pallas as pl
from jax.experimental.pallas import tpu as pltpu

F32 = jnp.float32
BF16 = jnp.bfloat16

D_MODEL = 1024
HEAD_DIM = 64
N_Q_HEADS = 8
N_KV_HEADS = 2
GQA_GROUP = N_Q_HEADS // N_KV_HEADS
WINDOW = 128
ROPE_THETA = 10000.0
ML_HEADS = 4
ML_DIM = 128
ML_CHUNK = 128
D_FF = 4 * D_MODEL
D_PLE = 256
EPS = 1e-6
PAST_LEN = 16384

ATTN_Q = N_Q_HEADS * HEAD_DIM
ATTN_KV = N_KV_HEADS * HEAD_DIM
ML_W = ML_HEADS * ML_DIM
LANES = 128

OFF_AQ = 0
OFF_AK = OFF_AQ + ATTN_Q
OFF_AV = OFF_AK + ATTN_KV
OFF_MQ = OFF_AV + ATTN_KV
OFF_MK = OFF_MQ + ML_W
OFF_MV = OFF_MK + ML_W
OFF_MO = OFF_MV + ML_W
OFF_MI = OFF_MO + ML_W
OFF_MF = OFF_MI + ML_HEADS
OFF_GA = OFF_MF + ML_HEADS
OFF_GM = OFF_GA + D_MODEL
D_IN = OFF_GM + D_MODEL
N_PROJ = OFF_MI + LANES

VMEM_LIMIT = 60 * 1024 * 1024

def _rms(x, g):
    r = lax.rsqrt(jnp.mean(x * x, axis=-1, keepdims=True) + EPS)
    return (x * r) * g


def _mm(a, b):
    return jnp.dot(a, b, preferred_element_type=F32)


def _mm_nt(a, b):
    return lax.dot_general(a, b, (((1,), (1,)), ((), ())), preferred_element_type=F32)


def _mm_tn(a, b):
    return lax.dot_general(a, b, (((0,), (0,)), ((), ())), preferred_element_type=F32)


def _log_sigmoid(x):
    return jnp.minimum(x, 0.0) - jnp.log1p(jnp.exp(-jnp.abs(x)))


def _weave(*streams, delays=None):
    delays = delays or [0] * len(streams)
    live = list(zip(streams, delays))
    rnd = 0
    while live:
        for g, d in list(live):
            if rnd >= d:
                try:
                    next(g)
                except StopIteration:
                    live.remove((g, d))
        rnd += 1


WEIGHT_ROW_SPLIT = 8


def _weight_cast_kernel(*refs):
    n_plain = (len(refs) - 3) // 2
    w_in_t_ref, plain_in = refs[0], refs[1:1 + n_plain]
    w_proj_ref, w_gate_ref, plain_out = refs[1 + n_plain], refs[2 + n_plain], refs[3 + n_plain:]
    for g in range(N_PROJ // LANES):
        w_proj_ref[:, g * LANES:(g + 1) * LANES] = w_in_t_ref[g * LANES:(g + 1) * LANES, :].T.astype(BF16)
    for g in range(2 * D_MODEL // LANES):
        w_gate_ref[:, g * LANES:(g + 1) * LANES] = w_in_t_ref[OFF_GA + g * LANES:OFF_GA + (g + 1) * LANES, :].T.astype(BF16)
    for src, dst in zip(plain_in, plain_out):
        dst[...] = src[...].astype(BF16)


def _weight_cast(w_in_t, plain):
    split = WEIGHT_ROW_SPLIT
    assert D_MODEL // split == LANES
    rows_blk = lambda w: pl.BlockSpec((w.shape[0] // split, w.shape[1]), lambda i: (i, 0))
    out_shapes = (jax.ShapeDtypeStruct((D_MODEL, N_PROJ), BF16), jax.ShapeDtypeStruct((D_MODEL, 2 * D_MODEL), BF16),
                  *(jax.ShapeDtypeStruct(w.shape, BF16) for w in plain))
    return pl.pallas_call(
        _weight_cast_kernel,
        grid=(split,),
        in_specs=[pl.BlockSpec((D_IN, LANES), lambda i: (0, i))] + [rows_blk(w) for w in plain],
        out_specs=tuple(rows_blk(s) for s in out_shapes),
        out_shape=out_shapes,
        compiler_params=pltpu.CompilerParams(dimension_semantics=("parallel",), vmem_limit_bytes=VMEM_LIMIT),
        name="weight_cast",
    )(w_in_t, *plain)


def _in_proj_stages(x, g_ref, w_ref, b_ref, cos_ref, sin_ref,
                    q_ref, k_ref, v_ref, kl_ref, vl_ref, mq_ref, mk_ref, mv_ref, og_ref, gif_ref, gt_ref,
                    *, feature_major):
    xn = _rms(x, g_ref[...]).astype(BF16)
    tm = x.shape[0]

    def proj(lo, hi):
        return _mm(xn, w_ref[:, lo:hi]) + b_ref[:, lo:hi]

    cos = cos_ref[...]
    sin = sin_ref[...]
    lane = lax.broadcasted_iota(jnp.int32, (1, LANES), 1)
    first_half = (lane % HEAD_DIM) < (HEAD_DIM // 2)

    def rope(z):
        partner = jnp.where(first_half, pltpu.roll(z, LANES - HEAD_DIM // 2, 1), pltpu.roll(z, HEAD_DIM // 2, 1))
        return z * cos + partner * sin

    zq = proj(OFF_AQ, OFF_AK)
    for c in range(ATTN_Q // LANES):
        sl = slice(c * LANES, (c + 1) * LANES)
        qc = rope(zq[:, sl]) * (HEAD_DIM ** -0.5)
        if feature_major:
            q_ref[sl, :] = qc.T.astype(q_ref.dtype)
        else:
            q_ref[:, sl] = qc.astype(q_ref.dtype)
    yield
    zkv = proj(OFF_AK, OFF_MQ)
    k = rope(zkv[:, :ATTN_KV])
    v = zkv[:, ATTN_KV:]
    k_ref[...] = k.astype(k_ref.dtype)
    v_ref[...] = (v.T if feature_major else v).astype(v_ref.dtype)
    kl_ref[...] = k[tm - WINDOW:, :]
    vl_ref[...] = v[tm - WINDOW:, :]
    yield
    for z_off, dst in ((OFF_MQ, mq_ref), (OFF_MV, mv_ref)):
        z = proj(z_off, z_off + ML_W)
        if feature_major:
            for h in range(ML_HEADS):
                hs = slice(h * ML_DIM, (h + 1) * ML_DIM)
                dst[0, hs, :] = z[:, hs].T.astype(dst.dtype)
        else:
            dst[...] = z.astype(dst.dtype)
        yield
    mk_ref[...] = (proj(OFF_MK, OFF_MV) * (ML_DIM ** -0.5)).astype(mk_ref.dtype)
    yield
    og_ref[...] = jax.nn.sigmoid(proj(OFF_MO, OFF_MI))
    yield
    zg = proj(OFF_MI, N_PROJ)
    gif = jnp.where(lane < ML_HEADS, zg, jnp.where(lane < 2 * ML_HEADS, _log_sigmoid(zg), 0.0))
    if feature_major:
        L = ML_CHUNK
        upper = (lax.broadcasted_iota(jnp.int32, (L, L), 0) <= lax.broadcasted_iota(jnp.int32, (L, L), 1)).astype(F32)
        row8 = lax.broadcasted_iota(jnp.int32, (8, 1), 0)
        fill = jnp.zeros((L - 16, L), F32)
        for c in range(tm // L):
            cs = slice(c * L, (c + 1) * L)
            g_rows = gif[cs, :].T[0:8, :]
            cum = jnp.dot(g_rows, upper, precision=lax.Precision.HIGHEST, preferred_element_type=F32)
            rows = jnp.where(row8 < ML_HEADS, g_rows, cum)
            gt_ref[0, :, cs] = rows
            diff = rows - pltpu.roll(rows, ML_HEADS, 0)
            gif_ref[cs, :] = jnp.concatenate([rows, diff, fill], axis=0).T
    else:
        gif_ref[...] = gif
        gt_ref[0] = jnp.zeros(gt_ref.shape[1:], F32)
    yield


def _in_proj_kernel(x_ref, *refs, feature_major):
    _weave(_in_proj_stages(x_ref[...], *refs, feature_major=feature_major))


def _in_proj(x, g, w, b, cos, sin, tm, m_dtype, feature_major):
    rows = x.shape[0]
    seq = cos.shape[0]
    tps = seq // tm
    nseq = rows // seq
    row = lambda i: (i, 0)
    const = lambda i: (0, 0)
    pos = lambda i: (i % tps, 0)
    last = lambda i: (i // tps, 0)
    row_out = lambda n, dt: (jax.ShapeDtypeStruct((rows, n), dt), pl.BlockSpec((tm, n), row))
    col_out = lambda n, dt: (jax.ShapeDtypeStruct((n, rows), dt), pl.BlockSpec((n, tm), lambda i: (0, i)))
    att_out = col_out if feature_major else row_out
    seq_out = lambda n, dt: (jax.ShapeDtypeStruct((nseq, n, seq), dt),
                             pl.BlockSpec((1, n, tm), lambda i: (i // tps, 0, i % tps)))
    ml_out = seq_out if feature_major else row_out
    outs = (
        att_out(ATTN_Q, BF16),
        row_out(ATTN_KV, BF16),
        att_out(ATTN_KV, BF16),
        (jax.ShapeDtypeStruct((nseq * WINDOW, ATTN_KV), F32), pl.BlockSpec((WINDOW, ATTN_KV), last)),
        (jax.ShapeDtypeStruct((nseq * WINDOW, ATTN_KV), F32), pl.BlockSpec((WINDOW, ATTN_KV), last)),
        ml_out(ML_W, m_dtype),
        row_out(ML_W, m_dtype),
        ml_out(ML_W, m_dtype),
        row_out(ML_W, F32),
        row_out(LANES, F32),
        seq_out(8, F32),
    )
    return pl.pallas_call(
        functools.partial(_in_proj_kernel, feature_major=feature_major),
        grid=(rows // tm,),
        in_specs=[
            pl.BlockSpec((tm, D_MODEL), row),
            pl.BlockSpec((1, D_MODEL), const),
            pl.BlockSpec((D_MODEL, N_PROJ), const),
            pl.BlockSpec((1, N_PROJ), const),
            pl.BlockSpec((tm, LANES), pos),
            pl.BlockSpec((tm, LANES), pos),
        ],
        out_specs=tuple(o[1] for o in outs),
        out_shape=tuple(o[0] for o in outs),
        compiler_params=pltpu.CompilerParams(dimension_semantics=("arbitrary",), vmem_limit_bytes=VMEM_LIMIT),
        name="in_proj",
    )(x, g, w, b, cos, sin)


def _attn_stages(sink_ref, qt_ref, kp_ref, kc_ref, vtp_ref, vtc_ref, first_block, emit):
    L = WINDOW
    nsub = kc_ref.shape[0] // L
    key = lax.broadcasted_iota(jnp.int32, (2 * L, L), 0)
    t = lax.broadcasted_iota(jnp.int32, (2 * L, L), 1)
    is_prev = key < L
    allowed = (is_prev & (key >= t)) | (~is_prev & (key - L <= t))
    bias = jnp.where(allowed, 0.0, -jnp.inf)
    bias_first = jnp.where(is_prev & first_block, -jnp.inf, bias)
    zeros_q = jnp.zeros((HEAD_DIM, L), BF16)
    heads = range(N_Q_HEADS)
    for r in range(nsub):
        ts = slice(r * L, (r + 1) * L)
        k_prev = kp_ref[...] if r == 0 else kc_ref[(r - 1) * L:r * L, :]
        vt_prev = vtp_ref[...] if r == 0 else vtc_ref[:, (r - 1) * L:r * L]
        k2 = jnp.concatenate([k_prev, kc_ref[ts, :]], axis=0)
        vt2 = jnp.concatenate([vt_prev, vtc_ref[:, ts]], axis=1)
        b_r = bias_first if r == 0 else bias
        q_op = [jnp.concatenate([qt_ref[hd * HEAD_DIM:(hd + 1) * HEAD_DIM, ts], zeros_q][::1 if hd < GQA_GROUP else -1],
                                axis=0) for hd in heads]
        s = [_mm(k2, q_op[hd]) + b_r for hd in heads]
        yield
        m = [jnp.maximum(jnp.max(s[hd], axis=0, keepdims=True), sink_ref[hd]) for hd in heads]
        yield
        p = [jnp.exp(s[hd] - m[hd]) for hd in heads]
        yield
        den = [jnp.sum(p[hd], axis=0, keepdims=True) + jnp.exp(sink_ref[hd] - m[hd]) for hd in heads]
        o = [_mm(vt2[(hd // GQA_GROUP) * HEAD_DIM:(hd // GQA_GROUP + 1) * HEAD_DIM, :], p[hd].astype(BF16)) / den[hd]
             for hd in heads]
        yield
        emit(r, jnp.concatenate(o, axis=0).T)
        yield


ML_STATE_ROWS = ML_DIM + 8


def _mlstm_stages(chains, emit):
    L = ML_CHUNK
    s_idx = lax.broadcasted_iota(jnp.int32, (L, L), 0)
    t_idx = lax.broadcasted_iota(jnp.int32, (L, L), 1)
    causal = s_idx <= t_idx
    n = range(len(chains))
    dlog = [jnp.where(causal, ch["b_r"] + ch["u_c"], -jnp.inf) for ch in chains]
    yield
    qk = [_mm(ch["k"], ch["qt"]) for ch in chains]
    inter = [_mm(ch["state"].astype(BF16), ch["qt"]) for ch in chains]
    yield
    a = [ch["b_r"] + ch["m_prev"] for ch in chains]
    m_t = [jnp.maximum(a[i], jnp.max(dlog[i], axis=0, keepdims=True)) for i in n]
    yield
    sw = [qk[i] * jnp.exp(dlog[i] - m_t[i]) for i in n]
    yield
    sv = [_mm(chains[i]["vt"], sw[i].astype(BF16)) for i in n]
    yield
    hs = []
    for i in n:
        aw = jnp.exp(a[i] - m_t[i])
        num = aw * inter[i][:ML_DIM] + sv[i]
        den = aw * inter[i][ML_DIM:ML_DIM + 1] + jnp.sum(sw[i], axis=0, keepdims=True)
        hh = num / jnp.maximum(jnp.abs(den), jnp.exp(-m_t[i]))
        hs.append(hh.T * chains[i]["og"])
    yield
    pad = jnp.zeros((ML_STATE_ROWS - ML_DIM - 1, L), F32)
    new_state, new_m = [], []
    for i in n:
        ch = chains[i]
        b_end = ch["b_r"][:, L - 1:L]
        g = b_end - ch["b_r"] + ch["ig_r"]
        m_new = jnp.maximum(b_end + ch["m_prev"], jnp.max(g, axis=1, keepdims=True))
        decay = jnp.exp(b_end + ch["m_prev"] - m_new)
        w = jnp.exp(g - m_new)
        v_aug = jnp.concatenate([ch["vt"].astype(F32) * w, w, pad], axis=0).astype(BF16)
        new_state.append(decay * ch["state"] + _mm(v_aug, ch["k"]))
        new_m.append(m_new)
    emit(hs, new_state, new_m)
    yield


DENSE_COLS = 512


def _dense_stages(x, a, hm, p, gmix_ref, wg_ref, bg_ref, wau_ref, wmu_ref, wo_ref,
                  gmlp_ref, wff1_ref, wff2_ref, gple_ref, wpg_ref, wpp_ref, gfin_ref, emit):
    nc = DENSE_COLS
    pieces = lambda n: [slice(c, c + nc) for c in range(0, n, nc)]
    xn = _rms(x, gmix_ref[...]).astype(BF16)
    gates = []
    for cs in pieces(2 * D_MODEL):
        gates.append(jax.nn.sigmoid(_mm(xn, wg_ref[:, cs]) + bg_ref[:, cs]))
        yield
    half = D_MODEL // nc
    mix = []
    for j, cs in enumerate(pieces(D_MODEL)):
        mix.append((gates[j] * _mm(a, wau_ref[:, cs]) + gates[half + j] * _mm(hm, wmu_ref[:, cs])).astype(BF16))
        yield
    mix = jnp.concatenate(mix, axis=1)
    h = []
    for cs in pieces(D_MODEL):
        h.append(x[:, cs] + _mm(mix, wo_ref[:, cs]))
        yield
    h = jnp.concatenate(h, axis=1)
    hn = _rms(h, gmlp_ref[...]).astype(BF16)
    u = []
    for cs in pieces(D_FF):
        u.append(jnp.square(jnp.maximum(_mm(hn, wff1_ref[:, cs]), 0.0)).astype(BF16))
        yield
    u = jnp.concatenate(u, axis=1)
    h2 = []
    for cs in pieces(D_MODEL):
        h2.append(h[:, cs] + _mm(u, wff2_ref[:, cs]))
        yield
    h = jnp.concatenate(h2, axis=1)
    hn = _rms(h, gple_ref[...]).astype(BF16)
    pb = p.astype(BF16)
    h3 = []
    for cs in pieces(D_MODEL):
        h3.append(h[:, cs] + _mm(pb, wpp_ref[:, cs]) * jax.nn.sigmoid(_mm(hn, wpg_ref[:, cs])))
        yield
    emit(_rms(jnp.concatenate(h3, axis=1), gfin_ref[...]))
    yield


def _out_kernel(x_ref, a_ref, hm_ref, p_ref, *rest):
    weights, y_ref = rest[:-1], rest[-1]

    def emit(y):
        y_ref[...] = y

    _weave(_dense_stages(x_ref[...], a_ref[...], hm_ref[...], p_ref[...], *weights, emit=emit))


def _out_proj(x, a, hm, p, weights, tm):
    rows = x.shape[0]
    row = lambda i: (i, 0)
    const = lambda i: (0, 0)
    w_specs = [pl.BlockSpec(w.shape, const, pipeline_mode=pl.Buffered(1)) for w in weights]
    return pl.pallas_call(
        _out_kernel,
        grid=(rows // tm,),
        in_specs=[
            pl.BlockSpec((tm, D_MODEL), row),
            pl.BlockSpec((tm, ATTN_Q), row),
            pl.BlockSpec((tm, ML_W), row),
            pl.BlockSpec((tm, D_PLE), row),
        ] + w_specs,
        out_specs=pl.BlockSpec((tm, D_MODEL), row),
        out_shape=jax.ShapeDtypeStruct((rows, D_MODEL), F32),
        compiler_params=pltpu.CompilerParams(dimension_semantics=("parallel",), vmem_limit_bytes=VMEM_LIMIT),
        name="out_proj",
    )(x, a, hm, p, *weights)


N_FUSED_WEIGHTS = 13
N_PROJ_SLOTS = 9
PROJ_DELAY = 0


def _prompt_kernel(sink_ref, x0_ref, cos_ref, sin_ref, x2_ref, p_ref, gmix_ref, wp_ref, bp_ref, *rest, tiles_per_seq):
    weights = rest[:N_FUSED_WEIGHTS]
    y_ref, kl_ref, vl_ref, cst_ref, nst_ref, mst_ref = rest[N_FUSED_WEIGHTS:N_FUSED_WEIGHTS + 6]
    scr = rest[N_FUSED_WEIGHTS + 6:]
    proj_scr, (kprev_scr, vtprev_scr, a_scr, h_scr, c_scr, m_scr) = scr[:N_PROJ_SLOTS], scr[N_PROJ_SLOTS:]
    i = pl.program_id(0)
    n_tiles = pl.num_programs(0) - 2
    t1 = jnp.clip(i - 1, 0, n_tiles - 1)
    tile_in_seq = t1 % tiles_per_seq
    s0, s1 = i % 2, (i + 1) % 2
    L = ML_CHUNK
    tm = x0_ref.shape[0]

    @pl.when(i == 0)
    def _():
        for r in scr:
            r[...] = jnp.zeros_like(r)

    @pl.when(tile_in_seq == 0)
    def _():
        c_scr[...] = jnp.zeros_like(c_scr)
        m_scr[...] = jnp.zeros_like(m_scr)

    def projection():
        outs = [r.at[s0] for r in proj_scr]
        yield from _in_proj_stages(x0_ref[...], gmix_ref, wp_ref, bp_ref, cos_ref, sin_ref,
                                   outs[0], outs[1], outs[2], kl_ref, vl_ref, *outs[3:], feature_major=True)

    def branches():
        qt_ref, kc_ref, vtc_ref, mqt_ref, mk_ref, mvt_ref, og_ref, gif_ref, gt_ref = [r.at[s1] for r in proj_scr]
        slot = s1

        def emit_attn(r, blk):
            a_scr[slot, r * WINDOW:(r + 1) * WINDOW, :] = blk.astype(a_scr.dtype)

        yield from _attn_stages(sink_ref, qt_ref, kprev_scr, kc_ref, vtprev_scr, vtc_ref, tile_in_seq == 0, emit_attn)
        kprev_scr[...] = kc_ref[tm - WINDOW:, :]
        vtprev_scr[...] = vtc_ref[:, tm - WINDOW:]
        hsl = lambda h: slice(h * ML_DIM, (h + 1) * ML_DIM)
        carry = dict(state=[c_scr[h] for h in range(ML_HEADS)], m=[m_scr[h:h + 1, :] for h in range(ML_HEADS)])
        for c in range(tm // L):
            cs = slice(c * L, (c + 1) * L)

            def emit_ml(hs, new_state, new_m, cs=cs):
                for h in range(ML_HEADS):
                    h_scr[slot, cs, hsl(h)] = hs[h].astype(h_scr.dtype)
                carry["state"], carry["m"] = new_state, new_m

            chains = [dict(qt=mqt_ref[0, hsl(h), cs], k=mk_ref[cs, hsl(h)], vt=mvt_ref[0, hsl(h), cs],
                           og=og_ref[cs, hsl(h)], ig_r=gt_ref[0, h:h + 1, cs],
                           b_r=gt_ref[0, ML_HEADS + h:ML_HEADS + h + 1, cs],
                           u_c=gif_ref[cs, 2 * ML_HEADS + h:2 * ML_HEADS + h + 1],
                           state=carry["state"][h], m_prev=carry["m"][h]) for h in range(ML_HEADS)]
            yield from _mlstm_stages(chains, emit_ml)
        for h in range(ML_HEADS):
            c_scr[h] = carry["state"][h]
            m_scr[h:h + 1, :] = carry["m"][h]

    def dense():
        def emit_y(y):
            y_ref[...] = y

        return _dense_stages(x2_ref[...], a_scr[s0], h_scr[s0], p_ref[...], *weights, emit=emit_y)

    @pl.when(i < 2)
    def _():
        _weave(branches(), projection())

    @pl.when((i >= 2) & (i < n_tiles))
    def _():
        _weave(branches(), dense(), projection())

    @pl.when(i >= n_tiles)
    def _():
        _weave(branches(), dense())

    @pl.when((tile_in_seq == tiles_per_seq - 1) & (i >= 1) & (i <= n_tiles))
    def _():
        nst_ref[...] = jnp.zeros_like(nst_ref)
        for h in range(ML_HEADS):
            final = c_scr[h]
            cst_ref[0, h] = final[:ML_DIM].T
            nst_ref[0, h:h + 1, :] = final[ML_DIM:ML_DIM + 1]
        mst_ref[0] = m_scr[...]


def _prompt_fused(sinks, x, p, cos, sin, g_mix, w_proj, b_proj, weights, batch, seq, tm):
    assert len(weights) == N_FUSED_WEIGHTS
    rows = batch * seq
    n_tiles = rows // tm
    tps = seq // tm
    t0 = lambda i: jnp.minimum(i, n_tiles - 1)
    t1 = lambda i: jnp.clip(i - 1, 0, n_tiles - 1)
    t2 = lambda i: jnp.maximum(i - 2, 0)
    const = lambda i: (0, 0)
    single = lambda a: pl.BlockSpec(a.shape, const, pipeline_mode=pl.Buffered(1))
    slot = lambda shape, dt: pltpu.VMEM((2,) + shape, dt)
    return pl.pallas_call(
        functools.partial(_prompt_kernel, tiles_per_seq=tps),
        grid=(n_tiles + 2,),
        in_specs=[
            pl.BlockSpec(memory_space=pltpu.SMEM),
            pl.BlockSpec((tm, D_MODEL), lambda i: (t0(i), 0)),
            pl.BlockSpec((tm, LANES), lambda i: (t0(i) % tps, 0)),
            pl.BlockSpec((tm, LANES), lambda i: (t0(i) % tps, 0)),
            pl.BlockSpec((tm, D_MODEL), lambda i: (t2(i), 0)),
            pl.BlockSpec((tm, D_PLE), lambda i: (t2(i), 0)),
            single(g_mix), single(w_proj), single(b_proj),
        ] + [single(w) for w in weights],
        out_specs=(
            pl.BlockSpec((tm, D_MODEL), lambda i: (t2(i), 0)),
            pl.BlockSpec((WINDOW, ATTN_KV), lambda i: (t0(i) // tps, 0)),
            pl.BlockSpec((WINDOW, ATTN_KV), lambda i: (t0(i) // tps, 0)),
            pl.BlockSpec((1, ML_HEADS, ML_DIM, ML_DIM), lambda i: (t1(i) // tps, 0, 0, 0)),
            pl.BlockSpec((1, 8, LANES), lambda i: (t1(i) // tps, 0, 0)),
            pl.BlockSpec((1, 8, LANES), lambda i: (t1(i) // tps, 0, 0)),
        ),
        out_shape=(
            jax.ShapeDtypeStruct((rows, D_MODEL), F32),
            jax.ShapeDtypeStruct((batch * WINDOW, ATTN_KV), F32),
            jax.ShapeDtypeStruct((batch * WINDOW, ATTN_KV), F32),
            jax.ShapeDtypeStruct((batch, ML_HEADS, ML_DIM, ML_DIM), F32),
            jax.ShapeDtypeStruct((batch, 8, LANES), F32),
            jax.ShapeDtypeStruct((batch, 8, LANES), F32),
        ),
        scratch_shapes=[
            slot((ATTN_Q, tm), BF16), slot((tm, ATTN_KV), BF16), slot((ATTN_KV, tm), BF16),
            slot((1, ML_W, tm), BF16), slot((tm, ML_W), BF16), slot((1, ML_W, tm), BF16),
            slot((tm, ML_W), F32), slot((tm, LANES), F32), slot((1, 8, tm), F32),
            pltpu.VMEM((WINDOW, ATTN_KV), BF16), pltpu.VMEM((ATTN_KV, WINDOW), BF16),
            slot((tm, ATTN_Q), BF16), slot((tm, ML_W), BF16),
            pltpu.VMEM((ML_HEADS, ML_STATE_ROWS, ML_DIM), F32), pltpu.VMEM((8, LANES), F32),
        ],
        compiler_params=pltpu.CompilerParams(dimension_semantics=("arbitrary",), vmem_limit_bytes=VMEM_LIMIT),
        name="prompt_fused",
    )(sinks, x, cos, sin, x, p, g_mix, w_proj, b_proj, *weights)


def _attn_decode_kernel(sink_ref, qm_ref, kn_ref, vn_ref, ck_ref, cv_ref, o_ref, ko_ref, vo_ref):
    bt = qm_ref.shape[0]
    qm = qm_ref[...]
    kn = kn_ref[...]
    vn = vn_ref[...]
    ck = ck_ref[...]
    cv = cv_ref[...]
    sink = sink_ref[...]
    s = jnp.einsum('bhc,bck->bhk', qm, ck.astype(BF16), preferred_element_type=F32)
    s_new = jnp.sum(qm.astype(F32) * kn[:, None, :], axis=2, keepdims=True)
    m = jnp.maximum(jnp.maximum(jnp.max(s, axis=2, keepdims=True), s_new), sink)
    p = jnp.exp(s - m)
    p_new = jnp.exp(s_new - m)
    den = jnp.sum(p, axis=2, keepdims=True) + p_new + jnp.exp(sink - m)
    o = jnp.einsum('bhk,bck->bhc', p.astype(BF16), cv.astype(BF16), preferred_element_type=F32)
    o_ref[...] = (o + p_new * vn[:, None, :]) / den
    pad = jnp.zeros((LANES - bt, ATTN_KV), F32)
    kn_t = jnp.concatenate([kn, pad], axis=0).T
    vn_t = jnp.concatenate([vn, pad], axis=0).T
    newest = lax.broadcasted_iota(jnp.int32, (ATTN_KV, WINDOW), 1) == WINDOW - 1
    for b in range(bt):
        ko_ref[b] = jnp.where(newest, kn_t[:, b:b + 1], pltpu.roll(ck[b], WINDOW - 1, 1))
        vo_ref[b] = jnp.where(newest, vn_t[:, b:b + 1], pltpu.roll(cv[b], WINDOW - 1, 1))


def _attn_decode(sinks_col, qmat, k_new, v_new, cache_k, cache_v, bt):
    nbatch = qmat.shape[0]
    b3 = lambda i: (i, 0, 0)
    b2 = lambda i: (i, 0)
    return pl.pallas_call(
        _attn_decode_kernel,
        grid=(nbatch // bt,),
        in_specs=[
            pl.BlockSpec((N_Q_HEADS, 1), lambda i: (0, 0)),
            pl.BlockSpec((bt, N_Q_HEADS, ATTN_KV), b3),
            pl.BlockSpec((bt, ATTN_KV), b2),
            pl.BlockSpec((bt, ATTN_KV), b2),
            pl.BlockSpec((bt, WINDOW, ATTN_KV), b3),
            pl.BlockSpec((bt, WINDOW, ATTN_KV), b3),
        ],
        out_specs=(
            pl.BlockSpec((bt, N_Q_HEADS, ATTN_KV), b3),
            pl.BlockSpec((bt, WINDOW, ATTN_KV), b3),
            pl.BlockSpec((bt, WINDOW, ATTN_KV), b3),
        ),
        out_shape=(
            jax.ShapeDtypeStruct((nbatch, N_Q_HEADS, ATTN_KV), F32),
            jax.ShapeDtypeStruct((nbatch, WINDOW, ATTN_KV), F32),
            jax.ShapeDtypeStruct((nbatch, WINDOW, ATTN_KV), F32),
        ),
        compiler_params=pltpu.CompilerParams(dimension_semantics=("parallel",)),
        name="attn_decode",
    )(sinks_col, qmat, k_new, v_new, cache_k, cache_v)


def _mlstm_decode_kernel(q_ref, k_ref, v_ref, og_ref, gif_ref, c_ref, n_ref, m_ref, h_ref, co_ref, no_ref, mo_ref):
    bt = q_ref.shape[0]
    gif = gif_ref[...]
    ig = gif[:, 0:ML_HEADS]
    lf = gif[:, ML_HEADS:2 * ML_HEADS]
    m_prev = m_ref[...]
    a = lf + m_prev
    m_t = jnp.maximum(a, ig)
    aw = jnp.exp(a - m_t)
    e_i = jnp.exp(ig - m_t)
    floor = jnp.exp(-m_t)
    m_new = jnp.maximum(a, ig)
    decay = jnp.exp(a - m_new)
    w = jnp.exp(ig - m_new)
    mo_ref[...] = m_new
    pad = jnp.zeros((LANES - bt, ML_DIM), F32)
    lane = lax.broadcasted_iota(jnp.int32, (ML_DIM, LANES), 1)
    for h in range(ML_HEADS):
        hs = slice(h * ML_DIM, (h + 1) * ML_DIM)
        q = q_ref[:, hs]
        k = k_ref[:, hs]
        v = v_ref[:, hs]
        n = n_ref[:, h, :]
        qk = jnp.sum(q * k, axis=1, keepdims=True)
        qn = jnp.sum(q * n, axis=1, keepdims=True)
        sw = qk * e_i[:, h:h + 1]
        den = aw[:, h:h + 1] * qn + sw
        inv = 1.0 / jnp.maximum(jnp.abs(den), floor[:, h:h + 1])
        no_ref[:, h, :] = decay[:, h:h + 1] * n + w[:, h:h + 1] * k
        q_b = q.astype(BF16)
        wk_t = jnp.concatenate([w[:, h:h + 1] * k, pad], axis=0).T.astype(BF16)
        v_rows = jnp.concatenate([v, pad], axis=0).astype(BF16)
        nums = []
        for b in range(bt):
            C = c_ref[b, h]
            q_c = _mm(q_b, C.astype(BF16))[b:b + 1, :]
            nums.append(aw[b:b + 1, h:h + 1] * q_c)
            outer = _mm(jnp.where(lane == b, wk_t, jnp.zeros_like(wk_t)), v_rows)
            co_ref[b, h] = decay[b:b + 1, h:h + 1] * C + outer
        num = jnp.concatenate(nums, axis=0) + sw * v
        h_ref[:, hs] = (num * inv * og_ref[:, hs]).astype(h_ref.dtype)


def _mlstm_decode(mq, mk, mv, og, gif, state_c, state_n, state_m, bt):
    nbatch = mq.shape[0]
    b2 = lambda i: (i, 0)
    b3 = lambda i: (i, 0, 0)
    b4 = lambda i: (i, 0, 0, 0)
    return pl.pallas_call(
        _mlstm_decode_kernel,
        grid=(nbatch // bt,),
        in_specs=[
            pl.BlockSpec((bt, ML_W), b2),
            pl.BlockSpec((bt, ML_W), b2),
            pl.BlockSpec((bt, ML_W), b2),
            pl.BlockSpec((bt, ML_W), b2),
            pl.BlockSpec((bt, LANES), b2),
            pl.BlockSpec((bt, ML_HEADS, ML_DIM, ML_DIM), b4),
            pl.BlockSpec((bt, ML_HEADS, ML_DIM), b3),
            pl.BlockSpec((bt, ML_HEADS), b2),
        ],
        out_specs=(
            pl.BlockSpec((bt, ML_W), b2),
            pl.BlockSpec((bt, ML_HEADS, ML_DIM, ML_DIM), b4),
            pl.BlockSpec((bt, ML_HEADS, ML_DIM), b3),
            pl.BlockSpec((bt, ML_HEADS), b2),
        ),
        out_shape=(
            jax.ShapeDtypeStruct((nbatch, ML_W), BF16),
            jax.ShapeDtypeStruct((nbatch, ML_HEADS, ML_DIM, ML_DIM), F32),
            jax.ShapeDtypeStruct((nbatch, ML_HEADS, ML_DIM), F32),
            jax.ShapeDtypeStruct((nbatch, ML_HEADS), F32),
        ),
        compiler_params=pltpu.CompilerParams(dimension_semantics=("parallel",)),
        name="mlstm_decode",
    )(mq, mk, mv, og, gif, state_c, state_n, state_m)


def _rope_tables(pos):
    half = HEAD_DIM // 2
    inv = 1.0 / (ROPE_THETA ** (jnp.arange(half, dtype=F32) / half))
    ang = pos.astype(F32)[:, None] * inv[None, :]
    cos = jnp.cos(ang)
    sin = jnp.sin(ang)
    cos_t = jnp.concatenate([cos, cos, cos, cos], axis=1)
    sin_t = jnp.concatenate([-sin, sin, -sin, sin], axis=1)
    return cos_t, sin_t


def kernel(x_prompt, x_sample, cache_k, cache_v, state_C, state_n, state_m, p_prompt, p_sample, norm_mix, w_in, b_in, attn_sinks, w_attn_up, w_ml_up, w_o, norm_mlp, w_ff1, w_ff2, norm_ple, w_ple_gate, w_ple_proj, norm_final):
    assert w_in.shape[0] == 1, "single layer"
    batch, seq, _ = x_prompt.shape
    nbatch = x_sample.shape[0]
    assert x_sample.shape[1] == 1

    w_in0 = w_in[0]
    b_in0 = b_in[0]
    w_proj, w_gate, w_au, w_mu, w_out, w_f1, w_f2, w_pg, w_pp = _weight_cast(
        w_in0.T, (w_attn_up[0], w_ml_up[0], w_o[0], w_ff1[0], w_ff2[0], w_ple_gate[0], w_ple_proj[0]))
    b_proj = b_in0[:N_PROJ][None, :]
    b_gate = b_in0[OFF_GA:][None, :]
    g_mix = norm_mix[0][None, :]
    out_weights = (
        g_mix, w_gate, b_gate, w_au, w_mu, w_out,
        norm_mlp[0][None, :], w_f1, w_f2,
        norm_ple[0][None, :], w_pg, w_pp, norm_final[None, :],
    )
    sinks = attn_sinks[0]

    cos_p, sin_p = _rope_tables(jnp.arange(seq, dtype=jnp.int32))
    xp = x_prompt.reshape(batch * seq, D_MODEL)
    y_prompt, k_last, v_last, cst, nst, mst = _prompt_fused(
        sinks, xp, p_prompt[0].reshape(batch * seq, D_PLE), cos_p, sin_p, g_mix, w_proj, b_proj, out_weights,
        batch, seq, tm=256)
    y_prompt = y_prompt.reshape(batch, seq, D_MODEL)
    k_prompt = k_last.reshape(1, batch, WINDOW, N_KV_HEADS, HEAD_DIM)
    v_prompt = v_last.reshape(1, batch, WINDOW, N_KV_HEADS, HEAD_DIM)
    c_prompt = cst[None]
    n_prompt = nst[:, :ML_HEADS, :][None]
    m_prompt = mst[:, :ML_HEADS, 0][None]

    cos_s, sin_s = _rope_tables(jnp.full((nbatch,), PAST_LEN, dtype=jnp.int32))
    xs = x_sample.reshape(nbatch, D_MODEL)
    qs, _, _, ks, vs, mqs, mks, mvs, ogs, gifs, _ = _in_proj(xs, g_mix, w_proj, b_proj, cos_s, sin_s, tm=nbatch,
                                                          m_dtype=F32, feature_major=False)
    q4 = qs.reshape(nbatch, N_KV_HEADS, GQA_GROUP, 1, HEAD_DIM)
    sel = jnp.eye(N_KV_HEADS, dtype=BF16)[None, :, None, :, None]
    qmat = (q4 * sel).reshape(nbatch, N_Q_HEADS, ATTN_KV)
    to_fp = lambda c: c[0].transpose(0, 2, 3, 1).reshape(nbatch, ATTN_KV, WINDOW)
    from_fp = lambda c: c.reshape(nbatch, N_KV_HEADS, HEAD_DIM, WINDOW).transpose(0, 3, 1, 2)[None]
    o_full, k_s, v_s = _attn_decode(sinks[:, None], qmat, ks, vs, to_fp(cache_k), to_fp(cache_v), bt=8)
    o5 = o_full.reshape(nbatch, N_KV_HEADS, GQA_GROUP, N_KV_HEADS, HEAD_DIM)
    a_s = jnp.stack([o5[:, kv, :, kv, :] for kv in range(N_KV_HEADS)], axis=1).reshape(nbatch, ATTN_Q).astype(BF16)
    h_s, c_s, n_s, m_s = _mlstm_decode(mqs, mks, mvs, ogs, gifs, state_C[0], state_n[0], state_m[0], bt=8)
    y_sample = _out_proj(xs, a_s, h_s, p_sample[0].reshape(nbatch, D_PLE), out_weights, tm=nbatch)
    y_sample = y_sample.reshape(nbatch, 1, D_MODEL)
    k_sample = from_fp(k_s)
    v_sample = from_fp(v_s)

    return (y_prompt, y_sample, k_prompt, v_prompt, c_prompt, n_prompt, m_prompt,
            k_sample, v_sample, c_s[None], n_s[None], m_s[None])
```

```python
import functools

import numpy as np
import jax
import jax.numpy as jnp
from jax import lax
from jax.experimental import pallas as pl
from jax.experimental.pallas import tpu as pltpu

F32 = jnp.float32
BF16 = jnp.bfloat16

D_MODEL = 1024
HEAD_DIM = 64
N_Q_HEADS = 8
N_KV_HEADS = 2
GQA_GROUP = N_Q_HEADS // N_KV_HEADS
WINDOW = 128
ROPE_THETA = 10000.0
ML_HEADS = 4
ML_DIM = 128
ML_CHUNK = 128
D_FF = 4 * D_MODEL
D_PLE = 256
EPS = 1e-6
PAST_LEN = 16384

ATTN_Q = N_Q_HEADS * HEAD_DIM
ATTN_KV = N_KV_HEADS * HEAD_DIM
ML_W = ML_HEADS * ML_DIM
LANES = 128

OFF_AQ = 0
OFF_AK = OFF_AQ + ATTN_Q
OFF_AV = OFF_AK + ATTN_KV
OFF_MQ = OFF_AV + ATTN_KV
OFF_MK = OFF_MQ + ML_W
OFF_MV = OFF_MK + ML_W
OFF_MO = OFF_MV + ML_W
OFF_MI = OFF_MO + ML_W
OFF_MF = OFF_MI + ML_HEADS
OFF_GA = OFF_MF + ML_HEADS
OFF_GM = OFF_GA + D_MODEL
D_IN = OFF_GM + D_MODEL
N_PROJ = OFF_MI + LANES

VMEM_LIMIT = 60 * 1024 * 1024
PROMPT_TILE = 256
DECODE_ROWS = 16

def _rms(x, g):
    r = lax.rsqrt(jnp.mean(x * x, axis=-1, keepdims=True) + EPS)
    return (x * r) * g


def _mm(a, b):
    return jnp.dot(a, b, preferred_element_type=F32)


def _mm_nt(a, b):
    return lax.dot_general(a, b, (((1,), (1,)), ((), ())), preferred_element_type=F32)


def _mm_tn(a, b):
    return lax.dot_general(a, b, (((0,), (0,)), ((), ())), preferred_element_type=F32)


def _log_sigmoid(x):
    return jnp.minimum(x, 0.0) - jnp.log1p(jnp.exp(-jnp.abs(x)))


def _weave(*streams, delays=None):
    delays = delays or [0] * len(streams)
    live = list(zip(streams, delays))
    rnd = 0
    while live:
        for g, d in list(live):
            if rnd >= d:
                try:
                    next(g)
                except StopIteration:
                    live.remove((g, d))
        rnd += 1


WEIGHT_ROW_SPLIT = 8


def _weight_cast_kernel(*refs):
    n_plain = (len(refs) - 3) // 2
    w_in_t_ref, plain_in = refs[0], refs[1:1 + n_plain]
    w_proj_ref, w_gate_ref, plain_out = refs[1 + n_plain], refs[2 + n_plain], refs[3 + n_plain:]
    for g in range(N_PROJ // LANES):
        w_proj_ref[:, g * LANES:(g + 1) * LANES] = w_in_t_ref[g * LANES:(g + 1) * LANES, :].T.astype(BF16)
    for g in range(2 * D_MODEL // LANES):
        w_gate_ref[:, g * LANES:(g + 1) * LANES] = w_in_t_ref[OFF_GA + g * LANES:OFF_GA + (g + 1) * LANES, :].T.astype(BF16)
    for src, dst in zip(plain_in, plain_out):
        dst[...] = src[...].astype(BF16)


def _weight_cast(w_in_t, plain):
    split = WEIGHT_ROW_SPLIT
    assert D_MODEL // split == LANES
    rows_blk = lambda w: pl.BlockSpec((w.shape[0] // split, w.shape[1]), lambda i: (i, 0))
    out_shapes = (jax.ShapeDtypeStruct((D_MODEL, N_PROJ), BF16), jax.ShapeDtypeStruct((D_MODEL, 2 * D_MODEL), BF16),
                  *(jax.ShapeDtypeStruct(w.shape, BF16) for w in plain))
    return pl.pallas_call(
        _weight_cast_kernel,
        grid=(split,),
        in_specs=[pl.BlockSpec((D_IN, LANES), lambda i: (0, i))] + [rows_blk(w) for w in plain],
        out_specs=tuple(rows_blk(s) for s in out_shapes),
        out_shape=out_shapes,
        compiler_params=pltpu.CompilerParams(dimension_semantics=("parallel",), vmem_limit_bytes=VMEM_LIMIT),
        name="weight_cast",
    )(w_in_t, *plain)


def _in_proj_stages(x, g_ref, w_ref, b_ref, cos_ref, sin_ref,
                    q_ref, k_ref, v_ref, kl_ref, vl_ref, mq_ref, mk_ref, mv_ref, og_ref, gif_ref, gt_ref,
                    *, feature_major):
    xn = _rms(x, g_ref[...]).astype(BF16)
    tm = x.shape[0]

    def proj(lo, hi):
        return _mm(xn, w_ref[:, lo:hi]) + b_ref[:, lo:hi]

    cos = cos_ref[...]
    sin = sin_ref[...]
    lane = lax.broadcasted_iota(jnp.int32, (1, LANES), 1)
    first_half = (lane % HEAD_DIM) < (HEAD_DIM // 2)

    def rope(z):
        partner = jnp.where(first_half, pltpu.roll(z, LANES - HEAD_DIM // 2, 1), pltpu.roll(z, HEAD_DIM // 2, 1))
        return z * cos + partner * sin

    zq = proj(OFF_AQ, OFF_AK)
    for c in range(ATTN_Q // LANES):
        sl = slice(c * LANES, (c + 1) * LANES)
        qc = rope(zq[:, sl]) * (HEAD_DIM ** -0.5)
        if feature_major:
            q_ref[sl, :] = qc.T.astype(q_ref.dtype)
        else:
            q_ref[:, sl] = qc.astype(q_ref.dtype)
    yield
    zkv = proj(OFF_AK, OFF_MQ)
    k = rope(zkv[:, :ATTN_KV])
    v = zkv[:, ATTN_KV:]
    k_ref[...] = k.astype(k_ref.dtype)
    v_ref[...] = (v.T if feature_major else v).astype(v_ref.dtype)
    kl_ref[...] = k[tm - WINDOW:, :]
    vl_ref[...] = v[tm - WINDOW:, :]
    yield
    for z_off, dst in ((OFF_MQ, mq_ref), (OFF_MV, mv_ref)):
        z = proj(z_off, z_off + ML_W)
        if feature_major:
            for h in range(ML_HEADS):
                hs = slice(h * ML_DIM, (h + 1) * ML_DIM)
                dst[0, hs, :] = z[:, hs].T.astype(dst.dtype)
        else:
            dst[...] = z.astype(dst.dtype)
        yield
    mk_ref[...] = (proj(OFF_MK, OFF_MV) * (ML_DIM ** -0.5)).astype(mk_ref.dtype)
    yield
    og_ref[...] = jax.nn.sigmoid(proj(OFF_MO, OFF_MI))
    yield
    zg = proj(OFF_MI, N_PROJ)
    gif = jnp.where(lane < ML_HEADS, zg, jnp.where(lane < 2 * ML_HEADS, _log_sigmoid(zg), 0.0))
    if feature_major:
        L = ML_CHUNK
        upper = (lax.broadcasted_iota(jnp.int32, (L, L), 0) <= lax.broadcasted_iota(jnp.int32, (L, L), 1)).astype(F32)
        row8 = lax.broadcasted_iota(jnp.int32, (8, 1), 0)
        fill = jnp.zeros((L - 16, L), F32)
        for c in range(tm // L):
            cs = slice(c * L, (c + 1) * L)
            g_rows = gif[cs, :].T[0:8, :]
            cum = jnp.dot(g_rows, upper, precision=lax.Precision.HIGHEST, preferred_element_type=F32)
            rows = jnp.where(row8 < ML_HEADS, g_rows, cum)
            gt_ref[0, :, cs] = rows
            diff = rows - pltpu.roll(rows, ML_HEADS, 0)
            gif_ref[cs, :] = jnp.concatenate([rows, diff, fill], axis=0).T
    else:
        gif_ref[...] = gif
        gt_ref[0] = jnp.zeros(gt_ref.shape[1:], F32)
    yield


def _in_proj_kernel(x_ref, *refs, feature_major):
    _weave(_in_proj_stages(x_ref[...], *refs, feature_major=feature_major))


def _in_proj(x, g, w, b, cos, sin, tm, m_dtype, feature_major):
    rows = x.shape[0]
    seq = cos.shape[0]
    tps = seq // tm
    nseq = rows // seq
    row = lambda i: (i, 0)
    const = lambda i: (0, 0)
    pos = lambda i: (i % tps, 0)
    last = lambda i: (i // tps, 0)
    row_out = lambda n, dt: (jax.ShapeDtypeStruct((rows, n), dt), pl.BlockSpec((tm, n), row))
    col_out = lambda n, dt: (jax.ShapeDtypeStruct((n, rows), dt), pl.BlockSpec((n, tm), lambda i: (0, i)))
    att_out = col_out if feature_major else row_out
    seq_out = lambda n, dt: (jax.ShapeDtypeStruct((nseq, n, seq), dt),
                             pl.BlockSpec((1, n, tm), lambda i: (i // tps, 0, i % tps)))
    ml_out = seq_out if feature_major else row_out
    outs = (
        att_out(ATTN_Q, BF16),
        row_out(ATTN_KV, BF16),
        att_out(ATTN_KV, BF16),
        (jax.ShapeDtypeStruct((nseq * WINDOW, ATTN_KV), F32), pl.BlockSpec((WINDOW, ATTN_KV), last)),
        (jax.ShapeDtypeStruct((nseq * WINDOW, ATTN_KV), F32), pl.BlockSpec((WINDOW, ATTN_KV), last)),
        ml_out(ML_W, m_dtype),
        row_out(ML_W, m_dtype),
        ml_out(ML_W, m_dtype),
        row_out(ML_W, F32),
        row_out(LANES, F32),
        seq_out(8, F32),
    )
    return pl.pallas_call(
        functools.partial(_in_proj_kernel, feature_major=feature_major),
        grid=(rows // tm,),
        in_specs=[
            pl.BlockSpec((tm, D_MODEL), row),
            pl.BlockSpec((1, D_MODEL), const),
            pl.BlockSpec((D_MODEL, N_PROJ), const),
            pl.BlockSpec((1, N_PROJ), const),
            pl.BlockSpec((tm, LANES), pos),
            pl.BlockSpec((tm, LANES), pos),
        ],
        out_specs=tuple(o[1] for o in outs),
        out_shape=tuple(o[0] for o in outs),
        compiler_params=pltpu.CompilerParams(dimension_semantics=("arbitrary",), vmem_limit_bytes=VMEM_LIMIT),
        name="in_proj",
    )(x, g, w, b, cos, sin)


def _attn_stages(sink_ref, qt_ref, kp_ref, kc_ref, vtp_ref, vtc_ref, first_block, emit):
    L = WINDOW
    nsub = kc_ref.shape[0] // L
    key = lax.broadcasted_iota(jnp.int32, (2 * L, L), 0)
    t = lax.broadcasted_iota(jnp.int32, (2 * L, L), 1)
    is_prev = key < L
    allowed = (is_prev & (key >= t)) | (~is_prev & (key - L <= t))
    bias = jnp.where(allowed, 0.0, -jnp.inf)
    bias_first = jnp.where(is_prev & first_block, -jnp.inf, bias)
    zeros_q = jnp.zeros((HEAD_DIM, L), BF16)
    heads = range(N_Q_HEADS)
    for r in range(nsub):
        ts = slice(r * L, (r + 1) * L)
        k_prev = kp_ref[...] if r == 0 else kc_ref[(r - 1) * L:r * L, :]
        vt_prev = vtp_ref[...] if r == 0 else vtc_ref[:, (r - 1) * L:r * L]
        k2 = jnp.concatenate([k_prev, kc_ref[ts, :]], axis=0)
        vt2 = jnp.concatenate([vt_prev, vtc_ref[:, ts]], axis=1)
        b_r = bias_first if r == 0 else bias
        q_op = [jnp.concatenate([qt_ref[hd * HEAD_DIM:(hd + 1) * HEAD_DIM, ts], zeros_q][::1 if hd < GQA_GROUP else -1],
                                axis=0) for hd in heads]
        s = [_mm(k2, q_op[hd]) + b_r for hd in heads]
        yield
        m = [jnp.maximum(jnp.max(s[hd], axis=0, keepdims=True), sink_ref[hd]) for hd in heads]
        yield
        p = [jnp.exp(s[hd] - m[hd]) for hd in heads]
        yield
        den = [jnp.sum(p[hd], axis=0, keepdims=True) + jnp.exp(sink_ref[hd] - m[hd]) for hd in heads]
        o = [_mm(vt2[(hd // GQA_GROUP) * HEAD_DIM:(hd // GQA_GROUP + 1) * HEAD_DIM, :], p[hd].astype(BF16)) / den[hd]
             for hd in heads]
        yield
        emit(r, jnp.concatenate(o, axis=0).T)
        yield


ML_STATE_ROWS = ML_DIM + 8


def _mlstm_stages(chains, emit):
    L = ML_CHUNK
    s_idx = lax.broadcasted_iota(jnp.int32, (L, L), 0)
    t_idx = lax.broadcasted_iota(jnp.int32, (L, L), 1)
    causal = s_idx <= t_idx
    n = range(len(chains))
    dlog = [jnp.where(causal, ch["b_r"] + ch["u_c"], -jnp.inf) for ch in chains]
    yield
    qk = [_mm(ch["k"], ch["qt"]) for ch in chains]
    inter = [_mm(ch["state"].astype(BF16), ch["qt"]) for ch in chains]
    yield
    a = [ch["b_r"] + ch["m_prev"] for ch in chains]
    m_t = [jnp.maximum(a[i], jnp.max(dlog[i], axis=0, keepdims=True)) for i in n]
    yield
    sw = [qk[i] * jnp.exp(dlog[i] - m_t[i]) for i in n]
    yield
    sv = [_mm(chains[i]["vt"], sw[i].astype(BF16)) for i in n]
    yield
    hs = []
    for i in n:
        aw = jnp.exp(a[i] - m_t[i])
        num = aw * inter[i][:ML_DIM] + sv[i]
        den = aw * inter[i][ML_DIM:ML_DIM + 1] + jnp.sum(sw[i], axis=0, keepdims=True)
        hh = num / jnp.maximum(jnp.abs(den), jnp.exp(-m_t[i]))
        hs.append(hh.T * chains[i]["og"])
    yield
    pad = jnp.zeros((ML_STATE_ROWS - ML_DIM - 1, L), F32)
    new_state, new_m = [], []
    for i in n:
        ch = chains[i]
        b_end = ch["b_r"][:, L - 1:L]
        g = b_end - ch["b_r"] + ch["ig_r"]
        m_new = jnp.maximum(b_end + ch["m_prev"], jnp.max(g, axis=1, keepdims=True))
        decay = jnp.exp(b_end + ch["m_prev"] - m_new)
        w = jnp.exp(g - m_new)
        v_aug = jnp.concatenate([ch["vt"].astype(F32) * w, w, pad], axis=0).astype(BF16)
        new_state.append(decay * ch["state"] + _mm(v_aug, ch["k"]))
        new_m.append(m_new)
    emit(hs, new_state, new_m)
    yield


DENSE_COLS = 512


def _dense_stages(x, a, hm, p, gmix_ref, wg_ref, bg_ref, wau_ref, wmu_ref, wo_ref,
                  gmlp_ref, wff1_ref, wff2_ref, gple_ref, wpg_ref, wpp_ref, gfin_ref, emit):
    nc = DENSE_COLS
    pieces = lambda n: [slice(c, c + nc) for c in range(0, n, nc)]
    xn = _rms(x, gmix_ref[...]).astype(BF16)
    gates = []
    for cs in pieces(2 * D_MODEL):
        gates.append(jax.nn.sigmoid(_mm(xn, wg_ref[:, cs]) + bg_ref[:, cs]))
        yield
    half = D_MODEL // nc
    mix = []
    for j, cs in enumerate(pieces(D_MODEL)):
        mix.append((gates[j] * _mm(a, wau_ref[:, cs]) + gates[half + j] * _mm(hm, wmu_ref[:, cs])).astype(BF16))
        yield
    mix = jnp.concatenate(mix, axis=1)
    h = []
    for cs in pieces(D_MODEL):
        h.append(x[:, cs] + _mm(mix, wo_ref[:, cs]))
        yield
    h = jnp.concatenate(h, axis=1)
    hn = _rms(h, gmlp_ref[...]).astype(BF16)
    u = []
    for cs in pieces(D_FF):
        u.append(jnp.square(jnp.maximum(_mm(hn, wff1_ref[:, cs]), 0.0)).astype(BF16))
        yield
    u = jnp.concatenate(u, axis=1)
    h2 = []
    for cs in pieces(D_MODEL):
        h2.append(h[:, cs] + _mm(u, wff2_ref[:, cs]))
        yield
    h = jnp.concatenate(h2, axis=1)
    hn = _rms(h, gple_ref[...]).astype(BF16)
    pb = p.astype(BF16)
    h3 = []
    for cs in pieces(D_MODEL):
        h3.append(h[:, cs] + _mm(pb, wpp_ref[:, cs]) * jax.nn.sigmoid(_mm(hn, wpg_ref[:, cs])))
        yield
    emit(_rms(jnp.concatenate(h3, axis=1), gfin_ref[...]))
    yield


def _out_kernel(x_ref, a_ref, hm_ref, p_ref, *rest):
    weights, y_ref = rest[:-1], rest[-1]

    def emit(y):
        y_ref[...] = y

    _weave(_dense_stages(x_ref[...], a_ref[...], hm_ref[...], p_ref[...], *weights, emit=emit))


def _out_proj(x, a, hm, p, weights, tm):
    rows = x.shape[0]
    row = lambda i: (i, 0)
    const = lambda i: (0, 0)
    w_specs = [pl.BlockSpec(w.shape, const, pipeline_mode=pl.Buffered(1)) for w in weights]
    return pl.pallas_call(
        _out_kernel,
        grid=(rows // tm,),
        in_specs=[
            pl.BlockSpec((tm, D_MODEL), row),
            pl.BlockSpec((tm, ATTN_Q), row),
            pl.BlockSpec((tm, ML_W), row),
            pl.BlockSpec((tm, D_PLE), row),
        ] + w_specs,
        out_specs=pl.BlockSpec((tm, D_MODEL), row),
        out_shape=jax.ShapeDtypeStruct((rows, D_MODEL), F32),
        compiler_params=pltpu.CompilerParams(dimension_semantics=("parallel",), vmem_limit_bytes=VMEM_LIMIT),
        name="out_proj",
    )(x, a, hm, p, *weights)


N_FUSED_WEIGHTS = 13
N_PROJ_SLOTS = 9
PROJ_DELAY = 0


def _prompt_kernel(sink_ref, x0_ref, cos_ref, sin_ref, x2_ref, p_ref, gmix_ref, wp_ref, bp_ref, *rest, tiles_per_seq):
    weights = rest[:N_FUSED_WEIGHTS]
    y_ref, kl_ref, vl_ref, cst_ref, nst_ref, mst_ref = rest[N_FUSED_WEIGHTS:N_FUSED_WEIGHTS + 6]
    scr = rest[N_FUSED_WEIGHTS + 6:]
    proj_scr, (kprev_scr, vtprev_scr, a_scr, h_scr, c_scr, m_scr) = scr[:N_PROJ_SLOTS], scr[N_PROJ_SLOTS:]
    i = pl.program_id(0)
    n_tiles = pl.num_programs(0) - 2
    t1 = jnp.clip(i - 1, 0, n_tiles - 1)
    tile_in_seq = t1 % tiles_per_seq
    s0, s1 = i % 2, (i + 1) % 2
    L = ML_CHUNK
    tm = x0_ref.shape[0]

    @pl.when(i == 0)
    def _():
        for r in scr:
            r[...] = jnp.zeros_like(r)

    @pl.when(tile_in_seq == 0)
    def _():
        c_scr[...] = jnp.zeros_like(c_scr)
        m_scr[...] = jnp.zeros_like(m_scr)

    def projection():
        outs = [r.at[s0] for r in proj_scr]
        yield from _in_proj_stages(x0_ref[...], gmix_ref, wp_ref, bp_ref, cos_ref, sin_ref,
                                   outs[0], outs[1], outs[2], kl_ref, vl_ref, *outs[3:], feature_major=True)

    def branches():
        qt_ref, kc_ref, vtc_ref, mqt_ref, mk_ref, mvt_ref, og_ref, gif_ref, gt_ref = [r.at[s1] for r in proj_scr]
        slot = s1

        def emit_attn(r, blk):
            a_scr[slot, r * WINDOW:(r + 1) * WINDOW, :] = blk.astype(a_scr.dtype)

        yield from _attn_stages(sink_ref, qt_ref, kprev_scr, kc_ref, vtprev_scr, vtc_ref, tile_in_seq == 0, emit_attn)
        kprev_scr[...] = kc_ref[tm - WINDOW:, :]
        vtprev_scr[...] = vtc_ref[:, tm - WINDOW:]
        hsl = lambda h: slice(h * ML_DIM, (h + 1) * ML_DIM)
        carry = dict(state=[c_scr[h] for h in range(ML_HEADS)], m=[m_scr[h:h + 1, :] for h in range(ML_HEADS)])
        for c in range(tm // L):
            cs = slice(c * L, (c + 1) * L)

            def emit_ml(hs, new_state, new_m, cs=cs):
                for h in range(ML_HEADS):
                    h_scr[slot, cs, hsl(h)] = hs[h].astype(h_scr.dtype)
                carry["state"], carry["m"] = new_state, new_m

            chains = [dict(qt=mqt_ref[0, hsl(h), cs], k=mk_ref[cs, hsl(h)], vt=mvt_ref[0, hsl(h), cs],
                           og=og_ref[cs, hsl(h)], ig_r=gt_ref[0, h:h + 1, cs],
                           b_r=gt_ref[0, ML_HEADS + h:ML_HEADS + h + 1, cs],
                           u_c=gif_ref[cs, 2 * ML_HEADS + h:2 * ML_HEADS + h + 1],
                           state=carry["state"][h], m_prev=carry["m"][h]) for h in range(ML_HEADS)]
            yield from _mlstm_stages(chains, emit_ml)
        for h in range(ML_HEADS):
            c_scr[h] = carry["state"][h]
            m_scr[h:h + 1, :] = carry["m"][h]

    def dense():
        def emit_y(y):
            y_ref[...] = y

        return _dense_stages(x2_ref[...], a_scr[s0], h_scr[s0], p_ref[...], *weights, emit=emit_y)

    @pl.when(i < 2)
    def _():
        _weave(branches(), projection())

    @pl.when((i >= 2) & (i < n_tiles))
    def _():
        _weave(branches(), dense(), projection())

    @pl.when(i >= n_tiles)
    def _():
        _weave(branches(), dense())

    @pl.when((tile_in_seq == tiles_per_seq - 1) & (i >= 1) & (i <= n_tiles))
    def _():
        nst_ref[...] = jnp.zeros_like(nst_ref)
        for h in range(ML_HEADS):
            final = c_scr[h]
            cst_ref[0, h] = final[:ML_DIM].T
            nst_ref[0, h:h + 1, :] = final[ML_DIM:ML_DIM + 1]
        mst_ref[0] = m_scr[...]


def _prompt_fused(sinks, x, p, cos, sin, g_mix, w_proj, b_proj, weights, batch, seq, tm):
    assert len(weights) == N_FUSED_WEIGHTS
    rows = batch * seq
    n_tiles = rows // tm
    tps = seq // tm
    t0 = lambda i: jnp.minimum(i, n_tiles - 1)
    t1 = lambda i: jnp.clip(i - 1, 0, n_tiles - 1)
    t2 = lambda i: jnp.maximum(i - 2, 0)
    const = lambda i: (0, 0)
    single = lambda a: pl.BlockSpec(a.shape, const, pipeline_mode=pl.Buffered(1))
    slot = lambda shape, dt: pltpu.VMEM((2,) + shape, dt)
    return pl.pallas_call(
        functools.partial(_prompt_kernel, tiles_per_seq=tps),
        grid=(n_tiles + 2,),
        in_specs=[
            pl.BlockSpec(memory_space=pltpu.SMEM),
            pl.BlockSpec((tm, D_MODEL), lambda i: (t0(i), 0)),
            pl.BlockSpec((tm, LANES), lambda i: (t0(i) % tps, 0)),
            pl.BlockSpec((tm, LANES), lambda i: (t0(i) % tps, 0)),
            pl.BlockSpec((tm, D_MODEL), lambda i: (t2(i), 0)),
            pl.BlockSpec((tm, D_PLE), lambda i: (t2(i), 0)),
            single(g_mix), single(w_proj), single(b_proj),
        ] + [single(w) for w in weights],
        out_specs=(
            pl.BlockSpec((tm, D_MODEL), lambda i: (t2(i), 0)),
            pl.BlockSpec((WINDOW, ATTN_KV), lambda i: (t0(i) // tps, 0)),
            pl.BlockSpec((WINDOW, ATTN_KV), lambda i: (t0(i) // tps, 0)),
            pl.BlockSpec((1, ML_HEADS, ML_DIM, ML_DIM), lambda i: (t1(i) // tps, 0, 0, 0)),
            pl.BlockSpec((1, 8, LANES), lambda i: (t1(i) // tps, 0, 0)),
            pl.BlockSpec((1, 8, LANES), lambda i: (t1(i) // tps, 0, 0)),
        ),
        out_shape=(
            jax.ShapeDtypeStruct((rows, D_MODEL), F32),
            jax.ShapeDtypeStruct((batch * WINDOW, ATTN_KV), F32),
            jax.ShapeDtypeStruct((batch * WINDOW, ATTN_KV), F32),
            jax.ShapeDtypeStruct((batch, ML_HEADS, ML_DIM, ML_DIM), F32),
            jax.ShapeDtypeStruct((batch, 8, LANES), F32),
            jax.ShapeDtypeStruct((batch, 8, LANES), F32),
        ),
        scratch_shapes=[
            slot((ATTN_Q, tm), BF16), slot((tm, ATTN_KV), BF16), slot((ATTN_KV, tm), BF16),
            slot((1, ML_W, tm), BF16), slot((tm, ML_W), BF16), slot((1, ML_W, tm), BF16),
            slot((tm, ML_W), F32), slot((tm, LANES), F32), slot((1, 8, tm), F32),
            pltpu.VMEM((WINDOW, ATTN_KV), BF16), pltpu.VMEM((ATTN_KV, WINDOW), BF16),
            slot((tm, ATTN_Q), BF16), slot((tm, ML_W), BF16),
            pltpu.VMEM((ML_HEADS, ML_STATE_ROWS, ML_DIM), F32), pltpu.VMEM((8, LANES), F32),
        ],
        compiler_params=pltpu.CompilerParams(dimension_semantics=("arbitrary",), vmem_limit_bytes=VMEM_LIMIT),
        name="prompt_fused",
    )(sinks, x, cos, sin, x, p, g_mix, w_proj, b_proj, *weights)


def _attn_decode_kernel(sink_ref, qm_ref, kn_ref, vn_ref, ck_ref, cv_ref, o_ref, ko_ref, vo_ref):
    bt = qm_ref.shape[0]
    qm = qm_ref[...]
    kn = kn_ref[...]
    vn = vn_ref[...]
    ck = ck_ref[...]
    cv = cv_ref[...]
    sink = sink_ref[...]
    s = jnp.einsum('bhc,bck->bhk', qm, ck.astype(BF16), preferred_element_type=F32)
    s_new = jnp.sum(qm.astype(F32) * kn[:, None, :], axis=2, keepdims=True)
    m = jnp.maximum(jnp.maximum(jnp.max(s, axis=2, keepdims=True), s_new), sink)
    p = jnp.exp(s - m)
    p_new = jnp.exp(s_new - m)
    den = jnp.sum(p, axis=2, keepdims=True) + p_new + jnp.exp(sink - m)
    o = jnp.einsum('bhk,bck->bhc', p.astype(BF16), cv.astype(BF16), preferred_element_type=F32)
    o_ref[...] = (o + p_new * vn[:, None, :]) / den
    pad = jnp.zeros((LANES - bt, ATTN_KV), F32)
    kn_t = jnp.concatenate([kn, pad], axis=0).T
    vn_t = jnp.concatenate([vn, pad], axis=0).T
    newest = lax.broadcasted_iota(jnp.int32, (ATTN_KV, WINDOW), 1) == WINDOW - 1
    for b in range(bt):
        ko_ref[b] = jnp.where(newest, kn_t[:, b:b + 1], pltpu.roll(ck[b], WINDOW - 1, 1))
        vo_ref[b] = jnp.where(newest, vn_t[:, b:b + 1], pltpu.roll(cv[b], WINDOW - 1, 1))


def _attn_decode(sinks_col, qmat, k_new, v_new, cache_k, cache_v, bt):
    nbatch = qmat.shape[0]
    b3 = lambda i: (i, 0, 0)
    b2 = lambda i: (i, 0)
    return pl.pallas_call(
        _attn_decode_kernel,
        grid=(nbatch // bt,),
        in_specs=[
            pl.BlockSpec((N_Q_HEADS, 1), lambda i: (0, 0)),
            pl.BlockSpec((bt, N_Q_HEADS, ATTN_KV), b3),
            pl.BlockSpec((bt, ATTN_KV), b2),
            pl.BlockSpec((bt, ATTN_KV), b2),
            pl.BlockSpec((bt, WINDOW, ATTN_KV), b3),
            pl.BlockSpec((bt, WINDOW, ATTN_KV), b3),
        ],
        out_specs=(
            pl.BlockSpec((bt, N_Q_HEADS, ATTN_KV), b3),
            pl.BlockSpec((bt, WINDOW, ATTN_KV), b3),
            pl.BlockSpec((bt, WINDOW, ATTN_KV), b3),
        ),
        out_shape=(
            jax.ShapeDtypeStruct((nbatch, N_Q_HEADS, ATTN_KV), F32),
            jax.ShapeDtypeStruct((nbatch, WINDOW, ATTN_KV), F32),
            jax.ShapeDtypeStruct((nbatch, WINDOW, ATTN_KV), F32),
        ),
        compiler_params=pltpu.CompilerParams(dimension_semantics=("parallel",)),
        name="attn_decode",
    )(sinks_col, qmat, k_new, v_new, cache_k, cache_v)


def _mlstm_decode_kernel(q_ref, k_ref, v_ref, og_ref, gif_ref, c_ref, n_ref, m_ref, h_ref, co_ref, no_ref, mo_ref):
    bt = q_ref.shape[0]
    gif = gif_ref[...]
    ig = gif[:, 0:ML_HEADS]
    lf = gif[:, ML_HEADS:2 * ML_HEADS]
    m_prev = m_ref[...]
    a = lf + m_prev
    m_t = jnp.maximum(a, ig)
    aw = jnp.exp(a - m_t)
    e_i = jnp.exp(ig - m_t)
    floor = jnp.exp(-m_t)
    m_new = jnp.maximum(a, ig)
    decay = jnp.exp(a - m_new)
    w = jnp.exp(ig - m_new)
    mo_ref[...] = m_new
    pad = jnp.zeros((LANES - bt, ML_DIM), F32)
    lane = lax.broadcasted_iota(jnp.int32, (ML_DIM, LANES), 1)
    for h in range(ML_HEADS):
        hs = slice(h * ML_DIM, (h + 1) * ML_DIM)
        q = q_ref[:, hs]
        k = k_ref[:, hs]
        v = v_ref[:, hs]
        n = n_ref[:, h, :]
        qk = jnp.sum(q * k, axis=1, keepdims=True)
        qn = jnp.sum(q * n, axis=1, keepdims=True)
        sw = qk * e_i[:, h:h + 1]
        den = aw[:, h:h + 1] * qn + sw
        inv = 1.0 / jnp.maximum(jnp.abs(den), floor[:, h:h + 1])
        no_ref[:, h, :] = decay[:, h:h + 1] * n + w[:, h:h + 1] * k
        q_b = q.astype(BF16)
        wk_t = jnp.concatenate([w[:, h:h + 1] * k, pad], axis=0).T.astype(BF16)
        v_rows = jnp.concatenate([v, pad], axis=0).astype(BF16)
        nums = []
        for b in range(bt):
            C = c_ref[b, h]
            q_c = _mm(q_b, C.astype(BF16))[b:b + 1, :]
            nums.append(aw[b:b + 1, h:h + 1] * q_c)
            outer = _mm(jnp.where(lane == b, wk_t, jnp.zeros_like(wk_t)), v_rows)
            co_ref[b, h] = decay[b:b + 1, h:h + 1] * C + outer
        num = jnp.concatenate(nums, axis=0) + sw * v
        h_ref[:, hs] = (num * inv * og_ref[:, hs]).astype(h_ref.dtype)


def _mlstm_decode(mq, mk, mv, og, gif, state_c, state_n, state_m, bt):
    nbatch = mq.shape[0]
    b2 = lambda i: (i, 0)
    b3 = lambda i: (i, 0, 0)
    b4 = lambda i: (i, 0, 0, 0)
    return pl.pallas_call(
        _mlstm_decode_kernel,
        grid=(nbatch // bt,),
        in_specs=[
            pl.BlockSpec((bt, ML_W), b2),
            pl.BlockSpec((bt, ML_W), b2),
            pl.BlockSpec((bt, ML_W), b2),
            pl.BlockSpec((bt, ML_W), b2),
            pl.BlockSpec((bt, LANES), b2),
            pl.BlockSpec((bt, ML_HEADS, ML_DIM, ML_DIM), b4),
            pl.BlockSpec((bt, ML_HEADS, ML_DIM), b3),
            pl.BlockSpec((bt, ML_HEADS), b2),
        ],
        out_specs=(
            pl.BlockSpec((bt, ML_W), b2),
            pl.BlockSpec((bt, ML_HEADS, ML_DIM, ML_DIM), b4),
            pl.BlockSpec((bt, ML_HEADS, ML_DIM), b3),
            pl.BlockSpec((bt, ML_HEADS), b2),
        ),
        out_shape=(
            jax.ShapeDtypeStruct((nbatch, ML_W), BF16),
            jax.ShapeDtypeStruct((nbatch, ML_HEADS, ML_DIM, ML_DIM), F32),
            jax.ShapeDtypeStruct((nbatch, ML_HEADS, ML_DIM), F32),
            jax.ShapeDtypeStruct((nbatch, ML_HEADS), F32),
        ),
        compiler_params=pltpu.CompilerParams(dimension_semantics=("parallel",)),
        name="mlstm_decode",
    )(mq, mk, mv, og, gif, state_c, state_n, state_m)


def _rope_tables(pos):
    half = HEAD_DIM // 2
    inv = 1.0 / (ROPE_THETA ** (jnp.arange(half, dtype=F32) / half))
    ang = pos.astype(F32)[:, None] * inv[None, :]
    cos = jnp.cos(ang)
    sin = jnp.sin(ang)
    cos_t = jnp.concatenate([cos, cos, cos, cos], axis=1)
    sin_t = jnp.concatenate([-sin, sin, -sin, sin], axis=1)
    return cos_t, sin_t


def kernel(x_prompt, x_sample, cache_k, cache_v, state_C, state_n, state_m, p_prompt, p_sample, norm_mix, w_in, b_in, attn_sinks, w_attn_up, w_ml_up, w_o, norm_mlp, w_ff1, w_ff2, norm_ple, w_ple_gate, w_ple_proj, norm_final):
    assert w_in.shape[0] == 1, "single layer"
    batch, seq, _ = x_prompt.shape
    nbatch = x_sample.shape[0]
    assert x_sample.shape[1] == 1

    w_in0 = w_in[0]
    b_in0 = b_in[0]
    w_proj, w_gate, w_au, w_mu, w_out, w_f1, w_f2, w_pg, w_pp = _weight_cast(
        w_in0.T, (w_attn_up[0], w_ml_up[0], w_o[0], w_ff1[0], w_ff2[0], w_ple_gate[0], w_ple_proj[0]))
    b_proj = b_in0[:N_PROJ][None, :]
    b_gate = b_in0[OFF_GA:][None, :]
    g_mix = norm_mix[0][None, :]
    out_weights = (
        g_mix, w_gate, b_gate, w_au, w_mu, w_out,
        norm_mlp[0][None, :], w_f1, w_f2,
        norm_ple[0][None, :], w_pg, w_pp, norm_final[None, :],
    )
    sinks = attn_sinks[0]

    cos_p, sin_p = _rope_tables(jnp.arange(seq, dtype=jnp.int32))
    xp = x_prompt.reshape(batch * seq, D_MODEL)
    y_prompt, k_last, v_last, cst, nst, mst = _prompt_fused(
        sinks, xp, p_prompt[0].reshape(batch * seq, D_PLE), cos_p, sin_p, g_mix, w_proj, b_proj, out_weights,
        batch, seq, tm=PROMPT_TILE)
    y_prompt = y_prompt.reshape(batch, seq, D_MODEL)
    k_prompt = k_last.reshape(1, batch, WINDOW, N_KV_HEADS, HEAD_DIM)
    v_prompt = v_last.reshape(1, batch, WINDOW, N_KV_HEADS, HEAD_DIM)
    c_prompt = cst[None]
    n_prompt = nst[:, :ML_HEADS, :][None]
    m_prompt = mst[:, :ML_HEADS, 0][None]

    cos_s, sin_s = _rope_tables(jnp.full((nbatch,), PAST_LEN, dtype=jnp.int32))
    xs = x_sample.reshape(nbatch, D_MODEL)
    qs, _, _, ks, vs, mqs, mks, mvs, ogs, gifs, _ = _in_proj(xs, g_mix, w_proj, b_proj, cos_s, sin_s, tm=nbatch,
                                                          m_dtype=F32, feature_major=False)
    q4 = qs.reshape(nbatch, N_KV_HEADS, GQA_GROUP, 1, HEAD_DIM)
    sel = jnp.eye(N_KV_HEADS, dtype=BF16)[None, :, None, :, None]
    qmat = (q4 * sel).reshape(nbatch, N_Q_HEADS, ATTN_KV)
    to_fp = lambda c: c[0].transpose(0, 2, 3, 1).reshape(nbatch, ATTN_KV, WINDOW)
    from_fp = lambda c: c.reshape(nbatch, N_KV_HEADS, HEAD_DIM, WINDOW).transpose(0, 3, 1, 2)[None]
    o_full, k_s, v_s = _attn_decode(sinks[:, None], qmat, ks, vs, to_fp(cache_k), to_fp(cache_v), bt=DECODE_ROWS)
    o5 = o_full.reshape(nbatch, N_KV_HEADS, GQA_GROUP, N_KV_HEADS, HEAD_DIM)
    a_s = jnp.stack([o5[:, kv, :, kv, :] for kv in range(N_KV_HEADS)], axis=1).reshape(nbatch, ATTN_Q).astype(BF16)
    h_s, c_s, n_s, m_s = _mlstm_decode(mqs, mks, mvs, ogs, gifs, state_C[0], state_n[0], state_m[0], bt=DECODE_ROWS)
    y_sample = _out_proj(xs, a_s, h_s, p_sample[0].reshape(nbatch, D_PLE), out_weights, tm=nbatch)
    y_sample = y_sample.reshape(nbatch, 1, D_MODEL)
    k_sample = from_fp(k_s)
    v_sample = from_fp(v_s)

    return (y_prompt, y_sample, k_prompt, v_prompt, c_prompt, n_prompt, m_prompt,
            k_sample, v_sample, c_s[None], n_s[None], m_s[None])
```

```python
import functools

import numpy as np
import jax
import jax.numpy as jnp
from jax import lax
from jax.experimental import pallas as pl
from jax.experimental.pallas import tpu as pltpu

F32 = jnp.float32
BF16 = jnp.bfloat16

D_MODEL = 1024
HEAD_DIM = 64
N_Q_HEADS = 8
N_KV_HEADS = 2
GQA_GROUP = N_Q_HEADS // N_KV_HEADS
WINDOW = 128
ROPE_THETA = 10000.0
ML_HEADS = 4
ML_DIM = 128
ML_CHUNK = 128
D_FF = 4 * D_MODEL
D_PLE = 256
EPS = 1e-6
PAST_LEN = 16384

ATTN_Q = N_Q_HEADS * HEAD_DIM
ATTN_KV = N_KV_HEADS * HEAD_DIM
ML_W = ML_HEADS * ML_DIM
LANES = 128

OFF_AQ = 0
OFF_AK = OFF_AQ + ATTN_Q
OFF_AV = OFF_AK + ATTN_KV
OFF_MQ = OFF_AV + ATTN_KV
OFF_MK = OFF_MQ + ML_W
OFF_MV = OFF_MK + ML_W
OFF_MO = OFF_MV + ML_W
OFF_MI = OFF_MO + ML_W
OFF_MF = OFF_MI + ML_HEADS
OFF_GA = OFF_MF + ML_HEADS
OFF_GM = OFF_GA + D_MODEL
D_IN = OFF_GM + D_MODEL
N_PROJ = OFF_MI + LANES

VMEM_LIMIT = 60 * 1024 * 1024
PROMPT_TILE = 256
DECODE_ROWS = 16

def _rms(x, g):
    r = lax.rsqrt(jnp.mean(x * x, axis=-1, keepdims=True) + EPS)
    return (x * r) * g


def _mm(a, b):
    return jnp.dot(a, b, preferred_element_type=F32)


def _mm_nt(a, b):
    return lax.dot_general(a, b, (((1,), (1,)), ((), ())), preferred_element_type=F32)


def _mm_tn(a, b):
    return lax.dot_general(a, b, (((0,), (0,)), ((), ())), preferred_element_type=F32)


def _log_sigmoid(x):
    return jnp.minimum(x, 0.0) - jnp.log1p(jnp.exp(-jnp.abs(x)))


def _weave(*streams, delays=None):
    delays = delays or [0] * len(streams)
    live = list(zip(streams, delays))
    rnd = 0
    while live:
        for g, d in list(live):
            if rnd >= d:
                try:
                    next(g)
                except StopIteration:
                    live.remove((g, d))
        rnd += 1


WEIGHT_ROW_SPLIT = 8


def _weight_cast_kernel(*refs):
    n_plain = (len(refs) - 3) // 2
    w_in_t_ref, plain_in = refs[0], refs[1:1 + n_plain]
    w_proj_ref, w_gate_ref, plain_out = refs[1 + n_plain], refs[2 + n_plain], refs[3 + n_plain:]
    for g in range(N_PROJ // LANES):
        w_proj_ref[:, g * LANES:(g + 1) * LANES] = w_in_t_ref[g * LANES:(g + 1) * LANES, :].T.astype(BF16)
    for g in range(2 * D_MODEL // LANES):
        w_gate_ref[:, g * LANES:(g + 1) * LANES] = w_in_t_ref[OFF_GA + g * LANES:OFF_GA + (g + 1) * LANES, :].T.astype(BF16)
    for src, dst in zip(plain_in, plain_out):
        dst[...] = src[...].astype(BF16)


def _weight_cast(w_in_t, plain):
    split = WEIGHT_ROW_SPLIT
    assert D_MODEL // split == LANES
    rows_blk = lambda w: pl.BlockSpec((w.shape[0] // split, w.shape[1]), lambda i: (i, 0))
    out_shapes = (jax.ShapeDtypeStruct((D_MODEL, N_PROJ), BF16), jax.ShapeDtypeStruct((D_MODEL, 2 * D_MODEL), BF16),
                  *(jax.ShapeDtypeStruct(w.shape, BF16) for w in plain))
    return pl.pallas_call(
        _weight_cast_kernel,
        grid=(split,),
        in_specs=[pl.BlockSpec((D_IN, LANES), lambda i: (0, i))] + [rows_blk(w) for w in plain],
        out_specs=tuple(rows_blk(s) for s in out_shapes),
        out_shape=out_shapes,
        compiler_params=pltpu.CompilerParams(dimension_semantics=("parallel",), vmem_limit_bytes=VMEM_LIMIT),
        name="weight_cast",
    )(w_in_t, *plain)


def _in_proj_stages(x, g_ref, w_ref, b_ref, cos_ref, sin_ref,
                    q_ref, k_ref, v_ref, kl_ref, vl_ref, mq_ref, mk_ref, mv_ref, og_ref, gif_ref, gt_ref,
                    *, feature_major):
    xn = _rms(x, g_ref[...]).astype(BF16)
    tm = x.shape[0]

    def proj(lo, hi):
        return _mm(xn, w_ref[:, lo:hi]) + b_ref[:, lo:hi]

    cos = cos_ref[...]
    sin = sin_ref[...]
    lane = lax.broadcasted_iota(jnp.int32, (1, LANES), 1)
    first_half = (lane % HEAD_DIM) < (HEAD_DIM // 2)

    def rope(z):
        partner = jnp.where(first_half, pltpu.roll(z, LANES - HEAD_DIM // 2, 1), pltpu.roll(z, HEAD_DIM // 2, 1))
        return z * cos + partner * sin

    zq = proj(OFF_AQ, OFF_AK)
    for c in range(ATTN_Q // LANES):
        sl = slice(c * LANES, (c + 1) * LANES)
        qc = rope(zq[:, sl]) * (HEAD_DIM ** -0.5)
        if feature_major:
            q_ref[sl, :] = qc.T.astype(q_ref.dtype)
        else:
            kv = (2 * c) // GQA_GROUP
            own = (lane // HEAD_DIM) == kv
            swapped = pltpu.roll(qc, HEAD_DIM, 1)
            for j, src in enumerate((qc, swapped) if kv == 0 else (swapped, qc)):
                hd = 2 * c + j
                q_ref[:, hd * LANES:(hd + 1) * LANES] = jnp.where(own, src, 0.0).astype(q_ref.dtype)
    yield
    zkv = proj(OFF_AK, OFF_MQ)
    k = rope(zkv[:, :ATTN_KV])
    v = zkv[:, ATTN_KV:]
    k_ref[...] = k.astype(k_ref.dtype)
    v_ref[...] = (v.T if feature_major else v).astype(v_ref.dtype)
    kl_ref[...] = k[tm - WINDOW:, :]
    vl_ref[...] = v[tm - WINDOW:, :]
    yield
    for z_off, dst in ((OFF_MQ, mq_ref), (OFF_MV, mv_ref)):
        z = proj(z_off, z_off + ML_W)
        if feature_major:
            for h in range(ML_HEADS):
                hs = slice(h * ML_DIM, (h + 1) * ML_DIM)
                dst[0, hs, :] = z[:, hs].T.astype(dst.dtype)
        else:
            dst[...] = z.astype(dst.dtype)
        yield
    mk_ref[...] = (proj(OFF_MK, OFF_MV) * (ML_DIM ** -0.5)).astype(mk_ref.dtype)
    yield
    og_ref[...] = jax.nn.sigmoid(proj(OFF_MO, OFF_MI))
    yield
    zg = proj(OFF_MI, N_PROJ)
    gif = jnp.where(lane < ML_HEADS, zg, jnp.where(lane < 2 * ML_HEADS, _log_sigmoid(zg), 0.0))
    if feature_major:
        L = ML_CHUNK
        upper = (lax.broadcasted_iota(jnp.int32, (L, L), 0) <= lax.broadcasted_iota(jnp.int32, (L, L), 1)).astype(F32)
        row8 = lax.broadcasted_iota(jnp.int32, (8, 1), 0)
        fill = jnp.zeros((L - 16, L), F32)
        for c in range(tm // L):
            cs = slice(c * L, (c + 1) * L)
            g_rows = gif[cs, :].T[0:8, :]
            cum = jnp.dot(g_rows, upper, precision=lax.Precision.HIGHEST, preferred_element_type=F32)
            rows = jnp.where(row8 < ML_HEADS, g_rows, cum)
            gt_ref[0, :, cs] = rows
            diff = rows - pltpu.roll(rows, ML_HEADS, 0)
            gif_ref[cs, :] = jnp.concatenate([rows, diff, fill], axis=0).T
    else:
        gif_ref[...] = gif
        gt_ref[0] = jnp.zeros(gt_ref.shape[1:], F32)
    yield


def _in_proj_kernel(x_ref, *refs, feature_major):
    _weave(_in_proj_stages(x_ref[...], *refs, feature_major=feature_major))


def _in_proj(x, g, w, b, cos, sin, tm, m_dtype, feature_major):
    rows = x.shape[0]
    seq = cos.shape[0]
    tps = seq // tm
    nseq = rows // seq
    row = lambda i: (i, 0)
    const = lambda i: (0, 0)
    pos = lambda i: (i % tps, 0)
    last = lambda i: (i // tps, 0)
    row_out = lambda n, dt: (jax.ShapeDtypeStruct((rows, n), dt), pl.BlockSpec((tm, n), row))
    col_out = lambda n, dt: (jax.ShapeDtypeStruct((n, rows), dt), pl.BlockSpec((n, tm), lambda i: (0, i)))
    att_out = col_out if feature_major else row_out
    seq_out = lambda n, dt: (jax.ShapeDtypeStruct((nseq, n, seq), dt),
                             pl.BlockSpec((1, n, tm), lambda i: (i // tps, 0, i % tps)))
    ml_out = seq_out if feature_major else row_out
    outs = (
        col_out(ATTN_Q, BF16) if feature_major else row_out(N_Q_HEADS * LANES, BF16),
        row_out(ATTN_KV, BF16),
        att_out(ATTN_KV, BF16),
        (jax.ShapeDtypeStruct((nseq * WINDOW, ATTN_KV), F32), pl.BlockSpec((WINDOW, ATTN_KV), last)),
        (jax.ShapeDtypeStruct((nseq * WINDOW, ATTN_KV), F32), pl.BlockSpec((WINDOW, ATTN_KV), last)),
        ml_out(ML_W, m_dtype),
        row_out(ML_W, m_dtype),
        ml_out(ML_W, m_dtype),
        row_out(ML_W, F32),
        row_out(LANES, F32),
        seq_out(8, F32),
    )
    return pl.pallas_call(
        functools.partial(_in_proj_kernel, feature_major=feature_major),
        grid=(rows // tm,),
        in_specs=[
            pl.BlockSpec((tm, D_MODEL), row),
            pl.BlockSpec((1, D_MODEL), const),
            pl.BlockSpec((D_MODEL, N_PROJ), const),
            pl.BlockSpec((1, N_PROJ), const),
            pl.BlockSpec((tm, LANES), pos),
            pl.BlockSpec((tm, LANES), pos),
        ],
        out_specs=tuple(o[1] for o in outs),
        out_shape=tuple(o[0] for o in outs),
        compiler_params=pltpu.CompilerParams(dimension_semantics=("arbitrary",), vmem_limit_bytes=VMEM_LIMIT),
        name="in_proj",
    )(x, g, w, b, cos, sin)


def _attn_stages(sink_ref, qt_ref, kp_ref, kc_ref, vtp_ref, vtc_ref, first_block, emit):
    L = WINDOW
    nsub = kc_ref.shape[0] // L
    key = lax.broadcasted_iota(jnp.int32, (2 * L, L), 0)
    t = lax.broadcasted_iota(jnp.int32, (2 * L, L), 1)
    is_prev = key < L
    allowed = (is_prev & (key >= t)) | (~is_prev & (key - L <= t))
    bias = jnp.where(allowed, 0.0, -jnp.inf)
    bias_first = jnp.where(is_prev & first_block, -jnp.inf, bias)
    zeros_q = jnp.zeros((HEAD_DIM, L), BF16)
    heads = range(N_Q_HEADS)
    for r in range(nsub):
        ts = slice(r * L, (r + 1) * L)
        k_prev = kp_ref[...] if r == 0 else kc_ref[(r - 1) * L:r * L, :]
        vt_prev = vtp_ref[...] if r == 0 else vtc_ref[:, (r - 1) * L:r * L]
        k2 = jnp.concatenate([k_prev, kc_ref[ts, :]], axis=0)
        vt2 = jnp.concatenate([vt_prev, vtc_ref[:, ts]], axis=1)
        b_r = bias_first if r == 0 else bias
        q_op = [jnp.concatenate([qt_ref[hd * HEAD_DIM:(hd + 1) * HEAD_DIM, ts], zeros_q][::1 if hd < GQA_GROUP else -1],
                                axis=0) for hd in heads]
        s = [_mm(k2, q_op[hd]) + b_r for hd in heads]
        yield
        m = [jnp.maximum(jnp.max(s[hd], axis=0, keepdims=True), sink_ref[hd]) for hd in heads]
        yield
        p = [jnp.exp(s[hd] - m[hd]) for hd in heads]
        yield
        den = [jnp.sum(p[hd], axis=0, keepdims=True) + jnp.exp(sink_ref[hd] - m[hd]) for hd in heads]
        o = [_mm(vt2[(hd // GQA_GROUP) * HEAD_DIM:(hd // GQA_GROUP + 1) * HEAD_DIM, :], p[hd].astype(BF16)) / den[hd]
             for hd in heads]
        yield
        emit(r, jnp.concatenate(o, axis=0).T)
        yield


ML_STATE_ROWS = ML_DIM + 8


def _mlstm_stages(chains, emit):
    L = ML_CHUNK
    s_idx = lax.broadcasted_iota(jnp.int32, (L, L), 0)
    t_idx = lax.broadcasted_iota(jnp.int32, (L, L), 1)
    causal = s_idx <= t_idx
    n = range(len(chains))
    dlog = [jnp.where(causal, ch["b_r"] + ch["u_c"], -jnp.inf) for ch in chains]
    yield
    qk = [_mm(ch["k"], ch["qt"]) for ch in chains]
    inter = [_mm(ch["state"].astype(BF16), ch["qt"]) for ch in chains]
    yield
    a = [ch["b_r"] + ch["m_prev"] for ch in chains]
    m_t = [jnp.maximum(a[i], jnp.max(dlog[i], axis=0, keepdims=True)) for i in n]
    yield
    sw = [qk[i] * jnp.exp(dlog[i] - m_t[i]) for i in n]
    yield
    sv = [_mm(chains[i]["vt"], sw[i].astype(BF16)) for i in n]
    yield
    hs = []
    for i in n:
        aw = jnp.exp(a[i] - m_t[i])
        num = aw * inter[i][:ML_DIM] + sv[i]
        den = aw * inter[i][ML_DIM:ML_DIM + 1] + jnp.sum(sw[i], axis=0, keepdims=True)
        hh = num / jnp.maximum(jnp.abs(den), jnp.exp(-m_t[i]))
        hs.append(hh.T * chains[i]["og"])
    yield
    pad = jnp.zeros((ML_STATE_ROWS - ML_DIM - 1, L), F32)
    new_state, new_m = [], []
    for i in n:
        ch = chains[i]
        b_end = ch["b_r"][:, L - 1:L]
        g = b_end - ch["b_r"] + ch["ig_r"]
        m_new = jnp.maximum(b_end + ch["m_prev"], jnp.max(g, axis=1, keepdims=True))
        decay = jnp.exp(b_end + ch["m_prev"] - m_new)
        w = jnp.exp(g - m_new)
        v_aug = jnp.concatenate([ch["vt"].astype(F32) * w, w, pad], axis=0).astype(BF16)
        new_state.append(decay * ch["state"] + _mm(v_aug, ch["k"]))
        new_m.append(m_new)
    emit(hs, new_state, new_m)
    yield


DENSE_COLS = 512


def _dense_stages(x, a, hm, p, gmix_ref, wg_ref, bg_ref, wau_ref, wmu_ref, wo_ref,
                  gmlp_ref, wff1_ref, wff2_ref, gple_ref, wpg_ref, wpp_ref, gfin_ref, emit):
    nc = DENSE_COLS
    pieces = lambda n: [slice(c, c + nc) for c in range(0, n, nc)]
    xn = _rms(x, gmix_ref[...]).astype(BF16)
    gates = []
    for cs in pieces(2 * D_MODEL):
        gates.append(jax.nn.sigmoid(_mm(xn, wg_ref[:, cs]) + bg_ref[:, cs]))
        yield
    half = D_MODEL // nc
    mix = []
    for j, cs in enumerate(pieces(D_MODEL)):
        mix.append((gates[j] * _mm(a, wau_ref[:, cs]) + gates[half + j] * _mm(hm, wmu_ref[:, cs])).astype(BF16))
        yield
    mix = jnp.concatenate(mix, axis=1)
    h = []
    for cs in pieces(D_MODEL):
        h.append(x[:, cs] + _mm(mix, wo_ref[:, cs]))
        yield
    h = jnp.concatenate(h, axis=1)
    hn = _rms(h, gmlp_ref[...]).astype(BF16)
    u = []
    for cs in pieces(D_FF):
        u.append(jnp.square(jnp.maximum(_mm(hn, wff1_ref[:, cs]), 0.0)).astype(BF16))
        yield
    u = jnp.concatenate(u, axis=1)
    h2 = []
    for cs in pieces(D_MODEL):
        h2.append(h[:, cs] + _mm(u, wff2_ref[:, cs]))
        yield
    h = jnp.concatenate(h2, axis=1)
    hn = _rms(h, gple_ref[...]).astype(BF16)
    pb = p.astype(BF16)
    h3 = []
    for cs in pieces(D_MODEL):
        h3.append(h[:, cs] + _mm(pb, wpp_ref[:, cs]) * jax.nn.sigmoid(_mm(hn, wpg_ref[:, cs])))
        yield
    emit(_rms(jnp.concatenate(h3, axis=1), gfin_ref[...]))
    yield


def _out_kernel(x_ref, a_ref, hm_ref, p_ref, *rest):
    weights, y_ref = rest[:-1], rest[-1]

    def emit(y):
        y_ref[...] = y

    _weave(_dense_stages(x_ref[...], a_ref[...], hm_ref[...], p_ref[...], *weights, emit=emit))


def _out_proj(x, a, hm, p, weights, tm):
    rows = x.shape[0]
    row = lambda i: (i, 0)
    const = lambda i: (0, 0)
    w_specs = [pl.BlockSpec(w.shape, const, pipeline_mode=pl.Buffered(1)) for w in weights]
    return pl.pallas_call(
        _out_kernel,
        grid=(rows // tm,),
        in_specs=[
            pl.BlockSpec((tm, D_MODEL), row),
            pl.BlockSpec((tm, ATTN_Q), row),
            pl.BlockSpec((tm, ML_W), row),
            pl.BlockSpec((tm, D_PLE), row),
        ] + w_specs,
        out_specs=pl.BlockSpec((tm, D_MODEL), row),
        out_shape=jax.ShapeDtypeStruct((rows, D_MODEL), F32),
        compiler_params=pltpu.CompilerParams(dimension_semantics=("parallel",), vmem_limit_bytes=VMEM_LIMIT),
        name="out_proj",
    )(x, a, hm, p, *weights)


N_FUSED_WEIGHTS = 13
N_PROJ_SLOTS = 9
PROJ_DELAY = 0


def _prompt_kernel(sink_ref, x0_ref, cos_ref, sin_ref, x2_ref, p_ref, gmix_ref, wp_ref, bp_ref, *rest, tiles_per_seq):
    weights = rest[:N_FUSED_WEIGHTS]
    y_ref, kl_ref, vl_ref, cst_ref, nst_ref, mst_ref = rest[N_FUSED_WEIGHTS:N_FUSED_WEIGHTS + 6]
    scr = rest[N_FUSED_WEIGHTS + 6:]
    proj_scr, (kprev_scr, vtprev_scr, a_scr, h_scr, c_scr, m_scr) = scr[:N_PROJ_SLOTS], scr[N_PROJ_SLOTS:]
    i = pl.program_id(0)
    n_tiles = pl.num_programs(0) - 2
    t1 = jnp.clip(i - 1, 0, n_tiles - 1)
    tile_in_seq = t1 % tiles_per_seq
    s0, s1 = i % 2, (i + 1) % 2
    L = ML_CHUNK
    tm = x0_ref.shape[0]

    @pl.when(i == 0)
    def _():
        for r in scr:
            r[...] = jnp.zeros_like(r)

    @pl.when(tile_in_seq == 0)
    def _():
        c_scr[...] = jnp.zeros_like(c_scr)
        m_scr[...] = jnp.zeros_like(m_scr)

    def projection():
        outs = [r.at[s0] for r in proj_scr]
        yield from _in_proj_stages(x0_ref[...], gmix_ref, wp_ref, bp_ref, cos_ref, sin_ref,
                                   outs[0], outs[1], outs[2], kl_ref, vl_ref, *outs[3:], feature_major=True)

    def branches():
        qt_ref, kc_ref, vtc_ref, mqt_ref, mk_ref, mvt_ref, og_ref, gif_ref, gt_ref = [r.at[s1] for r in proj_scr]
        slot = s1

        def emit_attn(r, blk):
            a_scr[slot, r * WINDOW:(r + 1) * WINDOW, :] = blk.astype(a_scr.dtype)

        yield from _attn_stages(sink_ref, qt_ref, kprev_scr, kc_ref, vtprev_scr, vtc_ref, tile_in_seq == 0, emit_attn)
        kprev_scr[...] = kc_ref[tm - WINDOW:, :]
        vtprev_scr[...] = vtc_ref[:, tm - WINDOW:]
        hsl = lambda h: slice(h * ML_DIM, (h + 1) * ML_DIM)
        carry = dict(state=[c_scr[h] for h in range(ML_HEADS)], m=[m_scr[h:h + 1, :] for h in range(ML_HEADS)])
        for c in range(tm // L):
            cs = slice(c * L, (c + 1) * L)

            def emit_ml(hs, new_state, new_m, cs=cs):
                for h in range(ML_HEADS):
                    h_scr[slot, cs, hsl(h)] = hs[h].astype(h_scr.dtype)
                carry["state"], carry["m"] = new_state, new_m

            chains = [dict(qt=mqt_ref[0, hsl(h), cs], k=mk_ref[cs, hsl(h)], vt=mvt_ref[0, hsl(h), cs],
                           og=og_ref[cs, hsl(h)], ig_r=gt_ref[0, h:h + 1, cs],
                           b_r=gt_ref[0, ML_HEADS + h:ML_HEADS + h + 1, cs],
                           u_c=gif_ref[cs, 2 * ML_HEADS + h:2 * ML_HEADS + h + 1],
                           state=carry["state"][h], m_prev=carry["m"][h]) for h in range(ML_HEADS)]
            yield from _mlstm_stages(chains, emit_ml)
        for h in range(ML_HEADS):
            c_scr[h] = carry["state"][h]
            m_scr[h:h + 1, :] = carry["m"][h]

    def dense():
        def emit_y(y):
            y_ref[...] = y

        return _dense_stages(x2_ref[...], a_scr[s0], h_scr[s0], p_ref[...], *weights, emit=emit_y)

    @pl.when(i < 2)
    def _():
        _weave(branches(), projection())

    @pl.when((i >= 2) & (i < n_tiles))
    def _():
        _weave(branches(), dense(), projection())

    @pl.when(i >= n_tiles)
    def _():
        _weave(branches(), dense())

    @pl.when((tile_in_seq == tiles_per_seq - 1) & (i >= 1) & (i <= n_tiles))
    def _():
        nst_ref[...] = jnp.zeros_like(nst_ref)
        for h in range(ML_HEADS):
            final = c_scr[h]
            cst_ref[0, h] = final[:ML_DIM].T
            nst_ref[0, h:h + 1, :] = final[ML_DIM:ML_DIM + 1]
        mst_ref[0] = m_scr[...]


def _prompt_fused(sinks, x, p, cos, sin, g_mix, w_proj, b_proj, weights, batch, seq, tm):
    assert len(weights) == N_FUSED_WEIGHTS
    rows = batch * seq
    n_tiles = rows // tm
    tps = seq // tm
    t0 = lambda i: jnp.minimum(i, n_tiles - 1)
    t1 = lambda i: jnp.clip(i - 1, 0, n_tiles - 1)
    t2 = lambda i: jnp.maximum(i - 2, 0)
    const = lambda i: (0, 0)
    single = lambda a: pl.BlockSpec(a.shape, const, pipeline_mode=pl.Buffered(1))
    slot = lambda shape, dt: pltpu.VMEM((2,) + shape, dt)
    return pl.pallas_call(
        functools.partial(_prompt_kernel, tiles_per_seq=tps),
        grid=(n_tiles + 2,),
        in_specs=[
            pl.BlockSpec(memory_space=pltpu.SMEM),
            pl.BlockSpec((tm, D_MODEL), lambda i: (t0(i), 0)),
            pl.BlockSpec((tm, LANES), lambda i: (t0(i) % tps, 0)),
            pl.BlockSpec((tm, LANES), lambda i: (t0(i) % tps, 0)),
            pl.BlockSpec((tm, D_MODEL), lambda i: (t2(i), 0)),
            pl.BlockSpec((tm, D_PLE), lambda i: (t2(i), 0)),
            single(g_mix), single(w_proj), single(b_proj),
        ] + [single(w) for w in weights],
        out_specs=(
            pl.BlockSpec((tm, D_MODEL), lambda i: (t2(i), 0)),
            pl.BlockSpec((WINDOW, ATTN_KV), lambda i: (t0(i) // tps, 0)),
            pl.BlockSpec((WINDOW, ATTN_KV), lambda i: (t0(i) // tps, 0)),
            pl.BlockSpec((1, ML_HEADS, ML_DIM, ML_DIM), lambda i: (t1(i) // tps, 0, 0, 0)),
            pl.BlockSpec((1, 8, LANES), lambda i: (t1(i) // tps, 0, 0)),
            pl.BlockSpec((1, 8, LANES), lambda i: (t1(i) // tps, 0, 0)),
        ),
        out_shape=(
            jax.ShapeDtypeStruct((rows, D_MODEL), F32),
            jax.ShapeDtypeStruct((batch * WINDOW, ATTN_KV), F32),
            jax.ShapeDtypeStruct((batch * WINDOW, ATTN_KV), F32),
            jax.ShapeDtypeStruct((batch, ML_HEADS, ML_DIM, ML_DIM), F32),
            jax.ShapeDtypeStruct((batch, 8, LANES), F32),
            jax.ShapeDtypeStruct((batch, 8, LANES), F32),
        ),
        scratch_shapes=[
            slot((ATTN_Q, tm), BF16), slot((tm, ATTN_KV), BF16), slot((ATTN_KV, tm), BF16),
            slot((1, ML_W, tm), BF16), slot((tm, ML_W), BF16), slot((1, ML_W, tm), BF16),
            slot((tm, ML_W), F32), slot((tm, LANES), F32), slot((1, 8, tm), F32),
            pltpu.VMEM((WINDOW, ATTN_KV), BF16), pltpu.VMEM((ATTN_KV, WINDOW), BF16),
            slot((tm, ATTN_Q), BF16), slot((tm, ML_W), BF16),
            pltpu.VMEM((ML_HEADS, ML_STATE_ROWS, ML_DIM), F32), pltpu.VMEM((8, LANES), F32),
        ],
        compiler_params=pltpu.CompilerParams(dimension_semantics=("arbitrary",), vmem_limit_bytes=VMEM_LIMIT),
        name="prompt_fused",
    )(sinks, x, cos, sin, x, p, g_mix, w_proj, b_proj, *weights)


def _attn_decode_kernel(sink_ref, qm_ref, kn_ref, vn_ref, ck_ref, cv_ref, o_ref, ko_ref, vo_ref):
    bt = qm_ref.shape[0]
    qm = qm_ref[...]
    kn = kn_ref[...]
    vn = vn_ref[...]
    ck = ck_ref[...]
    cv = cv_ref[...]
    sink = sink_ref[...]
    s = jnp.einsum('bhc,bck->bhk', qm, ck.astype(BF16), preferred_element_type=F32)
    s_new = jnp.sum(qm.astype(F32) * kn[:, None, :], axis=2, keepdims=True)
    m = jnp.maximum(jnp.maximum(jnp.max(s, axis=2, keepdims=True), s_new), sink)
    p = jnp.exp(s - m)
    p_new = jnp.exp(s_new - m)
    den = jnp.sum(p, axis=2, keepdims=True) + p_new + jnp.exp(sink - m)
    o = jnp.einsum('bhk,bck->bhc', p.astype(BF16), cv.astype(BF16), preferred_element_type=F32)
    o_ref[...] = (o + p_new * vn[:, None, :]) / den
    pad = jnp.zeros((LANES - bt, ATTN_KV), F32)
    kn_t = jnp.concatenate([kn, pad], axis=0).T
    vn_t = jnp.concatenate([vn, pad], axis=0).T
    newest = lax.broadcasted_iota(jnp.int32, (ATTN_KV, WINDOW), 1) == WINDOW - 1
    for b in range(bt):
        ko_ref[b] = jnp.where(newest, kn_t[:, b:b + 1], pltpu.roll(ck[b], WINDOW - 1, 1))
        vo_ref[b] = jnp.where(newest, vn_t[:, b:b + 1], pltpu.roll(cv[b], WINDOW - 1, 1))


def _attn_decode(sinks_col, qmat, k_new, v_new, cache_k, cache_v, bt):
    nbatch = qmat.shape[0]
    b3 = lambda i: (i, 0, 0)
    b2 = lambda i: (i, 0)
    return pl.pallas_call(
        _attn_decode_kernel,
        grid=(nbatch // bt,),
        in_specs=[
            pl.BlockSpec((N_Q_HEADS, 1), lambda i: (0, 0)),
            pl.BlockSpec((bt, N_Q_HEADS, ATTN_KV), b3),
            pl.BlockSpec((bt, ATTN_KV), b2),
            pl.BlockSpec((bt, ATTN_KV), b2),
            pl.BlockSpec((bt, WINDOW, ATTN_KV), b3),
            pl.BlockSpec((bt, WINDOW, ATTN_KV), b3),
        ],
        out_specs=(
            pl.BlockSpec((bt, N_Q_HEADS, ATTN_KV), b3),
            pl.BlockSpec((bt, WINDOW, ATTN_KV), b3),
            pl.BlockSpec((bt, WINDOW, ATTN_KV), b3),
        ),
        out_shape=(
            jax.ShapeDtypeStruct((nbatch, N_Q_HEADS, ATTN_KV), F32),
            jax.ShapeDtypeStruct((nbatch, WINDOW, ATTN_KV), F32),
            jax.ShapeDtypeStruct((nbatch, WINDOW, ATTN_KV), F32),
        ),
        compiler_params=pltpu.CompilerParams(dimension_semantics=("parallel",)),
        name="attn_decode",
    )(sinks_col, qmat, k_new, v_new, cache_k, cache_v)


def _mlstm_decode_kernel(q_ref, k_ref, v_ref, og_ref, gif_ref, c_ref, n_ref, m_ref, h_ref, co_ref, no_ref, mo_ref):
    bt = q_ref.shape[0]
    gif = gif_ref[...]
    ig = gif[:, 0:ML_HEADS]
    lf = gif[:, ML_HEADS:2 * ML_HEADS]
    m_prev = m_ref[...]
    a = lf + m_prev
    m_t = jnp.maximum(a, ig)
    aw = jnp.exp(a - m_t)
    e_i = jnp.exp(ig - m_t)
    floor = jnp.exp(-m_t)
    m_new = jnp.maximum(a, ig)
    decay = jnp.exp(a - m_new)
    w = jnp.exp(ig - m_new)
    mo_ref[...] = m_new
    pad = jnp.zeros((LANES - bt, ML_DIM), F32)
    lane = lax.broadcasted_iota(jnp.int32, (ML_DIM, LANES), 1)
    for h in range(ML_HEADS):
        hs = slice(h * ML_DIM, (h + 1) * ML_DIM)
        q = q_ref[:, hs]
        k = k_ref[:, hs]
        v = v_ref[:, hs]
        n = n_ref[:, h, :]
        qk = jnp.sum(q * k, axis=1, keepdims=True)
        qn = jnp.sum(q * n, axis=1, keepdims=True)
        sw = qk * e_i[:, h:h + 1]
        den = aw[:, h:h + 1] * qn + sw
        inv = 1.0 / jnp.maximum(jnp.abs(den), floor[:, h:h + 1])
        no_ref[:, h, :] = decay[:, h:h + 1] * n + w[:, h:h + 1] * k
        q_b = q.astype(BF16)
        wk_t = jnp.concatenate([w[:, h:h + 1] * k, pad], axis=0).T.astype(BF16)
        v_rows = jnp.concatenate([v, pad], axis=0).astype(BF16)
        nums = []
        for b in range(bt):
            C = c_ref[b, h]
            q_c = _mm(q_b, C.astype(BF16))[b:b + 1, :]
            nums.append(aw[b:b + 1, h:h + 1] * q_c)
            outer = _mm(jnp.where(lane == b, wk_t, jnp.zeros_like(wk_t)), v_rows)
            co_ref[b, h] = decay[b:b + 1, h:h + 1] * C + outer
        num = jnp.concatenate(nums, axis=0) + sw * v
        h_ref[:, hs] = (num * inv * og_ref[:, hs]).astype(h_ref.dtype)


def _mlstm_decode(mq, mk, mv, og, gif, state_c, state_n, state_m, bt):
    nbatch = mq.shape[0]
    b2 = lambda i: (i, 0)
    b3 = lambda i: (i, 0, 0)
    b4 = lambda i: (i, 0, 0, 0)
    return pl.pallas_call(
        _mlstm_decode_kernel,
        grid=(nbatch // bt,),
        in_specs=[
            pl.BlockSpec((bt, ML_W), b2),
            pl.BlockSpec((bt, ML_W), b2),
            pl.BlockSpec((bt, ML_W), b2),
            pl.BlockSpec((bt, ML_W), b2),
            pl.BlockSpec((bt, LANES), b2),
            pl.BlockSpec((bt, ML_HEADS, ML_DIM, ML_DIM), b4),
            pl.BlockSpec((bt, ML_HEADS, ML_DIM), b3),
            pl.BlockSpec((bt, ML_HEADS), b2),
        ],
        out_specs=(
            pl.BlockSpec((bt, ML_W), b2),
            pl.BlockSpec((bt, ML_HEADS, ML_DIM, ML_DIM), b4),
            pl.BlockSpec((bt, ML_HEADS, ML_DIM), b3),
            pl.BlockSpec((bt, ML_HEADS), b2),
        ),
        out_shape=(
            jax.ShapeDtypeStruct((nbatch, ML_W), BF16),
            jax.ShapeDtypeStruct((nbatch, ML_HEADS, ML_DIM, ML_DIM), F32),
            jax.ShapeDtypeStruct((nbatch, ML_HEADS, ML_DIM), F32),
            jax.ShapeDtypeStruct((nbatch, ML_HEADS), F32),
        ),
        compiler_params=pltpu.CompilerParams(dimension_semantics=("parallel",)),
        name="mlstm_decode",
    )(mq, mk, mv, og, gif, state_c, state_n, state_m)


def _rope_tables(pos):
    half = HEAD_DIM // 2
    inv = (1.0 / (ROPE_THETA ** (np.arange(half, dtype=np.float32) / half))).astype(np.float32)
    ang = pos.astype(np.float32)[:, None] * inv[None, :]
    cos = np.cos(ang)
    sin = np.sin(ang)
    cos_t = np.concatenate([cos, cos, cos, cos], axis=1)
    sin_t = np.concatenate([-sin, sin, -sin, sin], axis=1)
    return jnp.asarray(cos_t, F32), jnp.asarray(sin_t, F32)


def kernel(x_prompt, x_sample, cache_k, cache_v, state_C, state_n, state_m, p_prompt, p_sample, norm_mix, w_in, b_in, attn_sinks, w_attn_up, w_ml_up, w_o, norm_mlp, w_ff1, w_ff2, norm_ple, w_ple_gate, w_ple_proj, norm_final):
    assert w_in.shape[0] == 1, "single layer"
    batch, seq, _ = x_prompt.shape
    nbatch = x_sample.shape[0]
    assert x_sample.shape[1] == 1

    w_in0 = w_in[0]
    b_in0 = b_in[0]
    w_proj, w_gate, w_au, w_mu, w_out, w_f1, w_f2, w_pg, w_pp = _weight_cast(
        w_in0.T, (w_attn_up[0], w_ml_up[0], w_o[0], w_ff1[0], w_ff2[0], w_ple_gate[0], w_ple_proj[0]))
    b_proj = b_in0[:N_PROJ][None, :]
    b_gate = b_in0[OFF_GA:][None, :]
    g_mix = norm_mix[0][None, :]
    out_weights = (
        g_mix, w_gate, b_gate, w_au, w_mu, w_out,
        norm_mlp[0][None, :], w_f1, w_f2,
        norm_ple[0][None, :], w_pg, w_pp, norm_final[None, :],
    )
    sinks = attn_sinks[0]

    cos_p, sin_p = _rope_tables(np.arange(seq, dtype=np.int32))
    xp = x_prompt.reshape(batch * seq, D_MODEL)
    y_prompt, k_last, v_last, cst, nst, mst = _prompt_fused(
        sinks, xp, p_prompt[0].reshape(batch * seq, D_PLE), cos_p, sin_p, g_mix, w_proj, b_proj, out_weights,
        batch, seq, tm=PROMPT_TILE)
    y_prompt = y_prompt.reshape(batch, seq, D_MODEL)
    k_prompt = k_last.reshape(1, batch, WINDOW, N_KV_HEADS, HEAD_DIM)
    v_prompt = v_last.reshape(1, batch, WINDOW, N_KV_HEADS, HEAD_DIM)
    c_prompt = cst[None]
    n_prompt = nst[:, :ML_HEADS, :][None]
    m_prompt = mst[:, :ML_HEADS, 0][None]

    cos_s, sin_s = _rope_tables(np.full((nbatch,), PAST_LEN, dtype=np.int32))
    xs = x_sample.reshape(nbatch, D_MODEL)
    qs, _, _, ks, vs, mqs, mks, mvs, ogs, gifs, _ = _in_proj(xs, g_mix, w_proj, b_proj, cos_s, sin_s, tm=nbatch,
                                                          m_dtype=F32, feature_major=False)
    qmat = qs.reshape(nbatch, N_Q_HEADS, ATTN_KV)
    to_fp = lambda c: c[0].transpose(0, 2, 3, 1).reshape(nbatch, ATTN_KV, WINDOW)
    from_fp = lambda c: c.reshape(nbatch, N_KV_HEADS, HEAD_DIM, WINDOW).transpose(0, 3, 1, 2)[None]
    o_full, k_s, v_s = _attn_decode(sinks[:, None], qmat, ks, vs, to_fp(cache_k), to_fp(cache_v), bt=DECODE_ROWS)
    o5 = o_full.reshape(nbatch, N_KV_HEADS, GQA_GROUP, N_KV_HEADS, HEAD_DIM)
    a_s = jnp.stack([o5[:, kv, :, kv, :] for kv in range(N_KV_HEADS)], axis=1).reshape(nbatch, ATTN_Q).astype(BF16)
    h_s, c_s, n_s, m_s = _mlstm_decode(mqs, mks, mvs, ogs, gifs, state_C[0], state_n[0], state_m[0], bt=DECODE_ROWS)
    y_sample = _out_proj(xs, a_s, h_s, p_sample[0].reshape(nbatch, D_PLE), out_weights, tm=nbatch)
    y_sample = y_sample.reshape(nbatch, 1, D_MODEL)
    k_sample = from_fp(k_s)
    v_sample = from_fp(v_s)

    return (y_prompt, y_sample, k_prompt, v_prompt, c_prompt, n_prompt, m_prompt,
            k_sample, v_sample, c_s[None], n_s[None], m_s[None])
```

```python
import functools

import numpy as np
import jax
import jax.numpy as jnp
from jax import lax
from jax.experimental import pallas as pl
from jax.experimental.pallas import tpu as pltpu

F32 = jnp.float32
BF16 = jnp.bfloat16

D_MODEL = 1024
HEAD_DIM = 64
N_Q_HEADS = 8
N_KV_HEADS = 2
GQA_GROUP = N_Q_HEADS // N_KV_HEADS
WINDOW = 128
ROPE_THETA = 10000.0
ML_HEADS = 4
ML_DIM = 128
ML_CHUNK = 128
D_FF = 4 * D_MODEL
D_PLE = 256
EPS = 1e-6
PAST_LEN = 16384

ATTN_Q = N_Q_HEADS * HEAD_DIM
ATTN_KV = N_KV_HEADS * HEAD_DIM
ML_W = ML_HEADS * ML_DIM
LANES = 128

OFF_AQ = 0
OFF_AK = OFF_AQ + ATTN_Q
OFF_AV = OFF_AK + ATTN_KV
OFF_MQ = OFF_AV + ATTN_KV
OFF_MK = OFF_MQ + ML_W
OFF_MV = OFF_MK + ML_W
OFF_MO = OFF_MV + ML_W
OFF_MI = OFF_MO + ML_W
OFF_MF = OFF_MI + ML_HEADS
OFF_GA = OFF_MF + ML_HEADS
OFF_GM = OFF_GA + D_MODEL
D_IN = OFF_GM + D_MODEL
N_PROJ = OFF_MI + LANES

VMEM_LIMIT = 60 * 1024 * 1024
PROMPT_TILE = 256
DECODE_ROWS = 16

def _rms(x, g):
    r = lax.rsqrt(jnp.mean(x * x, axis=-1, keepdims=True) + EPS)
    return (x * r) * g


def _mm(a, b):
    return jnp.dot(a, b, preferred_element_type=F32)


def _mm_nt(a, b):
    return lax.dot_general(a, b, (((1,), (1,)), ((), ())), preferred_element_type=F32)


def _mm_tn(a, b):
    return lax.dot_general(a, b, (((0,), (0,)), ((), ())), preferred_element_type=F32)


def _log_sigmoid(x):
    return jnp.minimum(x, 0.0) - jnp.log1p(jnp.exp(-jnp.abs(x)))


def _weave(*streams, delays=None):
    delays = delays or [0] * len(streams)
    live = list(zip(streams, delays))
    rnd = 0
    while live:
        for g, d in list(live):
            if rnd >= d:
                try:
                    next(g)
                except StopIteration:
                    live.remove((g, d))
        rnd += 1


WEIGHT_ROW_SPLIT = 8


def _weight_cast_kernel(*refs):
    n_plain = (len(refs) - 3) // 2
    w_in_t_ref, plain_in = refs[0], refs[1:1 + n_plain]
    w_proj_ref, w_gate_ref, plain_out = refs[1 + n_plain], refs[2 + n_plain], refs[3 + n_plain:]
    for g in range(N_PROJ // LANES):
        w_proj_ref[:, g * LANES:(g + 1) * LANES] = w_in_t_ref[g * LANES:(g + 1) * LANES, :].T.astype(BF16)
    for g in range(2 * D_MODEL // LANES):
        w_gate_ref[:, g * LANES:(g + 1) * LANES] = w_in_t_ref[OFF_GA + g * LANES:OFF_GA + (g + 1) * LANES, :].T.astype(BF16)
    for src, dst in zip(plain_in, plain_out):
        dst[...] = src[...].astype(BF16)


def _weight_cast(w_in_t, plain):
    split = WEIGHT_ROW_SPLIT
    assert D_MODEL // split == LANES
    rows_blk = lambda w: pl.BlockSpec((w.shape[0] // split, w.shape[1]), lambda i: (i, 0))
    out_shapes = (jax.ShapeDtypeStruct((D_MODEL, N_PROJ), BF16), jax.ShapeDtypeStruct((D_MODEL, 2 * D_MODEL), BF16),
                  *(jax.ShapeDtypeStruct(w.shape, BF16) for w in plain))
    return pl.pallas_call(
        _weight_cast_kernel,
        grid=(split,),
        in_specs=[pl.BlockSpec((D_IN, LANES), lambda i: (0, i))] + [rows_blk(w) for w in plain],
        out_specs=tuple(rows_blk(s) for s in out_shapes),
        out_shape=out_shapes,
        compiler_params=pltpu.CompilerParams(dimension_semantics=("parallel",), vmem_limit_bytes=VMEM_LIMIT),
        name="weight_cast",
    )(w_in_t, *plain)


def _in_proj_stages(x, g_ref, w_ref, b_ref, cos_ref, sin_ref,
                    q_ref, k_ref, v_ref, kl_ref, vl_ref, mq_ref, mk_ref, mv_ref, og_ref, gif_ref, gt_ref,
                    *, feature_major):
    xn = _rms(x, g_ref[...]).astype(BF16)
    tm = x.shape[0]

    def proj(lo, hi):
        return _mm(xn, w_ref[:, lo:hi]) + b_ref[:, lo:hi]

    cos = cos_ref[...]
    sin = sin_ref[...]
    lane = lax.broadcasted_iota(jnp.int32, (1, LANES), 1)
    first_half = (lane % HEAD_DIM) < (HEAD_DIM // 2)

    def rope(z):
        partner = jnp.where(first_half, pltpu.roll(z, LANES - HEAD_DIM // 2, 1), pltpu.roll(z, HEAD_DIM // 2, 1))
        return z * cos + partner * sin

    zq = proj(OFF_AQ, OFF_AK)
    for c in range(ATTN_Q // LANES):
        sl = slice(c * LANES, (c + 1) * LANES)
        qc = rope(zq[:, sl]) * (HEAD_DIM ** -0.5)
        if feature_major:
            q_ref[sl, :] = qc.T.astype(q_ref.dtype)
        else:
            kv = (2 * c) // GQA_GROUP
            own = (lane // HEAD_DIM) == kv
            swapped = pltpu.roll(qc, HEAD_DIM, 1)
            for j, src in enumerate((qc, swapped) if kv == 0 else (swapped, qc)):
                hd = 2 * c + j
                q_ref[:, hd * LANES:(hd + 1) * LANES] = jnp.where(own, src, 0.0).astype(q_ref.dtype)
    yield
    zkv = proj(OFF_AK, OFF_MQ)
    k = rope(zkv[:, :ATTN_KV])
    v = zkv[:, ATTN_KV:]
    k_ref[...] = k.astype(k_ref.dtype)
    v_ref[...] = (v.T if feature_major else v).astype(v_ref.dtype)
    kl_ref[...] = k[tm - WINDOW:, :]
    vl_ref[...] = v[tm - WINDOW:, :]
    yield
    for z_off, dst in ((OFF_MQ, mq_ref), (OFF_MV, mv_ref)):
        z = proj(z_off, z_off + ML_W)
        if feature_major:
            for h in range(ML_HEADS):
                hs = slice(h * ML_DIM, (h + 1) * ML_DIM)
                dst[0, hs, :] = z[:, hs].T.astype(dst.dtype)
        else:
            dst[...] = z.astype(dst.dtype)
        yield
    mk_ref[...] = (proj(OFF_MK, OFF_MV) * (ML_DIM ** -0.5)).astype(mk_ref.dtype)
    yield
    og_ref[...] = jax.nn.sigmoid(proj(OFF_MO, OFF_MI))
    yield
    zg = proj(OFF_MI, N_PROJ)
    gif = jnp.where(lane < ML_HEADS, zg, jnp.where(lane < 2 * ML_HEADS, _log_sigmoid(zg), 0.0))
    if feature_major:
        L = ML_CHUNK
        upper = (lax.broadcasted_iota(jnp.int32, (L, L), 0) <= lax.broadcasted_iota(jnp.int32, (L, L), 1)).astype(F32)
        row8 = lax.broadcasted_iota(jnp.int32, (8, 1), 0)
        fill = jnp.zeros((L - 16, L), F32)
        for c in range(tm // L):
            cs = slice(c * L, (c + 1) * L)
            g_rows = gif[cs, :].T[0:8, :]
            cum = jnp.dot(g_rows, upper, precision=lax.Precision.HIGHEST, preferred_element_type=F32)
            rows = jnp.where(row8 < ML_HEADS, g_rows, cum)
            gt_ref[0, :, cs] = rows
            diff = rows - pltpu.roll(rows, ML_HEADS, 0)
            gif_ref[cs, :] = jnp.concatenate([rows, diff, fill], axis=0).T
    else:
        gif_ref[...] = gif
        gt_ref[0] = jnp.zeros(gt_ref.shape[1:], F32)
    yield


def _in_proj_kernel(x_ref, *refs, feature_major):
    _weave(_in_proj_stages(x_ref[...], *refs, feature_major=feature_major))


def _in_proj(x, g, w, b, cos, sin, tm, m_dtype, feature_major):
    rows = x.shape[0]
    seq = cos.shape[0]
    tps = seq // tm
    nseq = rows // seq
    row = lambda i: (i, 0)
    const = lambda i: (0, 0)
    pos = lambda i: (i % tps, 0)
    last = lambda i: (i // tps, 0)
    row_out = lambda n, dt: (jax.ShapeDtypeStruct((rows, n), dt), pl.BlockSpec((tm, n), row))
    col_out = lambda n, dt: (jax.ShapeDtypeStruct((n, rows), dt), pl.BlockSpec((n, tm), lambda i: (0, i)))
    att_out = col_out if feature_major else row_out
    seq_out = lambda n, dt: (jax.ShapeDtypeStruct((nseq, n, seq), dt),
                             pl.BlockSpec((1, n, tm), lambda i: (i // tps, 0, i % tps)))
    ml_out = seq_out if feature_major else row_out
    outs = (
        col_out(ATTN_Q, BF16) if feature_major else row_out(N_Q_HEADS * LANES, BF16),
        row_out(ATTN_KV, BF16),
        att_out(ATTN_KV, BF16),
        (jax.ShapeDtypeStruct((nseq * WINDOW, ATTN_KV), F32), pl.BlockSpec((WINDOW, ATTN_KV), last)),
        (jax.ShapeDtypeStruct((nseq * WINDOW, ATTN_KV), F32), pl.BlockSpec((WINDOW, ATTN_KV), last)),
        ml_out(ML_W, m_dtype),
        row_out(ML_W, m_dtype),
        ml_out(ML_W, m_dtype),
        row_out(ML_W, F32),
        row_out(LANES, F32),
        seq_out(8, F32),
    )
    return pl.pallas_call(
        functools.partial(_in_proj_kernel, feature_major=feature_major),
        grid=(rows // tm,),
        in_specs=[
            pl.BlockSpec((tm, D_MODEL), row),
            pl.BlockSpec((1, D_MODEL), const),
            pl.BlockSpec((D_MODEL, N_PROJ), const),
            pl.BlockSpec((1, N_PROJ), const),
            pl.BlockSpec((tm, LANES), pos),
            pl.BlockSpec((tm, LANES), pos),
        ],
        out_specs=tuple(o[1] for o in outs),
        out_shape=tuple(o[0] for o in outs),
        compiler_params=pltpu.CompilerParams(dimension_semantics=("arbitrary",), vmem_limit_bytes=VMEM_LIMIT),
        name="in_proj",
    )(x, g, w, b, cos, sin)


def _attn_stages(sink_ref, qt_ref, kp_ref, kc_ref, vtp_ref, vtc_ref, first_block, emit):
    L = WINDOW
    nsub = kc_ref.shape[0] // L
    key = lax.broadcasted_iota(jnp.int32, (2 * L, L), 0)
    t = lax.broadcasted_iota(jnp.int32, (2 * L, L), 1)
    is_prev = key < L
    allowed = (is_prev & (key >= t)) | (~is_prev & (key - L <= t))
    bias = jnp.where(allowed, 0.0, -jnp.inf)
    bias_first = jnp.where(is_prev & first_block, -jnp.inf, bias)
    zeros_q = jnp.zeros((HEAD_DIM, L), BF16)
    heads = range(N_Q_HEADS)
    for r in range(nsub):
        ts = slice(r * L, (r + 1) * L)
        k_prev = kp_ref[...] if r == 0 else kc_ref[(r - 1) * L:r * L, :]
        vt_prev = vtp_ref[...] if r == 0 else vtc_ref[:, (r - 1) * L:r * L]
        k2 = jnp.concatenate([k_prev, kc_ref[ts, :]], axis=0)
        vt2 = jnp.concatenate([vt_prev, vtc_ref[:, ts]], axis=1)
        b_r = bias_first if r == 0 else bias
        q_op = [jnp.concatenate([qt_ref[hd * HEAD_DIM:(hd + 1) * HEAD_DIM, ts], zeros_q][::1 if hd < GQA_GROUP else -1],
                                axis=0) for hd in heads]
        s = [_mm(k2, q_op[hd]) + b_r for hd in heads]
        yield
        m = [jnp.maximum(jnp.max(s[hd], axis=0, keepdims=True), sink_ref[hd]) for hd in heads]
        yield
        p = [jnp.exp(s[hd] - m[hd]) for hd in heads]
        yield
        den = [jnp.sum(p[hd], axis=0, keepdims=True) + jnp.exp(sink_ref[hd] - m[hd]) for hd in heads]
        o = [_mm(vt2[(hd // GQA_GROUP) * HEAD_DIM:(hd // GQA_GROUP + 1) * HEAD_DIM, :], p[hd].astype(BF16)) / den[hd]
             for hd in heads]
        yield
        emit(r, jnp.concatenate(o, axis=0).T)
        yield


ML_STATE_ROWS = ML_DIM + 8


def _mlstm_stages(chains, emit):
    L = ML_CHUNK
    s_idx = lax.broadcasted_iota(jnp.int32, (L, L), 0)
    t_idx = lax.broadcasted_iota(jnp.int32, (L, L), 1)
    causal = s_idx <= t_idx
    n = range(len(chains))
    dlog = [jnp.where(causal, ch["b_r"] + ch["u_c"], -jnp.inf) for ch in chains]
    yield
    qk = [_mm(ch["k"], ch["qt"]) for ch in chains]
    inter = [_mm(ch["state"].astype(BF16), ch["qt"]) for ch in chains]
    yield
    a = [ch["b_r"] + ch["m_prev"] for ch in chains]
    m_t = [jnp.maximum(a[i], jnp.max(dlog[i], axis=0, keepdims=True)) for i in n]
    yield
    sw = [qk[i] * jnp.exp(dlog[i] - m_t[i]) for i in n]
    yield
    sv = [_mm(chains[i]["vt"], sw[i].astype(BF16)) for i in n]
    yield
    hs = []
    for i in n:
        aw = jnp.exp(a[i] - m_t[i])
        num = aw * inter[i][:ML_DIM] + sv[i]
        den = aw * inter[i][ML_DIM:ML_DIM + 1] + jnp.sum(sw[i], axis=0, keepdims=True)
        hh = num / jnp.maximum(jnp.abs(den), jnp.exp(-m_t[i]))
        hs.append(hh.T * chains[i]["og"])
    yield
    pad = jnp.zeros((ML_STATE_ROWS - ML_DIM - 1, L), F32)
    new_state, new_m = [], []
    for i in n:
        ch = chains[i]
        b_end = ch["b_r"][:, L - 1:L]
        g = b_end - ch["b_r"] + ch["ig_r"]
        m_new = jnp.maximum(b_end + ch["m_prev"], jnp.max(g, axis=1, keepdims=True))
        decay = jnp.exp(b_end + ch["m_prev"] - m_new)
        w = jnp.exp(g - m_new)
        v_aug = jnp.concatenate([ch["vt"].astype(F32) * w, w, pad], axis=0).astype(BF16)
        new_state.append(decay * ch["state"] + _mm(v_aug, ch["k"]))
        new_m.append(m_new)
    emit(hs, new_state, new_m)
    yield


DENSE_COLS = 512


def _dense_stages(x, a, hm, p, gmix_ref, wg_ref, bg_ref, wau_ref, wmu_ref, wo_ref,
                  gmlp_ref, wff1_ref, wff2_ref, gple_ref, wpg_ref, wpp_ref, gfin_ref, emit):
    nc = DENSE_COLS
    pieces = lambda n: [slice(c, c + nc) for c in range(0, n, nc)]
    xn = _rms(x, gmix_ref[...]).astype(BF16)
    gates = []
    for cs in pieces(2 * D_MODEL):
        gates.append(jax.nn.sigmoid(_mm(xn, wg_ref[:, cs]) + bg_ref[:, cs]))
        yield
    half = D_MODEL // nc
    mix = []
    for j, cs in enumerate(pieces(D_MODEL)):
        mix.append((gates[j] * _mm(a, wau_ref[:, cs]) + gates[half + j] * _mm(hm, wmu_ref[:, cs])).astype(BF16))
        yield
    mix = jnp.concatenate(mix, axis=1)
    h = []
    for cs in pieces(D_MODEL):
        h.append(x[:, cs] + _mm(mix, wo_ref[:, cs]))
        yield
    h = jnp.concatenate(h, axis=1)
    hn = _rms(h, gmlp_ref[...]).astype(BF16)
    u = []
    for cs in pieces(D_FF):
        u.append(jnp.square(jnp.maximum(_mm(hn, wff1_ref[:, cs]), 0.0)).astype(BF16))
        yield
    u = jnp.concatenate(u, axis=1)
    h2 = []
    for cs in pieces(D_MODEL):
        h2.append(h[:, cs] + _mm(u, wff2_ref[:, cs]))
        yield
    h = jnp.concatenate(h2, axis=1)
    hn = _rms(h, gple_ref[...]).astype(BF16)
    pb = p.astype(BF16)
    h3 = []
    for cs in pieces(D_MODEL):
        h3.append(h[:, cs] + _mm(pb, wpp_ref[:, cs]) * jax.nn.sigmoid(_mm(hn, wpg_ref[:, cs])))
        yield
    emit(_rms(jnp.concatenate(h3, axis=1), gfin_ref[...]))
    yield


N_FUSED_WEIGHTS = 13
N_PROJ_SLOTS = 9
PROJ_DELAY = 0


def _prompt_kernel(sink_ref, x0_ref, cos_ref, sin_ref, x2_ref, p_ref, xs_ref, as_ref, hs_ref, ps_ref,
                   gmix_ref, wp_ref, bp_ref, *rest, tiles_per_seq):
    weights = rest[:N_FUSED_WEIGHTS]
    y_ref, ys_ref, kl_ref, vl_ref, cst_ref, nst_ref, mst_ref = rest[N_FUSED_WEIGHTS:N_FUSED_WEIGHTS + 7]
    scr = rest[N_FUSED_WEIGHTS + 7:]
    proj_scr, (kprev_scr, vtprev_scr, a_scr, h_scr, c_scr, m_scr) = scr[:N_PROJ_SLOTS], scr[N_PROJ_SLOTS:]
    i = pl.program_id(0)
    n_tiles = pl.num_programs(0) - 3
    t1 = jnp.clip(i - 1, 0, n_tiles - 1)
    tile_in_seq = t1 % tiles_per_seq
    s0, s1 = i % 2, (i + 1) % 2
    L = ML_CHUNK
    tm = x0_ref.shape[0]

    @pl.when(i == 0)
    def _():
        for r in scr:
            r[...] = jnp.zeros_like(r)

    @pl.when(tile_in_seq == 0)
    def _():
        c_scr[...] = jnp.zeros_like(c_scr)
        m_scr[...] = jnp.zeros_like(m_scr)

    def projection():
        outs = [r.at[s0] for r in proj_scr]
        yield from _in_proj_stages(x0_ref[...], gmix_ref, wp_ref, bp_ref, cos_ref, sin_ref,
                                   outs[0], outs[1], outs[2], kl_ref, vl_ref, *outs[3:], feature_major=True)

    def branches():
        qt_ref, kc_ref, vtc_ref, mqt_ref, mk_ref, mvt_ref, og_ref, gif_ref, gt_ref = [r.at[s1] for r in proj_scr]
        slot = s1

        def emit_attn(r, blk):
            a_scr[slot, r * WINDOW:(r + 1) * WINDOW, :] = blk.astype(a_scr.dtype)

        yield from _attn_stages(sink_ref, qt_ref, kprev_scr, kc_ref, vtprev_scr, vtc_ref, tile_in_seq == 0, emit_attn)
        kprev_scr[...] = kc_ref[tm - WINDOW:, :]
        vtprev_scr[...] = vtc_ref[:, tm - WINDOW:]
        hsl = lambda h: slice(h * ML_DIM, (h + 1) * ML_DIM)
        carry = dict(state=[c_scr[h] for h in range(ML_HEADS)], m=[m_scr[h:h + 1, :] for h in range(ML_HEADS)])
        for c in range(tm // L):
            cs = slice(c * L, (c + 1) * L)

            def emit_ml(hs, new_state, new_m, cs=cs):
                for h in range(ML_HEADS):
                    h_scr[slot, cs, hsl(h)] = hs[h].astype(h_scr.dtype)
                carry["state"], carry["m"] = new_state, new_m

            chains = [dict(qt=mqt_ref[0, hsl(h), cs], k=mk_ref[cs, hsl(h)], vt=mvt_ref[0, hsl(h), cs],
                           og=og_ref[cs, hsl(h)], ig_r=gt_ref[0, h:h + 1, cs],
                           b_r=gt_ref[0, ML_HEADS + h:ML_HEADS + h + 1, cs],
                           u_c=gif_ref[cs, 2 * ML_HEADS + h:2 * ML_HEADS + h + 1],
                           state=carry["state"][h], m_prev=carry["m"][h]) for h in range(ML_HEADS)]
            yield from _mlstm_stages(chains, emit_ml)
        for h in range(ML_HEADS):
            c_scr[h] = carry["state"][h]
            m_scr[h:h + 1, :] = carry["m"][h]

    def dense():
        def emit_y(y):
            y_ref[...] = y

        return _dense_stages(x2_ref[...], a_scr[s0], h_scr[s0], p_ref[...], *weights, emit=emit_y)

    @pl.when(i < 2)
    def _():
        _weave(branches(), projection())

    @pl.when((i >= 2) & (i < n_tiles))
    def _():
        _weave(branches(), dense(), projection())

    @pl.when((i >= n_tiles) & (i < n_tiles + 2))
    def _():
        _weave(branches(), dense())

    @pl.when(i == n_tiles + 2)
    def _():
        def emit_ys(y):
            ys_ref[...] = y

        _weave(_dense_stages(xs_ref[...], as_ref[...], hs_ref[...], ps_ref[...], *weights, emit=emit_ys))

    @pl.when((tile_in_seq == tiles_per_seq - 1) & (i >= 1) & (i <= n_tiles))
    def _():
        nst_ref[...] = jnp.zeros_like(nst_ref)
        for h in range(ML_HEADS):
            final = c_scr[h]
            cst_ref[0, h] = final[:ML_DIM].T
            nst_ref[0, h:h + 1, :] = final[ML_DIM:ML_DIM + 1]
        mst_ref[0] = m_scr[...]


def _prompt_fused(sinks, x, p, cos, sin, sample, g_mix, w_proj, b_proj, weights, batch, seq, tm):
    assert len(weights) == N_FUSED_WEIGHTS
    rows = batch * seq
    n_tiles = rows // tm
    tps = seq // tm
    t0 = lambda i: jnp.minimum(i, n_tiles - 1)
    t1 = lambda i: jnp.clip(i - 1, 0, n_tiles - 1)
    t2 = lambda i: jnp.clip(i - 2, 0, n_tiles - 1)
    const = lambda i: (0, 0)
    single = lambda a: pl.BlockSpec(a.shape, const, pipeline_mode=pl.Buffered(1))
    slot = lambda shape, dt: pltpu.VMEM((2,) + shape, dt)
    return pl.pallas_call(
        functools.partial(_prompt_kernel, tiles_per_seq=tps),
        grid=(n_tiles + 3,),
        in_specs=[
            pl.BlockSpec(memory_space=pltpu.SMEM),
            pl.BlockSpec((tm, D_MODEL), lambda i: (t0(i), 0)),
            pl.BlockSpec((tm, LANES), lambda i: (t0(i) % tps, 0)),
            pl.BlockSpec((tm, LANES), lambda i: (t0(i) % tps, 0)),
            pl.BlockSpec((tm, D_MODEL), lambda i: (t2(i), 0)),
            pl.BlockSpec((tm, D_PLE), lambda i: (t2(i), 0)),
            *(single(a) for a in sample),
            single(g_mix), single(w_proj), single(b_proj),
        ] + [single(w) for w in weights],
        out_specs=(
            pl.BlockSpec((tm, D_MODEL), lambda i: (t2(i), 0)),
            pl.BlockSpec(sample[0].shape, const),
            pl.BlockSpec((WINDOW, ATTN_KV), lambda i: (t0(i) // tps, 0)),
            pl.BlockSpec((WINDOW, ATTN_KV), lambda i: (t0(i) // tps, 0)),
            pl.BlockSpec((1, ML_HEADS, ML_DIM, ML_DIM), lambda i: (t1(i) // tps, 0, 0, 0)),
            pl.BlockSpec((1, 8, LANES), lambda i: (t1(i) // tps, 0, 0)),
            pl.BlockSpec((1, 8, LANES), lambda i: (t1(i) // tps, 0, 0)),
        ),
        out_shape=(
            jax.ShapeDtypeStruct((rows, D_MODEL), F32),
            jax.ShapeDtypeStruct(sample[0].shape, F32),
            jax.ShapeDtypeStruct((batch * WINDOW, ATTN_KV), F32),
            jax.ShapeDtypeStruct((batch * WINDOW, ATTN_KV), F32),
            jax.ShapeDtypeStruct((batch, ML_HEADS, ML_DIM, ML_DIM), F32),
            jax.ShapeDtypeStruct((batch, 8, LANES), F32),
            jax.ShapeDtypeStruct((batch, 8, LANES), F32),
        ),
        scratch_shapes=[
            slot((ATTN_Q, tm), BF16), slot((tm, ATTN_KV), BF16), slot((ATTN_KV, tm), BF16),
            slot((1, ML_W, tm), BF16), slot((tm, ML_W), BF16), slot((1, ML_W, tm), BF16),
            slot((tm, ML_W), F32), slot((tm, LANES), F32), slot((1, 8, tm), F32),
            pltpu.VMEM((WINDOW, ATTN_KV), BF16), pltpu.VMEM((ATTN_KV, WINDOW), BF16),
            slot((tm, ATTN_Q), BF16), slot((tm, ML_W), BF16),
            pltpu.VMEM((ML_HEADS, ML_STATE_ROWS, ML_DIM), F32), pltpu.VMEM((8, LANES), F32),
        ],
        compiler_params=pltpu.CompilerParams(dimension_semantics=("arbitrary",), vmem_limit_bytes=VMEM_LIMIT),
        name="prompt_fused",
    )(sinks, x, cos, sin, x, p, *sample, g_mix, w_proj, b_proj, *weights)


def _attn_decode_kernel(sink_ref, qm_ref, kn_ref, vn_ref, ck_ref, cv_ref, o_ref, ko_ref, vo_ref):
    bt = qm_ref.shape[0]
    qm = qm_ref[...]
    kn = kn_ref[...]
    vn = vn_ref[...]
    ck = ck_ref[...]
    cv = cv_ref[...]
    sink = sink_ref[...]
    s = jnp.einsum('bhc,bck->bhk', qm, ck.astype(BF16), preferred_element_type=F32)
    s_new = jnp.sum(qm.astype(F32) * kn[:, None, :], axis=2, keepdims=True)
    m = jnp.maximum(jnp.maximum(jnp.max(s, axis=2, keepdims=True), s_new), sink)
    p = jnp.exp(s - m)
    p_new = jnp.exp(s_new - m)
    den = jnp.sum(p, axis=2, keepdims=True) + p_new + jnp.exp(sink - m)
    o = jnp.einsum('bhk,bck->bhc', p.astype(BF16), cv.astype(BF16), preferred_element_type=F32)
    o_ref[...] = (o + p_new * vn[:, None, :]) / den
    pad = jnp.zeros((LANES - bt, ATTN_KV), F32)
    kn_t = jnp.concatenate([kn, pad], axis=0).T
    vn_t = jnp.concatenate([vn, pad], axis=0).T
    newest = lax.broadcasted_iota(jnp.int32, (ATTN_KV, WINDOW), 1) == WINDOW - 1
    for b in range(bt):
        ko_ref[b] = jnp.where(newest, kn_t[:, b:b + 1], pltpu.roll(ck[b], WINDOW - 1, 1))
        vo_ref[b] = jnp.where(newest, vn_t[:, b:b + 1], pltpu.roll(cv[b], WINDOW - 1, 1))


def _attn_decode(sinks_col, qmat, k_new, v_new, cache_k, cache_v, bt):
    nbatch = qmat.shape[0]
    b3 = lambda i: (i, 0, 0)
    b2 = lambda i: (i, 0)
    return pl.pallas_call(
        _attn_decode_kernel,
        grid=(nbatch // bt,),
        in_specs=[
            pl.BlockSpec((N_Q_HEADS, 1), lambda i: (0, 0)),
            pl.BlockSpec((bt, N_Q_HEADS, ATTN_KV), b3),
            pl.BlockSpec((bt, ATTN_KV), b2),
            pl.BlockSpec((bt, ATTN_KV), b2),
            pl.BlockSpec((bt, WINDOW, ATTN_KV), b3),
            pl.BlockSpec((bt, WINDOW, ATTN_KV), b3),
        ],
        out_specs=(
            pl.BlockSpec((bt, N_Q_HEADS, ATTN_KV), b3),
            pl.BlockSpec((bt, WINDOW, ATTN_KV), b3),
            pl.BlockSpec((bt, WINDOW, ATTN_KV), b3),
        ),
        out_shape=(
            jax.ShapeDtypeStruct((nbatch, N_Q_HEADS, ATTN_KV), F32),
            jax.ShapeDtypeStruct((nbatch, WINDOW, ATTN_KV), F32),
            jax.ShapeDtypeStruct((nbatch, WINDOW, ATTN_KV), F32),
        ),
        compiler_params=pltpu.CompilerParams(dimension_semantics=("parallel",)),
        name="attn_decode",
    )(sinks_col, qmat, k_new, v_new, cache_k, cache_v)


def _mlstm_decode_kernel(q_ref, k_ref, v_ref, og_ref, gif_ref, c_ref, n_ref, m_ref, h_ref, co_ref, no_ref, mo_ref):
    bt = q_ref.shape[0]
    gif = gif_ref[...]
    ig = gif[:, 0:ML_HEADS]
    lf = gif[:, ML_HEADS:2 * ML_HEADS]
    m_prev = m_ref[...]
    a = lf + m_prev
    m_t = jnp.maximum(a, ig)
    aw = jnp.exp(a - m_t)
    e_i = jnp.exp(ig - m_t)
    floor = jnp.exp(-m_t)
    m_new = jnp.maximum(a, ig)
    decay = jnp.exp(a - m_new)
    w = jnp.exp(ig - m_new)
    mo_ref[...] = m_new
    pad = jnp.zeros((LANES - bt, ML_DIM), F32)
    lane = lax.broadcasted_iota(jnp.int32, (ML_DIM, LANES), 1)
    for h in range(ML_HEADS):
        hs = slice(h * ML_DIM, (h + 1) * ML_DIM)
        q = q_ref[:, hs]
        k = k_ref[:, hs]
        v = v_ref[:, hs]
        n = n_ref[:, h, :]
        qk = jnp.sum(q * k, axis=1, keepdims=True)
        qn = jnp.sum(q * n, axis=1, keepdims=True)
        sw = qk * e_i[:, h:h + 1]
        den = aw[:, h:h + 1] * qn + sw
        inv = 1.0 / jnp.maximum(jnp.abs(den), floor[:, h:h + 1])
        no_ref[:, h, :] = decay[:, h:h + 1] * n + w[:, h:h + 1] * k
        q_b = q.astype(BF16)
        wk_t = jnp.concatenate([w[:, h:h + 1] * k, pad], axis=0).T.astype(BF16)
        v_rows = jnp.concatenate([v, pad], axis=0).astype(BF16)
        nums = []
        for b in range(bt):
            C = c_ref[b, h]
            q_c = _mm(q_b, C.astype(BF16))[b:b + 1, :]
            nums.append(aw[b:b + 1, h:h + 1] * q_c)
            outer = _mm(jnp.where(lane == b, wk_t, jnp.zeros_like(wk_t)), v_rows)
            co_ref[b, h] = decay[b:b + 1, h:h + 1] * C + outer
        num = jnp.concatenate(nums, axis=0) + sw * v
        h_ref[:, hs] = (num * inv * og_ref[:, hs]).astype(h_ref.dtype)


def _mlstm_decode(mq, mk, mv, og, gif, state_c, state_n, state_m, bt):
    nbatch = mq.shape[0]
    b2 = lambda i: (i, 0)
    b3 = lambda i: (i, 0, 0)
    b4 = lambda i: (i, 0, 0, 0)
    return pl.pallas_call(
        _mlstm_decode_kernel,
        grid=(nbatch // bt,),
        in_specs=[
            pl.BlockSpec((bt, ML_W), b2),
            pl.BlockSpec((bt, ML_W), b2),
            pl.BlockSpec((bt, ML_W), b2),
            pl.BlockSpec((bt, ML_W), b2),
            pl.BlockSpec((bt, LANES), b2),
            pl.BlockSpec((bt, ML_HEADS, ML_DIM, ML_DIM), b4),
            pl.BlockSpec((bt, ML_HEADS, ML_DIM), b3),
            pl.BlockSpec((bt, ML_HEADS), b2),
        ],
        out_specs=(
            pl.BlockSpec((bt, ML_W), b2),
            pl.BlockSpec((bt, ML_HEADS, ML_DIM, ML_DIM), b4),
            pl.BlockSpec((bt, ML_HEADS, ML_DIM), b3),
            pl.BlockSpec((bt, ML_HEADS), b2),
        ),
        out_shape=(
            jax.ShapeDtypeStruct((nbatch, ML_W), BF16),
            jax.ShapeDtypeStruct((nbatch, ML_HEADS, ML_DIM, ML_DIM), F32),
            jax.ShapeDtypeStruct((nbatch, ML_HEADS, ML_DIM), F32),
            jax.ShapeDtypeStruct((nbatch, ML_HEADS), F32),
        ),
        compiler_params=pltpu.CompilerParams(dimension_semantics=("parallel",)),
        name="mlstm_decode",
    )(mq, mk, mv, og, gif, state_c, state_n, state_m)


def _rope_tables(pos):
    half = HEAD_DIM // 2
    inv = (1.0 / (ROPE_THETA ** (np.arange(half, dtype=np.float32) / half))).astype(np.float32)
    ang = pos.astype(np.float32)[:, None] * inv[None, :]
    cos = np.cos(ang)
    sin = np.sin(ang)
    cos_t = np.concatenate([cos, cos, cos, cos], axis=1)
    sin_t = np.concatenate([-sin, sin, -sin, sin], axis=1)
    return jnp.asarray(cos_t, F32), jnp.asarray(sin_t, F32)


def kernel(x_prompt, x_sample, cache_k, cache_v, state_C, state_n, state_m, p_prompt, p_sample, norm_mix, w_in, b_in, attn_sinks, w_attn_up, w_ml_up, w_o, norm_mlp, w_ff1, w_ff2, norm_ple, w_ple_gate, w_ple_proj, norm_final):
    assert w_in.shape[0] == 1, "single layer"
    batch, seq, _ = x_prompt.shape
    nbatch = x_sample.shape[0]
    assert x_sample.shape[1] == 1

    w_in0 = w_in[0]
    b_in0 = b_in[0]
    w_proj, w_gate, w_au, w_mu, w_out, w_f1, w_f2, w_pg, w_pp = _weight_cast(
        w_in0.T, (w_attn_up[0], w_ml_up[0], w_o[0], w_ff1[0], w_ff2[0], w_ple_gate[0], w_ple_proj[0]))
    b_proj = b_in0[:N_PROJ][None, :]
    b_gate = b_in0[OFF_GA:][None, :]
    g_mix = norm_mix[0][None, :]
    out_weights = (
        g_mix, w_gate, b_gate, w_au, w_mu, w_out,
        norm_mlp[0][None, :], w_f1, w_f2,
        norm_ple[0][None, :], w_pg, w_pp, norm_final[None, :],
    )
    sinks = attn_sinks[0]

    cos_s, sin_s = _rope_tables(np.full((nbatch,), PAST_LEN, dtype=np.int32))
    xs = x_sample.reshape(nbatch, D_MODEL)
    qs, _, _, ks, vs, mqs, mks, mvs, ogs, gifs, _ = _in_proj(xs, g_mix, w_proj, b_proj, cos_s, sin_s, tm=nbatch,
                                                          m_dtype=F32, feature_major=False)
    qmat = qs.reshape(nbatch, N_Q_HEADS, ATTN_KV)
    to_fp = lambda c: c[0].transpose(0, 2, 3, 1).reshape(nbatch, ATTN_KV, WINDOW)
    from_fp = lambda c: c.reshape(nbatch, N_KV_HEADS, HEAD_DIM, WINDOW).transpose(0, 3, 1, 2)[None]
    o_full, k_s, v_s = _attn_decode(sinks[:, None], qmat, ks, vs, to_fp(cache_k), to_fp(cache_v), bt=DECODE_ROWS)
    o5 = o_full.reshape(nbatch, N_KV_HEADS, GQA_GROUP, N_KV_HEADS, HEAD_DIM)
    a_s = jnp.stack([o5[:, kv, :, kv, :] for kv in range(N_KV_HEADS)], axis=1).reshape(nbatch, ATTN_Q).astype(BF16)
    h_s, c_s, n_s, m_s = _mlstm_decode(mqs, mks, mvs, ogs, gifs, state_C[0], state_n[0], state_m[0], bt=DECODE_ROWS)
    k_sample = from_fp(k_s)
    v_sample = from_fp(v_s)

    cos_p, sin_p = _rope_tables(np.arange(seq, dtype=np.int32))
    xp = x_prompt.reshape(batch * seq, D_MODEL)
    y_prompt, y_sample, k_last, v_last, cst, nst, mst = _prompt_fused(
        sinks, xp, p_prompt[0].reshape(batch * seq, D_PLE), cos_p, sin_p,
        (xs, a_s, h_s, p_sample[0].reshape(nbatch, D_PLE)), g_mix, w_proj, b_proj, out_weights,
        batch, seq, tm=PROMPT_TILE)
    y_prompt = y_prompt.reshape(batch, seq, D_MODEL)
    y_sample = y_sample.reshape(nbatch, 1, D_MODEL)
    k_prompt = k_last.reshape(1, batch, WINDOW, N_KV_HEADS, HEAD_DIM)
    v_prompt = v_last.reshape(1, batch, WINDOW, N_KV_HEADS, HEAD_DIM)
    c_prompt = cst[None]
    n_prompt = nst[:, :ML_HEADS, :][None]
    m_prompt = mst[:, :ML_HEADS, 0][None]

    return (y_prompt, y_sample, k_prompt, v_prompt, c_prompt, n_prompt, m_prompt,
            k_sample, v_sample, c_s[None], n_s[None], m_s[None])
```

```python
import functools

import numpy as np
import jax
import jax.numpy as jnp
from jax import lax
from jax.experimental import pallas as pl
from jax.experimental.pallas import tpu as pltpu

F32 = jnp.float32
BF16 = jnp.bfloat16

D_MODEL = 1024
HEAD_DIM = 64
N_Q_HEADS = 8
N_KV_HEADS = 2
GQA_GROUP = N_Q_HEADS // N_KV_HEADS
WINDOW = 128
ROPE_THETA = 10000.0
ML_HEADS = 4
ML_DIM = 128
ML_CHUNK = 128
D_FF = 4 * D_MODEL
D_PLE = 256
EPS = 1e-6
PAST_LEN = 16384

ATTN_Q = N_Q_HEADS * HEAD_DIM
ATTN_KV = N_KV_HEADS * HEAD_DIM
ML_W = ML_HEADS * ML_DIM
LANES = 128

OFF_AQ = 0
OFF_AK = OFF_AQ + ATTN_Q
OFF_AV = OFF_AK + ATTN_KV
OFF_MQ = OFF_AV + ATTN_KV
OFF_MK = OFF_MQ + ML_W
OFF_MV = OFF_MK + ML_W
OFF_MO = OFF_MV + ML_W
OFF_MI = OFF_MO + ML_W
OFF_MF = OFF_MI + ML_HEADS
OFF_GA = OFF_MF + ML_HEADS
OFF_GM = OFF_GA + D_MODEL
D_IN = OFF_GM + D_MODEL
N_PROJ = OFF_MI + LANES

VMEM_LIMIT = 60 * 1024 * 1024
PROMPT_TILE = 256
DECODE_ROWS = 16

def _rms(x, g):
    r = lax.rsqrt(jnp.mean(x * x, axis=-1, keepdims=True) + EPS)
    return (x * r) * g


def _mm(a, b):
    return jnp.dot(a, b, preferred_element_type=F32)


def _mm_nt(a, b):
    return lax.dot_general(a, b, (((1,), (1,)), ((), ())), preferred_element_type=F32)


def _mm_tn(a, b):
    return lax.dot_general(a, b, (((0,), (0,)), ((), ())), preferred_element_type=F32)


def _log_sigmoid(x):
    return jnp.minimum(x, 0.0) - jnp.log1p(jnp.exp(-jnp.abs(x)))


def _weave(*streams, delays=None):
    delays = delays or [0] * len(streams)
    live = list(zip(streams, delays))
    rnd = 0
    while live:
        for g, d in list(live):
            if rnd >= d:
                try:
                    next(g)
                except StopIteration:
                    live.remove((g, d))
        rnd += 1


WEIGHT_ROW_SPLIT = 8


def _weight_cast_kernel(*refs):
    n_plain = (len(refs) - 3) // 2
    w_in_t_ref, plain_in = refs[0], refs[1:1 + n_plain]
    w_proj_ref, w_gate_ref, plain_out = refs[1 + n_plain], refs[2 + n_plain], refs[3 + n_plain:]
    for g in range(N_PROJ // LANES):
        w_proj_ref[:, g * LANES:(g + 1) * LANES] = w_in_t_ref[g * LANES:(g + 1) * LANES, :].T.astype(BF16)
    for g in range(2 * D_MODEL // LANES):
        w_gate_ref[:, g * LANES:(g + 1) * LANES] = w_in_t_ref[OFF_GA + g * LANES:OFF_GA + (g + 1) * LANES, :].T.astype(BF16)
    for src, dst in zip(plain_in, plain_out):
        dst[...] = src[...].astype(BF16)


def _weight_cast(w_in_t, plain):
    split = WEIGHT_ROW_SPLIT
    assert D_MODEL // split == LANES
    rows_blk = lambda w: pl.BlockSpec((w.shape[0] // split, w.shape[1]), lambda i: (i, 0))
    out_shapes = (jax.ShapeDtypeStruct((D_MODEL, N_PROJ), BF16), jax.ShapeDtypeStruct((D_MODEL, 2 * D_MODEL), BF16),
                  *(jax.ShapeDtypeStruct(w.shape, BF16) for w in plain))
    return pl.pallas_call(
        _weight_cast_kernel,
        grid=(split,),
        in_specs=[pl.BlockSpec((D_IN, LANES), lambda i: (0, i))] + [rows_blk(w) for w in plain],
        out_specs=tuple(rows_blk(s) for s in out_shapes),
        out_shape=out_shapes,
        compiler_params=pltpu.CompilerParams(dimension_semantics=("parallel",), vmem_limit_bytes=VMEM_LIMIT),
        name="weight_cast",
    )(w_in_t, *plain)


def _in_proj_stages(x, g_ref, w_ref, b_ref, cos_ref, sin_ref,
                    q_ref, k_ref, v_ref, kl_ref, vl_ref, mq_ref, mk_ref, mv_ref, og_ref, gif_ref, gt_ref,
                    *, feature_major):
    xn = _rms(x, g_ref[...]).astype(BF16)
    tm = x.shape[0]

    def proj(lo, hi):
        return _mm(xn, w_ref[:, lo:hi]) + b_ref[:, lo:hi]

    cos = cos_ref[...]
    sin = sin_ref[...]
    lane = lax.broadcasted_iota(jnp.int32, (1, LANES), 1)
    first_half = (lane % HEAD_DIM) < (HEAD_DIM // 2)

    def rope(z):
        partner = jnp.where(first_half, pltpu.roll(z, LANES - HEAD_DIM // 2, 1), pltpu.roll(z, HEAD_DIM // 2, 1))
        return z * cos + partner * sin

    zq = proj(OFF_AQ, OFF_AK)
    for c in range(ATTN_Q // LANES):
        sl = slice(c * LANES, (c + 1) * LANES)
        qc = rope(zq[:, sl]) * (HEAD_DIM ** -0.5)
        if feature_major:
            q_ref[sl, :] = qc.T.astype(q_ref.dtype)
        else:
            kv = (2 * c) // GQA_GROUP
            own = (lane // HEAD_DIM) == kv
            swapped = pltpu.roll(qc, HEAD_DIM, 1)
            for j, src in enumerate((qc, swapped) if kv == 0 else (swapped, qc)):
                hd = 2 * c + j
                q_ref[:, hd * LANES:(hd + 1) * LANES] = jnp.where(own, src, 0.0).astype(q_ref.dtype)
    yield
    zkv = proj(OFF_AK, OFF_MQ)
    k = rope(zkv[:, :ATTN_KV])
    v = zkv[:, ATTN_KV:]
    k_ref[...] = k.astype(k_ref.dtype)
    v_ref[...] = (v.T if feature_major else v).astype(v_ref.dtype)
    if feature_major:
        kl_ref[0] = k[tm - WINDOW:, :].T
        vl_ref[0] = v[tm - WINDOW:, :].T
    else:
        kl_ref[...] = k[tm - WINDOW:, :]
        vl_ref[...] = v[tm - WINDOW:, :]
    yield
    for z_off, dst in ((OFF_MQ, mq_ref), (OFF_MV, mv_ref)):
        z = proj(z_off, z_off + ML_W)
        if feature_major:
            for h in range(ML_HEADS):
                hs = slice(h * ML_DIM, (h + 1) * ML_DIM)
                dst[0, hs, :] = z[:, hs].T.astype(dst.dtype)
        else:
            dst[...] = z.astype(dst.dtype)
        yield
    mk_ref[...] = (proj(OFF_MK, OFF_MV) * (ML_DIM ** -0.5)).astype(mk_ref.dtype)
    yield
    og_ref[...] = jax.nn.sigmoid(proj(OFF_MO, OFF_MI))
    yield
    zg = proj(OFF_MI, N_PROJ)
    gif = jnp.where(lane < ML_HEADS, zg, jnp.where(lane < 2 * ML_HEADS, _log_sigmoid(zg), 0.0))
    if feature_major:
        L = ML_CHUNK
        upper = (lax.broadcasted_iota(jnp.int32, (L, L), 0) <= lax.broadcasted_iota(jnp.int32, (L, L), 1)).astype(F32)
        row8 = lax.broadcasted_iota(jnp.int32, (8, 1), 0)
        fill = jnp.zeros((L - 16, L), F32)
        for c in range(tm // L):
            cs = slice(c * L, (c + 1) * L)
            g_rows = gif[cs, :].T[0:8, :]
            cum = jnp.dot(g_rows, upper, precision=lax.Precision.HIGHEST, preferred_element_type=F32)
            rows = jnp.where(row8 < ML_HEADS, g_rows, cum)
            gt_ref[0, :, cs] = rows
            diff = rows - pltpu.roll(rows, ML_HEADS, 0)
            gif_ref[cs, :] = jnp.concatenate([rows, diff, fill], axis=0).T
    else:
        gif_ref[...] = gif
        gt_ref[0] = jnp.zeros(gt_ref.shape[1:], F32)
    yield


def _in_proj_kernel(x_ref, *refs, feature_major):
    _weave(_in_proj_stages(x_ref[...], *refs, feature_major=feature_major))


def _in_proj_sample(x, g, w, b, cos, sin):
    rows = x.shape[0]
    assert rows == WINDOW
    full = lambda n, dt: (jax.ShapeDtypeStruct((rows, n), dt), pl.BlockSpec((rows, n), lambda i: (0, 0)))
    outs = (
        full(N_Q_HEADS * LANES, BF16), full(ATTN_KV, BF16), full(ATTN_KV, BF16), full(ATTN_KV, F32), full(ATTN_KV, F32),
        full(ML_W, F32), full(ML_W, F32), full(ML_W, F32), full(ML_W, F32), full(LANES, F32),
        (jax.ShapeDtypeStruct((1, 8, rows), F32), pl.BlockSpec((1, 8, rows), lambda i: (0, 0, 0))),
    )
    return pl.pallas_call(
        functools.partial(_in_proj_kernel, feature_major=False),
        grid=(1,),
        in_specs=[pl.BlockSpec(a.shape, lambda i: (0, 0)) for a in (x, g, w, b, cos, sin)],
        out_specs=tuple(o[1] for o in outs),
        out_shape=tuple(o[0] for o in outs),
        compiler_params=pltpu.CompilerParams(dimension_semantics=("arbitrary",), vmem_limit_bytes=VMEM_LIMIT),
        name="in_proj",
    )(x, g, w, b, cos, sin)


def _attn_stages(sink_ref, qt_ref, kp_ref, kc_ref, vtp_ref, vtc_ref, first_block, emit):
    L = WINDOW
    nsub = kc_ref.shape[0] // L
    key = lax.broadcasted_iota(jnp.int32, (2 * L, L), 0)
    t = lax.broadcasted_iota(jnp.int32, (2 * L, L), 1)
    is_prev = key < L
    allowed = (is_prev & (key >= t)) | (~is_prev & (key - L <= t))
    bias = jnp.where(allowed, 0.0, -jnp.inf)
    bias_first = jnp.where(is_prev & first_block, -jnp.inf, bias)
    zeros_q = jnp.zeros((HEAD_DIM, L), BF16)
    heads = range(N_Q_HEADS)
    for r in range(nsub):
        ts = slice(r * L, (r + 1) * L)
        k_prev = kp_ref[...] if r == 0 else kc_ref[(r - 1) * L:r * L, :]
        vt_prev = vtp_ref[...] if r == 0 else vtc_ref[:, (r - 1) * L:r * L]
        k2 = jnp.concatenate([k_prev, kc_ref[ts, :]], axis=0)
        vt2 = jnp.concatenate([vt_prev, vtc_ref[:, ts]], axis=1)
        b_r = bias_first if r == 0 else bias
        q_op = [jnp.concatenate([qt_ref[hd * HEAD_DIM:(hd + 1) * HEAD_DIM, ts], zeros_q][::1 if hd < GQA_GROUP else -1],
                                axis=0) for hd in heads]
        s = [_mm(k2, q_op[hd]) + b_r for hd in heads]
        yield
        m = [jnp.maximum(jnp.max(s[hd], axis=0, keepdims=True), sink_ref[hd]) for hd in heads]
        yield
        p = [jnp.exp(s[hd] - m[hd]) for hd in heads]
        yield
        den = [jnp.sum(p[hd], axis=0, keepdims=True) + jnp.exp(sink_ref[hd] - m[hd]) for hd in heads]
        o = [_mm(vt2[(hd // GQA_GROUP) * HEAD_DIM:(hd // GQA_GROUP + 1) * HEAD_DIM, :], p[hd].astype(BF16)) / den[hd]
             for hd in heads]
        yield
        emit(r, jnp.concatenate(o, axis=0).T)
        yield


ML_STATE_ROWS = ML_DIM + 8


def _mlstm_stages(chains, emit):
    L = ML_CHUNK
    s_idx = lax.broadcasted_iota(jnp.int32, (L, L), 0)
    t_idx = lax.broadcasted_iota(jnp.int32, (L, L), 1)
    causal = s_idx <= t_idx
    n = range(len(chains))
    dlog = [jnp.where(causal, ch["b_r"] + ch["u_c"], -jnp.inf) for ch in chains]
    yield
    qk = [_mm(ch["k"], ch["qt"]) for ch in chains]
    inter = [_mm(ch["state"].astype(BF16), ch["qt"]) for ch in chains]
    yield
    a = [ch["b_r"] + ch["m_prev"] for ch in chains]
    m_t = [jnp.maximum(a[i], jnp.max(dlog[i], axis=0, keepdims=True)) for i in n]
    yield
    sw = [qk[i] * jnp.exp(dlog[i] - m_t[i]) for i in n]
    yield
    sv = [_mm(chains[i]["vt"], sw[i].astype(BF16)) for i in n]
    yield
    hs = []
    for i in n:
        aw = jnp.exp(a[i] - m_t[i])
        num = aw * inter[i][:ML_DIM] + sv[i]
        den = aw * inter[i][ML_DIM:ML_DIM + 1] + jnp.sum(sw[i], axis=0, keepdims=True)
        hh = num / jnp.maximum(jnp.abs(den), jnp.exp(-m_t[i]))
        hs.append(hh.T * chains[i]["og"])
    yield
    pad = jnp.zeros((ML_STATE_ROWS - ML_DIM - 1, L), F32)
    new_state, new_m = [], []
    for i in n:
        ch = chains[i]
        b_end = ch["b_r"][:, L - 1:L]
        g = b_end - ch["b_r"] + ch["ig_r"]
        m_new = jnp.maximum(b_end + ch["m_prev"], jnp.max(g, axis=1, keepdims=True))
        decay = jnp.exp(b_end + ch["m_prev"] - m_new)
        w = jnp.exp(g - m_new)
        v_aug = jnp.concatenate([ch["vt"].astype(F32) * w, w, pad], axis=0).astype(BF16)
        new_state.append(decay * ch["state"] + _mm(v_aug, ch["k"]))
        new_m.append(m_new)
    emit(hs, new_state, new_m)
    yield


DENSE_COLS = 512


def _dense_stages(x, a, hm, p, gmix_ref, wg_ref, bg_ref, wau_ref, wmu_ref, wo_ref,
                  gmlp_ref, wff1_ref, wff2_ref, gple_ref, wpg_ref, wpp_ref, gfin_ref, emit):
    nc = DENSE_COLS
    pieces = lambda n: [slice(c, c + nc) for c in range(0, n, nc)]
    xn = _rms(x, gmix_ref[...]).astype(BF16)
    gates = []
    for cs in pieces(2 * D_MODEL):
        gates.append(jax.nn.sigmoid(_mm(xn, wg_ref[:, cs]) + bg_ref[:, cs]))
        yield
    half = D_MODEL // nc
    mix = []
    for j, cs in enumerate(pieces(D_MODEL)):
        mix.append((gates[j] * _mm(a, wau_ref[:, cs]) + gates[half + j] * _mm(hm, wmu_ref[:, cs])).astype(BF16))
        yield
    mix = jnp.concatenate(mix, axis=1)
    h = []
    for cs in pieces(D_MODEL):
        h.append(x[:, cs] + _mm(mix, wo_ref[:, cs]))
        yield
    h = jnp.concatenate(h, axis=1)
    hn = _rms(h, gmlp_ref[...]).astype(BF16)
    u = []
    for cs in pieces(D_FF):
        u.append(jnp.square(jnp.maximum(_mm(hn, wff1_ref[:, cs]), 0.0)).astype(BF16))
        yield
    u = jnp.concatenate(u, axis=1)
    h2 = []
    for cs in pieces(D_MODEL):
        h2.append(h[:, cs] + _mm(u, wff2_ref[:, cs]))
        yield
    h = jnp.concatenate(h2, axis=1)
    hn = _rms(h, gple_ref[...]).astype(BF16)
    pb = p.astype(BF16)
    h3 = []
    for cs in pieces(D_MODEL):
        h3.append(h[:, cs] + _mm(pb, wpp_ref[:, cs]) * jax.nn.sigmoid(_mm(hn, wpg_ref[:, cs])))
        yield
    emit(_rms(jnp.concatenate(h3, axis=1), gfin_ref[...]))
    yield


def _out_kernel(x_ref, a_ref, hm_ref, p_ref, *rest):
    weights, y_ref = rest[:-1], rest[-1]

    def emit(y):
        y_ref[...] = y

    _weave(_dense_stages(x_ref[...], a_ref[...], hm_ref[...], p_ref[...], *weights, emit=emit))


def _out_proj_sample(x, a, hm, p, weights):
    operands = (x, a, hm, p, *weights)
    return pl.pallas_call(
        _out_kernel,
        grid=(1,),
        in_specs=[pl.BlockSpec(o.shape, lambda i: (0, 0), pipeline_mode=pl.Buffered(1)) for o in operands],
        out_specs=pl.BlockSpec(x.shape, lambda i: (0, 0)),
        out_shape=jax.ShapeDtypeStruct(x.shape, F32),
        compiler_params=pltpu.CompilerParams(dimension_semantics=("arbitrary",), vmem_limit_bytes=VMEM_LIMIT),
        name="out_proj",
    )(*operands)


N_FUSED_WEIGHTS = 13
N_PROJ_SLOTS = 9
PROJ_DELAY = 0


def _prompt_kernel(sink_ref, x0_ref, cos_ref, sin_ref, x2_ref, p_ref, gmix_ref, wp_ref, bp_ref, *rest, tiles_per_seq):
    weights = rest[:N_FUSED_WEIGHTS]
    y_ref, kl_ref, vl_ref, cst_ref, nst_ref, mst_ref = rest[N_FUSED_WEIGHTS:N_FUSED_WEIGHTS + 6]
    scr = rest[N_FUSED_WEIGHTS + 6:]
    proj_scr, (kprev_scr, vtprev_scr, a_scr, h_scr, c_scr, m_scr) = scr[:N_PROJ_SLOTS], scr[N_PROJ_SLOTS:]
    i = pl.program_id(0)
    n_tiles = pl.num_programs(0) - 2
    t1 = jnp.clip(i - 1, 0, n_tiles - 1)
    tile_in_seq = t1 % tiles_per_seq
    s0, s1 = i % 2, (i + 1) % 2
    L = ML_CHUNK
    tm = x0_ref.shape[0]

    @pl.when(i == 0)
    def _():
        for r in scr:
            r[...] = jnp.zeros_like(r)

    @pl.when(tile_in_seq == 0)
    def _():
        c_scr[...] = jnp.zeros_like(c_scr)
        m_scr[...] = jnp.zeros_like(m_scr)

    def projection():
        outs = [r.at[s0] for r in proj_scr]
        yield from _in_proj_stages(x0_ref[...], gmix_ref, wp_ref, bp_ref, cos_ref, sin_ref,
                                   outs[0], outs[1], outs[2], kl_ref, vl_ref, *outs[3:], feature_major=True)

    def branches():
        qt_ref, kc_ref, vtc_ref, mqt_ref, mk_ref, mvt_ref, og_ref, gif_ref, gt_ref = [r.at[s1] for r in proj_scr]
        slot = s1

        def emit_attn(r, blk):
            a_scr[slot, r * WINDOW:(r + 1) * WINDOW, :] = blk.astype(a_scr.dtype)

        yield from _attn_stages(sink_ref, qt_ref, kprev_scr, kc_ref, vtprev_scr, vtc_ref, tile_in_seq == 0, emit_attn)
        kprev_scr[...] = kc_ref[tm - WINDOW:, :]
        vtprev_scr[...] = vtc_ref[:, tm - WINDOW:]
        hsl = lambda h: slice(h * ML_DIM, (h + 1) * ML_DIM)
        carry = dict(state=[c_scr[h] for h in range(ML_HEADS)], m=[m_scr[h:h + 1, :] for h in range(ML_HEADS)])
        for c in range(tm // L):
            cs = slice(c * L, (c + 1) * L)

            def emit_ml(hs, new_state, new_m, cs=cs):
                for h in range(ML_HEADS):
                    h_scr[slot, cs, hsl(h)] = hs[h].astype(h_scr.dtype)
                carry["state"], carry["m"] = new_state, new_m

            chains = [dict(qt=mqt_ref[0, hsl(h), cs], k=mk_ref[cs, hsl(h)], vt=mvt_ref[0, hsl(h), cs],
                           og=og_ref[cs, hsl(h)], ig_r=gt_ref[0, h:h + 1, cs],
                           b_r=gt_ref[0, ML_HEADS + h:ML_HEADS + h + 1, cs],
                           u_c=gif_ref[cs, 2 * ML_HEADS + h:2 * ML_HEADS + h + 1],
                           state=carry["state"][h], m_prev=carry["m"][h]) for h in range(ML_HEADS)]
            yield from _mlstm_stages(chains, emit_ml)
        for h in range(ML_HEADS):
            c_scr[h] = carry["state"][h]
            m_scr[h:h + 1, :] = carry["m"][h]

    def dense():
        def emit_y(y):
            y_ref[...] = y

        return _dense_stages(x2_ref[...], a_scr[s0], h_scr[s0], p_ref[...], *weights, emit=emit_y)

    @pl.when(i < 2)
    def _():
        _weave(branches(), projection())

    @pl.when((i >= 2) & (i < n_tiles))
    def _():
        _weave(branches(), dense(), projection())

    @pl.when(i >= n_tiles)
    def _():
        _weave(branches(), dense())

    @pl.when((tile_in_seq == tiles_per_seq - 1) & (i >= 1) & (i <= n_tiles))
    def _():
        nst_ref[...] = jnp.zeros_like(nst_ref)
        for h in range(ML_HEADS):
            final = c_scr[h]
            cst_ref[0, h] = final[:ML_DIM].T
            nst_ref[0, h:h + 1, :] = final[ML_DIM:ML_DIM + 1]
        mst_ref[0] = m_scr[...]


def _prompt_fused(sinks, x, p, cos, sin, g_mix, w_proj, b_proj, weights, batch, seq, tm):
    assert len(weights) == N_FUSED_WEIGHTS
    rows = batch * seq
    n_tiles = rows // tm
    tps = seq // tm
    t0 = lambda i: jnp.minimum(i, n_tiles - 1)
    t1 = lambda i: jnp.clip(i - 1, 0, n_tiles - 1)
    t2 = lambda i: jnp.maximum(i - 2, 0)
    const = lambda i: (0, 0)
    single = lambda a: pl.BlockSpec(a.shape, const, pipeline_mode=pl.Buffered(1))
    slot = lambda shape, dt: pltpu.VMEM((2,) + shape, dt)
    return pl.pallas_call(
        functools.partial(_prompt_kernel, tiles_per_seq=tps),
        grid=(n_tiles + 2,),
        in_specs=[
            pl.BlockSpec(memory_space=pltpu.SMEM),
            pl.BlockSpec((tm, D_MODEL), lambda i: (t0(i), 0)),
            pl.BlockSpec((tm, LANES), lambda i: (t0(i) % tps, 0)),
            pl.BlockSpec((tm, LANES), lambda i: (t0(i) % tps, 0)),
            pl.BlockSpec((tm, D_MODEL), lambda i: (t2(i), 0)),
            pl.BlockSpec((tm, D_PLE), lambda i: (t2(i), 0)),
            single(g_mix), single(w_proj), single(b_proj),
        ] + [single(w) for w in weights],
        out_specs=(
            pl.BlockSpec((tm, D_MODEL), lambda i: (t2(i), 0)),
            pl.BlockSpec((1, ATTN_KV, WINDOW), lambda i: (t0(i) // tps, 0, 0)),
            pl.BlockSpec((1, ATTN_KV, WINDOW), lambda i: (t0(i) // tps, 0, 0)),
            pl.BlockSpec((1, ML_HEADS, ML_DIM, ML_DIM), lambda i: (t1(i) // tps, 0, 0, 0)),
            pl.BlockSpec((1, 8, LANES), lambda i: (t1(i) // tps, 0, 0)),
            pl.BlockSpec((1, 8, LANES), lambda i: (t1(i) // tps, 0, 0)),
        ),
        out_shape=(
            jax.ShapeDtypeStruct((rows, D_MODEL), F32),
            jax.ShapeDtypeStruct((batch, ATTN_KV, WINDOW), F32),
            jax.ShapeDtypeStruct((batch, ATTN_KV, WINDOW), F32),
            jax.ShapeDtypeStruct((batch, ML_HEADS, ML_DIM, ML_DIM), F32),
            jax.ShapeDtypeStruct((batch, 8, LANES), F32),
            jax.ShapeDtypeStruct((batch, 8, LANES), F32),
        ),
        scratch_shapes=[
            slot((ATTN_Q, tm), BF16), slot((tm, ATTN_KV), BF16), slot((ATTN_KV, tm), BF16),
            slot((1, ML_W, tm), BF16), slot((tm, ML_W), BF16), slot((1, ML_W, tm), BF16),
            slot((tm, ML_W), F32), slot((tm, LANES), F32), slot((1, 8, tm), F32),
            pltpu.VMEM((WINDOW, ATTN_KV), BF16), pltpu.VMEM((ATTN_KV, WINDOW), BF16),
            slot((tm, ATTN_Q), BF16), slot((tm, ML_W), BF16),
            pltpu.VMEM((ML_HEADS, ML_STATE_ROWS, ML_DIM), F32), pltpu.VMEM((8, LANES), F32),
        ],
        compiler_params=pltpu.CompilerParams(dimension_semantics=("arbitrary",), vmem_limit_bytes=VMEM_LIMIT),
        name="prompt_fused",
    )(sinks, x, cos, sin, x, p, g_mix, w_proj, b_proj, *weights)


def _attn_decode_kernel(sink_ref, qm_ref, kn_ref, vn_ref, ck_ref, cv_ref, o_ref, ko_ref, vo_ref):
    bt = qm_ref.shape[0]
    qm = qm_ref[...]
    kn = kn_ref[...]
    vn = vn_ref[...]
    ck = ck_ref[...]
    cv = cv_ref[...]
    head = lax.broadcasted_iota(jnp.int32, (N_Q_HEADS, 1), 0)
    sink = jnp.zeros((N_Q_HEADS, 1), F32)
    for hd in range(N_Q_HEADS):
        sink = jnp.where(head == hd, sink_ref[hd], sink)
    s = jnp.einsum('bhc,bck->bhk', qm, ck.astype(BF16), preferred_element_type=F32)
    s_new = jnp.sum(qm.astype(F32) * kn[:, None, :], axis=2, keepdims=True)
    m = jnp.maximum(jnp.maximum(jnp.max(s, axis=2, keepdims=True), s_new), sink)
    p = jnp.exp(s - m)
    p_new = jnp.exp(s_new - m)
    den = jnp.sum(p, axis=2, keepdims=True) + p_new + jnp.exp(sink - m)
    o = jnp.einsum('bhk,bck->bhc', p.astype(BF16), cv.astype(BF16), preferred_element_type=F32)
    o_ref[...] = (o + p_new * vn[:, None, :]) / den
    pad = jnp.zeros((LANES - bt, ATTN_KV), F32)
    kn_t = jnp.concatenate([kn, pad], axis=0).T
    vn_t = jnp.concatenate([vn, pad], axis=0).T
    newest = lax.broadcasted_iota(jnp.int32, (ATTN_KV, WINDOW), 1) == WINDOW - 1
    for b in range(bt):
        ko_ref[b] = jnp.where(newest, kn_t[:, b:b + 1], pltpu.roll(ck[b], WINDOW - 1, 1))
        vo_ref[b] = jnp.where(newest, vn_t[:, b:b + 1], pltpu.roll(cv[b], WINDOW - 1, 1))


def _attn_decode(sinks, qmat, k_new, v_new, cache_k, cache_v, bt):
    nbatch = qmat.shape[0]
    b3 = lambda i: (i, 0, 0)
    b2 = lambda i: (i, 0)
    return pl.pallas_call(
        _attn_decode_kernel,
        grid=(nbatch // bt,),
        in_specs=[
            pl.BlockSpec(memory_space=pltpu.SMEM),
            pl.BlockSpec((bt, N_Q_HEADS, ATTN_KV), b3),
            pl.BlockSpec((bt, ATTN_KV), b2),
            pl.BlockSpec((bt, ATTN_KV), b2),
            pl.BlockSpec((bt, WINDOW, ATTN_KV), b3),
            pl.BlockSpec((bt, WINDOW, ATTN_KV), b3),
        ],
        out_specs=(
            pl.BlockSpec((bt, N_Q_HEADS, ATTN_KV), b3),
            pl.BlockSpec((bt, WINDOW, ATTN_KV), b3),
            pl.BlockSpec((bt, WINDOW, ATTN_KV), b3),
        ),
        out_shape=(
            jax.ShapeDtypeStruct((nbatch, N_Q_HEADS, ATTN_KV), F32),
            jax.ShapeDtypeStruct((nbatch, WINDOW, ATTN_KV), F32),
            jax.ShapeDtypeStruct((nbatch, WINDOW, ATTN_KV), F32),
        ),
        compiler_params=pltpu.CompilerParams(dimension_semantics=("parallel",)),
        name="attn_decode",
    )(sinks, qmat, k_new, v_new, cache_k, cache_v)


def _mlstm_decode_kernel(q_ref, k_ref, v_ref, og_ref, gif_ref, c_ref, n_ref, m_ref, h_ref, co_ref, no_ref, mo_ref):
    bt = q_ref.shape[0]
    gif = gif_ref[...]
    ig = gif[:, 0:ML_HEADS]
    lf = gif[:, ML_HEADS:2 * ML_HEADS]
    m_prev = m_ref[...]
    a = lf + m_prev
    m_t = jnp.maximum(a, ig)
    aw = jnp.exp(a - m_t)
    e_i = jnp.exp(ig - m_t)
    floor = jnp.exp(-m_t)
    m_new = jnp.maximum(a, ig)
    decay = jnp.exp(a - m_new)
    w = jnp.exp(ig - m_new)
    mo_ref[...] = m_new
    pad = jnp.zeros((LANES - bt, ML_DIM), F32)
    lane = lax.broadcasted_iota(jnp.int32, (ML_DIM, LANES), 1)
    for h in range(ML_HEADS):
        hs = slice(h * ML_DIM, (h + 1) * ML_DIM)
        q = q_ref[:, hs]
        k = k_ref[:, hs]
        v = v_ref[:, hs]
        n = n_ref[:, h, :]
        qk = jnp.sum(q * k, axis=1, keepdims=True)
        qn = jnp.sum(q * n, axis=1, keepdims=True)
        sw = qk * e_i[:, h:h + 1]
        den = aw[:, h:h + 1] * qn + sw
        inv = 1.0 / jnp.maximum(jnp.abs(den), floor[:, h:h + 1])
        no_ref[:, h, :] = decay[:, h:h + 1] * n + w[:, h:h + 1] * k
        q_b = q.astype(BF16)
        wk_t = jnp.concatenate([w[:, h:h + 1] * k, pad], axis=0).T.astype(BF16)
        v_rows = jnp.concatenate([v, pad], axis=0).astype(BF16)
        nums = []
        for b in range(bt):
            C = c_ref[b, h]
            q_c = _mm(q_b, C.astype(BF16))[b:b + 1, :]
            nums.append(aw[b:b + 1, h:h + 1] * q_c)
            outer = _mm(jnp.where(lane == b, wk_t, jnp.zeros_like(wk_t)), v_rows)
            co_ref[b, h] = decay[b:b + 1, h:h + 1] * C + outer
        num = jnp.concatenate(nums, axis=0) + sw * v
        h_ref[:, hs] = (num * inv * og_ref[:, hs]).astype(h_ref.dtype)


def _mlstm_decode(mq, mk, mv, og, gif, state_c, state_n, state_m, bt):
    nbatch = mq.shape[0]
    b2 = lambda i: (i, 0)
    b3 = lambda i: (i, 0, 0)
    b4 = lambda i: (i, 0, 0, 0)
    return pl.pallas_call(
        _mlstm_decode_kernel,
        grid=(nbatch // bt,),
        in_specs=[
            pl.BlockSpec((bt, ML_W), b2),
            pl.BlockSpec((bt, ML_W), b2),
            pl.BlockSpec((bt, ML_W), b2),
            pl.BlockSpec((bt, ML_W), b2),
            pl.BlockSpec((bt, LANES), b2),
            pl.BlockSpec((bt, ML_HEADS, ML_DIM, ML_DIM), b4),
            pl.BlockSpec((bt, ML_HEADS, ML_DIM), b3),
            pl.BlockSpec((bt, ML_HEADS), b2),
        ],
        out_specs=(
            pl.BlockSpec((bt, ML_W), b2),
            pl.BlockSpec((bt, ML_HEADS, ML_DIM, ML_DIM), b4),
            pl.BlockSpec((bt, ML_HEADS, ML_DIM), b3),
            pl.BlockSpec((bt, ML_HEADS), b2),
        ),
        out_shape=(
            jax.ShapeDtypeStruct((nbatch, ML_W), BF16),
            jax.ShapeDtypeStruct((nbatch, ML_HEADS, ML_DIM, ML_DIM), F32),
            jax.ShapeDtypeStruct((nbatch, ML_HEADS, ML_DIM), F32),
            jax.ShapeDtypeStruct((nbatch, ML_HEADS), F32),
        ),
        compiler_params=pltpu.CompilerParams(dimension_semantics=("parallel",)),
        name="mlstm_decode",
    )(mq, mk, mv, og, gif, state_c, state_n, state_m)


def _rope_tables(pos):
    half = HEAD_DIM // 2
    inv = (1.0 / (ROPE_THETA ** (np.arange(half, dtype=np.float32) / half))).astype(np.float32)
    ang = pos.astype(np.float32)[:, None] * inv[None, :]
    cos = np.cos(ang)
    sin = np.sin(ang)
    cos_t = np.concatenate([cos, cos, cos, cos], axis=1)
    sin_t = np.concatenate([-sin, sin, -sin, sin], axis=1)
    return jnp.asarray(cos_t, F32), jnp.asarray(sin_t, F32)


def kernel(x_prompt, x_sample, cache_k, cache_v, state_C, state_n, state_m, p_prompt, p_sample, norm_mix, w_in, b_in, attn_sinks, w_attn_up, w_ml_up, w_o, norm_mlp, w_ff1, w_ff2, norm_ple, w_ple_gate, w_ple_proj, norm_final):
    assert w_in.shape[0] == 1, "single layer"
    batch, seq, _ = x_prompt.shape
    nbatch = x_sample.shape[0]
    assert x_sample.shape[1] == 1

    w_proj, w_gate, w_au, w_mu, w_out, w_f1, w_f2, w_pg, w_pp = _weight_cast(
        w_in[0].T, (w_attn_up[0], w_ml_up[0], w_o[0], w_ff1[0], w_ff2[0], w_ple_gate[0], w_ple_proj[0]))
    b_proj = b_in[:, :N_PROJ]
    b_gate = b_in[:, OFF_GA:]
    g_mix = norm_mix[0][None, :]
    out_weights = (
        g_mix, w_gate, b_gate, w_au, w_mu, w_out,
        norm_mlp[0][None, :], w_f1, w_f2,
        norm_ple[0][None, :], w_pg, w_pp, norm_final[None, :],
    )
    sinks = attn_sinks[0]
    to_fp = lambda c: c[0].transpose(0, 2, 3, 1).reshape(c.shape[1], ATTN_KV, WINDOW)
    from_fp = lambda c: c.reshape(c.shape[0], N_KV_HEADS, HEAD_DIM, WINDOW).transpose(0, 3, 1, 2)[None]

    cos_p, sin_p = _rope_tables(np.arange(seq, dtype=np.int32))
    xp = x_prompt.reshape(batch * seq, D_MODEL)
    y_prompt, k_last, v_last, cst, nst, mst = _prompt_fused(
        sinks, xp, p_prompt[0].reshape(batch * seq, D_PLE), cos_p, sin_p, g_mix, w_proj, b_proj, out_weights,
        batch, seq, tm=PROMPT_TILE)
    y_prompt = y_prompt.reshape(batch, seq, D_MODEL)
    k_prompt = from_fp(k_last)
    v_prompt = from_fp(v_last)
    c_prompt = cst[None]
    n_prompt = nst[:, :ML_HEADS, :][None]
    m_prompt = mst[:, :ML_HEADS, 0][None]

    cos_s, sin_s = _rope_tables(np.full((nbatch,), PAST_LEN, dtype=np.int32))
    xs = x_sample.reshape(nbatch, D_MODEL)
    qs, _, _, ks, vs, mqs, mks, mvs, ogs, gifs, _ = _in_proj_sample(xs, g_mix, w_proj, b_proj, cos_s, sin_s)
    qmat = qs.reshape(nbatch, N_Q_HEADS, ATTN_KV)
    o_full, k_s, v_s = _attn_decode(sinks, qmat, ks, vs, to_fp(cache_k), to_fp(cache_v), bt=DECODE_ROWS)
    o5 = o_full.reshape(nbatch, N_KV_HEADS, GQA_GROUP, N_KV_HEADS, HEAD_DIM)
    a_s = jnp.stack([o5[:, kv, :, kv, :] for kv in range(N_KV_HEADS)], axis=1).reshape(nbatch, ATTN_Q).astype(BF16)
    h_s, c_s, n_s, m_s = _mlstm_decode(mqs, mks, mvs, ogs, gifs, state_C[0], state_n[0], state_m[0], bt=DECODE_ROWS)
    y_sample = _out_proj_sample(xs, a_s, h_s, p_sample[0].reshape(nbatch, D_PLE), out_weights)
    y_sample = y_sample.reshape(nbatch, 1, D_MODEL)
    k_sample = from_fp(k_s)
    v_sample = from_fp(v_s)

    return (y_prompt, y_sample, k_prompt, v_prompt, c_prompt, n_prompt, m_prompt,
            k_sample, v_sample, c_s[None], n_s[None], m_s[None])
```

```python
import functools

import numpy as np
import jax
import jax.numpy as jnp
from jax import lax
from jax.experimental import pallas as pl
from jax.experimental.pallas import tpu as pltpu

F32 = jnp.float32
BF16 = jnp.bfloat16

D_MODEL = 1024
HEAD_DIM = 64
N_Q_HEADS = 8
N_KV_HEADS = 2
GQA_GROUP = N_Q_HEADS // N_KV_HEADS
WINDOW = 128
ROPE_THETA = 10000.0
ML_HEADS = 4
ML_DIM = 128
ML_CHUNK = 128
D_FF = 4 * D_MODEL
D_PLE = 256
EPS = 1e-6
PAST_LEN = 16384

ATTN_Q = N_Q_HEADS * HEAD_DIM
ATTN_KV = N_KV_HEADS * HEAD_DIM
ML_W = ML_HEADS * ML_DIM
LANES = 128

OFF_AQ = 0
OFF_AK = OFF_AQ + ATTN_Q
OFF_AV = OFF_AK + ATTN_KV
OFF_MQ = OFF_AV + ATTN_KV
OFF_MK = OFF_MQ + ML_W
OFF_MV = OFF_MK + ML_W
OFF_MO = OFF_MV + ML_W
OFF_MI = OFF_MO + ML_W
OFF_MF = OFF_MI + ML_HEADS
OFF_GA = OFF_MF + ML_HEADS
OFF_GM = OFF_GA + D_MODEL
D_IN = OFF_GM + D_MODEL
N_PROJ = OFF_MI + LANES

VMEM_LIMIT = 60 * 1024 * 1024
PROMPT_TILE = 256
DECODE_ROWS = 16

def _rms(x, g):
    r = lax.rsqrt(jnp.mean(x * x, axis=-1, keepdims=True) + EPS)
    return (x * r) * g


def _mm(a, b):
    return jnp.dot(a, b, preferred_element_type=F32)


def _mm_nt(a, b):
    return lax.dot_general(a, b, (((1,), (1,)), ((), ())), preferred_element_type=F32)


def _mm_tn(a, b):
    return lax.dot_general(a, b, (((0,), (0,)), ((), ())), preferred_element_type=F32)


def _log_sigmoid(x):
    return jnp.minimum(x, 0.0) - jnp.log1p(jnp.exp(-jnp.abs(x)))


def _weave(*streams, delays=None):
    delays = delays or [0] * len(streams)
    live = list(zip(streams, delays))
    rnd = 0
    while live:
        for g, d in list(live):
            if rnd >= d:
                try:
                    next(g)
                except StopIteration:
                    live.remove((g, d))
        rnd += 1


WEIGHT_ROW_SPLIT = 8


def _weight_cast_kernel(*refs):
    n_plain = (len(refs) - 3) // 2
    w_in_t_ref, plain_in = refs[0], refs[1:1 + n_plain]
    w_proj_ref, w_gate_ref, plain_out = refs[1 + n_plain], refs[2 + n_plain], refs[3 + n_plain:]
    for g in range(N_PROJ // LANES):
        w_proj_ref[:, g * LANES:(g + 1) * LANES] = w_in_t_ref[g * LANES:(g + 1) * LANES, :].T.astype(BF16)
    for g in range(2 * D_MODEL // LANES):
        w_gate_ref[:, g * LANES:(g + 1) * LANES] = w_in_t_ref[OFF_GA + g * LANES:OFF_GA + (g + 1) * LANES, :].T.astype(BF16)
    for src, dst in zip(plain_in, plain_out):
        dst[...] = src[...].astype(BF16)


def _weight_cast(w_in_t, plain):
    split = WEIGHT_ROW_SPLIT
    assert D_MODEL // split == LANES
    rows_blk = lambda w: pl.BlockSpec((w.shape[0] // split, w.shape[1]), lambda i: (i, 0))
    out_shapes = (jax.ShapeDtypeStruct((D_MODEL, N_PROJ), BF16), jax.ShapeDtypeStruct((D_MODEL, 2 * D_MODEL), BF16),
                  *(jax.ShapeDtypeStruct(w.shape, BF16) for w in plain))
    return pl.pallas_call(
        _weight_cast_kernel,
        grid=(split,),
        in_specs=[pl.BlockSpec((D_IN, LANES), lambda i: (0, i))] + [rows_blk(w) for w in plain],
        out_specs=tuple(rows_blk(s) for s in out_shapes),
        out_shape=out_shapes,
        compiler_params=pltpu.CompilerParams(dimension_semantics=("parallel",), vmem_limit_bytes=VMEM_LIMIT),
        name="weight_cast",
    )(w_in_t, *plain)


def _in_proj_stages(x, g_ref, w_ref, b_ref, cos_ref, sin_ref,
                    q_ref, k_ref, v_ref, kl_ref, vl_ref, mq_ref, mk_ref, mv_ref, og_ref, gif_ref, gt_ref,
                    *, feature_major):
    xn = _rms(x, g_ref[...]).astype(BF16)
    tm = x.shape[0]

    def proj(lo, hi):
        return _mm(xn, w_ref[:, lo:hi]) + b_ref[:, lo:hi]

    cos = cos_ref[...]
    sin = sin_ref[...]
    lane = lax.broadcasted_iota(jnp.int32, (1, LANES), 1)
    first_half = (lane % HEAD_DIM) < (HEAD_DIM // 2)

    def rope(z):
        partner = jnp.where(first_half, pltpu.roll(z, LANES - HEAD_DIM // 2, 1), pltpu.roll(z, HEAD_DIM // 2, 1))
        return z * cos + partner * sin

    zq = proj(OFF_AQ, OFF_AK)
    for c in range(ATTN_Q // LANES):
        sl = slice(c * LANES, (c + 1) * LANES)
        qc = rope(zq[:, sl]) * (HEAD_DIM ** -0.5)
        if feature_major:
            q_ref[sl, :] = qc.T.astype(q_ref.dtype)
        else:
            kv = (2 * c) // GQA_GROUP
            own = (lane // HEAD_DIM) == kv
            swapped = pltpu.roll(qc, HEAD_DIM, 1)
            for j, src in enumerate((qc, swapped) if kv == 0 else (swapped, qc)):
                hd = 2 * c + j
                q_ref[:, hd * LANES:(hd + 1) * LANES] = jnp.where(own, src, 0.0).astype(q_ref.dtype)
    yield
    zkv = proj(OFF_AK, OFF_MQ)
    k = rope(zkv[:, :ATTN_KV])
    v = zkv[:, ATTN_KV:]
    k_ref[...] = k.astype(k_ref.dtype)
    v_ref[...] = (v.T if feature_major else v).astype(v_ref.dtype)
    if feature_major:
        kl_ref[0] = k[tm - WINDOW:, :].T
        vl_ref[0] = v[tm - WINDOW:, :].T
    else:
        kl_ref[...] = k[tm - WINDOW:, :]
        vl_ref[...] = v[tm - WINDOW:, :]
    yield
    for z_off, dst in ((OFF_MQ, mq_ref), (OFF_MV, mv_ref)):
        z = proj(z_off, z_off + ML_W)
        if feature_major:
            for h in range(ML_HEADS):
                hs = slice(h * ML_DIM, (h + 1) * ML_DIM)
                dst[0, hs, :] = z[:, hs].T.astype(dst.dtype)
        else:
            dst[...] = z.astype(dst.dtype)
        yield
    mk_ref[...] = (proj(OFF_MK, OFF_MV) * (ML_DIM ** -0.5)).astype(mk_ref.dtype)
    yield
    og_ref[...] = jax.nn.sigmoid(proj(OFF_MO, OFF_MI))
    yield
    zg = proj(OFF_MI, N_PROJ)
    gif = jnp.where(lane < ML_HEADS, zg, jnp.where(lane < 2 * ML_HEADS, _log_sigmoid(zg), 0.0))
    if feature_major:
        L = ML_CHUNK
        upper = (lax.broadcasted_iota(jnp.int32, (L, L), 0) <= lax.broadcasted_iota(jnp.int32, (L, L), 1)).astype(F32)
        row8 = lax.broadcasted_iota(jnp.int32, (8, 1), 0)
        fill = jnp.zeros((L - 16, L), F32)
        for c in range(tm // L):
            cs = slice(c * L, (c + 1) * L)
            g_rows = gif[cs, :].T[0:8, :]
            cum = jnp.dot(g_rows, upper, precision=lax.Precision.HIGHEST, preferred_element_type=F32)
            rows = jnp.where(row8 < ML_HEADS, g_rows, cum)
            gt_ref[0, :, cs] = rows
            diff = rows - pltpu.roll(rows, ML_HEADS, 0)
            gif_ref[cs, :] = jnp.concatenate([rows, diff, fill], axis=0).T
    else:
        gif_ref[...] = gif
        gt_ref[0] = jnp.zeros(gt_ref.shape[1:], F32)
    yield


def _in_proj_kernel(x_ref, *refs, feature_major):
    _weave(_in_proj_stages(x_ref[...], *refs, feature_major=feature_major))


def _in_proj_sample(x, g, w, b, cos, sin):
    rows = x.shape[0]
    assert rows == WINDOW
    full = lambda n, dt: (jax.ShapeDtypeStruct((rows, n), dt), pl.BlockSpec((rows, n), lambda i: (0, 0)))
    outs = (
        full(N_Q_HEADS * LANES, BF16), full(ATTN_KV, BF16), full(ATTN_KV, BF16), full(ATTN_KV, F32), full(ATTN_KV, F32),
        full(ML_W, F32), full(ML_W, F32), full(ML_W, F32), full(ML_W, F32), full(LANES, F32),
        (jax.ShapeDtypeStruct((1, 8, rows), F32), pl.BlockSpec((1, 8, rows), lambda i: (0, 0, 0))),
    )
    return pl.pallas_call(
        functools.partial(_in_proj_kernel, feature_major=False),
        grid=(1,),
        in_specs=[pl.BlockSpec(a.shape, lambda i: (0, 0)) for a in (x, g, w, b, cos, sin)],
        out_specs=tuple(o[1] for o in outs),
        out_shape=tuple(o[0] for o in outs),
        compiler_params=pltpu.CompilerParams(dimension_semantics=("arbitrary",), vmem_limit_bytes=VMEM_LIMIT),
        name="in_proj",
    )(x, g, w, b, cos, sin)


def _attn_stages(sink_ref, qt_ref, kp_ref, kc_ref, vtp_ref, vtc_ref, first_block, emit):
    L = WINDOW
    nsub = kc_ref.shape[0] // L
    key = lax.broadcasted_iota(jnp.int32, (2 * L, L), 0)
    t = lax.broadcasted_iota(jnp.int32, (2 * L, L), 1)
    is_prev = key < L
    allowed = (is_prev & (key >= t)) | (~is_prev & (key - L <= t))
    bias = jnp.where(allowed, 0.0, -jnp.inf)
    bias_first = jnp.where(is_prev & first_block, -jnp.inf, bias)
    zeros_q = jnp.zeros((HEAD_DIM, L), BF16)
    heads = range(N_Q_HEADS)
    for r in range(nsub):
        ts = slice(r * L, (r + 1) * L)
        k_prev = kp_ref[...] if r == 0 else kc_ref[(r - 1) * L:r * L, :]
        vt_prev = vtp_ref[...] if r == 0 else vtc_ref[:, (r - 1) * L:r * L]
        k2 = jnp.concatenate([k_prev, kc_ref[ts, :]], axis=0)
        vt2 = jnp.concatenate([vt_prev, vtc_ref[:, ts]], axis=1)
        b_r = bias_first if r == 0 else bias
        q_op = [jnp.concatenate([qt_ref[hd * HEAD_DIM:(hd + 1) * HEAD_DIM, ts], zeros_q][::1 if hd < GQA_GROUP else -1],
                                axis=0) for hd in heads]
        s = [_mm(k2, q_op[hd]) + b_r for hd in heads]
        yield
        m = [jnp.maximum(jnp.max(s[hd], axis=0, keepdims=True), sink_ref[hd]) for hd in heads]
        yield
        p = [jnp.exp(s[hd] - m[hd]) for hd in heads]
        yield
        den = [jnp.sum(p[hd], axis=0, keepdims=True) + jnp.exp(sink_ref[hd] - m[hd]) for hd in heads]
        o = [_mm(vt2[(hd // GQA_GROUP) * HEAD_DIM:(hd // GQA_GROUP + 1) * HEAD_DIM, :], p[hd].astype(BF16)) / den[hd]
             for hd in heads]
        yield
        emit(r, jnp.concatenate(o, axis=0).T)
        yield


ML_STATE_ROWS = ML_DIM + 8


def _mlstm_stages(chains, emit):
    L = ML_CHUNK
    s_idx = lax.broadcasted_iota(jnp.int32, (L, L), 0)
    t_idx = lax.broadcasted_iota(jnp.int32, (L, L), 1)
    causal = s_idx <= t_idx
    n = range(len(chains))
    dlog = [jnp.where(causal, ch["b_r"] + ch["u_c"], -jnp.inf) for ch in chains]
    yield
    qk = [_mm(ch["k"], ch["qt"]) for ch in chains]
    inter = [_mm(ch["state"].astype(BF16), ch["qt"]) for ch in chains]
    yield
    a = [ch["b_r"] + ch["m_prev"] for ch in chains]
    m_t = [jnp.maximum(a[i], jnp.max(dlog[i], axis=0, keepdims=True)) for i in n]
    yield
    sw = [qk[i] * jnp.exp(dlog[i] - m_t[i]) for i in n]
    yield
    sv = [_mm(chains[i]["vt"], sw[i].astype(BF16)) for i in n]
    yield
    hs = []
    for i in n:
        aw = jnp.exp(a[i] - m_t[i])
        num = aw * inter[i][:ML_DIM] + sv[i]
        den = aw * inter[i][ML_DIM:ML_DIM + 1] + jnp.sum(sw[i], axis=0, keepdims=True)
        hh = num / jnp.maximum(jnp.abs(den), jnp.exp(-m_t[i]))
        hs.append(hh.T * chains[i]["og"])
    yield
    pad = jnp.zeros((ML_STATE_ROWS - ML_DIM - 1, L), F32)
    new_state, new_m = [], []
    for i in n:
        ch = chains[i]
        b_end = ch["b_r"][:, L - 1:L]
        g = b_end - ch["b_r"] + ch["ig_r"]
        m_new = jnp.maximum(b_end + ch["m_prev"], jnp.max(g, axis=1, keepdims=True))
        decay = jnp.exp(b_end + ch["m_prev"] - m_new)
        w = jnp.exp(g - m_new)
        v_aug = jnp.concatenate([ch["vt"].astype(F32) * w, w, pad], axis=0).astype(BF16)
        new_state.append(decay * ch["state"] + _mm(v_aug, ch["k"]))
        new_m.append(m_new)
    emit(hs, new_state, new_m)
    yield


DENSE_COLS = 512


def _dense_stages(x, a, hm, p, gmix_ref, wg_ref, bg_ref, wau_ref, wmu_ref, wo_ref,
                  gmlp_ref, wff1_ref, wff2_ref, gple_ref, wpg_ref, wpp_ref, gfin_ref, emit):
    nc = DENSE_COLS
    pieces = lambda n: [slice(c, c + nc) for c in range(0, n, nc)]
    xn = _rms(x, gmix_ref[...]).astype(BF16)
    gates = []
    for cs in pieces(2 * D_MODEL):
        gates.append(jax.nn.sigmoid(_mm(xn, wg_ref[:, cs]) + bg_ref[:, cs]))
        yield
    half = D_MODEL // nc
    mix = []
    for j, cs in enumerate(pieces(D_MODEL)):
        mix.append((gates[j] * _mm(a, wau_ref[:, cs]) + gates[half + j] * _mm(hm, wmu_ref[:, cs])).astype(BF16))
        yield
    mix = jnp.concatenate(mix, axis=1)
    h = []
    for cs in pieces(D_MODEL):
        h.append(x[:, cs] + _mm(mix, wo_ref[:, cs]))
        yield
    h = jnp.concatenate(h, axis=1)
    hn = _rms(h, gmlp_ref[...]).astype(BF16)
    u = []
    for cs in pieces(D_FF):
        u.append(jnp.square(jnp.maximum(_mm(hn, wff1_ref[:, cs]), 0.0)).astype(BF16))
        yield
    u = jnp.concatenate(u, axis=1)
    h2 = []
    for cs in pieces(D_MODEL):
        h2.append(h[:, cs] + _mm(u, wff2_ref[:, cs]))
        yield
    h = jnp.concatenate(h2, axis=1)
    hn = _rms(h, gple_ref[...]).astype(BF16)
    pb = p.astype(BF16)
    h3 = []
    for cs in pieces(D_MODEL):
        h3.append(h[:, cs] + _mm(pb, wpp_ref[:, cs]) * jax.nn.sigmoid(_mm(hn, wpg_ref[:, cs])))
        yield
    emit(_rms(jnp.concatenate(h3, axis=1), gfin_ref[...]))
    yield


def _out_kernel(x_ref, a_ref, hm_ref, p_ref, *rest):
    weights, y_ref = rest[:-1], rest[-1]

    def emit(y):
        y_ref[...] = y

    _weave(_dense_stages(x_ref[...], a_ref[...], hm_ref[...], p_ref[...], *weights, emit=emit))


def _out_proj_sample(x, a, hm, p, weights):
    operands = (x, a, hm, p, *weights)
    return pl.pallas_call(
        _out_kernel,
        grid=(1,),
        in_specs=[pl.BlockSpec(o.shape, lambda i: (0, 0), pipeline_mode=pl.Buffered(1)) for o in operands],
        out_specs=pl.BlockSpec(x.shape, lambda i: (0, 0)),
        out_shape=jax.ShapeDtypeStruct(x.shape, F32),
        compiler_params=pltpu.CompilerParams(dimension_semantics=("arbitrary",), vmem_limit_bytes=VMEM_LIMIT),
        name="out_proj",
    )(*operands)


N_FUSED_WEIGHTS = 13
N_PROJ_SLOTS = 9
PROJ_DELAY = 0


def _prompt_kernel(sink_ref, x0_ref, cos_ref, sin_ref, x2_ref, p_ref, gmix_ref, wp_ref, bp_ref, *rest,
                   tiles_per_seq, matrix_idx):
    weights = list(rest[:N_FUSED_WEIGHTS])
    y_ref, kl_ref, vl_ref, cst_ref, nst_ref, mst_ref = rest[N_FUSED_WEIGHTS:N_FUSED_WEIGHTS + 6]
    scr = rest[N_FUSED_WEIGHTS + 6:]
    proj_scr = scr[:N_PROJ_SLOTS]
    kprev_scr, vtprev_scr, a_scr, h_scr, c_scr, m_scr = scr[N_PROJ_SLOTS:N_PROJ_SLOTS + 6]
    w_scr, dma_sem = scr[N_PROJ_SLOTS + 6:-1], scr[-1]
    w_copies = [pltpu.make_async_copy(weights[j], w_scr[n], dma_sem.at[n]) for n, j in enumerate(matrix_idx)]
    for n, j in enumerate(matrix_idx):
        weights[j] = w_scr[n]
    i = pl.program_id(0)
    n_tiles = pl.num_programs(0) - 2
    t1 = jnp.clip(i - 1, 0, n_tiles - 1)
    tile_in_seq = t1 % tiles_per_seq
    s0, s1 = i % 2, (i + 1) % 2
    L = ML_CHUNK
    tm = x0_ref.shape[0]

    @pl.when(i == 0)
    def _():
        for c in w_copies:
            c.start()
        for r in (*proj_scr, kprev_scr, vtprev_scr, a_scr, h_scr, c_scr, m_scr):
            r[...] = jnp.zeros_like(r)

    @pl.when(i == 2)
    def _():
        for c in w_copies:
            c.wait()

    @pl.when(tile_in_seq == 0)
    def _():
        c_scr[...] = jnp.zeros_like(c_scr)
        m_scr[...] = jnp.zeros_like(m_scr)

    def projection():
        outs = [r.at[s0] for r in proj_scr]
        yield from _in_proj_stages(x0_ref[...], gmix_ref, wp_ref, bp_ref, cos_ref, sin_ref,
                                   outs[0], outs[1], outs[2], kl_ref, vl_ref, *outs[3:], feature_major=True)

    def branches():
        qt_ref, kc_ref, vtc_ref, mqt_ref, mk_ref, mvt_ref, og_ref, gif_ref, gt_ref = [r.at[s1] for r in proj_scr]
        slot = s1

        def emit_attn(r, blk):
            a_scr[slot, r * WINDOW:(r + 1) * WINDOW, :] = blk.astype(a_scr.dtype)

        yield from _attn_stages(sink_ref, qt_ref, kprev_scr, kc_ref, vtprev_scr, vtc_ref, tile_in_seq == 0, emit_attn)
        kprev_scr[...] = kc_ref[tm - WINDOW:, :]
        vtprev_scr[...] = vtc_ref[:, tm - WINDOW:]
        hsl = lambda h: slice(h * ML_DIM, (h + 1) * ML_DIM)
        carry = dict(state=[c_scr[h] for h in range(ML_HEADS)], m=[m_scr[h:h + 1, :] for h in range(ML_HEADS)])
        for c in range(tm // L):
            cs = slice(c * L, (c + 1) * L)

            def emit_ml(hs, new_state, new_m, cs=cs):
                for h in range(ML_HEADS):
                    h_scr[slot, cs, hsl(h)] = hs[h].astype(h_scr.dtype)
                carry["state"], carry["m"] = new_state, new_m

            chains = [dict(qt=mqt_ref[0, hsl(h), cs], k=mk_ref[cs, hsl(h)], vt=mvt_ref[0, hsl(h), cs],
                           og=og_ref[cs, hsl(h)], ig_r=gt_ref[0, h:h + 1, cs],
                           b_r=gt_ref[0, ML_HEADS + h:ML_HEADS + h + 1, cs],
                           u_c=gif_ref[cs, 2 * ML_HEADS + h:2 * ML_HEADS + h + 1],
                           state=carry["state"][h], m_prev=carry["m"][h]) for h in range(ML_HEADS)]
            yield from _mlstm_stages(chains, emit_ml)
        for h in range(ML_HEADS):
            c_scr[h] = carry["state"][h]
            m_scr[h:h + 1, :] = carry["m"][h]

    def dense():
        def emit_y(y):
            y_ref[...] = y

        return _dense_stages(x2_ref[...], a_scr[s0], h_scr[s0], p_ref[...], *weights, emit=emit_y)

    @pl.when(i < 2)
    def _():
        _weave(branches(), projection())

    @pl.when((i >= 2) & (i < n_tiles))
    def _():
        _weave(branches(), dense(), projection())

    @pl.when(i >= n_tiles)
    def _():
        _weave(branches(), dense())

    @pl.when((tile_in_seq == tiles_per_seq - 1) & (i >= 1) & (i <= n_tiles))
    def _():
        nst_ref[...] = jnp.zeros_like(nst_ref)
        for h in range(ML_HEADS):
            final = c_scr[h]
            cst_ref[0, h] = final[:ML_DIM].T
            nst_ref[0, h:h + 1, :] = final[ML_DIM:ML_DIM + 1]
        mst_ref[0] = m_scr[...]


def _prompt_fused(sinks, x, p, cos, sin, g_mix, w_proj, b_proj, weights, batch, seq, tm):
    assert len(weights) == N_FUSED_WEIGHTS
    rows = batch * seq
    n_tiles = rows // tm
    tps = seq // tm
    t0 = lambda i: jnp.minimum(i, n_tiles - 1)
    t1 = lambda i: jnp.clip(i - 1, 0, n_tiles - 1)
    t2 = lambda i: jnp.maximum(i - 2, 0)
    const = lambda i: (0, 0)
    single = lambda a: pl.BlockSpec(a.shape, const, pipeline_mode=pl.Buffered(1))
    slot = lambda shape, dt: pltpu.VMEM((2,) + shape, dt)
    matrix_idx = tuple(j for j, w in enumerate(weights) if w.dtype == BF16)
    matrices = [weights[j] for j in matrix_idx]
    return pl.pallas_call(
        functools.partial(_prompt_kernel, tiles_per_seq=tps, matrix_idx=matrix_idx),
        grid=(n_tiles + 2,),
        in_specs=[
            pl.BlockSpec(memory_space=pltpu.SMEM),
            pl.BlockSpec((tm, D_MODEL), lambda i: (t0(i), 0)),
            pl.BlockSpec((tm, LANES), lambda i: (t0(i) % tps, 0)),
            pl.BlockSpec((tm, LANES), lambda i: (t0(i) % tps, 0)),
            pl.BlockSpec((tm, D_MODEL), lambda i: (t2(i), 0)),
            pl.BlockSpec((tm, D_PLE), lambda i: (t2(i), 0)),
            single(g_mix), single(w_proj), single(b_proj),
        ] + [pl.BlockSpec(memory_space=pl.ANY) if j in matrix_idx else single(w) for j, w in enumerate(weights)],
        out_specs=(
            pl.BlockSpec((tm, D_MODEL), lambda i: (t2(i), 0)),
            pl.BlockSpec((1, ATTN_KV, WINDOW), lambda i: (t0(i) // tps, 0, 0)),
            pl.BlockSpec((1, ATTN_KV, WINDOW), lambda i: (t0(i) // tps, 0, 0)),
            pl.BlockSpec((1, ML_HEADS, ML_DIM, ML_DIM), lambda i: (t1(i) // tps, 0, 0, 0)),
            pl.BlockSpec((1, 8, LANES), lambda i: (t1(i) // tps, 0, 0)),
            pl.BlockSpec((1, 8, LANES), lambda i: (t1(i) // tps, 0, 0)),
        ),
        out_shape=(
            jax.ShapeDtypeStruct((rows, D_MODEL), F32),
            jax.ShapeDtypeStruct((batch, ATTN_KV, WINDOW), F32),
            jax.ShapeDtypeStruct((batch, ATTN_KV, WINDOW), F32),
            jax.ShapeDtypeStruct((batch, ML_HEADS, ML_DIM, ML_DIM), F32),
            jax.ShapeDtypeStruct((batch, 8, LANES), F32),
            jax.ShapeDtypeStruct((batch, 8, LANES), F32),
        ),
        scratch_shapes=[
            slot((ATTN_Q, tm), BF16), slot((tm, ATTN_KV), BF16), slot((ATTN_KV, tm), BF16),
            slot((1, ML_W, tm), BF16), slot((tm, ML_W), BF16), slot((1, ML_W, tm), BF16),
            slot((tm, ML_W), F32), slot((tm, LANES), F32), slot((1, 8, tm), F32),
            pltpu.VMEM((WINDOW, ATTN_KV), BF16), pltpu.VMEM((ATTN_KV, WINDOW), BF16),
            slot((tm, ATTN_Q), BF16), slot((tm, ML_W), BF16),
            pltpu.VMEM((ML_HEADS, ML_STATE_ROWS, ML_DIM), F32), pltpu.VMEM((8, LANES), F32),
            *(pltpu.VMEM(w.shape, w.dtype) for w in matrices), pltpu.SemaphoreType.DMA((len(matrices),)),
        ],
        compiler_params=pltpu.CompilerParams(dimension_semantics=("arbitrary",), vmem_limit_bytes=VMEM_LIMIT),
        name="prompt_fused",
    )(sinks, x, cos, sin, x, p, g_mix, w_proj, b_proj, *weights)


def _attn_decode_kernel(sink_ref, qm_ref, kn_ref, vn_ref, ck_ref, cv_ref, o_ref, ko_ref, vo_ref):
    bt = qm_ref.shape[0]
    qm = qm_ref[...]
    kn = kn_ref[...]
    vn = vn_ref[...]
    ck = ck_ref[...]
    cv = cv_ref[...]
    head = lax.broadcasted_iota(jnp.int32, (N_Q_HEADS, 1), 0)
    sink = jnp.zeros((N_Q_HEADS, 1), F32)
    for hd in range(N_Q_HEADS):
        sink = jnp.where(head == hd, sink_ref[hd], sink)
    s = jnp.einsum('bhc,bck->bhk', qm, ck.astype(BF16), preferred_element_type=F32)
    s_new = jnp.sum(qm.astype(F32) * kn[:, None, :], axis=2, keepdims=True)
    m = jnp.maximum(jnp.maximum(jnp.max(s, axis=2, keepdims=True), s_new), sink)
    p = jnp.exp(s - m)
    p_new = jnp.exp(s_new - m)
    den = jnp.sum(p, axis=2, keepdims=True) + p_new + jnp.exp(sink - m)
    o = jnp.einsum('bhk,bck->bhc', p.astype(BF16), cv.astype(BF16), preferred_element_type=F32)
    o_ref[...] = (o + p_new * vn[:, None, :]) / den
    pad = jnp.zeros((LANES - bt, ATTN_KV), F32)
    kn_t = jnp.concatenate([kn, pad], axis=0).T
    vn_t = jnp.concatenate([vn, pad], axis=0).T
    newest = lax.broadcasted_iota(jnp.int32, (ATTN_KV, WINDOW), 1) == WINDOW - 1
    for b in range(bt):
        ko_ref[b] = jnp.where(newest, kn_t[:, b:b + 1], pltpu.roll(ck[b], WINDOW - 1, 1))
        vo_ref[b] = jnp.where(newest, vn_t[:, b:b + 1], pltpu.roll(cv[b], WINDOW - 1, 1))


def _attn_decode(sinks, qmat, k_new, v_new, cache_k, cache_v, bt):
    nbatch = qmat.shape[0]
    b3 = lambda i: (i, 0, 0)
    b2 = lambda i: (i, 0)
    return pl.pallas_call(
        _attn_decode_kernel,
        grid=(nbatch // bt,),
        in_specs=[
            pl.BlockSpec(memory_space=pltpu.SMEM),
            pl.BlockSpec((bt, N_Q_HEADS, ATTN_KV), b3),
            pl.BlockSpec((bt, ATTN_KV), b2),
            pl.BlockSpec((bt, ATTN_KV), b2),
            pl.BlockSpec((bt, WINDOW, ATTN_KV), b3),
            pl.BlockSpec((bt, WINDOW, ATTN_KV), b3),
        ],
        out_specs=(
            pl.BlockSpec((bt, N_Q_HEADS, ATTN_KV), b3),
            pl.BlockSpec((bt, WINDOW, ATTN_KV), b3),
            pl.BlockSpec((bt, WINDOW, ATTN_KV), b3),
        ),
        out_shape=(
            jax.ShapeDtypeStruct((nbatch, N_Q_HEADS, ATTN_KV), F32),
            jax.ShapeDtypeStruct((nbatch, WINDOW, ATTN_KV), F32),
            jax.ShapeDtypeStruct((nbatch, WINDOW, ATTN_KV), F32),
        ),
        compiler_params=pltpu.CompilerParams(dimension_semantics=("parallel",)),
        name="attn_decode",
    )(sinks, qmat, k_new, v_new, cache_k, cache_v)


def _mlstm_decode_kernel(q_ref, k_ref, v_ref, og_ref, gif_ref, c_ref, n_ref, m_ref, h_ref, co_ref, no_ref, mo_ref):
    bt = q_ref.shape[0]
    gif = gif_ref[...]
    ig = gif[:, 0:ML_HEADS]
    lf = gif[:, ML_HEADS:2 * ML_HEADS]
    m_prev = m_ref[...]
    a = lf + m_prev
    m_t = jnp.maximum(a, ig)
    aw = jnp.exp(a - m_t)
    e_i = jnp.exp(ig - m_t)
    floor = jnp.exp(-m_t)
    m_new = jnp.maximum(a, ig)
    decay = jnp.exp(a - m_new)
    w = jnp.exp(ig - m_new)
    mo_ref[...] = m_new
    pad = jnp.zeros((LANES - bt, ML_DIM), F32)
    lane = lax.broadcasted_iota(jnp.int32, (ML_DIM, LANES), 1)
    for h in range(ML_HEADS):
        hs = slice(h * ML_DIM, (h + 1) * ML_DIM)
        q = q_ref[:, hs]
        k = k_ref[:, hs]
        v = v_ref[:, hs]
        n = n_ref[:, h, :]
        qk = jnp.sum(q * k, axis=1, keepdims=True)
        qn = jnp.sum(q * n, axis=1, keepdims=True)
        sw = qk * e_i[:, h:h + 1]
        den = aw[:, h:h + 1] * qn + sw
        inv = 1.0 / jnp.maximum(jnp.abs(den), floor[:, h:h + 1])
        no_ref[:, h, :] = decay[:, h:h + 1] * n + w[:, h:h + 1] * k
        q_b = q.astype(BF16)
        wk_t = jnp.concatenate([w[:, h:h + 1] * k, pad], axis=0).T.astype(BF16)
        v_rows = jnp.concatenate([v, pad], axis=0).astype(BF16)
        nums = []
        for b in range(bt):
            C = c_ref[b, h]
            q_c = _mm(q_b, C.astype(BF16))[b:b + 1, :]
            nums.append(aw[b:b + 1, h:h + 1] * q_c)
            outer = _mm(jnp.where(lane == b, wk_t, jnp.zeros_like(wk_t)), v_rows)
            co_ref[b, h] = decay[b:b + 1, h:h + 1] * C + outer
        num = jnp.concatenate(nums, axis=0) + sw * v
        h_ref[:, hs] = (num * inv * og_ref[:, hs]).astype(h_ref.dtype)


def _mlstm_decode(mq, mk, mv, og, gif, state_c, state_n, state_m, bt):
    nbatch = mq.shape[0]
    b2 = lambda i: (i, 0)
    b3 = lambda i: (i, 0, 0)
    b4 = lambda i: (i, 0, 0, 0)
    return pl.pallas_call(
        _mlstm_decode_kernel,
        grid=(nbatch // bt,),
        in_specs=[
            pl.BlockSpec((bt, ML_W), b2),
            pl.BlockSpec((bt, ML_W), b2),
            pl.BlockSpec((bt, ML_W), b2),
            pl.BlockSpec((bt, ML_W), b2),
            pl.BlockSpec((bt, LANES), b2),
            pl.BlockSpec((bt, ML_HEADS, ML_DIM, ML_DIM), b4),
            pl.BlockSpec((bt, ML_HEADS, ML_DIM), b3),
            pl.BlockSpec((bt, ML_HEADS), b2),
        ],
        out_specs=(
            pl.BlockSpec((bt, ML_W), b2),
            pl.BlockSpec((bt, ML_HEADS, ML_DIM, ML_DIM), b4),
            pl.BlockSpec((bt, ML_HEADS, ML_DIM), b3),
            pl.BlockSpec((bt, ML_HEADS), b2),
        ),
        out_shape=(
            jax.ShapeDtypeStruct((nbatch, ML_W), BF16),
            jax.ShapeDtypeStruct((nbatch, ML_HEADS, ML_DIM, ML_DIM), F32),
            jax.ShapeDtypeStruct((nbatch, ML_HEADS, ML_DIM), F32),
            jax.ShapeDtypeStruct((nbatch, ML_HEADS), F32),
        ),
        compiler_params=pltpu.CompilerParams(dimension_semantics=("parallel",)),
        name="mlstm_decode",
    )(mq, mk, mv, og, gif, state_c, state_n, state_m)


def _rope_tables(pos):
    half = HEAD_DIM // 2
    inv = (1.0 / (ROPE_THETA ** (np.arange(half, dtype=np.float32) / half))).astype(np.float32)
    ang = pos.astype(np.float32)[:, None] * inv[None, :]
    cos = np.cos(ang)
    sin = np.sin(ang)
    cos_t = np.concatenate([cos, cos, cos, cos], axis=1)
    sin_t = np.concatenate([-sin, sin, -sin, sin], axis=1)
    return jnp.asarray(cos_t, F32), jnp.asarray(sin_t, F32)


def kernel(x_prompt, x_sample, cache_k, cache_v, state_C, state_n, state_m, p_prompt, p_sample, norm_mix, w_in, b_in, attn_sinks, w_attn_up, w_ml_up, w_o, norm_mlp, w_ff1, w_ff2, norm_ple, w_ple_gate, w_ple_proj, norm_final):
    assert w_in.shape[0] == 1, "single layer"
    batch, seq, _ = x_prompt.shape
    nbatch = x_sample.shape[0]
    assert x_sample.shape[1] == 1

    w_proj, w_gate, w_au, w_mu, w_out, w_f1, w_f2, w_pg, w_pp = _weight_cast(
        w_in[0].T, (w_attn_up[0], w_ml_up[0], w_o[0], w_ff1[0], w_ff2[0], w_ple_gate[0], w_ple_proj[0]))
    b_proj = b_in[:, :N_PROJ]
    b_gate = b_in[:, OFF_GA:]
    g_mix = norm_mix[0][None, :]
    out_weights = (
        g_mix, w_gate, b_gate, w_au, w_mu, w_out,
        norm_mlp[0][None, :], w_f1, w_f2,
        norm_ple[0][None, :], w_pg, w_pp, norm_final[None, :],
    )
    sinks = attn_sinks[0]
    to_fp = lambda c: c[0].transpose(0, 2, 3, 1).reshape(c.shape[1], ATTN_KV, WINDOW)
    from_fp = lambda c: c.reshape(c.shape[0], N_KV_HEADS, HEAD_DIM, WINDOW).transpose(0, 3, 1, 2)[None]

    cos_p, sin_p = _rope_tables(np.arange(seq, dtype=np.int32))
    xp = x_prompt.reshape(batch * seq, D_MODEL)
    y_prompt, k_last, v_last, cst, nst, mst = _prompt_fused(
        sinks, xp, p_prompt[0].reshape(batch * seq, D_PLE), cos_p, sin_p, g_mix, w_proj, b_proj, out_weights,
        batch, seq, tm=PROMPT_TILE)
    y_prompt = y_prompt.reshape(batch, seq, D_MODEL)
    k_prompt = from_fp(k_last)
    v_prompt = from_fp(v_last)
    c_prompt = cst[None]
    n_prompt = nst[:, :ML_HEADS, :][None]
    m_prompt = mst[:, :ML_HEADS, 0][None]

    cos_s, sin_s = _rope_tables(np.full((nbatch,), PAST_LEN, dtype=np.int32))
    xs = x_sample.reshape(nbatch, D_MODEL)
    qs, _, _, ks, vs, mqs, mks, mvs, ogs, gifs, _ = _in_proj_sample(xs, g_mix, w_proj, b_proj, cos_s, sin_s)
    qmat = qs.reshape(nbatch, N_Q_HEADS, ATTN_KV)
    o_full, k_s, v_s = _attn_decode(sinks, qmat, ks, vs, to_fp(cache_k), to_fp(cache_v), bt=DECODE_ROWS)
    o5 = o_full.reshape(nbatch, N_KV_HEADS, GQA_GROUP, N_KV_HEADS, HEAD_DIM)
    a_s = jnp.stack([o5[:, kv, :, kv, :] for kv in range(N_KV_HEADS)], axis=1).reshape(nbatch, ATTN_Q).astype(BF16)
    h_s, c_s, n_s, m_s = _mlstm_decode(mqs, mks, mvs, ogs, gifs, state_C[0], state_n[0], state_m[0], bt=DECODE_ROWS)
    y_sample = _out_proj_sample(xs, a_s, h_s, p_sample[0].reshape(nbatch, D_PLE), out_weights)
    y_sample = y_sample.reshape(nbatch, 1, D_MODEL)
    k_sample = from_fp(k_s)
    v_sample = from_fp(v_s)

    return (y_prompt, y_sample, k_prompt, v_prompt, c_prompt, n_prompt, m_prompt,
            k_sample, v_sample, c_s[None], n_s[None], m_s[None])
```

```python
import functools

import numpy as np
import jax
import jax.numpy as jnp
from jax import lax
from jax.experimental import pallas as pl
from jax.experimental.pallas import tpu as pltpu

F32 = jnp.float32
BF16 = jnp.bfloat16

D_MODEL = 1024
HEAD_DIM = 64
N_Q_HEADS = 8
N_KV_HEADS = 2
GQA_GROUP = N_Q_HEADS // N_KV_HEADS
WINDOW = 128
ROPE_THETA = 10000.0
ML_HEADS = 4
ML_DIM = 128
ML_CHUNK = 128
D_FF = 4 * D_MODEL
D_PLE = 256
EPS = 1e-6
PAST_LEN = 16384

ATTN_Q = N_Q_HEADS * HEAD_DIM
ATTN_KV = N_KV_HEADS * HEAD_DIM
ML_W = ML_HEADS * ML_DIM
LANES = 128

OFF_AQ = 0
OFF_AK = OFF_AQ + ATTN_Q
OFF_AV = OFF_AK + ATTN_KV
OFF_MQ = OFF_AV + ATTN_KV
OFF_MK = OFF_MQ + ML_W
OFF_MV = OFF_MK + ML_W
OFF_MO = OFF_MV + ML_W
OFF_MI = OFF_MO + ML_W
OFF_MF = OFF_MI + ML_HEADS
OFF_GA = OFF_MF + ML_HEADS
OFF_GM = OFF_GA + D_MODEL
D_IN = OFF_GM + D_MODEL
N_PROJ = OFF_MI + LANES

VMEM_LIMIT = 60 * 1024 * 1024
PROMPT_TILE = 256

def _rms(x, g):
    r = lax.rsqrt(jnp.mean(x * x, axis=-1, keepdims=True) + EPS)
    return (x * r) * g


def _mm(a, b):
    return jnp.dot(a, b, preferred_element_type=F32)


def _mm_nt(a, b):
    return lax.dot_general(a, b, (((1,), (1,)), ((), ())), preferred_element_type=F32)


def _mm_tn(a, b):
    return lax.dot_general(a, b, (((0,), (0,)), ((), ())), preferred_element_type=F32)


def _log_sigmoid(x):
    return jnp.minimum(x, 0.0) - jnp.log1p(jnp.exp(-jnp.abs(x)))


def _weave(*streams, delays=None):
    delays = delays or [0] * len(streams)
    live = list(zip(streams, delays))
    rnd = 0
    while live:
        for g, d in list(live):
            if rnd >= d:
                try:
                    next(g)
                except StopIteration:
                    live.remove((g, d))
        rnd += 1


WEIGHT_ROW_SPLIT = 8


def _weight_cast_kernel(*refs):
    n_plain = (len(refs) - 3) // 2
    w_in_t_ref, plain_in = refs[0], refs[1:1 + n_plain]
    w_proj_ref, w_gate_ref, plain_out = refs[1 + n_plain], refs[2 + n_plain], refs[3 + n_plain:]
    for g in range(N_PROJ // LANES):
        w_proj_ref[:, g * LANES:(g + 1) * LANES] = w_in_t_ref[g * LANES:(g + 1) * LANES, :].T.astype(BF16)
    for g in range(2 * D_MODEL // LANES):
        w_gate_ref[:, g * LANES:(g + 1) * LANES] = w_in_t_ref[OFF_GA + g * LANES:OFF_GA + (g + 1) * LANES, :].T.astype(BF16)
    for src, dst in zip(plain_in, plain_out):
        dst[...] = src[...].astype(BF16)


def _weight_cast(w_in_t, plain):
    split = WEIGHT_ROW_SPLIT
    assert D_MODEL // split == LANES
    rows_blk = lambda w: pl.BlockSpec((w.shape[0] // split, w.shape[1]), lambda i: (i, 0))
    out_shapes = (jax.ShapeDtypeStruct((D_MODEL, N_PROJ), BF16), jax.ShapeDtypeStruct((D_MODEL, 2 * D_MODEL), BF16),
                  *(jax.ShapeDtypeStruct(w.shape, BF16) for w in plain))
    return pl.pallas_call(
        _weight_cast_kernel,
        grid=(split,),
        in_specs=[pl.BlockSpec((D_IN, LANES), lambda i: (0, i))] + [rows_blk(w) for w in plain],
        out_specs=tuple(rows_blk(s) for s in out_shapes),
        out_shape=out_shapes,
        compiler_params=pltpu.CompilerParams(dimension_semantics=("parallel",), vmem_limit_bytes=VMEM_LIMIT),
        name="weight_cast",
    )(w_in_t, *plain)


def _in_proj_stages(x, g_ref, w_ref, b_ref, cos_ref, sin_ref,
                    q_ref, k_ref, v_ref, kl_ref, vl_ref, mq_ref, mk_ref, mv_ref, og_ref, gif_ref, gt_ref,
                    *, feature_major):
    xn = _rms(x, g_ref[...]).astype(BF16)
    tm = x.shape[0]

    def proj(lo, hi):
        return _mm(xn, w_ref[:, lo:hi]) + b_ref[:, lo:hi]

    cos = cos_ref[...]
    sin = sin_ref[...]
    lane = lax.broadcasted_iota(jnp.int32, (1, LANES), 1)
    first_half = (lane % HEAD_DIM) < (HEAD_DIM // 2)

    def rope(z):
        partner = jnp.where(first_half, pltpu.roll(z, LANES - HEAD_DIM // 2, 1), pltpu.roll(z, HEAD_DIM // 2, 1))
        return z * cos + partner * sin

    zq = proj(OFF_AQ, OFF_AK)
    for c in range(ATTN_Q // LANES):
        sl = slice(c * LANES, (c + 1) * LANES)
        qc = rope(zq[:, sl]) * (HEAD_DIM ** -0.5)
        if feature_major:
            q_ref[sl, :] = qc.T.astype(q_ref.dtype)
        else:
            kv = (2 * c) // GQA_GROUP
            own = (lane // HEAD_DIM) == kv
            swapped = pltpu.roll(qc, HEAD_DIM, 1)
            for j, src in enumerate((qc, swapped) if kv == 0 else (swapped, qc)):
                hd = 2 * c + j
                q_ref[:, hd * LANES:(hd + 1) * LANES] = jnp.where(own, src, 0.0).astype(q_ref.dtype)
    yield
    zkv = proj(OFF_AK, OFF_MQ)
    k = rope(zkv[:, :ATTN_KV])
    v = zkv[:, ATTN_KV:]
    k_ref[...] = k.astype(k_ref.dtype)
    v_ref[...] = (v.T if feature_major else v).astype(v_ref.dtype)
    if feature_major:
        kl_ref[0] = k[tm - WINDOW:, :].T
        vl_ref[0] = v[tm - WINDOW:, :].T
    else:
        kl_ref[...] = k[tm - WINDOW:, :]
        vl_ref[...] = v[tm - WINDOW:, :]
    yield
    for z_off, dst in ((OFF_MQ, mq_ref), (OFF_MV, mv_ref)):
        z = proj(z_off, z_off + ML_W)
        if feature_major:
            for h in range(ML_HEADS):
                hs = slice(h * ML_DIM, (h + 1) * ML_DIM)
                dst[0, hs, :] = z[:, hs].T.astype(dst.dtype)
        else:
            dst[...] = z.astype(dst.dtype)
        yield
    mk_ref[...] = (proj(OFF_MK, OFF_MV) * (ML_DIM ** -0.5)).astype(mk_ref.dtype)
    yield
    og_ref[...] = jax.nn.sigmoid(proj(OFF_MO, OFF_MI))
    yield
    zg = proj(OFF_MI, N_PROJ)
    gif = jnp.where(lane < ML_HEADS, zg, jnp.where(lane < 2 * ML_HEADS, _log_sigmoid(zg), 0.0))
    if feature_major:
        L = ML_CHUNK
        upper = (lax.broadcasted_iota(jnp.int32, (L, L), 0) <= lax.broadcasted_iota(jnp.int32, (L, L), 1)).astype(F32)
        row8 = lax.broadcasted_iota(jnp.int32, (8, 1), 0)
        fill = jnp.zeros((L - 16, L), F32)
        for c in range(tm // L):
            cs = slice(c * L, (c + 1) * L)
            g_rows = gif[cs, :].T[0:8, :]
            cum = jnp.dot(g_rows, upper, precision=lax.Precision.HIGHEST, preferred_element_type=F32)
            rows = jnp.where(row8 < ML_HEADS, g_rows, cum)
            gt_ref[0, :, cs] = rows
            diff = rows - pltpu.roll(rows, ML_HEADS, 0)
            gif_ref[cs, :] = jnp.concatenate([rows, diff, fill], axis=0).T
    else:
        gif_ref[...] = gif
        gt_ref[0] = jnp.zeros(gt_ref.shape[1:], F32)
    yield


def _in_proj_kernel(x_ref, *refs, feature_major):
    _weave(_in_proj_stages(x_ref[...], *refs, feature_major=feature_major))


def _in_proj_sample(x, g, w, b, cos, sin):
    rows = x.shape[0]
    assert rows == WINDOW
    full = lambda n, dt: (jax.ShapeDtypeStruct((rows, n), dt), pl.BlockSpec((rows, n), lambda i: (0, 0)))
    outs = (
        full(N_Q_HEADS * LANES, BF16), full(ATTN_KV, BF16), full(ATTN_KV, BF16), full(ATTN_KV, F32), full(ATTN_KV, F32),
        full(ML_W, F32), full(ML_W, F32), full(ML_W, F32), full(ML_W, F32), full(LANES, F32),
        (jax.ShapeDtypeStruct((1, 8, rows), F32), pl.BlockSpec((1, 8, rows), lambda i: (0, 0, 0))),
    )
    return pl.pallas_call(
        functools.partial(_in_proj_kernel, feature_major=False),
        grid=(1,),
        in_specs=[pl.BlockSpec(a.shape, lambda i: (0, 0)) for a in (x, g, w, b, cos, sin)],
        out_specs=tuple(o[1] for o in outs),
        out_shape=tuple(o[0] for o in outs),
        compiler_params=pltpu.CompilerParams(dimension_semantics=("arbitrary",), vmem_limit_bytes=VMEM_LIMIT),
        name="in_proj",
    )(x, g, w, b, cos, sin)


def _attn_stages(sink_ref, qt_ref, kp_ref, kc_ref, vtp_ref, vtc_ref, first_block, emit):
    L = WINDOW
    nsub = kc_ref.shape[0] // L
    key = lax.broadcasted_iota(jnp.int32, (2 * L, L), 0)
    t = lax.broadcasted_iota(jnp.int32, (2 * L, L), 1)
    is_prev = key < L
    allowed = (is_prev & (key >= t)) | (~is_prev & (key - L <= t))
    bias = jnp.where(allowed, 0.0, -jnp.inf)
    bias_first = jnp.where(is_prev & first_block, -jnp.inf, bias)
    zeros_q = jnp.zeros((HEAD_DIM, L), BF16)
    heads = range(N_Q_HEADS)
    for r in range(nsub):
        ts = slice(r * L, (r + 1) * L)
        k_prev = kp_ref[...] if r == 0 else kc_ref[(r - 1) * L:r * L, :]
        vt_prev = vtp_ref[...] if r == 0 else vtc_ref[:, (r - 1) * L:r * L]
        k2 = jnp.concatenate([k_prev, kc_ref[ts, :]], axis=0)
        vt2 = jnp.concatenate([vt_prev, vtc_ref[:, ts]], axis=1)
        b_r = bias_first if r == 0 else bias
        q_op = [jnp.concatenate([qt_ref[hd * HEAD_DIM:(hd + 1) * HEAD_DIM, ts], zeros_q][::1 if hd < GQA_GROUP else -1],
                                axis=0) for hd in heads]
        s = [_mm(k2, q_op[hd]) + b_r for hd in heads]
        yield
        m = [jnp.maximum(jnp.max(s[hd], axis=0, keepdims=True), sink_ref[hd]) for hd in heads]
        yield
        p = [jnp.exp(s[hd] - m[hd]) for hd in heads]
        yield
        den = [jnp.sum(p[hd], axis=0, keepdims=True) + jnp.exp(sink_ref[hd] - m[hd]) for hd in heads]
        o = [_mm(vt2[(hd // GQA_GROUP) * HEAD_DIM:(hd // GQA_GROUP + 1) * HEAD_DIM, :], p[hd].astype(BF16)) / den[hd]
             for hd in heads]
        yield
        emit(r, jnp.concatenate(o, axis=0).T)
        yield


ML_STATE_ROWS = ML_DIM + 8


def _mlstm_stages(chains, emit):
    L = ML_CHUNK
    s_idx = lax.broadcasted_iota(jnp.int32, (L, L), 0)
    t_idx = lax.broadcasted_iota(jnp.int32, (L, L), 1)
    causal = s_idx <= t_idx
    n = range(len(chains))
    dlog = [jnp.where(causal, ch["b_r"] + ch["u_c"], -jnp.inf) for ch in chains]
    yield
    qk = [_mm(ch["k"], ch["qt"]) for ch in chains]
    inter = [_mm(ch["state"].astype(BF16), ch["qt"]) for ch in chains]
    yield
    a = [ch["b_r"] + ch["m_prev"] for ch in chains]
    m_t = [jnp.maximum(a[i], jnp.max(dlog[i], axis=0, keepdims=True)) for i in n]
    yield
    sw = [qk[i] * jnp.exp(dlog[i] - m_t[i]) for i in n]
    yield
    sv = [_mm(chains[i]["vt"], sw[i].astype(BF16)) for i in n]
    yield
    hs = []
    for i in n:
        aw = jnp.exp(a[i] - m_t[i])
        num = aw * inter[i][:ML_DIM] + sv[i]
        den = aw * inter[i][ML_DIM:ML_DIM + 1] + jnp.sum(sw[i], axis=0, keepdims=True)
        hh = num / jnp.maximum(jnp.abs(den), jnp.exp(-m_t[i]))
        hs.append(hh.T * chains[i]["og"])
    yield
    pad = jnp.zeros((ML_STATE_ROWS - ML_DIM - 1, L), F32)
    new_state, new_m = [], []
    for i in n:
        ch = chains[i]
        b_end = ch["b_r"][:, L - 1:L]
        g = b_end - ch["b_r"] + ch["ig_r"]
        m_new = jnp.maximum(b_end + ch["m_prev"], jnp.max(g, axis=1, keepdims=True))
        decay = jnp.exp(b_end + ch["m_prev"] - m_new)
        w = jnp.exp(g - m_new)
        v_aug = jnp.concatenate([ch["vt"].astype(F32) * w, w, pad], axis=0).astype(BF16)
        new_state.append(decay * ch["state"] + _mm(v_aug, ch["k"]))
        new_m.append(m_new)
    emit(hs, new_state, new_m)
    yield


DENSE_COLS = 512


def _dense_stages(x, a, hm, p, gmix_ref, wg_ref, bg_ref, wau_ref, wmu_ref, wo_ref,
                  gmlp_ref, wff1_ref, wff2_ref, gple_ref, wpg_ref, wpp_ref, gfin_ref, emit):
    nc = DENSE_COLS
    pieces = lambda n: [slice(c, c + nc) for c in range(0, n, nc)]
    xn = _rms(x, gmix_ref[...]).astype(BF16)
    gates = []
    for cs in pieces(2 * D_MODEL):
        gates.append(jax.nn.sigmoid(_mm(xn, wg_ref[:, cs]) + bg_ref[:, cs]))
        yield
    half = D_MODEL // nc
    mix = []
    for j, cs in enumerate(pieces(D_MODEL)):
        mix.append((gates[j] * _mm(a, wau_ref[:, cs]) + gates[half + j] * _mm(hm, wmu_ref[:, cs])).astype(BF16))
        yield
    mix = jnp.concatenate(mix, axis=1)
    h = []
    for cs in pieces(D_MODEL):
        h.append(x[:, cs] + _mm(mix, wo_ref[:, cs]))
        yield
    h = jnp.concatenate(h, axis=1)
    hn = _rms(h, gmlp_ref[...]).astype(BF16)
    u = []
    for cs in pieces(D_FF):
        u.append(jnp.square(jnp.maximum(_mm(hn, wff1_ref[:, cs]), 0.0)).astype(BF16))
        yield
    u = jnp.concatenate(u, axis=1)
    h2 = []
    for cs in pieces(D_MODEL):
        h2.append(h[:, cs] + _mm(u, wff2_ref[:, cs]))
        yield
    h = jnp.concatenate(h2, axis=1)
    hn = _rms(h, gple_ref[...]).astype(BF16)
    pb = p.astype(BF16)
    h3 = []
    for cs in pieces(D_MODEL):
        h3.append(h[:, cs] + _mm(pb, wpp_ref[:, cs]) * jax.nn.sigmoid(_mm(hn, wpg_ref[:, cs])))
        yield
    emit(_rms(jnp.concatenate(h3, axis=1), gfin_ref[...]))
    yield


def _out_kernel(x_ref, a_ref, hm_ref, p_ref, *rest):
    weights, y_ref = rest[:-1], rest[-1]

    def emit(y):
        y_ref[...] = y

    _weave(_dense_stages(x_ref[...], a_ref[...], hm_ref[...], p_ref[...], *weights, emit=emit))


def _out_proj_sample(x, a, hm, p, weights):
    operands = (x, a, hm, p, *weights)
    return pl.pallas_call(
        _out_kernel,
        grid=(1,),
        in_specs=[pl.BlockSpec(o.shape, lambda i: (0, 0), pipeline_mode=pl.Buffered(1)) for o in operands],
        out_specs=pl.BlockSpec(x.shape, lambda i: (0, 0)),
        out_shape=jax.ShapeDtypeStruct(x.shape, F32),
        compiler_params=pltpu.CompilerParams(dimension_semantics=("arbitrary",), vmem_limit_bytes=VMEM_LIMIT),
        name="out_proj",
    )(*operands)


N_FUSED_WEIGHTS = 13
N_SAMPLE_IN, N_SAMPLE_OUT = 13, 7
SAMPLE_ROWS = 4
SAMPLE_STEPS = 32
N_PROJ_SLOTS = 9
PROJ_DELAY = 0


def _prompt_kernel(sink_ref, x0_ref, cos_ref, sin_ref, x2_ref, p_ref, gmix_ref, wp_ref, bp_ref, *rest,
                   tiles_per_seq, matrix_idx):
    sample_in, rest = rest[:N_SAMPLE_IN], rest[N_SAMPLE_IN:]
    weights = list(rest[:N_FUSED_WEIGHTS])
    y_ref, kl_ref, vl_ref, cst_ref, nst_ref, mst_ref = rest[N_FUSED_WEIGHTS:N_FUSED_WEIGHTS + 6]
    sample_out = rest[N_FUSED_WEIGHTS + 6:N_FUSED_WEIGHTS + 6 + N_SAMPLE_OUT]
    scr = rest[N_FUSED_WEIGHTS + 6 + N_SAMPLE_OUT:]
    proj_scr = scr[:N_PROJ_SLOTS]
    kprev_scr, vtprev_scr, a_scr, h_scr, c_scr, m_scr = scr[N_PROJ_SLOTS:N_PROJ_SLOTS + 6]
    w_scr, dma_sem = scr[N_PROJ_SLOTS + 6:-1], scr[-1]
    w_copies = [pltpu.make_async_copy(weights[j], w_scr[n], dma_sem.at[n]) for n, j in enumerate(matrix_idx)]
    for n, j in enumerate(matrix_idx):
        weights[j] = w_scr[n]
    i = pl.program_id(0)
    n_tiles = pl.num_programs(0) - 2
    t1 = jnp.clip(i - 1, 0, n_tiles - 1)
    tile_in_seq = t1 % tiles_per_seq
    s0, s1 = i % 2, (i + 1) % 2
    L = ML_CHUNK
    tm = x0_ref.shape[0]

    @pl.when(i == 0)
    def _():
        for c in w_copies:
            c.start()
        for r in (*proj_scr, kprev_scr, vtprev_scr, a_scr, h_scr, c_scr, m_scr):
            r[...] = jnp.zeros_like(r)

    @pl.when(i == 2)
    def _():
        for c in w_copies:
            c.wait()

    @pl.when(tile_in_seq == 0)
    def _():
        c_scr[...] = jnp.zeros_like(c_scr)
        m_scr[...] = jnp.zeros_like(m_scr)

    def projection():
        outs = [r.at[s0] for r in proj_scr]
        yield from _in_proj_stages(x0_ref[...], gmix_ref, wp_ref, bp_ref, cos_ref, sin_ref,
                                   outs[0], outs[1], outs[2], kl_ref, vl_ref, *outs[3:], feature_major=True)

    def branches():
        qt_ref, kc_ref, vtc_ref, mqt_ref, mk_ref, mvt_ref, og_ref, gif_ref, gt_ref = [r.at[s1] for r in proj_scr]
        slot = s1

        def emit_attn(r, blk):
            a_scr[slot, r * WINDOW:(r + 1) * WINDOW, :] = blk.astype(a_scr.dtype)

        yield from _attn_stages(sink_ref, qt_ref, kprev_scr, kc_ref, vtprev_scr, vtc_ref, tile_in_seq == 0, emit_attn)
        kprev_scr[...] = kc_ref[tm - WINDOW:, :]
        vtprev_scr[...] = vtc_ref[:, tm - WINDOW:]
        hsl = lambda h: slice(h * ML_DIM, (h + 1) * ML_DIM)
        carry = dict(state=[c_scr[h] for h in range(ML_HEADS)], m=[m_scr[h:h + 1, :] for h in range(ML_HEADS)])
        for c in range(tm // L):
            cs = slice(c * L, (c + 1) * L)

            def emit_ml(hs, new_state, new_m, cs=cs):
                for h in range(ML_HEADS):
                    h_scr[slot, cs, hsl(h)] = hs[h].astype(h_scr.dtype)
                carry["state"], carry["m"] = new_state, new_m

            chains = [dict(qt=mqt_ref[0, hsl(h), cs], k=mk_ref[cs, hsl(h)], vt=mvt_ref[0, hsl(h), cs],
                           og=og_ref[cs, hsl(h)], ig_r=gt_ref[0, h:h + 1, cs],
                           b_r=gt_ref[0, ML_HEADS + h:ML_HEADS + h + 1, cs],
                           u_c=gif_ref[cs, 2 * ML_HEADS + h:2 * ML_HEADS + h + 1],
                           state=carry["state"][h], m_prev=carry["m"][h]) for h in range(ML_HEADS)]
            yield from _mlstm_stages(chains, emit_ml)
        for h in range(ML_HEADS):
            c_scr[h] = carry["state"][h]
            m_scr[h:h + 1, :] = carry["m"][h]

    def dense():
        def emit_y(y):
            y_ref[...] = y

        return _dense_stages(x2_ref[...], a_scr[s0], h_scr[s0], p_ref[...], *weights, emit=emit_y)

    @pl.when(i < 2)
    def _():
        _weave(branches(), projection())

    def sample_steps():
        qm, kn, vn, ck, cv, mq, mk, mv, og, gif, sc, sn, sm = sample_in
        o, ko, vo, hs, co, no, mo = sample_out
        yield from _attn_decode_stages(sink_ref, qm, kn.at[0], vn.at[0], ck, cv, o, ko, vo)
        yield from _mlstm_decode_stages(mq.at[0], mk.at[0], mv.at[0], og.at[0], gif.at[0], sc, sn, sm.at[0],
                                        hs.at[0], co, no, mo.at[0])

    @pl.when((i >= 2) & (i < 2 + SAMPLE_STEPS))
    def _():
        _weave(branches(), dense(), projection(), sample_steps())

    @pl.when((i >= 2 + SAMPLE_STEPS) & (i < n_tiles))
    def _():
        _weave(branches(), dense(), projection())

    @pl.when(i >= n_tiles)
    def _():
        _weave(branches(), dense())

    @pl.when((tile_in_seq == tiles_per_seq - 1) & (i >= 1) & (i <= n_tiles))
    def _():
        nst_ref[...] = jnp.zeros_like(nst_ref)
        for h in range(ML_HEADS):
            final = c_scr[h]
            cst_ref[0, h] = final[:ML_DIM].T
            nst_ref[0, h:h + 1, :] = final[ML_DIM:ML_DIM + 1]
        mst_ref[0] = m_scr[...]


def _prompt_fused(sinks, x, p, cos, sin, sample, g_mix, w_proj, b_proj, weights, batch, seq, tm):
    assert len(weights) == N_FUSED_WEIGHTS and len(sample) == N_SAMPLE_IN
    r = SAMPLE_ROWS
    nsamp = sample[0].shape[0]
    assert nsamp == r * SAMPLE_STEPS and SAMPLE_STEPS + 2 <= batch * seq // tm
    ds = lambda i: jnp.clip(i - 2, 0, SAMPLE_STEPS - 1)

    def sample_spec(a):
        lead = r if a.shape[0] == nsamp else 1
        return pl.BlockSpec((lead,) + a.shape[1:], lambda i: (ds(i),) + (0,) * (a.ndim - 1))

    qm, kn, vn, ck, cv, mq, mk, mv, og, gif, sc, sn, sm = sample
    sample_out_shapes = (
        jax.ShapeDtypeStruct(qm.shape, F32), jax.ShapeDtypeStruct(ck.shape, F32), jax.ShapeDtypeStruct(cv.shape, F32),
        jax.ShapeDtypeStruct(mq.shape, BF16), jax.ShapeDtypeStruct(sc.shape, F32), jax.ShapeDtypeStruct(sn.shape, F32),
        jax.ShapeDtypeStruct(sm.shape, F32),
    )
    rows = batch * seq
    n_tiles = rows // tm
    tps = seq // tm
    t0 = lambda i: jnp.minimum(i, n_tiles - 1)
    t1 = lambda i: jnp.clip(i - 1, 0, n_tiles - 1)
    t2 = lambda i: jnp.maximum(i - 2, 0)
    const = lambda i: (0, 0)
    single = lambda a: pl.BlockSpec(a.shape, const, pipeline_mode=pl.Buffered(1))
    slot = lambda shape, dt: pltpu.VMEM((2,) + shape, dt)
    matrix_idx = tuple(j for j, w in enumerate(weights) if w.dtype == BF16)
    matrices = [weights[j] for j in matrix_idx]
    return pl.pallas_call(
        functools.partial(_prompt_kernel, tiles_per_seq=tps, matrix_idx=matrix_idx),
        grid=(n_tiles + 2,),
        in_specs=[
            pl.BlockSpec(memory_space=pltpu.SMEM),
            pl.BlockSpec((tm, D_MODEL), lambda i: (t0(i), 0)),
            pl.BlockSpec((tm, LANES), lambda i: (t0(i) % tps, 0)),
            pl.BlockSpec((tm, LANES), lambda i: (t0(i) % tps, 0)),
            pl.BlockSpec((tm, D_MODEL), lambda i: (t2(i), 0)),
            pl.BlockSpec((tm, D_PLE), lambda i: (t2(i), 0)),
            single(g_mix), single(w_proj), single(b_proj),
        ] + [sample_spec(a) for a in sample] + [pl.BlockSpec(memory_space=pl.ANY) if j in matrix_idx else single(w) for j, w in enumerate(weights)],
        out_specs=(
            pl.BlockSpec((tm, D_MODEL), lambda i: (t2(i), 0)),
            pl.BlockSpec((1, ATTN_KV, WINDOW), lambda i: (t0(i) // tps, 0, 0)),
            pl.BlockSpec((1, ATTN_KV, WINDOW), lambda i: (t0(i) // tps, 0, 0)),
            pl.BlockSpec((1, ML_HEADS, ML_DIM, ML_DIM), lambda i: (t1(i) // tps, 0, 0, 0)),
            pl.BlockSpec((1, 8, LANES), lambda i: (t1(i) // tps, 0, 0)),
            pl.BlockSpec((1, 8, LANES), lambda i: (t1(i) // tps, 0, 0)),
            *(sample_spec(a) for a in sample_out_shapes),
        ),
        out_shape=(
            jax.ShapeDtypeStruct((rows, D_MODEL), F32),
            jax.ShapeDtypeStruct((batch, ATTN_KV, WINDOW), F32),
            jax.ShapeDtypeStruct((batch, ATTN_KV, WINDOW), F32),
            jax.ShapeDtypeStruct((batch, ML_HEADS, ML_DIM, ML_DIM), F32),
            jax.ShapeDtypeStruct((batch, 8, LANES), F32),
            jax.ShapeDtypeStruct((batch, 8, LANES), F32),
            *sample_out_shapes,
        ),
        scratch_shapes=[
            slot((ATTN_Q, tm), BF16), slot((tm, ATTN_KV), BF16), slot((ATTN_KV, tm), BF16),
            slot((1, ML_W, tm), BF16), slot((tm, ML_W), BF16), slot((1, ML_W, tm), BF16),
            slot((tm, ML_W), F32), slot((tm, LANES), F32), slot((1, 8, tm), F32),
            pltpu.VMEM((WINDOW, ATTN_KV), BF16), pltpu.VMEM((ATTN_KV, WINDOW), BF16),
            slot((tm, ATTN_Q), BF16), slot((tm, ML_W), BF16),
            pltpu.VMEM((ML_HEADS, ML_STATE_ROWS, ML_DIM), F32), pltpu.VMEM((8, LANES), F32),
            *(pltpu.VMEM(w.shape, w.dtype) for w in matrices), pltpu.SemaphoreType.DMA((len(matrices),)),
        ],
        compiler_params=pltpu.CompilerParams(dimension_semantics=("arbitrary",), vmem_limit_bytes=VMEM_LIMIT),
        name="prompt_fused",
    )(sinks, x, cos, sin, x, p, g_mix, w_proj, b_proj, *sample, *weights)


def _attn_decode_stages(sink_ref, qm_ref, kn_ref, vn_ref, ck_ref, cv_ref, o_ref, ko_ref, vo_ref):
    bt = qm_ref.shape[0]
    qm = qm_ref[...]
    kn = kn_ref[...]
    vn = vn_ref[...]
    ck = ck_ref[...]
    cv = cv_ref[...]
    head = lax.broadcasted_iota(jnp.int32, (N_Q_HEADS, 1), 0)
    sink = jnp.zeros((N_Q_HEADS, 1), F32)
    for hd in range(N_Q_HEADS):
        sink = jnp.where(head == hd, sink_ref[hd], sink)
    s = jnp.einsum('bhc,bck->bhk', qm, ck.astype(BF16), preferred_element_type=F32)
    s_new = jnp.sum(qm.astype(F32) * kn[:, None, :], axis=2, keepdims=True)
    m = jnp.maximum(jnp.maximum(jnp.max(s, axis=2, keepdims=True), s_new), sink)
    p = jnp.exp(s - m)
    p_new = jnp.exp(s_new - m)
    den = jnp.sum(p, axis=2, keepdims=True) + p_new + jnp.exp(sink - m)
    yield
    o = jnp.einsum('bhk,bck->bhc', p.astype(BF16), cv.astype(BF16), preferred_element_type=F32)
    o_ref[...] = (o + p_new * vn[:, None, :]) / den
    yield
    pad = jnp.zeros((LANES - bt, ATTN_KV), F32)
    kn_t = jnp.concatenate([kn, pad], axis=0).T
    vn_t = jnp.concatenate([vn, pad], axis=0).T
    newest = lax.broadcasted_iota(jnp.int32, (ATTN_KV, WINDOW), 1) == WINDOW - 1
    for b in range(bt):
        ko_ref[b] = jnp.where(newest, kn_t[:, b:b + 1], pltpu.roll(ck[b], WINDOW - 1, 1))
        vo_ref[b] = jnp.where(newest, vn_t[:, b:b + 1], pltpu.roll(cv[b], WINDOW - 1, 1))
    yield


def _mlstm_decode_stages(q_ref, k_ref, v_ref, og_ref, gif_ref, c_ref, n_ref, m_ref, h_ref, co_ref, no_ref, mo_ref):
    bt = q_ref.shape[0]
    gif = gif_ref[...]
    ig = gif[:, 0:ML_HEADS]
    lf = gif[:, ML_HEADS:2 * ML_HEADS]
    m_prev = m_ref[...]
    a = lf + m_prev
    m_t = jnp.maximum(a, ig)
    aw = jnp.exp(a - m_t)
    e_i = jnp.exp(ig - m_t)
    floor = jnp.exp(-m_t)
    m_new = jnp.maximum(a, ig)
    decay = jnp.exp(a - m_new)
    w = jnp.exp(ig - m_new)
    mo_ref[...] = m_new
    pad = jnp.zeros((LANES - bt, ML_DIM), F32)
    lane = lax.broadcasted_iota(jnp.int32, (ML_DIM, LANES), 1)
    for h in range(ML_HEADS):
        hs = slice(h * ML_DIM, (h + 1) * ML_DIM)
        q = q_ref[:, hs]
        k = k_ref[:, hs]
        v = v_ref[:, hs]
        n = n_ref[:, h, :]
        qk = jnp.sum(q * k, axis=1, keepdims=True)
        qn = jnp.sum(q * n, axis=1, keepdims=True)
        sw = qk * e_i[:, h:h + 1]
        den = aw[:, h:h + 1] * qn + sw
        inv = 1.0 / jnp.maximum(jnp.abs(den), floor[:, h:h + 1])
        no_ref[:, h, :] = decay[:, h:h + 1] * n + w[:, h:h + 1] * k
        q_b = q.astype(BF16)
        wk_t = jnp.concatenate([w[:, h:h + 1] * k, pad], axis=0).T.astype(BF16)
        v_rows = jnp.concatenate([v, pad], axis=0).astype(BF16)
        nums = []
        for b in range(bt):
            C = c_ref[b, h]
            q_c = _mm(q_b, C.astype(BF16))[b:b + 1, :]
            nums.append(aw[b:b + 1, h:h + 1] * q_c)
            outer = _mm(jnp.where(lane == b, wk_t, jnp.zeros_like(wk_t)), v_rows)
            co_ref[b, h] = decay[b:b + 1, h:h + 1] * C + outer
        num = jnp.concatenate(nums, axis=0) + sw * v
        h_ref[:, hs] = (num * inv * og_ref[:, hs]).astype(h_ref.dtype)
        yield


def _rope_tables(pos):
    half = HEAD_DIM // 2
    inv = (1.0 / (ROPE_THETA ** (np.arange(half, dtype=np.float32) / half))).astype(np.float32)
    ang = pos.astype(np.float32)[:, None] * inv[None, :]
    cos = np.cos(ang)
    sin = np.sin(ang)
    cos_t = np.concatenate([cos, cos, cos, cos], axis=1)
    sin_t = np.concatenate([-sin, sin, -sin, sin], axis=1)
    return jnp.asarray(cos_t, F32), jnp.asarray(sin_t, F32)


def kernel(x_prompt, x_sample, cache_k, cache_v, state_C, state_n, state_m, p_prompt, p_sample, norm_mix, w_in, b_in, attn_sinks, w_attn_up, w_ml_up, w_o, norm_mlp, w_ff1, w_ff2, norm_ple, w_ple_gate, w_ple_proj, norm_final):
    assert w_in.shape[0] == 1, "single layer"
    batch, seq, _ = x_prompt.shape
    nbatch = x_sample.shape[0]
    assert x_sample.shape[1] == 1

    w_proj, w_gate, w_au, w_mu, w_out, w_f1, w_f2, w_pg, w_pp = _weight_cast(
        w_in[0].T, (w_attn_up[0], w_ml_up[0], w_o[0], w_ff1[0], w_ff2[0], w_ple_gate[0], w_ple_proj[0]))
    b_proj = b_in[:, :N_PROJ]
    b_gate = b_in[:, OFF_GA:]
    g_mix = norm_mix[0][None, :]
    out_weights = (
        g_mix, w_gate, b_gate, w_au, w_mu, w_out,
        norm_mlp[0][None, :], w_f1, w_f2,
        norm_ple[0][None, :], w_pg, w_pp, norm_final[None, :],
    )
    sinks = attn_sinks[0]
    to_fp = lambda c: c[0].transpose(0, 2, 3, 1).reshape(c.shape[1], ATTN_KV, WINDOW)
    from_fp = lambda c: c.reshape(c.shape[0], N_KV_HEADS, HEAD_DIM, WINDOW).transpose(0, 3, 1, 2)[None]

    cos_s, sin_s = _rope_tables(np.full((nbatch,), PAST_LEN, dtype=np.int32))
    xs = x_sample.reshape(nbatch, D_MODEL)
    qs, _, _, ks, vs, mqs, mks, mvs, ogs, gifs, _ = _in_proj_sample(xs, g_mix, w_proj, b_proj, cos_s, sin_s)
    rows3 = lambda t: t.reshape(nbatch // SAMPLE_ROWS, SAMPLE_ROWS, t.shape[-1])
    sample = (qs.reshape(nbatch, N_Q_HEADS, ATTN_KV),
              rows3(ks), rows3(vs), to_fp(cache_k), to_fp(cache_v),
              rows3(mqs), rows3(mks), rows3(mvs), rows3(ogs), rows3(gifs),
              state_C[0], state_n[0], rows3(state_m[0]))

    cos_p, sin_p = _rope_tables(np.arange(seq, dtype=np.int32))
    xp = x_prompt.reshape(batch * seq, D_MODEL)
    y_prompt, k_last, v_last, cst, nst, mst, o_full, k_s, v_s, h_s, c_s, n_s, m_s = _prompt_fused(
        sinks, xp, p_prompt[0].reshape(batch * seq, D_PLE), cos_p, sin_p, sample, g_mix, w_proj, b_proj, out_weights,
        batch, seq, tm=PROMPT_TILE)
    y_prompt = y_prompt.reshape(batch, seq, D_MODEL)
    k_prompt = from_fp(k_last)
    v_prompt = from_fp(v_last)
    c_prompt = cst[None]
    n_prompt = nst[:, :ML_HEADS, :][None]
    m_prompt = mst[:, :ML_HEADS, 0][None]

    o5 = o_full.reshape(nbatch, N_KV_HEADS, GQA_GROUP, N_KV_HEADS, HEAD_DIM)
    a_s = jnp.stack([o5[:, kv, :, kv, :] for kv in range(N_KV_HEADS)], axis=1).reshape(nbatch, ATTN_Q).astype(BF16)
    y_sample = _out_proj_sample(xs, a_s, h_s.reshape(nbatch, ML_W), p_sample[0].reshape(nbatch, D_PLE), out_weights)
    y_sample = y_sample.reshape(nbatch, 1, D_MODEL)
    k_sample = from_fp(k_s)
    v_sample = from_fp(v_s)
    m_s = m_s.reshape(nbatch, ML_HEADS)

    return (y_prompt, y_sample, k_prompt, v_prompt, c_prompt, n_prompt, m_prompt,
            k_sample, v_sample, c_s[None], n_s[None], m_s[None])
```

```python
import functools

import numpy as np
import jax
import jax.numpy as jnp
from jax import lax
from jax.experimental import pallas as pl
from jax.experimental.pallas import tpu as pltpu

F32 = jnp.float32
BF16 = jnp.bfloat16

D_MODEL = 1024
HEAD_DIM = 64
N_Q_HEADS = 8
N_KV_HEADS = 2
GQA_GROUP = N_Q_HEADS // N_KV_HEADS
WINDOW = 128
ROPE_THETA = 10000.0
ML_HEADS = 4
ML_DIM = 128
ML_CHUNK = 128
D_FF = 4 * D_MODEL
D_PLE = 256
EPS = 1e-6
PAST_LEN = 16384

ATTN_Q = N_Q_HEADS * HEAD_DIM
ATTN_KV = N_KV_HEADS * HEAD_DIM
ML_W = ML_HEADS * ML_DIM
LANES = 128

OFF_AQ = 0
OFF_AK = OFF_AQ + ATTN_Q
OFF_AV = OFF_AK + ATTN_KV
OFF_MQ = OFF_AV + ATTN_KV
OFF_MK = OFF_MQ + ML_W
OFF_MV = OFF_MK + ML_W
OFF_MO = OFF_MV + ML_W
OFF_MI = OFF_MO + ML_W
OFF_MF = OFF_MI + ML_HEADS
OFF_GA = OFF_MF + ML_HEADS
OFF_GM = OFF_GA + D_MODEL
D_IN = OFF_GM + D_MODEL
N_PROJ = OFF_MI + LANES

VMEM_LIMIT = 60 * 1024 * 1024
PROMPT_TILE = 256
DECODE_ROWS = 16

def _rms(x, g):
    r = lax.rsqrt(jnp.mean(x * x, axis=-1, keepdims=True) + EPS)
    return (x * r) * g


def _mm(a, b):
    return jnp.dot(a, b, preferred_element_type=F32)


def _log_sigmoid(x):
    return jnp.minimum(x, 0.0) - jnp.log1p(jnp.exp(-jnp.abs(x)))


def _weave(*streams):
    live = list(streams)
    while live:
        for g in list(live):
            try:
                next(g)
            except StopIteration:
                live.remove(g)


class _AwaitedRef:
    def __init__(self, ref, copy):
        self._ref, self._copy = ref, copy

    def __getitem__(self, idx):
        if self._copy is not None:
            self._copy.wait()
            self._copy = None
        return self._ref[idx]


WEIGHT_ROW_SPLIT = 8


def _weight_cast_kernel(*refs):
    n_plain = (len(refs) - 3) // 2
    w_in_t_ref, plain_in = refs[0], refs[1:1 + n_plain]
    w_proj_ref, w_gate_ref, plain_out = refs[1 + n_plain], refs[2 + n_plain], refs[3 + n_plain:]
    for g in range(N_PROJ // LANES):
        w_proj_ref[:, g * LANES:(g + 1) * LANES] = w_in_t_ref[g * LANES:(g + 1) * LANES, :].T.astype(BF16)
    for g in range(2 * D_MODEL // LANES):
        w_gate_ref[:, g * LANES:(g + 1) * LANES] = w_in_t_ref[OFF_GA + g * LANES:OFF_GA + (g + 1) * LANES, :].T.astype(BF16)
    for src, dst in zip(plain_in, plain_out):
        dst[...] = src[...].astype(BF16)


def _weight_cast(w_in_t, plain):
    split = WEIGHT_ROW_SPLIT
    assert D_MODEL // split == LANES
    rows_blk = lambda w: pl.BlockSpec((w.shape[0] // split, w.shape[1]), lambda i: (i, 0))
    out_shapes = (jax.ShapeDtypeStruct((D_MODEL, N_PROJ), BF16), jax.ShapeDtypeStruct((D_MODEL, 2 * D_MODEL), BF16),
                  *(jax.ShapeDtypeStruct(w.shape, BF16) for w in plain))
    return pl.pallas_call(
        _weight_cast_kernel,
        grid=(split,),
        in_specs=[pl.BlockSpec((D_IN, LANES), lambda i: (0, i))] + [rows_blk(w) for w in plain],
        out_specs=tuple(rows_blk(s) for s in out_shapes),
        out_shape=out_shapes,
        compiler_params=pltpu.CompilerParams(dimension_semantics=("parallel",), vmem_limit_bytes=VMEM_LIMIT),
        name="weight_cast",
    )(w_in_t, *plain)


def _in_proj_stages(x, g_ref, w_ref, b_ref, cos_ref, sin_ref,
                    q_ref, k_ref, v_ref, kl_ref, vl_ref, mq_ref, mk_ref, mv_ref, og_ref, gif_ref, gt_ref,
                    *, feature_major):
    xn = _rms(x, g_ref[...]).astype(BF16)
    tm = x.shape[0]

    def proj(lo, hi):
        return _mm(xn, w_ref[:, lo:hi]) + b_ref[:, lo:hi]

    cos = cos_ref[...]
    sin = sin_ref[...]
    lane = lax.broadcasted_iota(jnp.int32, (1, LANES), 1)
    first_half = (lane % HEAD_DIM) < (HEAD_DIM // 2)

    def rope(z):
        partner = jnp.where(first_half, pltpu.roll(z, LANES - HEAD_DIM // 2, 1), pltpu.roll(z, HEAD_DIM // 2, 1))
        return z * cos + partner * sin

    zq = proj(OFF_AQ, OFF_AK)
    for c in range(ATTN_Q // LANES):
        sl = slice(c * LANES, (c + 1) * LANES)
        qc = rope(zq[:, sl]) * (HEAD_DIM ** -0.5)
        if feature_major:
            q_ref[sl, :] = qc.T.astype(q_ref.dtype)
        else:
            kv = (2 * c) // GQA_GROUP
            own = (lane // HEAD_DIM) == kv
            swapped = pltpu.roll(qc, HEAD_DIM, 1)
            for j, src in enumerate((qc, swapped) if kv == 0 else (swapped, qc)):
                hd = 2 * c + j
                q_ref[:, hd * LANES:(hd + 1) * LANES] = jnp.where(own, src, 0.0).astype(q_ref.dtype)
    yield
    zkv = proj(OFF_AK, OFF_MQ)
    k = rope(zkv[:, :ATTN_KV])
    v = zkv[:, ATTN_KV:]
    k_ref[...] = k.astype(k_ref.dtype)
    v_ref[...] = (v.T if feature_major else v).astype(v_ref.dtype)
    if feature_major:
        kl_ref[0] = k[tm - WINDOW:, :].T
        vl_ref[0] = v[tm - WINDOW:, :].T
    else:
        kl_ref[...] = k[tm - WINDOW:, :]
        vl_ref[...] = v[tm - WINDOW:, :]
    yield
    for z_off, dst in ((OFF_MQ, mq_ref), (OFF_MV, mv_ref)):
        z = proj(z_off, z_off + ML_W)
        if feature_major:
            for h in range(ML_HEADS):
                hs = slice(h * ML_DIM, (h + 1) * ML_DIM)
                dst[0, hs, :] = z[:, hs].T.astype(dst.dtype)
        else:
            dst[...] = z.astype(dst.dtype)
        yield
    mk_ref[...] = (proj(OFF_MK, OFF_MV) * (ML_DIM ** -0.5)).astype(mk_ref.dtype)
    yield
    og_ref[...] = jax.nn.sigmoid(proj(OFF_MO, OFF_MI))
    yield
    zg = proj(OFF_MI, N_PROJ)
    gif = jnp.where(lane < ML_HEADS, zg, jnp.where(lane < 2 * ML_HEADS, _log_sigmoid(zg), 0.0))
    if feature_major:
        L = ML_CHUNK
        upper = (lax.broadcasted_iota(jnp.int32, (L, L), 0) <= lax.broadcasted_iota(jnp.int32, (L, L), 1)).astype(F32)
        row8 = lax.broadcasted_iota(jnp.int32, (8, 1), 0)
        fill = jnp.zeros((L - 16, L), F32)
        for c in range(tm // L):
            cs = slice(c * L, (c + 1) * L)
            g_rows = gif[cs, :].T[0:8, :]
            cum = jnp.dot(g_rows, upper, precision=lax.Precision.HIGHEST, preferred_element_type=F32)
            rows = jnp.where(row8 < ML_HEADS, g_rows, cum)
            gt_ref[0, :, cs] = rows
            diff = rows - pltpu.roll(rows, ML_HEADS, 0)
            gif_ref[cs, :] = jnp.concatenate([rows, diff, fill], axis=0).T
    else:
        gif_ref[...] = gif
        gt_ref[0] = jnp.zeros(gt_ref.shape[1:], F32)
    yield


def _in_proj_kernel(x_ref, *refs, feature_major):
    _weave(_in_proj_stages(x_ref[...], *refs, feature_major=feature_major))


def _in_proj_sample(x, g, w, b, cos, sin):
    rows = x.shape[0]
    assert rows == WINDOW
    full = lambda n, dt: (jax.ShapeDtypeStruct((rows, n), dt), pl.BlockSpec((rows, n), lambda i: (0, 0)))
    outs = (
        full(N_Q_HEADS * LANES, BF16), full(ATTN_KV, BF16), full(ATTN_KV, BF16), full(ATTN_KV, F32), full(ATTN_KV, F32),
        full(ML_W, F32), full(ML_W, F32), full(ML_W, F32), full(ML_W, F32), full(LANES, F32),
        (jax.ShapeDtypeStruct((1, 8, rows), F32), pl.BlockSpec((1, 8, rows), lambda i: (0, 0, 0))),
    )
    return pl.pallas_call(
        functools.partial(_in_proj_kernel, feature_major=False),
        grid=(1,),
        in_specs=[pl.BlockSpec(a.shape, lambda i: (0, 0)) for a in (x, g, w, b, cos, sin)],
        out_specs=tuple(o[1] for o in outs),
        out_shape=tuple(o[0] for o in outs),
        compiler_params=pltpu.CompilerParams(dimension_semantics=("arbitrary",), vmem_limit_bytes=VMEM_LIMIT),
        name="in_proj",
    )(x, g, w, b, cos, sin)


def _attn_stages(sink_ref, qt_ref, kp_ref, kc_ref, vtp_ref, vtc_ref, first_block, emit):
    L = WINDOW
    nsub = kc_ref.shape[0] // L
    key = lax.broadcasted_iota(jnp.int32, (2 * L, L), 0)
    t = lax.broadcasted_iota(jnp.int32, (2 * L, L), 1)
    is_prev = key < L
    allowed = (is_prev & (key >= t)) | (~is_prev & (key - L <= t))
    bias = jnp.where(allowed, 0.0, -jnp.inf)
    bias_first = jnp.where(is_prev & first_block, -jnp.inf, bias)
    zeros_q = jnp.zeros((HEAD_DIM, L), BF16)
    heads = range(N_Q_HEADS)
    for r in range(nsub):
        ts = slice(r * L, (r + 1) * L)
        k_prev = kp_ref[...] if r == 0 else kc_ref[(r - 1) * L:r * L, :]
        vt_prev = vtp_ref[...] if r == 0 else vtc_ref[:, (r - 1) * L:r * L]
        k2 = jnp.concatenate([k_prev, kc_ref[ts, :]], axis=0)
        vt2 = jnp.concatenate([vt_prev, vtc_ref[:, ts]], axis=1)
        b_r = bias_first if r == 0 else bias
        q_op = [jnp.concatenate([qt_ref[hd * HEAD_DIM:(hd + 1) * HEAD_DIM, ts], zeros_q][::1 if hd < GQA_GROUP else -1],
                                axis=0) for hd in heads]
        s = [_mm(k2, q_op[hd]) + b_r for hd in heads]
        yield
        m = [jnp.maximum(jnp.max(s[hd], axis=0, keepdims=True), sink_ref[hd]) for hd in heads]
        yield
        p = [jnp.exp(s[hd] - m[hd]) for hd in heads]
        yield
        den = [jnp.sum(p[hd], axis=0, keepdims=True) + jnp.exp(sink_ref[hd] - m[hd]) for hd in heads]
        o = [_mm(vt2[(hd // GQA_GROUP) * HEAD_DIM:(hd // GQA_GROUP + 1) * HEAD_DIM, :], p[hd].astype(BF16)) / den[hd]
             for hd in heads]
        yield
        emit(r, jnp.concatenate(o, axis=0).T)
        yield


ML_STATE_ROWS = ML_DIM + 8


def _mlstm_stages(chains, emit):
    L = ML_CHUNK
    s_idx = lax.broadcasted_iota(jnp.int32, (L, L), 0)
    t_idx = lax.broadcasted_iota(jnp.int32, (L, L), 1)
    causal = s_idx <= t_idx
    n = range(len(chains))
    dlog = [jnp.where(causal, ch["b_r"] + ch["u_c"], -jnp.inf) for ch in chains]
    yield
    qk = [_mm(ch["k"], ch["qt"]) for ch in chains]
    inter = [_mm(ch["state"].astype(BF16), ch["qt"]) for ch in chains]
    yield
    a = [ch["b_r"] + ch["m_prev"] for ch in chains]
    m_t = [jnp.maximum(a[i], jnp.max(dlog[i], axis=0, keepdims=True)) for i in n]
    yield
    sw = [qk[i] * jnp.exp(dlog[i] - m_t[i]) for i in n]
    yield
    sv = [_mm(chains[i]["vt"], sw[i].astype(BF16)) for i in n]
    yield
    hs = []
    for i in n:
        aw = jnp.exp(a[i] - m_t[i])
        num = aw * inter[i][:ML_DIM] + sv[i]
        den = aw * inter[i][ML_DIM:ML_DIM + 1] + jnp.sum(sw[i], axis=0, keepdims=True)
        hh = num / jnp.maximum(jnp.abs(den), jnp.exp(-m_t[i]))
        hs.append(hh.T * chains[i]["og"])
    yield
    pad = jnp.zeros((ML_STATE_ROWS - ML_DIM - 1, L), F32)
    new_state, new_m = [], []
    for i in n:
        ch = chains[i]
        b_end = ch["b_r"][:, L - 1:L]
        g = b_end - ch["b_r"] + ch["ig_r"]
        m_new = jnp.maximum(b_end + ch["m_prev"], jnp.max(g, axis=1, keepdims=True))
        decay = jnp.exp(b_end + ch["m_prev"] - m_new)
        w = jnp.exp(g - m_new)
        v_aug = jnp.concatenate([ch["vt"].astype(F32) * w, w, pad], axis=0).astype(BF16)
        new_state.append(decay * ch["state"] + _mm(v_aug, ch["k"]))
        new_m.append(m_new)
    emit(hs, new_state, new_m)
    yield


DENSE_COLS = 512


def _dense_stages(x, a, hm, p, gmix_ref, wg_ref, bg_ref, wau_ref, wmu_ref, wo_ref,
                  gmlp_ref, wff1_ref, wff2_ref, gple_ref, wpg_ref, wpp_ref, gfin_ref, emit):
    nc = DENSE_COLS
    pieces = lambda n: [slice(c, c + nc) for c in range(0, n, nc)]
    xn = _rms(x, gmix_ref[...]).astype(BF16)
    gates = []
    for cs in pieces(2 * D_MODEL):
        gates.append(jax.nn.sigmoid(_mm(xn, wg_ref[:, cs]) + bg_ref[:, cs]))
        yield
    half = D_MODEL // nc
    mix = []
    for j, cs in enumerate(pieces(D_MODEL)):
        mix.append((gates[j] * _mm(a, wau_ref[:, cs]) + gates[half + j] * _mm(hm, wmu_ref[:, cs])).astype(BF16))
        yield
    mix = jnp.concatenate(mix, axis=1)
    h = []
    for cs in pieces(D_MODEL):
        h.append(x[:, cs] + _mm(mix, wo_ref[:, cs]))
        yield
    h = jnp.concatenate(h, axis=1)
    hn = _rms(h, gmlp_ref[...]).astype(BF16)
    u = []
    for cs in pieces(D_FF):
        u.append(jnp.square(jnp.maximum(_mm(hn, wff1_ref[:, cs]), 0.0)).astype(BF16))
        yield
    u = jnp.concatenate(u, axis=1)
    h2 = []
    for cs in pieces(D_MODEL):
        h2.append(h[:, cs] + _mm(u, wff2_ref[:, cs]))
        yield
    h = jnp.concatenate(h2, axis=1)
    hn = _rms(h, gple_ref[...]).astype(BF16)
    pb = p.astype(BF16)
    h3 = []
    for cs in pieces(D_MODEL):
        h3.append(h[:, cs] + _mm(pb, wpp_ref[:, cs]) * jax.nn.sigmoid(_mm(hn, wpg_ref[:, cs])))
        yield
    emit(_rms(jnp.concatenate(h3, axis=1), gfin_ref[...]))
    yield


def _out_kernel(x_ref, a_ref, hm_ref, p_ref, *rest, matrix_idx):
    weights, y_ref = list(rest[:N_FUSED_WEIGHTS]), rest[N_FUSED_WEIGHTS]
    w_scr, dma_sem = rest[N_FUSED_WEIGHTS + 1:-1], rest[-1]
    copies = [pltpu.make_async_copy(weights[j], w_scr[n], dma_sem.at[n]) for n, j in enumerate(matrix_idx)]
    for c in copies:
        c.start()
    for n, j in enumerate(matrix_idx):
        weights[j] = _AwaitedRef(w_scr[n], copies[n])

    def emit(y):
        y_ref[...] = y

    _weave(_dense_stages(x_ref[...], a_ref[...], hm_ref[...], p_ref[...], *weights, emit=emit))


def _out_proj_sample(x, a, hm, p, weights):
    assert len(weights) == N_FUSED_WEIGHTS
    matrix_idx = tuple(j for j, w in enumerate(weights) if w.dtype == BF16)
    matrices = [weights[j] for j in matrix_idx]
    whole = lambda o: pl.BlockSpec(o.shape, lambda i: (0, 0), pipeline_mode=pl.Buffered(1))
    return pl.pallas_call(
        functools.partial(_out_kernel, matrix_idx=matrix_idx),
        grid=(1,),
        in_specs=[whole(o) for o in (x, a, hm, p)]
        + [pl.BlockSpec(memory_space=pl.ANY) if j in matrix_idx else whole(w) for j, w in enumerate(weights)],
        out_specs=pl.BlockSpec(x.shape, lambda i: (0, 0)),
        out_shape=jax.ShapeDtypeStruct(x.shape, F32),
        scratch_shapes=[*(pltpu.VMEM(w.shape, w.dtype) for w in matrices), pltpu.SemaphoreType.DMA((len(matrices),))],
        compiler_params=pltpu.CompilerParams(dimension_semantics=("arbitrary",), vmem_limit_bytes=VMEM_LIMIT),
        name="out_proj",
    )(x, a, hm, p, *weights)


N_FUSED_WEIGHTS = 13
N_PROJ_SLOTS = 9


def _prompt_kernel(sink_ref, x0_ref, cos_ref, sin_ref, x2_ref, p_ref, gmix_ref, wp_ref, bp_ref, *rest,
                   tiles_per_seq, matrix_idx):
    weights = list(rest[:N_FUSED_WEIGHTS])
    y_ref, kl_ref, vl_ref, cst_ref, nst_ref, mst_ref = rest[N_FUSED_WEIGHTS:N_FUSED_WEIGHTS + 6]
    scr = rest[N_FUSED_WEIGHTS + 6:]
    proj_scr = scr[:N_PROJ_SLOTS]
    kprev_scr, vtprev_scr, a_scr, h_scr, c_scr, m_scr = scr[N_PROJ_SLOTS:N_PROJ_SLOTS + 6]
    w_scr, dma_sem = scr[N_PROJ_SLOTS + 6:-1], scr[-1]
    w_copies = [pltpu.make_async_copy(weights[j], w_scr[n], dma_sem.at[n]) for n, j in enumerate(matrix_idx)]
    for n, j in enumerate(matrix_idx):
        weights[j] = w_scr[n]
    i = pl.program_id(0)
    n_tiles = pl.num_programs(0) - 2
    t1 = jnp.clip(i - 1, 0, n_tiles - 1)
    tile_in_seq = t1 % tiles_per_seq
    s0, s1 = i % 2, (i + 1) % 2
    L = ML_CHUNK
    tm = x0_ref.shape[0]

    @pl.when(i == 0)
    def _():
        for c in w_copies:
            c.start()
        for r in (*proj_scr, kprev_scr, vtprev_scr, a_scr, h_scr, c_scr, m_scr):
            r[...] = jnp.zeros_like(r)

    @pl.when(i == 2)
    def _():
        for c in w_copies:
            c.wait()

    @pl.when(tile_in_seq == 0)
    def _():
        c_scr[...] = jnp.zeros_like(c_scr)
        m_scr[...] = jnp.zeros_like(m_scr)

    def projection():
        outs = [r.at[s0] for r in proj_scr]
        yield from _in_proj_stages(x0_ref[...], gmix_ref, wp_ref, bp_ref, cos_ref, sin_ref,
                                   outs[0], outs[1], outs[2], kl_ref, vl_ref, *outs[3:], feature_major=True)

    def branches():
        qt_ref, kc_ref, vtc_ref, mqt_ref, mk_ref, mvt_ref, og_ref, gif_ref, gt_ref = [r.at[s1] for r in proj_scr]
        slot = s1

        def emit_attn(r, blk):
            a_scr[slot, r * WINDOW:(r + 1) * WINDOW, :] = blk.astype(a_scr.dtype)

        yield from _attn_stages(sink_ref, qt_ref, kprev_scr, kc_ref, vtprev_scr, vtc_ref, tile_in_seq == 0, emit_attn)
        kprev_scr[...] = kc_ref[tm - WINDOW:, :]
        vtprev_scr[...] = vtc_ref[:, tm - WINDOW:]
        hsl = lambda h: slice(h * ML_DIM, (h + 1) * ML_DIM)
        carry = dict(state=[c_scr[h] for h in range(ML_HEADS)], m=[m_scr[h:h + 1, :] for h in range(ML_HEADS)])
        for c in range(tm // L):
            cs = slice(c * L, (c + 1) * L)

            def emit_ml(hs, new_state, new_m, cs=cs):
                for h in range(ML_HEADS):
                    h_scr[slot, cs, hsl(h)] = hs[h].astype(h_scr.dtype)
                carry["state"], carry["m"] = new_state, new_m

            chains = [dict(qt=mqt_ref[0, hsl(h), cs], k=mk_ref[cs, hsl(h)], vt=mvt_ref[0, hsl(h), cs],
                           og=og_ref[cs, hsl(h)], ig_r=gt_ref[0, h:h + 1, cs],
                           b_r=gt_ref[0, ML_HEADS + h:ML_HEADS + h + 1, cs],
                           u_c=gif_ref[cs, 2 * ML_HEADS + h:2 * ML_HEADS + h + 1],
                           state=carry["state"][h], m_prev=carry["m"][h]) for h in range(ML_HEADS)]
            yield from _mlstm_stages(chains, emit_ml)
        for h in range(ML_HEADS):
            c_scr[h] = carry["state"][h]
            m_scr[h:h + 1, :] = carry["m"][h]

    def dense():
        def emit_y(y):
            y_ref[...] = y

        return _dense_stages(x2_ref[...], a_scr[s0], h_scr[s0], p_ref[...], *weights, emit=emit_y)

    @pl.when(i < 2)
    def _():
        _weave(branches(), projection())

    @pl.when((i >= 2) & (i < n_tiles))
    def _():
        _weave(branches(), dense(), projection())

    @pl.when(i >= n_tiles)
    def _():
        _weave(branches(), dense())

    @pl.when((tile_in_seq == tiles_per_seq - 1) & (i >= 1) & (i <= n_tiles))
    def _():
        nst_ref[...] = jnp.zeros_like(nst_ref)
        for h in range(ML_HEADS):
            final = c_scr[h]
            cst_ref[0, h] = final[:ML_DIM].T
            nst_ref[0, h:h + 1, :] = final[ML_DIM:ML_DIM + 1]
        mst_ref[0] = m_scr[...]


def _prompt_fused(sinks, x, p, cos, sin, g_mix, w_proj, b_proj, weights, batch, seq, tm):
    assert len(weights) == N_FUSED_WEIGHTS
    rows = batch * seq
    n_tiles = rows // tm
    tps = seq // tm
    t0 = lambda i: jnp.minimum(i, n_tiles - 1)
    t1 = lambda i: jnp.clip(i - 1, 0, n_tiles - 1)
    t2 = lambda i: jnp.maximum(i - 2, 0)
    const = lambda i: (0, 0)
    single = lambda a: pl.BlockSpec(a.shape, const, pipeline_mode=pl.Buffered(1))
    slot = lambda shape, dt: pltpu.VMEM((2,) + shape, dt)
    matrix_idx = tuple(j for j, w in enumerate(weights) if w.dtype == BF16)
    matrices = [weights[j] for j in matrix_idx]
    return pl.pallas_call(
        functools.partial(_prompt_kernel, tiles_per_seq=tps, matrix_idx=matrix_idx),
        grid=(n_tiles + 2,),
        in_specs=[
            pl.BlockSpec(memory_space=pltpu.SMEM),
            pl.BlockSpec((tm, D_MODEL), lambda i: (t0(i), 0)),
            pl.BlockSpec((tm, LANES), lambda i: (t0(i) % tps, 0)),
            pl.BlockSpec((tm, LANES), lambda i: (t0(i) % tps, 0)),
            pl.BlockSpec((tm, D_MODEL), lambda i: (t2(i), 0)),
            pl.BlockSpec((tm, D_PLE), lambda i: (t2(i), 0)),
            single(g_mix), single(w_proj), single(b_proj),
        ] + [pl.BlockSpec(memory_space=pl.ANY) if j in matrix_idx else single(w) for j, w in enumerate(weights)],
        out_specs=(
            pl.BlockSpec((tm, D_MODEL), lambda i: (t2(i), 0)),
            pl.BlockSpec((1, ATTN_KV, WINDOW), lambda i: (t0(i) // tps, 0, 0)),
            pl.BlockSpec((1, ATTN_KV, WINDOW), lambda i: (t0(i) // tps, 0, 0)),
            pl.BlockSpec((1, ML_HEADS, ML_DIM, ML_DIM), lambda i: (t1(i) // tps, 0, 0, 0)),
            pl.BlockSpec((1, 8, LANES), lambda i: (t1(i) // tps, 0, 0)),
            pl.BlockSpec((1, 8, LANES), lambda i: (t1(i) // tps, 0, 0)),
        ),
        out_shape=(
            jax.ShapeDtypeStruct((rows, D_MODEL), F32),
            jax.ShapeDtypeStruct((batch, ATTN_KV, WINDOW), F32),
            jax.ShapeDtypeStruct((batch, ATTN_KV, WINDOW), F32),
            jax.ShapeDtypeStruct((batch, ML_HEADS, ML_DIM, ML_DIM), F32),
            jax.ShapeDtypeStruct((batch, 8, LANES), F32),
            jax.ShapeDtypeStruct((batch, 8, LANES), F32),
        ),
        scratch_shapes=[
            slot((ATTN_Q, tm), BF16), slot((tm, ATTN_KV), BF16), slot((ATTN_KV, tm), BF16),
            slot((1, ML_W, tm), BF16), slot((tm, ML_W), BF16), slot((1, ML_W, tm), BF16),
            slot((tm, ML_W), F32), slot((tm, LANES), F32), slot((1, 8, tm), F32),
            pltpu.VMEM((WINDOW, ATTN_KV), BF16), pltpu.VMEM((ATTN_KV, WINDOW), BF16),
            slot((tm, ATTN_Q), BF16), slot((tm, ML_W), BF16),
            pltpu.VMEM((ML_HEADS, ML_STATE_ROWS, ML_DIM), F32), pltpu.VMEM((8, LANES), F32),
            *(pltpu.VMEM(w.shape, w.dtype) for w in matrices), pltpu.SemaphoreType.DMA((len(matrices),)),
        ],
        compiler_params=pltpu.CompilerParams(dimension_semantics=("arbitrary",), vmem_limit_bytes=VMEM_LIMIT),
        name="prompt_fused",
    )(sinks, x, cos, sin, x, p, g_mix, w_proj, b_proj, *weights)


def _attn_decode_kernel(sink_ref, qm_ref, kn_ref, vn_ref, ck_ref, cv_ref, o_ref, ko_ref, vo_ref):
    bt = qm_ref.shape[0]
    qm = qm_ref[...]
    kn = kn_ref[...]
    vn = vn_ref[...]
    ck = ck_ref[...]
    cv = cv_ref[...]
    head = lax.broadcasted_iota(jnp.int32, (N_Q_HEADS, 1), 0)
    sink = jnp.zeros((N_Q_HEADS, 1), F32)
    for hd in range(N_Q_HEADS):
        sink = jnp.where(head == hd, sink_ref[hd], sink)
    s = jnp.einsum('bhc,bck->bhk', qm, ck.astype(BF16), preferred_element_type=F32)
    s_new = jnp.sum(qm.astype(F32) * kn[:, None, :], axis=2, keepdims=True)
    m = jnp.maximum(jnp.maximum(jnp.max(s, axis=2, keepdims=True), s_new), sink)
    p = jnp.exp(s - m)
    p_new = jnp.exp(s_new - m)
    den = jnp.sum(p, axis=2, keepdims=True) + p_new + jnp.exp(sink - m)
    o = jnp.einsum('bhk,bck->bhc', p.astype(BF16), cv.astype(BF16), preferred_element_type=F32)
    o_ref[...] = (o + p_new * vn[:, None, :]) / den
    pad = jnp.zeros((LANES - bt, ATTN_KV), F32)
    kn_t = jnp.concatenate([kn, pad], axis=0).T
    vn_t = jnp.concatenate([vn, pad], axis=0).T
    newest = lax.broadcasted_iota(jnp.int32, (ATTN_KV, WINDOW), 1) == WINDOW - 1
    for b in range(bt):
        ko_ref[b] = jnp.where(newest, kn_t[:, b:b + 1], pltpu.roll(ck[b], WINDOW - 1, 1))
        vo_ref[b] = jnp.where(newest, vn_t[:, b:b + 1], pltpu.roll(cv[b], WINDOW - 1, 1))


def _attn_decode(sinks, qmat, k_new, v_new, cache_k, cache_v, bt):
    nbatch = qmat.shape[0]
    b3 = lambda i: (i, 0, 0)
    b2 = lambda i: (i, 0)
    return pl.pallas_call(
        _attn_decode_kernel,
        grid=(nbatch // bt,),
        in_specs=[
            pl.BlockSpec(memory_space=pltpu.SMEM),
            pl.BlockSpec((bt, N_Q_HEADS, ATTN_KV), b3),
            pl.BlockSpec((bt, ATTN_KV), b2),
            pl.BlockSpec((bt, ATTN_KV), b2),
            pl.BlockSpec((bt, WINDOW, ATTN_KV), b3),
            pl.BlockSpec((bt, WINDOW, ATTN_KV), b3),
        ],
        out_specs=(
            pl.BlockSpec((bt, N_Q_HEADS, ATTN_KV), b3),
            pl.BlockSpec((bt, WINDOW, ATTN_KV), b3),
            pl.BlockSpec((bt, WINDOW, ATTN_KV), b3),
        ),
        out_shape=(
            jax.ShapeDtypeStruct((nbatch, N_Q_HEADS, ATTN_KV), F32),
            jax.ShapeDtypeStruct((nbatch, WINDOW, ATTN_KV), F32),
            jax.ShapeDtypeStruct((nbatch, WINDOW, ATTN_KV), F32),
        ),
        compiler_params=pltpu.CompilerParams(dimension_semantics=("parallel",)),
        name="attn_decode",
    )(sinks, qmat, k_new, v_new, cache_k, cache_v)


def _mlstm_decode_kernel(q_ref, k_ref, v_ref, og_ref, gif_ref, c_ref, n_ref, m_ref, h_ref, co_ref, no_ref, mo_ref):
    bt = q_ref.shape[0]
    gif = gif_ref[...]
    ig = gif[:, 0:ML_HEADS]
    lf = gif[:, ML_HEADS:2 * ML_HEADS]
    m_prev = m_ref[...]
    a = lf + m_prev
    m_t = jnp.maximum(a, ig)
    aw = jnp.exp(a - m_t)
    e_i = jnp.exp(ig - m_t)
    floor = jnp.exp(-m_t)
    m_new = jnp.maximum(a, ig)
    decay = jnp.exp(a - m_new)
    w = jnp.exp(ig - m_new)
    mo_ref[...] = m_new
    pad = jnp.zeros((LANES - bt, ML_DIM), F32)
    lane = lax.broadcasted_iota(jnp.int32, (ML_DIM, LANES), 1)
    for h in range(ML_HEADS):
        hs = slice(h * ML_DIM, (h + 1) * ML_DIM)
        q = q_ref[:, hs]
        k = k_ref[:, hs]
        v = v_ref[:, hs]
        n = n_ref[:, h, :]
        qk = jnp.sum(q * k, axis=1, keepdims=True)
        qn = jnp.sum(q * n, axis=1, keepdims=True)
        sw = qk * e_i[:, h:h + 1]
        den = aw[:, h:h + 1] * qn + sw
        inv = 1.0 / jnp.maximum(jnp.abs(den), floor[:, h:h + 1])
        no_ref[:, h, :] = decay[:, h:h + 1] * n + w[:, h:h + 1] * k
        q_b = q.astype(BF16)
        wk_t = jnp.concatenate([w[:, h:h + 1] * k, pad], axis=0).T.astype(BF16)
        v_rows = jnp.concatenate([v, pad], axis=0).astype(BF16)
        nums = []
        for b in range(bt):
            C = c_ref[b, h]
            q_c = _mm(q_b, C.astype(BF16))[b:b + 1, :]
            nums.append(aw[b:b + 1, h:h + 1] * q_c)
            outer = _mm(jnp.where(lane == b, wk_t, jnp.zeros_like(wk_t)), v_rows)
            co_ref[b, h] = decay[b:b + 1, h:h + 1] * C + outer
        num = jnp.concatenate(nums, axis=0) + sw * v
        h_ref[:, hs] = (num * inv * og_ref[:, hs]).astype(h_ref.dtype)


def _mlstm_decode(mq, mk, mv, og, gif, state_c, state_n, state_m, bt):
    nbatch = mq.shape[0]
    b2 = lambda i: (i, 0)
    b3 = lambda i: (i, 0, 0)
    b4 = lambda i: (i, 0, 0, 0)
    return pl.pallas_call(
        _mlstm_decode_kernel,
        grid=(nbatch // bt,),
        in_specs=[
            pl.BlockSpec((bt, ML_W), b2),
            pl.BlockSpec((bt, ML_W), b2),
            pl.BlockSpec((bt, ML_W), b2),
            pl.BlockSpec((bt, ML_W), b2),
            pl.BlockSpec((bt, LANES), b2),
            pl.BlockSpec((bt, ML_HEADS, ML_DIM, ML_DIM), b4),
            pl.BlockSpec((bt, ML_HEADS, ML_DIM), b3),
            pl.BlockSpec((bt, ML_HEADS), b2),
        ],
        out_specs=(
            pl.BlockSpec((bt, ML_W), b2),
            pl.BlockSpec((bt, ML_HEADS, ML_DIM, ML_DIM), b4),
            pl.BlockSpec((bt, ML_HEADS, ML_DIM), b3),
            pl.BlockSpec((bt, ML_HEADS), b2),
        ),
        out_shape=(
            jax.ShapeDtypeStruct((nbatch, ML_W), BF16),
            jax.ShapeDtypeStruct((nbatch, ML_HEADS, ML_DIM, ML_DIM), F32),
            jax.ShapeDtypeStruct((nbatch, ML_HEADS, ML_DIM), F32),
            jax.ShapeDtypeStruct((nbatch, ML_HEADS), F32),
        ),
        compiler_params=pltpu.CompilerParams(dimension_semantics=("parallel",)),
        name="mlstm_decode",
    )(mq, mk, mv, og, gif, state_c, state_n, state_m)


def _rope_tables(pos):
    half = HEAD_DIM // 2
    inv = (1.0 / (ROPE_THETA ** (np.arange(half, dtype=np.float32) / half))).astype(np.float32)
    ang = pos.astype(np.float32)[:, None] * inv[None, :]
    cos = np.cos(ang)
    sin = np.sin(ang)
    cos_t = np.concatenate([cos, cos, cos, cos], axis=1)
    sin_t = np.concatenate([-sin, sin, -sin, sin], axis=1)
    return jnp.asarray(cos_t, F32), jnp.asarray(sin_t, F32)


def kernel(x_prompt, x_sample, cache_k, cache_v, state_C, state_n, state_m, p_prompt, p_sample, norm_mix, w_in, b_in, attn_sinks, w_attn_up, w_ml_up, w_o, norm_mlp, w_ff1, w_ff2, norm_ple, w_ple_gate, w_ple_proj, norm_final):
    assert w_in.shape[0] == 1, "single layer"
    batch, seq, _ = x_prompt.shape
    nbatch = x_sample.shape[0]
    assert x_sample.shape[1] == 1

    w_proj, w_gate, w_au, w_mu, w_out, w_f1, w_f2, w_pg, w_pp = _weight_cast(
        w_in[0].T, (w_attn_up[0], w_ml_up[0], w_o[0], w_ff1[0], w_ff2[0], w_ple_gate[0], w_ple_proj[0]))
    b_proj = b_in[:, :N_PROJ]
    b_gate = b_in[:, OFF_GA:]
    g_mix = norm_mix[0][None, :]
    out_weights = (
        g_mix, w_gate, b_gate, w_au, w_mu, w_out,
        norm_mlp[0][None, :], w_f1, w_f2,
        norm_ple[0][None, :], w_pg, w_pp, norm_final[None, :],
    )
    sinks = attn_sinks[0]
    to_fp = lambda c: c[0].transpose(0, 2, 3, 1).reshape(c.shape[1], ATTN_KV, WINDOW)
    from_fp = lambda c: c.reshape(c.shape[0], N_KV_HEADS, HEAD_DIM, WINDOW).transpose(0, 3, 1, 2)[None]

    cos_p, sin_p = _rope_tables(np.arange(seq, dtype=np.int32))
    xp = x_prompt.reshape(batch * seq, D_MODEL)
    y_prompt, k_last, v_last, cst, nst, mst = _prompt_fused(
        sinks, xp, p_prompt[0].reshape(batch * seq, D_PLE), cos_p, sin_p, g_mix, w_proj, b_proj, out_weights,
        batch, seq, tm=PROMPT_TILE)
    y_prompt = y_prompt.reshape(batch, seq, D_MODEL)
    k_prompt = from_fp(k_last)
    v_prompt = from_fp(v_last)
    c_prompt = cst[None]
    n_prompt = nst[:, :ML_HEADS, :][None]
    m_prompt = mst[:, :ML_HEADS, 0][None]

    cos_s, sin_s = _rope_tables(np.full((nbatch,), PAST_LEN, dtype=np.int32))
    xs = x_sample.reshape(nbatch, D_MODEL)
    qs, _, _, ks, vs, mqs, mks, mvs, ogs, gifs, _ = _in_proj_sample(xs, g_mix, w_proj, b_proj, cos_s, sin_s)
    qmat = qs.reshape(nbatch, N_Q_HEADS, ATTN_KV)
    o_full, k_s, v_s = _attn_decode(sinks, qmat, ks, vs, to_fp(cache_k), to_fp(cache_v), bt=DECODE_ROWS)
    o5 = o_full.reshape(nbatch, N_KV_HEADS, GQA_GROUP, N_KV_HEADS, HEAD_DIM)
    a_s = jnp.stack([o5[:, kv, :, kv, :] for kv in range(N_KV_HEADS)], axis=1).reshape(nbatch, ATTN_Q).astype(BF16)
    h_s, c_s, n_s, m_s = _mlstm_decode(mqs, mks, mvs, ogs, gifs, state_C[0], state_n[0], state_m[0], bt=DECODE_ROWS)
    y_sample = _out_proj_sample(xs, a_s, h_s, p_sample[0].reshape(nbatch, D_PLE), out_weights)
    y_sample = y_sample.reshape(nbatch, 1, D_MODEL)
    k_sample = from_fp(k_s)
    v_sample = from_fp(v_s)

    return (y_prompt, y_sample, k_prompt, v_prompt, c_prompt, n_prompt, m_prompt,
            k_sample, v_sample, c_s[None], n_s[None], m_s[None])
```

```python
import functools

import numpy as np
import jax
import jax.numpy as jnp
from jax import lax
from jax.experimental import pallas as pl
from jax.experimental.pallas import tpu as pltpu

F32 = jnp.float32
BF16 = jnp.bfloat16

D_MODEL = 1024
HEAD_DIM = 64
N_Q_HEADS = 8
N_KV_HEADS = 2
GQA_GROUP = N_Q_HEADS // N_KV_HEADS
WINDOW = 128
ROPE_THETA = 10000.0
ML_HEADS = 4
ML_DIM = 128
ML_CHUNK = 128
D_FF = 4 * D_MODEL
D_PLE = 256
EPS = 1e-6
PAST_LEN = 16384

ATTN_Q = N_Q_HEADS * HEAD_DIM
ATTN_KV = N_KV_HEADS * HEAD_DIM
ML_W = ML_HEADS * ML_DIM
LANES = 128

OFF_AQ = 0
OFF_AK = OFF_AQ + ATTN_Q
OFF_AV = OFF_AK + ATTN_KV
OFF_MQ = OFF_AV + ATTN_KV
OFF_MK = OFF_MQ + ML_W
OFF_MV = OFF_MK + ML_W
OFF_MO = OFF_MV + ML_W
OFF_MI = OFF_MO + ML_W
OFF_MF = OFF_MI + ML_HEADS
OFF_GA = OFF_MF + ML_HEADS
OFF_GM = OFF_GA + D_MODEL
D_IN = OFF_GM + D_MODEL
N_PROJ = OFF_MI + LANES

VMEM_LIMIT = 60 * 1024 * 1024
PROMPT_TILE = 256
DECODE_ROWS = 16

def _rms(x, g):
    r = lax.rsqrt(jnp.mean(x * x, axis=-1, keepdims=True) + EPS)
    return (x * r) * g


def _mm(a, b):
    return jnp.dot(a, b, preferred_element_type=F32)


def _log_sigmoid(x):
    return jnp.minimum(x, 0.0) - jnp.log1p(jnp.exp(-jnp.abs(x)))


def _weave(*streams):
    live = list(streams)
    while live:
        for g in list(live):
            try:
                next(g)
            except StopIteration:
                live.remove(g)


class _AwaitedRef:
    def __init__(self, ref, copy):
        self._ref, self._copy = ref, copy

    def __getitem__(self, idx):
        if self._copy is not None:
            self._copy.wait()
            self._copy = None
        return self._ref[idx]


WEIGHT_ROW_SPLIT = 8


def _weight_cast_kernel(*refs):
    n_plain = (len(refs) - 3) // 2
    w_in_t_ref, plain_in = refs[0], refs[1:1 + n_plain]
    w_proj_ref, w_gate_ref, plain_out = refs[1 + n_plain], refs[2 + n_plain], refs[3 + n_plain:]
    for g in range(N_PROJ // LANES):
        w_proj_ref[:, g * LANES:(g + 1) * LANES] = w_in_t_ref[g * LANES:(g + 1) * LANES, :].T.astype(BF16)
    for g in range(2 * D_MODEL // LANES):
        w_gate_ref[:, g * LANES:(g + 1) * LANES] = w_in_t_ref[OFF_GA + g * LANES:OFF_GA + (g + 1) * LANES, :].T.astype(BF16)
    for src, dst in zip(plain_in, plain_out):
        dst[...] = src[...].astype(BF16)


def _weight_cast(w_in_t, plain):
    split = WEIGHT_ROW_SPLIT
    assert D_MODEL // split == LANES
    rows_blk = lambda w: pl.BlockSpec((w.shape[0] // split, w.shape[1]), lambda i: (i, 0))
    out_shapes = (jax.ShapeDtypeStruct((D_MODEL, N_PROJ), BF16), jax.ShapeDtypeStruct((D_MODEL, 2 * D_MODEL), BF16),
                  *(jax.ShapeDtypeStruct(w.shape, BF16) for w in plain))
    return pl.pallas_call(
        _weight_cast_kernel,
        grid=(split,),
        in_specs=[pl.BlockSpec((D_IN, LANES), lambda i: (0, i))] + [rows_blk(w) for w in plain],
        out_specs=tuple(rows_blk(s) for s in out_shapes),
        out_shape=out_shapes,
        compiler_params=pltpu.CompilerParams(dimension_semantics=("parallel",), vmem_limit_bytes=VMEM_LIMIT),
        name="weight_cast",
    )(w_in_t, *plain)


def _in_proj_stages(x, g_ref, w_ref, b_ref, cos_ref, sin_ref,
                    q_ref, k_ref, v_ref, kl_ref, vl_ref, mq_ref, mk_ref, mv_ref, og_ref, gif_ref, gt_ref,
                    *, feature_major):
    xn = _rms(x, g_ref[...]).astype(BF16)
    tm = x.shape[0]

    def proj(lo, hi):
        return _mm(xn, w_ref[:, lo:hi]) + b_ref[:, lo:hi]

    cos = cos_ref[...]
    sin = sin_ref[...]
    lane = lax.broadcasted_iota(jnp.int32, (1, LANES), 1)
    first_half = (lane % HEAD_DIM) < (HEAD_DIM // 2)

    def rope(z):
        partner = jnp.where(first_half, pltpu.roll(z, LANES - HEAD_DIM // 2, 1), pltpu.roll(z, HEAD_DIM // 2, 1))
        return z * cos + partner * sin

    zq = proj(OFF_AQ, OFF_AK)
    for c in range(ATTN_Q // LANES):
        sl = slice(c * LANES, (c + 1) * LANES)
        qc = rope(zq[:, sl]) * (HEAD_DIM ** -0.5)
        if feature_major:
            q_ref[sl, :] = qc.T.astype(q_ref.dtype)
        else:
            kv = (2 * c) // GQA_GROUP
            own = (lane // HEAD_DIM) == kv
            swapped = pltpu.roll(qc, HEAD_DIM, 1)
            for j, src in enumerate((qc, swapped) if kv == 0 else (swapped, qc)):
                hd = 2 * c + j
                q_ref[:, hd * LANES:(hd + 1) * LANES] = jnp.where(own, src, 0.0).astype(q_ref.dtype)
    yield
    zkv = proj(OFF_AK, OFF_MQ)
    k = rope(zkv[:, :ATTN_KV])
    v = zkv[:, ATTN_KV:]
    k_ref[...] = k.astype(k_ref.dtype)
    v_ref[...] = (v.T if feature_major else v).astype(v_ref.dtype)
    if feature_major:
        kl_ref[0] = k[tm - WINDOW:, :].T
        vl_ref[0] = v[tm - WINDOW:, :].T
    else:
        kl_ref[...] = k[tm - WINDOW:, :]
        vl_ref[...] = v[tm - WINDOW:, :]
    yield
    for z_off, dst in ((OFF_MQ, mq_ref), (OFF_MV, mv_ref)):
        z = proj(z_off, z_off + ML_W)
        if feature_major:
            for h in range(ML_HEADS):
                hs = slice(h * ML_DIM, (h + 1) * ML_DIM)
                dst[0, hs, :] = z[:, hs].T.astype(dst.dtype)
        else:
            dst[...] = z.astype(dst.dtype)
        yield
    mk_ref[...] = (proj(OFF_MK, OFF_MV) * (ML_DIM ** -0.5)).astype(mk_ref.dtype)
    yield
    og_ref[...] = jax.nn.sigmoid(proj(OFF_MO, OFF_MI))
    yield
    zg = proj(OFF_MI, N_PROJ)
    gif = jnp.where(lane < ML_HEADS, zg, jnp.where(lane < 2 * ML_HEADS, _log_sigmoid(zg), 0.0))
    if feature_major:
        L = ML_CHUNK
        upper = (lax.broadcasted_iota(jnp.int32, (L, L), 0) <= lax.broadcasted_iota(jnp.int32, (L, L), 1)).astype(F32)
        row8 = lax.broadcasted_iota(jnp.int32, (8, 1), 0)
        fill = jnp.zeros((L - 16, L), F32)
        for c in range(tm // L):
            cs = slice(c * L, (c + 1) * L)
            g_rows = gif[cs, :].T[0:8, :]
            cum = jnp.dot(g_rows, upper, precision=lax.Precision.HIGHEST, preferred_element_type=F32)
            rows = jnp.where(row8 < ML_HEADS, g_rows, cum)
            gt_ref[0, :, cs] = rows
            diff = rows - pltpu.roll(rows, ML_HEADS, 0)
            gif_ref[cs, :] = jnp.concatenate([rows, diff, fill], axis=0).T
    else:
        gif_ref[...] = gif
        gt_ref[0] = jnp.zeros(gt_ref.shape[1:], F32)
    yield


def _in_proj_kernel(x_ref, *refs, feature_major):
    _weave(_in_proj_stages(x_ref[...], *refs, feature_major=feature_major))


def _in_proj_sample(x, g, w, b, cos, sin):
    rows = x.shape[0]
    assert rows == WINDOW
    full = lambda n, dt: (jax.ShapeDtypeStruct((rows, n), dt), pl.BlockSpec((rows, n), lambda i: (0, 0)))
    outs = (
        full(N_Q_HEADS * LANES, BF16), full(ATTN_KV, BF16), full(ATTN_KV, BF16), full(ATTN_KV, F32), full(ATTN_KV, F32),
        full(ML_W, F32), full(ML_W, F32), full(ML_W, F32), full(ML_W, F32), full(LANES, F32),
        (jax.ShapeDtypeStruct((1, 8, rows), F32), pl.BlockSpec((1, 8, rows), lambda i: (0, 0, 0))),
    )
    return pl.pallas_call(
        functools.partial(_in_proj_kernel, feature_major=False),
        grid=(1,),
        in_specs=[pl.BlockSpec(a.shape, lambda i: (0, 0)) for a in (x, g, w, b, cos, sin)],
        out_specs=tuple(o[1] for o in outs),
        out_shape=tuple(o[0] for o in outs),
        compiler_params=pltpu.CompilerParams(dimension_semantics=("arbitrary",), vmem_limit_bytes=VMEM_LIMIT),
        name="in_proj",
    )(x, g, w, b, cos, sin)


def _attn_stages(sink_ref, qt_ref, kp_ref, kc_ref, vtp_ref, vtc_ref, first_block, emit):
    L = WINDOW
    nsub = kc_ref.shape[0] // L
    key = lax.broadcasted_iota(jnp.int32, (2 * L, L), 0)
    t = lax.broadcasted_iota(jnp.int32, (2 * L, L), 1)
    is_prev = key < L
    allowed = (is_prev & (key >= t)) | (~is_prev & (key - L <= t))
    bias = jnp.where(allowed, 0.0, -jnp.inf)
    bias_first = jnp.where(is_prev & first_block, -jnp.inf, bias)
    zeros_q = jnp.zeros((HEAD_DIM, L), BF16)
    heads = range(N_Q_HEADS)
    for r in range(nsub):
        ts = slice(r * L, (r + 1) * L)
        k_prev = kp_ref[...] if r == 0 else kc_ref[(r - 1) * L:r * L, :]
        vt_prev = vtp_ref[...] if r == 0 else vtc_ref[:, (r - 1) * L:r * L]
        k2 = jnp.concatenate([k_prev, kc_ref[ts, :]], axis=0)
        vt2 = jnp.concatenate([vt_prev, vtc_ref[:, ts]], axis=1)
        b_r = bias_first if r == 0 else bias
        q_op = [jnp.concatenate([qt_ref[hd * HEAD_DIM:(hd + 1) * HEAD_DIM, ts], zeros_q][::1 if hd < GQA_GROUP else -1],
                                axis=0) for hd in heads]
        s = [_mm(k2, q_op[hd]) + b_r for hd in heads]
        yield
        m = [jnp.maximum(jnp.max(s[hd], axis=0, keepdims=True), sink_ref[hd]) for hd in heads]
        yield
        p = [jnp.exp(s[hd] - m[hd]) for hd in heads]
        yield
        den = [jnp.sum(p[hd], axis=0, keepdims=True) + jnp.exp(sink_ref[hd] - m[hd]) for hd in heads]
        o = [_mm(vt2[(hd // GQA_GROUP) * HEAD_DIM:(hd // GQA_GROUP + 1) * HEAD_DIM, :], p[hd].astype(BF16)) / den[hd]
             for hd in heads]
        yield
        emit(r, jnp.concatenate(o, axis=0).T)
        yield


ML_STATE_ROWS = ML_DIM + 8


def _mlstm_stages(chains, emit):
    L = ML_CHUNK
    s_idx = lax.broadcasted_iota(jnp.int32, (L, L), 0)
    t_idx = lax.broadcasted_iota(jnp.int32, (L, L), 1)
    causal = s_idx <= t_idx
    n = range(len(chains))
    dlog = [jnp.where(causal, ch["b_r"] + ch["u_c"], -jnp.inf) for ch in chains]
    yield
    qk = [_mm(ch["k"], ch["qt"]) for ch in chains]
    inter = [_mm(ch["state"].astype(BF16), ch["qt"]) for ch in chains]
    yield
    a = [ch["b_r"] + ch["m_prev"] for ch in chains]
    m_t = [jnp.maximum(a[i], jnp.max(dlog[i], axis=0, keepdims=True)) for i in n]
    yield
    sw = [qk[i] * jnp.exp(dlog[i] - m_t[i]) for i in n]
    yield
    sv = [_mm(chains[i]["vt"], sw[i].astype(BF16)) for i in n]
    yield
    hs = []
    for i in n:
        aw = jnp.exp(a[i] - m_t[i])
        num = aw * inter[i][:ML_DIM] + sv[i]
        den = aw * inter[i][ML_DIM:ML_DIM + 1] + jnp.sum(sw[i], axis=0, keepdims=True)
        hh = num / jnp.maximum(jnp.abs(den), jnp.exp(-m_t[i]))
        hs.append(hh.T * chains[i]["og"])
    yield
    pad = jnp.zeros((ML_STATE_ROWS - ML_DIM - 1, L), F32)
    new_state, new_m = [], []
    for i in n:
        ch = chains[i]
        b_end = ch["b_r"][:, L - 1:L]
        g = b_end - ch["b_r"] + ch["ig_r"]
        m_new = jnp.maximum(b_end + ch["m_prev"], jnp.max(g, axis=1, keepdims=True))
        decay = jnp.exp(b_end + ch["m_prev"] - m_new)
        w = jnp.exp(g - m_new)
        v_aug = jnp.concatenate([ch["vt"].astype(F32) * w, w, pad], axis=0).astype(BF16)
        new_state.append(decay * ch["state"] + _mm(v_aug, ch["k"]))
        new_m.append(m_new)
    emit(hs, new_state, new_m)
    yield


DENSE_COLS = 512


def _dense_stages(x, a, hm, p, gmix_ref, wg_ref, bg_ref, wau_ref, wmu_ref, wo_ref,
                  gmlp_ref, wff1_ref, wff2_ref, gple_ref, wpg_ref, wpp_ref, gfin_ref, emit):
    nc = DENSE_COLS
    pieces = lambda n: [slice(c, c + nc) for c in range(0, n, nc)]
    xn = _rms(x, gmix_ref[...]).astype(BF16)
    gates = []
    for cs in pieces(2 * D_MODEL):
        gates.append(jax.nn.sigmoid(_mm(xn, wg_ref[:, cs]) + bg_ref[:, cs]))
        yield
    half = D_MODEL // nc
    mix = []
    for j, cs in enumerate(pieces(D_MODEL)):
        mix.append((gates[j] * _mm(a, wau_ref[:, cs]) + gates[half + j] * _mm(hm, wmu_ref[:, cs])).astype(BF16))
        yield
    mix = jnp.concatenate(mix, axis=1)
    h = []
    for cs in pieces(D_MODEL):
        h.append(x[:, cs] + _mm(mix, wo_ref[:, cs]))
        yield
    h = jnp.concatenate(h, axis=1)
    hn = _rms(h, gmlp_ref[...]).astype(BF16)
    u = []
    for cs in pieces(D_FF):
        u.append(jnp.square(jnp.maximum(_mm(hn, wff1_ref[:, cs]), 0.0)).astype(BF16))
        yield
    u = jnp.concatenate(u, axis=1)
    h2 = []
    for cs in pieces(D_MODEL):
        h2.append(h[:, cs] + _mm(u, wff2_ref[:, cs]))
        yield
    h = jnp.concatenate(h2, axis=1)
    hn = _rms(h, gple_ref[...]).astype(BF16)
    pb = p.astype(BF16)
    h3 = []
    for cs in pieces(D_MODEL):
        h3.append(h[:, cs] + _mm(pb, wpp_ref[:, cs]) * jax.nn.sigmoid(_mm(hn, wpg_ref[:, cs])))
        yield
    emit(_rms(jnp.concatenate(h3, axis=1), gfin_ref[...]))
    yield


def _out_kernel(x_ref, a_ref, hm_ref, p_ref, *rest, matrix_idx):
    weights, y_ref = list(rest[:N_FUSED_WEIGHTS]), rest[N_FUSED_WEIGHTS]
    w_scr, dma_sem = rest[N_FUSED_WEIGHTS + 1:-1], rest[-1]
    copies = [pltpu.make_async_copy(weights[j], w_scr[n], dma_sem.at[n]) for n, j in enumerate(matrix_idx)]
    for c in copies:
        c.start()
    for n, j in enumerate(matrix_idx):
        weights[j] = _AwaitedRef(w_scr[n], copies[n])

    def emit(y):
        y_ref[...] = y

    _weave(_dense_stages(x_ref[...], a_ref[...], hm_ref[...], p_ref[...], *weights, emit=emit))


def _out_proj_sample(x, a, hm, p, weights):
    assert len(weights) == N_FUSED_WEIGHTS
    matrix_idx = tuple(j for j, w in enumerate(weights) if w.dtype == BF16)
    matrices = [weights[j] for j in matrix_idx]
    whole = lambda o: pl.BlockSpec(o.shape, lambda i: (0, 0), pipeline_mode=pl.Buffered(1))
    return pl.pallas_call(
        functools.partial(_out_kernel, matrix_idx=matrix_idx),
        grid=(1,),
        in_specs=[whole(o) for o in (x, a, hm, p)]
        + [pl.BlockSpec(memory_space=pl.ANY) if j in matrix_idx else whole(w) for j, w in enumerate(weights)],
        out_specs=pl.BlockSpec(x.shape, lambda i: (0, 0)),
        out_shape=jax.ShapeDtypeStruct(x.shape, F32),
        scratch_shapes=[*(pltpu.VMEM(w.shape, w.dtype) for w in matrices), pltpu.SemaphoreType.DMA((len(matrices),))],
        compiler_params=pltpu.CompilerParams(dimension_semantics=("arbitrary",), vmem_limit_bytes=VMEM_LIMIT),
        name="out_proj",
    )(x, a, hm, p, *weights)


N_FUSED_WEIGHTS = 13
N_PROJ_SLOTS = 9


def _prompt_kernel(sink_ref, x0_ref, cos_ref, sin_ref, x2_ref, p_ref, gmix_ref, wp_ref, bp_ref, *rest,
                   tiles_per_seq, matrix_idx):
    weights = list(rest[:N_FUSED_WEIGHTS])
    y_ref, kl_ref, vl_ref, cst_ref, nst_ref, mst_ref = rest[N_FUSED_WEIGHTS:N_FUSED_WEIGHTS + 6]
    scr = rest[N_FUSED_WEIGHTS + 6:]
    proj_scr = scr[:N_PROJ_SLOTS]
    kprev_scr, vtprev_scr, a_scr, h_scr, c_scr, m_scr = scr[N_PROJ_SLOTS:N_PROJ_SLOTS + 6]
    w_scr, dma_sem = scr[N_PROJ_SLOTS + 6:-1], scr[-1]
    w_copies = [pltpu.make_async_copy(weights[j], w_scr[n], dma_sem.at[n]) for n, j in enumerate(matrix_idx)]
    for n, j in enumerate(matrix_idx):
        weights[j] = w_scr[n]
    i = pl.program_id(0)
    n_tiles = pl.num_programs(0) - 2
    t1 = jnp.clip(i - 1, 0, n_tiles - 1)
    tile_in_seq = t1 % tiles_per_seq
    s0, s1 = i % 2, (i + 1) % 2
    L = ML_CHUNK
    tm = x0_ref.shape[0]

    @pl.when(i == 0)
    def _():
        for c in w_copies:
            c.start()
        for r in (*proj_scr, kprev_scr, vtprev_scr, a_scr, h_scr, c_scr, m_scr):
            r[...] = jnp.zeros_like(r)

    @pl.when(i == 2)
    def _():
        for c in w_copies:
            c.wait()

    @pl.when(tile_in_seq == 0)
    def _():
        c_scr[...] = jnp.zeros_like(c_scr)
        m_scr[...] = jnp.zeros_like(m_scr)

    def projection():
        outs = [r.at[s0] for r in proj_scr]
        yield from _in_proj_stages(x0_ref[...], gmix_ref, wp_ref, bp_ref, cos_ref, sin_ref,
                                   outs[0], outs[1], outs[2], kl_ref, vl_ref, *outs[3:], feature_major=True)

    def branches():
        qt_ref, kc_ref, vtc_ref, mqt_ref, mk_ref, mvt_ref, og_ref, gif_ref, gt_ref = [r.at[s1] for r in proj_scr]
        slot = s1

        def emit_attn(r, blk):
            a_scr[slot, r * WINDOW:(r + 1) * WINDOW, :] = blk.astype(a_scr.dtype)

        yield from _attn_stages(sink_ref, qt_ref, kprev_scr, kc_ref, vtprev_scr, vtc_ref, tile_in_seq == 0, emit_attn)
        kprev_scr[...] = kc_ref[tm - WINDOW:, :]
        vtprev_scr[...] = vtc_ref[:, tm - WINDOW:]
        hsl = lambda h: slice(h * ML_DIM, (h + 1) * ML_DIM)
        carry = dict(state=[c_scr[h] for h in range(ML_HEADS)], m=[m_scr[h:h + 1, :] for h in range(ML_HEADS)])
        for c in range(tm // L):
            cs = slice(c * L, (c + 1) * L)

            def emit_ml(hs, new_state, new_m, cs=cs):
                for h in range(ML_HEADS):
                    h_scr[slot, cs, hsl(h)] = hs[h].astype(h_scr.dtype)
                carry["state"], carry["m"] = new_state, new_m

            chains = [dict(qt=mqt_ref[0, hsl(h), cs], k=mk_ref[cs, hsl(h)], vt=mvt_ref[0, hsl(h), cs],
                           og=og_ref[cs, hsl(h)], ig_r=gt_ref[0, h:h + 1, cs],
                           b_r=gt_ref[0, ML_HEADS + h:ML_HEADS + h + 1, cs],
                           u_c=gif_ref[cs, 2 * ML_HEADS + h:2 * ML_HEADS + h + 1],
                           state=carry["state"][h], m_prev=carry["m"][h]) for h in range(ML_HEADS)]
            yield from _mlstm_stages(chains, emit_ml)
        for h in range(ML_HEADS):
            c_scr[h] = carry["state"][h]
            m_scr[h:h + 1, :] = carry["m"][h]

    def dense():
        def emit_y(y):
            y_ref[...] = y

        return _dense_stages(x2_ref[...], a_scr[s0], h_scr[s0], p_ref[...], *weights, emit=emit_y)

    @pl.when(i < 2)
    def _():
        _weave(branches(), projection())

    @pl.when((i >= 2) & (i < n_tiles))
    def _():
        br = branches()
        _weave(dense(), br, projection(), br)

    @pl.when(i >= n_tiles)
    def _():
        _weave(branches(), dense())

    @pl.when((tile_in_seq == tiles_per_seq - 1) & (i >= 1) & (i <= n_tiles))
    def _():
        nst_ref[...] = jnp.zeros_like(nst_ref)
        for h in range(ML_HEADS):
            final = c_scr[h]
            cst_ref[0, h] = final[:ML_DIM].T
            nst_ref[0, h:h + 1, :] = final[ML_DIM:ML_DIM + 1]
        mst_ref[0] = m_scr[...]


def _prompt_fused(sinks, x, p, cos, sin, g_mix, w_proj, b_proj, weights, batch, seq, tm):
    assert len(weights) == N_FUSED_WEIGHTS
    rows = batch * seq
    n_tiles = rows // tm
    tps = seq // tm
    t0 = lambda i: jnp.minimum(i, n_tiles - 1)
    t1 = lambda i: jnp.clip(i - 1, 0, n_tiles - 1)
    t2 = lambda i: jnp.maximum(i - 2, 0)
    const = lambda i: (0, 0)
    single = lambda a: pl.BlockSpec(a.shape, const, pipeline_mode=pl.Buffered(1))
    slot = lambda shape, dt: pltpu.VMEM((2,) + shape, dt)
    matrix_idx = tuple(j for j, w in enumerate(weights) if w.dtype == BF16)
    matrices = [weights[j] for j in matrix_idx]
    return pl.pallas_call(
        functools.partial(_prompt_kernel, tiles_per_seq=tps, matrix_idx=matrix_idx),
        grid=(n_tiles + 2,),
        in_specs=[
            pl.BlockSpec(memory_space=pltpu.SMEM),
            pl.BlockSpec((tm, D_MODEL), lambda i: (t0(i), 0)),
            pl.BlockSpec((tm, LANES), lambda i: (t0(i) % tps, 0)),
            pl.BlockSpec((tm, LANES), lambda i: (t0(i) % tps, 0)),
            pl.BlockSpec((tm, D_MODEL), lambda i: (t2(i), 0)),
            pl.BlockSpec((tm, D_PLE), lambda i: (t2(i), 0)),
            single(g_mix), single(w_proj), single(b_proj),
        ] + [pl.BlockSpec(memory_space=pl.ANY) if j in matrix_idx else single(w) for j, w in enumerate(weights)],
        out_specs=(
            pl.BlockSpec((tm, D_MODEL), lambda i: (t2(i), 0)),
            pl.BlockSpec((1, ATTN_KV, WINDOW), lambda i: (t0(i) // tps, 0, 0)),
            pl.BlockSpec((1, ATTN_KV, WINDOW), lambda i: (t0(i) // tps, 0, 0)),
            pl.BlockSpec((1, ML_HEADS, ML_DIM, ML_DIM), lambda i: (t1(i) // tps, 0, 0, 0)),
            pl.BlockSpec((1, 8, LANES), lambda i: (t1(i) // tps, 0, 0)),
            pl.BlockSpec((1, 8, LANES), lambda i: (t1(i) // tps, 0, 0)),
        ),
        out_shape=(
            jax.ShapeDtypeStruct((rows, D_MODEL), F32),
            jax.ShapeDtypeStruct((batch, ATTN_KV, WINDOW), F32),
            jax.ShapeDtypeStruct((batch, ATTN_KV, WINDOW), F32),
            jax.ShapeDtypeStruct((batch, ML_HEADS, ML_DIM, ML_DIM), F32),
            jax.ShapeDtypeStruct((batch, 8, LANES), F32),
            jax.ShapeDtypeStruct((batch, 8, LANES), F32),
        ),
        scratch_shapes=[
            slot((ATTN_Q, tm), BF16), slot((tm, ATTN_KV), BF16), slot((ATTN_KV, tm), BF16),
            slot((1, ML_W, tm), BF16), slot((tm, ML_W), BF16), slot((1, ML_W, tm), BF16),
            slot((tm, ML_W), F32), slot((tm, LANES), F32), slot((1, 8, tm), F32),
            pltpu.VMEM((WINDOW, ATTN_KV), BF16), pltpu.VMEM((ATTN_KV, WINDOW), BF16),
            slot((tm, ATTN_Q), BF16), slot((tm, ML_W), BF16),
            pltpu.VMEM((ML_HEADS, ML_STATE_ROWS, ML_DIM), F32), pltpu.VMEM((8, LANES), F32),
            *(pltpu.VMEM(w.shape, w.dtype) for w in matrices), pltpu.SemaphoreType.DMA((len(matrices),)),
        ],
        compiler_params=pltpu.CompilerParams(dimension_semantics=("arbitrary",), vmem_limit_bytes=VMEM_LIMIT),
        name="prompt_fused",
    )(sinks, x, cos, sin, x, p, g_mix, w_proj, b_proj, *weights)


def _attn_decode_kernel(sink_ref, qm_ref, kn_ref, vn_ref, ck_ref, cv_ref, o_ref, ko_ref, vo_ref):
    bt = qm_ref.shape[0]
    qm = qm_ref[...]
    kn = kn_ref[...]
    vn = vn_ref[...]
    ck = ck_ref[...]
    cv = cv_ref[...]
    head = lax.broadcasted_iota(jnp.int32, (N_Q_HEADS, 1), 0)
    sink = jnp.zeros((N_Q_HEADS, 1), F32)
    for hd in range(N_Q_HEADS):
        sink = jnp.where(head == hd, sink_ref[hd], sink)
    s = jnp.einsum('bhc,bck->bhk', qm, ck.astype(BF16), preferred_element_type=F32)
    s_new = jnp.sum(qm.astype(F32) * kn[:, None, :], axis=2, keepdims=True)
    m = jnp.maximum(jnp.maximum(jnp.max(s, axis=2, keepdims=True), s_new), sink)
    p = jnp.exp(s - m)
    p_new = jnp.exp(s_new - m)
    den = jnp.sum(p, axis=2, keepdims=True) + p_new + jnp.exp(sink - m)
    o = jnp.einsum('bhk,bck->bhc', p.astype(BF16), cv.astype(BF16), preferred_element_type=F32)
    o_ref[...] = (o + p_new * vn[:, None, :]) / den
    pad = jnp.zeros((LANES - bt, ATTN_KV), F32)
    kn_t = jnp.concatenate([kn, pad], axis=0).T
    vn_t = jnp.concatenate([vn, pad], axis=0).T
    newest = lax.broadcasted_iota(jnp.int32, (ATTN_KV, WINDOW), 1) == WINDOW - 1
    for b in range(bt):
        ko_ref[b] = jnp.where(newest, kn_t[:, b:b + 1], pltpu.roll(ck[b], WINDOW - 1, 1))
        vo_ref[b] = jnp.where(newest, vn_t[:, b:b + 1], pltpu.roll(cv[b], WINDOW - 1, 1))


def _attn_decode(sinks, qmat, k_new, v_new, cache_k, cache_v, bt):
    nbatch = qmat.shape[0]
    b3 = lambda i: (i, 0, 0)
    b2 = lambda i: (i, 0)
    return pl.pallas_call(
        _attn_decode_kernel,
        grid=(nbatch // bt,),
        in_specs=[
            pl.BlockSpec(memory_space=pltpu.SMEM),
            pl.BlockSpec((bt, N_Q_HEADS, ATTN_KV), b3),
            pl.BlockSpec((bt, ATTN_KV), b2),
            pl.BlockSpec((bt, ATTN_KV), b2),
            pl.BlockSpec((bt, WINDOW, ATTN_KV), b3),
            pl.BlockSpec((bt, WINDOW, ATTN_KV), b3),
        ],
        out_specs=(
            pl.BlockSpec((bt, N_Q_HEADS, ATTN_KV), b3),
            pl.BlockSpec((bt, WINDOW, ATTN_KV), b3),
            pl.BlockSpec((bt, WINDOW, ATTN_KV), b3),
        ),
        out_shape=(
            jax.ShapeDtypeStruct((nbatch, N_Q_HEADS, ATTN_KV), F32),
            jax.ShapeDtypeStruct((nbatch, WINDOW, ATTN_KV), F32),
            jax.ShapeDtypeStruct((nbatch, WINDOW, ATTN_KV), F32),
        ),
        compiler_params=pltpu.CompilerParams(dimension_semantics=("parallel",)),
        name="attn_decode",
    )(sinks, qmat, k_new, v_new, cache_k, cache_v)


def _mlstm_decode_kernel(q_ref, k_ref, v_ref, og_ref, gif_ref, c_ref, n_ref, m_ref, h_ref, co_ref, no_ref, mo_ref):
    bt = q_ref.shape[0]
    gif = gif_ref[...]
    ig = gif[:, 0:ML_HEADS]
    lf = gif[:, ML_HEADS:2 * ML_HEADS]
    m_prev = m_ref[...]
    a = lf + m_prev
    m_t = jnp.maximum(a, ig)
    aw = jnp.exp(a - m_t)
    e_i = jnp.exp(ig - m_t)
    floor = jnp.exp(-m_t)
    m_new = jnp.maximum(a, ig)
    decay = jnp.exp(a - m_new)
    w = jnp.exp(ig - m_new)
    mo_ref[...] = m_new
    pad = jnp.zeros((LANES - bt, ML_DIM), F32)
    lane = lax.broadcasted_iota(jnp.int32, (ML_DIM, LANES), 1)
    for h in range(ML_HEADS):
        hs = slice(h * ML_DIM, (h + 1) * ML_DIM)
        q = q_ref[:, hs]
        k = k_ref[:, hs]
        v = v_ref[:, hs]
        n = n_ref[:, h, :]
        qk = jnp.sum(q * k, axis=1, keepdims=True)
        qn = jnp.sum(q * n, axis=1, keepdims=True)
        sw = qk * e_i[:, h:h + 1]
        den = aw[:, h:h + 1] * qn + sw
        inv = 1.0 / jnp.maximum(jnp.abs(den), floor[:, h:h + 1])
        no_ref[:, h, :] = decay[:, h:h + 1] * n + w[:, h:h + 1] * k
        q_b = q.astype(BF16)
        wk_t = jnp.concatenate([w[:, h:h + 1] * k, pad], axis=0).T.astype(BF16)
        v_rows = jnp.concatenate([v, pad], axis=0).astype(BF16)
        nums = []
        for b in range(bt):
            C = c_ref[b, h]
            q_c = _mm(q_b, C.astype(BF16))[b:b + 1, :]
            nums.append(aw[b:b + 1, h:h + 1] * q_c)
            outer = _mm(jnp.where(lane == b, wk_t, jnp.zeros_like(wk_t)), v_rows)
            co_ref[b, h] = decay[b:b + 1, h:h + 1] * C + outer
        num = jnp.concatenate(nums, axis=0) + sw * v
        h_ref[:, hs] = (num * inv * og_ref[:, hs]).astype(h_ref.dtype)


def _mlstm_decode(mq, mk, mv, og, gif, state_c, state_n, state_m, bt):
    nbatch = mq.shape[0]
    b2 = lambda i: (i, 0)
    b3 = lambda i: (i, 0, 0)
    b4 = lambda i: (i, 0, 0, 0)
    return pl.pallas_call(
        _mlstm_decode_kernel,
        grid=(nbatch // bt,),
        in_specs=[
            pl.BlockSpec((bt, ML_W), b2),
            pl.BlockSpec((bt, ML_W), b2),
            pl.BlockSpec((bt, ML_W), b2),
            pl.BlockSpec((bt, ML_W), b2),
            pl.BlockSpec((bt, LANES), b2),
            pl.BlockSpec((bt, ML_HEADS, ML_DIM, ML_DIM), b4),
            pl.BlockSpec((bt, ML_HEADS, ML_DIM), b3),
            pl.BlockSpec((bt, ML_HEADS), b2),
        ],
        out_specs=(
            pl.BlockSpec((bt, ML_W), b2),
            pl.BlockSpec((bt, ML_HEADS, ML_DIM, ML_DIM), b4),
            pl.BlockSpec((bt, ML_HEADS, ML_DIM), b3),
            pl.BlockSpec((bt, ML_HEADS), b2),
        ),
        out_shape=(
            jax.ShapeDtypeStruct((nbatch, ML_W), BF16),
            jax.ShapeDtypeStruct((nbatch, ML_HEADS, ML_DIM, ML_DIM), F32),
            jax.ShapeDtypeStruct((nbatch, ML_HEADS, ML_DIM), F32),
            jax.ShapeDtypeStruct((nbatch, ML_HEADS), F32),
        ),
        compiler_params=pltpu.CompilerParams(dimension_semantics=("parallel",)),
        name="mlstm_decode",
    )(mq, mk, mv, og, gif, state_c, state_n, state_m)


def _rope_tables(pos):
    half = HEAD_DIM // 2
    inv = (1.0 / (ROPE_THETA ** (np.arange(half, dtype=np.float32) / half))).astype(np.float32)
    ang = pos.astype(np.float32)[:, None] * inv[None, :]
    cos = np.cos(ang)
    sin = np.sin(ang)
    cos_t = np.concatenate([cos, cos, cos, cos], axis=1)
    sin_t = np.concatenate([-sin, sin, -sin, sin], axis=1)
    return jnp.asarray(cos_t, F32), jnp.asarray(sin_t, F32)


def kernel(x_prompt, x_sample, cache_k, cache_v, state_C, state_n, state_m, p_prompt, p_sample, norm_mix, w_in, b_in, attn_sinks, w_attn_up, w_ml_up, w_o, norm_mlp, w_ff1, w_ff2, norm_ple, w_ple_gate, w_ple_proj, norm_final):
    assert w_in.shape[0] == 1, "single layer"
    batch, seq, _ = x_prompt.shape
    nbatch = x_sample.shape[0]
    assert x_sample.shape[1] == 1

    w_proj, w_gate, w_au, w_mu, w_out, w_f1, w_f2, w_pg, w_pp = _weight_cast(
        w_in[0].T, (w_attn_up[0], w_ml_up[0], w_o[0], w_ff1[0], w_ff2[0], w_ple_gate[0], w_ple_proj[0]))
    b_proj = b_in[:, :N_PROJ]
    b_gate = b_in[:, OFF_GA:]
    g_mix = norm_mix[0][None, :]
    out_weights = (
        g_mix, w_gate, b_gate, w_au, w_mu, w_out,
        norm_mlp[0][None, :], w_f1, w_f2,
        norm_ple[0][None, :], w_pg, w_pp, norm_final[None, :],
    )
    sinks = attn_sinks[0]
    to_fp = lambda c: c[0].transpose(0, 2, 3, 1).reshape(c.shape[1], ATTN_KV, WINDOW)
    from_fp = lambda c: c.reshape(c.shape[0], N_KV_HEADS, HEAD_DIM, WINDOW).transpose(0, 3, 1, 2)[None]

    cos_p, sin_p = _rope_tables(np.arange(seq, dtype=np.int32))
    xp = x_prompt.reshape(batch * seq, D_MODEL)
    y_prompt, k_last, v_last, cst, nst, mst = _prompt_fused(
        sinks, xp, p_prompt[0].reshape(batch * seq, D_PLE), cos_p, sin_p, g_mix, w_proj, b_proj, out_weights,
        batch, seq, tm=PROMPT_TILE)
    y_prompt = y_prompt.reshape(batch, seq, D_MODEL)
    k_prompt = from_fp(k_last)
    v_prompt = from_fp(v_last)
    c_prompt = cst[None]
    n_prompt = nst[:, :ML_HEADS, :][None]
    m_prompt = mst[:, :ML_HEADS, 0][None]

    cos_s, sin_s = _rope_tables(np.full((nbatch,), PAST_LEN, dtype=np.int32))
    xs = x_sample.reshape(nbatch, D_MODEL)
    qs, _, _, ks, vs, mqs, mks, mvs, ogs, gifs, _ = _in_proj_sample(xs, g_mix, w_proj, b_proj, cos_s, sin_s)
    qmat = qs.reshape(nbatch, N_Q_HEADS, ATTN_KV)
    o_full, k_s, v_s = _attn_decode(sinks, qmat, ks, vs, to_fp(cache_k), to_fp(cache_v), bt=DECODE_ROWS)
    o5 = o_full.reshape(nbatch, N_KV_HEADS, GQA_GROUP, N_KV_HEADS, HEAD_DIM)
    a_s = jnp.stack([o5[:, kv, :, kv, :] for kv in range(N_KV_HEADS)], axis=1).reshape(nbatch, ATTN_Q).astype(BF16)
    h_s, c_s, n_s, m_s = _mlstm_decode(mqs, mks, mvs, ogs, gifs, state_C[0], state_n[0], state_m[0], bt=DECODE_ROWS)
    y_sample = _out_proj_sample(xs, a_s, h_s, p_sample[0].reshape(nbatch, D_PLE), out_weights)
    y_sample = y_sample.reshape(nbatch, 1, D_MODEL)
    k_sample = from_fp(k_s)
    v_sample = from_fp(v_s)

    return (y_prompt, y_sample, k_prompt, v_prompt, c_prompt, n_prompt, m_prompt,
            k_sample, v_sample, c_s[None], n_s[None], m_s[None])
```

```python
import functools

import numpy as np
import jax
import jax.numpy as jnp
from jax import lax
from jax.experimental import pallas as pl
from jax.experimental.pallas import tpu as pltpu

F32 = jnp.float32
BF16 = jnp.bfloat16

D_MODEL = 1024
HEAD_DIM = 64
N_Q_HEADS = 8
N_KV_HEADS = 2
GQA_GROUP = N_Q_HEADS // N_KV_HEADS
WINDOW = 128
ROPE_THETA = 10000.0
ML_HEADS = 4
ML_DIM = 128
ML_CHUNK = 128
D_FF = 4 * D_MODEL
D_PLE = 256
EPS = 1e-6
PAST_LEN = 16384

ATTN_Q = N_Q_HEADS * HEAD_DIM
ATTN_KV = N_KV_HEADS * HEAD_DIM
ML_W = ML_HEADS * ML_DIM
LANES = 128

OFF_AQ = 0
OFF_AK = OFF_AQ + ATTN_Q
OFF_AV = OFF_AK + ATTN_KV
OFF_MQ = OFF_AV + ATTN_KV
OFF_MK = OFF_MQ + ML_W
OFF_MV = OFF_MK + ML_W
OFF_MO = OFF_MV + ML_W
OFF_MI = OFF_MO + ML_W
OFF_MF = OFF_MI + ML_HEADS
OFF_GA = OFF_MF + ML_HEADS
OFF_GM = OFF_GA + D_MODEL
D_IN = OFF_GM + D_MODEL
N_PROJ = OFF_MI + LANES

VMEM_LIMIT = 60 * 1024 * 1024
PROMPT_TILE = 256
DECODE_ROWS = 32

def _rms(x, g):
    r = lax.rsqrt(jnp.mean(x * x, axis=-1, keepdims=True) + EPS)
    return (x * r) * g


def _mm(a, b):
    return jnp.dot(a, b, preferred_element_type=F32)


def _log_sigmoid(x):
    return jnp.minimum(x, 0.0) - jnp.log1p(jnp.exp(-jnp.abs(x)))


def _weave(*streams):
    live = list(streams)
    while live:
        for g in list(live):
            try:
                next(g)
            except StopIteration:
                live.remove(g)


class _AwaitedRef:
    def __init__(self, ref, copy):
        self._ref, self._copy = ref, copy

    def __getitem__(self, idx):
        if self._copy is not None:
            self._copy.wait()
            self._copy = None
        return self._ref[idx]


WEIGHT_ROW_SPLIT = 8


def _weight_cast_kernel(*refs):
    n_plain = (len(refs) - 3) // 2
    w_in_t_ref, plain_in = refs[0], refs[1:1 + n_plain]
    w_proj_ref, w_gate_ref, plain_out = refs[1 + n_plain], refs[2 + n_plain], refs[3 + n_plain:]
    for g in range(N_PROJ // LANES):
        w_proj_ref[:, g * LANES:(g + 1) * LANES] = w_in_t_ref[g * LANES:(g + 1) * LANES, :].T.astype(BF16)
    for g in range(2 * D_MODEL // LANES):
        w_gate_ref[:, g * LANES:(g + 1) * LANES] = w_in_t_ref[OFF_GA + g * LANES:OFF_GA + (g + 1) * LANES, :].T.astype(BF16)
    for src, dst in zip(plain_in, plain_out):
        dst[...] = src[...].astype(BF16)


def _weight_cast(w_in_t, plain):
    split = WEIGHT_ROW_SPLIT
    assert D_MODEL // split == LANES
    rows_blk = lambda w: pl.BlockSpec((w.shape[0] // split, w.shape[1]), lambda i: (i, 0))
    out_shapes = (jax.ShapeDtypeStruct((D_MODEL, N_PROJ), BF16), jax.ShapeDtypeStruct((D_MODEL, 2 * D_MODEL), BF16),
                  *(jax.ShapeDtypeStruct(w.shape, BF16) for w in plain))
    return pl.pallas_call(
        _weight_cast_kernel,
        grid=(split,),
        in_specs=[pl.BlockSpec((D_IN, LANES), lambda i: (0, i))] + [rows_blk(w) for w in plain],
        out_specs=tuple(rows_blk(s) for s in out_shapes),
        out_shape=out_shapes,
        compiler_params=pltpu.CompilerParams(dimension_semantics=("parallel",), vmem_limit_bytes=VMEM_LIMIT),
        name="weight_cast",
    )(w_in_t, *plain)


def _in_proj_stages(x, g_ref, w_ref, b_ref, cos_ref, sin_ref,
                    q_ref, k_ref, v_ref, kl_ref, vl_ref, mq_ref, mk_ref, mv_ref, og_ref, gif_ref, gt_ref,
                    *, feature_major):
    xn = _rms(x, g_ref[...]).astype(BF16)
    tm = x.shape[0]

    def proj(lo, hi):
        return _mm(xn, w_ref[:, lo:hi]) + b_ref[:, lo:hi]

    cos = cos_ref[...]
    sin = sin_ref[...]
    lane = lax.broadcasted_iota(jnp.int32, (1, LANES), 1)
    first_half = (lane % HEAD_DIM) < (HEAD_DIM // 2)

    def rope(z):
        partner = jnp.where(first_half, pltpu.roll(z, LANES - HEAD_DIM // 2, 1), pltpu.roll(z, HEAD_DIM // 2, 1))
        return z * cos + partner * sin

    zq = proj(OFF_AQ, OFF_AK)
    for c in range(ATTN_Q // LANES):
        sl = slice(c * LANES, (c + 1) * LANES)
        qc = rope(zq[:, sl]) * (HEAD_DIM ** -0.5)
        if feature_major:
            q_ref[sl, :] = qc.T.astype(q_ref.dtype)
        else:
            kv = (2 * c) // GQA_GROUP
            own = (lane // HEAD_DIM) == kv
            swapped = pltpu.roll(qc, HEAD_DIM, 1)
            for j, src in enumerate((qc, swapped) if kv == 0 else (swapped, qc)):
                hd = 2 * c + j
                q_ref[:, hd * LANES:(hd + 1) * LANES] = jnp.where(own, src, 0.0).astype(q_ref.dtype)
    yield
    zkv = proj(OFF_AK, OFF_MQ)
    k = rope(zkv[:, :ATTN_KV])
    v = zkv[:, ATTN_KV:]
    k_ref[...] = k.astype(k_ref.dtype)
    v_ref[...] = (v.T if feature_major else v).astype(v_ref.dtype)
    if feature_major:
        kl_ref[0] = k[tm - WINDOW:, :].T
        vl_ref[0] = v[tm - WINDOW:, :].T
    else:
        kl_ref[...] = k[tm - WINDOW:, :]
        vl_ref[...] = v[tm - WINDOW:, :]
    yield
    for z_off, dst in ((OFF_MQ, mq_ref), (OFF_MV, mv_ref)):
        z = proj(z_off, z_off + ML_W)
        if feature_major:
            for h in range(ML_HEADS):
                hs = slice(h * ML_DIM, (h + 1) * ML_DIM)
                dst[0, hs, :] = z[:, hs].T.astype(dst.dtype)
        else:
            dst[...] = z.astype(dst.dtype)
        yield
    mk_ref[...] = (proj(OFF_MK, OFF_MV) * (ML_DIM ** -0.5)).astype(mk_ref.dtype)
    yield
    og_ref[...] = jax.nn.sigmoid(proj(OFF_MO, OFF_MI))
    yield
    zg = proj(OFF_MI, N_PROJ)
    gif = jnp.where(lane < ML_HEADS, zg, jnp.where(lane < 2 * ML_HEADS, _log_sigmoid(zg), 0.0))
    if feature_major:
        L = ML_CHUNK
        upper = (lax.broadcasted_iota(jnp.int32, (L, L), 0) <= lax.broadcasted_iota(jnp.int32, (L, L), 1)).astype(F32)
        row8 = lax.broadcasted_iota(jnp.int32, (8, 1), 0)
        fill = jnp.zeros((L - 16, L), F32)
        for c in range(tm // L):
            cs = slice(c * L, (c + 1) * L)
            g_rows = gif[cs, :].T[0:8, :]
            cum = jnp.dot(g_rows, upper, precision=lax.Precision.HIGHEST, preferred_element_type=F32)
            rows = jnp.where(row8 < ML_HEADS, g_rows, cum)
            gt_ref[0, :, cs] = rows
            diff = rows - pltpu.roll(rows, ML_HEADS, 0)
            gif_ref[cs, :] = jnp.concatenate([rows, diff, fill], axis=0).T
    else:
        gif_ref[...] = gif
        gt_ref[0] = jnp.zeros(gt_ref.shape[1:], F32)
    yield


def _in_proj_kernel(x_ref, *refs, feature_major):
    _weave(_in_proj_stages(x_ref[...], *refs, feature_major=feature_major))


def _in_proj_sample(x, g, w, b, cos, sin):
    rows = x.shape[0]
    assert rows == WINDOW
    full = lambda n, dt: (jax.ShapeDtypeStruct((rows, n), dt), pl.BlockSpec((rows, n), lambda i: (0, 0)))
    outs = (
        full(N_Q_HEADS * LANES, BF16), full(ATTN_KV, BF16), full(ATTN_KV, BF16), full(ATTN_KV, F32), full(ATTN_KV, F32),
        full(ML_W, F32), full(ML_W, F32), full(ML_W, F32), full(ML_W, F32), full(LANES, F32),
        (jax.ShapeDtypeStruct((1, 8, rows), F32), pl.BlockSpec((1, 8, rows), lambda i: (0, 0, 0))),
    )
    return pl.pallas_call(
        functools.partial(_in_proj_kernel, feature_major=False),
        grid=(1,),
        in_specs=[pl.BlockSpec(a.shape, lambda i: (0, 0)) for a in (x, g, w, b, cos, sin)],
        out_specs=tuple(o[1] for o in outs),
        out_shape=tuple(o[0] for o in outs),
        compiler_params=pltpu.CompilerParams(dimension_semantics=("arbitrary",), vmem_limit_bytes=VMEM_LIMIT),
        name="in_proj",
    )(x, g, w, b, cos, sin)


def _attn_stages(sink_ref, qt_ref, kp_ref, kc_ref, vtp_ref, vtc_ref, first_block, emit):
    L = WINDOW
    nsub = kc_ref.shape[0] // L
    key = lax.broadcasted_iota(jnp.int32, (2 * L, L), 0)
    t = lax.broadcasted_iota(jnp.int32, (2 * L, L), 1)
    is_prev = key < L
    allowed = (is_prev & (key >= t)) | (~is_prev & (key - L <= t))
    bias = jnp.where(allowed, 0.0, -jnp.inf)
    bias_first = jnp.where(is_prev & first_block, -jnp.inf, bias)
    zeros_q = jnp.zeros((HEAD_DIM, L), BF16)
    heads = range(N_Q_HEADS)
    for r in range(nsub):
        ts = slice(r * L, (r + 1) * L)
        k_prev = kp_ref[...] if r == 0 else kc_ref[(r - 1) * L:r * L, :]
        vt_prev = vtp_ref[...] if r == 0 else vtc_ref[:, (r - 1) * L:r * L]
        k2 = jnp.concatenate([k_prev, kc_ref[ts, :]], axis=0)
        vt2 = jnp.concatenate([vt_prev, vtc_ref[:, ts]], axis=1)
        b_r = bias_first if r == 0 else bias
        q_op = [jnp.concatenate([qt_ref[hd * HEAD_DIM:(hd + 1) * HEAD_DIM, ts], zeros_q][::1 if hd < GQA_GROUP else -1],
                                axis=0) for hd in heads]
        s = [_mm(k2, q_op[hd]) + b_r for hd in heads]
        yield
        m = [jnp.maximum(jnp.max(s[hd], axis=0, keepdims=True), sink_ref[hd]) for hd in heads]
        yield
        p = [jnp.exp(s[hd] - m[hd]) for hd in heads]
        yield
        den = [jnp.sum(p[hd], axis=0, keepdims=True) + jnp.exp(sink_ref[hd] - m[hd]) for hd in heads]
        o = [_mm(vt2[(hd // GQA_GROUP) * HEAD_DIM:(hd // GQA_GROUP + 1) * HEAD_DIM, :], p[hd].astype(BF16)) / den[hd]
             for hd in heads]
        yield
        emit(r, jnp.concatenate(o, axis=0).T)
        yield


ML_STATE_ROWS = ML_DIM + 8


def _mlstm_stages(chains, emit):
    L = ML_CHUNK
    s_idx = lax.broadcasted_iota(jnp.int32, (L, L), 0)
    t_idx = lax.broadcasted_iota(jnp.int32, (L, L), 1)
    causal = s_idx <= t_idx
    n = range(len(chains))
    dlog = [jnp.where(causal, ch["b_r"] + ch["u_c"], -jnp.inf) for ch in chains]
    yield
    qk = [_mm(ch["k"], ch["qt"]) for ch in chains]
    inter = [_mm(ch["state"].astype(BF16), ch["qt"]) for ch in chains]
    yield
    a = [ch["b_r"] + ch["m_prev"] for ch in chains]
    m_t = [jnp.maximum(a[i], jnp.max(dlog[i], axis=0, keepdims=True)) for i in n]
    yield
    sw = [qk[i] * jnp.exp(dlog[i] - m_t[i]) for i in n]
    yield
    sv = [_mm(chains[i]["vt"], sw[i].astype(BF16)) for i in n]
    yield
    hs = []
    for i in n:
        aw = jnp.exp(a[i] - m_t[i])
        num = aw * inter[i][:ML_DIM] + sv[i]
        den = aw * inter[i][ML_DIM:ML_DIM + 1] + jnp.sum(sw[i], axis=0, keepdims=True)
        hh = num / jnp.maximum(jnp.abs(den), jnp.exp(-m_t[i]))
        hs.append(hh.T * chains[i]["og"])
    yield
    pad = jnp.zeros((ML_STATE_ROWS - ML_DIM - 1, L), F32)
    new_state, new_m = [], []
    for i in n:
        ch = chains[i]
        b_end = ch["b_r"][:, L - 1:L]
        g = b_end - ch["b_r"] + ch["ig_r"]
        m_new = jnp.maximum(b_end + ch["m_prev"], jnp.max(g, axis=1, keepdims=True))
        decay = jnp.exp(b_end + ch["m_prev"] - m_new)
        w = jnp.exp(g - m_new)
        v_aug = jnp.concatenate([ch["vt"].astype(F32) * w, w, pad], axis=0).astype(BF16)
        new_state.append(decay * ch["state"] + _mm(v_aug, ch["k"]))
        new_m.append(m_new)
    emit(hs, new_state, new_m)
    yield


DENSE_COLS = 512


def _dense_stages(x, a, hm, p, gmix_ref, wg_ref, bg_ref, wau_ref, wmu_ref, wo_ref,
                  gmlp_ref, wff1_ref, wff2_ref, gple_ref, wpg_ref, wpp_ref, gfin_ref, emit):
    nc = DENSE_COLS
    pieces = lambda n: [slice(c, c + nc) for c in range(0, n, nc)]
    xn = _rms(x, gmix_ref[...]).astype(BF16)
    gates = []
    for cs in pieces(2 * D_MODEL):
        gates.append(jax.nn.sigmoid(_mm(xn, wg_ref[:, cs]) + bg_ref[:, cs]))
        yield
    half = D_MODEL // nc
    mix = []
    for j, cs in enumerate(pieces(D_MODEL)):
        mix.append((gates[j] * _mm(a, wau_ref[:, cs]) + gates[half + j] * _mm(hm, wmu_ref[:, cs])).astype(BF16))
        yield
    mix = jnp.concatenate(mix, axis=1)
    h = []
    for cs in pieces(D_MODEL):
        h.append(x[:, cs] + _mm(mix, wo_ref[:, cs]))
        yield
    h = jnp.concatenate(h, axis=1)
    hn = _rms(h, gmlp_ref[...]).astype(BF16)
    u = []
    for cs in pieces(D_FF):
        u.append(jnp.square(jnp.maximum(_mm(hn, wff1_ref[:, cs]), 0.0)).astype(BF16))
        yield
    u = jnp.concatenate(u, axis=1)
    h2 = []
    for cs in pieces(D_MODEL):
        h2.append(h[:, cs] + _mm(u, wff2_ref[:, cs]))
        yield
    h = jnp.concatenate(h2, axis=1)
    hn = _rms(h, gple_ref[...]).astype(BF16)
    pb = p.astype(BF16)
    h3 = []
    for cs in pieces(D_MODEL):
        h3.append(h[:, cs] + _mm(pb, wpp_ref[:, cs]) * jax.nn.sigmoid(_mm(hn, wpg_ref[:, cs])))
        yield
    emit(_rms(jnp.concatenate(h3, axis=1), gfin_ref[...]))
    yield


def _out_kernel(x_ref, a_ref, hm_ref, p_ref, *rest, matrix_idx):
    weights, y_ref = list(rest[:N_FUSED_WEIGHTS]), rest[N_FUSED_WEIGHTS]
    w_scr, dma_sem = rest[N_FUSED_WEIGHTS + 1:-1], rest[-1]
    copies = [pltpu.make_async_copy(weights[j], w_scr[n], dma_sem.at[n]) for n, j in enumerate(matrix_idx)]
    for c in copies:
        c.start()
    for n, j in enumerate(matrix_idx):
        weights[j] = _AwaitedRef(w_scr[n], copies[n])

    def emit(y):
        y_ref[...] = y

    _weave(_dense_stages(x_ref[...], a_ref[...], hm_ref[...], p_ref[...], *weights, emit=emit))


def _out_proj_sample(x, a, hm, p, weights):
    assert len(weights) == N_FUSED_WEIGHTS
    matrix_idx = tuple(j for j, w in enumerate(weights) if w.dtype == BF16)
    matrices = [weights[j] for j in matrix_idx]
    whole = lambda o: pl.BlockSpec(o.shape, lambda i: (0, 0), pipeline_mode=pl.Buffered(1))
    return pl.pallas_call(
        functools.partial(_out_kernel, matrix_idx=matrix_idx),
        grid=(1,),
        in_specs=[whole(o) for o in (x, a, hm, p)]
        + [pl.BlockSpec(memory_space=pl.ANY) if j in matrix_idx else whole(w) for j, w in enumerate(weights)],
        out_specs=pl.BlockSpec(x.shape, lambda i: (0, 0)),
        out_shape=jax.ShapeDtypeStruct(x.shape, F32),
        scratch_shapes=[*(pltpu.VMEM(w.shape, w.dtype) for w in matrices), pltpu.SemaphoreType.DMA((len(matrices),))],
        compiler_params=pltpu.CompilerParams(dimension_semantics=("arbitrary",), vmem_limit_bytes=VMEM_LIMIT),
        name="out_proj",
    )(x, a, hm, p, *weights)


N_FUSED_WEIGHTS = 13
N_PROJ_SLOTS = 9


def _prompt_kernel(sink_ref, x0_ref, cos_ref, sin_ref, x2_ref, p_ref, gmix_ref, wp_ref, bp_ref, *rest,
                   tiles_per_seq, matrix_idx):
    weights = list(rest[:N_FUSED_WEIGHTS])
    y_ref, kl_ref, vl_ref, cst_ref, nst_ref, mst_ref = rest[N_FUSED_WEIGHTS:N_FUSED_WEIGHTS + 6]
    scr = rest[N_FUSED_WEIGHTS + 6:]
    proj_scr = scr[:N_PROJ_SLOTS]
    kprev_scr, vtprev_scr, a_scr, h_scr, c_scr, m_scr = scr[N_PROJ_SLOTS:N_PROJ_SLOTS + 6]
    w_scr, dma_sem = scr[N_PROJ_SLOTS + 6:-1], scr[-1]
    w_copies = [pltpu.make_async_copy(weights[j], w_scr[n], dma_sem.at[n]) for n, j in enumerate(matrix_idx)]
    for n, j in enumerate(matrix_idx):
        weights[j] = w_scr[n]
    i = pl.program_id(0)
    n_tiles = pl.num_programs(0) - 2
    t1 = jnp.clip(i - 1, 0, n_tiles - 1)
    tile_in_seq = t1 % tiles_per_seq
    s0, s1 = i % 2, (i + 1) % 2
    L = ML_CHUNK
    tm = x0_ref.shape[0]

    @pl.when(i == 0)
    def _():
        for c in w_copies:
            c.start()
        for r in (*proj_scr, kprev_scr, vtprev_scr, a_scr, h_scr, c_scr, m_scr):
            r[...] = jnp.zeros_like(r)

    @pl.when(i == 2)
    def _():
        for c in w_copies:
            c.wait()

    @pl.when(tile_in_seq == 0)
    def _():
        c_scr[...] = jnp.zeros_like(c_scr)
        m_scr[...] = jnp.zeros_like(m_scr)

    def projection():
        outs = [r.at[s0] for r in proj_scr]
        yield from _in_proj_stages(x0_ref[...], gmix_ref, wp_ref, bp_ref, cos_ref, sin_ref,
                                   outs[0], outs[1], outs[2], kl_ref, vl_ref, *outs[3:], feature_major=True)

    def branches():
        qt_ref, kc_ref, vtc_ref, mqt_ref, mk_ref, mvt_ref, og_ref, gif_ref, gt_ref = [r.at[s1] for r in proj_scr]
        slot = s1

        def emit_attn(r, blk):
            a_scr[slot, r * WINDOW:(r + 1) * WINDOW, :] = blk.astype(a_scr.dtype)

        yield from _attn_stages(sink_ref, qt_ref, kprev_scr, kc_ref, vtprev_scr, vtc_ref, tile_in_seq == 0, emit_attn)
        kprev_scr[...] = kc_ref[tm - WINDOW:, :]
        vtprev_scr[...] = vtc_ref[:, tm - WINDOW:]
        hsl = lambda h: slice(h * ML_DIM, (h + 1) * ML_DIM)
        carry = dict(state=[c_scr[h] for h in range(ML_HEADS)], m=[m_scr[h:h + 1, :] for h in range(ML_HEADS)])
        for c in range(tm // L):
            cs = slice(c * L, (c + 1) * L)

            def emit_ml(hs, new_state, new_m, cs=cs):
                for h in range(ML_HEADS):
                    h_scr[slot, cs, hsl(h)] = hs[h].astype(h_scr.dtype)
                carry["state"], carry["m"] = new_state, new_m

            chains = [dict(qt=mqt_ref[0, hsl(h), cs], k=mk_ref[cs, hsl(h)], vt=mvt_ref[0, hsl(h), cs],
                           og=og_ref[cs, hsl(h)], ig_r=gt_ref[0, h:h + 1, cs],
                           b_r=gt_ref[0, ML_HEADS + h:ML_HEADS + h + 1, cs],
                           u_c=gif_ref[cs, 2 * ML_HEADS + h:2 * ML_HEADS + h + 1],
                           state=carry["state"][h], m_prev=carry["m"][h]) for h in range(ML_HEADS)]
            yield from _mlstm_stages(chains, emit_ml)
        for h in range(ML_HEADS):
            c_scr[h] = carry["state"][h]
            m_scr[h:h + 1, :] = carry["m"][h]

    def dense():
        def emit_y(y):
            y_ref[...] = y

        return _dense_stages(x2_ref[...], a_scr[s0], h_scr[s0], p_ref[...], *weights, emit=emit_y)

    @pl.when(i < 2)
    def _():
        _weave(branches(), projection())

    @pl.when((i >= 2) & (i < n_tiles))
    def _():
        _weave(branches(), dense(), projection())

    @pl.when(i >= n_tiles)
    def _():
        _weave(branches(), dense())

    @pl.when((tile_in_seq == tiles_per_seq - 1) & (i >= 1) & (i <= n_tiles))
    def _():
        nst_ref[...] = jnp.zeros_like(nst_ref)
        for h in range(ML_HEADS):
            final = c_scr[h]
            cst_ref[0, h] = final[:ML_DIM].T
            nst_ref[0, h:h + 1, :] = final[ML_DIM:ML_DIM + 1]
        mst_ref[0] = m_scr[...]


def _prompt_fused(sinks, x, p, cos, sin, g_mix, w_proj, b_proj, weights, batch, seq, tm):
    assert len(weights) == N_FUSED_WEIGHTS
    rows = batch * seq
    n_tiles = rows // tm
    tps = seq // tm
    t0 = lambda i: jnp.minimum(i, n_tiles - 1)
    t1 = lambda i: jnp.clip(i - 1, 0, n_tiles - 1)
    t2 = lambda i: jnp.maximum(i - 2, 0)
    const = lambda i: (0, 0)
    single = lambda a: pl.BlockSpec(a.shape, const, pipeline_mode=pl.Buffered(1))
    slot = lambda shape, dt: pltpu.VMEM((2,) + shape, dt)
    matrix_idx = tuple(j for j, w in enumerate(weights) if w.dtype == BF16)
    matrices = [weights[j] for j in matrix_idx]
    return pl.pallas_call(
        functools.partial(_prompt_kernel, tiles_per_seq=tps, matrix_idx=matrix_idx),
        grid=(n_tiles + 2,),
        in_specs=[
            pl.BlockSpec(memory_space=pltpu.SMEM),
            pl.BlockSpec((tm, D_MODEL), lambda i: (t0(i), 0)),
            pl.BlockSpec((tm, LANES), lambda i: (t0(i) % tps, 0)),
            pl.BlockSpec((tm, LANES), lambda i: (t0(i) % tps, 0)),
            pl.BlockSpec((tm, D_MODEL), lambda i: (t2(i), 0)),
            pl.BlockSpec((tm, D_PLE), lambda i: (t2(i), 0)),
            single(g_mix), single(w_proj), single(b_proj),
        ] + [pl.BlockSpec(memory_space=pl.ANY) if j in matrix_idx else single(w) for j, w in enumerate(weights)],
        out_specs=(
            pl.BlockSpec((tm, D_MODEL), lambda i: (t2(i), 0)),
            pl.BlockSpec((1, ATTN_KV, WINDOW), lambda i: (t0(i) // tps, 0, 0)),
            pl.BlockSpec((1, ATTN_KV, WINDOW), lambda i: (t0(i) // tps, 0, 0)),
            pl.BlockSpec((1, ML_HEADS, ML_DIM, ML_DIM), lambda i: (t1(i) // tps, 0, 0, 0)),
            pl.BlockSpec((1, 8, LANES), lambda i: (t1(i) // tps, 0, 0)),
            pl.BlockSpec((1, 8, LANES), lambda i: (t1(i) // tps, 0, 0)),
        ),
        out_shape=(
            jax.ShapeDtypeStruct((rows, D_MODEL), F32),
            jax.ShapeDtypeStruct((batch, ATTN_KV, WINDOW), F32),
            jax.ShapeDtypeStruct((batch, ATTN_KV, WINDOW), F32),
            jax.ShapeDtypeStruct((batch, ML_HEADS, ML_DIM, ML_DIM), F32),
            jax.ShapeDtypeStruct((batch, 8, LANES), F32),
            jax.ShapeDtypeStruct((batch, 8, LANES), F32),
        ),
        scratch_shapes=[
            slot((ATTN_Q, tm), BF16), slot((tm, ATTN_KV), BF16), slot((ATTN_KV, tm), BF16),
            slot((1, ML_W, tm), BF16), slot((tm, ML_W), BF16), slot((1, ML_W, tm), BF16),
            slot((tm, ML_W), F32), slot((tm, LANES), F32), slot((1, 8, tm), F32),
            pltpu.VMEM((WINDOW, ATTN_KV), BF16), pltpu.VMEM((ATTN_KV, WINDOW), BF16),
            slot((tm, ATTN_Q), BF16), slot((tm, ML_W), BF16),
            pltpu.VMEM((ML_HEADS, ML_STATE_ROWS, ML_DIM), F32), pltpu.VMEM((8, LANES), F32),
            *(pltpu.VMEM(w.shape, w.dtype) for w in matrices), pltpu.SemaphoreType.DMA((len(matrices),)),
        ],
        compiler_params=pltpu.CompilerParams(dimension_semantics=("arbitrary",), vmem_limit_bytes=VMEM_LIMIT),
        name="prompt_fused",
    )(sinks, x, cos, sin, x, p, g_mix, w_proj, b_proj, *weights)


def _attn_decode_kernel(sink_ref, qm_ref, kn_ref, vn_ref, ck_ref, cv_ref, o_ref, ko_ref, vo_ref):
    bt = qm_ref.shape[0]
    qm = qm_ref[...]
    kn = kn_ref[...]
    vn = vn_ref[...]
    ck = ck_ref[...]
    cv = cv_ref[...]
    head = lax.broadcasted_iota(jnp.int32, (N_Q_HEADS, 1), 0)
    sink = jnp.zeros((N_Q_HEADS, 1), F32)
    for hd in range(N_Q_HEADS):
        sink = jnp.where(head == hd, sink_ref[hd], sink)
    s = jnp.einsum('bhc,bck->bhk', qm, ck.astype(BF16), preferred_element_type=F32)
    s_new = jnp.sum(qm.astype(F32) * kn[:, None, :], axis=2, keepdims=True)
    m = jnp.maximum(jnp.maximum(jnp.max(s, axis=2, keepdims=True), s_new), sink)
    p = jnp.exp(s - m)
    p_new = jnp.exp(s_new - m)
    den = jnp.sum(p, axis=2, keepdims=True) + p_new + jnp.exp(sink - m)
    o = jnp.einsum('bhk,bck->bhc', p.astype(BF16), cv.astype(BF16), preferred_element_type=F32)
    o_ref[...] = (o + p_new * vn[:, None, :]) / den
    pad = jnp.zeros((LANES - bt, ATTN_KV), F32)
    kn_t = jnp.concatenate([kn, pad], axis=0).T
    vn_t = jnp.concatenate([vn, pad], axis=0).T
    newest = lax.broadcasted_iota(jnp.int32, (ATTN_KV, WINDOW), 1) == WINDOW - 1
    for b in range(bt):
        ko_ref[b] = jnp.where(newest, kn_t[:, b:b + 1], pltpu.roll(ck[b], WINDOW - 1, 1))
        vo_ref[b] = jnp.where(newest, vn_t[:, b:b + 1], pltpu.roll(cv[b], WINDOW - 1, 1))


def _attn_decode(sinks, qmat, k_new, v_new, cache_k, cache_v, bt):
    nbatch = qmat.shape[0]
    b3 = lambda i: (i, 0, 0)
    b2 = lambda i: (i, 0)
    return pl.pallas_call(
        _attn_decode_kernel,
        grid=(nbatch // bt,),
        in_specs=[
            pl.BlockSpec(memory_space=pltpu.SMEM),
            pl.BlockSpec((bt, N_Q_HEADS, ATTN_KV), b3),
            pl.BlockSpec((bt, ATTN_KV), b2),
            pl.BlockSpec((bt, ATTN_KV), b2),
            pl.BlockSpec((bt, WINDOW, ATTN_KV), b3),
            pl.BlockSpec((bt, WINDOW, ATTN_KV), b3),
        ],
        out_specs=(
            pl.BlockSpec((bt, N_Q_HEADS, ATTN_KV), b3),
            pl.BlockSpec((bt, WINDOW, ATTN_KV), b3),
            pl.BlockSpec((bt, WINDOW, ATTN_KV), b3),
        ),
        out_shape=(
            jax.ShapeDtypeStruct((nbatch, N_Q_HEADS, ATTN_KV), F32),
            jax.ShapeDtypeStruct((nbatch, WINDOW, ATTN_KV), F32),
            jax.ShapeDtypeStruct((nbatch, WINDOW, ATTN_KV), F32),
        ),
        compiler_params=pltpu.CompilerParams(dimension_semantics=("parallel",)),
        name="attn_decode",
    )(sinks, qmat, k_new, v_new, cache_k, cache_v)


def _mlstm_decode_kernel(q_ref, k_ref, v_ref, og_ref, gif_ref, c_ref, n_ref, m_ref, h_ref, co_ref, no_ref, mo_ref):
    bt = q_ref.shape[0]
    gif = gif_ref[...]
    ig = gif[:, 0:ML_HEADS]
    lf = gif[:, ML_HEADS:2 * ML_HEADS]
    m_prev = m_ref[...]
    a = lf + m_prev
    m_t = jnp.maximum(a, ig)
    aw = jnp.exp(a - m_t)
    e_i = jnp.exp(ig - m_t)
    floor = jnp.exp(-m_t)
    m_new = jnp.maximum(a, ig)
    decay = jnp.exp(a - m_new)
    w = jnp.exp(ig - m_new)
    mo_ref[...] = m_new
    pad = jnp.zeros((LANES - bt, ML_DIM), F32)
    lane = lax.broadcasted_iota(jnp.int32, (ML_DIM, LANES), 1)
    for h in range(ML_HEADS):
        hs = slice(h * ML_DIM, (h + 1) * ML_DIM)
        q = q_ref[:, hs]
        k = k_ref[:, hs]
        v = v_ref[:, hs]
        n = n_ref[:, h, :]
        qk = jnp.sum(q * k, axis=1, keepdims=True)
        qn = jnp.sum(q * n, axis=1, keepdims=True)
        sw = qk * e_i[:, h:h + 1]
        den = aw[:, h:h + 1] * qn + sw
        inv = 1.0 / jnp.maximum(jnp.abs(den), floor[:, h:h + 1])
        no_ref[:, h, :] = decay[:, h:h + 1] * n + w[:, h:h + 1] * k
        q_b = q.astype(BF16)
        wk_t = jnp.concatenate([w[:, h:h + 1] * k, pad], axis=0).T.astype(BF16)
        v_rows = jnp.concatenate([v, pad], axis=0).astype(BF16)
        nums = []
        for b in range(bt):
            C = c_ref[b, h]
            q_c = _mm(q_b, C.astype(BF16))[b:b + 1, :]
            nums.append(aw[b:b + 1, h:h + 1] * q_c)
            outer = _mm(jnp.where(lane == b, wk_t, jnp.zeros_like(wk_t)), v_rows)
            co_ref[b, h] = decay[b:b + 1, h:h + 1] * C + outer
        num = jnp.concatenate(nums, axis=0) + sw * v
        h_ref[:, hs] = (num * inv * og_ref[:, hs]).astype(h_ref.dtype)


def _mlstm_decode(mq, mk, mv, og, gif, state_c, state_n, state_m, bt):
    nbatch = mq.shape[0]
    b2 = lambda i: (i, 0)
    b3 = lambda i: (i, 0, 0)
    b4 = lambda i: (i, 0, 0, 0)
    return pl.pallas_call(
        _mlstm_decode_kernel,
        grid=(nbatch // bt,),
        in_specs=[
            pl.BlockSpec((bt, ML_W), b2),
            pl.BlockSpec((bt, ML_W), b2),
            pl.BlockSpec((bt, ML_W), b2),
            pl.BlockSpec((bt, ML_W), b2),
            pl.BlockSpec((bt, LANES), b2),
            pl.BlockSpec((bt, ML_HEADS, ML_DIM, ML_DIM), b4),
            pl.BlockSpec((bt, ML_HEADS, ML_DIM), b3),
            pl.BlockSpec((bt, ML_HEADS), b2),
        ],
        out_specs=(
            pl.BlockSpec((bt, ML_W), b2),
            pl.BlockSpec((bt, ML_HEADS, ML_DIM, ML_DIM), b4),
            pl.BlockSpec((bt, ML_HEADS, ML_DIM), b3),
            pl.BlockSpec((bt, ML_HEADS), b2),
        ),
        out_shape=(
            jax.ShapeDtypeStruct((nbatch, ML_W), BF16),
            jax.ShapeDtypeStruct((nbatch, ML_HEADS, ML_DIM, ML_DIM), F32),
            jax.ShapeDtypeStruct((nbatch, ML_HEADS, ML_DIM), F32),
            jax.ShapeDtypeStruct((nbatch, ML_HEADS), F32),
        ),
        compiler_params=pltpu.CompilerParams(dimension_semantics=("parallel",)),
        name="mlstm_decode",
    )(mq, mk, mv, og, gif, state_c, state_n, state_m)


def _rope_tables(pos):
    half = HEAD_DIM // 2
    inv = (1.0 / (ROPE_THETA ** (np.arange(half, dtype=np.float32) / half))).astype(np.float32)
    ang = pos.astype(np.float32)[:, None] * inv[None, :]
    cos = np.cos(ang)
    sin = np.sin(ang)
    cos_t = np.concatenate([cos, cos, cos, cos], axis=1)
    sin_t = np.concatenate([-sin, sin, -sin, sin], axis=1)
    return jnp.asarray(cos_t, F32), jnp.asarray(sin_t, F32)


def kernel(x_prompt, x_sample, cache_k, cache_v, state_C, state_n, state_m, p_prompt, p_sample, norm_mix, w_in, b_in, attn_sinks, w_attn_up, w_ml_up, w_o, norm_mlp, w_ff1, w_ff2, norm_ple, w_ple_gate, w_ple_proj, norm_final):
    assert w_in.shape[0] == 1, "single layer"
    batch, seq, _ = x_prompt.shape
    nbatch = x_sample.shape[0]
    assert x_sample.shape[1] == 1

    w_proj, w_gate, w_au, w_mu, w_out, w_f1, w_f2, w_pg, w_pp = _weight_cast(
        w_in[0].T, (w_attn_up[0], w_ml_up[0], w_o[0], w_ff1[0], w_ff2[0], w_ple_gate[0], w_ple_proj[0]))
    b_proj = b_in[:, :N_PROJ]
    b_gate = b_in[:, OFF_GA:]
    g_mix = norm_mix[0][None, :]
    out_weights = (
        g_mix, w_gate, b_gate, w_au, w_mu, w_out,
        norm_mlp[0][None, :], w_f1, w_f2,
        norm_ple[0][None, :], w_pg, w_pp, norm_final[None, :],
    )
    sinks = attn_sinks[0]
    to_fp = lambda c: c[0].transpose(0, 2, 3, 1).reshape(c.shape[1], ATTN_KV, WINDOW)
    from_fp = lambda c: c.reshape(c.shape[0], N_KV_HEADS, HEAD_DIM, WINDOW).transpose(0, 3, 1, 2)[None]

    cos_p, sin_p = _rope_tables(np.arange(seq, dtype=np.int32))
    xp = x_prompt.reshape(batch * seq, D_MODEL)
    y_prompt, k_last, v_last, cst, nst, mst = _prompt_fused(
        sinks, xp, p_prompt[0].reshape(batch * seq, D_PLE), cos_p, sin_p, g_mix, w_proj, b_proj, out_weights,
        batch, seq, tm=PROMPT_TILE)
    y_prompt = y_prompt.reshape(batch, seq, D_MODEL)
    k_prompt = from_fp(k_last)
    v_prompt = from_fp(v_last)
    c_prompt = cst[None]
    n_prompt = nst[:, :ML_HEADS, :][None]
    m_prompt = mst[:, :ML_HEADS, 0][None]

    cos_s, sin_s = _rope_tables(np.full((nbatch,), PAST_LEN, dtype=np.int32))
    xs = x_sample.reshape(nbatch, D_MODEL)
    qs, _, _, ks, vs, mqs, mks, mvs, ogs, gifs, _ = _in_proj_sample(xs, g_mix, w_proj, b_proj, cos_s, sin_s)
    qmat = qs.reshape(nbatch, N_Q_HEADS, ATTN_KV)
    o_full, k_s, v_s = _attn_decode(sinks, qmat, ks, vs, to_fp(cache_k), to_fp(cache_v), bt=DECODE_ROWS)
    o5 = o_full.reshape(nbatch, N_KV_HEADS, GQA_GROUP, N_KV_HEADS, HEAD_DIM)
    a_s = jnp.stack([o5[:, kv, :, kv, :] for kv in range(N_KV_HEADS)], axis=1).reshape(nbatch, ATTN_Q).astype(BF16)
    h_s, c_s, n_s, m_s = _mlstm_decode(mqs, mks, mvs, ogs, gifs, state_C[0], state_n[0], state_m[0], bt=DECODE_ROWS)
    y_sample = _out_proj_sample(xs, a_s, h_s, p_sample[0].reshape(nbatch, D_PLE), out_weights)
    y_sample = y_sample.reshape(nbatch, 1, D_MODEL)
    k_sample = from_fp(k_s)
    v_sample = from_fp(v_s)

    return (y_prompt, y_sample, k_prompt, v_prompt, c_prompt, n_prompt, m_prompt,
            k_sample, v_sample, c_s[None], n_s[None], m_s[None])
```

```python
import functools

import numpy as np
import jax
import jax.numpy as jnp
from jax import lax
from jax.experimental import pallas as pl
from jax.experimental.pallas import tpu as pltpu

F32 = jnp.float32
BF16 = jnp.bfloat16

D_MODEL = 1024
HEAD_DIM = 64
N_Q_HEADS = 8
N_KV_HEADS = 2
GQA_GROUP = N_Q_HEADS // N_KV_HEADS
WINDOW = 128
ROPE_THETA = 10000.0
ML_HEADS = 4
ML_DIM = 128
ML_CHUNK = 128
D_FF = 4 * D_MODEL
D_PLE = 256
EPS = 1e-6
PAST_LEN = 16384

ATTN_Q = N_Q_HEADS * HEAD_DIM
ATTN_KV = N_KV_HEADS * HEAD_DIM
ML_W = ML_HEADS * ML_DIM
LANES = 128

OFF_AQ = 0
OFF_AK = OFF_AQ + ATTN_Q
OFF_AV = OFF_AK + ATTN_KV
OFF_MQ = OFF_AV + ATTN_KV
OFF_MK = OFF_MQ + ML_W
OFF_MV = OFF_MK + ML_W
OFF_MO = OFF_MV + ML_W
OFF_MI = OFF_MO + ML_W
OFF_MF = OFF_MI + ML_HEADS
OFF_GA = OFF_MF + ML_HEADS
OFF_GM = OFF_GA + D_MODEL
D_IN = OFF_GM + D_MODEL
N_PROJ = OFF_MI + LANES

VMEM_LIMIT = 60 * 1024 * 1024
PROMPT_TILE = 256
DECODE_ROWS = 32

def _rms(x, g):
    r = lax.rsqrt(jnp.mean(x * x, axis=-1, keepdims=True) + EPS)
    return (x * r) * g


def _mm(a, b):
    return jnp.dot(a, b, preferred_element_type=F32)


def _log_sigmoid(x):
    return jnp.minimum(x, 0.0) - jnp.log1p(jnp.exp(-jnp.abs(x)))


def _weave(*streams):
    live = list(streams)
    while live:
        for g in list(live):
            try:
                next(g)
            except StopIteration:
                live.remove(g)


class _AwaitedRef:
    def __init__(self, ref, copy):
        self._ref, self._copy = ref, copy

    def __getitem__(self, idx):
        if self._copy is not None:
            self._copy.wait()
            self._copy = None
        return self._ref[idx]


WEIGHT_ROW_SPLIT = 8


def _weight_cast_kernel(*refs):
    n_plain = (len(refs) - 3) // 2
    w_in_t_ref, plain_in = refs[0], refs[1:1 + n_plain]
    w_proj_ref, w_gate_ref, plain_out = refs[1 + n_plain], refs[2 + n_plain], refs[3 + n_plain:]
    for g in range(N_PROJ // LANES):
        w_proj_ref[:, g * LANES:(g + 1) * LANES] = w_in_t_ref[g * LANES:(g + 1) * LANES, :].T.astype(BF16)
    for g in range(2 * D_MODEL // LANES):
        w_gate_ref[:, g * LANES:(g + 1) * LANES] = w_in_t_ref[OFF_GA + g * LANES:OFF_GA + (g + 1) * LANES, :].T.astype(BF16)
    for src, dst in zip(plain_in, plain_out):
        dst[...] = src[...].astype(BF16)


def _weight_cast(w_in_t, plain):
    split = WEIGHT_ROW_SPLIT
    assert D_MODEL // split == LANES
    rows_blk = lambda w: pl.BlockSpec((w.shape[0] // split, w.shape[1]), lambda i: (i, 0))
    out_shapes = (jax.ShapeDtypeStruct((D_MODEL, N_PROJ), BF16), jax.ShapeDtypeStruct((D_MODEL, 2 * D_MODEL), BF16),
                  *(jax.ShapeDtypeStruct(w.shape, BF16) for w in plain))
    return pl.pallas_call(
        _weight_cast_kernel,
        grid=(split,),
        in_specs=[pl.BlockSpec((D_IN, LANES), lambda i: (0, i))] + [rows_blk(w) for w in plain],
        out_specs=tuple(rows_blk(s) for s in out_shapes),
        out_shape=out_shapes,
        compiler_params=pltpu.CompilerParams(dimension_semantics=("parallel",), vmem_limit_bytes=VMEM_LIMIT),
        name="weight_cast",
    )(w_in_t, *plain)


def _in_proj_stages(x, g_ref, w_ref, b_ref, cos_ref, sin_ref,
                    q_ref, k_ref, v_ref, kl_ref, vl_ref, mq_ref, mk_ref, mv_ref, og_ref, gif_ref, gt_ref,
                    *, feature_major):
    xn = _rms(x, g_ref[...]).astype(BF16)
    tm = x.shape[0]

    def proj(lo, hi):
        return _mm(xn, w_ref[:, lo:hi]) + b_ref[:, lo:hi]

    cos = cos_ref[...]
    sin = sin_ref[...]
    lane = lax.broadcasted_iota(jnp.int32, (1, LANES), 1)
    first_half = (lane % HEAD_DIM) < (HEAD_DIM // 2)

    def rope(z):
        partner = jnp.where(first_half, pltpu.roll(z, LANES - HEAD_DIM // 2, 1), pltpu.roll(z, HEAD_DIM // 2, 1))
        return z * cos + partner * sin

    zq = proj(OFF_AQ, OFF_AK)
    for c in range(ATTN_Q // LANES):
        sl = slice(c * LANES, (c + 1) * LANES)
        qc = rope(zq[:, sl]) * (HEAD_DIM ** -0.5)
        if feature_major:
            q_ref[sl, :] = qc.T.astype(q_ref.dtype)
        else:
            kv = (2 * c) // GQA_GROUP
            own = (lane // HEAD_DIM) == kv
            swapped = pltpu.roll(qc, HEAD_DIM, 1)
            for j, src in enumerate((qc, swapped) if kv == 0 else (swapped, qc)):
                hd = 2 * c + j
                q_ref[:, hd * LANES:(hd + 1) * LANES] = jnp.where(own, src, 0.0).astype(q_ref.dtype)
    yield
    zkv = proj(OFF_AK, OFF_MQ)
    k = rope(zkv[:, :ATTN_KV])
    v = zkv[:, ATTN_KV:]
    k_ref[...] = k.astype(k_ref.dtype)
    v_ref[...] = (v.T if feature_major else v).astype(v_ref.dtype)
    if feature_major:
        kl_ref[0] = k[tm - WINDOW:, :].T
        vl_ref[0] = v[tm - WINDOW:, :].T
    else:
        kl_ref[...] = k[tm - WINDOW:, :]
        vl_ref[...] = v[tm - WINDOW:, :]
    yield
    for z_off, dst in ((OFF_MQ, mq_ref), (OFF_MV, mv_ref)):
        z = proj(z_off, z_off + ML_W)
        if feature_major:
            for h in range(ML_HEADS):
                hs = slice(h * ML_DIM, (h + 1) * ML_DIM)
                dst[0, hs, :] = z[:, hs].T.astype(dst.dtype)
        else:
            dst[...] = z.astype(dst.dtype)
        yield
    mk_ref[...] = (proj(OFF_MK, OFF_MV) * (ML_DIM ** -0.5)).astype(mk_ref.dtype)
    yield
    og_ref[...] = jax.nn.sigmoid(proj(OFF_MO, OFF_MI))
    yield
    zg = proj(OFF_MI, N_PROJ)
    gif = jnp.where(lane < ML_HEADS, zg, jnp.where(lane < 2 * ML_HEADS, _log_sigmoid(zg), 0.0))
    if feature_major:
        L = ML_CHUNK
        upper = (lax.broadcasted_iota(jnp.int32, (L, L), 0) <= lax.broadcasted_iota(jnp.int32, (L, L), 1)).astype(F32)
        row8 = lax.broadcasted_iota(jnp.int32, (8, 1), 0)
        fill = jnp.zeros((L - 16, L), F32)
        for c in range(tm // L):
            cs = slice(c * L, (c + 1) * L)
            g_rows = gif[cs, :].T[0:8, :]
            cum = jnp.dot(g_rows, upper, precision=lax.Precision.HIGHEST, preferred_element_type=F32)
            rows = jnp.where(row8 < ML_HEADS, g_rows, cum)
            gt_ref[0, :, cs] = rows
            diff = rows - pltpu.roll(rows, ML_HEADS, 0)
            gif_ref[cs, :] = jnp.concatenate([rows, diff, fill], axis=0).T
    else:
        gif_ref[...] = gif
        gt_ref[0] = jnp.zeros(gt_ref.shape[1:], F32)
    yield


def _in_proj_kernel(x_ref, *refs, feature_major):
    _weave(_in_proj_stages(x_ref[...], *refs, feature_major=feature_major))


def _in_proj_sample(x, g, w, b, cos, sin):
    rows = x.shape[0]
    assert rows == WINDOW
    full = lambda n, dt: (jax.ShapeDtypeStruct((rows, n), dt), pl.BlockSpec((rows, n), lambda i: (0, 0)))
    outs = (
        full(N_Q_HEADS * LANES, BF16), full(ATTN_KV, BF16), full(ATTN_KV, BF16), full(ATTN_KV, F32), full(ATTN_KV, F32),
        full(ML_W, F32), full(ML_W, F32), full(ML_W, F32), full(ML_W, F32), full(LANES, F32),
        (jax.ShapeDtypeStruct((1, 8, rows), F32), pl.BlockSpec((1, 8, rows), lambda i: (0, 0, 0))),
    )
    return pl.pallas_call(
        functools.partial(_in_proj_kernel, feature_major=False),
        grid=(1,),
        in_specs=[pl.BlockSpec(a.shape, lambda i: (0, 0)) for a in (x, g, w, b, cos, sin)],
        out_specs=tuple(o[1] for o in outs),
        out_shape=tuple(o[0] for o in outs),
        compiler_params=pltpu.CompilerParams(dimension_semantics=("arbitrary",), vmem_limit_bytes=VMEM_LIMIT),
        name="in_proj",
    )(x, g, w, b, cos, sin)


def _attn_stages(sink_ref, qt_ref, kp_ref, kc_ref, vtp_ref, vtc_ref, first_block, emit):
    L = WINDOW
    nsub = kc_ref.shape[0] // L
    key = lax.broadcasted_iota(jnp.int32, (2 * L, L), 0)
    t = lax.broadcasted_iota(jnp.int32, (2 * L, L), 1)
    is_prev = key < L
    allowed = (is_prev & (key >= t)) | (~is_prev & (key - L <= t))
    bias = jnp.where(allowed, 0.0, -jnp.inf)
    bias_first = jnp.where(is_prev & first_block, -jnp.inf, bias)
    zeros_q = jnp.zeros((HEAD_DIM, L), BF16)
    heads = range(N_Q_HEADS)
    for r in range(nsub):
        ts = slice(r * L, (r + 1) * L)
        k_prev = kp_ref[...] if r == 0 else kc_ref[(r - 1) * L:r * L, :]
        vt_prev = vtp_ref[...] if r == 0 else vtc_ref[:, (r - 1) * L:r * L]
        k2 = jnp.concatenate([k_prev, kc_ref[ts, :]], axis=0)
        vt2 = jnp.concatenate([vt_prev, vtc_ref[:, ts]], axis=1)
        b_r = bias_first if r == 0 else bias
        q_op = [jnp.concatenate([qt_ref[hd * HEAD_DIM:(hd + 1) * HEAD_DIM, ts], zeros_q][::1 if hd < GQA_GROUP else -1],
                                axis=0) for hd in heads]
        s = [_mm(k2, q_op[hd]) + b_r for hd in heads]
        yield
        m = [jnp.maximum(jnp.max(s[hd], axis=0, keepdims=True), sink_ref[hd]) for hd in heads]
        yield
        p = [jnp.exp(s[hd] - m[hd]) for hd in heads]
        yield
        den = [jnp.sum(p[hd], axis=0, keepdims=True) + jnp.exp(sink_ref[hd] - m[hd]) for hd in heads]
        o = [_mm(vt2[(hd // GQA_GROUP) * HEAD_DIM:(hd // GQA_GROUP + 1) * HEAD_DIM, :], p[hd].astype(BF16)) / den[hd]
             for hd in heads]
        yield
        emit(r, jnp.concatenate(o, axis=0).T)
        yield


ML_STATE_ROWS = ML_DIM + 8


def _mlstm_stages(chains, emit):
    L = ML_CHUNK
    s_idx = lax.broadcasted_iota(jnp.int32, (L, L), 0)
    t_idx = lax.broadcasted_iota(jnp.int32, (L, L), 1)
    causal = s_idx <= t_idx
    n = range(len(chains))
    dlog = [jnp.where(causal, ch["b_r"] + ch["u_c"], -jnp.inf) for ch in chains]
    yield
    qk = [_mm(ch["k"], ch["qt"]) for ch in chains]
    inter = [_mm(ch["state"].astype(BF16), ch["qt"]) for ch in chains]
    yield
    a = [ch["b_r"] + ch["m_prev"] for ch in chains]
    m_t = [jnp.maximum(a[i], jnp.max(dlog[i], axis=0, keepdims=True)) for i in n]
    yield
    sw = [qk[i] * jnp.exp(dlog[i] - m_t[i]) for i in n]
    yield
    sv = [_mm(chains[i]["vt"], sw[i].astype(BF16)) for i in n]
    yield
    hs = []
    for i in n:
        aw = jnp.exp(a[i] - m_t[i])
        num = aw * inter[i][:ML_DIM] + sv[i]
        den = aw * inter[i][ML_DIM:ML_DIM + 1] + jnp.sum(sw[i], axis=0, keepdims=True)
        hh = num / jnp.maximum(jnp.abs(den), jnp.exp(-m_t[i]))
        hs.append(hh.T * chains[i]["og"])
    yield
    pad = jnp.zeros((ML_STATE_ROWS - ML_DIM - 1, L), F32)
    new_state, new_m = [], []
    for i in n:
        ch = chains[i]
        b_end = ch["b_r"][:, L - 1:L]
        g = b_end - ch["b_r"] + ch["ig_r"]
        m_new = jnp.maximum(b_end + ch["m_prev"], jnp.max(g, axis=1, keepdims=True))
        decay = jnp.exp(b_end + ch["m_prev"] - m_new)
        w = jnp.exp(g - m_new)
        v_aug = jnp.concatenate([ch["vt"].astype(F32) * w, w, pad], axis=0).astype(BF16)
        new_state.append(decay * ch["state"] + _mm(v_aug, ch["k"]))
        new_m.append(m_new)
    emit(hs, new_state, new_m)
    yield


DENSE_COLS = 512


def _dense_stages(x, a, hm, p, gmix_ref, wg_ref, bg_ref, wau_ref, wmu_ref, wo_ref,
                  gmlp_ref, wff1_ref, wff2_ref, gple_ref, wpg_ref, wpp_ref, gfin_ref, emit):
    nc = DENSE_COLS
    pieces = lambda n: [slice(c, c + nc) for c in range(0, n, nc)]
    xn = _rms(x, gmix_ref[...]).astype(BF16)
    gates = []
    for cs in pieces(2 * D_MODEL):
        gates.append(jax.nn.sigmoid(_mm(xn, wg_ref[:, cs]) + bg_ref[:, cs]))
        yield
    half = D_MODEL // nc
    mix = []
    for j, cs in enumerate(pieces(D_MODEL)):
        mix.append((gates[j] * _mm(a, wau_ref[:, cs]) + gates[half + j] * _mm(hm, wmu_ref[:, cs])).astype(BF16))
        yield
    mix = jnp.concatenate(mix, axis=1)
    h = []
    for cs in pieces(D_MODEL):
        h.append(x[:, cs] + _mm(mix, wo_ref[:, cs]))
        yield
    h = jnp.concatenate(h, axis=1)
    hn = _rms(h, gmlp_ref[...]).astype(BF16)
    u = []
    for cs in pieces(D_FF):
        u.append(jnp.square(jnp.maximum(_mm(hn, wff1_ref[:, cs]), 0.0)).astype(BF16))
        yield
    u = jnp.concatenate(u, axis=1)
    h2 = []
    for cs in pieces(D_MODEL):
        h2.append(h[:, cs] + _mm(u, wff2_ref[:, cs]))
        yield
    h = jnp.concatenate(h2, axis=1)
    hn = _rms(h, gple_ref[...]).astype(BF16)
    pb = p.astype(BF16)
    h3 = []
    for cs in pieces(D_MODEL):
        h3.append(h[:, cs] + _mm(pb, wpp_ref[:, cs]) * jax.nn.sigmoid(_mm(hn, wpg_ref[:, cs])))
        yield
    emit(_rms(jnp.concatenate(h3, axis=1), gfin_ref[...]))
    yield


def _out_kernel(x_ref, a_ref, hm_ref, p_ref, *rest, matrix_idx):
    weights, y_ref = list(rest[:N_FUSED_WEIGHTS]), rest[N_FUSED_WEIGHTS]
    w_scr, dma_sem = rest[N_FUSED_WEIGHTS + 1:-1], rest[-1]
    copies = [pltpu.make_async_copy(weights[j], w_scr[n], dma_sem.at[n]) for n, j in enumerate(matrix_idx)]
    for c in copies:
        c.start()
    for n, j in enumerate(matrix_idx):
        weights[j] = _AwaitedRef(w_scr[n], copies[n])

    def emit(y):
        y_ref[...] = y

    _weave(_dense_stages(x_ref[...], a_ref[...], hm_ref[...], p_ref[...], *weights, emit=emit))


def _out_proj_sample(x, a, hm, p, weights):
    assert len(weights) == N_FUSED_WEIGHTS
    matrix_idx = tuple(j for j, w in enumerate(weights) if w.dtype == BF16)
    matrices = [weights[j] for j in matrix_idx]
    whole = lambda o: pl.BlockSpec(o.shape, lambda i: (0, 0), pipeline_mode=pl.Buffered(1))
    return pl.pallas_call(
        functools.partial(_out_kernel, matrix_idx=matrix_idx),
        grid=(1,),
        in_specs=[whole(o) for o in (x, a, hm, p)]
        + [pl.BlockSpec(memory_space=pl.ANY) if j in matrix_idx else whole(w) for j, w in enumerate(weights)],
        out_specs=pl.BlockSpec(x.shape, lambda i: (0, 0)),
        out_shape=jax.ShapeDtypeStruct(x.shape, F32),
        scratch_shapes=[*(pltpu.VMEM(w.shape, w.dtype) for w in matrices), pltpu.SemaphoreType.DMA((len(matrices),))],
        compiler_params=pltpu.CompilerParams(dimension_semantics=("arbitrary",), vmem_limit_bytes=VMEM_LIMIT),
        name="out_proj",
    )(x, a, hm, p, *weights)


N_FUSED_WEIGHTS = 13
N_PROJ_SLOTS = 9


def _prompt_kernel(sink_ref, x0_ref, cos_ref, sin_ref, x2_ref, p_ref, gmix_ref, wp_ref, bp_ref, *rest,
                   tiles_per_seq, matrix_idx):
    weights = list(rest[:N_FUSED_WEIGHTS])
    y_ref, kl_ref, vl_ref, cst_ref, nst_ref, mst_ref = rest[N_FUSED_WEIGHTS:N_FUSED_WEIGHTS + 6]
    scr = rest[N_FUSED_WEIGHTS + 6:]
    proj_scr = scr[:N_PROJ_SLOTS]
    kprev_scr, vtprev_scr, a_scr, h_scr, c_scr, m_scr = scr[N_PROJ_SLOTS:N_PROJ_SLOTS + 6]
    w_scr, dma_sem = scr[N_PROJ_SLOTS + 6:-1], scr[-1]
    w_copies = [pltpu.make_async_copy(weights[j], w_scr[n], dma_sem.at[n]) for n, j in enumerate(matrix_idx)]
    for n, j in enumerate(matrix_idx):
        weights[j] = w_scr[n]
    i = pl.program_id(0)
    n_tiles = pl.num_programs(0) - 2
    t1 = jnp.clip(i - 1, 0, n_tiles - 1)
    tile_in_seq = t1 % tiles_per_seq
    s0, s1 = i % 2, (i + 1) % 2
    L = ML_CHUNK
    tm = x0_ref.shape[0]

    @pl.when(i == 0)
    def _():
        for c in w_copies:
            c.start()
        for r in (*proj_scr, kprev_scr, vtprev_scr, a_scr, h_scr, c_scr, m_scr):
            r[...] = jnp.zeros_like(r)

    @pl.when(i == 2)
    def _():
        for c in w_copies:
            c.wait()

    @pl.when(tile_in_seq == 0)
    def _():
        c_scr[...] = jnp.zeros_like(c_scr)
        m_scr[...] = jnp.zeros_like(m_scr)

    def projection():
        outs = [r.at[s0] for r in proj_scr]
        yield from _in_proj_stages(x0_ref[...], gmix_ref, wp_ref, bp_ref, cos_ref, sin_ref,
                                   outs[0], outs[1], outs[2], kl_ref, vl_ref, *outs[3:], feature_major=True)

    def branches():
        qt_ref, kc_ref, vtc_ref, mqt_ref, mk_ref, mvt_ref, og_ref, gif_ref, gt_ref = [r.at[s1] for r in proj_scr]
        slot = s1

        def emit_attn(r, blk):
            a_scr[slot, r * WINDOW:(r + 1) * WINDOW, :] = blk.astype(a_scr.dtype)

        yield from _attn_stages(sink_ref, qt_ref, kprev_scr, kc_ref, vtprev_scr, vtc_ref, tile_in_seq == 0, emit_attn)
        kprev_scr[...] = kc_ref[tm - WINDOW:, :]
        vtprev_scr[...] = vtc_ref[:, tm - WINDOW:]
        hsl = lambda h: slice(h * ML_DIM, (h + 1) * ML_DIM)
        carry = dict(state=[c_scr[h] for h in range(ML_HEADS)], m=[m_scr[h:h + 1, :] for h in range(ML_HEADS)])
        for c in range(tm // L):
            cs = slice(c * L, (c + 1) * L)

            def emit_ml(hs, new_state, new_m, cs=cs):
                for h in range(ML_HEADS):
                    h_scr[slot, cs, hsl(h)] = hs[h].astype(h_scr.dtype)
                carry["state"], carry["m"] = new_state, new_m

            chains = [dict(qt=mqt_ref[0, hsl(h), cs], k=mk_ref[cs, hsl(h)], vt=mvt_ref[0, hsl(h), cs],
                           og=og_ref[cs, hsl(h)], ig_r=gt_ref[0, h:h + 1, cs],
                           b_r=gt_ref[0, ML_HEADS + h:ML_HEADS + h + 1, cs],
                           u_c=gif_ref[cs, 2 * ML_HEADS + h:2 * ML_HEADS + h + 1],
                           state=carry["state"][h], m_prev=carry["m"][h]) for h in range(ML_HEADS)]
            yield from _mlstm_stages(chains, emit_ml)
        for h in range(ML_HEADS):
            c_scr[h] = carry["state"][h]
            m_scr[h:h + 1, :] = carry["m"][h]

    def dense():
        def emit_y(y):
            y_ref[...] = y

        return _dense_stages(x2_ref[...], a_scr[s0], h_scr[s0], p_ref[...], *weights, emit=emit_y)

    @pl.when(i < 2)
    def _():
        _weave(branches(), projection())

    @pl.when((i >= 2) & (i < n_tiles))
    def _():
        _weave(branches(), projection(), dense())

    @pl.when(i >= n_tiles)
    def _():
        _weave(branches(), dense())

    @pl.when((tile_in_seq == tiles_per_seq - 1) & (i >= 1) & (i <= n_tiles))
    def _():
        nst_ref[...] = jnp.zeros_like(nst_ref)
        for h in range(ML_HEADS):
            final = c_scr[h]
            cst_ref[0, h] = final[:ML_DIM].T
            nst_ref[0, h:h + 1, :] = final[ML_DIM:ML_DIM + 1]
        mst_ref[0] = m_scr[...]


def _prompt_fused(sinks, x, p, cos, sin, g_mix, w_proj, b_proj, weights, batch, seq, tm):
    assert len(weights) == N_FUSED_WEIGHTS
    rows = batch * seq
    n_tiles = rows // tm
    tps = seq // tm
    t0 = lambda i: jnp.minimum(i, n_tiles - 1)
    t1 = lambda i: jnp.clip(i - 1, 0, n_tiles - 1)
    t2 = lambda i: jnp.maximum(i - 2, 0)
    const = lambda i: (0, 0)
    single = lambda a: pl.BlockSpec(a.shape, const, pipeline_mode=pl.Buffered(1))
    slot = lambda shape, dt: pltpu.VMEM((2,) + shape, dt)
    matrix_idx = tuple(j for j, w in enumerate(weights) if w.dtype == BF16)
    matrices = [weights[j] for j in matrix_idx]
    return pl.pallas_call(
        functools.partial(_prompt_kernel, tiles_per_seq=tps, matrix_idx=matrix_idx),
        grid=(n_tiles + 2,),
        in_specs=[
            pl.BlockSpec(memory_space=pltpu.SMEM),
            pl.BlockSpec((tm, D_MODEL), lambda i: (t0(i), 0)),
            pl.BlockSpec((tm, LANES), lambda i: (t0(i) % tps, 0)),
            pl.BlockSpec((tm, LANES), lambda i: (t0(i) % tps, 0)),
            pl.BlockSpec((tm, D_MODEL), lambda i: (t2(i), 0)),
            pl.BlockSpec((tm, D_PLE), lambda i: (t2(i), 0)),
            single(g_mix), single(w_proj), single(b_proj),
        ] + [pl.BlockSpec(memory_space=pl.ANY) if j in matrix_idx else single(w) for j, w in enumerate(weights)],
        out_specs=(
            pl.BlockSpec((tm, D_MODEL), lambda i: (t2(i), 0)),
            pl.BlockSpec((1, ATTN_KV, WINDOW), lambda i: (t0(i) // tps, 0, 0)),
            pl.BlockSpec((1, ATTN_KV, WINDOW), lambda i: (t0(i) // tps, 0, 0)),
            pl.BlockSpec((1, ML_HEADS, ML_DIM, ML_DIM), lambda i: (t1(i) // tps, 0, 0, 0)),
            pl.BlockSpec((1, 8, LANES), lambda i: (t1(i) // tps, 0, 0)),
            pl.BlockSpec((1, 8, LANES), lambda i: (t1(i) // tps, 0, 0)),
        ),
        out_shape=(
            jax.ShapeDtypeStruct((rows, D_MODEL), F32),
            jax.ShapeDtypeStruct((batch, ATTN_KV, WINDOW), F32),
            jax.ShapeDtypeStruct((batch, ATTN_KV, WINDOW), F32),
            jax.ShapeDtypeStruct((batch, ML_HEADS, ML_DIM, ML_DIM), F32),
            jax.ShapeDtypeStruct((batch, 8, LANES), F32),
            jax.ShapeDtypeStruct((batch, 8, LANES), F32),
        ),
        scratch_shapes=[
            slot((ATTN_Q, tm), BF16), slot((tm, ATTN_KV), BF16), slot((ATTN_KV, tm), BF16),
            slot((1, ML_W, tm), BF16), slot((tm, ML_W), BF16), slot((1, ML_W, tm), BF16),
            slot((tm, ML_W), F32), slot((tm, LANES), F32), slot((1, 8, tm), F32),
            pltpu.VMEM((WINDOW, ATTN_KV), BF16), pltpu.VMEM((ATTN_KV, WINDOW), BF16),
            slot((tm, ATTN_Q), BF16), slot((tm, ML_W), BF16),
            pltpu.VMEM((ML_HEADS, ML_STATE_ROWS, ML_DIM), F32), pltpu.VMEM((8, LANES), F32),
            *(pltpu.VMEM(w.shape, w.dtype) for w in matrices), pltpu.SemaphoreType.DMA((len(matrices),)),
        ],
        compiler_params=pltpu.CompilerParams(dimension_semantics=("arbitrary",), vmem_limit_bytes=VMEM_LIMIT),
        name="prompt_fused",
    )(sinks, x, cos, sin, x, p, g_mix, w_proj, b_proj, *weights)


def _attn_decode_kernel(sink_ref, qm_ref, kn_ref, vn_ref, ck_ref, cv_ref, o_ref, ko_ref, vo_ref):
    bt = qm_ref.shape[0]
    qm = qm_ref[...]
    kn = kn_ref[...]
    vn = vn_ref[...]
    ck = ck_ref[...]
    cv = cv_ref[...]
    head = lax.broadcasted_iota(jnp.int32, (N_Q_HEADS, 1), 0)
    sink = jnp.zeros((N_Q_HEADS, 1), F32)
    for hd in range(N_Q_HEADS):
        sink = jnp.where(head == hd, sink_ref[hd], sink)
    s = jnp.einsum('bhc,bck->bhk', qm, ck.astype(BF16), preferred_element_type=F32)
    s_new = jnp.sum(qm.astype(F32) * kn[:, None, :], axis=2, keepdims=True)
    m = jnp.maximum(jnp.maximum(jnp.max(s, axis=2, keepdims=True), s_new), sink)
    p = jnp.exp(s - m)
    p_new = jnp.exp(s_new - m)
    den = jnp.sum(p, axis=2, keepdims=True) + p_new + jnp.exp(sink - m)
    o = jnp.einsum('bhk,bck->bhc', p.astype(BF16), cv.astype(BF16), preferred_element_type=F32)
    o_ref[...] = (o + p_new * vn[:, None, :]) / den
    pad = jnp.zeros((LANES - bt, ATTN_KV), F32)
    kn_t = jnp.concatenate([kn, pad], axis=0).T
    vn_t = jnp.concatenate([vn, pad], axis=0).T
    newest = lax.broadcasted_iota(jnp.int32, (ATTN_KV, WINDOW), 1) == WINDOW - 1
    for b in range(bt):
        ko_ref[b] = jnp.where(newest, kn_t[:, b:b + 1], pltpu.roll(ck[b], WINDOW - 1, 1))
        vo_ref[b] = jnp.where(newest, vn_t[:, b:b + 1], pltpu.roll(cv[b], WINDOW - 1, 1))


def _attn_decode(sinks, qmat, k_new, v_new, cache_k, cache_v, bt):
    nbatch = qmat.shape[0]
    b3 = lambda i: (i, 0, 0)
    b2 = lambda i: (i, 0)
    return pl.pallas_call(
        _attn_decode_kernel,
        grid=(nbatch // bt,),
        in_specs=[
            pl.BlockSpec(memory_space=pltpu.SMEM),
            pl.BlockSpec((bt, N_Q_HEADS, ATTN_KV), b3),
            pl.BlockSpec((bt, ATTN_KV), b2),
            pl.BlockSpec((bt, ATTN_KV), b2),
            pl.BlockSpec((bt, WINDOW, ATTN_KV), b3),
            pl.BlockSpec((bt, WINDOW, ATTN_KV), b3),
        ],
        out_specs=(
            pl.BlockSpec((bt, N_Q_HEADS, ATTN_KV), b3),
            pl.BlockSpec((bt, WINDOW, ATTN_KV), b3),
            pl.BlockSpec((bt, WINDOW, ATTN_KV), b3),
        ),
        out_shape=(
            jax.ShapeDtypeStruct((nbatch, N_Q_HEADS, ATTN_KV), F32),
            jax.ShapeDtypeStruct((nbatch, WINDOW, ATTN_KV), F32),
            jax.ShapeDtypeStruct((nbatch, WINDOW, ATTN_KV), F32),
        ),
        compiler_params=pltpu.CompilerParams(dimension_semantics=("parallel",)),
        name="attn_decode",
    )(sinks, qmat, k_new, v_new, cache_k, cache_v)


def _mlstm_decode_kernel(q_ref, k_ref, v_ref, og_ref, gif_ref, c_ref, n_ref, m_ref, h_ref, co_ref, no_ref, mo_ref):
    bt = q_ref.shape[0]
    gif = gif_ref[...]
    ig = gif[:, 0:ML_HEADS]
    lf = gif[:, ML_HEADS:2 * ML_HEADS]
    m_prev = m_ref[...]
    a = lf + m_prev
    m_t = jnp.maximum(a, ig)
    aw = jnp.exp(a - m_t)
    e_i = jnp.exp(ig - m_t)
    floor = jnp.exp(-m_t)
    m_new = jnp.maximum(a, ig)
    decay = jnp.exp(a - m_new)
    w = jnp.exp(ig - m_new)
    mo_ref[...] = m_new
    pad = jnp.zeros((LANES - bt, ML_DIM), F32)
    lane = lax.broadcasted_iota(jnp.int32, (ML_DIM, LANES), 1)
    for h in range(ML_HEADS):
        hs = slice(h * ML_DIM, (h + 1) * ML_DIM)
        q = q_ref[:, hs]
        k = k_ref[:, hs]
        v = v_ref[:, hs]
        n = n_ref[:, h, :]
        qk = jnp.sum(q * k, axis=1, keepdims=True)
        qn = jnp.sum(q * n, axis=1, keepdims=True)
        sw = qk * e_i[:, h:h + 1]
        den = aw[:, h:h + 1] * qn + sw
        inv = 1.0 / jnp.maximum(jnp.abs(den), floor[:, h:h + 1])
        no_ref[:, h, :] = decay[:, h:h + 1] * n + w[:, h:h + 1] * k
        q_b = q.astype(BF16)
        wk_t = jnp.concatenate([w[:, h:h + 1] * k, pad], axis=0).T.astype(BF16)
        v_rows = jnp.concatenate([v, pad], axis=0).astype(BF16)
        nums = []
        for b in range(bt):
            C = c_ref[b, h]
            q_c = _mm(q_b, C.astype(BF16))[b:b + 1, :]
            nums.append(aw[b:b + 1, h:h + 1] * q_c)
            outer = _mm(jnp.where(lane == b, wk_t, jnp.zeros_like(wk_t)), v_rows)
            co_ref[b, h] = decay[b:b + 1, h:h + 1] * C + outer
        num = jnp.concatenate(nums, axis=0) + sw * v
        h_ref[:, hs] = (num * inv * og_ref[:, hs]).astype(h_ref.dtype)


def _mlstm_decode(mq, mk, mv, og, gif, state_c, state_n, state_m, bt):
    nbatch = mq.shape[0]
    b2 = lambda i: (i, 0)
    b3 = lambda i: (i, 0, 0)
    b4 = lambda i: (i, 0, 0, 0)
    return pl.pallas_call(
        _mlstm_decode_kernel,
        grid=(nbatch // bt,),
        in_specs=[
            pl.BlockSpec((bt, ML_W), b2),
            pl.BlockSpec((bt, ML_W), b2),
            pl.BlockSpec((bt, ML_W), b2),
            pl.BlockSpec((bt, ML_W), b2),
            pl.BlockSpec((bt, LANES), b2),
            pl.BlockSpec((bt, ML_HEADS, ML_DIM, ML_DIM), b4),
            pl.BlockSpec((bt, ML_HEADS, ML_DIM), b3),
            pl.BlockSpec((bt, ML_HEADS), b2),
        ],
        out_specs=(
            pl.BlockSpec((bt, ML_W), b2),
            pl.BlockSpec((bt, ML_HEADS, ML_DIM, ML_DIM), b4),
            pl.BlockSpec((bt, ML_HEADS, ML_DIM), b3),
            pl.BlockSpec((bt, ML_HEADS), b2),
        ),
        out_shape=(
            jax.ShapeDtypeStruct((nbatch, ML_W), BF16),
            jax.ShapeDtypeStruct((nbatch, ML_HEADS, ML_DIM, ML_DIM), F32),
            jax.ShapeDtypeStruct((nbatch, ML_HEADS, ML_DIM), F32),
            jax.ShapeDtypeStruct((nbatch, ML_HEADS), F32),
        ),
        compiler_params=pltpu.CompilerParams(dimension_semantics=("parallel",)),
        name="mlstm_decode",
    )(mq, mk, mv, og, gif, state_c, state_n, state_m)


def _rope_tables(pos):
    half = HEAD_DIM // 2
    inv = (1.0 / (ROPE_THETA ** (np.arange(half, dtype=np.float32) / half))).astype(np.float32)
    ang = pos.astype(np.float32)[:, None] * inv[None, :]
    cos = np.cos(ang)
    sin = np.sin(ang)
    cos_t = np.concatenate([cos, cos, cos, cos], axis=1)
    sin_t = np.concatenate([-sin, sin, -sin, sin], axis=1)
    return jnp.asarray(cos_t, F32), jnp.asarray(sin_t, F32)


def kernel(x_prompt, x_sample, cache_k, cache_v, state_C, state_n, state_m, p_prompt, p_sample, norm_mix, w_in, b_in, attn_sinks, w_attn_up, w_ml_up, w_o, norm_mlp, w_ff1, w_ff2, norm_ple, w_ple_gate, w_ple_proj, norm_final):
    assert w_in.shape[0] == 1, "single layer"
    batch, seq, _ = x_prompt.shape
    nbatch = x_sample.shape[0]
    assert x_sample.shape[1] == 1

    w_proj, w_gate, w_au, w_mu, w_out, w_f1, w_f2, w_pg, w_pp = _weight_cast(
        w_in[0].T, (w_attn_up[0], w_ml_up[0], w_o[0], w_ff1[0], w_ff2[0], w_ple_gate[0], w_ple_proj[0]))
    b_proj = b_in[:, :N_PROJ]
    b_gate = b_in[:, OFF_GA:]
    g_mix = norm_mix[0][None, :]
    out_weights = (
        g_mix, w_gate, b_gate, w_au, w_mu, w_out,
        norm_mlp[0][None, :], w_f1, w_f2,
        norm_ple[0][None, :], w_pg, w_pp, norm_final[None, :],
    )
    sinks = attn_sinks[0]
    to_fp = lambda c: c[0].transpose(0, 2, 3, 1).reshape(c.shape[1], ATTN_KV, WINDOW)
    from_fp = lambda c: c.reshape(c.shape[0], N_KV_HEADS, HEAD_DIM, WINDOW).transpose(0, 3, 1, 2)[None]

    cos_p, sin_p = _rope_tables(np.arange(seq, dtype=np.int32))
    xp = x_prompt.reshape(batch * seq, D_MODEL)
    y_prompt, k_last, v_last, cst, nst, mst = _prompt_fused(
        sinks, xp, p_prompt[0].reshape(batch * seq, D_PLE), cos_p, sin_p, g_mix, w_proj, b_proj, out_weights,
        batch, seq, tm=PROMPT_TILE)
    y_prompt = y_prompt.reshape(batch, seq, D_MODEL)
    k_prompt = from_fp(k_last)
    v_prompt = from_fp(v_last)
    c_prompt = cst[None]
    n_prompt = nst[:, :ML_HEADS, :][None]
    m_prompt = mst[:, :ML_HEADS, 0][None]

    cos_s, sin_s = _rope_tables(np.full((nbatch,), PAST_LEN, dtype=np.int32))
    xs = x_sample.reshape(nbatch, D_MODEL)
    qs, _, _, ks, vs, mqs, mks, mvs, ogs, gifs, _ = _in_proj_sample(xs, g_mix, w_proj, b_proj, cos_s, sin_s)
    qmat = qs.reshape(nbatch, N_Q_HEADS, ATTN_KV)
    o_full, k_s, v_s = _attn_decode(sinks, qmat, ks, vs, to_fp(cache_k), to_fp(cache_v), bt=DECODE_ROWS)
    o5 = o_full.reshape(nbatch, N_KV_HEADS, GQA_GROUP, N_KV_HEADS, HEAD_DIM)
    a_s = jnp.stack([o5[:, kv, :, kv, :] for kv in range(N_KV_HEADS)], axis=1).reshape(nbatch, ATTN_Q).astype(BF16)
    h_s, c_s, n_s, m_s = _mlstm_decode(mqs, mks, mvs, ogs, gifs, state_C[0], state_n[0], state_m[0], bt=DECODE_ROWS)
    y_sample = _out_proj_sample(xs, a_s, h_s, p_sample[0].reshape(nbatch, D_PLE), out_weights)
    y_sample = y_sample.reshape(nbatch, 1, D_MODEL)
    k_sample = from_fp(k_s)
    v_sample = from_fp(v_s)

    return (y_prompt, y_sample, k_prompt, v_prompt, c_prompt, n_prompt, m_prompt,
            k_sample, v_sample, c_s[None], n_s[None], m_s[None])
```

```python
import functools

import numpy as np
import jax
import jax.numpy as jnp
from jax import lax
from jax.experimental import pallas as pl
from jax.experimental.pallas import tpu as pltpu

F32 = jnp.float32
BF16 = jnp.bfloat16

D_MODEL = 1024
HEAD_DIM = 64
N_Q_HEADS = 8
N_KV_HEADS = 2
GQA_GROUP = N_Q_HEADS // N_KV_HEADS
WINDOW = 128
ROPE_THETA = 10000.0
ML_HEADS = 4
ML_DIM = 128
ML_CHUNK = 128
D_FF = 4 * D_MODEL
D_PLE = 256
EPS = 1e-6
PAST_LEN = 16384

ATTN_Q = N_Q_HEADS * HEAD_DIM
ATTN_KV = N_KV_HEADS * HEAD_DIM
ML_W = ML_HEADS * ML_DIM
LANES = 128

OFF_AQ = 0
OFF_AK = OFF_AQ + ATTN_Q
OFF_AV = OFF_AK + ATTN_KV
OFF_MQ = OFF_AV + ATTN_KV
OFF_MK = OFF_MQ + ML_W
OFF_MV = OFF_MK + ML_W
OFF_MO = OFF_MV + ML_W
OFF_MI = OFF_MO + ML_W
OFF_MF = OFF_MI + ML_HEADS
OFF_GA = OFF_MF + ML_HEADS
OFF_GM = OFF_GA + D_MODEL
D_IN = OFF_GM + D_MODEL
N_PROJ = OFF_MI + LANES

VMEM_LIMIT = 60 * 1024 * 1024
PROMPT_TILE = 256
DECODE_ROWS = 32

def _rms(x, g):
    r = lax.rsqrt(jnp.mean(x * x, axis=-1, keepdims=True) + EPS)
    return (x * r) * g


def _mm(a, b):
    return jnp.dot(a, b, preferred_element_type=F32)


def _log_sigmoid(x):
    return jnp.minimum(x, 0.0) - jnp.log1p(jnp.exp(-jnp.abs(x)))


def _weave(*streams):
    live = list(streams)
    while live:
        for g in list(live):
            try:
                next(g)
            except StopIteration:
                live.remove(g)


def _spaced(stream, every):
    for _ in stream:
        yield
        for _ in range(every - 1):
            yield


class _AwaitedRef:
    def __init__(self, ref, copy):
        self._ref, self._copy = ref, copy

    def __getitem__(self, idx):
        if self._copy is not None:
            self._copy.wait()
            self._copy = None
        return self._ref[idx]


WEIGHT_ROW_SPLIT = 8


def _weight_cast_kernel(*refs):
    n_plain = (len(refs) - 3) // 2
    w_in_t_ref, plain_in = refs[0], refs[1:1 + n_plain]
    w_proj_ref, w_gate_ref, plain_out = refs[1 + n_plain], refs[2 + n_plain], refs[3 + n_plain:]
    for g in range(N_PROJ // LANES):
        w_proj_ref[:, g * LANES:(g + 1) * LANES] = w_in_t_ref[g * LANES:(g + 1) * LANES, :].T.astype(BF16)
    for g in range(2 * D_MODEL // LANES):
        w_gate_ref[:, g * LANES:(g + 1) * LANES] = w_in_t_ref[OFF_GA + g * LANES:OFF_GA + (g + 1) * LANES, :].T.astype(BF16)
    for src, dst in zip(plain_in, plain_out):
        dst[...] = src[...].astype(BF16)


def _weight_cast(w_in_t, plain):
    split = WEIGHT_ROW_SPLIT
    assert D_MODEL // split == LANES
    rows_blk = lambda w: pl.BlockSpec((w.shape[0] // split, w.shape[1]), lambda i: (i, 0))
    out_shapes = (jax.ShapeDtypeStruct((D_MODEL, N_PROJ), BF16), jax.ShapeDtypeStruct((D_MODEL, 2 * D_MODEL), BF16),
                  *(jax.ShapeDtypeStruct(w.shape, BF16) for w in plain))
    return pl.pallas_call(
        _weight_cast_kernel,
        grid=(split,),
        in_specs=[pl.BlockSpec((D_IN, LANES), lambda i: (0, i))] + [rows_blk(w) for w in plain],
        out_specs=tuple(rows_blk(s) for s in out_shapes),
        out_shape=out_shapes,
        compiler_params=pltpu.CompilerParams(dimension_semantics=("parallel",), vmem_limit_bytes=VMEM_LIMIT),
        name="weight_cast",
    )(w_in_t, *plain)


def _in_proj_stages(x, g_ref, w_ref, b_ref, cos_ref, sin_ref,
                    q_ref, k_ref, v_ref, kl_ref, vl_ref, mq_ref, mk_ref, mv_ref, og_ref, gif_ref, gt_ref,
                    *, feature_major):
    xn = _rms(x, g_ref[...]).astype(BF16)
    tm = x.shape[0]

    def proj(lo, hi):
        return _mm(xn, w_ref[:, lo:hi]) + b_ref[:, lo:hi]

    cos = cos_ref[...]
    sin = sin_ref[...]
    lane = lax.broadcasted_iota(jnp.int32, (1, LANES), 1)
    first_half = (lane % HEAD_DIM) < (HEAD_DIM // 2)

    def rope(z):
        partner = jnp.where(first_half, pltpu.roll(z, LANES - HEAD_DIM // 2, 1), pltpu.roll(z, HEAD_DIM // 2, 1))
        return z * cos + partner * sin

    zq = proj(OFF_AQ, OFF_AK)
    for c in range(ATTN_Q // LANES):
        sl = slice(c * LANES, (c + 1) * LANES)
        qc = rope(zq[:, sl]) * (HEAD_DIM ** -0.5)
        if feature_major:
            q_ref[sl, :] = qc.T.astype(q_ref.dtype)
        else:
            kv = (2 * c) // GQA_GROUP
            own = (lane // HEAD_DIM) == kv
            swapped = pltpu.roll(qc, HEAD_DIM, 1)
            for j, src in enumerate((qc, swapped) if kv == 0 else (swapped, qc)):
                hd = 2 * c + j
                q_ref[:, hd * LANES:(hd + 1) * LANES] = jnp.where(own, src, 0.0).astype(q_ref.dtype)
    yield
    zkv = proj(OFF_AK, OFF_MQ)
    k = rope(zkv[:, :ATTN_KV])
    v = zkv[:, ATTN_KV:]
    k_ref[...] = k.astype(k_ref.dtype)
    v_ref[...] = (v.T if feature_major else v).astype(v_ref.dtype)
    if feature_major:
        kl_ref[0] = k[tm - WINDOW:, :].T
        vl_ref[0] = v[tm - WINDOW:, :].T
    else:
        kl_ref[...] = k[tm - WINDOW:, :]
        vl_ref[...] = v[tm - WINDOW:, :]
    yield
    for z_off, dst in ((OFF_MQ, mq_ref), (OFF_MV, mv_ref)):
        z = proj(z_off, z_off + ML_W)
        if feature_major:
            for h in range(ML_HEADS):
                hs = slice(h * ML_DIM, (h + 1) * ML_DIM)
                dst[0, hs, :] = z[:, hs].T.astype(dst.dtype)
        else:
            dst[...] = z.astype(dst.dtype)
        yield
    mk_ref[...] = (proj(OFF_MK, OFF_MV) * (ML_DIM ** -0.5)).astype(mk_ref.dtype)
    yield
    og_ref[...] = jax.nn.sigmoid(proj(OFF_MO, OFF_MI))
    yield
    zg = proj(OFF_MI, N_PROJ)
    gif = jnp.where(lane < ML_HEADS, zg, jnp.where(lane < 2 * ML_HEADS, _log_sigmoid(zg), 0.0))
    if feature_major:
        L = ML_CHUNK
        upper = (lax.broadcasted_iota(jnp.int32, (L, L), 0) <= lax.broadcasted_iota(jnp.int32, (L, L), 1)).astype(F32)
        row8 = lax.broadcasted_iota(jnp.int32, (8, 1), 0)
        fill = jnp.zeros((L - 16, L), F32)
        for c in range(tm // L):
            cs = slice(c * L, (c + 1) * L)
            g_rows = gif[cs, :].T[0:8, :]
            cum = jnp.dot(g_rows, upper, precision=lax.Precision.HIGHEST, preferred_element_type=F32)
            rows = jnp.where(row8 < ML_HEADS, g_rows, cum)
            gt_ref[0, :, cs] = rows
            diff = rows - pltpu.roll(rows, ML_HEADS, 0)
            gif_ref[cs, :] = jnp.concatenate([rows, diff, fill], axis=0).T
    else:
        gif_ref[...] = gif
        gt_ref[0] = jnp.zeros(gt_ref.shape[1:], F32)
    yield


def _in_proj_kernel(x_ref, *refs, feature_major):
    _weave(_in_proj_stages(x_ref[...], *refs, feature_major=feature_major))


def _in_proj_sample(x, g, w, b, cos, sin):
    rows = x.shape[0]
    assert rows == WINDOW
    full = lambda n, dt: (jax.ShapeDtypeStruct((rows, n), dt), pl.BlockSpec((rows, n), lambda i: (0, 0)))
    outs = (
        full(N_Q_HEADS * LANES, BF16), full(ATTN_KV, BF16), full(ATTN_KV, BF16), full(ATTN_KV, F32), full(ATTN_KV, F32),
        full(ML_W, F32), full(ML_W, F32), full(ML_W, F32), full(ML_W, F32), full(LANES, F32),
        (jax.ShapeDtypeStruct((1, 8, rows), F32), pl.BlockSpec((1, 8, rows), lambda i: (0, 0, 0))),
    )
    return pl.pallas_call(
        functools.partial(_in_proj_kernel, feature_major=False),
        grid=(1,),
        in_specs=[pl.BlockSpec(a.shape, lambda i: (0, 0)) for a in (x, g, w, b, cos, sin)],
        out_specs=tuple(o[1] for o in outs),
        out_shape=tuple(o[0] for o in outs),
        compiler_params=pltpu.CompilerParams(dimension_semantics=("arbitrary",), vmem_limit_bytes=VMEM_LIMIT),
        name="in_proj",
    )(x, g, w, b, cos, sin)


def _attn_stages(sink_ref, qt_ref, kp_ref, kc_ref, vtp_ref, vtc_ref, first_block, emit):
    L = WINDOW
    nsub = kc_ref.shape[0] // L
    key = lax.broadcasted_iota(jnp.int32, (2 * L, L), 0)
    t = lax.broadcasted_iota(jnp.int32, (2 * L, L), 1)
    is_prev = key < L
    allowed = (is_prev & (key >= t)) | (~is_prev & (key - L <= t))
    bias = jnp.where(allowed, 0.0, -jnp.inf)
    bias_first = jnp.where(is_prev & first_block, -jnp.inf, bias)
    zeros_q = jnp.zeros((HEAD_DIM, L), BF16)
    heads = range(N_Q_HEADS)
    for r in range(nsub):
        ts = slice(r * L, (r + 1) * L)
        k_prev = kp_ref[...] if r == 0 else kc_ref[(r - 1) * L:r * L, :]
        vt_prev = vtp_ref[...] if r == 0 else vtc_ref[:, (r - 1) * L:r * L]
        k2 = jnp.concatenate([k_prev, kc_ref[ts, :]], axis=0)
        vt2 = jnp.concatenate([vt_prev, vtc_ref[:, ts]], axis=1)
        b_r = bias_first if r == 0 else bias
        q_op = [jnp.concatenate([qt_ref[hd * HEAD_DIM:(hd + 1) * HEAD_DIM, ts], zeros_q][::1 if hd < GQA_GROUP else -1],
                                axis=0) for hd in heads]
        s = [_mm(k2, q_op[hd]) + b_r for hd in heads]
        yield
        m = [jnp.maximum(jnp.max(s[hd], axis=0, keepdims=True), sink_ref[hd]) for hd in heads]
        yield
        p = [jnp.exp(s[hd] - m[hd]) for hd in heads]
        yield
        den = [jnp.sum(p[hd], axis=0, keepdims=True) + jnp.exp(sink_ref[hd] - m[hd]) for hd in heads]
        o = [_mm(vt2[(hd // GQA_GROUP) * HEAD_DIM:(hd // GQA_GROUP + 1) * HEAD_DIM, :], p[hd].astype(BF16)) / den[hd]
             for hd in heads]
        yield
        emit(r, jnp.concatenate(o, axis=0).T)
        yield


ML_STATE_ROWS = ML_DIM + 8


def _mlstm_stages(chains, emit):
    L = ML_CHUNK
    s_idx = lax.broadcasted_iota(jnp.int32, (L, L), 0)
    t_idx = lax.broadcasted_iota(jnp.int32, (L, L), 1)
    causal = s_idx <= t_idx
    n = range(len(chains))
    dlog = [jnp.where(causal, ch["b_r"] + ch["u_c"], -jnp.inf) for ch in chains]
    yield
    qk = [_mm(ch["k"], ch["qt"]) for ch in chains]
    inter = [_mm(ch["state"].astype(BF16), ch["qt"]) for ch in chains]
    yield
    a = [ch["b_r"] + ch["m_prev"] for ch in chains]
    m_t = [jnp.maximum(a[i], jnp.max(dlog[i], axis=0, keepdims=True)) for i in n]
    yield
    sw = [qk[i] * jnp.exp(dlog[i] - m_t[i]) for i in n]
    yield
    sv = [_mm(chains[i]["vt"], sw[i].astype(BF16)) for i in n]
    yield
    hs = []
    for i in n:
        aw = jnp.exp(a[i] - m_t[i])
        num = aw * inter[i][:ML_DIM] + sv[i]
        den = aw * inter[i][ML_DIM:ML_DIM + 1] + jnp.sum(sw[i], axis=0, keepdims=True)
        hh = num / jnp.maximum(jnp.abs(den), jnp.exp(-m_t[i]))
        hs.append(hh.T * chains[i]["og"])
    yield
    pad = jnp.zeros((ML_STATE_ROWS - ML_DIM - 1, L), F32)
    new_state, new_m = [], []
    for i in n:
        ch = chains[i]
        b_end = ch["b_r"][:, L - 1:L]
        g = b_end - ch["b_r"] + ch["ig_r"]
        m_new = jnp.maximum(b_end + ch["m_prev"], jnp.max(g, axis=1, keepdims=True))
        decay = jnp.exp(b_end + ch["m_prev"] - m_new)
        w = jnp.exp(g - m_new)
        v_aug = jnp.concatenate([ch["vt"].astype(F32) * w, w, pad], axis=0).astype(BF16)
        new_state.append(decay * ch["state"] + _mm(v_aug, ch["k"]))
        new_m.append(m_new)
    emit(hs, new_state, new_m)
    yield


DENSE_COLS = 512


def _dense_stages(x, a, hm, p, gmix_ref, wg_ref, bg_ref, wau_ref, wmu_ref, wo_ref,
                  gmlp_ref, wff1_ref, wff2_ref, gple_ref, wpg_ref, wpp_ref, gfin_ref, emit):
    nc = DENSE_COLS
    pieces = lambda n: [slice(c, c + nc) for c in range(0, n, nc)]
    xn = _rms(x, gmix_ref[...]).astype(BF16)
    gates = []
    for cs in pieces(2 * D_MODEL):
        gates.append(jax.nn.sigmoid(_mm(xn, wg_ref[:, cs]) + bg_ref[:, cs]))
        yield
    half = D_MODEL // nc
    mix = []
    for j, cs in enumerate(pieces(D_MODEL)):
        mix.append((gates[j] * _mm(a, wau_ref[:, cs]) + gates[half + j] * _mm(hm, wmu_ref[:, cs])).astype(BF16))
        yield
    mix = jnp.concatenate(mix, axis=1)
    h = []
    for cs in pieces(D_MODEL):
        h.append(x[:, cs] + _mm(mix, wo_ref[:, cs]))
        yield
    h = jnp.concatenate(h, axis=1)
    hn = _rms(h, gmlp_ref[...]).astype(BF16)
    u = []
    for cs in pieces(D_FF):
        u.append(jnp.square(jnp.maximum(_mm(hn, wff1_ref[:, cs]), 0.0)).astype(BF16))
        yield
    u = jnp.concatenate(u, axis=1)
    h2 = []
    for cs in pieces(D_MODEL):
        h2.append(h[:, cs] + _mm(u, wff2_ref[:, cs]))
        yield
    h = jnp.concatenate(h2, axis=1)
    hn = _rms(h, gple_ref[...]).astype(BF16)
    pb = p.astype(BF16)
    h3 = []
    for cs in pieces(D_MODEL):
        h3.append(h[:, cs] + _mm(pb, wpp_ref[:, cs]) * jax.nn.sigmoid(_mm(hn, wpg_ref[:, cs])))
        yield
    emit(_rms(jnp.concatenate(h3, axis=1), gfin_ref[...]))
    yield


def _out_kernel(x_ref, a_ref, hm_ref, p_ref, *rest, matrix_idx):
    weights, y_ref = list(rest[:N_FUSED_WEIGHTS]), rest[N_FUSED_WEIGHTS]
    w_scr, dma_sem = rest[N_FUSED_WEIGHTS + 1:-1], rest[-1]
    copies = [pltpu.make_async_copy(weights[j], w_scr[n], dma_sem.at[n]) for n, j in enumerate(matrix_idx)]
    for c in copies:
        c.start()
    for n, j in enumerate(matrix_idx):
        weights[j] = _AwaitedRef(w_scr[n], copies[n])

    def emit(y):
        y_ref[...] = y

    _weave(_dense_stages(x_ref[...], a_ref[...], hm_ref[...], p_ref[...], *weights, emit=emit))


def _out_proj_sample(x, a, hm, p, weights):
    assert len(weights) == N_FUSED_WEIGHTS
    matrix_idx = tuple(j for j, w in enumerate(weights) if w.dtype == BF16)
    matrices = [weights[j] for j in matrix_idx]
    whole = lambda o: pl.BlockSpec(o.shape, lambda i: (0, 0), pipeline_mode=pl.Buffered(1))
    return pl.pallas_call(
        functools.partial(_out_kernel, matrix_idx=matrix_idx),
        grid=(1,),
        in_specs=[whole(o) for o in (x, a, hm, p)]
        + [pl.BlockSpec(memory_space=pl.ANY) if j in matrix_idx else whole(w) for j, w in enumerate(weights)],
        out_specs=pl.BlockSpec(x.shape, lambda i: (0, 0)),
        out_shape=jax.ShapeDtypeStruct(x.shape, F32),
        scratch_shapes=[*(pltpu.VMEM(w.shape, w.dtype) for w in matrices), pltpu.SemaphoreType.DMA((len(matrices),))],
        compiler_params=pltpu.CompilerParams(dimension_semantics=("arbitrary",), vmem_limit_bytes=VMEM_LIMIT),
        name="out_proj",
    )(x, a, hm, p, *weights)


N_FUSED_WEIGHTS = 13
N_PROJ_SLOTS = 9
PROJ_SPACING = 3


def _prompt_kernel(sink_ref, x0_ref, cos_ref, sin_ref, x2_ref, p_ref, gmix_ref, wp_ref, bp_ref, *rest,
                   tiles_per_seq, matrix_idx):
    weights = list(rest[:N_FUSED_WEIGHTS])
    y_ref, kl_ref, vl_ref, cst_ref, nst_ref, mst_ref = rest[N_FUSED_WEIGHTS:N_FUSED_WEIGHTS + 6]
    scr = rest[N_FUSED_WEIGHTS + 6:]
    proj_scr = scr[:N_PROJ_SLOTS]
    kprev_scr, vtprev_scr, a_scr, h_scr, c_scr, m_scr = scr[N_PROJ_SLOTS:N_PROJ_SLOTS + 6]
    w_scr, dma_sem = scr[N_PROJ_SLOTS + 6:-1], scr[-1]
    w_copies = [pltpu.make_async_copy(weights[j], w_scr[n], dma_sem.at[n]) for n, j in enumerate(matrix_idx)]
    for n, j in enumerate(matrix_idx):
        weights[j] = w_scr[n]
    i = pl.program_id(0)
    n_tiles = pl.num_programs(0) - 2
    t1 = jnp.clip(i - 1, 0, n_tiles - 1)
    tile_in_seq = t1 % tiles_per_seq
    s0, s1 = i % 2, (i + 1) % 2
    L = ML_CHUNK
    tm = x0_ref.shape[0]

    @pl.when(i == 0)
    def _():
        for c in w_copies:
            c.start()
        for r in (*proj_scr, kprev_scr, vtprev_scr, a_scr, h_scr, c_scr, m_scr):
            r[...] = jnp.zeros_like(r)

    @pl.when(i == 2)
    def _():
        for c in w_copies:
            c.wait()

    @pl.when(tile_in_seq == 0)
    def _():
        c_scr[...] = jnp.zeros_like(c_scr)
        m_scr[...] = jnp.zeros_like(m_scr)

    def projection():
        outs = [r.at[s0] for r in proj_scr]
        yield from _in_proj_stages(x0_ref[...], gmix_ref, wp_ref, bp_ref, cos_ref, sin_ref,
                                   outs[0], outs[1], outs[2], kl_ref, vl_ref, *outs[3:], feature_major=True)

    def branches():
        qt_ref, kc_ref, vtc_ref, mqt_ref, mk_ref, mvt_ref, og_ref, gif_ref, gt_ref = [r.at[s1] for r in proj_scr]
        slot = s1

        def emit_attn(r, blk):
            a_scr[slot, r * WINDOW:(r + 1) * WINDOW, :] = blk.astype(a_scr.dtype)

        yield from _attn_stages(sink_ref, qt_ref, kprev_scr, kc_ref, vtprev_scr, vtc_ref, tile_in_seq == 0, emit_attn)
        kprev_scr[...] = kc_ref[tm - WINDOW:, :]
        vtprev_scr[...] = vtc_ref[:, tm - WINDOW:]
        hsl = lambda h: slice(h * ML_DIM, (h + 1) * ML_DIM)
        carry = dict(state=[c_scr[h] for h in range(ML_HEADS)], m=[m_scr[h:h + 1, :] for h in range(ML_HEADS)])
        for c in range(tm // L):
            cs = slice(c * L, (c + 1) * L)

            def emit_ml(hs, new_state, new_m, cs=cs):
                for h in range(ML_HEADS):
                    h_scr[slot, cs, hsl(h)] = hs[h].astype(h_scr.dtype)
                carry["state"], carry["m"] = new_state, new_m

            chains = [dict(qt=mqt_ref[0, hsl(h), cs], k=mk_ref[cs, hsl(h)], vt=mvt_ref[0, hsl(h), cs],
                           og=og_ref[cs, hsl(h)], ig_r=gt_ref[0, h:h + 1, cs],
                           b_r=gt_ref[0, ML_HEADS + h:ML_HEADS + h + 1, cs],
                           u_c=gif_ref[cs, 2 * ML_HEADS + h:2 * ML_HEADS + h + 1],
                           state=carry["state"][h], m_prev=carry["m"][h]) for h in range(ML_HEADS)]
            yield from _mlstm_stages(chains, emit_ml)
        for h in range(ML_HEADS):
            c_scr[h] = carry["state"][h]
            m_scr[h:h + 1, :] = carry["m"][h]

    def dense():
        def emit_y(y):
            y_ref[...] = y

        return _dense_stages(x2_ref[...], a_scr[s0], h_scr[s0], p_ref[...], *weights, emit=emit_y)

    @pl.when(i < 2)
    def _():
        _weave(branches(), projection())

    @pl.when((i >= 2) & (i < n_tiles))
    def _():
        _weave(branches(), dense(), _spaced(projection(), PROJ_SPACING))

    @pl.when(i >= n_tiles)
    def _():
        _weave(branches(), dense())

    @pl.when((tile_in_seq == tiles_per_seq - 1) & (i >= 1) & (i <= n_tiles))
    def _():
        nst_ref[...] = jnp.zeros_like(nst_ref)
        for h in range(ML_HEADS):
            final = c_scr[h]
            cst_ref[0, h] = final[:ML_DIM].T
            nst_ref[0, h:h + 1, :] = final[ML_DIM:ML_DIM + 1]
        mst_ref[0] = m_scr[...]


def _prompt_fused(sinks, x, p, cos, sin, g_mix, w_proj, b_proj, weights, batch, seq, tm):
    assert len(weights) == N_FUSED_WEIGHTS
    rows = batch * seq
    n_tiles = rows // tm
    tps = seq // tm
    t0 = lambda i: jnp.minimum(i, n_tiles - 1)
    t1 = lambda i: jnp.clip(i - 1, 0, n_tiles - 1)
    t2 = lambda i: jnp.maximum(i - 2, 0)
    const = lambda i: (0, 0)
    single = lambda a: pl.BlockSpec(a.shape, const, pipeline_mode=pl.Buffered(1))
    slot = lambda shape, dt: pltpu.VMEM((2,) + shape, dt)
    matrix_idx = tuple(j for j, w in enumerate(weights) if w.dtype == BF16)
    matrices = [weights[j] for j in matrix_idx]
    return pl.pallas_call(
        functools.partial(_prompt_kernel, tiles_per_seq=tps, matrix_idx=matrix_idx),
        grid=(n_tiles + 2,),
        in_specs=[
            pl.BlockSpec(memory_space=pltpu.SMEM),
            pl.BlockSpec((tm, D_MODEL), lambda i: (t0(i), 0)),
            pl.BlockSpec((tm, LANES), lambda i: (t0(i) % tps, 0)),
            pl.BlockSpec((tm, LANES), lambda i: (t0(i) % tps, 0)),
            pl.BlockSpec((tm, D_MODEL), lambda i: (t2(i), 0)),
            pl.BlockSpec((tm, D_PLE), lambda i: (t2(i), 0)),
            single(g_mix), single(w_proj), single(b_proj),
        ] + [pl.BlockSpec(memory_space=pl.ANY) if j in matrix_idx else single(w) for j, w in enumerate(weights)],
        out_specs=(
            pl.BlockSpec((tm, D_MODEL), lambda i: (t2(i), 0)),
            pl.BlockSpec((1, ATTN_KV, WINDOW), lambda i: (t0(i) // tps, 0, 0)),
            pl.BlockSpec((1, ATTN_KV, WINDOW), lambda i: (t0(i) // tps, 0, 0)),
            pl.BlockSpec((1, ML_HEADS, ML_DIM, ML_DIM), lambda i: (t1(i) // tps, 0, 0, 0)),
            pl.BlockSpec((1, 8, LANES), lambda i: (t1(i) // tps, 0, 0)),
            pl.BlockSpec((1, 8, LANES), lambda i: (t1(i) // tps, 0, 0)),
        ),
        out_shape=(
            jax.ShapeDtypeStruct((rows, D_MODEL), F32),
            jax.ShapeDtypeStruct((batch, ATTN_KV, WINDOW), F32),
            jax.ShapeDtypeStruct((batch, ATTN_KV, WINDOW), F32),
            jax.ShapeDtypeStruct((batch, ML_HEADS, ML_DIM, ML_DIM), F32),
            jax.ShapeDtypeStruct((batch, 8, LANES), F32),
            jax.ShapeDtypeStruct((batch, 8, LANES), F32),
        ),
        scratch_shapes=[
            slot((ATTN_Q, tm), BF16), slot((tm, ATTN_KV), BF16), slot((ATTN_KV, tm), BF16),
            slot((1, ML_W, tm), BF16), slot((tm, ML_W), BF16), slot((1, ML_W, tm), BF16),
            slot((tm, ML_W), F32), slot((tm, LANES), F32), slot((1, 8, tm), F32),
            pltpu.VMEM((WINDOW, ATTN_KV), BF16), pltpu.VMEM((ATTN_KV, WINDOW), BF16),
            slot((tm, ATTN_Q), BF16), slot((tm, ML_W), BF16),
            pltpu.VMEM((ML_HEADS, ML_STATE_ROWS, ML_DIM), F32), pltpu.VMEM((8, LANES), F32),
            *(pltpu.VMEM(w.shape, w.dtype) for w in matrices), pltpu.SemaphoreType.DMA((len(matrices),)),
        ],
        compiler_params=pltpu.CompilerParams(dimension_semantics=("arbitrary",), vmem_limit_bytes=VMEM_LIMIT),
        name="prompt_fused",
    )(sinks, x, cos, sin, x, p, g_mix, w_proj, b_proj, *weights)


def _attn_decode_kernel(sink_ref, qm_ref, kn_ref, vn_ref, ck_ref, cv_ref, o_ref, ko_ref, vo_ref):
    bt = qm_ref.shape[0]
    qm = qm_ref[...]
    kn = kn_ref[...]
    vn = vn_ref[...]
    ck = ck_ref[...]
    cv = cv_ref[...]
    head = lax.broadcasted_iota(jnp.int32, (N_Q_HEADS, 1), 0)
    sink = jnp.zeros((N_Q_HEADS, 1), F32)
    for hd in range(N_Q_HEADS):
        sink = jnp.where(head == hd, sink_ref[hd], sink)
    s = jnp.einsum('bhc,bck->bhk', qm, ck.astype(BF16), preferred_element_type=F32)
    s_new = jnp.sum(qm.astype(F32) * kn[:, None, :], axis=2, keepdims=True)
    m = jnp.maximum(jnp.maximum(jnp.max(s, axis=2, keepdims=True), s_new), sink)
    p = jnp.exp(s - m)
    p_new = jnp.exp(s_new - m)
    den = jnp.sum(p, axis=2, keepdims=True) + p_new + jnp.exp(sink - m)
    o = jnp.einsum('bhk,bck->bhc', p.astype(BF16), cv.astype(BF16), preferred_element_type=F32)
    o_ref[...] = (o + p_new * vn[:, None, :]) / den
    pad = jnp.zeros((LANES - bt, ATTN_KV), F32)
    kn_t = jnp.concatenate([kn, pad], axis=0).T
    vn_t = jnp.concatenate([vn, pad], axis=0).T
    newest = lax.broadcasted_iota(jnp.int32, (ATTN_KV, WINDOW), 1) == WINDOW - 1
    for b in range(bt):
        ko_ref[b] = jnp.where(newest, kn_t[:, b:b + 1], pltpu.roll(ck[b], WINDOW - 1, 1))
        vo_ref[b] = jnp.where(newest, vn_t[:, b:b + 1], pltpu.roll(cv[b], WINDOW - 1, 1))


def _attn_decode(sinks, qmat, k_new, v_new, cache_k, cache_v, bt):
    nbatch = qmat.shape[0]
    b3 = lambda i: (i, 0, 0)
    b2 = lambda i: (i, 0)
    return pl.pallas_call(
        _attn_decode_kernel,
        grid=(nbatch // bt,),
        in_specs=[
            pl.BlockSpec(memory_space=pltpu.SMEM),
            pl.BlockSpec((bt, N_Q_HEADS, ATTN_KV), b3),
            pl.BlockSpec((bt, ATTN_KV), b2),
            pl.BlockSpec((bt, ATTN_KV), b2),
            pl.BlockSpec((bt, WINDOW, ATTN_KV), b3),
            pl.BlockSpec((bt, WINDOW, ATTN_KV), b3),
        ],
        out_specs=(
            pl.BlockSpec((bt, N_Q_HEADS, ATTN_KV), b3),
            pl.BlockSpec((bt, WINDOW, ATTN_KV), b3),
            pl.BlockSpec((bt, WINDOW, ATTN_KV), b3),
        ),
        out_shape=(
            jax.ShapeDtypeStruct((nbatch, N_Q_HEADS, ATTN_KV), F32),
            jax.ShapeDtypeStruct((nbatch, WINDOW, ATTN_KV), F32),
            jax.ShapeDtypeStruct((nbatch, WINDOW, ATTN_KV), F32),
        ),
        compiler_params=pltpu.CompilerParams(dimension_semantics=("parallel",)),
        name="attn_decode",
    )(sinks, qmat, k_new, v_new, cache_k, cache_v)


def _mlstm_decode_kernel(q_ref, k_ref, v_ref, og_ref, gif_ref, c_ref, n_ref, m_ref, h_ref, co_ref, no_ref, mo_ref):
    bt = q_ref.shape[0]
    gif = gif_ref[...]
    ig = gif[:, 0:ML_HEADS]
    lf = gif[:, ML_HEADS:2 * ML_HEADS]
    m_prev = m_ref[...]
    a = lf + m_prev
    m_t = jnp.maximum(a, ig)
    aw = jnp.exp(a - m_t)
    e_i = jnp.exp(ig - m_t)
    floor = jnp.exp(-m_t)
    m_new = jnp.maximum(a, ig)
    decay = jnp.exp(a - m_new)
    w = jnp.exp(ig - m_new)
    mo_ref[...] = m_new
    pad = jnp.zeros((LANES - bt, ML_DIM), F32)
    lane = lax.broadcasted_iota(jnp.int32, (ML_DIM, LANES), 1)
    for h in range(ML_HEADS):
        hs = slice(h * ML_DIM, (h + 1) * ML_DIM)
        q = q_ref[:, hs]
        k = k_ref[:, hs]
        v = v_ref[:, hs]
        n = n_ref[:, h, :]
        qk = jnp.sum(q * k, axis=1, keepdims=True)
        qn = jnp.sum(q * n, axis=1, keepdims=True)
        sw = qk * e_i[:, h:h + 1]
        den = aw[:, h:h + 1] * qn + sw
        inv = 1.0 / jnp.maximum(jnp.abs(den), floor[:, h:h + 1])
        no_ref[:, h, :] = decay[:, h:h + 1] * n + w[:, h:h + 1] * k
        q_b = q.astype(BF16)
        wk_t = jnp.concatenate([w[:, h:h + 1] * k, pad], axis=0).T.astype(BF16)
        v_rows = jnp.concatenate([v, pad], axis=0).astype(BF16)
        nums = []
        for b in range(bt):
            C = c_ref[b, h]
            q_c = _mm(q_b, C.astype(BF16))[b:b + 1, :]
            nums.append(aw[b:b + 1, h:h + 1] * q_c)
            outer = _mm(jnp.where(lane == b, wk_t, jnp.zeros_like(wk_t)), v_rows)
            co_ref[b, h] = decay[b:b + 1, h:h + 1] * C + outer
        num = jnp.concatenate(nums, axis=0) + sw * v
        h_ref[:, hs] = (num * inv * og_ref[:, hs]).astype(h_ref.dtype)


def _mlstm_decode(mq, mk, mv, og, gif, state_c, state_n, state_m, bt):
    nbatch = mq.shape[0]
    b2 = lambda i: (i, 0)
    b3 = lambda i: (i, 0, 0)
    b4 = lambda i: (i, 0, 0, 0)
    return pl.pallas_call(
        _mlstm_decode_kernel,
        grid=(nbatch // bt,),
        in_specs=[
            pl.BlockSpec((bt, ML_W), b2),
            pl.BlockSpec((bt, ML_W), b2),
            pl.BlockSpec((bt, ML_W), b2),
            pl.BlockSpec((bt, ML_W), b2),
            pl.BlockSpec((bt, LANES), b2),
            pl.BlockSpec((bt, ML_HEADS, ML_DIM, ML_DIM), b4),
            pl.BlockSpec((bt, ML_HEADS, ML_DIM), b3),
            pl.BlockSpec((bt, ML_HEADS), b2),
        ],
        out_specs=(
            pl.BlockSpec((bt, ML_W), b2),
            pl.BlockSpec((bt, ML_HEADS, ML_DIM, ML_DIM), b4),
            pl.BlockSpec((bt, ML_HEADS, ML_DIM), b3),
            pl.BlockSpec((bt, ML_HEADS), b2),
        ),
        out_shape=(
            jax.ShapeDtypeStruct((nbatch, ML_W), BF16),
            jax.ShapeDtypeStruct((nbatch, ML_HEADS, ML_DIM, ML_DIM), F32),
            jax.ShapeDtypeStruct((nbatch, ML_HEADS, ML_DIM), F32),
            jax.ShapeDtypeStruct((nbatch, ML_HEADS), F32),
        ),
        compiler_params=pltpu.CompilerParams(dimension_semantics=("parallel",)),
        name="mlstm_decode",
    )(mq, mk, mv, og, gif, state_c, state_n, state_m)


def _rope_tables(pos):
    half = HEAD_DIM // 2
    inv = (1.0 / (ROPE_THETA ** (np.arange(half, dtype=np.float32) / half))).astype(np.float32)
    ang = pos.astype(np.float32)[:, None] * inv[None, :]
    cos = np.cos(ang)
    sin = np.sin(ang)
    cos_t = np.concatenate([cos, cos, cos, cos], axis=1)
    sin_t = np.concatenate([-sin, sin, -sin, sin], axis=1)
    return jnp.asarray(cos_t, F32), jnp.asarray(sin_t, F32)


def kernel(x_prompt, x_sample, cache_k, cache_v, state_C, state_n, state_m, p_prompt, p_sample, norm_mix, w_in, b_in, attn_sinks, w_attn_up, w_ml_up, w_o, norm_mlp, w_ff1, w_ff2, norm_ple, w_ple_gate, w_ple_proj, norm_final):
    assert w_in.shape[0] == 1, "single layer"
    batch, seq, _ = x_prompt.shape
    nbatch = x_sample.shape[0]
    assert x_sample.shape[1] == 1

    w_proj, w_gate, w_au, w_mu, w_out, w_f1, w_f2, w_pg, w_pp = _weight_cast(
        w_in[0].T, (w_attn_up[0], w_ml_up[0], w_o[0], w_ff1[0], w_ff2[0], w_ple_gate[0], w_ple_proj[0]))
    b_proj = b_in[:, :N_PROJ]
    b_gate = b_in[:, OFF_GA:]
    g_mix = norm_mix[0][None, :]
    out_weights = (
        g_mix, w_gate, b_gate, w_au, w_mu, w_out,
        norm_mlp[0][None, :], w_f1, w_f2,
        norm_ple[0][None, :], w_pg, w_pp, norm_final[None, :],
    )
    sinks = attn_sinks[0]
    to_fp = lambda c: c[0].transpose(0, 2, 3, 1).reshape(c.shape[1], ATTN_KV, WINDOW)
    from_fp = lambda c: c.reshape(c.shape[0], N_KV_HEADS, HEAD_DIM, WINDOW).transpose(0, 3, 1, 2)[None]

    cos_p, sin_p = _rope_tables(np.arange(seq, dtype=np.int32))
    xp = x_prompt.reshape(batch * seq, D_MODEL)
    y_prompt, k_last, v_last, cst, nst, mst = _prompt_fused(
        sinks, xp, p_prompt[0].reshape(batch * seq, D_PLE), cos_p, sin_p, g_mix, w_proj, b_proj, out_weights,
        batch, seq, tm=PROMPT_TILE)
    y_prompt = y_prompt.reshape(batch, seq, D_MODEL)
    k_prompt = from_fp(k_last)
    v_prompt = from_fp(v_last)
    c_prompt = cst[None]
    n_prompt = nst[:, :ML_HEADS, :][None]
    m_prompt = mst[:, :ML_HEADS, 0][None]

    cos_s, sin_s = _rope_tables(np.full((nbatch,), PAST_LEN, dtype=np.int32))
    xs = x_sample.reshape(nbatch, D_MODEL)
    qs, _, _, ks, vs, mqs, mks, mvs, ogs, gifs, _ = _in_proj_sample(xs, g_mix, w_proj, b_proj, cos_s, sin_s)
    qmat = qs.reshape(nbatch, N_Q_HEADS, ATTN_KV)
    o_full, k_s, v_s = _attn_decode(sinks, qmat, ks, vs, to_fp(cache_k), to_fp(cache_v), bt=DECODE_ROWS)
    o5 = o_full.reshape(nbatch, N_KV_HEADS, GQA_GROUP, N_KV_HEADS, HEAD_DIM)
    a_s = jnp.stack([o5[:, kv, :, kv, :] for kv in range(N_KV_HEADS)], axis=1).reshape(nbatch, ATTN_Q).astype(BF16)
    h_s, c_s, n_s, m_s = _mlstm_decode(mqs, mks, mvs, ogs, gifs, state_C[0], state_n[0], state_m[0], bt=DECODE_ROWS)
    y_sample = _out_proj_sample(xs, a_s, h_s, p_sample[0].reshape(nbatch, D_PLE), out_weights)
    y_sample = y_sample.reshape(nbatch, 1, D_MODEL)
    k_sample = from_fp(k_s)
    v_sample = from_fp(v_s)

    return (y_prompt, y_sample, k_prompt, v_prompt, c_prompt, n_prompt, m_prompt,
            k_sample, v_sample, c_s[None], n_s[None], m_s[None])
```

```python
import functools

import numpy as np
import jax
import jax.numpy as jnp
from jax import lax
from jax.experimental import pallas as pl
from jax.experimental.pallas import tpu as pltpu

F32 = jnp.float32
BF16 = jnp.bfloat16

D_MODEL = 1024
HEAD_DIM = 64
N_Q_HEADS = 8
N_KV_HEADS = 2
GQA_GROUP = N_Q_HEADS // N_KV_HEADS
WINDOW = 128
ROPE_THETA = 10000.0
ML_HEADS = 4
ML_DIM = 128
ML_CHUNK = 128
D_FF = 4 * D_MODEL
D_PLE = 256
EPS = 1e-6
PAST_LEN = 16384

ATTN_Q = N_Q_HEADS * HEAD_DIM
ATTN_KV = N_KV_HEADS * HEAD_DIM
ML_W = ML_HEADS * ML_DIM
LANES = 128

OFF_AQ = 0
OFF_AK = OFF_AQ + ATTN_Q
OFF_AV = OFF_AK + ATTN_KV
OFF_MQ = OFF_AV + ATTN_KV
OFF_MK = OFF_MQ + ML_W
OFF_MV = OFF_MK + ML_W
OFF_MO = OFF_MV + ML_W
OFF_MI = OFF_MO + ML_W
OFF_MF = OFF_MI + ML_HEADS
OFF_GA = OFF_MF + ML_HEADS
OFF_GM = OFF_GA + D_MODEL
D_IN = OFF_GM + D_MODEL
N_PROJ = OFF_MI + LANES

VMEM_LIMIT = 60 * 1024 * 1024
PROMPT_TILE = 256
DECODE_ROWS = 32

def _rms(x, g):
    r = lax.rsqrt(jnp.mean(x * x, axis=-1, keepdims=True) + EPS)
    return (x * r) * g


def _mm(a, b):
    return jnp.dot(a, b, preferred_element_type=F32)


def _log_sigmoid(x):
    return jnp.minimum(x, 0.0) - jnp.log1p(jnp.exp(-jnp.abs(x)))


def _weave(*streams):
    live = list(streams)
    while live:
        for g in list(live):
            try:
                next(g)
            except StopIteration:
                live.remove(g)


class _AwaitedRef:
    def __init__(self, ref, copy):
        self._ref, self._copy = ref, copy

    def __getitem__(self, idx):
        if self._copy is not None:
            self._copy.wait()
            self._copy = None
        return self._ref[idx]


WEIGHT_ROW_SPLIT = 8


def _weight_cast_kernel(*refs):
    n_plain = (len(refs) - 3) // 2
    w_in_t_ref, plain_in = refs[0], refs[1:1 + n_plain]
    w_proj_ref, w_gate_ref, plain_out = refs[1 + n_plain], refs[2 + n_plain], refs[3 + n_plain:]
    for g in range(N_PROJ // LANES):
        w_proj_ref[:, g * LANES:(g + 1) * LANES] = w_in_t_ref[g * LANES:(g + 1) * LANES, :].T.astype(BF16)
    for g in range(2 * D_MODEL // LANES):
        w_gate_ref[:, g * LANES:(g + 1) * LANES] = w_in_t_ref[OFF_GA + g * LANES:OFF_GA + (g + 1) * LANES, :].T.astype(BF16)
    for src, dst in zip(plain_in, plain_out):
        dst[...] = src[...].astype(BF16)


def _weight_cast(w_in_t, plain):
    split = WEIGHT_ROW_SPLIT
    assert D_MODEL // split == LANES
    rows_blk = lambda w: pl.BlockSpec((w.shape[0] // split, w.shape[1]), lambda i: (i, 0))
    out_shapes = (jax.ShapeDtypeStruct((D_MODEL, N_PROJ), BF16), jax.ShapeDtypeStruct((D_MODEL, 2 * D_MODEL), BF16),
                  *(jax.ShapeDtypeStruct(w.shape, BF16) for w in plain))
    return pl.pallas_call(
        _weight_cast_kernel,
        grid=(split,),
        in_specs=[pl.BlockSpec((D_IN, LANES), lambda i: (0, i))] + [rows_blk(w) for w in plain],
        out_specs=tuple(rows_blk(s) for s in out_shapes),
        out_shape=out_shapes,
        compiler_params=pltpu.CompilerParams(dimension_semantics=("parallel",), vmem_limit_bytes=VMEM_LIMIT),
        name="weight_cast",
    )(w_in_t, *plain)


def _in_proj_stages(x, g_ref, w_ref, b_ref, cos_ref, sin_ref,
                    q_ref, k_ref, v_ref, kl_ref, vl_ref, mq_ref, mk_ref, mv_ref, og_ref, gif_ref, gt_ref,
                    *, feature_major):
    xn = _rms(x, g_ref[...]).astype(BF16)
    tm = x.shape[0]

    def proj(lo, hi):
        return _mm(xn, w_ref[:, lo:hi]) + b_ref[:, lo:hi]

    cos = cos_ref[...]
    sin = sin_ref[...]
    lane = lax.broadcasted_iota(jnp.int32, (1, LANES), 1)
    first_half = (lane % HEAD_DIM) < (HEAD_DIM // 2)

    def rope(z):
        partner = jnp.where(first_half, pltpu.roll(z, LANES - HEAD_DIM // 2, 1), pltpu.roll(z, HEAD_DIM // 2, 1))
        return z * cos + partner * sin

    zq = proj(OFF_AQ, OFF_AK)
    for c in range(ATTN_Q // LANES):
        sl = slice(c * LANES, (c + 1) * LANES)
        qc = rope(zq[:, sl]) * (HEAD_DIM ** -0.5)
        if feature_major:
            q_ref[sl, :] = qc.T.astype(q_ref.dtype)
        else:
            kv = (2 * c) // GQA_GROUP
            own = (lane // HEAD_DIM) == kv
            swapped = pltpu.roll(qc, HEAD_DIM, 1)
            for j, src in enumerate((qc, swapped) if kv == 0 else (swapped, qc)):
                hd = 2 * c + j
                q_ref[:, hd * LANES:(hd + 1) * LANES] = jnp.where(own, src, 0.0).astype(q_ref.dtype)
    yield
    zkv = proj(OFF_AK, OFF_MQ)
    k = rope(zkv[:, :ATTN_KV])
    v = zkv[:, ATTN_KV:]
    k_ref[...] = k.astype(k_ref.dtype)
    v_ref[...] = (v.T if feature_major else v).astype(v_ref.dtype)
    if feature_major:
        kl_ref[0] = k[tm - WINDOW:, :].T
        vl_ref[0] = v[tm - WINDOW:, :].T
    else:
        kl_ref[...] = k[tm - WINDOW:, :]
        vl_ref[...] = v[tm - WINDOW:, :]
    yield
    for z_off, dst in ((OFF_MQ, mq_ref), (OFF_MV, mv_ref)):
        z = proj(z_off, z_off + ML_W)
        if feature_major:
            for h in range(ML_HEADS):
                hs = slice(h * ML_DIM, (h + 1) * ML_DIM)
                dst[0, hs, :] = z[:, hs].T.astype(dst.dtype)
        else:
            dst[...] = z.astype(dst.dtype)
        yield
    mk_ref[...] = (proj(OFF_MK, OFF_MV) * (ML_DIM ** -0.5)).astype(mk_ref.dtype)
    yield
    og_ref[...] = jax.nn.sigmoid(proj(OFF_MO, OFF_MI))
    yield
    zg = proj(OFF_MI, N_PROJ)
    gif = jnp.where(lane < ML_HEADS, zg, jnp.where(lane < 2 * ML_HEADS, _log_sigmoid(zg), 0.0))
    if feature_major:
        L = ML_CHUNK
        upper = (lax.broadcasted_iota(jnp.int32, (L, L), 0) <= lax.broadcasted_iota(jnp.int32, (L, L), 1)).astype(F32)
        row8 = lax.broadcasted_iota(jnp.int32, (8, 1), 0)
        fill = jnp.zeros((L - 16, L), F32)
        for c in range(tm // L):
            cs = slice(c * L, (c + 1) * L)
            g_rows = gif[cs, :].T[0:8, :]
            cum = jnp.dot(g_rows, upper, precision=lax.Precision.HIGHEST, preferred_element_type=F32)
            rows = jnp.where(row8 < ML_HEADS, g_rows, cum)
            gt_ref[0, :, cs] = rows
            diff = rows - pltpu.roll(rows, ML_HEADS, 0)
            gif_ref[cs, :] = jnp.concatenate([rows, diff, fill], axis=0).T
    else:
        gif_ref[...] = gif
        gt_ref[0] = jnp.zeros(gt_ref.shape[1:], F32)
    yield


def _in_proj_kernel(x_ref, *refs, feature_major):
    _weave(_in_proj_stages(x_ref[...], *refs, feature_major=feature_major))


def _in_proj_sample(x, g, w, b, cos, sin):
    rows = x.shape[0]
    assert rows == WINDOW
    full = lambda n, dt: (jax.ShapeDtypeStruct((rows, n), dt), pl.BlockSpec((rows, n), lambda i: (0, 0)))
    outs = (
        full(N_Q_HEADS * LANES, BF16), full(ATTN_KV, BF16), full(ATTN_KV, BF16), full(ATTN_KV, F32), full(ATTN_KV, F32),
        full(ML_W, F32), full(ML_W, F32), full(ML_W, F32), full(ML_W, F32), full(LANES, F32),
        (jax.ShapeDtypeStruct((1, 8, rows), F32), pl.BlockSpec((1, 8, rows), lambda i: (0, 0, 0))),
    )
    return pl.pallas_call(
        functools.partial(_in_proj_kernel, feature_major=False),
        grid=(1,),
        in_specs=[pl.BlockSpec(a.shape, lambda i: (0, 0)) for a in (x, g, w, b, cos, sin)],
        out_specs=tuple(o[1] for o in outs),
        out_shape=tuple(o[0] for o in outs),
        compiler_params=pltpu.CompilerParams(dimension_semantics=("arbitrary",), vmem_limit_bytes=VMEM_LIMIT),
        name="in_proj",
    )(x, g, w, b, cos, sin)


def _attn_stages(sink_ref, qt_ref, kp_ref, kc_ref, vtp_ref, vtc_ref, first_block, emit):
    L = WINDOW
    nsub = kc_ref.shape[0] // L
    key = lax.broadcasted_iota(jnp.int32, (2 * L, L), 0)
    t = lax.broadcasted_iota(jnp.int32, (2 * L, L), 1)
    is_prev = key < L
    allowed = (is_prev & (key >= t)) | (~is_prev & (key - L <= t))
    bias = jnp.where(allowed, 0.0, -jnp.inf)
    bias_first = jnp.where(is_prev & first_block, -jnp.inf, bias)
    zeros_q = jnp.zeros((HEAD_DIM, L), BF16)
    heads = range(N_Q_HEADS)
    for r in range(nsub):
        ts = slice(r * L, (r + 1) * L)
        k_prev = kp_ref[...] if r == 0 else kc_ref[(r - 1) * L:r * L, :]
        vt_prev = vtp_ref[...] if r == 0 else vtc_ref[:, (r - 1) * L:r * L]
        k2 = jnp.concatenate([k_prev, kc_ref[ts, :]], axis=0)
        vt2 = jnp.concatenate([vt_prev, vtc_ref[:, ts]], axis=1)
        b_r = bias_first if r == 0 else bias
        q_op = [jnp.concatenate([qt_ref[hd * HEAD_DIM:(hd + 1) * HEAD_DIM, ts], zeros_q][::1 if hd < GQA_GROUP else -1],
                                axis=0) for hd in heads]
        s = [_mm(k2, q_op[hd]) + b_r for hd in heads]
        yield
        m = [jnp.maximum(jnp.max(s[hd], axis=0, keepdims=True), sink_ref[hd]) for hd in heads]
        yield
        p = [jnp.exp(s[hd] - m[hd]) for hd in heads]
        yield
        den = [jnp.sum(p[hd], axis=0, keepdims=True) + jnp.exp(sink_ref[hd] - m[hd]) for hd in heads]
        o = [_mm(vt2[(hd // GQA_GROUP) * HEAD_DIM:(hd // GQA_GROUP + 1) * HEAD_DIM, :], p[hd].astype(BF16)) / den[hd]
             for hd in heads]
        yield
        emit(r, jnp.concatenate(o, axis=0).T)
        yield


ML_STATE_ROWS = ML_DIM + 8


def _mlstm_stages(chains, emit):
    L = ML_CHUNK
    s_idx = lax.broadcasted_iota(jnp.int32, (L, L), 0)
    t_idx = lax.broadcasted_iota(jnp.int32, (L, L), 1)
    causal = s_idx <= t_idx
    n = range(len(chains))
    dlog = [jnp.where(causal, ch["b_r"] + ch["u_c"], -jnp.inf) for ch in chains]
    yield
    qk = [_mm(ch["k"], ch["qt"]) for ch in chains]
    inter = [_mm(ch["state"].astype(BF16), ch["qt"]) for ch in chains]
    yield
    a = [ch["b_r"] + ch["m_prev"] for ch in chains]
    m_t = [jnp.maximum(a[i], jnp.max(dlog[i], axis=0, keepdims=True)) for i in n]
    yield
    sw = [qk[i] * jnp.exp(dlog[i] - m_t[i]) for i in n]
    yield
    sv = [_mm(chains[i]["vt"], sw[i].astype(BF16)) for i in n]
    yield
    hs = []
    for i in n:
        aw = jnp.exp(a[i] - m_t[i])
        num = aw * inter[i][:ML_DIM] + sv[i]
        den = aw * inter[i][ML_DIM:ML_DIM + 1] + jnp.sum(sw[i], axis=0, keepdims=True)
        hh = num / jnp.maximum(jnp.abs(den), jnp.exp(-m_t[i]))
        hs.append(hh.T * chains[i]["og"])
    yield
    pad = jnp.zeros((ML_STATE_ROWS - ML_DIM - 1, L), F32)
    new_state, new_m = [], []
    for i in n:
        ch = chains[i]
        b_end = ch["b_r"][:, L - 1:L]
        g = b_end - ch["b_r"] + ch["ig_r"]
        m_new = jnp.maximum(b_end + ch["m_prev"], jnp.max(g, axis=1, keepdims=True))
        decay = jnp.exp(b_end + ch["m_prev"] - m_new)
        w = jnp.exp(g - m_new)
        v_aug = jnp.concatenate([ch["vt"].astype(F32) * w, w, pad], axis=0).astype(BF16)
        new_state.append(decay * ch["state"] + _mm(v_aug, ch["k"]))
        new_m.append(m_new)
    emit(hs, new_state, new_m)
    yield


DENSE_COLS = 512


def _dense_stages(x, a, hm, p, gmix_ref, wg_ref, bg_ref, wau_ref, wmu_ref, wo_ref,
                  gmlp_ref, wff1_ref, wff2_ref, gple_ref, wpg_ref, wpp_ref, gfin_ref, emit):
    nc = DENSE_COLS
    pieces = lambda n: [slice(c, c + nc) for c in range(0, n, nc)]
    xn = _rms(x, gmix_ref[...]).astype(BF16)
    gates = []
    for cs in pieces(2 * D_MODEL):
        gates.append(jax.nn.sigmoid(_mm(xn, wg_ref[:, cs]) + bg_ref[:, cs]))
        yield
    half = D_MODEL // nc
    mix = []
    for j, cs in enumerate(pieces(D_MODEL)):
        mix.append((gates[j] * _mm(a, wau_ref[:, cs]) + gates[half + j] * _mm(hm, wmu_ref[:, cs])).astype(BF16))
        yield
    mix = jnp.concatenate(mix, axis=1)
    h = []
    for cs in pieces(D_MODEL):
        h.append(x[:, cs] + _mm(mix, wo_ref[:, cs]))
        yield
    h = jnp.concatenate(h, axis=1)
    hn = _rms(h, gmlp_ref[...]).astype(BF16)
    u = []
    for cs in pieces(D_FF):
        u.append(jnp.square(jnp.maximum(_mm(hn, wff1_ref[:, cs]), 0.0)).astype(BF16))
        yield
    u = jnp.concatenate(u, axis=1)
    h2 = []
    for cs in pieces(D_MODEL):
        h2.append(h[:, cs] + _mm(u, wff2_ref[:, cs]))
        yield
    h = jnp.concatenate(h2, axis=1)
    hn = _rms(h, gple_ref[...]).astype(BF16)
    pb = p.astype(BF16)
    h3 = []
    for cs in pieces(D_MODEL):
        h3.append(h[:, cs] + _mm(pb, wpp_ref[:, cs]) * jax.nn.sigmoid(_mm(hn, wpg_ref[:, cs])))
        yield
    emit(_rms(jnp.concatenate(h3, axis=1), gfin_ref[...]))
    yield


def _out_kernel(x_ref, a_ref, hm_ref, p_ref, *rest, matrix_idx):
    weights, y_ref = list(rest[:N_FUSED_WEIGHTS]), rest[N_FUSED_WEIGHTS]
    w_scr, dma_sem = rest[N_FUSED_WEIGHTS + 1:-1], rest[-1]
    copies = [pltpu.make_async_copy(weights[j], w_scr[n], dma_sem.at[n]) for n, j in enumerate(matrix_idx)]
    for c in copies:
        c.start()
    for n, j in enumerate(matrix_idx):
        weights[j] = _AwaitedRef(w_scr[n], copies[n])

    def emit(y):
        y_ref[...] = y

    _weave(_dense_stages(x_ref[...], a_ref[...], hm_ref[...], p_ref[...], *weights, emit=emit))


def _out_proj_sample(x, a, hm, p, weights):
    assert len(weights) == N_FUSED_WEIGHTS
    matrix_idx = tuple(j for j, w in enumerate(weights) if w.dtype == BF16)
    matrices = [weights[j] for j in matrix_idx]
    whole = lambda o: pl.BlockSpec(o.shape, lambda i: (0, 0), pipeline_mode=pl.Buffered(1))
    return pl.pallas_call(
        functools.partial(_out_kernel, matrix_idx=matrix_idx),
        grid=(1,),
        in_specs=[whole(o) for o in (x, a, hm, p)]
        + [pl.BlockSpec(memory_space=pl.ANY) if j in matrix_idx else whole(w) for j, w in enumerate(weights)],
        out_specs=pl.BlockSpec(x.shape, lambda i: (0, 0)),
        out_shape=jax.ShapeDtypeStruct(x.shape, F32),
        scratch_shapes=[*(pltpu.VMEM(w.shape, w.dtype) for w in matrices), pltpu.SemaphoreType.DMA((len(matrices),))],
        compiler_params=pltpu.CompilerParams(dimension_semantics=("arbitrary",), vmem_limit_bytes=VMEM_LIMIT),
        name="out_proj",
    )(x, a, hm, p, *weights)


N_FUSED_WEIGHTS = 13
N_PROJ_SLOTS = 9


def _prompt_kernel(sink_ref, x0_ref, cos_ref, sin_ref, x2_ref, p_ref, gmix_ref, wp_ref, bp_ref, *rest,
                   tiles_per_seq, matrix_idx):
    weights = list(rest[:N_FUSED_WEIGHTS])
    y_ref, kl_ref, vl_ref, cst_ref, nst_ref, mst_ref = rest[N_FUSED_WEIGHTS:N_FUSED_WEIGHTS + 6]
    scr = rest[N_FUSED_WEIGHTS + 6:]
    proj_scr = scr[:N_PROJ_SLOTS]
    kprev_scr, vtprev_scr, a_scr, h_scr, c_scr, m_scr = scr[N_PROJ_SLOTS:N_PROJ_SLOTS + 6]
    w_scr, dma_sem = scr[N_PROJ_SLOTS + 6:-1], scr[-1]
    w_copies = [pltpu.make_async_copy(weights[j], w_scr[n], dma_sem.at[n]) for n, j in enumerate(matrix_idx)]
    for n, j in enumerate(matrix_idx):
        weights[j] = w_scr[n]
    i = pl.program_id(0)
    n_tiles = pl.num_programs(0) - 2
    t1 = jnp.clip(i - 1, 0, n_tiles - 1)
    tile_in_seq = t1 % tiles_per_seq
    s0, s1 = i % 2, (i + 1) % 2
    L = ML_CHUNK
    tm = x0_ref.shape[0]

    @pl.when(i == 0)
    def _():
        for c in w_copies:
            c.start()
        for r in (*proj_scr, kprev_scr, vtprev_scr, a_scr, h_scr, c_scr, m_scr):
            r[...] = jnp.zeros_like(r)

    @pl.when(i == 2)
    def _():
        for c in w_copies:
            c.wait()

    @pl.when(tile_in_seq == 0)
    def _():
        c_scr[...] = jnp.zeros_like(c_scr)
        m_scr[...] = jnp.zeros_like(m_scr)

    def projection():
        outs = [r.at[s0] for r in proj_scr]
        yield from _in_proj_stages(x0_ref[...], gmix_ref, wp_ref, bp_ref, cos_ref, sin_ref,
                                   outs[0], outs[1], outs[2], kl_ref, vl_ref, *outs[3:], feature_major=True)

    def branches():
        qt_ref, kc_ref, vtc_ref, mqt_ref, mk_ref, mvt_ref, og_ref, gif_ref, gt_ref = [r.at[s1] for r in proj_scr]
        slot = s1

        def emit_attn(r, blk):
            a_scr[slot, r * WINDOW:(r + 1) * WINDOW, :] = blk.astype(a_scr.dtype)

        yield from _attn_stages(sink_ref, qt_ref, kprev_scr, kc_ref, vtprev_scr, vtc_ref, tile_in_seq == 0, emit_attn)
        kprev_scr[...] = kc_ref[tm - WINDOW:, :]
        vtprev_scr[...] = vtc_ref[:, tm - WINDOW:]
        hsl = lambda h: slice(h * ML_DIM, (h + 1) * ML_DIM)
        carry = dict(state=[c_scr[h] for h in range(ML_HEADS)], m=[m_scr[h:h + 1, :] for h in range(ML_HEADS)])
        for c in range(tm // L):
            cs = slice(c * L, (c + 1) * L)

            def emit_ml(hs, new_state, new_m, cs=cs):
                for h in range(ML_HEADS):
                    h_scr[slot, cs, hsl(h)] = hs[h].astype(h_scr.dtype)
                carry["state"], carry["m"] = new_state, new_m

            chains = [dict(qt=mqt_ref[0, hsl(h), cs], k=mk_ref[cs, hsl(h)], vt=mvt_ref[0, hsl(h), cs],
                           og=og_ref[cs, hsl(h)], ig_r=gt_ref[0, h:h + 1, cs],
                           b_r=gt_ref[0, ML_HEADS + h:ML_HEADS + h + 1, cs],
                           u_c=gif_ref[cs, 2 * ML_HEADS + h:2 * ML_HEADS + h + 1],
                           state=carry["state"][h], m_prev=carry["m"][h]) for h in range(ML_HEADS)]
            yield from _mlstm_stages(chains, emit_ml)
        for h in range(ML_HEADS):
            c_scr[h] = carry["state"][h]
            m_scr[h:h + 1, :] = carry["m"][h]

    def dense():
        def emit_y(y):
            y_ref[...] = y

        return _dense_stages(x2_ref[...], a_scr[s0], h_scr[s0], p_ref[...], *weights, emit=emit_y)

    @pl.when(i < 2)
    def _():
        _weave(branches(), projection())

    @pl.when((i >= 2) & (i < n_tiles))
    def _():
        _weave(projection())
        _weave(branches(), dense())

    @pl.when(i >= n_tiles)
    def _():
        _weave(branches(), dense())

    @pl.when((tile_in_seq == tiles_per_seq - 1) & (i >= 1) & (i <= n_tiles))
    def _():
        nst_ref[...] = jnp.zeros_like(nst_ref)
        for h in range(ML_HEADS):
            final = c_scr[h]
            cst_ref[0, h] = final[:ML_DIM].T
            nst_ref[0, h:h + 1, :] = final[ML_DIM:ML_DIM + 1]
        mst_ref[0] = m_scr[...]


def _prompt_fused(sinks, x, p, cos, sin, g_mix, w_proj, b_proj, weights, batch, seq, tm):
    assert len(weights) == N_FUSED_WEIGHTS
    rows = batch * seq
    n_tiles = rows // tm
    tps = seq // tm
    t0 = lambda i: jnp.minimum(i, n_tiles - 1)
    t1 = lambda i: jnp.clip(i - 1, 0, n_tiles - 1)
    t2 = lambda i: jnp.maximum(i - 2, 0)
    const = lambda i: (0, 0)
    single = lambda a: pl.BlockSpec(a.shape, const, pipeline_mode=pl.Buffered(1))
    slot = lambda shape, dt: pltpu.VMEM((2,) + shape, dt)
    matrix_idx = tuple(j for j, w in enumerate(weights) if w.dtype == BF16)
    matrices = [weights[j] for j in matrix_idx]
    return pl.pallas_call(
        functools.partial(_prompt_kernel, tiles_per_seq=tps, matrix_idx=matrix_idx),
        grid=(n_tiles + 2,),
        in_specs=[
            pl.BlockSpec(memory_space=pltpu.SMEM),
            pl.BlockSpec((tm, D_MODEL), lambda i: (t0(i), 0)),
            pl.BlockSpec((tm, LANES), lambda i: (t0(i) % tps, 0)),
            pl.BlockSpec((tm, LANES), lambda i: (t0(i) % tps, 0)),
            pl.BlockSpec((tm, D_MODEL), lambda i: (t2(i), 0)),
            pl.BlockSpec((tm, D_PLE), lambda i: (t2(i), 0)),
            single(g_mix), single(w_proj), single(b_proj),
        ] + [pl.BlockSpec(memory_space=pl.ANY) if j in matrix_idx else single(w) for j, w in enumerate(weights)],
        out_specs=(
            pl.BlockSpec((tm, D_MODEL), lambda i: (t2(i), 0)),
            pl.BlockSpec((1, ATTN_KV, WINDOW), lambda i: (t0(i) // tps, 0, 0)),
            pl.BlockSpec((1, ATTN_KV, WINDOW), lambda i: (t0(i) // tps, 0, 0)),
            pl.BlockSpec((1, ML_HEADS, ML_DIM, ML_DIM), lambda i: (t1(i) // tps, 0, 0, 0)),
            pl.BlockSpec((1, 8, LANES), lambda i: (t1(i) // tps, 0, 0)),
            pl.BlockSpec((1, 8, LANES), lambda i: (t1(i) // tps, 0, 0)),
        ),
        out_shape=(
            jax.ShapeDtypeStruct((rows, D_MODEL), F32),
            jax.ShapeDtypeStruct((batch, ATTN_KV, WINDOW), F32),
            jax.ShapeDtypeStruct((batch, ATTN_KV, WINDOW), F32),
            jax.ShapeDtypeStruct((batch, ML_HEADS, ML_DIM, ML_DIM), F32),
            jax.ShapeDtypeStruct((batch, 8, LANES), F32),
            jax.ShapeDtypeStruct((batch, 8, LANES), F32),
        ),
        scratch_shapes=[
            slot((ATTN_Q, tm), BF16), slot((tm, ATTN_KV), BF16), slot((ATTN_KV, tm), BF16),
            slot((1, ML_W, tm), BF16), slot((tm, ML_W), BF16), slot((1, ML_W, tm), BF16),
            slot((tm, ML_W), F32), slot((tm, LANES), F32), slot((1, 8, tm), F32),
            pltpu.VMEM((WINDOW, ATTN_KV), BF16), pltpu.VMEM((ATTN_KV, WINDOW), BF16),
            slot((tm, ATTN_Q), BF16), slot((tm, ML_W), BF16),
            pltpu.VMEM((ML_HEADS, ML_STATE_ROWS, ML_DIM), F32), pltpu.VMEM((8, LANES), F32),
            *(pltpu.VMEM(w.shape, w.dtype) for w in matrices), pltpu.SemaphoreType.DMA((len(matrices),)),
        ],
        compiler_params=pltpu.CompilerParams(dimension_semantics=("arbitrary",), vmem_limit_bytes=VMEM_LIMIT),
        name="prompt_fused",
    )(sinks, x, cos, sin, x, p, g_mix, w_proj, b_proj, *weights)


def _attn_decode_kernel(sink_ref, qm_ref, kn_ref, vn_ref, ck_ref, cv_ref, o_ref, ko_ref, vo_ref):
    bt = qm_ref.shape[0]
    qm = qm_ref[...]
    kn = kn_ref[...]
    vn = vn_ref[...]
    ck = ck_ref[...]
    cv = cv_ref[...]
    head = lax.broadcasted_iota(jnp.int32, (N_Q_HEADS, 1), 0)
    sink = jnp.zeros((N_Q_HEADS, 1), F32)
    for hd in range(N_Q_HEADS):
        sink = jnp.where(head == hd, sink_ref[hd], sink)
    s = jnp.einsum('bhc,bck->bhk', qm, ck.astype(BF16), preferred_element_type=F32)
    s_new = jnp.sum(qm.astype(F32) * kn[:, None, :], axis=2, keepdims=True)
    m = jnp.maximum(jnp.maximum(jnp.max(s, axis=2, keepdims=True), s_new), sink)
    p = jnp.exp(s - m)
    p_new = jnp.exp(s_new - m)
    den = jnp.sum(p, axis=2, keepdims=True) + p_new + jnp.exp(sink - m)
    o = jnp.einsum('bhk,bck->bhc', p.astype(BF16), cv.astype(BF16), preferred_element_type=F32)
    o_ref[...] = (o + p_new * vn[:, None, :]) / den
    pad = jnp.zeros((LANES - bt, ATTN_KV), F32)
    kn_t = jnp.concatenate([kn, pad], axis=0).T
    vn_t = jnp.concatenate([vn, pad], axis=0).T
    newest = lax.broadcasted_iota(jnp.int32, (ATTN_KV, WINDOW), 1) == WINDOW - 1
    for b in range(bt):
        ko_ref[b] = jnp.where(newest, kn_t[:, b:b + 1], pltpu.roll(ck[b], WINDOW - 1, 1))
        vo_ref[b] = jnp.where(newest, vn_t[:, b:b + 1], pltpu.roll(cv[b], WINDOW - 1, 1))


def _attn_decode(sinks, qmat, k_new, v_new, cache_k, cache_v, bt):
    nbatch = qmat.shape[0]
    b3 = lambda i: (i, 0, 0)
    b2 = lambda i: (i, 0)
    return pl.pallas_call(
        _attn_decode_kernel,
        grid=(nbatch // bt,),
        in_specs=[
            pl.BlockSpec(memory_space=pltpu.SMEM),
            pl.BlockSpec((bt, N_Q_HEADS, ATTN_KV), b3),
            pl.BlockSpec((bt, ATTN_KV), b2),
            pl.BlockSpec((bt, ATTN_KV), b2),
            pl.BlockSpec((bt, WINDOW, ATTN_KV), b3),
            pl.BlockSpec((bt, WINDOW, ATTN_KV), b3),
        ],
        out_specs=(
            pl.BlockSpec((bt, N_Q_HEADS, ATTN_KV), b3),
            pl.BlockSpec((bt, WINDOW, ATTN_KV), b3),
            pl.BlockSpec((bt, WINDOW, ATTN_KV), b3),
        ),
        out_shape=(
            jax.ShapeDtypeStruct((nbatch, N_Q_HEADS, ATTN_KV), F32),
            jax.ShapeDtypeStruct((nbatch, WINDOW, ATTN_KV), F32),
            jax.ShapeDtypeStruct((nbatch, WINDOW, ATTN_KV), F32),
        ),
        compiler_params=pltpu.CompilerParams(dimension_semantics=("parallel",)),
        name="attn_decode",
    )(sinks, qmat, k_new, v_new, cache_k, cache_v)


def _mlstm_decode_kernel(q_ref, k_ref, v_ref, og_ref, gif_ref, c_ref, n_ref, m_ref, h_ref, co_ref, no_ref, mo_ref):
    bt = q_ref.shape[0]
    gif = gif_ref[...]
    ig = gif[:, 0:ML_HEADS]
    lf = gif[:, ML_HEADS:2 * ML_HEADS]
    m_prev = m_ref[...]
    a = lf + m_prev
    m_t = jnp.maximum(a, ig)
    aw = jnp.exp(a - m_t)
    e_i = jnp.exp(ig - m_t)
    floor = jnp.exp(-m_t)
    m_new = jnp.maximum(a, ig)
    decay = jnp.exp(a - m_new)
    w = jnp.exp(ig - m_new)
    mo_ref[...] = m_new
    pad = jnp.zeros((LANES - bt, ML_DIM), F32)
    lane = lax.broadcasted_iota(jnp.int32, (ML_DIM, LANES), 1)
    for h in range(ML_HEADS):
        hs = slice(h * ML_DIM, (h + 1) * ML_DIM)
        q = q_ref[:, hs]
        k = k_ref[:, hs]
        v = v_ref[:, hs]
        n = n_ref[:, h, :]
        qk = jnp.sum(q * k, axis=1, keepdims=True)
        qn = jnp.sum(q * n, axis=1, keepdims=True)
        sw = qk * e_i[:, h:h + 1]
        den = aw[:, h:h + 1] * qn + sw
        inv = 1.0 / jnp.maximum(jnp.abs(den), floor[:, h:h + 1])
        no_ref[:, h, :] = decay[:, h:h + 1] * n + w[:, h:h + 1] * k
        q_b = q.astype(BF16)
        wk_t = jnp.concatenate([w[:, h:h + 1] * k, pad], axis=0).T.astype(BF16)
        v_rows = jnp.concatenate([v, pad], axis=0).astype(BF16)
        nums = []
        for b in range(bt):
            C = c_ref[b, h]
            q_c = _mm(q_b, C.astype(BF16))[b:b + 1, :]
            nums.append(aw[b:b + 1, h:h + 1] * q_c)
            outer = _mm(jnp.where(lane == b, wk_t, jnp.zeros_like(wk_t)), v_rows)
            co_ref[b, h] = decay[b:b + 1, h:h + 1] * C + outer
        num = jnp.concatenate(nums, axis=0) + sw * v
        h_ref[:, hs] = (num * inv * og_ref[:, hs]).astype(h_ref.dtype)


def _mlstm_decode(mq, mk, mv, og, gif, state_c, state_n, state_m, bt):
    nbatch = mq.shape[0]
    b2 = lambda i: (i, 0)
    b3 = lambda i: (i, 0, 0)
    b4 = lambda i: (i, 0, 0, 0)
    return pl.pallas_call(
        _mlstm_decode_kernel,
        grid=(nbatch // bt,),
        in_specs=[
            pl.BlockSpec((bt, ML_W), b2),
            pl.BlockSpec((bt, ML_W), b2),
            pl.BlockSpec((bt, ML_W), b2),
            pl.BlockSpec((bt, ML_W), b2),
            pl.BlockSpec((bt, LANES), b2),
            pl.BlockSpec((bt, ML_HEADS, ML_DIM, ML_DIM), b4),
            pl.BlockSpec((bt, ML_HEADS, ML_DIM), b3),
            pl.BlockSpec((bt, ML_HEADS), b2),
        ],
        out_specs=(
            pl.BlockSpec((bt, ML_W), b2),
            pl.BlockSpec((bt, ML_HEADS, ML_DIM, ML_DIM), b4),
            pl.BlockSpec((bt, ML_HEADS, ML_DIM), b3),
            pl.BlockSpec((bt, ML_HEADS), b2),
        ),
        out_shape=(
            jax.ShapeDtypeStruct((nbatch, ML_W), BF16),
            jax.ShapeDtypeStruct((nbatch, ML_HEADS, ML_DIM, ML_DIM), F32),
            jax.ShapeDtypeStruct((nbatch, ML_HEADS, ML_DIM), F32),
            jax.ShapeDtypeStruct((nbatch, ML_HEADS), F32),
        ),
        compiler_params=pltpu.CompilerParams(dimension_semantics=("parallel",)),
        name="mlstm_decode",
    )(mq, mk, mv, og, gif, state_c, state_n, state_m)


def _rope_tables(pos):
    half = HEAD_DIM // 2
    inv = (1.0 / (ROPE_THETA ** (np.arange(half, dtype=np.float32) / half))).astype(np.float32)
    ang = pos.astype(np.float32)[:, None] * inv[None, :]
    cos = np.cos(ang)
    sin = np.sin(ang)
    cos_t = np.concatenate([cos, cos, cos, cos], axis=1)
    sin_t = np.concatenate([-sin, sin, -sin, sin], axis=1)
    return jnp.asarray(cos_t, F32), jnp.asarray(sin_t, F32)


def kernel(x_prompt, x_sample, cache_k, cache_v, state_C, state_n, state_m, p_prompt, p_sample, norm_mix, w_in, b_in, attn_sinks, w_attn_up, w_ml_up, w_o, norm_mlp, w_ff1, w_ff2, norm_ple, w_ple_gate, w_ple_proj, norm_final):
    assert w_in.shape[0] == 1, "single layer"
    batch, seq, _ = x_prompt.shape
    nbatch = x_sample.shape[0]
    assert x_sample.shape[1] == 1

    w_proj, w_gate, w_au, w_mu, w_out, w_f1, w_f2, w_pg, w_pp = _weight_cast(
        w_in[0].T, (w_attn_up[0], w_ml_up[0], w_o[0], w_ff1[0], w_ff2[0], w_ple_gate[0], w_ple_proj[0]))
    b_proj = b_in[:, :N_PROJ]
    b_gate = b_in[:, OFF_GA:]
    g_mix = norm_mix[0][None, :]
    out_weights = (
        g_mix, w_gate, b_gate, w_au, w_mu, w_out,
        norm_mlp[0][None, :], w_f1, w_f2,
        norm_ple[0][None, :], w_pg, w_pp, norm_final[None, :],
    )
    sinks = attn_sinks[0]
    to_fp = lambda c: c[0].transpose(0, 2, 3, 1).reshape(c.shape[1], ATTN_KV, WINDOW)
    from_fp = lambda c: c.reshape(c.shape[0], N_KV_HEADS, HEAD_DIM, WINDOW).transpose(0, 3, 1, 2)[None]

    cos_p, sin_p = _rope_tables(np.arange(seq, dtype=np.int32))
    xp = x_prompt.reshape(batch * seq, D_MODEL)
    y_prompt, k_last, v_last, cst, nst, mst = _prompt_fused(
        sinks, xp, p_prompt[0].reshape(batch * seq, D_PLE), cos_p, sin_p, g_mix, w_proj, b_proj, out_weights,
        batch, seq, tm=PROMPT_TILE)
    y_prompt = y_prompt.reshape(batch, seq, D_MODEL)
    k_prompt = from_fp(k_last)
    v_prompt = from_fp(v_last)
    c_prompt = cst[None]
    n_prompt = nst[:, :ML_HEADS, :][None]
    m_prompt = mst[:, :ML_HEADS, 0][None]

    cos_s, sin_s = _rope_tables(np.full((nbatch,), PAST_LEN, dtype=np.int32))
    xs = x_sample.reshape(nbatch, D_MODEL)
    qs, _, _, ks, vs, mqs, mks, mvs, ogs, gifs, _ = _in_proj_sample(xs, g_mix, w_proj, b_proj, cos_s, sin_s)
    qmat = qs.reshape(nbatch, N_Q_HEADS, ATTN_KV)
    o_full, k_s, v_s = _attn_decode(sinks, qmat, ks, vs, to_fp(cache_k), to_fp(cache_v), bt=DECODE_ROWS)
    o5 = o_full.reshape(nbatch, N_KV_HEADS, GQA_GROUP, N_KV_HEADS, HEAD_DIM)
    a_s = jnp.stack([o5[:, kv, :, kv, :] for kv in range(N_KV_HEADS)], axis=1).reshape(nbatch, ATTN_Q).astype(BF16)
    h_s, c_s, n_s, m_s = _mlstm_decode(mqs, mks, mvs, ogs, gifs, state_C[0], state_n[0], state_m[0], bt=DECODE_ROWS)
    y_sample = _out_proj_sample(xs, a_s, h_s, p_sample[0].reshape(nbatch, D_PLE), out_weights)
    y_sample = y_sample.reshape(nbatch, 1, D_MODEL)
    k_sample = from_fp(k_s)
    v_sample = from_fp(v_s)

    return (y_prompt, y_sample, k_prompt, v_prompt, c_prompt, n_prompt, m_prompt,
            k_sample, v_sample, c_s[None], n_s[None], m_s[None])
```

```python
import functools

import numpy as np
import jax
import jax.numpy as jnp
from jax import lax
from jax.experimental import pallas as pl
from jax.experimental.pallas import tpu as pltpu

F32 = jnp.float32
BF16 = jnp.bfloat16

D_MODEL = 1024
HEAD_DIM = 64
N_Q_HEADS = 8
N_KV_HEADS = 2
GQA_GROUP = N_Q_HEADS // N_KV_HEADS
WINDOW = 128
ROPE_THETA = 10000.0
ML_HEADS = 4
ML_DIM = 128
ML_CHUNK = 128
D_FF = 4 * D_MODEL
D_PLE = 256
EPS = 1e-6
PAST_LEN = 16384

ATTN_Q = N_Q_HEADS * HEAD_DIM
ATTN_KV = N_KV_HEADS * HEAD_DIM
ML_W = ML_HEADS * ML_DIM
LANES = 128

OFF_AQ = 0
OFF_AK = OFF_AQ + ATTN_Q
OFF_AV = OFF_AK + ATTN_KV
OFF_MQ = OFF_AV + ATTN_KV
OFF_MK = OFF_MQ + ML_W
OFF_MV = OFF_MK + ML_W
OFF_MO = OFF_MV + ML_W
OFF_MI = OFF_MO + ML_W
OFF_MF = OFF_MI + ML_HEADS
OFF_GA = OFF_MF + ML_HEADS
OFF_GM = OFF_GA + D_MODEL
D_IN = OFF_GM + D_MODEL
N_PROJ = OFF_MI + LANES

VMEM_LIMIT = 60 * 1024 * 1024
PROMPT_TILE = 256
DECODE_ROWS = 32

def _rms(x, g):
    r = lax.rsqrt(jnp.mean(x * x, axis=-1, keepdims=True) + EPS)
    return (x * r) * g


def _mm(a, b):
    return jnp.dot(a, b, preferred_element_type=F32)


def _log_sigmoid(x):
    return jnp.minimum(x, 0.0) - jnp.log1p(jnp.exp(-jnp.abs(x)))


def _weave(*streams):
    live = list(streams)
    while live:
        for g in list(live):
            try:
                next(g)
            except StopIteration:
                live.remove(g)


class _AwaitedRef:
    def __init__(self, ref, copy):
        self._ref, self._copy = ref, copy

    def __getitem__(self, idx):
        if self._copy is not None:
            self._copy.wait()
            self._copy = None
        return self._ref[idx]


WEIGHT_ROW_SPLIT = 8


def _weight_cast_kernel(*refs):
    n_plain = (len(refs) - 3) // 2
    w_in_t_ref, plain_in = refs[0], refs[1:1 + n_plain]
    w_proj_ref, w_gate_ref, plain_out = refs[1 + n_plain], refs[2 + n_plain], refs[3 + n_plain:]
    for g in range(N_PROJ // LANES):
        w_proj_ref[:, g * LANES:(g + 1) * LANES] = w_in_t_ref[g * LANES:(g + 1) * LANES, :].T.astype(BF16)
    for g in range(2 * D_MODEL // LANES):
        w_gate_ref[:, g * LANES:(g + 1) * LANES] = w_in_t_ref[OFF_GA + g * LANES:OFF_GA + (g + 1) * LANES, :].T.astype(BF16)
    for src, dst in zip(plain_in, plain_out):
        dst[...] = src[...].astype(BF16)


def _weight_cast(w_in_t, plain):
    split = WEIGHT_ROW_SPLIT
    assert D_MODEL // split == LANES
    rows_blk = lambda w: pl.BlockSpec((w.shape[0] // split, w.shape[1]), lambda i: (i, 0))
    out_shapes = (jax.ShapeDtypeStruct((D_MODEL, N_PROJ), BF16), jax.ShapeDtypeStruct((D_MODEL, 2 * D_MODEL), BF16),
                  *(jax.ShapeDtypeStruct(w.shape, BF16) for w in plain))
    return pl.pallas_call(
        _weight_cast_kernel,
        grid=(split,),
        in_specs=[pl.BlockSpec((D_IN, LANES), lambda i: (0, i))] + [rows_blk(w) for w in plain],
        out_specs=tuple(rows_blk(s) for s in out_shapes),
        out_shape=out_shapes,
        compiler_params=pltpu.CompilerParams(dimension_semantics=("parallel",), vmem_limit_bytes=VMEM_LIMIT),
        name="weight_cast",
    )(w_in_t, *plain)


def _in_proj_stages(x, g_ref, w_ref, b_ref, cos_ref, sin_ref,
                    q_ref, k_ref, v_ref, kl_ref, vl_ref, mq_ref, mk_ref, mv_ref, og_ref, gif_ref, gt_ref,
                    *, feature_major):
    xn = _rms(x, g_ref[...]).astype(BF16)
    tm = x.shape[0]

    def proj(lo, hi):
        return _mm(xn, w_ref[:, lo:hi]) + b_ref[:, lo:hi]

    cos = cos_ref[...]
    sin = sin_ref[...]
    lane = lax.broadcasted_iota(jnp.int32, (1, LANES), 1)
    first_half = (lane % HEAD_DIM) < (HEAD_DIM // 2)

    def rope(z):
        partner = jnp.where(first_half, pltpu.roll(z, LANES - HEAD_DIM // 2, 1), pltpu.roll(z, HEAD_DIM // 2, 1))
        return z * cos + partner * sin

    zq = proj(OFF_AQ, OFF_AK)
    for c in range(ATTN_Q // LANES):
        sl = slice(c * LANES, (c + 1) * LANES)
        qc = rope(zq[:, sl]) * (HEAD_DIM ** -0.5)
        if feature_major:
            q_ref[sl, :] = qc.T.astype(q_ref.dtype)
        else:
            kv = (2 * c) // GQA_GROUP
            own = (lane // HEAD_DIM) == kv
            swapped = pltpu.roll(qc, HEAD_DIM, 1)
            for j, src in enumerate((qc, swapped) if kv == 0 else (swapped, qc)):
                hd = 2 * c + j
                q_ref[:, hd * LANES:(hd + 1) * LANES] = jnp.where(own, src, 0.0).astype(q_ref.dtype)
    yield
    zkv = proj(OFF_AK, OFF_MQ)
    k = rope(zkv[:, :ATTN_KV])
    v = zkv[:, ATTN_KV:]
    k_ref[...] = k.astype(k_ref.dtype)
    v_ref[...] = (v.T if feature_major else v).astype(v_ref.dtype)
    if feature_major:
        kl_ref[0] = k[tm - WINDOW:, :].T
        vl_ref[0] = v[tm - WINDOW:, :].T
    else:
        kl_ref[...] = k[tm - WINDOW:, :]
        vl_ref[...] = v[tm - WINDOW:, :]
    yield
    for z_off, dst in ((OFF_MQ, mq_ref), (OFF_MV, mv_ref)):
        z = proj(z_off, z_off + ML_W)
        if feature_major:
            for h in range(ML_HEADS):
                hs = slice(h * ML_DIM, (h + 1) * ML_DIM)
                dst[0, hs, :] = z[:, hs].T.astype(dst.dtype)
        else:
            dst[...] = z.astype(dst.dtype)
        yield
    mk_ref[...] = (proj(OFF_MK, OFF_MV) * (ML_DIM ** -0.5)).astype(mk_ref.dtype)
    yield
    og_ref[...] = jax.nn.sigmoid(proj(OFF_MO, OFF_MI))
    yield
    zg = proj(OFF_MI, N_PROJ)
    gif = jnp.where(lane < ML_HEADS, zg, jnp.where(lane < 2 * ML_HEADS, _log_sigmoid(zg), 0.0))
    if feature_major:
        L = ML_CHUNK
        upper = (lax.broadcasted_iota(jnp.int32, (L, L), 0) <= lax.broadcasted_iota(jnp.int32, (L, L), 1)).astype(F32)
        row8 = lax.broadcasted_iota(jnp.int32, (8, 1), 0)
        fill = jnp.zeros((L - 16, L), F32)
        for c in range(tm // L):
            cs = slice(c * L, (c + 1) * L)
            g_rows = gif[cs, :].T[0:8, :]
            cum = jnp.dot(g_rows, upper, precision=lax.Precision.HIGHEST, preferred_element_type=F32)
            rows = jnp.where(row8 < ML_HEADS, g_rows, cum)
            gt_ref[0, :, cs] = rows
            diff = rows - pltpu.roll(rows, ML_HEADS, 0)
            gif_ref[cs, :] = jnp.concatenate([rows, diff, fill], axis=0).T
    else:
        gif_ref[...] = gif
        gt_ref[0] = jnp.zeros(gt_ref.shape[1:], F32)
    yield


def _in_proj_kernel(x_ref, *refs, feature_major):
    _weave(_in_proj_stages(x_ref[...], *refs, feature_major=feature_major))


def _in_proj_sample(x, g, w, b, cos, sin):
    rows = x.shape[0]
    assert rows == WINDOW
    full = lambda n, dt: (jax.ShapeDtypeStruct((rows, n), dt), pl.BlockSpec((rows, n), lambda i: (0, 0)))
    outs = (
        full(N_Q_HEADS * LANES, BF16), full(ATTN_KV, BF16), full(ATTN_KV, BF16), full(ATTN_KV, F32), full(ATTN_KV, F32),
        full(ML_W, F32), full(ML_W, F32), full(ML_W, F32), full(ML_W, F32), full(LANES, F32),
        (jax.ShapeDtypeStruct((1, 8, rows), F32), pl.BlockSpec((1, 8, rows), lambda i: (0, 0, 0))),
    )
    return pl.pallas_call(
        functools.partial(_in_proj_kernel, feature_major=False),
        grid=(1,),
        in_specs=[pl.BlockSpec(a.shape, lambda i: (0, 0)) for a in (x, g, w, b, cos, sin)],
        out_specs=tuple(o[1] for o in outs),
        out_shape=tuple(o[0] for o in outs),
        compiler_params=pltpu.CompilerParams(dimension_semantics=("arbitrary",), vmem_limit_bytes=VMEM_LIMIT),
        name="in_proj",
    )(x, g, w, b, cos, sin)


def _attn_stages(sink_ref, qt_ref, kp_ref, kc_ref, vtp_ref, vtc_ref, first_block, emit):
    L = WINDOW
    nsub = kc_ref.shape[0] // L
    key = lax.broadcasted_iota(jnp.int32, (2 * L, L), 0)
    t = lax.broadcasted_iota(jnp.int32, (2 * L, L), 1)
    is_prev = key < L
    allowed = (is_prev & (key >= t)) | (~is_prev & (key - L <= t))
    bias = jnp.where(allowed, 0.0, -jnp.inf)
    bias_first = jnp.where(is_prev & first_block, -jnp.inf, bias)
    zeros_q = jnp.zeros((HEAD_DIM, L), BF16)
    heads = range(N_Q_HEADS)
    for r in range(nsub):
        ts = slice(r * L, (r + 1) * L)
        k_prev = kp_ref[...] if r == 0 else kc_ref[(r - 1) * L:r * L, :]
        vt_prev = vtp_ref[...] if r == 0 else vtc_ref[:, (r - 1) * L:r * L]
        k2 = jnp.concatenate([k_prev, kc_ref[ts, :]], axis=0)
        vt2 = jnp.concatenate([vt_prev, vtc_ref[:, ts]], axis=1)
        b_r = bias_first if r == 0 else bias
        q_op = [jnp.concatenate([qt_ref[hd * HEAD_DIM:(hd + 1) * HEAD_DIM, ts], zeros_q][::1 if hd < GQA_GROUP else -1],
                                axis=0) for hd in heads]
        s = [_mm(k2, q_op[hd]) + b_r for hd in heads]
        yield
        m = [jnp.maximum(jnp.max(s[hd], axis=0, keepdims=True), sink_ref[hd]) for hd in heads]
        yield
        p = [jnp.exp(s[hd] - m[hd]) for hd in heads]
        yield
        den = [jnp.sum(p[hd], axis=0, keepdims=True) + jnp.exp(sink_ref[hd] - m[hd]) for hd in heads]
        o = [_mm(vt2[(hd // GQA_GROUP) * HEAD_DIM:(hd // GQA_GROUP + 1) * HEAD_DIM, :], p[hd].astype(BF16)) / den[hd]
             for hd in heads]
        yield
        emit(r, jnp.concatenate(o, axis=0).T)
        yield


ML_STATE_ROWS = ML_DIM + 8


def _mlstm_stages(chains, emit):
    L = ML_CHUNK
    s_idx = lax.broadcasted_iota(jnp.int32, (L, L), 0)
    t_idx = lax.broadcasted_iota(jnp.int32, (L, L), 1)
    causal = s_idx <= t_idx
    n = range(len(chains))
    dlog = [jnp.where(causal, ch["b_r"] + ch["u_c"], -jnp.inf) for ch in chains]
    yield
    qk = [_mm(ch["k"], ch["qt"]) for ch in chains]
    inter = [_mm(ch["state"].astype(BF16), ch["qt"]) for ch in chains]
    yield
    a = [ch["b_r"] + ch["m_prev"] for ch in chains]
    m_t = [jnp.maximum(a[i], jnp.max(dlog[i], axis=0, keepdims=True)) for i in n]
    yield
    sw = [qk[i] * jnp.exp(dlog[i] - m_t[i]) for i in n]
    yield
    sv = [_mm(chains[i]["vt"], sw[i].astype(BF16)) for i in n]
    yield
    hs = []
    for i in n:
        aw = jnp.exp(a[i] - m_t[i])
        num = aw * inter[i][:ML_DIM] + sv[i]
        den = aw * inter[i][ML_DIM:ML_DIM + 1] + jnp.sum(sw[i], axis=0, keepdims=True)
        hh = num / jnp.maximum(jnp.abs(den), jnp.exp(-m_t[i]))
        hs.append(hh.T * chains[i]["og"])
    yield
    pad = jnp.zeros((ML_STATE_ROWS - ML_DIM - 1, L), F32)
    new_state, new_m = [], []
    for i in n:
        ch = chains[i]
        b_end = ch["b_r"][:, L - 1:L]
        g = b_end - ch["b_r"] + ch["ig_r"]
        m_new = jnp.maximum(b_end + ch["m_prev"], jnp.max(g, axis=1, keepdims=True))
        decay = jnp.exp(b_end + ch["m_prev"] - m_new)
        w = jnp.exp(g - m_new)
        v_aug = jnp.concatenate([ch["vt"].astype(F32) * w, w, pad], axis=0).astype(BF16)
        new_state.append(decay * ch["state"] + _mm(v_aug, ch["k"]))
        new_m.append(m_new)
    emit(hs, new_state, new_m)
    yield


DENSE_COLS = 512


def _dense_stages(x, a, hm, p, gmix_ref, wg_ref, bg_ref, wau_ref, wmu_ref, wo_ref,
                  gmlp_ref, wff1_ref, wff2_ref, gple_ref, wpg_ref, wpp_ref, gfin_ref, emit):
    nc = DENSE_COLS
    pieces = lambda n: [slice(c, c + nc) for c in range(0, n, nc)]
    xn = _rms(x, gmix_ref[...]).astype(BF16)
    gates = []
    for cs in pieces(2 * D_MODEL):
        gates.append(jax.nn.sigmoid(_mm(xn, wg_ref[:, cs]) + bg_ref[:, cs]))
        yield
    half = D_MODEL // nc
    mix = []
    for j, cs in enumerate(pieces(D_MODEL)):
        mix.append((gates[j] * _mm(a, wau_ref[:, cs]) + gates[half + j] * _mm(hm, wmu_ref[:, cs])).astype(BF16))
        yield
    mix = jnp.concatenate(mix, axis=1)
    h = []
    for cs in pieces(D_MODEL):
        h.append(x[:, cs] + _mm(mix, wo_ref[:, cs]))
        yield
    h = jnp.concatenate(h, axis=1)
    hn = _rms(h, gmlp_ref[...]).astype(BF16)
    u = []
    for cs in pieces(D_FF):
        u.append(jnp.square(jnp.maximum(_mm(hn, wff1_ref[:, cs]), 0.0)).astype(BF16))
        yield
    u = jnp.concatenate(u, axis=1)
    h2 = []
    for cs in pieces(D_MODEL):
        h2.append(h[:, cs] + _mm(u, wff2_ref[:, cs]))
        yield
    h = jnp.concatenate(h2, axis=1)
    hn = _rms(h, gple_ref[...]).astype(BF16)
    pb = p.astype(BF16)
    h3 = []
    for cs in pieces(D_MODEL):
        h3.append(h[:, cs] + _mm(pb, wpp_ref[:, cs]) * jax.nn.sigmoid(_mm(hn, wpg_ref[:, cs])))
        yield
    emit(_rms(jnp.concatenate(h3, axis=1), gfin_ref[...]))
    yield


def _out_kernel(x_ref, a_ref, hm_ref, p_ref, *rest, matrix_idx):
    weights, y_ref = list(rest[:N_FUSED_WEIGHTS]), rest[N_FUSED_WEIGHTS]
    w_scr, dma_sem = rest[N_FUSED_WEIGHTS + 1:-1], rest[-1]
    copies = [pltpu.make_async_copy(weights[j], w_scr[n], dma_sem.at[n]) for n, j in enumerate(matrix_idx)]
    for c in copies:
        c.start()
    for n, j in enumerate(matrix_idx):
        weights[j] = _AwaitedRef(w_scr[n], copies[n])

    def emit(y):
        y_ref[...] = y

    _weave(_dense_stages(x_ref[...], a_ref[...], hm_ref[...], p_ref[...], *weights, emit=emit))


def _out_proj_sample(x, a, hm, p, weights):
    assert len(weights) == N_FUSED_WEIGHTS
    matrix_idx = tuple(j for j, w in enumerate(weights) if w.dtype == BF16)
    matrices = [weights[j] for j in matrix_idx]
    whole = lambda o: pl.BlockSpec(o.shape, lambda i: (0, 0), pipeline_mode=pl.Buffered(1))
    return pl.pallas_call(
        functools.partial(_out_kernel, matrix_idx=matrix_idx),
        grid=(1,),
        in_specs=[whole(o) for o in (x, a, hm, p)]
        + [pl.BlockSpec(memory_space=pl.ANY) if j in matrix_idx else whole(w) for j, w in enumerate(weights)],
        out_specs=pl.BlockSpec(x.shape, lambda i: (0, 0)),
        out_shape=jax.ShapeDtypeStruct(x.shape, F32),
        scratch_shapes=[*(pltpu.VMEM(w.shape, w.dtype) for w in matrices), pltpu.SemaphoreType.DMA((len(matrices),))],
        compiler_params=pltpu.CompilerParams(dimension_semantics=("arbitrary",), vmem_limit_bytes=VMEM_LIMIT),
        name="out_proj",
    )(x, a, hm, p, *weights)


N_FUSED_WEIGHTS = 13
N_PROJ_SLOTS = 9


def _prompt_kernel(sink_ref, x0_ref, cos_ref, sin_ref, x2_ref, p_ref, gmix_ref, wp_ref, bp_ref, *rest,
                   tiles_per_seq, matrix_idx):
    weights = list(rest[:N_FUSED_WEIGHTS])
    y_ref, kl_ref, vl_ref, cst_ref, nst_ref, mst_ref = rest[N_FUSED_WEIGHTS:N_FUSED_WEIGHTS + 6]
    scr = rest[N_FUSED_WEIGHTS + 6:]
    proj_scr = scr[:N_PROJ_SLOTS]
    kprev_scr, vtprev_scr, a_scr, h_scr, c_scr, m_scr = scr[N_PROJ_SLOTS:N_PROJ_SLOTS + 6]
    w_scr, dma_sem = scr[N_PROJ_SLOTS + 6:-1], scr[-1]
    w_copies = [pltpu.make_async_copy(weights[j], w_scr[n], dma_sem.at[n]) for n, j in enumerate(matrix_idx)]
    for n, j in enumerate(matrix_idx):
        weights[j] = w_scr[n]
    i = pl.program_id(0)
    n_tiles = pl.num_programs(0) - 2
    t1 = jnp.clip(i - 1, 0, n_tiles - 1)
    tile_in_seq = t1 % tiles_per_seq
    s0, s1 = i % 2, (i + 1) % 2
    L = ML_CHUNK
    tm = x0_ref.shape[0]

    @pl.when(i == 0)
    def _():
        for c in w_copies:
            c.start()
        for r in (*proj_scr, kprev_scr, vtprev_scr, a_scr, h_scr, c_scr, m_scr):
            r[...] = jnp.zeros_like(r)

    @pl.when(i == 2)
    def _():
        for c in w_copies:
            c.wait()

    @pl.when(tile_in_seq == 0)
    def _():
        c_scr[...] = jnp.zeros_like(c_scr)
        m_scr[...] = jnp.zeros_like(m_scr)

    def projection():
        outs = [r.at[s0] for r in proj_scr]
        yield from _in_proj_stages(x0_ref[...], gmix_ref, wp_ref, bp_ref, cos_ref, sin_ref,
                                   outs[0], outs[1], outs[2], kl_ref, vl_ref, *outs[3:], feature_major=True)

    def branches():
        qt_ref, kc_ref, vtc_ref, mqt_ref, mk_ref, mvt_ref, og_ref, gif_ref, gt_ref = [r.at[s1] for r in proj_scr]
        slot = s1

        def emit_attn(r, blk):
            a_scr[slot, r * WINDOW:(r + 1) * WINDOW, :] = blk.astype(a_scr.dtype)

        yield from _attn_stages(sink_ref, qt_ref, kprev_scr, kc_ref, vtprev_scr, vtc_ref, tile_in_seq == 0, emit_attn)
        kprev_scr[...] = kc_ref[tm - WINDOW:, :]
        vtprev_scr[...] = vtc_ref[:, tm - WINDOW:]
        hsl = lambda h: slice(h * ML_DIM, (h + 1) * ML_DIM)
        carry = dict(state=[c_scr[h] for h in range(ML_HEADS)], m=[m_scr[h:h + 1, :] for h in range(ML_HEADS)])
        for c in range(tm // L):
            cs = slice(c * L, (c + 1) * L)

            def emit_ml(hs, new_state, new_m, cs=cs):
                for h in range(ML_HEADS):
                    h_scr[slot, cs, hsl(h)] = hs[h].astype(h_scr.dtype)
                carry["state"], carry["m"] = new_state, new_m

            chains = [dict(qt=mqt_ref[0, hsl(h), cs], k=mk_ref[cs, hsl(h)], vt=mvt_ref[0, hsl(h), cs],
                           og=og_ref[cs, hsl(h)], ig_r=gt_ref[0, h:h + 1, cs],
                           b_r=gt_ref[0, ML_HEADS + h:ML_HEADS + h + 1, cs],
                           u_c=gif_ref[cs, 2 * ML_HEADS + h:2 * ML_HEADS + h + 1],
                           state=carry["state"][h], m_prev=carry["m"][h]) for h in range(ML_HEADS)]
            yield from _mlstm_stages(chains, emit_ml)
        for h in range(ML_HEADS):
            c_scr[h] = carry["state"][h]
            m_scr[h:h + 1, :] = carry["m"][h]

    def dense():
        def emit_y(y):
            y_ref[...] = y

        return _dense_stages(x2_ref[...], a_scr[s0], h_scr[s0], p_ref[...], *weights, emit=emit_y)

    @pl.when(i < 2)
    def _():
        _weave(branches(), projection())

    @pl.when((i >= 2) & (i < n_tiles))
    def _():
        _weave(branches(), dense(), projection())

    @pl.when(i >= n_tiles)
    def _():
        _weave(branches(), dense())

    @pl.when((tile_in_seq == tiles_per_seq - 1) & (i >= 1) & (i <= n_tiles))
    def _():
        nst_ref[...] = jnp.zeros_like(nst_ref)
        for h in range(ML_HEADS):
            final = c_scr[h]
            cst_ref[0, h] = final[:ML_DIM].T
            nst_ref[0, h:h + 1, :] = final[ML_DIM:ML_DIM + 1]
        mst_ref[0] = m_scr[...]


def _prompt_fused(sinks, x, p, cos, sin, g_mix, w_proj, b_proj, weights, batch, seq, tm):
    assert len(weights) == N_FUSED_WEIGHTS
    rows = batch * seq
    n_tiles = rows // tm
    tps = seq // tm
    t0 = lambda i: jnp.minimum(i, n_tiles - 1)
    t1 = lambda i: jnp.clip(i - 1, 0, n_tiles - 1)
    t2 = lambda i: jnp.maximum(i - 2, 0)
    const = lambda i: (0, 0)
    single = lambda a: pl.BlockSpec(a.shape, const, pipeline_mode=pl.Buffered(1))
    slot = lambda shape, dt: pltpu.VMEM((2,) + shape, dt)
    matrix_idx = tuple(j for j, w in enumerate(weights) if w.dtype == BF16)
    matrices = [weights[j] for j in matrix_idx]
    return pl.pallas_call(
        functools.partial(_prompt_kernel, tiles_per_seq=tps, matrix_idx=matrix_idx),
        grid=(n_tiles + 2,),
        in_specs=[
            pl.BlockSpec(memory_space=pltpu.SMEM),
            pl.BlockSpec((tm, D_MODEL), lambda i: (t0(i), 0)),
            pl.BlockSpec((tm, LANES), lambda i: (t0(i) % tps, 0)),
            pl.BlockSpec((tm, LANES), lambda i: (t0(i) % tps, 0)),
            pl.BlockSpec((tm, D_MODEL), lambda i: (t2(i), 0)),
            pl.BlockSpec((tm, D_PLE), lambda i: (t2(i), 0)),
            single(g_mix), single(w_proj), single(b_proj),
        ] + [pl.BlockSpec(memory_space=pl.ANY) if j in matrix_idx else single(w) for j, w in enumerate(weights)],
        out_specs=(
            pl.BlockSpec((tm, D_MODEL), lambda i: (t2(i), 0)),
            pl.BlockSpec((1, ATTN_KV, WINDOW), lambda i: (t0(i) // tps, 0, 0)),
            pl.BlockSpec((1, ATTN_KV, WINDOW), lambda i: (t0(i) // tps, 0, 0)),
            pl.BlockSpec((1, ML_HEADS, ML_DIM, ML_DIM), lambda i: (t1(i) // tps, 0, 0, 0)),
            pl.BlockSpec((1, 8, LANES), lambda i: (t1(i) // tps, 0, 0)),
            pl.BlockSpec((1, 8, LANES), lambda i: (t1(i) // tps, 0, 0)),
        ),
        out_shape=(
            jax.ShapeDtypeStruct((rows, D_MODEL), F32),
            jax.ShapeDtypeStruct((batch, ATTN_KV, WINDOW), F32),
            jax.ShapeDtypeStruct((batch, ATTN_KV, WINDOW), F32),
            jax.ShapeDtypeStruct((batch, ML_HEADS, ML_DIM, ML_DIM), F32),
            jax.ShapeDtypeStruct((batch, 8, LANES), F32),
            jax.ShapeDtypeStruct((batch, 8, LANES), F32),
        ),
        scratch_shapes=[
            slot((ATTN_Q, tm), BF16), slot((tm, ATTN_KV), BF16), slot((ATTN_KV, tm), BF16),
            slot((1, ML_W, tm), BF16), slot((tm, ML_W), BF16), slot((1, ML_W, tm), BF16),
            slot((tm, ML_W), F32), slot((tm, LANES), F32), slot((1, 8, tm), F32),
            pltpu.VMEM((WINDOW, ATTN_KV), BF16), pltpu.VMEM((ATTN_KV, WINDOW), BF16),
            slot((tm, ATTN_Q), BF16), slot((tm, ML_W), BF16),
            pltpu.VMEM((ML_HEADS, ML_STATE_ROWS, ML_DIM), F32), pltpu.VMEM((8, LANES), F32),
            *(pltpu.VMEM(w.shape, w.dtype) for w in matrices), pltpu.SemaphoreType.DMA((len(matrices),)),
        ],
        compiler_params=pltpu.CompilerParams(dimension_semantics=("arbitrary",), vmem_limit_bytes=VMEM_LIMIT),
        name="prompt_fused",
    )(sinks, x, cos, sin, x, p, g_mix, w_proj, b_proj, *weights)


def _attn_decode_kernel(sink_ref, qm_ref, kn_ref, vn_ref, ck_ref, cv_ref, o_ref, ko_ref, vo_ref, o_scr):
    bt = qm_ref.shape[0]
    qm = qm_ref[...]
    kn = kn_ref[...]
    vn = vn_ref[...]
    ck = ck_ref[...]
    cv = cv_ref[...]
    head = lax.broadcasted_iota(jnp.int32, (N_Q_HEADS, 1), 0)
    sink = jnp.zeros((N_Q_HEADS, 1), F32)
    for hd in range(N_Q_HEADS):
        sink = jnp.where(head == hd, sink_ref[hd], sink)
    s = jnp.einsum('bhc,bck->bhk', qm, ck.astype(BF16), preferred_element_type=F32)
    s_new = jnp.sum(qm.astype(F32) * kn[:, None, :], axis=2, keepdims=True)
    m = jnp.maximum(jnp.maximum(jnp.max(s, axis=2, keepdims=True), s_new), sink)
    p = jnp.exp(s - m)
    p_new = jnp.exp(s_new - m)
    den = jnp.sum(p, axis=2, keepdims=True) + p_new + jnp.exp(sink - m)
    o = jnp.einsum('bhk,bck->bhc', p.astype(BF16), cv.astype(BF16), preferred_element_type=F32)
    o_scr[...] = (o + p_new * vn[:, None, :]) / den
    low = lax.broadcasted_iota(jnp.int32, (1, LANES), 1) < HEAD_DIM
    for c in range(N_Q_HEADS // 2):
        even, odd = o_scr[:, 2 * c, :], o_scr[:, 2 * c + 1, :]
        if (2 * c) // GQA_GROUP == 0:
            odd = pltpu.roll(odd, HEAD_DIM, 1)
        else:
            even = pltpu.roll(even, HEAD_DIM, 1)
        o_ref[:, c * LANES:(c + 1) * LANES] = jnp.where(low, even, odd).astype(o_ref.dtype)
    pad = jnp.zeros((LANES - bt, ATTN_KV), F32)
    kn_t = jnp.concatenate([kn, pad], axis=0).T
    vn_t = jnp.concatenate([vn, pad], axis=0).T
    newest = lax.broadcasted_iota(jnp.int32, (ATTN_KV, WINDOW), 1) == WINDOW - 1
    for b in range(bt):
        ko_ref[b] = jnp.where(newest, kn_t[:, b:b + 1], pltpu.roll(ck[b], WINDOW - 1, 1))
        vo_ref[b] = jnp.where(newest, vn_t[:, b:b + 1], pltpu.roll(cv[b], WINDOW - 1, 1))


def _attn_decode(sinks, qmat, k_new, v_new, cache_k, cache_v, bt):
    nbatch = qmat.shape[0]
    b3 = lambda i: (i, 0, 0)
    b2 = lambda i: (i, 0)
    return pl.pallas_call(
        _attn_decode_kernel,
        grid=(nbatch // bt,),
        in_specs=[
            pl.BlockSpec(memory_space=pltpu.SMEM),
            pl.BlockSpec((bt, N_Q_HEADS, ATTN_KV), b3),
            pl.BlockSpec((bt, ATTN_KV), b2),
            pl.BlockSpec((bt, ATTN_KV), b2),
            pl.BlockSpec((bt, WINDOW, ATTN_KV), b3),
            pl.BlockSpec((bt, WINDOW, ATTN_KV), b3),
        ],
        out_specs=(
            pl.BlockSpec((bt, ATTN_Q), b2),
            pl.BlockSpec((bt, WINDOW, ATTN_KV), b3),
            pl.BlockSpec((bt, WINDOW, ATTN_KV), b3),
        ),
        out_shape=(
            jax.ShapeDtypeStruct((nbatch, ATTN_Q), BF16),
            jax.ShapeDtypeStruct((nbatch, WINDOW, ATTN_KV), F32),
            jax.ShapeDtypeStruct((nbatch, WINDOW, ATTN_KV), F32),
        ),
        scratch_shapes=[pltpu.VMEM((bt, N_Q_HEADS, ATTN_KV), F32)],
        compiler_params=pltpu.CompilerParams(dimension_semantics=("parallel",)),
        name="attn_decode",
    )(sinks, qmat, k_new, v_new, cache_k, cache_v)


def _mlstm_decode_kernel(q_ref, k_ref, v_ref, og_ref, gif_ref, c_ref, n_ref, m_ref, h_ref, co_ref, no_ref, mo_ref):
    bt = q_ref.shape[0]
    gif = gif_ref[...]
    ig = gif[:, 0:ML_HEADS]
    lf = gif[:, ML_HEADS:2 * ML_HEADS]
    m_prev = m_ref[...]
    a = lf + m_prev
    m_t = jnp.maximum(a, ig)
    aw = jnp.exp(a - m_t)
    e_i = jnp.exp(ig - m_t)
    floor = jnp.exp(-m_t)
    m_new = jnp.maximum(a, ig)
    decay = jnp.exp(a - m_new)
    w = jnp.exp(ig - m_new)
    mo_ref[...] = m_new
    pad = jnp.zeros((LANES - bt, ML_DIM), F32)
    lane = lax.broadcasted_iota(jnp.int32, (ML_DIM, LANES), 1)
    for h in range(ML_HEADS):
        hs = slice(h * ML_DIM, (h + 1) * ML_DIM)
        q = q_ref[:, hs]
        k = k_ref[:, hs]
        v = v_ref[:, hs]
        n = n_ref[:, h, :]
        qk = jnp.sum(q * k, axis=1, keepdims=True)
        qn = jnp.sum(q * n, axis=1, keepdims=True)
        sw = qk * e_i[:, h:h + 1]
        den = aw[:, h:h + 1] * qn + sw
        inv = 1.0 / jnp.maximum(jnp.abs(den), floor[:, h:h + 1])
        no_ref[:, h, :] = decay[:, h:h + 1] * n + w[:, h:h + 1] * k
        q_b = q.astype(BF16)
        wk_t = jnp.concatenate([w[:, h:h + 1] * k, pad], axis=0).T.astype(BF16)
        v_rows = jnp.concatenate([v, pad], axis=0).astype(BF16)
        nums = []
        for b in range(bt):
            C = c_ref[b, h]
            q_c = _mm(q_b, C.astype(BF16))[b:b + 1, :]
            nums.append(aw[b:b + 1, h:h + 1] * q_c)
            outer = _mm(jnp.where(lane == b, wk_t, jnp.zeros_like(wk_t)), v_rows)
            co_ref[b, h] = decay[b:b + 1, h:h + 1] * C + outer
        num = jnp.concatenate(nums, axis=0) + sw * v
        h_ref[:, hs] = (num * inv * og_ref[:, hs]).astype(h_ref.dtype)


def _mlstm_decode(mq, mk, mv, og, gif, state_c, state_n, state_m, bt):
    nbatch = mq.shape[0]
    b2 = lambda i: (i, 0)
    b3 = lambda i: (i, 0, 0)
    b4 = lambda i: (i, 0, 0, 0)
    return pl.pallas_call(
        _mlstm_decode_kernel,
        grid=(nbatch // bt,),
        in_specs=[
            pl.BlockSpec((bt, ML_W), b2),
            pl.BlockSpec((bt, ML_W), b2),
            pl.BlockSpec((bt, ML_W), b2),
            pl.BlockSpec((bt, ML_W), b2),
            pl.BlockSpec((bt, LANES), b2),
            pl.BlockSpec((bt, ML_HEADS, ML_DIM, ML_DIM), b4),
            pl.BlockSpec((bt, ML_HEADS, ML_DIM), b3),
            pl.BlockSpec((bt, ML_HEADS), b2),
        ],
        out_specs=(
            pl.BlockSpec((bt, ML_W), b2),
            pl.BlockSpec((bt, ML_HEADS, ML_DIM, ML_DIM), b4),
            pl.BlockSpec((bt, ML_HEADS, ML_DIM), b3),
            pl.BlockSpec((bt, ML_HEADS), b2),
        ),
        out_shape=(
            jax.ShapeDtypeStruct((nbatch, ML_W), BF16),
            jax.ShapeDtypeStruct((nbatch, ML_HEADS, ML_DIM, ML_DIM), F32),
            jax.ShapeDtypeStruct((nbatch, ML_HEADS, ML_DIM), F32),
            jax.ShapeDtypeStruct((nbatch, ML_HEADS), F32),
        ),
        compiler_params=pltpu.CompilerParams(dimension_semantics=("parallel",)),
        name="mlstm_decode",
    )(mq, mk, mv, og, gif, state_c, state_n, state_m)


def _rope_tables(pos):
    half = HEAD_DIM // 2
    inv = (1.0 / (ROPE_THETA ** (np.arange(half, dtype=np.float32) / half))).astype(np.float32)
    ang = pos.astype(np.float32)[:, None] * inv[None, :]
    cos = np.cos(ang)
    sin = np.sin(ang)
    cos_t = np.concatenate([cos, cos, cos, cos], axis=1)
    sin_t = np.concatenate([-sin, sin, -sin, sin], axis=1)
    return jnp.asarray(cos_t, F32), jnp.asarray(sin_t, F32)


def kernel(x_prompt, x_sample, cache_k, cache_v, state_C, state_n, state_m, p_prompt, p_sample, norm_mix, w_in, b_in, attn_sinks, w_attn_up, w_ml_up, w_o, norm_mlp, w_ff1, w_ff2, norm_ple, w_ple_gate, w_ple_proj, norm_final):
    assert w_in.shape[0] == 1, "single layer"
    batch, seq, _ = x_prompt.shape
    nbatch = x_sample.shape[0]
    assert x_sample.shape[1] == 1

    w_proj, w_gate, w_au, w_mu, w_out, w_f1, w_f2, w_pg, w_pp = _weight_cast(
        w_in[0].T, (w_attn_up[0], w_ml_up[0], w_o[0], w_ff1[0], w_ff2[0], w_ple_gate[0], w_ple_proj[0]))
    b_proj = b_in[:, :N_PROJ]
    b_gate = b_in[:, OFF_GA:]
    g_mix = norm_mix[0][None, :]
    out_weights = (
        g_mix, w_gate, b_gate, w_au, w_mu, w_out,
        norm_mlp[0][None, :], w_f1, w_f2,
        norm_ple[0][None, :], w_pg, w_pp, norm_final[None, :],
    )
    sinks = attn_sinks[0]
    to_fp = lambda c: c[0].transpose(0, 2, 3, 1).reshape(c.shape[1], ATTN_KV, WINDOW)
    from_fp = lambda c: c.reshape(c.shape[0], N_KV_HEADS, HEAD_DIM, WINDOW).transpose(0, 3, 1, 2)[None]

    cos_p, sin_p = _rope_tables(np.arange(seq, dtype=np.int32))
    xp = x_prompt.reshape(batch * seq, D_MODEL)
    y_prompt, k_last, v_last, cst, nst, mst = _prompt_fused(
        sinks, xp, p_prompt[0].reshape(batch * seq, D_PLE), cos_p, sin_p, g_mix, w_proj, b_proj, out_weights,
        batch, seq, tm=PROMPT_TILE)
    y_prompt = y_prompt.reshape(batch, seq, D_MODEL)
    k_prompt = from_fp(k_last)
    v_prompt = from_fp(v_last)
    c_prompt = cst[None]
    n_prompt = nst[:, :ML_HEADS, :][None]
    m_prompt = mst[:, :ML_HEADS, 0][None]

    cos_s, sin_s = _rope_tables(np.full((nbatch,), PAST_LEN, dtype=np.int32))
    xs = x_sample.reshape(nbatch, D_MODEL)
    qs, _, _, ks, vs, mqs, mks, mvs, ogs, gifs, _ = _in_proj_sample(xs, g_mix, w_proj, b_proj, cos_s, sin_s)
    qmat = qs.reshape(nbatch, N_Q_HEADS, ATTN_KV)
    a_s, k_s, v_s = _attn_decode(sinks, qmat, ks, vs, to_fp(cache_k), to_fp(cache_v), bt=DECODE_ROWS)
    h_s, c_s, n_s, m_s = _mlstm_decode(mqs, mks, mvs, ogs, gifs, state_C[0], state_n[0], state_m[0], bt=DECODE_ROWS)
    y_sample = _out_proj_sample(xs, a_s, h_s, p_sample[0].reshape(nbatch, D_PLE), out_weights)
    y_sample = y_sample.reshape(nbatch, 1, D_MODEL)
    k_sample = from_fp(k_s)
    v_sample = from_fp(v_s)

    return (y_prompt, y_sample, k_prompt, v_prompt, c_prompt, n_prompt, m_prompt,
            k_sample, v_sample, c_s[None], n_s[None], m_s[None])
```

```python
import functools

import numpy as np
import jax
import jax.numpy as jnp
from jax import lax
from jax.experimental import pallas as pl
from jax.experimental.pallas import tpu as pltpu

F32 = jnp.float32
BF16 = jnp.bfloat16

D_MODEL = 1024
HEAD_DIM = 64
N_Q_HEADS = 8
N_KV_HEADS = 2
GQA_GROUP = N_Q_HEADS // N_KV_HEADS
WINDOW = 128
ROPE_THETA = 10000.0
ML_HEADS = 4
ML_DIM = 128
ML_CHUNK = 128
D_FF = 4 * D_MODEL
D_PLE = 256
EPS = 1e-6
PAST_LEN = 16384

ATTN_Q = N_Q_HEADS * HEAD_DIM
ATTN_KV = N_KV_HEADS * HEAD_DIM
ML_W = ML_HEADS * ML_DIM
LANES = 128

OFF_AQ = 0
OFF_AK = OFF_AQ + ATTN_Q
OFF_AV = OFF_AK + ATTN_KV
OFF_MQ = OFF_AV + ATTN_KV
OFF_MK = OFF_MQ + ML_W
OFF_MV = OFF_MK + ML_W
OFF_MO = OFF_MV + ML_W
OFF_MI = OFF_MO + ML_W
OFF_MF = OFF_MI + ML_HEADS
OFF_GA = OFF_MF + ML_HEADS
OFF_GM = OFF_GA + D_MODEL
D_IN = OFF_GM + D_MODEL
N_PROJ = OFF_MI + LANES

VMEM_LIMIT = 60 * 1024 * 1024
PROMPT_TILE = 256
DECODE_ROWS = 32

def _rms(x, g):
    r = lax.rsqrt(jnp.mean(x * x, axis=-1, keepdims=True) + EPS)
    return (x * r) * g


def _mm(a, b):
    return jnp.dot(a, b, preferred_element_type=F32)


def _log_sigmoid(x):
    return jnp.minimum(x, 0.0) - jnp.log1p(jnp.exp(-jnp.abs(x)))


def _weave(*streams):
    live = list(streams)
    while live:
        for g in list(live):
            try:
                next(g)
            except StopIteration:
                live.remove(g)


class _AwaitedRef:
    def __init__(self, ref, copy):
        self._ref, self._copy = ref, copy

    def __getitem__(self, idx):
        if self._copy is not None:
            self._copy.wait()
            self._copy = None
        return self._ref[idx]


WEIGHT_ROW_SPLIT = 8


def _weight_cast_kernel(*refs):
    n_plain = (len(refs) - 3) // 2
    w_in_t_ref, plain_in = refs[0], refs[1:1 + n_plain]
    w_proj_ref, w_gate_ref, plain_out = refs[1 + n_plain], refs[2 + n_plain], refs[3 + n_plain:]
    for g in range(N_PROJ // LANES):
        w_proj_ref[:, g * LANES:(g + 1) * LANES] = w_in_t_ref[g * LANES:(g + 1) * LANES, :].T.astype(BF16)
    for g in range(2 * D_MODEL // LANES):
        w_gate_ref[:, g * LANES:(g + 1) * LANES] = w_in_t_ref[OFF_GA + g * LANES:OFF_GA + (g + 1) * LANES, :].T.astype(BF16)
    for src, dst in zip(plain_in, plain_out):
        dst[...] = src[...].astype(BF16)


def _weight_cast(w_in_t, plain):
    split = WEIGHT_ROW_SPLIT
    assert D_MODEL // split == LANES
    rows_blk = lambda w: pl.BlockSpec((w.shape[0] // split, w.shape[1]), lambda i: (i, 0))
    out_shapes = (jax.ShapeDtypeStruct((D_MODEL, N_PROJ), BF16), jax.ShapeDtypeStruct((D_MODEL, 2 * D_MODEL), BF16),
                  *(jax.ShapeDtypeStruct(w.shape, BF16) for w in plain))
    return pl.pallas_call(
        _weight_cast_kernel,
        grid=(split,),
        in_specs=[pl.BlockSpec((D_IN, LANES), lambda i: (0, i))] + [rows_blk(w) for w in plain],
        out_specs=tuple(rows_blk(s) for s in out_shapes),
        out_shape=out_shapes,
        compiler_params=pltpu.CompilerParams(dimension_semantics=("parallel",), vmem_limit_bytes=VMEM_LIMIT),
        name="weight_cast",
    )(w_in_t, *plain)


def _in_proj_stages(x, g_ref, w_ref, b_ref, cos_ref, sin_ref,
                    q_ref, k_ref, v_ref, kl_ref, vl_ref, mq_ref, mk_ref, mv_ref, og_ref, gif_ref, gt_ref,
                    *, feature_major):
    xn = _rms(x, g_ref[...]).astype(BF16)
    tm = x.shape[0]

    def proj(lo, hi):
        return _mm(xn, w_ref[:, lo:hi]) + b_ref[:, lo:hi]

    cos = cos_ref[...]
    sin = sin_ref[...]
    lane = lax.broadcasted_iota(jnp.int32, (1, LANES), 1)
    first_half = (lane % HEAD_DIM) < (HEAD_DIM // 2)

    def rope(z):
        partner = jnp.where(first_half, pltpu.roll(z, LANES - HEAD_DIM // 2, 1), pltpu.roll(z, HEAD_DIM // 2, 1))
        return z * cos + partner * sin

    zq = proj(OFF_AQ, OFF_AK)
    for c in range(ATTN_Q // LANES):
        sl = slice(c * LANES, (c + 1) * LANES)
        qc = rope(zq[:, sl]) * (HEAD_DIM ** -0.5)
        if feature_major:
            q_ref[sl, :] = qc.T.astype(q_ref.dtype)
        else:
            kv = (2 * c) // GQA_GROUP
            own = (lane // HEAD_DIM) == kv
            swapped = pltpu.roll(qc, HEAD_DIM, 1)
            for j, src in enumerate((qc, swapped) if kv == 0 else (swapped, qc)):
                hd = 2 * c + j
                q_ref[:, hd * LANES:(hd + 1) * LANES] = jnp.where(own, src, 0.0).astype(q_ref.dtype)
    yield
    zkv = proj(OFF_AK, OFF_MQ)
    k = rope(zkv[:, :ATTN_KV])
    v = zkv[:, ATTN_KV:]
    k_ref[...] = k.astype(k_ref.dtype)
    v_ref[...] = (v.T if feature_major else v).astype(v_ref.dtype)
    if feature_major:
        kl_ref[0] = k[tm - WINDOW:, :].T
        vl_ref[0] = v[tm - WINDOW:, :].T
    else:
        kl_ref[...] = k[tm - WINDOW:, :]
        vl_ref[...] = v[tm - WINDOW:, :]
    yield
    for z_off, dst in ((OFF_MQ, mq_ref), (OFF_MV, mv_ref)):
        z = proj(z_off, z_off + ML_W)
        if feature_major:
            for h in range(ML_HEADS):
                hs = slice(h * ML_DIM, (h + 1) * ML_DIM)
                dst[0, hs, :] = z[:, hs].T.astype(dst.dtype)
        else:
            dst[...] = z.astype(dst.dtype)
        yield
    mk_ref[...] = (proj(OFF_MK, OFF_MV) * (ML_DIM ** -0.5)).astype(mk_ref.dtype)
    yield
    og_ref[...] = jax.nn.sigmoid(proj(OFF_MO, OFF_MI))
    yield
    zg = proj(OFF_MI, N_PROJ)
    gif = jnp.where(lane < ML_HEADS, zg, jnp.where(lane < 2 * ML_HEADS, _log_sigmoid(zg), 0.0))
    if feature_major:
        L = ML_CHUNK
        upper = (lax.broadcasted_iota(jnp.int32, (L, L), 0) <= lax.broadcasted_iota(jnp.int32, (L, L), 1)).astype(F32)
        row8 = lax.broadcasted_iota(jnp.int32, (8, 1), 0)
        fill = jnp.zeros((L - 16, L), F32)
        for c in range(tm // L):
            cs = slice(c * L, (c + 1) * L)
            g_rows = gif[cs, :].T[0:8, :]
            cum = jnp.dot(g_rows, upper, precision=lax.Precision.HIGHEST, preferred_element_type=F32)
            rows = jnp.where(row8 < ML_HEADS, g_rows, cum)
            gt_ref[0, :, cs] = rows
            diff = rows - pltpu.roll(rows, ML_HEADS, 0)
            gif_ref[cs, :] = jnp.concatenate([rows, diff, fill], axis=0).T
    else:
        gif_ref[...] = gif
        gt_ref[0] = jnp.zeros(gt_ref.shape[1:], F32)
    yield


def _in_proj_kernel(x_ref, *refs, feature_major):
    _weave(_in_proj_stages(x_ref[...], *refs, feature_major=feature_major))


def _in_proj_sample(x, g, w, b, cos, sin):
    rows = x.shape[0]
    assert rows == WINDOW
    full = lambda n, dt: (jax.ShapeDtypeStruct((rows, n), dt), pl.BlockSpec((rows, n), lambda i: (0, 0)))
    outs = (
        full(N_Q_HEADS * LANES, BF16), full(ATTN_KV, BF16), full(ATTN_KV, BF16), full(ATTN_KV, F32), full(ATTN_KV, F32),
        full(ML_W, F32), full(ML_W, F32), full(ML_W, F32), full(ML_W, F32), full(LANES, F32),
        (jax.ShapeDtypeStruct((1, 8, rows), F32), pl.BlockSpec((1, 8, rows), lambda i: (0, 0, 0))),
    )
    return pl.pallas_call(
        functools.partial(_in_proj_kernel, feature_major=False),
        grid=(1,),
        in_specs=[pl.BlockSpec(a.shape, lambda i: (0, 0)) for a in (x, g, w, b, cos, sin)],
        out_specs=tuple(o[1] for o in outs),
        out_shape=tuple(o[0] for o in outs),
        compiler_params=pltpu.CompilerParams(dimension_semantics=("arbitrary",), vmem_limit_bytes=VMEM_LIMIT),
        name="in_proj",
    )(x, g, w, b, cos, sin)


def _attn_stages(sink_ref, qt_ref, kp_ref, kc_ref, vtp_ref, vtc_ref, first_block, emit):
    L = WINDOW
    nsub = kc_ref.shape[0] // L
    key = lax.broadcasted_iota(jnp.int32, (2 * L, L), 0)
    t = lax.broadcasted_iota(jnp.int32, (2 * L, L), 1)
    is_prev = key < L
    allowed = (is_prev & (key >= t)) | (~is_prev & (key - L <= t))
    bias = jnp.where(allowed, 0.0, -jnp.inf)
    bias_first = jnp.where(is_prev & first_block, -jnp.inf, bias)
    zeros_q = jnp.zeros((HEAD_DIM, L), BF16)
    heads = range(N_Q_HEADS)
    for r in range(nsub):
        ts = slice(r * L, (r + 1) * L)
        k_prev = kp_ref[...] if r == 0 else kc_ref[(r - 1) * L:r * L, :]
        vt_prev = vtp_ref[...] if r == 0 else vtc_ref[:, (r - 1) * L:r * L]
        k2 = jnp.concatenate([k_prev, kc_ref[ts, :]], axis=0)
        vt2 = jnp.concatenate([vt_prev, vtc_ref[:, ts]], axis=1)
        b_r = bias_first if r == 0 else bias
        q_op = [jnp.concatenate([qt_ref[hd * HEAD_DIM:(hd + 1) * HEAD_DIM, ts], zeros_q][::1 if hd < GQA_GROUP else -1],
                                axis=0) for hd in heads]
        s = [_mm(k2, q_op[hd]) + b_r for hd in heads]
        yield
        m = [jnp.maximum(jnp.max(s[hd], axis=0, keepdims=True), sink_ref[hd]) for hd in heads]
        yield
        p = [jnp.exp(s[hd] - m[hd]) for hd in heads]
        yield
        den = [jnp.sum(p[hd], axis=0, keepdims=True) + jnp.exp(sink_ref[hd] - m[hd]) for hd in heads]
        o = [_mm(vt2[(hd // GQA_GROUP) * HEAD_DIM:(hd // GQA_GROUP + 1) * HEAD_DIM, :], p[hd].astype(BF16)) / den[hd]
             for hd in heads]
        yield
        emit(r, jnp.concatenate(o, axis=0).T)
        yield


ML_STATE_ROWS = ML_DIM + 8


def _mlstm_stages(chains, emit):
    L = ML_CHUNK
    s_idx = lax.broadcasted_iota(jnp.int32, (L, L), 0)
    t_idx = lax.broadcasted_iota(jnp.int32, (L, L), 1)
    causal = s_idx <= t_idx
    n = range(len(chains))
    dlog = [jnp.where(causal, ch["b_r"] + ch["u_c"], -jnp.inf) for ch in chains]
    yield
    qk = [_mm(ch["k"], ch["qt"]) for ch in chains]
    inter = [_mm(ch["state"].astype(BF16), ch["qt"]) for ch in chains]
    yield
    a = [ch["b_r"] + ch["m_prev"] for ch in chains]
    m_t = [jnp.maximum(a[i], jnp.max(dlog[i], axis=0, keepdims=True)) for i in n]
    yield
    sw = [qk[i] * jnp.exp(dlog[i] - m_t[i]) for i in n]
    yield
    sv = [_mm(chains[i]["vt"], sw[i].astype(BF16)) for i in n]
    yield
    hs = []
    for i in n:
        aw = jnp.exp(a[i] - m_t[i])
        num = aw * inter[i][:ML_DIM] + sv[i]
        den = aw * inter[i][ML_DIM:ML_DIM + 1] + jnp.sum(sw[i], axis=0, keepdims=True)
        hh = num / jnp.maximum(jnp.abs(den), jnp.exp(-m_t[i]))
        hs.append(hh.T * chains[i]["og"])
    yield
    pad = jnp.zeros((ML_STATE_ROWS - ML_DIM - 1, L), F32)
    new_state, new_m = [], []
    for i in n:
        ch = chains[i]
        b_end = ch["b_r"][:, L - 1:L]
        g = b_end - ch["b_r"] + ch["ig_r"]
        m_new = jnp.maximum(b_end + ch["m_prev"], jnp.max(g, axis=1, keepdims=True))
        decay = jnp.exp(b_end + ch["m_prev"] - m_new)
        w = jnp.exp(g - m_new)
        v_aug = jnp.concatenate([ch["vt"].astype(F32) * w, w, pad], axis=0).astype(BF16)
        new_state.append(decay * ch["state"] + _mm(v_aug, ch["k"]))
        new_m.append(m_new)
    emit(hs, new_state, new_m)
    yield


DENSE_COLS = 512


def _dense_stages(x, a, hm, p, gmix_ref, wg_ref, bg_ref, wau_ref, wmu_ref, wo_ref,
                  gmlp_ref, wff1_ref, wff2_ref, gple_ref, wpg_ref, wpp_ref, gfin_ref, emit):
    nc = DENSE_COLS
    pieces = lambda n: [slice(c, c + nc) for c in range(0, n, nc)]
    xn = _rms(x, gmix_ref[...]).astype(BF16)
    gates = []
    for cs in pieces(2 * D_MODEL):
        gates.append(jax.nn.sigmoid(_mm(xn, wg_ref[:, cs]) + bg_ref[:, cs]))
        yield
    half = D_MODEL // nc
    mix = []
    for j, cs in enumerate(pieces(D_MODEL)):
        mix.append((gates[j] * _mm(a, wau_ref[:, cs]) + gates[half + j] * _mm(hm, wmu_ref[:, cs])).astype(BF16))
        yield
    mix = jnp.concatenate(mix, axis=1)
    h = []
    for cs in pieces(D_MODEL):
        h.append(x[:, cs] + _mm(mix, wo_ref[:, cs]))
        yield
    h = jnp.concatenate(h, axis=1)
    hn = _rms(h, gmlp_ref[...]).astype(BF16)
    u = []
    for cs in pieces(D_FF):
        u.append(jnp.square(jnp.maximum(_mm(hn, wff1_ref[:, cs]), 0.0)).astype(BF16))
        yield
    u = jnp.concatenate(u, axis=1)
    h2 = []
    for cs in pieces(D_MODEL):
        h2.append(h[:, cs] + _mm(u, wff2_ref[:, cs]))
        yield
    h = jnp.concatenate(h2, axis=1)
    hn = _rms(h, gple_ref[...]).astype(BF16)
    pb = p.astype(BF16)
    h3 = []
    for cs in pieces(D_MODEL):
        h3.append(h[:, cs] + _mm(pb, wpp_ref[:, cs]) * jax.nn.sigmoid(_mm(hn, wpg_ref[:, cs])))
        yield
    emit(_rms(jnp.concatenate(h3, axis=1), gfin_ref[...]))
    yield


def _out_kernel(x_ref, a_ref, hm_ref, p_ref, *rest, matrix_idx):
    weights, y_ref = list(rest[:N_FUSED_WEIGHTS]), rest[N_FUSED_WEIGHTS]
    w_scr, dma_sem = rest[N_FUSED_WEIGHTS + 1:-1], rest[-1]
    copies = [pltpu.make_async_copy(weights[j], w_scr[n], dma_sem.at[n]) for n, j in enumerate(matrix_idx)]
    for c in copies:
        c.start()
    for n, j in enumerate(matrix_idx):
        weights[j] = _AwaitedRef(w_scr[n], copies[n])

    def emit(y):
        y_ref[...] = y

    _weave(_dense_stages(x_ref[...], a_ref[...], hm_ref[...], p_ref[...], *weights, emit=emit))


def _out_proj_sample(x, a, hm, p, weights):
    assert len(weights) == N_FUSED_WEIGHTS
    matrix_idx = tuple(j for j, w in enumerate(weights) if w.dtype == BF16)
    matrices = [weights[j] for j in matrix_idx]
    whole = lambda o: pl.BlockSpec(o.shape, lambda i: (0, 0), pipeline_mode=pl.Buffered(1))
    return pl.pallas_call(
        functools.partial(_out_kernel, matrix_idx=matrix_idx),
        grid=(1,),
        in_specs=[whole(o) for o in (x, a, hm, p)]
        + [pl.BlockSpec(memory_space=pl.ANY) if j in matrix_idx else whole(w) for j, w in enumerate(weights)],
        out_specs=pl.BlockSpec(x.shape, lambda i: (0, 0)),
        out_shape=jax.ShapeDtypeStruct(x.shape, F32),
        scratch_shapes=[*(pltpu.VMEM(w.shape, w.dtype) for w in matrices), pltpu.SemaphoreType.DMA((len(matrices),))],
        compiler_params=pltpu.CompilerParams(dimension_semantics=("arbitrary",), vmem_limit_bytes=VMEM_LIMIT),
        name="out_proj",
    )(x, a, hm, p, *weights)


N_FUSED_WEIGHTS = 13
N_PROJ_SLOTS = 9


def _prompt_kernel(sink_ref, x0_ref, cos_ref, sin_ref, x2_ref, p_ref, gmix_ref, wp_ref, bp_ref, *rest,
                   tiles_per_seq, matrix_idx):
    weights = list(rest[:N_FUSED_WEIGHTS])
    y_ref, kl_ref, vl_ref, cst_ref, nst_ref, mst_ref = rest[N_FUSED_WEIGHTS:N_FUSED_WEIGHTS + 6]
    scr = rest[N_FUSED_WEIGHTS + 6:]
    proj_scr = scr[:N_PROJ_SLOTS]
    kprev_scr, vtprev_scr, a_scr, h_scr, c_scr, m_scr = scr[N_PROJ_SLOTS:N_PROJ_SLOTS + 6]
    w_scr, dma_sem = scr[N_PROJ_SLOTS + 6:-1], scr[-1]
    w_copies = [pltpu.make_async_copy(weights[j], w_scr[n], dma_sem.at[n]) for n, j in enumerate(matrix_idx)]
    for n, j in enumerate(matrix_idx):
        weights[j] = w_scr[n]
    i = pl.program_id(0)
    n_tiles = pl.num_programs(0) - 2
    t1 = jnp.clip(i - 1, 0, n_tiles - 1)
    tile_in_seq = t1 % tiles_per_seq
    s0, s1 = i % 2, (i + 1) % 2
    L = ML_CHUNK
    tm = x0_ref.shape[0]

    @pl.when(i == 0)
    def _():
        for c in w_copies:
            c.start()
        for r in (*proj_scr, kprev_scr, vtprev_scr, a_scr, h_scr, c_scr, m_scr):
            r[...] = jnp.zeros_like(r)

    @pl.when(i == 2)
    def _():
        for c in w_copies:
            c.wait()

    @pl.when(tile_in_seq == 0)
    def _():
        c_scr[...] = jnp.zeros_like(c_scr)
        m_scr[...] = jnp.zeros_like(m_scr)

    def projection():
        outs = [r.at[s0] for r in proj_scr]
        yield from _in_proj_stages(x0_ref[...], gmix_ref, wp_ref, bp_ref, cos_ref, sin_ref,
                                   outs[0], outs[1], outs[2], kl_ref, vl_ref, *outs[3:], feature_major=True)

    def branches():
        qt_ref, kc_ref, vtc_ref, mqt_ref, mk_ref, mvt_ref, og_ref, gif_ref, gt_ref = [r.at[s1] for r in proj_scr]
        slot = s1

        def emit_attn(r, blk):
            a_scr[slot, r * WINDOW:(r + 1) * WINDOW, :] = blk.astype(a_scr.dtype)

        yield from _attn_stages(sink_ref, qt_ref, kprev_scr, kc_ref, vtprev_scr, vtc_ref, tile_in_seq == 0, emit_attn)
        kprev_scr[...] = kc_ref[tm - WINDOW:, :]
        vtprev_scr[...] = vtc_ref[:, tm - WINDOW:]
        hsl = lambda h: slice(h * ML_DIM, (h + 1) * ML_DIM)
        carry = dict(state=[c_scr[h] for h in range(ML_HEADS)], m=[m_scr[h:h + 1, :] for h in range(ML_HEADS)])
        for c in range(tm // L):
            cs = slice(c * L, (c + 1) * L)

            def emit_ml(hs, new_state, new_m, cs=cs):
                for h in range(ML_HEADS):
                    h_scr[slot, cs, hsl(h)] = hs[h].astype(h_scr.dtype)
                carry["state"], carry["m"] = new_state, new_m

            chains = [dict(qt=mqt_ref[0, hsl(h), cs], k=mk_ref[cs, hsl(h)], vt=mvt_ref[0, hsl(h), cs],
                           og=og_ref[cs, hsl(h)], ig_r=gt_ref[0, h:h + 1, cs],
                           b_r=gt_ref[0, ML_HEADS + h:ML_HEADS + h + 1, cs],
                           u_c=gif_ref[cs, 2 * ML_HEADS + h:2 * ML_HEADS + h + 1],
                           state=carry["state"][h], m_prev=carry["m"][h]) for h in range(ML_HEADS)]
            yield from _mlstm_stages(chains, emit_ml)
        for h in range(ML_HEADS):
            c_scr[h] = carry["state"][h]
            m_scr[h:h + 1, :] = carry["m"][h]

    def dense():
        def emit_y(y):
            y_ref[...] = y

        return _dense_stages(x2_ref[...], a_scr[s0], h_scr[s0], p_ref[...], *weights, emit=emit_y)

    @pl.when(i < 2)
    def _():
        _weave(branches(), projection())

    @pl.when((i >= 2) & (i < n_tiles))
    def _():
        _weave(branches(), dense(), projection())

    @pl.when(i >= n_tiles)
    def _():
        _weave(branches(), dense())

    @pl.when((tile_in_seq == tiles_per_seq - 1) & (i >= 1) & (i <= n_tiles))
    def _():
        nst_ref[...] = jnp.zeros_like(nst_ref)
        for h in range(ML_HEADS):
            final = c_scr[h]
            cst_ref[0, h] = final[:ML_DIM].T
            nst_ref[0, h:h + 1, :] = final[ML_DIM:ML_DIM + 1]
        mst_ref[0] = m_scr[...]


def _prompt_fused(sinks, x, p, cos, sin, g_mix, w_proj, b_proj, weights, batch, seq, tm):
    assert len(weights) == N_FUSED_WEIGHTS
    rows = batch * seq
    n_tiles = rows // tm
    tps = seq // tm
    t0 = lambda i: jnp.minimum(i, n_tiles - 1)
    t1 = lambda i: jnp.clip(i - 1, 0, n_tiles - 1)
    t2 = lambda i: jnp.maximum(i - 2, 0)
    const = lambda i: (0, 0)
    single = lambda a: pl.BlockSpec(a.shape, const, pipeline_mode=pl.Buffered(1))
    slot = lambda shape, dt: pltpu.VMEM((2,) + shape, dt)
    matrix_idx = tuple(j for j, w in enumerate(weights) if w.dtype == BF16)
    matrices = [weights[j] for j in matrix_idx]
    return pl.pallas_call(
        functools.partial(_prompt_kernel, tiles_per_seq=tps, matrix_idx=matrix_idx),
        grid=(n_tiles + 2,),
        in_specs=[
            pl.BlockSpec(memory_space=pltpu.SMEM),
            pl.BlockSpec((tm, D_MODEL), lambda i: (t0(i), 0)),
            pl.BlockSpec((tm, LANES), lambda i: (t0(i) % tps, 0)),
            pl.BlockSpec((tm, LANES), lambda i: (t0(i) % tps, 0)),
            pl.BlockSpec((tm, D_MODEL), lambda i: (t2(i), 0)),
            pl.BlockSpec((tm, D_PLE), lambda i: (t2(i), 0)),
            single(g_mix), single(w_proj), single(b_proj),
        ] + [pl.BlockSpec(memory_space=pl.ANY) if j in matrix_idx else single(w) for j, w in enumerate(weights)],
        out_specs=(
            pl.BlockSpec((tm, D_MODEL), lambda i: (t2(i), 0)),
            pl.BlockSpec((1, ATTN_KV, WINDOW), lambda i: (t0(i) // tps, 0, 0)),
            pl.BlockSpec((1, ATTN_KV, WINDOW), lambda i: (t0(i) // tps, 0, 0)),
            pl.BlockSpec((1, ML_HEADS, ML_DIM, ML_DIM), lambda i: (t1(i) // tps, 0, 0, 0)),
            pl.BlockSpec((1, 8, LANES), lambda i: (t1(i) // tps, 0, 0)),
            pl.BlockSpec((1, 8, LANES), lambda i: (t1(i) // tps, 0, 0)),
        ),
        out_shape=(
            jax.ShapeDtypeStruct((rows, D_MODEL), F32),
            jax.ShapeDtypeStruct((batch, ATTN_KV, WINDOW), F32),
            jax.ShapeDtypeStruct((batch, ATTN_KV, WINDOW), F32),
            jax.ShapeDtypeStruct((batch, ML_HEADS, ML_DIM, ML_DIM), F32),
            jax.ShapeDtypeStruct((batch, 8, LANES), F32),
            jax.ShapeDtypeStruct((batch, 8, LANES), F32),
        ),
        scratch_shapes=[
            slot((ATTN_Q, tm), BF16), slot((tm, ATTN_KV), BF16), slot((ATTN_KV, tm), BF16),
            slot((1, ML_W, tm), BF16), slot((tm, ML_W), BF16), slot((1, ML_W, tm), BF16),
            slot((tm, ML_W), F32), slot((tm, LANES), F32), slot((1, 8, tm), F32),
            pltpu.VMEM((WINDOW, ATTN_KV), BF16), pltpu.VMEM((ATTN_KV, WINDOW), BF16),
            slot((tm, ATTN_Q), BF16), slot((tm, ML_W), BF16),
            pltpu.VMEM((ML_HEADS, ML_STATE_ROWS, ML_DIM), F32), pltpu.VMEM((8, LANES), F32),
            *(pltpu.VMEM(w.shape, w.dtype) for w in matrices), pltpu.SemaphoreType.DMA((len(matrices),)),
        ],
        compiler_params=pltpu.CompilerParams(dimension_semantics=("arbitrary",), vmem_limit_bytes=VMEM_LIMIT),
        name="prompt_fused",
    )(sinks, x, cos, sin, x, p, g_mix, w_proj, b_proj, *weights)


def _attn_decode_kernel(sink_ref, q_ref, kn_ref, vn_ref, ck_ref, cv_ref, o_ref, ko_ref, vo_ref, o_scr, qm_scr):
    bt = q_ref.shape[0]
    for hd in range(N_Q_HEADS):
        qm_scr[:, hd, :] = q_ref[:, hd * LANES:(hd + 1) * LANES]
    qm = qm_scr[...]
    kn = kn_ref[...]
    vn = vn_ref[...]
    ck = ck_ref[...]
    cv = cv_ref[...]
    head = lax.broadcasted_iota(jnp.int32, (N_Q_HEADS, 1), 0)
    sink = jnp.zeros((N_Q_HEADS, 1), F32)
    for hd in range(N_Q_HEADS):
        sink = jnp.where(head == hd, sink_ref[hd], sink)
    s = jnp.einsum('bhc,bck->bhk', qm, ck.astype(BF16), preferred_element_type=F32)
    s_new = jnp.sum(qm.astype(F32) * kn[:, None, :], axis=2, keepdims=True)
    m = jnp.maximum(jnp.maximum(jnp.max(s, axis=2, keepdims=True), s_new), sink)
    p = jnp.exp(s - m)
    p_new = jnp.exp(s_new - m)
    den = jnp.sum(p, axis=2, keepdims=True) + p_new + jnp.exp(sink - m)
    o = jnp.einsum('bhk,bck->bhc', p.astype(BF16), cv.astype(BF16), preferred_element_type=F32)
    o_scr[...] = (o + p_new * vn[:, None, :]) / den
    low = lax.broadcasted_iota(jnp.int32, (1, LANES), 1) < HEAD_DIM
    for c in range(N_Q_HEADS // 2):
        even, odd = o_scr[:, 2 * c, :], o_scr[:, 2 * c + 1, :]
        if (2 * c) // GQA_GROUP == 0:
            odd = pltpu.roll(odd, HEAD_DIM, 1)
        else:
            even = pltpu.roll(even, HEAD_DIM, 1)
        o_ref[:, c * LANES:(c + 1) * LANES] = jnp.where(low, even, odd).astype(o_ref.dtype)
    pad = jnp.zeros((LANES - bt, ATTN_KV), F32)
    kn_t = jnp.concatenate([kn, pad], axis=0).T
    vn_t = jnp.concatenate([vn, pad], axis=0).T
    newest = lax.broadcasted_iota(jnp.int32, (ATTN_KV, WINDOW), 1) == WINDOW - 1
    for b in range(bt):
        ko_ref[b] = jnp.where(newest, kn_t[:, b:b + 1], pltpu.roll(ck[b], WINDOW - 1, 1))
        vo_ref[b] = jnp.where(newest, vn_t[:, b:b + 1], pltpu.roll(cv[b], WINDOW - 1, 1))


def _attn_decode(sinks, q_heads, k_new, v_new, cache_k, cache_v, bt):
    nbatch = q_heads.shape[0]
    b3 = lambda i: (i, 0, 0)
    b2 = lambda i: (i, 0)
    return pl.pallas_call(
        _attn_decode_kernel,
        grid=(nbatch // bt,),
        in_specs=[
            pl.BlockSpec(memory_space=pltpu.SMEM),
            pl.BlockSpec((bt, N_Q_HEADS * LANES), b2),
            pl.BlockSpec((bt, ATTN_KV), b2),
            pl.BlockSpec((bt, ATTN_KV), b2),
            pl.BlockSpec((bt, WINDOW, ATTN_KV), b3),
            pl.BlockSpec((bt, WINDOW, ATTN_KV), b3),
        ],
        out_specs=(
            pl.BlockSpec((bt, ATTN_Q), b2),
            pl.BlockSpec((bt, WINDOW, ATTN_KV), b3),
            pl.BlockSpec((bt, WINDOW, ATTN_KV), b3),
        ),
        out_shape=(
            jax.ShapeDtypeStruct((nbatch, ATTN_Q), BF16),
            jax.ShapeDtypeStruct((nbatch, WINDOW, ATTN_KV), F32),
            jax.ShapeDtypeStruct((nbatch, WINDOW, ATTN_KV), F32),
        ),
        scratch_shapes=[pltpu.VMEM((bt, N_Q_HEADS, ATTN_KV), F32), pltpu.VMEM((bt, N_Q_HEADS, ATTN_KV), BF16)],
        compiler_params=pltpu.CompilerParams(dimension_semantics=("parallel",)),
        name="attn_decode",
    )(sinks, q_heads, k_new, v_new, cache_k, cache_v)


def _mlstm_decode_kernel(q_ref, k_ref, v_ref, og_ref, gif_ref, c_ref, n_ref, m_ref, h_ref, co_ref, no_ref, mo_ref):
    bt = q_ref.shape[0]
    gif = gif_ref[...]
    ig = gif[:, 0:ML_HEADS]
    lf = gif[:, ML_HEADS:2 * ML_HEADS]
    m_prev = m_ref[...]
    a = lf + m_prev
    m_t = jnp.maximum(a, ig)
    aw = jnp.exp(a - m_t)
    e_i = jnp.exp(ig - m_t)
    floor = jnp.exp(-m_t)
    m_new = jnp.maximum(a, ig)
    decay = jnp.exp(a - m_new)
    w = jnp.exp(ig - m_new)
    mo_ref[...] = m_new
    pad = jnp.zeros((LANES - bt, ML_DIM), F32)
    lane = lax.broadcasted_iota(jnp.int32, (ML_DIM, LANES), 1)
    for h in range(ML_HEADS):
        hs = slice(h * ML_DIM, (h + 1) * ML_DIM)
        q = q_ref[:, hs]
        k = k_ref[:, hs]
        v = v_ref[:, hs]
        n = n_ref[:, h, :]
        qk = jnp.sum(q * k, axis=1, keepdims=True)
        qn = jnp.sum(q * n, axis=1, keepdims=True)
        sw = qk * e_i[:, h:h + 1]
        den = aw[:, h:h + 1] * qn + sw
        inv = 1.0 / jnp.maximum(jnp.abs(den), floor[:, h:h + 1])
        no_ref[:, h, :] = decay[:, h:h + 1] * n + w[:, h:h + 1] * k
        q_b = q.astype(BF16)
        wk_t = jnp.concatenate([w[:, h:h + 1] * k, pad], axis=0).T.astype(BF16)
        v_rows = jnp.concatenate([v, pad], axis=0).astype(BF16)
        nums = []
        for b in range(bt):
            C = c_ref[b, h]
            q_c = _mm(q_b, C.astype(BF16))[b:b + 1, :]
            nums.append(aw[b:b + 1, h:h + 1] * q_c)
            outer = _mm(jnp.where(lane == b, wk_t, jnp.zeros_like(wk_t)), v_rows)
            co_ref[b, h] = decay[b:b + 1, h:h + 1] * C + outer
        num = jnp.concatenate(nums, axis=0) + sw * v
        h_ref[:, hs] = (num * inv * og_ref[:, hs]).astype(h_ref.dtype)


def _mlstm_decode(mq, mk, mv, og, gif, state_c, state_n, state_m, bt):
    nbatch = mq.shape[0]
    b2 = lambda i: (i, 0)
    b3 = lambda i: (i, 0, 0)
    b4 = lambda i: (i, 0, 0, 0)
    return pl.pallas_call(
        _mlstm_decode_kernel,
        grid=(nbatch // bt,),
        in_specs=[
            pl.BlockSpec((bt, ML_W), b2),
            pl.BlockSpec((bt, ML_W), b2),
            pl.BlockSpec((bt, ML_W), b2),
            pl.BlockSpec((bt, ML_W), b2),
            pl.BlockSpec((bt, LANES), b2),
            pl.BlockSpec((bt, ML_HEADS, ML_DIM, ML_DIM), b4),
            pl.BlockSpec((bt, ML_HEADS, ML_DIM), b3),
            pl.BlockSpec((bt, ML_HEADS), b2),
        ],
        out_specs=(
            pl.BlockSpec((bt, ML_W), b2),
            pl.BlockSpec((bt, ML_HEADS, ML_DIM, ML_DIM), b4),
            pl.BlockSpec((bt, ML_HEADS, ML_DIM), b3),
            pl.BlockSpec((bt, ML_HEADS), b2),
        ),
        out_shape=(
            jax.ShapeDtypeStruct((nbatch, ML_W), BF16),
            jax.ShapeDtypeStruct((nbatch, ML_HEADS, ML_DIM, ML_DIM), F32),
            jax.ShapeDtypeStruct((nbatch, ML_HEADS, ML_DIM), F32),
            jax.ShapeDtypeStruct((nbatch, ML_HEADS), F32),
        ),
        compiler_params=pltpu.CompilerParams(dimension_semantics=("parallel",)),
        name="mlstm_decode",
    )(mq, mk, mv, og, gif, state_c, state_n, state_m)


def _rope_tables(pos):
    half = HEAD_DIM // 2
    inv = (1.0 / (ROPE_THETA ** (np.arange(half, dtype=np.float32) / half))).astype(np.float32)
    ang = pos.astype(np.float32)[:, None] * inv[None, :]
    cos = np.cos(ang)
    sin = np.sin(ang)
    cos_t = np.concatenate([cos, cos, cos, cos], axis=1)
    sin_t = np.concatenate([-sin, sin, -sin, sin], axis=1)
    return jnp.asarray(cos_t, F32), jnp.asarray(sin_t, F32)


def kernel(x_prompt, x_sample, cache_k, cache_v, state_C, state_n, state_m, p_prompt, p_sample, norm_mix, w_in, b_in, attn_sinks, w_attn_up, w_ml_up, w_o, norm_mlp, w_ff1, w_ff2, norm_ple, w_ple_gate, w_ple_proj, norm_final):
    assert w_in.shape[0] == 1, "single layer"
    batch, seq, _ = x_prompt.shape
    nbatch = x_sample.shape[0]
    assert x_sample.shape[1] == 1

    w_proj, w_gate, w_au, w_mu, w_out, w_f1, w_f2, w_pg, w_pp = _weight_cast(
        w_in[0].T, (w_attn_up[0], w_ml_up[0], w_o[0], w_ff1[0], w_ff2[0], w_ple_gate[0], w_ple_proj[0]))
    b_proj = b_in[:, :N_PROJ]
    b_gate = b_in[:, OFF_GA:]
    g_mix = norm_mix[0][None, :]
    out_weights = (
        g_mix, w_gate, b_gate, w_au, w_mu, w_out,
        norm_mlp[0][None, :], w_f1, w_f2,
        norm_ple[0][None, :], w_pg, w_pp, norm_final[None, :],
    )
    sinks = attn_sinks[0]
    to_fp = lambda c: c[0].transpose(0, 2, 3, 1).reshape(c.shape[1], ATTN_KV, WINDOW)
    from_fp = lambda c: c.reshape(c.shape[0], N_KV_HEADS, HEAD_DIM, WINDOW).transpose(0, 3, 1, 2)[None]

    cos_p, sin_p = _rope_tables(np.arange(seq, dtype=np.int32))
    xp = x_prompt.reshape(batch * seq, D_MODEL)
    y_prompt, k_last, v_last, cst, nst, mst = _prompt_fused(
        sinks, xp, p_prompt[0].reshape(batch * seq, D_PLE), cos_p, sin_p, g_mix, w_proj, b_proj, out_weights,
        batch, seq, tm=PROMPT_TILE)
    y_prompt = y_prompt.reshape(batch, seq, D_MODEL)
    k_prompt = from_fp(k_last)
    v_prompt = from_fp(v_last)
    c_prompt = cst[None]
    n_prompt = nst[:, :ML_HEADS, :][None]
    m_prompt = mst[:, :ML_HEADS, 0][None]

    cos_s, sin_s = _rope_tables(np.full((nbatch,), PAST_LEN, dtype=np.int32))
    xs = x_sample.reshape(nbatch, D_MODEL)
    qs, _, _, ks, vs, mqs, mks, mvs, ogs, gifs, _ = _in_proj_sample(xs, g_mix, w_proj, b_proj, cos_s, sin_s)
    a_s, k_s, v_s = _attn_decode(sinks, qs, ks, vs, to_fp(cache_k), to_fp(cache_v), bt=DECODE_ROWS)
    h_s, c_s, n_s, m_s = _mlstm_decode(mqs, mks, mvs, ogs, gifs, state_C[0], state_n[0], state_m[0], bt=DECODE_ROWS)
    y_sample = _out_proj_sample(xs, a_s, h_s, p_sample[0].reshape(nbatch, D_PLE), out_weights)
    y_sample = y_sample.reshape(nbatch, 1, D_MODEL)
    k_sample = from_fp(k_s)
    v_sample = from_fp(v_s)

    return (y_prompt, y_sample, k_prompt, v_prompt, c_prompt, n_prompt, m_prompt,
            k_sample, v_sample, c_s[None], n_s[None], m_s[None])
```

```python
import functools

import numpy as np
import jax
import jax.numpy as jnp
from jax import lax
from jax.experimental import pallas as pl
from jax.experimental.pallas import tpu as pltpu

F32 = jnp.float32
BF16 = jnp.bfloat16

D_MODEL = 1024
HEAD_DIM = 64
N_Q_HEADS = 8
N_KV_HEADS = 2
GQA_GROUP = N_Q_HEADS // N_KV_HEADS
WINDOW = 128
ROPE_THETA = 10000.0
ML_HEADS = 4
ML_DIM = 128
ML_CHUNK = 128
D_FF = 4 * D_MODEL
D_PLE = 256
EPS = 1e-6
PAST_LEN = 16384

ATTN_Q = N_Q_HEADS * HEAD_DIM
ATTN_KV = N_KV_HEADS * HEAD_DIM
ML_W = ML_HEADS * ML_DIM
LANES = 128

OFF_AQ = 0
OFF_AK = OFF_AQ + ATTN_Q
OFF_AV = OFF_AK + ATTN_KV
OFF_MQ = OFF_AV + ATTN_KV
OFF_MK = OFF_MQ + ML_W
OFF_MV = OFF_MK + ML_W
OFF_MO = OFF_MV + ML_W
OFF_MI = OFF_MO + ML_W
OFF_MF = OFF_MI + ML_HEADS
OFF_GA = OFF_MF + ML_HEADS
OFF_GM = OFF_GA + D_MODEL
D_IN = OFF_GM + D_MODEL
N_PROJ = OFF_MI + LANES

VMEM_LIMIT = 60 * 1024 * 1024
PROMPT_TILE = 256
DECODE_ROWS = 32

def _rms(x, g):
    r = lax.rsqrt(jnp.mean(x * x, axis=-1, keepdims=True) + EPS)
    return (x * r) * g


def _mm(a, b):
    return jnp.dot(a, b, preferred_element_type=F32)


def _log_sigmoid(x):
    return jnp.minimum(x, 0.0) - jnp.log1p(jnp.exp(-jnp.abs(x)))


def _weave(*streams):
    live = list(streams)
    while live:
        for g in list(live):
            try:
                next(g)
            except StopIteration:
                live.remove(g)


class _AwaitedRef:
    def __init__(self, ref, copy):
        self._ref, self._copy = ref, copy

    def __getitem__(self, idx):
        if self._copy is not None:
            self._copy.wait()
            self._copy = None
        return self._ref[idx]


WEIGHT_ROW_SPLIT = 8


def _weight_cast_kernel(*refs):
    n_plain = (len(refs) - 3) // 2
    w_in_t_ref, plain_in = refs[0], refs[1:1 + n_plain]
    w_proj_ref, w_gate_ref, plain_out = refs[1 + n_plain], refs[2 + n_plain], refs[3 + n_plain:]
    for g in range(N_PROJ // LANES):
        w_proj_ref[:, g * LANES:(g + 1) * LANES] = w_in_t_ref[g * LANES:(g + 1) * LANES, :].T.astype(BF16)
    for g in range(2 * D_MODEL // LANES):
        w_gate_ref[:, g * LANES:(g + 1) * LANES] = w_in_t_ref[OFF_GA + g * LANES:OFF_GA + (g + 1) * LANES, :].T.astype(BF16)
    for src, dst in zip(plain_in, plain_out):
        dst[...] = src[...].astype(BF16)


def _weight_cast(w_in_t, plain):
    split = WEIGHT_ROW_SPLIT
    assert D_MODEL // split == LANES
    rows_blk = lambda w: pl.BlockSpec((w.shape[0] // split, w.shape[1]), lambda i: (i, 0))
    out_shapes = (jax.ShapeDtypeStruct((D_MODEL, N_PROJ), BF16), jax.ShapeDtypeStruct((D_MODEL, 2 * D_MODEL), BF16),
                  *(jax.ShapeDtypeStruct(w.shape, BF16) for w in plain))
    return pl.pallas_call(
        _weight_cast_kernel,
        grid=(split,),
        in_specs=[pl.BlockSpec((D_IN, LANES), lambda i: (0, i))] + [rows_blk(w) for w in plain],
        out_specs=tuple(rows_blk(s) for s in out_shapes),
        out_shape=out_shapes,
        compiler_params=pltpu.CompilerParams(dimension_semantics=("parallel",), vmem_limit_bytes=VMEM_LIMIT),
        name="weight_cast",
    )(w_in_t, *plain)


def _in_proj_stages(x, g_ref, w_ref, b_ref, cos_ref, sin_ref,
                    q_ref, k_ref, v_ref, kl_ref, vl_ref, mq_ref, mk_ref, mv_ref, og_ref, gif_ref, gt_ref,
                    *, feature_major):
    xn = _rms(x, g_ref[...]).astype(BF16)
    tm = x.shape[0]

    def proj(lo, hi):
        return _mm(xn, w_ref[:, lo:hi]) + b_ref[:, lo:hi]

    cos = cos_ref[...]
    sin = sin_ref[...]
    lane = lax.broadcasted_iota(jnp.int32, (1, LANES), 1)
    first_half = (lane % HEAD_DIM) < (HEAD_DIM // 2)

    def rope(z):
        partner = jnp.where(first_half, pltpu.roll(z, LANES - HEAD_DIM // 2, 1), pltpu.roll(z, HEAD_DIM // 2, 1))
        return z * cos + partner * sin

    zq = proj(OFF_AQ, OFF_AK)
    for c in range(ATTN_Q // LANES):
        sl = slice(c * LANES, (c + 1) * LANES)
        qc = rope(zq[:, sl]) * (HEAD_DIM ** -0.5)
        if feature_major:
            q_ref[sl, :] = qc.T.astype(q_ref.dtype)
        else:
            kv = (2 * c) // GQA_GROUP
            own = (lane // HEAD_DIM) == kv
            swapped = pltpu.roll(qc, HEAD_DIM, 1)
            for j, src in enumerate((qc, swapped) if kv == 0 else (swapped, qc)):
                hd = 2 * c + j
                q_ref[:, hd * LANES:(hd + 1) * LANES] = jnp.where(own, src, 0.0).astype(q_ref.dtype)
    yield
    zkv = proj(OFF_AK, OFF_MQ)
    k = rope(zkv[:, :ATTN_KV])
    v = zkv[:, ATTN_KV:]
    k_ref[...] = k.astype(k_ref.dtype)
    v_ref[...] = (v.T if feature_major else v).astype(v_ref.dtype)
    if feature_major:
        kl_ref[0] = k[tm - WINDOW:, :].T
        vl_ref[0] = v[tm - WINDOW:, :].T
    else:
        kl_ref[...] = k[tm - WINDOW:, :]
        vl_ref[...] = v[tm - WINDOW:, :]
    yield
    for z_off, dst in ((OFF_MQ, mq_ref), (OFF_MV, mv_ref)):
        z = proj(z_off, z_off + ML_W)
        if feature_major:
            for h in range(ML_HEADS):
                hs = slice(h * ML_DIM, (h + 1) * ML_DIM)
                dst[0, hs, :] = z[:, hs].T.astype(dst.dtype)
        else:
            dst[...] = z.astype(dst.dtype)
        yield
    mk_ref[...] = (proj(OFF_MK, OFF_MV) * (ML_DIM ** -0.5)).astype(mk_ref.dtype)
    yield
    og_ref[...] = jax.nn.sigmoid(proj(OFF_MO, OFF_MI))
    yield
    zg = proj(OFF_MI, N_PROJ)
    gif = jnp.where(lane < ML_HEADS, zg, jnp.where(lane < 2 * ML_HEADS, _log_sigmoid(zg), 0.0))
    if feature_major:
        L = ML_CHUNK
        upper = (lax.broadcasted_iota(jnp.int32, (L, L), 0) <= lax.broadcasted_iota(jnp.int32, (L, L), 1)).astype(F32)
        row8 = lax.broadcasted_iota(jnp.int32, (8, 1), 0)
        fill = jnp.zeros((L - 16, L), F32)
        for c in range(tm // L):
            cs = slice(c * L, (c + 1) * L)
            g_rows = gif[cs, :].T[0:8, :]
            cum = jnp.dot(g_rows, upper, precision=lax.Precision.HIGHEST, preferred_element_type=F32)
            rows = jnp.where(row8 < ML_HEADS, g_rows, cum)
            gt_ref[0, :, cs] = rows
            diff = rows - pltpu.roll(rows, ML_HEADS, 0)
            gif_ref[cs, :] = jnp.concatenate([rows, diff, fill], axis=0).T
    else:
        gif_ref[...] = gif
        gt_ref[0] = jnp.zeros(gt_ref.shape[1:], F32)
    yield


def _in_proj_kernel(x_ref, *refs, feature_major):
    _weave(_in_proj_stages(x_ref[...], *refs, feature_major=feature_major))


def _in_proj_sample(x, g, w, b, cos, sin):
    rows = x.shape[0]
    assert rows == WINDOW
    full = lambda n, dt: (jax.ShapeDtypeStruct((rows, n), dt), pl.BlockSpec((rows, n), lambda i: (0, 0)))
    outs = (
        full(N_Q_HEADS * LANES, BF16), full(ATTN_KV, BF16), full(ATTN_KV, BF16), full(ATTN_KV, F32), full(ATTN_KV, F32),
        full(ML_W, F32), full(ML_W, F32), full(ML_W, F32), full(ML_W, F32), full(LANES, F32),
        (jax.ShapeDtypeStruct((1, 8, rows), F32), pl.BlockSpec((1, 8, rows), lambda i: (0, 0, 0))),
    )
    return pl.pallas_call(
        functools.partial(_in_proj_kernel, feature_major=False),
        grid=(1,),
        in_specs=[pl.BlockSpec(a.shape, lambda i: (0, 0)) for a in (x, g, w, b, cos, sin)],
        out_specs=tuple(o[1] for o in outs),
        out_shape=tuple(o[0] for o in outs),
        compiler_params=pltpu.CompilerParams(dimension_semantics=("arbitrary",), vmem_limit_bytes=VMEM_LIMIT),
        name="in_proj",
    )(x, g, w, b, cos, sin)


def _attn_stages(sink_ref, qt_ref, kp_ref, kc_ref, vtp_ref, vtc_ref, first_block, emit):
    L = WINDOW
    nsub = kc_ref.shape[0] // L
    key = lax.broadcasted_iota(jnp.int32, (2 * L, L), 0)
    t = lax.broadcasted_iota(jnp.int32, (2 * L, L), 1)
    is_prev = key < L
    allowed = (is_prev & (key >= t)) | (~is_prev & (key - L <= t))
    bias = jnp.where(allowed, 0.0, -jnp.inf)
    bias_first = jnp.where(is_prev & first_block, -jnp.inf, bias)
    zeros_q = jnp.zeros((HEAD_DIM, L), BF16)
    heads = range(N_Q_HEADS)
    for r in range(nsub):
        ts = slice(r * L, (r + 1) * L)
        k_prev = kp_ref[...] if r == 0 else kc_ref[(r - 1) * L:r * L, :]
        vt_prev = vtp_ref[...] if r == 0 else vtc_ref[:, (r - 1) * L:r * L]
        k2 = jnp.concatenate([k_prev, kc_ref[ts, :]], axis=0)
        vt2 = jnp.concatenate([vt_prev, vtc_ref[:, ts]], axis=1)
        b_r = bias_first if r == 0 else bias
        q_op = [jnp.concatenate([qt_ref[hd * HEAD_DIM:(hd + 1) * HEAD_DIM, ts], zeros_q][::1 if hd < GQA_GROUP else -1],
                                axis=0) for hd in heads]
        s = [_mm(k2, q_op[hd]) + b_r for hd in heads]
        yield
        m = [jnp.maximum(jnp.max(s[hd], axis=0, keepdims=True), sink_ref[hd]) for hd in heads]
        yield
        p = [jnp.exp(s[hd] - m[hd]) for hd in heads]
        yield
        den = [jnp.sum(p[hd], axis=0, keepdims=True) + jnp.exp(sink_ref[hd] - m[hd]) for hd in heads]
        o = [_mm(vt2[(hd // GQA_GROUP) * HEAD_DIM:(hd // GQA_GROUP + 1) * HEAD_DIM, :], p[hd].astype(BF16)) / den[hd]
             for hd in heads]
        yield
        emit(r, jnp.concatenate(o, axis=0).T)
        yield


ML_STATE_ROWS = ML_DIM + 8


def _mlstm_stages(chains, emit):
    L = ML_CHUNK
    s_idx = lax.broadcasted_iota(jnp.int32, (L, L), 0)
    t_idx = lax.broadcasted_iota(jnp.int32, (L, L), 1)
    causal = s_idx <= t_idx
    n = range(len(chains))
    dlog = [jnp.where(causal, ch["b_r"] + ch["u_c"], -jnp.inf) for ch in chains]
    yield
    qk = [_mm(ch["k"], ch["qt"]) for ch in chains]
    inter = [_mm(ch["state"].astype(BF16), ch["qt"]) for ch in chains]
    yield
    a = [ch["b_r"] + ch["m_prev"] for ch in chains]
    m_t = [jnp.maximum(a[i], jnp.max(dlog[i], axis=0, keepdims=True)) for i in n]
    yield
    sw = [qk[i] * jnp.exp(dlog[i] - m_t[i]) for i in n]
    yield
    sv = [_mm(chains[i]["vt"], sw[i].astype(BF16)) for i in n]
    yield
    hs = []
    for i in n:
        aw = jnp.exp(a[i] - m_t[i])
        num = aw * inter[i][:ML_DIM] + sv[i]
        den = aw * inter[i][ML_DIM:ML_DIM + 1] + jnp.sum(sw[i], axis=0, keepdims=True)
        hh = num / jnp.maximum(jnp.abs(den), jnp.exp(-m_t[i]))
        hs.append(hh.T * chains[i]["og"])
    yield
    pad = jnp.zeros((ML_STATE_ROWS - ML_DIM - 1, L), F32)
    new_state, new_m = [], []
    for i in n:
        ch = chains[i]
        b_end = ch["b_r"][:, L - 1:L]
        g = b_end - ch["b_r"] + ch["ig_r"]
        m_new = jnp.maximum(b_end + ch["m_prev"], jnp.max(g, axis=1, keepdims=True))
        decay = jnp.exp(b_end + ch["m_prev"] - m_new)
        w = jnp.exp(g - m_new)
        v_aug = jnp.concatenate([ch["vt"].astype(F32) * w, w, pad], axis=0).astype(BF16)
        new_state.append(decay * ch["state"] + _mm(v_aug, ch["k"]))
        new_m.append(m_new)
    emit(hs, new_state, new_m)
    yield


DENSE_COLS = 256


def _dense_stages(x, a, hm, p, gmix_ref, wg_ref, bg_ref, wau_ref, wmu_ref, wo_ref,
                  gmlp_ref, wff1_ref, wff2_ref, gple_ref, wpg_ref, wpp_ref, gfin_ref, emit):
    nc = DENSE_COLS
    pieces = lambda n: [slice(c, c + nc) for c in range(0, n, nc)]
    xn = _rms(x, gmix_ref[...]).astype(BF16)
    gates = []
    for cs in pieces(2 * D_MODEL):
        gates.append(jax.nn.sigmoid(_mm(xn, wg_ref[:, cs]) + bg_ref[:, cs]))
        yield
    half = D_MODEL // nc
    mix = []
    for j, cs in enumerate(pieces(D_MODEL)):
        mix.append((gates[j] * _mm(a, wau_ref[:, cs]) + gates[half + j] * _mm(hm, wmu_ref[:, cs])).astype(BF16))
        yield
    mix = jnp.concatenate(mix, axis=1)
    h = []
    for cs in pieces(D_MODEL):
        h.append(x[:, cs] + _mm(mix, wo_ref[:, cs]))
        yield
    h = jnp.concatenate(h, axis=1)
    hn = _rms(h, gmlp_ref[...]).astype(BF16)
    u = []
    for cs in pieces(D_FF):
        u.append(jnp.square(jnp.maximum(_mm(hn, wff1_ref[:, cs]), 0.0)).astype(BF16))
        yield
    u = jnp.concatenate(u, axis=1)
    h2 = []
    for cs in pieces(D_MODEL):
        h2.append(h[:, cs] + _mm(u, wff2_ref[:, cs]))
        yield
    h = jnp.concatenate(h2, axis=1)
    hn = _rms(h, gple_ref[...]).astype(BF16)
    pb = p.astype(BF16)
    h3 = []
    for cs in pieces(D_MODEL):
        h3.append(h[:, cs] + _mm(pb, wpp_ref[:, cs]) * jax.nn.sigmoid(_mm(hn, wpg_ref[:, cs])))
        yield
    emit(_rms(jnp.concatenate(h3, axis=1), gfin_ref[...]))
    yield


def _out_kernel(x_ref, a_ref, hm_ref, p_ref, *rest, matrix_idx):
    weights, y_ref = list(rest[:N_FUSED_WEIGHTS]), rest[N_FUSED_WEIGHTS]
    w_scr, dma_sem = rest[N_FUSED_WEIGHTS + 1:-1], rest[-1]
    copies = [pltpu.make_async_copy(weights[j], w_scr[n], dma_sem.at[n]) for n, j in enumerate(matrix_idx)]
    for c in copies:
        c.start()
    for n, j in enumerate(matrix_idx):
        weights[j] = _AwaitedRef(w_scr[n], copies[n])

    def emit(y):
        y_ref[...] = y

    _weave(_dense_stages(x_ref[...], a_ref[...], hm_ref[...], p_ref[...], *weights, emit=emit))


def _out_proj_sample(x, a, hm, p, weights):
    assert len(weights) == N_FUSED_WEIGHTS
    matrix_idx = tuple(j for j, w in enumerate(weights) if w.dtype == BF16)
    matrices = [weights[j] for j in matrix_idx]
    whole = lambda o: pl.BlockSpec(o.shape, lambda i: (0, 0), pipeline_mode=pl.Buffered(1))
    return pl.pallas_call(
        functools.partial(_out_kernel, matrix_idx=matrix_idx),
        grid=(1,),
        in_specs=[whole(o) for o in (x, a, hm, p)]
        + [pl.BlockSpec(memory_space=pl.ANY) if j in matrix_idx else whole(w) for j, w in enumerate(weights)],
        out_specs=pl.BlockSpec(x.shape, lambda i: (0, 0)),
        out_shape=jax.ShapeDtypeStruct(x.shape, F32),
        scratch_shapes=[*(pltpu.VMEM(w.shape, w.dtype) for w in matrices), pltpu.SemaphoreType.DMA((len(matrices),))],
        compiler_params=pltpu.CompilerParams(dimension_semantics=("arbitrary",), vmem_limit_bytes=VMEM_LIMIT),
        name="out_proj",
    )(x, a, hm, p, *weights)


N_FUSED_WEIGHTS = 13
N_PROJ_SLOTS = 9


def _prompt_kernel(sink_ref, x0_ref, cos_ref, sin_ref, x2_ref, p_ref, gmix_ref, wp_ref, bp_ref, *rest,
                   tiles_per_seq, matrix_idx):
    weights = list(rest[:N_FUSED_WEIGHTS])
    y_ref, kl_ref, vl_ref, cst_ref, nst_ref, mst_ref = rest[N_FUSED_WEIGHTS:N_FUSED_WEIGHTS + 6]
    scr = rest[N_FUSED_WEIGHTS + 6:]
    proj_scr = scr[:N_PROJ_SLOTS]
    kprev_scr, vtprev_scr, a_scr, h_scr, c_scr, m_scr = scr[N_PROJ_SLOTS:N_PROJ_SLOTS + 6]
    w_scr, dma_sem = scr[N_PROJ_SLOTS + 6:-1], scr[-1]
    w_copies = [pltpu.make_async_copy(weights[j], w_scr[n], dma_sem.at[n]) for n, j in enumerate(matrix_idx)]
    for n, j in enumerate(matrix_idx):
        weights[j] = w_scr[n]
    i = pl.program_id(0)
    n_tiles = pl.num_programs(0) - 2
    t1 = jnp.clip(i - 1, 0, n_tiles - 1)
    tile_in_seq = t1 % tiles_per_seq
    s0, s1 = i % 2, (i + 1) % 2
    L = ML_CHUNK
    tm = x0_ref.shape[0]

    @pl.when(i == 0)
    def _():
        for c in w_copies:
            c.start()
        for r in (*proj_scr, kprev_scr, vtprev_scr, a_scr, h_scr, c_scr, m_scr):
            r[...] = jnp.zeros_like(r)

    @pl.when(i == 2)
    def _():
        for c in w_copies:
            c.wait()

    @pl.when(tile_in_seq == 0)
    def _():
        c_scr[...] = jnp.zeros_like(c_scr)
        m_scr[...] = jnp.zeros_like(m_scr)

    def projection():
        outs = [r.at[s0] for r in proj_scr]
        yield from _in_proj_stages(x0_ref[...], gmix_ref, wp_ref, bp_ref, cos_ref, sin_ref,
                                   outs[0], outs[1], outs[2], kl_ref, vl_ref, *outs[3:], feature_major=True)

    def branches():
        qt_ref, kc_ref, vtc_ref, mqt_ref, mk_ref, mvt_ref, og_ref, gif_ref, gt_ref = [r.at[s1] for r in proj_scr]
        slot = s1

        def emit_attn(r, blk):
            a_scr[slot, r * WINDOW:(r + 1) * WINDOW, :] = blk.astype(a_scr.dtype)

        yield from _attn_stages(sink_ref, qt_ref, kprev_scr, kc_ref, vtprev_scr, vtc_ref, tile_in_seq == 0, emit_attn)
        kprev_scr[...] = kc_ref[tm - WINDOW:, :]
        vtprev_scr[...] = vtc_ref[:, tm - WINDOW:]
        hsl = lambda h: slice(h * ML_DIM, (h + 1) * ML_DIM)
        carry = dict(state=[c_scr[h] for h in range(ML_HEADS)], m=[m_scr[h:h + 1, :] for h in range(ML_HEADS)])
        for c in range(tm // L):
            cs = slice(c * L, (c + 1) * L)

            def emit_ml(hs, new_state, new_m, cs=cs):
                for h in range(ML_HEADS):
                    h_scr[slot, cs, hsl(h)] = hs[h].astype(h_scr.dtype)
                carry["state"], carry["m"] = new_state, new_m

            chains = [dict(qt=mqt_ref[0, hsl(h), cs], k=mk_ref[cs, hsl(h)], vt=mvt_ref[0, hsl(h), cs],
                           og=og_ref[cs, hsl(h)], ig_r=gt_ref[0, h:h + 1, cs],
                           b_r=gt_ref[0, ML_HEADS + h:ML_HEADS + h + 1, cs],
                           u_c=gif_ref[cs, 2 * ML_HEADS + h:2 * ML_HEADS + h + 1],
                           state=carry["state"][h], m_prev=carry["m"][h]) for h in range(ML_HEADS)]
            yield from _mlstm_stages(chains, emit_ml)
        for h in range(ML_HEADS):
            c_scr[h] = carry["state"][h]
            m_scr[h:h + 1, :] = carry["m"][h]

    def dense():
        def emit_y(y):
            y_ref[...] = y

        return _dense_stages(x2_ref[...], a_scr[s0], h_scr[s0], p_ref[...], *weights, emit=emit_y)

    @pl.when(i < 2)
    def _():
        _weave(branches(), projection())

    @pl.when((i >= 2) & (i < n_tiles))
    def _():
        _weave(branches(), dense(), projection())

    @pl.when(i >= n_tiles)
    def _():
        _weave(branches(), dense())

    @pl.when((tile_in_seq == tiles_per_seq - 1) & (i >= 1) & (i <= n_tiles))
    def _():
        nst_ref[...] = jnp.zeros_like(nst_ref)
        for h in range(ML_HEADS):
            final = c_scr[h]
            cst_ref[0, h] = final[:ML_DIM].T
            nst_ref[0, h:h + 1, :] = final[ML_DIM:ML_DIM + 1]
        mst_ref[0] = m_scr[...]


def _prompt_fused(sinks, x, p, cos, sin, g_mix, w_proj, b_proj, weights, batch, seq, tm):
    assert len(weights) == N_FUSED_WEIGHTS
    rows = batch * seq
    n_tiles = rows // tm
    tps = seq // tm
    t0 = lambda i: jnp.minimum(i, n_tiles - 1)
    t1 = lambda i: jnp.clip(i - 1, 0, n_tiles - 1)
    t2 = lambda i: jnp.maximum(i - 2, 0)
    const = lambda i: (0, 0)
    single = lambda a: pl.BlockSpec(a.shape, const, pipeline_mode=pl.Buffered(1))
    slot = lambda shape, dt: pltpu.VMEM((2,) + shape, dt)
    matrix_idx = tuple(j for j, w in enumerate(weights) if w.dtype == BF16)
    matrices = [weights[j] for j in matrix_idx]
    return pl.pallas_call(
        functools.partial(_prompt_kernel, tiles_per_seq=tps, matrix_idx=matrix_idx),
        grid=(n_tiles + 2,),
        in_specs=[
            pl.BlockSpec(memory_space=pltpu.SMEM),
            pl.BlockSpec((tm, D_MODEL), lambda i: (t0(i), 0)),
            pl.BlockSpec((tm, LANES), lambda i: (t0(i) % tps, 0)),
            pl.BlockSpec((tm, LANES), lambda i: (t0(i) % tps, 0)),
            pl.BlockSpec((tm, D_MODEL), lambda i: (t2(i), 0)),
            pl.BlockSpec((tm, D_PLE), lambda i: (t2(i), 0)),
            single(g_mix), single(w_proj), single(b_proj),
        ] + [pl.BlockSpec(memory_space=pl.ANY) if j in matrix_idx else single(w) for j, w in enumerate(weights)],
        out_specs=(
            pl.BlockSpec((tm, D_MODEL), lambda i: (t2(i), 0)),
            pl.BlockSpec((1, ATTN_KV, WINDOW), lambda i: (t0(i) // tps, 0, 0)),
            pl.BlockSpec((1, ATTN_KV, WINDOW), lambda i: (t0(i) // tps, 0, 0)),
            pl.BlockSpec((1, ML_HEADS, ML_DIM, ML_DIM), lambda i: (t1(i) // tps, 0, 0, 0)),
            pl.BlockSpec((1, 8, LANES), lambda i: (t1(i) // tps, 0, 0)),
            pl.BlockSpec((1, 8, LANES), lambda i: (t1(i) // tps, 0, 0)),
        ),
        out_shape=(
            jax.ShapeDtypeStruct((rows, D_MODEL), F32),
            jax.ShapeDtypeStruct((batch, ATTN_KV, WINDOW), F32),
            jax.ShapeDtypeStruct((batch, ATTN_KV, WINDOW), F32),
            jax.ShapeDtypeStruct((batch, ML_HEADS, ML_DIM, ML_DIM), F32),
            jax.ShapeDtypeStruct((batch, 8, LANES), F32),
            jax.ShapeDtypeStruct((batch, 8, LANES), F32),
        ),
        scratch_shapes=[
            slot((ATTN_Q, tm), BF16), slot((tm, ATTN_KV), BF16), slot((ATTN_KV, tm), BF16),
            slot((1, ML_W, tm), BF16), slot((tm, ML_W), BF16), slot((1, ML_W, tm), BF16),
            slot((tm, ML_W), F32), slot((tm, LANES), F32), slot((1, 8, tm), F32),
            pltpu.VMEM((WINDOW, ATTN_KV), BF16), pltpu.VMEM((ATTN_KV, WINDOW), BF16),
            slot((tm, ATTN_Q), BF16), slot((tm, ML_W), BF16),
            pltpu.VMEM((ML_HEADS, ML_STATE_ROWS, ML_DIM), F32), pltpu.VMEM((8, LANES), F32),
            *(pltpu.VMEM(w.shape, w.dtype) for w in matrices), pltpu.SemaphoreType.DMA((len(matrices),)),
        ],
        compiler_params=pltpu.CompilerParams(dimension_semantics=("arbitrary",), vmem_limit_bytes=VMEM_LIMIT),
        name="prompt_fused",
    )(sinks, x, cos, sin, x, p, g_mix, w_proj, b_proj, *weights)


def _attn_decode_kernel(sink_ref, q_ref, kn_ref, vn_ref, ck_ref, cv_ref, o_ref, ko_ref, vo_ref, o_scr, qm_scr):
    bt = q_ref.shape[0]
    for hd in range(N_Q_HEADS):
        qm_scr[:, hd, :] = q_ref[:, hd * LANES:(hd + 1) * LANES]
    qm = qm_scr[...]
    kn = kn_ref[...]
    vn = vn_ref[...]
    ck = ck_ref[...]
    cv = cv_ref[...]
    head = lax.broadcasted_iota(jnp.int32, (N_Q_HEADS, 1), 0)
    sink = jnp.zeros((N_Q_HEADS, 1), F32)
    for hd in range(N_Q_HEADS):
        sink = jnp.where(head == hd, sink_ref[hd], sink)
    s = jnp.einsum('bhc,bck->bhk', qm, ck.astype(BF16), preferred_element_type=F32)
    s_new = jnp.sum(qm.astype(F32) * kn[:, None, :], axis=2, keepdims=True)
    m = jnp.maximum(jnp.maximum(jnp.max(s, axis=2, keepdims=True), s_new), sink)
    p = jnp.exp(s - m)
    p_new = jnp.exp(s_new - m)
    den = jnp.sum(p, axis=2, keepdims=True) + p_new + jnp.exp(sink - m)
    o = jnp.einsum('bhk,bck->bhc', p.astype(BF16), cv.astype(BF16), preferred_element_type=F32)
    o_scr[...] = (o + p_new * vn[:, None, :]) / den
    low = lax.broadcasted_iota(jnp.int32, (1, LANES), 1) < HEAD_DIM
    for c in range(N_Q_HEADS // 2):
        even, odd = o_scr[:, 2 * c, :], o_scr[:, 2 * c + 1, :]
        if (2 * c) // GQA_GROUP == 0:
            odd = pltpu.roll(odd, HEAD_DIM, 1)
        else:
            even = pltpu.roll(even, HEAD_DIM, 1)
        o_ref[:, c * LANES:(c + 1) * LANES] = jnp.where(low, even, odd).astype(o_ref.dtype)
    pad = jnp.zeros((LANES - bt, ATTN_KV), F32)
    kn_t = jnp.concatenate([kn, pad], axis=0).T
    vn_t = jnp.concatenate([vn, pad], axis=0).T
    newest = lax.broadcasted_iota(jnp.int32, (ATTN_KV, WINDOW), 1) == WINDOW - 1
    for b in range(bt):
        ko_ref[b] = jnp.where(newest, kn_t[:, b:b + 1], pltpu.roll(ck[b], WINDOW - 1, 1))
        vo_ref[b] = jnp.where(newest, vn_t[:, b:b + 1], pltpu.roll(cv[b], WINDOW - 1, 1))


def _attn_decode(sinks, q_heads, k_new, v_new, cache_k, cache_v, bt):
    nbatch = q_heads.shape[0]
    b3 = lambda i: (i, 0, 0)
    b2 = lambda i: (i, 0)
    return pl.pallas_call(
        _attn_decode_kernel,
        grid=(nbatch // bt,),
        in_specs=[
            pl.BlockSpec(memory_space=pltpu.SMEM),
            pl.BlockSpec((bt, N_Q_HEADS * LANES), b2),
            pl.BlockSpec((bt, ATTN_KV), b2),
            pl.BlockSpec((bt, ATTN_KV), b2),
            pl.BlockSpec((bt, WINDOW, ATTN_KV), b3),
            pl.BlockSpec((bt, WINDOW, ATTN_KV), b3),
        ],
        out_specs=(
            pl.BlockSpec((bt, ATTN_Q), b2),
            pl.BlockSpec((bt, WINDOW, ATTN_KV), b3),
            pl.BlockSpec((bt, WINDOW, ATTN_KV), b3),
        ),
        out_shape=(
            jax.ShapeDtypeStruct((nbatch, ATTN_Q), BF16),
            jax.ShapeDtypeStruct((nbatch, WINDOW, ATTN_KV), F32),
            jax.ShapeDtypeStruct((nbatch, WINDOW, ATTN_KV), F32),
        ),
        scratch_shapes=[pltpu.VMEM((bt, N_Q_HEADS, ATTN_KV), F32), pltpu.VMEM((bt, N_Q_HEADS, ATTN_KV), BF16)],
        compiler_params=pltpu.CompilerParams(dimension_semantics=("parallel",)),
        name="attn_decode",
    )(sinks, q_heads, k_new, v_new, cache_k, cache_v)


def _mlstm_decode_kernel(q_ref, k_ref, v_ref, og_ref, gif_ref, c_ref, n_ref, m_ref, h_ref, co_ref, no_ref, mo_ref):
    bt = q_ref.shape[0]
    gif = gif_ref[...]
    ig = gif[:, 0:ML_HEADS]
    lf = gif[:, ML_HEADS:2 * ML_HEADS]
    m_prev = m_ref[...]
    a = lf + m_prev
    m_t = jnp.maximum(a, ig)
    aw = jnp.exp(a - m_t)
    e_i = jnp.exp(ig - m_t)
    floor = jnp.exp(-m_t)
    m_new = jnp.maximum(a, ig)
    decay = jnp.exp(a - m_new)
    w = jnp.exp(ig - m_new)
    mo_ref[...] = m_new
    pad = jnp.zeros((LANES - bt, ML_DIM), F32)
    lane = lax.broadcasted_iota(jnp.int32, (ML_DIM, LANES), 1)
    for h in range(ML_HEADS):
        hs = slice(h * ML_DIM, (h + 1) * ML_DIM)
        q = q_ref[:, hs]
        k = k_ref[:, hs]
        v = v_ref[:, hs]
        n = n_ref[:, h, :]
        qk = jnp.sum(q * k, axis=1, keepdims=True)
        qn = jnp.sum(q * n, axis=1, keepdims=True)
        sw = qk * e_i[:, h:h + 1]
        den = aw[:, h:h + 1] * qn + sw
        inv = 1.0 / jnp.maximum(jnp.abs(den), floor[:, h:h + 1])
        no_ref[:, h, :] = decay[:, h:h + 1] * n + w[:, h:h + 1] * k
        q_b = q.astype(BF16)
        wk_t = jnp.concatenate([w[:, h:h + 1] * k, pad], axis=0).T.astype(BF16)
        v_rows = jnp.concatenate([v, pad], axis=0).astype(BF16)
        nums = []
        for b in range(bt):
            C = c_ref[b, h]
            q_c = _mm(q_b, C.astype(BF16))[b:b + 1, :]
            nums.append(aw[b:b + 1, h:h + 1] * q_c)
            outer = _mm(jnp.where(lane == b, wk_t, jnp.zeros_like(wk_t)), v_rows)
            co_ref[b, h] = decay[b:b + 1, h:h + 1] * C + outer
        num = jnp.concatenate(nums, axis=0) + sw * v
        h_ref[:, hs] = (num * inv * og_ref[:, hs]).astype(h_ref.dtype)


def _mlstm_decode(mq, mk, mv, og, gif, state_c, state_n, state_m, bt):
    nbatch = mq.shape[0]
    b2 = lambda i: (i, 0)
    b3 = lambda i: (i, 0, 0)
    b4 = lambda i: (i, 0, 0, 0)
    return pl.pallas_call(
        _mlstm_decode_kernel,
        grid=(nbatch // bt,),
        in_specs=[
            pl.BlockSpec((bt, ML_W), b2),
            pl.BlockSpec((bt, ML_W), b2),
            pl.BlockSpec((bt, ML_W), b2),
            pl.BlockSpec((bt, ML_W), b2),
            pl.BlockSpec((bt, LANES), b2),
            pl.BlockSpec((bt, ML_HEADS, ML_DIM, ML_DIM), b4),
            pl.BlockSpec((bt, ML_HEADS, ML_DIM), b3),
            pl.BlockSpec((bt, ML_HEADS), b2),
        ],
        out_specs=(
            pl.BlockSpec((bt, ML_W), b2),
            pl.BlockSpec((bt, ML_HEADS, ML_DIM, ML_DIM), b4),
            pl.BlockSpec((bt, ML_HEADS, ML_DIM), b3),
            pl.BlockSpec((bt, ML_HEADS), b2),
        ),
        out_shape=(
            jax.ShapeDtypeStruct((nbatch, ML_W), BF16),
            jax.ShapeDtypeStruct((nbatch, ML_HEADS, ML_DIM, ML_DIM), F32),
            jax.ShapeDtypeStruct((nbatch, ML_HEADS, ML_DIM), F32),
            jax.ShapeDtypeStruct((nbatch, ML_HEADS), F32),
        ),
        compiler_params=pltpu.CompilerParams(dimension_semantics=("parallel",)),
        name="mlstm_decode",
    )(mq, mk, mv, og, gif, state_c, state_n, state_m)


def _rope_tables(pos):
    half = HEAD_DIM // 2
    inv = (1.0 / (ROPE_THETA ** (np.arange(half, dtype=np.float32) / half))).astype(np.float32)
    ang = pos.astype(np.float32)[:, None] * inv[None, :]
    cos = np.cos(ang)
    sin = np.sin(ang)
    cos_t = np.concatenate([cos, cos, cos, cos], axis=1)
    sin_t = np.concatenate([-sin, sin, -sin, sin], axis=1)
    return jnp.asarray(cos_t, F32), jnp.asarray(sin_t, F32)


def kernel(x_prompt, x_sample, cache_k, cache_v, state_C, state_n, state_m, p_prompt, p_sample, norm_mix, w_in, b_in, attn_sinks, w_attn_up, w_ml_up, w_o, norm_mlp, w_ff1, w_ff2, norm_ple, w_ple_gate, w_ple_proj, norm_final):
    assert w_in.shape[0] == 1, "single layer"
    batch, seq, _ = x_prompt.shape
    nbatch = x_sample.shape[0]
    assert x_sample.shape[1] == 1

    w_proj, w_gate, w_au, w_mu, w_out, w_f1, w_f2, w_pg, w_pp = _weight_cast(
        w_in[0].T, (w_attn_up[0], w_ml_up[0], w_o[0], w_ff1[0], w_ff2[0], w_ple_gate[0], w_ple_proj[0]))
    b_proj = b_in[:, :N_PROJ]
    b_gate = b_in[:, OFF_GA:]
    g_mix = norm_mix[0][None, :]
    out_weights = (
        g_mix, w_gate, b_gate, w_au, w_mu, w_out,
        norm_mlp[0][None, :], w_f1, w_f2,
        norm_ple[0][None, :], w_pg, w_pp, norm_final[None, :],
    )
    sinks = attn_sinks[0]
    to_fp = lambda c: c[0].transpose(0, 2, 3, 1).reshape(c.shape[1], ATTN_KV, WINDOW)
    from_fp = lambda c: c.reshape(c.shape[0], N_KV_HEADS, HEAD_DIM, WINDOW).transpose(0, 3, 1, 2)[None]

    cos_p, sin_p = _rope_tables(np.arange(seq, dtype=np.int32))
    xp = x_prompt.reshape(batch * seq, D_MODEL)
    y_prompt, k_last, v_last, cst, nst, mst = _prompt_fused(
        sinks, xp, p_prompt[0].reshape(batch * seq, D_PLE), cos_p, sin_p, g_mix, w_proj, b_proj, out_weights,
        batch, seq, tm=PROMPT_TILE)
    y_prompt = y_prompt.reshape(batch, seq, D_MODEL)
    k_prompt = from_fp(k_last)
    v_prompt = from_fp(v_last)
    c_prompt = cst[None]
    n_prompt = nst[:, :ML_HEADS, :][None]
    m_prompt = mst[:, :ML_HEADS, 0][None]

    cos_s, sin_s = _rope_tables(np.full((nbatch,), PAST_LEN, dtype=np.int32))
    xs = x_sample.reshape(nbatch, D_MODEL)
    qs, _, _, ks, vs, mqs, mks, mvs, ogs, gifs, _ = _in_proj_sample(xs, g_mix, w_proj, b_proj, cos_s, sin_s)
    a_s, k_s, v_s = _attn_decode(sinks, qs, ks, vs, to_fp(cache_k), to_fp(cache_v), bt=DECODE_ROWS)
    h_s, c_s, n_s, m_s = _mlstm_decode(mqs, mks, mvs, ogs, gifs, state_C[0], state_n[0], state_m[0], bt=DECODE_ROWS)
    y_sample = _out_proj_sample(xs, a_s, h_s, p_sample[0].reshape(nbatch, D_PLE), out_weights)
    y_sample = y_sample.reshape(nbatch, 1, D_MODEL)
    k_sample = from_fp(k_s)
    v_sample = from_fp(v_s)

    return (y_prompt, y_sample, k_prompt, v_prompt, c_prompt, n_prompt, m_prompt,
            k_sample, v_sample, c_s[None], n_s[None], m_s[None])
```

```python
import functools

import numpy as np
import jax
import jax.numpy as jnp
from jax import lax
from jax.experimental import pallas as pl
from jax.experimental.pallas import tpu as pltpu

F32 = jnp.float32
BF16 = jnp.bfloat16

D_MODEL = 1024
HEAD_DIM = 64
N_Q_HEADS = 8
N_KV_HEADS = 2
GQA_GROUP = N_Q_HEADS // N_KV_HEADS
WINDOW = 128
ROPE_THETA = 10000.0
ML_HEADS = 4
ML_DIM = 128
ML_CHUNK = 128
D_FF = 4 * D_MODEL
D_PLE = 256
EPS = 1e-6
PAST_LEN = 16384

ATTN_Q = N_Q_HEADS * HEAD_DIM
ATTN_KV = N_KV_HEADS * HEAD_DIM
ML_W = ML_HEADS * ML_DIM
LANES = 128

OFF_AQ = 0
OFF_AK = OFF_AQ + ATTN_Q
OFF_AV = OFF_AK + ATTN_KV
OFF_MQ = OFF_AV + ATTN_KV
OFF_MK = OFF_MQ + ML_W
OFF_MV = OFF_MK + ML_W
OFF_MO = OFF_MV + ML_W
OFF_MI = OFF_MO + ML_W
OFF_MF = OFF_MI + ML_HEADS
OFF_GA = OFF_MF + ML_HEADS
OFF_GM = OFF_GA + D_MODEL
D_IN = OFF_GM + D_MODEL
N_PROJ = OFF_MI + LANES

VMEM_LIMIT = 60 * 1024 * 1024
PROMPT_TILE = 256
DECODE_ROWS = 32

def _rms(x, g):
    r = lax.rsqrt(jnp.mean(x * x, axis=-1, keepdims=True) + EPS)
    return (x * r) * g


def _mm(a, b):
    return jnp.dot(a, b, preferred_element_type=F32)


def _log_sigmoid(x):
    return jnp.minimum(x, 0.0) - jnp.log1p(jnp.exp(-jnp.abs(x)))


def _weave(*streams):
    live = list(streams)
    while live:
        for g in list(live):
            try:
                next(g)
            except StopIteration:
                live.remove(g)


class _AwaitedRef:
    def __init__(self, ref, copy):
        self._ref, self._copy = ref, copy

    def __getitem__(self, idx):
        if self._copy is not None:
            self._copy.wait()
            self._copy = None
        return self._ref[idx]


WEIGHT_ROW_SPLIT = 8


def _weight_cast_kernel(*refs):
    n_plain = (len(refs) - 3) // 2
    w_in_t_ref, plain_in = refs[0], refs[1:1 + n_plain]
    w_proj_ref, w_gate_ref, plain_out = refs[1 + n_plain], refs[2 + n_plain], refs[3 + n_plain:]
    for g in range(N_PROJ // LANES):
        w_proj_ref[:, g * LANES:(g + 1) * LANES] = w_in_t_ref[g * LANES:(g + 1) * LANES, :].T.astype(BF16)
    for g in range(2 * D_MODEL // LANES):
        w_gate_ref[:, g * LANES:(g + 1) * LANES] = w_in_t_ref[OFF_GA + g * LANES:OFF_GA + (g + 1) * LANES, :].T.astype(BF16)
    for src, dst in zip(plain_in, plain_out):
        dst[...] = src[...].astype(BF16)


def _weight_cast(w_in_t, plain):
    split = WEIGHT_ROW_SPLIT
    assert D_MODEL // split == LANES
    rows_blk = lambda w: pl.BlockSpec((w.shape[0] // split, w.shape[1]), lambda i: (i, 0))
    out_shapes = (jax.ShapeDtypeStruct((D_MODEL, N_PROJ), BF16), jax.ShapeDtypeStruct((D_MODEL, 2 * D_MODEL), BF16),
                  *(jax.ShapeDtypeStruct(w.shape, BF16) for w in plain))
    return pl.pallas_call(
        _weight_cast_kernel,
        grid=(split,),
        in_specs=[pl.BlockSpec((D_IN, LANES), lambda i: (0, i))] + [rows_blk(w) for w in plain],
        out_specs=tuple(rows_blk(s) for s in out_shapes),
        out_shape=out_shapes,
        compiler_params=pltpu.CompilerParams(dimension_semantics=("parallel",), vmem_limit_bytes=VMEM_LIMIT),
        name="weight_cast",
    )(w_in_t, *plain)


def _in_proj_stages(x, g_ref, w_ref, b_ref, cos_ref, sin_ref,
                    q_ref, k_ref, v_ref, kl_ref, vl_ref, mq_ref, mk_ref, mv_ref, og_ref, gif_ref, gt_ref,
                    *, feature_major):
    xn = _rms(x, g_ref[...]).astype(BF16)
    tm = x.shape[0]

    def proj(lo, hi):
        return _mm(xn, w_ref[:, lo:hi]) + b_ref[:, lo:hi]

    cos = cos_ref[...]
    sin = sin_ref[...]
    lane = lax.broadcasted_iota(jnp.int32, (1, LANES), 1)
    first_half = (lane % HEAD_DIM) < (HEAD_DIM // 2)

    def rope(z):
        partner = jnp.where(first_half, pltpu.roll(z, LANES - HEAD_DIM // 2, 1), pltpu.roll(z, HEAD_DIM // 2, 1))
        return z * cos + partner * sin

    zq = proj(OFF_AQ, OFF_AK)
    for c in range(ATTN_Q // LANES):
        sl = slice(c * LANES, (c + 1) * LANES)
        qc = rope(zq[:, sl]) * (HEAD_DIM ** -0.5)
        if feature_major:
            q_ref[sl, :] = qc.T.astype(q_ref.dtype)
        else:
            kv = (2 * c) // GQA_GROUP
            own = (lane // HEAD_DIM) == kv
            swapped = pltpu.roll(qc, HEAD_DIM, 1)
            for j, src in enumerate((qc, swapped) if kv == 0 else (swapped, qc)):
                hd = 2 * c + j
                q_ref[:, hd * LANES:(hd + 1) * LANES] = jnp.where(own, src, 0.0).astype(q_ref.dtype)
    yield
    zkv = proj(OFF_AK, OFF_MQ)
    k = rope(zkv[:, :ATTN_KV])
    v = zkv[:, ATTN_KV:]
    k_ref[...] = k.astype(k_ref.dtype)
    v_ref[...] = (v.T if feature_major else v).astype(v_ref.dtype)
    if feature_major:
        kl_ref[0] = k[tm - WINDOW:, :].T
        vl_ref[0] = v[tm - WINDOW:, :].T
    else:
        kl_ref[...] = k[tm - WINDOW:, :]
        vl_ref[...] = v[tm - WINDOW:, :]
    yield
    for z_off, dst in ((OFF_MQ, mq_ref), (OFF_MV, mv_ref)):
        z = proj(z_off, z_off + ML_W)
        if feature_major:
            for h in range(ML_HEADS):
                hs = slice(h * ML_DIM, (h + 1) * ML_DIM)
                dst[0, hs, :] = z[:, hs].T.astype(dst.dtype)
        else:
            dst[...] = z.astype(dst.dtype)
        yield
    mk_ref[...] = (proj(OFF_MK, OFF_MV) * (ML_DIM ** -0.5)).astype(mk_ref.dtype)
    yield
    og_ref[...] = jax.nn.sigmoid(proj(OFF_MO, OFF_MI))
    yield
    zg = proj(OFF_MI, N_PROJ)
    gif = jnp.where(lane < ML_HEADS, zg, jnp.where(lane < 2 * ML_HEADS, _log_sigmoid(zg), 0.0))
    if feature_major:
        L = ML_CHUNK
        upper = (lax.broadcasted_iota(jnp.int32, (L, L), 0) <= lax.broadcasted_iota(jnp.int32, (L, L), 1)).astype(F32)
        row8 = lax.broadcasted_iota(jnp.int32, (8, 1), 0)
        fill = jnp.zeros((L - 16, L), F32)
        for c in range(tm // L):
            cs = slice(c * L, (c + 1) * L)
            g_rows = gif[cs, :].T[0:8, :]
            cum = jnp.dot(g_rows, upper, precision=lax.Precision.HIGHEST, preferred_element_type=F32)
            rows = jnp.where(row8 < ML_HEADS, g_rows, cum)
            gt_ref[0, :, cs] = rows
            diff = rows - pltpu.roll(rows, ML_HEADS, 0)
            gif_ref[cs, :] = jnp.concatenate([rows, diff, fill], axis=0).T
    else:
        gif_ref[...] = gif
        gt_ref[0] = jnp.zeros(gt_ref.shape[1:], F32)
    yield


def _in_proj_kernel(x_ref, *refs, feature_major):
    _weave(_in_proj_stages(x_ref[...], *refs, feature_major=feature_major))


def _in_proj_sample(x, g, w, b, cos, sin):
    rows = x.shape[0]
    assert rows == WINDOW
    full = lambda n, dt: (jax.ShapeDtypeStruct((rows, n), dt), pl.BlockSpec((rows, n), lambda i: (0, 0)))
    outs = (
        full(N_Q_HEADS * LANES, BF16), full(ATTN_KV, BF16), full(ATTN_KV, BF16), full(ATTN_KV, F32), full(ATTN_KV, F32),
        full(ML_W, F32), full(ML_W, F32), full(ML_W, F32), full(ML_W, F32), full(LANES, F32),
        (jax.ShapeDtypeStruct((1, 8, rows), F32), pl.BlockSpec((1, 8, rows), lambda i: (0, 0, 0))),
    )
    return pl.pallas_call(
        functools.partial(_in_proj_kernel, feature_major=False),
        grid=(1,),
        in_specs=[pl.BlockSpec(a.shape, lambda i: (0, 0)) for a in (x, g, w, b, cos, sin)],
        out_specs=tuple(o[1] for o in outs),
        out_shape=tuple(o[0] for o in outs),
        compiler_params=pltpu.CompilerParams(dimension_semantics=("arbitrary",), vmem_limit_bytes=VMEM_LIMIT),
        name="in_proj",
    )(x, g, w, b, cos, sin)


def _attn_stages(sink_ref, qt_ref, kp_ref, kc_ref, vtp_ref, vtc_ref, first_block, emit):
    L = WINDOW
    nsub = kc_ref.shape[0] // L
    key = lax.broadcasted_iota(jnp.int32, (2 * L, L), 0)
    t = lax.broadcasted_iota(jnp.int32, (2 * L, L), 1)
    is_prev = key < L
    allowed = (is_prev & (key >= t)) | (~is_prev & (key - L <= t))
    bias = jnp.where(allowed, 0.0, -jnp.inf)
    bias_first = jnp.where(is_prev & first_block, -jnp.inf, bias)
    zeros_q = jnp.zeros((HEAD_DIM, L), BF16)
    heads = range(N_Q_HEADS)
    for r in range(nsub):
        ts = slice(r * L, (r + 1) * L)
        k_prev = kp_ref[...] if r == 0 else kc_ref[(r - 1) * L:r * L, :]
        vt_prev = vtp_ref[...] if r == 0 else vtc_ref[:, (r - 1) * L:r * L]
        k2 = jnp.concatenate([k_prev, kc_ref[ts, :]], axis=0)
        vt2 = jnp.concatenate([vt_prev, vtc_ref[:, ts]], axis=1)
        b_r = bias_first if r == 0 else bias
        q_op = [jnp.concatenate([qt_ref[hd * HEAD_DIM:(hd + 1) * HEAD_DIM, ts], zeros_q][::1 if hd < GQA_GROUP else -1],
                                axis=0) for hd in heads]
        s = [_mm(k2, q_op[hd]) + b_r for hd in heads]
        yield
        m = [jnp.maximum(jnp.max(s[hd], axis=0, keepdims=True), sink_ref[hd]) for hd in heads]
        yield
        p = [jnp.exp(s[hd] - m[hd]) for hd in heads]
        yield
        den = [jnp.sum(p[hd], axis=0, keepdims=True) + jnp.exp(sink_ref[hd] - m[hd]) for hd in heads]
        o = [_mm(vt2[(hd // GQA_GROUP) * HEAD_DIM:(hd // GQA_GROUP + 1) * HEAD_DIM, :], p[hd].astype(BF16)) / den[hd]
             for hd in heads]
        yield
        emit(r, jnp.concatenate(o, axis=0).T)
        yield


ML_STATE_ROWS = ML_DIM + 8


def _mlstm_stages(chains, emit):
    L = ML_CHUNK
    s_idx = lax.broadcasted_iota(jnp.int32, (L, L), 0)
    t_idx = lax.broadcasted_iota(jnp.int32, (L, L), 1)
    causal = s_idx <= t_idx
    n = range(len(chains))
    dlog = [jnp.where(causal, ch["b_r"] + ch["u_c"], -jnp.inf) for ch in chains]
    yield
    qk = [_mm(ch["k"], ch["qt"]) for ch in chains]
    inter = [_mm(ch["state"].astype(BF16), ch["qt"]) for ch in chains]
    yield
    a = [ch["b_r"] + ch["m_prev"] for ch in chains]
    m_t = [jnp.maximum(a[i], jnp.max(dlog[i], axis=0, keepdims=True)) for i in n]
    yield
    sw = [qk[i] * jnp.exp(dlog[i] - m_t[i]) for i in n]
    yield
    sv = [_mm(chains[i]["vt"], sw[i].astype(BF16)) for i in n]
    yield
    hs = []
    for i in n:
        aw = jnp.exp(a[i] - m_t[i])
        num = aw * inter[i][:ML_DIM] + sv[i]
        den = aw * inter[i][ML_DIM:ML_DIM + 1] + jnp.sum(sw[i], axis=0, keepdims=True)
        hh = num / jnp.maximum(jnp.abs(den), jnp.exp(-m_t[i]))
        hs.append(hh.T * chains[i]["og"])
    yield
    pad = jnp.zeros((ML_STATE_ROWS - ML_DIM - 1, L), F32)
    new_state, new_m = [], []
    for i in n:
        ch = chains[i]
        b_end = ch["b_r"][:, L - 1:L]
        g = b_end - ch["b_r"] + ch["ig_r"]
        m_new = jnp.maximum(b_end + ch["m_prev"], jnp.max(g, axis=1, keepdims=True))
        decay = jnp.exp(b_end + ch["m_prev"] - m_new)
        w = jnp.exp(g - m_new)
        v_aug = jnp.concatenate([ch["vt"].astype(F32) * w, w, pad], axis=0).astype(BF16)
        new_state.append(decay * ch["state"] + _mm(v_aug, ch["k"]))
        new_m.append(m_new)
    emit(hs, new_state, new_m)
    yield


DENSE_COLS = 512


def _dense_stages(x, a, hm, p, gmix_ref, wg_ref, bg_ref, wau_ref, wmu_ref, wo_ref,
                  gmlp_ref, wff1_ref, wff2_ref, gple_ref, wpg_ref, wpp_ref, gfin_ref, emit):
    nc = DENSE_COLS
    pieces = lambda n: [slice(c, c + nc) for c in range(0, n, nc)]
    xn = _rms(x, gmix_ref[...]).astype(BF16)
    gates = []
    for cs in pieces(2 * D_MODEL):
        gates.append(jax.nn.sigmoid(_mm(xn, wg_ref[:, cs]) + bg_ref[:, cs]))
        yield
    half = D_MODEL // nc
    mix = []
    for j, cs in enumerate(pieces(D_MODEL)):
        mix.append((gates[j] * _mm(a, wau_ref[:, cs]) + gates[half + j] * _mm(hm, wmu_ref[:, cs])).astype(BF16))
        yield
    mix = jnp.concatenate(mix, axis=1)
    h = []
    for cs in pieces(D_MODEL):
        h.append(x[:, cs] + _mm(mix, wo_ref[:, cs]))
        yield
    h = jnp.concatenate(h, axis=1)
    hn = _rms(h, gmlp_ref[...]).astype(BF16)
    u = []
    for cs in pieces(D_FF):
        u.append(jnp.square(jnp.maximum(_mm(hn, wff1_ref[:, cs]), 0.0)).astype(BF16))
        yield
    u = jnp.concatenate(u, axis=1)
    h2 = []
    for cs in pieces(D_MODEL):
        h2.append(h[:, cs] + _mm(u, wff2_ref[:, cs]))
        yield
    h = jnp.concatenate(h2, axis=1)
    hn = _rms(h, gple_ref[...]).astype(BF16)
    pb = p.astype(BF16)
    h3 = []
    for cs in pieces(D_MODEL):
        h3.append(h[:, cs] + _mm(pb, wpp_ref[:, cs]) * jax.nn.sigmoid(_mm(hn, wpg_ref[:, cs])))
        yield
    emit(_rms(jnp.concatenate(h3, axis=1), gfin_ref[...]))
    yield


def _out_kernel(x_ref, a_ref, hm_ref, p_ref, *rest, matrix_idx):
    weights, y_ref = list(rest[:N_FUSED_WEIGHTS]), rest[N_FUSED_WEIGHTS]
    w_scr, dma_sem = rest[N_FUSED_WEIGHTS + 1:-1], rest[-1]
    copies = [pltpu.make_async_copy(weights[j], w_scr[n], dma_sem.at[n]) for n, j in enumerate(matrix_idx)]
    for c in copies:
        c.start()
    for n, j in enumerate(matrix_idx):
        weights[j] = _AwaitedRef(w_scr[n], copies[n])

    def emit(y):
        y_ref[...] = y

    _weave(_dense_stages(x_ref[...], a_ref[...], hm_ref[...], p_ref[...], *weights, emit=emit))


def _out_proj_sample(x, a, hm, p, weights):
    assert len(weights) == N_FUSED_WEIGHTS
    matrix_idx = tuple(j for j, w in enumerate(weights) if w.dtype == BF16)
    matrices = [weights[j] for j in matrix_idx]
    whole = lambda o: pl.BlockSpec(o.shape, lambda i: (0, 0), pipeline_mode=pl.Buffered(1))
    return pl.pallas_call(
        functools.partial(_out_kernel, matrix_idx=matrix_idx),
        grid=(1,),
        in_specs=[whole(o) for o in (x, a, hm, p)]
        + [pl.BlockSpec(memory_space=pl.ANY) if j in matrix_idx else whole(w) for j, w in enumerate(weights)],
        out_specs=pl.BlockSpec(x.shape, lambda i: (0, 0)),
        out_shape=jax.ShapeDtypeStruct(x.shape, F32),
        scratch_shapes=[*(pltpu.VMEM(w.shape, w.dtype) for w in matrices), pltpu.SemaphoreType.DMA((len(matrices),))],
        compiler_params=pltpu.CompilerParams(dimension_semantics=("arbitrary",), vmem_limit_bytes=VMEM_LIMIT),
        name="out_proj",
    )(x, a, hm, p, *weights)


N_FUSED_WEIGHTS = 13
N_PROJ_SLOTS = 9


def _prompt_kernel(sink_ref, x0_ref, cos_ref, sin_ref, x2_ref, p_ref, gmix_ref, wp_ref, bp_ref, *rest,
                   tiles_per_seq, matrix_idx):
    weights = list(rest[:N_FUSED_WEIGHTS])
    y_ref, kl_ref, vl_ref, cst_ref, nst_ref, mst_ref = rest[N_FUSED_WEIGHTS:N_FUSED_WEIGHTS + 6]
    scr = rest[N_FUSED_WEIGHTS + 6:]
    proj_scr = scr[:N_PROJ_SLOTS]
    kprev_scr, vtprev_scr, a_scr, h_scr, c_scr, m_scr = scr[N_PROJ_SLOTS:N_PROJ_SLOTS + 6]
    w_scr, dma_sem = scr[N_PROJ_SLOTS + 6:-1], scr[-1]
    w_copies = [pltpu.make_async_copy(weights[j], w_scr[n], dma_sem.at[n]) for n, j in enumerate(matrix_idx)]
    for n, j in enumerate(matrix_idx):
        weights[j] = w_scr[n]
    i = pl.program_id(0)
    n_tiles = pl.num_programs(0) - 2
    t1 = jnp.clip(i - 1, 0, n_tiles - 1)
    tile_in_seq = t1 % tiles_per_seq
    s0, s1 = i % 2, (i + 1) % 2
    L = ML_CHUNK
    tm = x0_ref.shape[0]

    @pl.when(i == 0)
    def _():
        for c in w_copies:
            c.start()
        for r in (*proj_scr, kprev_scr, vtprev_scr, a_scr, h_scr, c_scr, m_scr):
            r[...] = jnp.zeros_like(r)

    @pl.when(i == 2)
    def _():
        for c in w_copies:
            c.wait()

    @pl.when(tile_in_seq == 0)
    def _():
        c_scr[...] = jnp.zeros_like(c_scr)
        m_scr[...] = jnp.zeros_like(m_scr)

    def projection():
        outs = [r.at[s0] for r in proj_scr]
        yield from _in_proj_stages(x0_ref[...], gmix_ref, wp_ref, bp_ref, cos_ref, sin_ref,
                                   outs[0], outs[1], outs[2], kl_ref, vl_ref, *outs[3:], feature_major=True)

    def branches():
        qt_ref, kc_ref, vtc_ref, mqt_ref, mk_ref, mvt_ref, og_ref, gif_ref, gt_ref = [r.at[s1] for r in proj_scr]
        slot = s1

        def emit_attn(r, blk):
            a_scr[slot, r * WINDOW:(r + 1) * WINDOW, :] = blk.astype(a_scr.dtype)

        hsl = lambda h: slice(h * ML_DIM, (h + 1) * ML_DIM)
        carry = dict(state=[c_scr[h] for h in range(ML_HEADS)], m=[m_scr[h:h + 1, :] for h in range(ML_HEADS)])
        for c in range(tm // L):
            cs = slice(c * L, (c + 1) * L)

            def emit_ml(hs, new_state, new_m, cs=cs):
                for h in range(ML_HEADS):
                    h_scr[slot, cs, hsl(h)] = hs[h].astype(h_scr.dtype)
                carry["state"], carry["m"] = new_state, new_m

            chains = [dict(qt=mqt_ref[0, hsl(h), cs], k=mk_ref[cs, hsl(h)], vt=mvt_ref[0, hsl(h), cs],
                           og=og_ref[cs, hsl(h)], ig_r=gt_ref[0, h:h + 1, cs],
                           b_r=gt_ref[0, ML_HEADS + h:ML_HEADS + h + 1, cs],
                           u_c=gif_ref[cs, 2 * ML_HEADS + h:2 * ML_HEADS + h + 1],
                           state=carry["state"][h], m_prev=carry["m"][h]) for h in range(ML_HEADS)]
            yield from _mlstm_stages(chains, emit_ml)
        for h in range(ML_HEADS):
            c_scr[h] = carry["state"][h]
            m_scr[h:h + 1, :] = carry["m"][h]
        yield from _attn_stages(sink_ref, qt_ref, kprev_scr, kc_ref, vtprev_scr, vtc_ref, tile_in_seq == 0, emit_attn)
        kprev_scr[...] = kc_ref[tm - WINDOW:, :]
        vtprev_scr[...] = vtc_ref[:, tm - WINDOW:]

    def dense():
        def emit_y(y):
            y_ref[...] = y

        return _dense_stages(x2_ref[...], a_scr[s0], h_scr[s0], p_ref[...], *weights, emit=emit_y)

    @pl.when(i < 2)
    def _():
        _weave(branches(), projection())

    @pl.when((i >= 2) & (i < n_tiles))
    def _():
        _weave(branches(), dense(), projection())

    @pl.when(i >= n_tiles)
    def _():
        _weave(branches(), dense())

    @pl.when((tile_in_seq == tiles_per_seq - 1) & (i >= 1) & (i <= n_tiles))
    def _():
        nst_ref[...] = jnp.zeros_like(nst_ref)
        for h in range(ML_HEADS):
            final = c_scr[h]
            cst_ref[0, h] = final[:ML_DIM].T
            nst_ref[0, h:h + 1, :] = final[ML_DIM:ML_DIM + 1]
        mst_ref[0] = m_scr[...]


def _prompt_fused(sinks, x, p, cos, sin, g_mix, w_proj, b_proj, weights, batch, seq, tm):
    assert len(weights) == N_FUSED_WEIGHTS
    rows = batch * seq
    n_tiles = rows // tm
    tps = seq // tm
    t0 = lambda i: jnp.minimum(i, n_tiles - 1)
    t1 = lambda i: jnp.clip(i - 1, 0, n_tiles - 1)
    t2 = lambda i: jnp.maximum(i - 2, 0)
    const = lambda i: (0, 0)
    single = lambda a: pl.BlockSpec(a.shape, const, pipeline_mode=pl.Buffered(1))
    slot = lambda shape, dt: pltpu.VMEM((2,) + shape, dt)
    matrix_idx = tuple(j for j, w in enumerate(weights) if w.dtype == BF16)
    matrices = [weights[j] for j in matrix_idx]
    return pl.pallas_call(
        functools.partial(_prompt_kernel, tiles_per_seq=tps, matrix_idx=matrix_idx),
        grid=(n_tiles + 2,),
        in_specs=[
            pl.BlockSpec(memory_space=pltpu.SMEM),
            pl.BlockSpec((tm, D_MODEL), lambda i: (t0(i), 0)),
            pl.BlockSpec((tm, LANES), lambda i: (t0(i) % tps, 0)),
            pl.BlockSpec((tm, LANES), lambda i: (t0(i) % tps, 0)),
            pl.BlockSpec((tm, D_MODEL), lambda i: (t2(i), 0)),
            pl.BlockSpec((tm, D_PLE), lambda i: (t2(i), 0)),
            single(g_mix), single(w_proj), single(b_proj),
        ] + [pl.BlockSpec(memory_space=pl.ANY) if j in matrix_idx else single(w) for j, w in enumerate(weights)],
        out_specs=(
            pl.BlockSpec((tm, D_MODEL), lambda i: (t2(i), 0)),
            pl.BlockSpec((1, ATTN_KV, WINDOW), lambda i: (t0(i) // tps, 0, 0)),
            pl.BlockSpec((1, ATTN_KV, WINDOW), lambda i: (t0(i) // tps, 0, 0)),
            pl.BlockSpec((1, ML_HEADS, ML_DIM, ML_DIM), lambda i: (t1(i) // tps, 0, 0, 0)),
            pl.BlockSpec((1, 8, LANES), lambda i: (t1(i) // tps, 0, 0)),
            pl.BlockSpec((1, 8, LANES), lambda i: (t1(i) // tps, 0, 0)),
        ),
        out_shape=(
            jax.ShapeDtypeStruct((rows, D_MODEL), F32),
            jax.ShapeDtypeStruct((batch, ATTN_KV, WINDOW), F32),
            jax.ShapeDtypeStruct((batch, ATTN_KV, WINDOW), F32),
            jax.ShapeDtypeStruct((batch, ML_HEADS, ML_DIM, ML_DIM), F32),
            jax.ShapeDtypeStruct((batch, 8, LANES), F32),
            jax.ShapeDtypeStruct((batch, 8, LANES), F32),
        ),
        scratch_shapes=[
            slot((ATTN_Q, tm), BF16), slot((tm, ATTN_KV), BF16), slot((ATTN_KV, tm), BF16),
            slot((1, ML_W, tm), BF16), slot((tm, ML_W), BF16), slot((1, ML_W, tm), BF16),
            slot((tm, ML_W), F32), slot((tm, LANES), F32), slot((1, 8, tm), F32),
            pltpu.VMEM((WINDOW, ATTN_KV), BF16), pltpu.VMEM((ATTN_KV, WINDOW), BF16),
            slot((tm, ATTN_Q), BF16), slot((tm, ML_W), BF16),
            pltpu.VMEM((ML_HEADS, ML_STATE_ROWS, ML_DIM), F32), pltpu.VMEM((8, LANES), F32),
            *(pltpu.VMEM(w.shape, w.dtype) for w in matrices), pltpu.SemaphoreType.DMA((len(matrices),)),
        ],
        compiler_params=pltpu.CompilerParams(dimension_semantics=("arbitrary",), vmem_limit_bytes=VMEM_LIMIT),
        name="prompt_fused",
    )(sinks, x, cos, sin, x, p, g_mix, w_proj, b_proj, *weights)


def _attn_decode_kernel(sink_ref, q_ref, kn_ref, vn_ref, ck_ref, cv_ref, o_ref, ko_ref, vo_ref, o_scr, qm_scr):
    bt = q_ref.shape[0]
    for hd in range(N_Q_HEADS):
        qm_scr[:, hd, :] = q_ref[:, hd * LANES:(hd + 1) * LANES]
    qm = qm_scr[...]
    kn = kn_ref[...]
    vn = vn_ref[...]
    ck = ck_ref[...]
    cv = cv_ref[...]
    head = lax.broadcasted_iota(jnp.int32, (N_Q_HEADS, 1), 0)
    sink = jnp.zeros((N_Q_HEADS, 1), F32)
    for hd in range(N_Q_HEADS):
        sink = jnp.where(head == hd, sink_ref[hd], sink)
    s = jnp.einsum('bhc,bck->bhk', qm, ck.astype(BF16), preferred_element_type=F32)
    s_new = jnp.sum(qm.astype(F32) * kn[:, None, :], axis=2, keepdims=True)
    m = jnp.maximum(jnp.maximum(jnp.max(s, axis=2, keepdims=True), s_new), sink)
    p = jnp.exp(s - m)
    p_new = jnp.exp(s_new - m)
    den = jnp.sum(p, axis=2, keepdims=True) + p_new + jnp.exp(sink - m)
    o = jnp.einsum('bhk,bck->bhc', p.astype(BF16), cv.astype(BF16), preferred_element_type=F32)
    o_scr[...] = (o + p_new * vn[:, None, :]) / den
    low = lax.broadcasted_iota(jnp.int32, (1, LANES), 1) < HEAD_DIM
    for c in range(N_Q_HEADS // 2):
        even, odd = o_scr[:, 2 * c, :], o_scr[:, 2 * c + 1, :]
        if (2 * c) // GQA_GROUP == 0:
            odd = pltpu.roll(odd, HEAD_DIM, 1)
        else:
            even = pltpu.roll(even, HEAD_DIM, 1)
        o_ref[:, c * LANES:(c + 1) * LANES] = jnp.where(low, even, odd).astype(o_ref.dtype)
    pad = jnp.zeros((LANES - bt, ATTN_KV), F32)
    kn_t = jnp.concatenate([kn, pad], axis=0).T
    vn_t = jnp.concatenate([vn, pad], axis=0).T
    newest = lax.broadcasted_iota(jnp.int32, (ATTN_KV, WINDOW), 1) == WINDOW - 1
    for b in range(bt):
        ko_ref[b] = jnp.where(newest, kn_t[:, b:b + 1], pltpu.roll(ck[b], WINDOW - 1, 1))
        vo_ref[b] = jnp.where(newest, vn_t[:, b:b + 1], pltpu.roll(cv[b], WINDOW - 1, 1))


def _attn_decode(sinks, q_heads, k_new, v_new, cache_k, cache_v, bt):
    nbatch = q_heads.shape[0]
    b3 = lambda i: (i, 0, 0)
    b2 = lambda i: (i, 0)
    return pl.pallas_call(
        _attn_decode_kernel,
        grid=(nbatch // bt,),
        in_specs=[
            pl.BlockSpec(memory_space=pltpu.SMEM),
            pl.BlockSpec((bt, N_Q_HEADS * LANES), b2),
            pl.BlockSpec((bt, ATTN_KV), b2),
            pl.BlockSpec((bt, ATTN_KV), b2),
            pl.BlockSpec((bt, WINDOW, ATTN_KV), b3),
            pl.BlockSpec((bt, WINDOW, ATTN_KV), b3),
        ],
        out_specs=(
            pl.BlockSpec((bt, ATTN_Q), b2),
            pl.BlockSpec((bt, WINDOW, ATTN_KV), b3),
            pl.BlockSpec((bt, WINDOW, ATTN_KV), b3),
        ),
        out_shape=(
            jax.ShapeDtypeStruct((nbatch, ATTN_Q), BF16),
            jax.ShapeDtypeStruct((nbatch, WINDOW, ATTN_KV), F32),
            jax.ShapeDtypeStruct((nbatch, WINDOW, ATTN_KV), F32),
        ),
        scratch_shapes=[pltpu.VMEM((bt, N_Q_HEADS, ATTN_KV), F32), pltpu.VMEM((bt, N_Q_HEADS, ATTN_KV), BF16)],
        compiler_params=pltpu.CompilerParams(dimension_semantics=("parallel",)),
        name="attn_decode",
    )(sinks, q_heads, k_new, v_new, cache_k, cache_v)


def _mlstm_decode_kernel(q_ref, k_ref, v_ref, og_ref, gif_ref, c_ref, n_ref, m_ref, h_ref, co_ref, no_ref, mo_ref):
    bt = q_ref.shape[0]
    gif = gif_ref[...]
    ig = gif[:, 0:ML_HEADS]
    lf = gif[:, ML_HEADS:2 * ML_HEADS]
    m_prev = m_ref[...]
    a = lf + m_prev
    m_t = jnp.maximum(a, ig)
    aw = jnp.exp(a - m_t)
    e_i = jnp.exp(ig - m_t)
    floor = jnp.exp(-m_t)
    m_new = jnp.maximum(a, ig)
    decay = jnp.exp(a - m_new)
    w = jnp.exp(ig - m_new)
    mo_ref[...] = m_new
    pad = jnp.zeros((LANES - bt, ML_DIM), F32)
    lane = lax.broadcasted_iota(jnp.int32, (ML_DIM, LANES), 1)
    for h in range(ML_HEADS):
        hs = slice(h * ML_DIM, (h + 1) * ML_DIM)
        q = q_ref[:, hs]
        k = k_ref[:, hs]
        v = v_ref[:, hs]
        n = n_ref[:, h, :]
        qk = jnp.sum(q * k, axis=1, keepdims=True)
        qn = jnp.sum(q * n, axis=1, keepdims=True)
        sw = qk * e_i[:, h:h + 1]
        den = aw[:, h:h + 1] * qn + sw
        inv = 1.0 / jnp.maximum(jnp.abs(den), floor[:, h:h + 1])
        no_ref[:, h, :] = decay[:, h:h + 1] * n + w[:, h:h + 1] * k
        q_b = q.astype(BF16)
        wk_t = jnp.concatenate([w[:, h:h + 1] * k, pad], axis=0).T.astype(BF16)
        v_rows = jnp.concatenate([v, pad], axis=0).astype(BF16)
        nums = []
        for b in range(bt):
            C = c_ref[b, h]
            q_c = _mm(q_b, C.astype(BF16))[b:b + 1, :]
            nums.append(aw[b:b + 1, h:h + 1] * q_c)
            outer = _mm(jnp.where(lane == b, wk_t, jnp.zeros_like(wk_t)), v_rows)
            co_ref[b, h] = decay[b:b + 1, h:h + 1] * C + outer
        num = jnp.concatenate(nums, axis=0) + sw * v
        h_ref[:, hs] = (num * inv * og_ref[:, hs]).astype(h_ref.dtype)


def _mlstm_decode(mq, mk, mv, og, gif, state_c, state_n, state_m, bt):
    nbatch = mq.shape[0]
    b2 = lambda i: (i, 0)
    b3 = lambda i: (i, 0, 0)
    b4 = lambda i: (i, 0, 0, 0)
    return pl.pallas_call(
        _mlstm_decode_kernel,
        grid=(nbatch // bt,),
        in_specs=[
            pl.BlockSpec((bt, ML_W), b2),
            pl.BlockSpec((bt, ML_W), b2),
            pl.BlockSpec((bt, ML_W), b2),
            pl.BlockSpec((bt, ML_W), b2),
            pl.BlockSpec((bt, LANES), b2),
            pl.BlockSpec((bt, ML_HEADS, ML_DIM, ML_DIM), b4),
            pl.BlockSpec((bt, ML_HEADS, ML_DIM), b3),
            pl.BlockSpec((bt, ML_HEADS), b2),
        ],
        out_specs=(
            pl.BlockSpec((bt, ML_W), b2),
            pl.BlockSpec((bt, ML_HEADS, ML_DIM, ML_DIM), b4),
            pl.BlockSpec((bt, ML_HEADS, ML_DIM), b3),
            pl.BlockSpec((bt, ML_HEADS), b2),
        ),
        out_shape=(
            jax.ShapeDtypeStruct((nbatch, ML_W), BF16),
            jax.ShapeDtypeStruct((nbatch, ML_HEADS, ML_DIM, ML_DIM), F32),
            jax.ShapeDtypeStruct((nbatch, ML_HEADS, ML_DIM), F32),
            jax.ShapeDtypeStruct((nbatch, ML_HEADS), F32),
        ),
        compiler_params=pltpu.CompilerParams(dimension_semantics=("parallel",)),
        name="mlstm_decode",
    )(mq, mk, mv, og, gif, state_c, state_n, state_m)


def _rope_tables(pos):
    half = HEAD_DIM // 2
    inv = (1.0 / (ROPE_THETA ** (np.arange(half, dtype=np.float32) / half))).astype(np.float32)
    ang = pos.astype(np.float32)[:, None] * inv[None, :]
    cos = np.cos(ang)
    sin = np.sin(ang)
    cos_t = np.concatenate([cos, cos, cos, cos], axis=1)
    sin_t = np.concatenate([-sin, sin, -sin, sin], axis=1)
    return jnp.asarray(cos_t, F32), jnp.asarray(sin_t, F32)


def kernel(x_prompt, x_sample, cache_k, cache_v, state_C, state_n, state_m, p_prompt, p_sample, norm_mix, w_in, b_in, attn_sinks, w_attn_up, w_ml_up, w_o, norm_mlp, w_ff1, w_ff2, norm_ple, w_ple_gate, w_ple_proj, norm_final):
    assert w_in.shape[0] == 1, "single layer"
    batch, seq, _ = x_prompt.shape
    nbatch = x_sample.shape[0]
    assert x_sample.shape[1] == 1

    w_proj, w_gate, w_au, w_mu, w_out, w_f1, w_f2, w_pg, w_pp = _weight_cast(
        w_in[0].T, (w_attn_up[0], w_ml_up[0], w_o[0], w_ff1[0], w_ff2[0], w_ple_gate[0], w_ple_proj[0]))
    b_proj = b_in[:, :N_PROJ]
    b_gate = b_in[:, OFF_GA:]
    g_mix = norm_mix[0][None, :]
    out_weights = (
        g_mix, w_gate, b_gate, w_au, w_mu, w_out,
        norm_mlp[0][None, :], w_f1, w_f2,
        norm_ple[0][None, :], w_pg, w_pp, norm_final[None, :],
    )
    sinks = attn_sinks[0]
    to_fp = lambda c: c[0].transpose(0, 2, 3, 1).reshape(c.shape[1], ATTN_KV, WINDOW)
    from_fp = lambda c: c.reshape(c.shape[0], N_KV_HEADS, HEAD_DIM, WINDOW).transpose(0, 3, 1, 2)[None]

    cos_p, sin_p = _rope_tables(np.arange(seq, dtype=np.int32))
    xp = x_prompt.reshape(batch * seq, D_MODEL)
    y_prompt, k_last, v_last, cst, nst, mst = _prompt_fused(
        sinks, xp, p_prompt[0].reshape(batch * seq, D_PLE), cos_p, sin_p, g_mix, w_proj, b_proj, out_weights,
        batch, seq, tm=PROMPT_TILE)
    y_prompt = y_prompt.reshape(batch, seq, D_MODEL)
    k_prompt = from_fp(k_last)
    v_prompt = from_fp(v_last)
    c_prompt = cst[None]
    n_prompt = nst[:, :ML_HEADS, :][None]
    m_prompt = mst[:, :ML_HEADS, 0][None]

    cos_s, sin_s = _rope_tables(np.full((nbatch,), PAST_LEN, dtype=np.int32))
    xs = x_sample.reshape(nbatch, D_MODEL)
    qs, _, _, ks, vs, mqs, mks, mvs, ogs, gifs, _ = _in_proj_sample(xs, g_mix, w_proj, b_proj, cos_s, sin_s)
    a_s, k_s, v_s = _attn_decode(sinks, qs, ks, vs, to_fp(cache_k), to_fp(cache_v), bt=DECODE_ROWS)
    h_s, c_s, n_s, m_s = _mlstm_decode(mqs, mks, mvs, ogs, gifs, state_C[0], state_n[0], state_m[0], bt=DECODE_ROWS)
    y_sample = _out_proj_sample(xs, a_s, h_s, p_sample[0].reshape(nbatch, D_PLE), out_weights)
    y_sample = y_sample.reshape(nbatch, 1, D_MODEL)
    k_sample = from_fp(k_s)
    v_sample = from_fp(v_s)

    return (y_prompt, y_sample, k_prompt, v_prompt, c_prompt, n_prompt, m_prompt,
            k_sample, v_sample, c_s[None], n_s[None], m_s[None])
```

```python
import functools

import numpy as np
import jax
import jax.numpy as jnp
from jax import lax
from jax.experimental import pallas as pl
from jax.experimental.pallas import tpu as pltpu

F32 = jnp.float32
BF16 = jnp.bfloat16

D_MODEL = 1024
HEAD_DIM = 64
N_Q_HEADS = 8
N_KV_HEADS = 2
GQA_GROUP = N_Q_HEADS // N_KV_HEADS
WINDOW = 128
ROPE_THETA = 10000.0
ML_HEADS = 4
ML_DIM = 128
ML_CHUNK = 128
D_FF = 4 * D_MODEL
D_PLE = 256
EPS = 1e-6
PAST_LEN = 16384

ATTN_Q = N_Q_HEADS * HEAD_DIM
ATTN_KV = N_KV_HEADS * HEAD_DIM
ML_W = ML_HEADS * ML_DIM
LANES = 128

OFF_AQ = 0
OFF_AK = OFF_AQ + ATTN_Q
OFF_AV = OFF_AK + ATTN_KV
OFF_MQ = OFF_AV + ATTN_KV
OFF_MK = OFF_MQ + ML_W
OFF_MV = OFF_MK + ML_W
OFF_MO = OFF_MV + ML_W
OFF_MI = OFF_MO + ML_W
OFF_MF = OFF_MI + ML_HEADS
OFF_GA = OFF_MF + ML_HEADS
OFF_GM = OFF_GA + D_MODEL
D_IN = OFF_GM + D_MODEL
N_PROJ = OFF_MI + LANES

VMEM_LIMIT = 60 * 1024 * 1024
PROMPT_TILE = 256
DECODE_ROWS = 32

def _rms(x, g):
    r = lax.rsqrt(jnp.mean(x * x, axis=-1, keepdims=True) + EPS)
    return (x * r) * g


def _mm(a, b):
    return jnp.dot(a, b, preferred_element_type=F32)


def _log_sigmoid(x):
    return jnp.minimum(x, 0.0) - jnp.log1p(jnp.exp(-jnp.abs(x)))


def _weave(*streams):
    live = list(streams)
    while live:
        for g in list(live):
            try:
                next(g)
            except StopIteration:
                live.remove(g)


class _AwaitedRef:
    def __init__(self, ref, copy):
        self._ref, self._copy = ref, copy

    def __getitem__(self, idx):
        if self._copy is not None:
            self._copy.wait()
            self._copy = None
        return self._ref[idx]


WEIGHT_ROW_SPLIT = 8


def _weight_cast_kernel(*refs):
    n_plain = (len(refs) - 3) // 2
    w_in_t_ref, plain_in = refs[0], refs[1:1 + n_plain]
    w_proj_ref, w_gate_ref, plain_out = refs[1 + n_plain], refs[2 + n_plain], refs[3 + n_plain:]
    for g in range(N_PROJ // LANES):
        w_proj_ref[:, g * LANES:(g + 1) * LANES] = w_in_t_ref[g * LANES:(g + 1) * LANES, :].T.astype(BF16)
    for g in range(2 * D_MODEL // LANES):
        w_gate_ref[:, g * LANES:(g + 1) * LANES] = w_in_t_ref[OFF_GA + g * LANES:OFF_GA + (g + 1) * LANES, :].T.astype(BF16)
    for src, dst in zip(plain_in, plain_out):
        dst[...] = src[...].astype(BF16)


def _weight_cast(w_in_t, plain):
    split = WEIGHT_ROW_SPLIT
    assert D_MODEL // split == LANES
    rows_blk = lambda w: pl.BlockSpec((w.shape[0] // split, w.shape[1]), lambda i: (i, 0))
    out_shapes = (jax.ShapeDtypeStruct((D_MODEL, N_PROJ), BF16), jax.ShapeDtypeStruct((D_MODEL, 2 * D_MODEL), BF16),
                  *(jax.ShapeDtypeStruct(w.shape, BF16) for w in plain))
    return pl.pallas_call(
        _weight_cast_kernel,
        grid=(split,),
        in_specs=[pl.BlockSpec((D_IN, LANES), lambda i: (0, i))] + [rows_blk(w) for w in plain],
        out_specs=tuple(rows_blk(s) for s in out_shapes),
        out_shape=out_shapes,
        compiler_params=pltpu.CompilerParams(dimension_semantics=("parallel",), vmem_limit_bytes=VMEM_LIMIT),
        name="weight_cast",
    )(w_in_t, *plain)


def _in_proj_stages(x, g_ref, w_ref, b_ref, cos_ref, sin_ref,
                    q_ref, k_ref, v_ref, kl_ref, vl_ref, mq_ref, mk_ref, mv_ref, og_ref, gif_ref, gt_ref,
                    *, feature_major):
    xn = _rms(x, g_ref[...]).astype(BF16)
    tm = x.shape[0]

    def proj(lo, hi):
        return _mm(xn, w_ref[:, lo:hi]) + b_ref[:, lo:hi]

    cos = cos_ref[...]
    sin = sin_ref[...]
    lane = lax.broadcasted_iota(jnp.int32, (1, LANES), 1)
    first_half = (lane % HEAD_DIM) < (HEAD_DIM // 2)

    def rope(z):
        partner = jnp.where(first_half, pltpu.roll(z, LANES - HEAD_DIM // 2, 1), pltpu.roll(z, HEAD_DIM // 2, 1))
        return z * cos + partner * sin

    zq = proj(OFF_AQ, OFF_AK)
    for c in range(ATTN_Q // LANES):
        sl = slice(c * LANES, (c + 1) * LANES)
        qc = rope(zq[:, sl]) * (HEAD_DIM ** -0.5)
        if feature_major:
            q_ref[sl, :] = qc.T.astype(q_ref.dtype)
        else:
            kv = (2 * c) // GQA_GROUP
            own = (lane // HEAD_DIM) == kv
            swapped = pltpu.roll(qc, HEAD_DIM, 1)
            for j, src in enumerate((qc, swapped) if kv == 0 else (swapped, qc)):
                hd = 2 * c + j
                q_ref[:, hd * LANES:(hd + 1) * LANES] = jnp.where(own, src, 0.0).astype(q_ref.dtype)
    yield
    zkv = proj(OFF_AK, OFF_MQ)
    k = rope(zkv[:, :ATTN_KV])
    v = zkv[:, ATTN_KV:]
    k_ref[...] = k.astype(k_ref.dtype)
    v_ref[...] = (v.T if feature_major else v).astype(v_ref.dtype)
    if feature_major:
        kl_ref[0] = k[tm - WINDOW:, :].T
        vl_ref[0] = v[tm - WINDOW:, :].T
    else:
        kl_ref[...] = k[tm - WINDOW:, :]
        vl_ref[...] = v[tm - WINDOW:, :]
    yield
    for z_off, dst in ((OFF_MQ, mq_ref), (OFF_MV, mv_ref)):
        z = proj(z_off, z_off + ML_W)
        if feature_major:
            for h in range(ML_HEADS):
                hs = slice(h * ML_DIM, (h + 1) * ML_DIM)
                dst[0, hs, :] = z[:, hs].T.astype(dst.dtype)
        else:
            dst[...] = z.astype(dst.dtype)
        yield
    mk_ref[...] = (proj(OFF_MK, OFF_MV) * (ML_DIM ** -0.5)).astype(mk_ref.dtype)
    yield
    og_ref[...] = jax.nn.sigmoid(proj(OFF_MO, OFF_MI))
    yield
    zg = proj(OFF_MI, N_PROJ)
    gif = jnp.where(lane < ML_HEADS, zg, jnp.where(lane < 2 * ML_HEADS, _log_sigmoid(zg), 0.0))
    if feature_major:
        L = ML_CHUNK
        upper = (lax.broadcasted_iota(jnp.int32, (L, L), 0) <= lax.broadcasted_iota(jnp.int32, (L, L), 1)).astype(F32)
        row8 = lax.broadcasted_iota(jnp.int32, (8, 1), 0)
        fill = jnp.zeros((L - 16, L), F32)
        for c in range(tm // L):
            cs = slice(c * L, (c + 1) * L)
            g_rows = gif[cs, :].T[0:8, :]
            cum = jnp.dot(g_rows, upper, precision=lax.Precision.HIGHEST, preferred_element_type=F32)
            rows = jnp.where(row8 < ML_HEADS, g_rows, cum)
            gt_ref[0, :, cs] = rows
            diff = rows - pltpu.roll(rows, ML_HEADS, 0)
            gif_ref[cs, :] = jnp.concatenate([rows, diff, fill], axis=0).T
    else:
        gif_ref[...] = gif
        gt_ref[0] = jnp.zeros(gt_ref.shape[1:], F32)
    yield


def _in_proj_kernel(x_ref, *refs, feature_major):
    _weave(_in_proj_stages(x_ref[...], *refs, feature_major=feature_major))


def _in_proj_sample(x, g, w, b, cos, sin):
    rows = x.shape[0]
    assert rows == WINDOW
    full = lambda n, dt: (jax.ShapeDtypeStruct((rows, n), dt), pl.BlockSpec((rows, n), lambda i: (0, 0)))
    outs = (
        full(N_Q_HEADS * LANES, BF16), full(ATTN_KV, BF16), full(ATTN_KV, BF16), full(ATTN_KV, F32), full(ATTN_KV, F32),
        full(ML_W, F32), full(ML_W, F32), full(ML_W, F32), full(ML_W, F32), full(LANES, F32),
        (jax.ShapeDtypeStruct((1, 8, rows), F32), pl.BlockSpec((1, 8, rows), lambda i: (0, 0, 0))),
    )
    return pl.pallas_call(
        functools.partial(_in_proj_kernel, feature_major=False),
        grid=(1,),
        in_specs=[pl.BlockSpec(a.shape, lambda i: (0, 0)) for a in (x, g, w, b, cos, sin)],
        out_specs=tuple(o[1] for o in outs),
        out_shape=tuple(o[0] for o in outs),
        compiler_params=pltpu.CompilerParams(dimension_semantics=("arbitrary",), vmem_limit_bytes=VMEM_LIMIT),
        name="in_proj",
    )(x, g, w, b, cos, sin)


def _attn_stages(sink_ref, qt_ref, kp_ref, kc_ref, vtp_ref, vtc_ref, first_block, emit):
    L = WINDOW
    nsub = kc_ref.shape[0] // L
    key = lax.broadcasted_iota(jnp.int32, (2 * L, L), 0)
    t = lax.broadcasted_iota(jnp.int32, (2 * L, L), 1)
    is_prev = key < L
    allowed = (is_prev & (key >= t)) | (~is_prev & (key - L <= t))
    bias = jnp.where(allowed, 0.0, -jnp.inf)
    bias_first = jnp.where(is_prev & first_block, -jnp.inf, bias)
    zeros_q = jnp.zeros((HEAD_DIM, L), BF16)
    heads = range(N_Q_HEADS)
    for r in range(nsub):
        ts = slice(r * L, (r + 1) * L)
        k_prev = kp_ref[...] if r == 0 else kc_ref[(r - 1) * L:r * L, :]
        vt_prev = vtp_ref[...] if r == 0 else vtc_ref[:, (r - 1) * L:r * L]
        k2 = jnp.concatenate([k_prev, kc_ref[ts, :]], axis=0)
        vt2 = jnp.concatenate([vt_prev, vtc_ref[:, ts]], axis=1)
        b_r = bias_first if r == 0 else bias
        q_op = [jnp.concatenate([qt_ref[hd * HEAD_DIM:(hd + 1) * HEAD_DIM, ts], zeros_q][::1 if hd < GQA_GROUP else -1],
                                axis=0) for hd in heads]
        s = [_mm(k2, q_op[hd]) + b_r for hd in heads]
        yield
        m = [jnp.maximum(jnp.max(s[hd], axis=0, keepdims=True), sink_ref[hd]) for hd in heads]
        yield
        p = [jnp.exp(s[hd] - m[hd]) for hd in heads]
        yield
        den = [jnp.sum(p[hd], axis=0, keepdims=True) + jnp.exp(sink_ref[hd] - m[hd]) for hd in heads]
        o = [_mm(vt2[(hd // GQA_GROUP) * HEAD_DIM:(hd // GQA_GROUP + 1) * HEAD_DIM, :], p[hd].astype(BF16))
             * (1.0 / den[hd]) for hd in heads]
        yield
        emit(r, jnp.concatenate(o, axis=0).T)
        yield


ML_STATE_ROWS = ML_DIM + 8


def _mlstm_stages(chains, emit):
    L = ML_CHUNK
    s_idx = lax.broadcasted_iota(jnp.int32, (L, L), 0)
    t_idx = lax.broadcasted_iota(jnp.int32, (L, L), 1)
    causal = s_idx <= t_idx
    n = range(len(chains))
    dlog = [jnp.where(causal, ch["b_r"] + ch["u_c"], -jnp.inf) for ch in chains]
    yield
    qk = [_mm(ch["k"], ch["qt"]) for ch in chains]
    inter = [_mm(ch["state"].astype(BF16), ch["qt"]) for ch in chains]
    yield
    a = [ch["b_r"] + ch["m_prev"] for ch in chains]
    m_t = [jnp.maximum(a[i], jnp.max(dlog[i], axis=0, keepdims=True)) for i in n]
    yield
    sw = [qk[i] * jnp.exp(dlog[i] - m_t[i]) for i in n]
    yield
    sv = [_mm(chains[i]["vt"], sw[i].astype(BF16)) for i in n]
    yield
    hs = []
    for i in n:
        aw = jnp.exp(a[i] - m_t[i])
        num = aw * inter[i][:ML_DIM] + sv[i]
        den = aw * inter[i][ML_DIM:ML_DIM + 1] + jnp.sum(sw[i], axis=0, keepdims=True)
        hh = num * (1.0 / jnp.maximum(jnp.abs(den), jnp.exp(-m_t[i])))
        hs.append(hh.T * chains[i]["og"])
    yield
    pad = jnp.zeros((ML_STATE_ROWS - ML_DIM - 1, L), F32)
    new_state, new_m = [], []
    for i in n:
        ch = chains[i]
        b_end = ch["b_r"][:, L - 1:L]
        g = b_end - ch["b_r"] + ch["ig_r"]
        m_new = jnp.maximum(b_end + ch["m_prev"], jnp.max(g, axis=1, keepdims=True))
        decay = jnp.exp(b_end + ch["m_prev"] - m_new)
        w = jnp.exp(g - m_new)
        v_aug = jnp.concatenate([ch["vt"].astype(F32) * w, w, pad], axis=0).astype(BF16)
        new_state.append(decay * ch["state"] + _mm(v_aug, ch["k"]))
        new_m.append(m_new)
    emit(hs, new_state, new_m)
    yield


DENSE_COLS = 512


def _dense_stages(x, a, hm, p, gmix_ref, wg_ref, bg_ref, wau_ref, wmu_ref, wo_ref,
                  gmlp_ref, wff1_ref, wff2_ref, gple_ref, wpg_ref, wpp_ref, gfin_ref, emit):
    nc = DENSE_COLS
    pieces = lambda n: [slice(c, c + nc) for c in range(0, n, nc)]
    xn = _rms(x, gmix_ref[...]).astype(BF16)
    gates = []
    for cs in pieces(2 * D_MODEL):
        gates.append(jax.nn.sigmoid(_mm(xn, wg_ref[:, cs]) + bg_ref[:, cs]))
        yield
    half = D_MODEL // nc
    mix = []
    for j, cs in enumerate(pieces(D_MODEL)):
        mix.append((gates[j] * _mm(a, wau_ref[:, cs]) + gates[half + j] * _mm(hm, wmu_ref[:, cs])).astype(BF16))
        yield
    mix = jnp.concatenate(mix, axis=1)
    h = []
    for cs in pieces(D_MODEL):
        h.append(x[:, cs] + _mm(mix, wo_ref[:, cs]))
        yield
    h = jnp.concatenate(h, axis=1)
    hn = _rms(h, gmlp_ref[...]).astype(BF16)
    u = []
    for cs in pieces(D_FF):
        u.append(jnp.square(jnp.maximum(_mm(hn, wff1_ref[:, cs]), 0.0)).astype(BF16))
        yield
    u = jnp.concatenate(u, axis=1)
    h2 = []
    for cs in pieces(D_MODEL):
        h2.append(h[:, cs] + _mm(u, wff2_ref[:, cs]))
        yield
    h = jnp.concatenate(h2, axis=1)
    hn = _rms(h, gple_ref[...]).astype(BF16)
    pb = p.astype(BF16)
    h3 = []
    for cs in pieces(D_MODEL):
        h3.append(h[:, cs] + _mm(pb, wpp_ref[:, cs]) * jax.nn.sigmoid(_mm(hn, wpg_ref[:, cs])))
        yield
    emit(_rms(jnp.concatenate(h3, axis=1), gfin_ref[...]))
    yield


def _out_kernel(x_ref, a_ref, hm_ref, p_ref, *rest, matrix_idx):
    weights, y_ref = list(rest[:N_FUSED_WEIGHTS]), rest[N_FUSED_WEIGHTS]
    w_scr, dma_sem = rest[N_FUSED_WEIGHTS + 1:-1], rest[-1]
    copies = [pltpu.make_async_copy(weights[j], w_scr[n], dma_sem.at[n]) for n, j in enumerate(matrix_idx)]
    for c in copies:
        c.start()
    for n, j in enumerate(matrix_idx):
        weights[j] = _AwaitedRef(w_scr[n], copies[n])

    def emit(y):
        y_ref[...] = y

    _weave(_dense_stages(x_ref[...], a_ref[...], hm_ref[...], p_ref[...], *weights, emit=emit))


def _out_proj_sample(x, a, hm, p, weights):
    assert len(weights) == N_FUSED_WEIGHTS
    matrix_idx = tuple(j for j, w in enumerate(weights) if w.dtype == BF16)
    matrices = [weights[j] for j in matrix_idx]
    whole = lambda o: pl.BlockSpec(o.shape, lambda i: (0, 0), pipeline_mode=pl.Buffered(1))
    return pl.pallas_call(
        functools.partial(_out_kernel, matrix_idx=matrix_idx),
        grid=(1,),
        in_specs=[whole(o) for o in (x, a, hm, p)]
        + [pl.BlockSpec(memory_space=pl.ANY) if j in matrix_idx else whole(w) for j, w in enumerate(weights)],
        out_specs=pl.BlockSpec(x.shape, lambda i: (0, 0)),
        out_shape=jax.ShapeDtypeStruct(x.shape, F32),
        scratch_shapes=[*(pltpu.VMEM(w.shape, w.dtype) for w in matrices), pltpu.SemaphoreType.DMA((len(matrices),))],
        compiler_params=pltpu.CompilerParams(dimension_semantics=("arbitrary",), vmem_limit_bytes=VMEM_LIMIT),
        name="out_proj",
    )(x, a, hm, p, *weights)


N_FUSED_WEIGHTS = 13
N_PROJ_SLOTS = 9


def _prompt_kernel(sink_ref, x0_ref, cos_ref, sin_ref, x2_ref, p_ref, gmix_ref, wp_ref, bp_ref, *rest,
                   tiles_per_seq, matrix_idx):
    weights = list(rest[:N_FUSED_WEIGHTS])
    y_ref, kl_ref, vl_ref, cst_ref, nst_ref, mst_ref = rest[N_FUSED_WEIGHTS:N_FUSED_WEIGHTS + 6]
    scr = rest[N_FUSED_WEIGHTS + 6:]
    proj_scr = scr[:N_PROJ_SLOTS]
    kprev_scr, vtprev_scr, a_scr, h_scr, c_scr, m_scr = scr[N_PROJ_SLOTS:N_PROJ_SLOTS + 6]
    w_scr, dma_sem = scr[N_PROJ_SLOTS + 6:-1], scr[-1]
    w_copies = [pltpu.make_async_copy(weights[j], w_scr[n], dma_sem.at[n]) for n, j in enumerate(matrix_idx)]
    for n, j in enumerate(matrix_idx):
        weights[j] = w_scr[n]
    i = pl.program_id(0)
    n_tiles = pl.num_programs(0) - 2
    t1 = jnp.clip(i - 1, 0, n_tiles - 1)
    tile_in_seq = t1 % tiles_per_seq
    s0, s1 = i % 2, (i + 1) % 2
    L = ML_CHUNK
    tm = x0_ref.shape[0]

    @pl.when(i == 0)
    def _():
        for c in w_copies:
            c.start()
        for r in (*proj_scr, kprev_scr, vtprev_scr, a_scr, h_scr, c_scr, m_scr):
            r[...] = jnp.zeros_like(r)

    @pl.when(i == 2)
    def _():
        for c in w_copies:
            c.wait()

    @pl.when(tile_in_seq == 0)
    def _():
        c_scr[...] = jnp.zeros_like(c_scr)
        m_scr[...] = jnp.zeros_like(m_scr)

    def projection():
        outs = [r.at[s0] for r in proj_scr]
        yield from _in_proj_stages(x0_ref[...], gmix_ref, wp_ref, bp_ref, cos_ref, sin_ref,
                                   outs[0], outs[1], outs[2], kl_ref, vl_ref, *outs[3:], feature_major=True)

    def branches():
        qt_ref, kc_ref, vtc_ref, mqt_ref, mk_ref, mvt_ref, og_ref, gif_ref, gt_ref = [r.at[s1] for r in proj_scr]
        slot = s1

        def emit_attn(r, blk):
            a_scr[slot, r * WINDOW:(r + 1) * WINDOW, :] = blk.astype(a_scr.dtype)

        yield from _attn_stages(sink_ref, qt_ref, kprev_scr, kc_ref, vtprev_scr, vtc_ref, tile_in_seq == 0, emit_attn)
        kprev_scr[...] = kc_ref[tm - WINDOW:, :]
        vtprev_scr[...] = vtc_ref[:, tm - WINDOW:]
        hsl = lambda h: slice(h * ML_DIM, (h + 1) * ML_DIM)
        carry = dict(state=[c_scr[h] for h in range(ML_HEADS)], m=[m_scr[h:h + 1, :] for h in range(ML_HEADS)])
        for c in range(tm // L):
            cs = slice(c * L, (c + 1) * L)

            def emit_ml(hs, new_state, new_m, cs=cs):
                for h in range(ML_HEADS):
                    h_scr[slot, cs, hsl(h)] = hs[h].astype(h_scr.dtype)
                carry["state"], carry["m"] = new_state, new_m

            chains = [dict(qt=mqt_ref[0, hsl(h), cs], k=mk_ref[cs, hsl(h)], vt=mvt_ref[0, hsl(h), cs],
                           og=og_ref[cs, hsl(h)], ig_r=gt_ref[0, h:h + 1, cs],
                           b_r=gt_ref[0, ML_HEADS + h:ML_HEADS + h + 1, cs],
                           u_c=gif_ref[cs, 2 * ML_HEADS + h:2 * ML_HEADS + h + 1],
                           state=carry["state"][h], m_prev=carry["m"][h]) for h in range(ML_HEADS)]
            yield from _mlstm_stages(chains, emit_ml)
        for h in range(ML_HEADS):
            c_scr[h] = carry["state"][h]
            m_scr[h:h + 1, :] = carry["m"][h]

    def dense():
        def emit_y(y):
            y_ref[...] = y

        return _dense_stages(x2_ref[...], a_scr[s0], h_scr[s0], p_ref[...], *weights, emit=emit_y)

    @pl.when(i < 2)
    def _():
        _weave(branches(), projection())

    @pl.when((i >= 2) & (i < n_tiles))
    def _():
        _weave(branches(), dense(), projection())

    @pl.when(i >= n_tiles)
    def _():
        _weave(branches(), dense())

    @pl.when((tile_in_seq == tiles_per_seq - 1) & (i >= 1) & (i <= n_tiles))
    def _():
        nst_ref[...] = jnp.zeros_like(nst_ref)
        for h in range(ML_HEADS):
            final = c_scr[h]
            cst_ref[0, h] = final[:ML_DIM].T
            nst_ref[0, h:h + 1, :] = final[ML_DIM:ML_DIM + 1]
        mst_ref[0] = m_scr[...]


def _prompt_fused(sinks, x, p, cos, sin, g_mix, w_proj, b_proj, weights, batch, seq, tm):
    assert len(weights) == N_FUSED_WEIGHTS
    rows = batch * seq
    n_tiles = rows // tm
    tps = seq // tm
    t0 = lambda i: jnp.minimum(i, n_tiles - 1)
    t1 = lambda i: jnp.clip(i - 1, 0, n_tiles - 1)
    t2 = lambda i: jnp.maximum(i - 2, 0)
    const = lambda i: (0, 0)
    single = lambda a: pl.BlockSpec(a.shape, const, pipeline_mode=pl.Buffered(1))
    slot = lambda shape, dt: pltpu.VMEM((2,) + shape, dt)
    matrix_idx = tuple(j for j, w in enumerate(weights) if w.dtype == BF16)
    matrices = [weights[j] for j in matrix_idx]
    return pl.pallas_call(
        functools.partial(_prompt_kernel, tiles_per_seq=tps, matrix_idx=matrix_idx),
        grid=(n_tiles + 2,),
        in_specs=[
            pl.BlockSpec(memory_space=pltpu.SMEM),
            pl.BlockSpec((tm, D_MODEL), lambda i: (t0(i), 0)),
            pl.BlockSpec((tm, LANES), lambda i: (t0(i) % tps, 0)),
            pl.BlockSpec((tm, LANES), lambda i: (t0(i) % tps, 0)),
            pl.BlockSpec((tm, D_MODEL), lambda i: (t2(i), 0)),
            pl.BlockSpec((tm, D_PLE), lambda i: (t2(i), 0)),
            single(g_mix), single(w_proj), single(b_proj),
        ] + [pl.BlockSpec(memory_space=pl.ANY) if j in matrix_idx else single(w) for j, w in enumerate(weights)],
        out_specs=(
            pl.BlockSpec((tm, D_MODEL), lambda i: (t2(i), 0)),
            pl.BlockSpec((1, ATTN_KV, WINDOW), lambda i: (t0(i) // tps, 0, 0)),
            pl.BlockSpec((1, ATTN_KV, WINDOW), lambda i: (t0(i) // tps, 0, 0)),
            pl.BlockSpec((1, ML_HEADS, ML_DIM, ML_DIM), lambda i: (t1(i) // tps, 0, 0, 0)),
            pl.BlockSpec((1, 8, LANES), lambda i: (t1(i) // tps, 0, 0)),
            pl.BlockSpec((1, 8, LANES), lambda i: (t1(i) // tps, 0, 0)),
        ),
        out_shape=(
            jax.ShapeDtypeStruct((rows, D_MODEL), F32),
            jax.ShapeDtypeStruct((batch, ATTN_KV, WINDOW), F32),
            jax.ShapeDtypeStruct((batch, ATTN_KV, WINDOW), F32),
            jax.ShapeDtypeStruct((batch, ML_HEADS, ML_DIM, ML_DIM), F32),
            jax.ShapeDtypeStruct((batch, 8, LANES), F32),
            jax.ShapeDtypeStruct((batch, 8, LANES), F32),
        ),
        scratch_shapes=[
            slot((ATTN_Q, tm), BF16), slot((tm, ATTN_KV), BF16), slot((ATTN_KV, tm), BF16),
            slot((1, ML_W, tm), BF16), slot((tm, ML_W), BF16), slot((1, ML_W, tm), BF16),
            slot((tm, ML_W), F32), slot((tm, LANES), F32), slot((1, 8, tm), F32),
            pltpu.VMEM((WINDOW, ATTN_KV), BF16), pltpu.VMEM((ATTN_KV, WINDOW), BF16),
            slot((tm, ATTN_Q), BF16), slot((tm, ML_W), BF16),
            pltpu.VMEM((ML_HEADS, ML_STATE_ROWS, ML_DIM), F32), pltpu.VMEM((8, LANES), F32),
            *(pltpu.VMEM(w.shape, w.dtype) for w in matrices), pltpu.SemaphoreType.DMA((len(matrices),)),
        ],
        compiler_params=pltpu.CompilerParams(dimension_semantics=("arbitrary",), vmem_limit_bytes=VMEM_LIMIT),
        name="prompt_fused",
    )(sinks, x, cos, sin, x, p, g_mix, w_proj, b_proj, *weights)


def _attn_decode_kernel(sink_ref, q_ref, kn_ref, vn_ref, ck_ref, cv_ref, o_ref, ko_ref, vo_ref, o_scr, qm_scr):
    bt = q_ref.shape[0]
    for hd in range(N_Q_HEADS):
        qm_scr[:, hd, :] = q_ref[:, hd * LANES:(hd + 1) * LANES]
    qm = qm_scr[...]
    kn = kn_ref[...]
    vn = vn_ref[...]
    ck = ck_ref[...]
    cv = cv_ref[...]
    head = lax.broadcasted_iota(jnp.int32, (N_Q_HEADS, 1), 0)
    sink = jnp.zeros((N_Q_HEADS, 1), F32)
    for hd in range(N_Q_HEADS):
        sink = jnp.where(head == hd, sink_ref[hd], sink)
    s = jnp.einsum('bhc,bck->bhk', qm, ck.astype(BF16), preferred_element_type=F32)
    s_new = jnp.sum(qm.astype(F32) * kn[:, None, :], axis=2, keepdims=True)
    m = jnp.maximum(jnp.maximum(jnp.max(s, axis=2, keepdims=True), s_new), sink)
    p = jnp.exp(s - m)
    p_new = jnp.exp(s_new - m)
    den = jnp.sum(p, axis=2, keepdims=True) + p_new + jnp.exp(sink - m)
    o = jnp.einsum('bhk,bck->bhc', p.astype(BF16), cv.astype(BF16), preferred_element_type=F32)
    o_scr[...] = (o + p_new * vn[:, None, :]) / den
    low = lax.broadcasted_iota(jnp.int32, (1, LANES), 1) < HEAD_DIM
    for c in range(N_Q_HEADS // 2):
        even, odd = o_scr[:, 2 * c, :], o_scr[:, 2 * c + 1, :]
        if (2 * c) // GQA_GROUP == 0:
            odd = pltpu.roll(odd, HEAD_DIM, 1)
        else:
            even = pltpu.roll(even, HEAD_DIM, 1)
        o_ref[:, c * LANES:(c + 1) * LANES] = jnp.where(low, even, odd).astype(o_ref.dtype)
    pad = jnp.zeros((LANES - bt, ATTN_KV), F32)
    kn_t = jnp.concatenate([kn, pad], axis=0).T
    vn_t = jnp.concatenate([vn, pad], axis=0).T
    newest = lax.broadcasted_iota(jnp.int32, (ATTN_KV, WINDOW), 1) == WINDOW - 1
    for b in range(bt):
        ko_ref[b] = jnp.where(newest, kn_t[:, b:b + 1], pltpu.roll(ck[b], WINDOW - 1, 1))
        vo_ref[b] = jnp.where(newest, vn_t[:, b:b + 1], pltpu.roll(cv[b], WINDOW - 1, 1))


def _attn_decode(sinks, q_heads, k_new, v_new, cache_k, cache_v, bt):
    nbatch = q_heads.shape[0]
    b3 = lambda i: (i, 0, 0)
    b2 = lambda i: (i, 0)
    return pl.pallas_call(
        _attn_decode_kernel,
        grid=(nbatch // bt,),
        in_specs=[
            pl.BlockSpec(memory_space=pltpu.SMEM),
            pl.BlockSpec((bt, N_Q_HEADS * LANES), b2),
            pl.BlockSpec((bt, ATTN_KV), b2),
            pl.BlockSpec((bt, ATTN_KV), b2),
            pl.BlockSpec((bt, WINDOW, ATTN_KV), b3),
            pl.BlockSpec((bt, WINDOW, ATTN_KV), b3),
        ],
        out_specs=(
            pl.BlockSpec((bt, ATTN_Q), b2),
            pl.BlockSpec((bt, WINDOW, ATTN_KV), b3),
            pl.BlockSpec((bt, WINDOW, ATTN_KV), b3),
        ),
        out_shape=(
            jax.ShapeDtypeStruct((nbatch, ATTN_Q), BF16),
            jax.ShapeDtypeStruct((nbatch, WINDOW, ATTN_KV), F32),
            jax.ShapeDtypeStruct((nbatch, WINDOW, ATTN_KV), F32),
        ),
        scratch_shapes=[pltpu.VMEM((bt, N_Q_HEADS, ATTN_KV), F32), pltpu.VMEM((bt, N_Q_HEADS, ATTN_KV), BF16)],
        compiler_params=pltpu.CompilerParams(dimension_semantics=("parallel",)),
        name="attn_decode",
    )(sinks, q_heads, k_new, v_new, cache_k, cache_v)


def _mlstm_decode_kernel(q_ref, k_ref, v_ref, og_ref, gif_ref, c_ref, n_ref, m_ref, h_ref, co_ref, no_ref, mo_ref):
    bt = q_ref.shape[0]
    gif = gif_ref[...]
    ig = gif[:, 0:ML_HEADS]
    lf = gif[:, ML_HEADS:2 * ML_HEADS]
    m_prev = m_ref[...]
    a = lf + m_prev
    m_t = jnp.maximum(a, ig)
    aw = jnp.exp(a - m_t)
    e_i = jnp.exp(ig - m_t)
    floor = jnp.exp(-m_t)
    m_new = jnp.maximum(a, ig)
    decay = jnp.exp(a - m_new)
    w = jnp.exp(ig - m_new)
    mo_ref[...] = m_new
    pad = jnp.zeros((LANES - bt, ML_DIM), F32)
    lane = lax.broadcasted_iota(jnp.int32, (ML_DIM, LANES), 1)
    for h in range(ML_HEADS):
        hs = slice(h * ML_DIM, (h + 1) * ML_DIM)
        q = q_ref[:, hs]
        k = k_ref[:, hs]
        v = v_ref[:, hs]
        n = n_ref[:, h, :]
        qk = jnp.sum(q * k, axis=1, keepdims=True)
        qn = jnp.sum(q * n, axis=1, keepdims=True)
        sw = qk * e_i[:, h:h + 1]
        den = aw[:, h:h + 1] * qn + sw
        inv = 1.0 / jnp.maximum(jnp.abs(den), floor[:, h:h + 1])
        no_ref[:, h, :] = decay[:, h:h + 1] * n + w[:, h:h + 1] * k
        q_b = q.astype(BF16)
        wk_t = jnp.concatenate([w[:, h:h + 1] * k, pad], axis=0).T.astype(BF16)
        v_rows = jnp.concatenate([v, pad], axis=0).astype(BF16)
        nums = []
        for b in range(bt):
            C = c_ref[b, h]
            q_c = _mm(q_b, C.astype(BF16))[b:b + 1, :]
            nums.append(aw[b:b + 1, h:h + 1] * q_c)
            outer = _mm(jnp.where(lane == b, wk_t, jnp.zeros_like(wk_t)), v_rows)
            co_ref[b, h] = decay[b:b + 1, h:h + 1] * C + outer
        num = jnp.concatenate(nums, axis=0) + sw * v
        h_ref[:, hs] = (num * inv * og_ref[:, hs]).astype(h_ref.dtype)


def _mlstm_decode(mq, mk, mv, og, gif, state_c, state_n, state_m, bt):
    nbatch = mq.shape[0]
    b2 = lambda i: (i, 0)
    b3 = lambda i: (i, 0, 0)
    b4 = lambda i: (i, 0, 0, 0)
    return pl.pallas_call(
        _mlstm_decode_kernel,
        grid=(nbatch // bt,),
        in_specs=[
            pl.BlockSpec((bt, ML_W), b2),
            pl.BlockSpec((bt, ML_W), b2),
            pl.BlockSpec((bt, ML_W), b2),
            pl.BlockSpec((bt, ML_W), b2),
            pl.BlockSpec((bt, LANES), b2),
            pl.BlockSpec((bt, ML_HEADS, ML_DIM, ML_DIM), b4),
            pl.BlockSpec((bt, ML_HEADS, ML_DIM), b3),
            pl.BlockSpec((bt, ML_HEADS), b2),
        ],
        out_specs=(
            pl.BlockSpec((bt, ML_W), b2),
            pl.BlockSpec((bt, ML_HEADS, ML_DIM, ML_DIM), b4),
            pl.BlockSpec((bt, ML_HEADS, ML_DIM), b3),
            pl.BlockSpec((bt, ML_HEADS), b2),
        ),
        out_shape=(
            jax.ShapeDtypeStruct((nbatch, ML_W), BF16),
            jax.ShapeDtypeStruct((nbatch, ML_HEADS, ML_DIM, ML_DIM), F32),
            jax.ShapeDtypeStruct((nbatch, ML_HEADS, ML_DIM), F32),
            jax.ShapeDtypeStruct((nbatch, ML_HEADS), F32),
        ),
        compiler_params=pltpu.CompilerParams(dimension_semantics=("parallel",)),
        name="mlstm_decode",
    )(mq, mk, mv, og, gif, state_c, state_n, state_m)


def _rope_tables(pos):
    half = HEAD_DIM // 2
    inv = (1.0 / (ROPE_THETA ** (np.arange(half, dtype=np.float32) / half))).astype(np.float32)
    ang = pos.astype(np.float32)[:, None] * inv[None, :]
    cos = np.cos(ang)
    sin = np.sin(ang)
    cos_t = np.concatenate([cos, cos, cos, cos], axis=1)
    sin_t = np.concatenate([-sin, sin, -sin, sin], axis=1)
    return jnp.asarray(cos_t, F32), jnp.asarray(sin_t, F32)


def kernel(x_prompt, x_sample, cache_k, cache_v, state_C, state_n, state_m, p_prompt, p_sample, norm_mix, w_in, b_in, attn_sinks, w_attn_up, w_ml_up, w_o, norm_mlp, w_ff1, w_ff2, norm_ple, w_ple_gate, w_ple_proj, norm_final):
    assert w_in.shape[0] == 1, "single layer"
    batch, seq, _ = x_prompt.shape
    nbatch = x_sample.shape[0]
    assert x_sample.shape[1] == 1

    w_proj, w_gate, w_au, w_mu, w_out, w_f1, w_f2, w_pg, w_pp = _weight_cast(
        w_in[0].T, (w_attn_up[0], w_ml_up[0], w_o[0], w_ff1[0], w_ff2[0], w_ple_gate[0], w_ple_proj[0]))
    b_proj = b_in[:, :N_PROJ]
    b_gate = b_in[:, OFF_GA:]
    g_mix = norm_mix[0][None, :]
    out_weights = (
        g_mix, w_gate, b_gate, w_au, w_mu, w_out,
        norm_mlp[0][None, :], w_f1, w_f2,
        norm_ple[0][None, :], w_pg, w_pp, norm_final[None, :],
    )
    sinks = attn_sinks[0]
    to_fp = lambda c: c[0].transpose(0, 2, 3, 1).reshape(c.shape[1], ATTN_KV, WINDOW)
    from_fp = lambda c: c.reshape(c.shape[0], N_KV_HEADS, HEAD_DIM, WINDOW).transpose(0, 3, 1, 2)[None]

    cos_p, sin_p = _rope_tables(np.arange(seq, dtype=np.int32))
    xp = x_prompt.reshape(batch * seq, D_MODEL)
    y_prompt, k_last, v_last, cst, nst, mst = _prompt_fused(
        sinks, xp, p_prompt[0].reshape(batch * seq, D_PLE), cos_p, sin_p, g_mix, w_proj, b_proj, out_weights,
        batch, seq, tm=PROMPT_TILE)
    y_prompt = y_prompt.reshape(batch, seq, D_MODEL)
    k_prompt = from_fp(k_last)
    v_prompt = from_fp(v_last)
    c_prompt = cst[None]
    n_prompt = nst[:, :ML_HEADS, :][None]
    m_prompt = mst[:, :ML_HEADS, 0][None]

    cos_s, sin_s = _rope_tables(np.full((nbatch,), PAST_LEN, dtype=np.int32))
    xs = x_sample.reshape(nbatch, D_MODEL)
    qs, _, _, ks, vs, mqs, mks, mvs, ogs, gifs, _ = _in_proj_sample(xs, g_mix, w_proj, b_proj, cos_s, sin_s)
    a_s, k_s, v_s = _attn_decode(sinks, qs, ks, vs, to_fp(cache_k), to_fp(cache_v), bt=DECODE_ROWS)
    h_s, c_s, n_s, m_s = _mlstm_decode(mqs, mks, mvs, ogs, gifs, state_C[0], state_n[0], state_m[0], bt=DECODE_ROWS)
    y_sample = _out_proj_sample(xs, a_s, h_s, p_sample[0].reshape(nbatch, D_PLE), out_weights)
    y_sample = y_sample.reshape(nbatch, 1, D_MODEL)
    k_sample = from_fp(k_s)
    v_sample = from_fp(v_s)

    return (y_prompt, y_sample, k_prompt, v_prompt, c_prompt, n_prompt, m_prompt,
            k_sample, v_sample, c_s[None], n_s[None], m_s[None])
```

```python
import functools

import numpy as np
import jax
import jax.numpy as jnp
from jax import lax
from jax.experimental import pallas as pl
from jax.experimental.pallas import tpu as pltpu

F32 = jnp.float32
BF16 = jnp.bfloat16

D_MODEL = 1024
HEAD_DIM = 64
N_Q_HEADS = 8
N_KV_HEADS = 2
GQA_GROUP = N_Q_HEADS // N_KV_HEADS
WINDOW = 128
ROPE_THETA = 10000.0
ML_HEADS = 4
ML_DIM = 128
ML_CHUNK = 128
D_FF = 4 * D_MODEL
D_PLE = 256
EPS = 1e-6
PAST_LEN = 16384

ATTN_Q = N_Q_HEADS * HEAD_DIM
ATTN_KV = N_KV_HEADS * HEAD_DIM
ML_W = ML_HEADS * ML_DIM
LANES = 128

OFF_AQ = 0
OFF_AK = OFF_AQ + ATTN_Q
OFF_AV = OFF_AK + ATTN_KV
OFF_MQ = OFF_AV + ATTN_KV
OFF_MK = OFF_MQ + ML_W
OFF_MV = OFF_MK + ML_W
OFF_MO = OFF_MV + ML_W
OFF_MI = OFF_MO + ML_W
OFF_MF = OFF_MI + ML_HEADS
OFF_GA = OFF_MF + ML_HEADS
OFF_GM = OFF_GA + D_MODEL
D_IN = OFF_GM + D_MODEL
N_PROJ = OFF_MI + LANES

VMEM_LIMIT = 60 * 1024 * 1024
PROMPT_TILE = 256
DECODE_ROWS = 32

def _rms(x, g):
    r = lax.rsqrt(jnp.mean(x * x, axis=-1, keepdims=True) + EPS)
    return (x * r) * g


def _mm(a, b):
    return jnp.dot(a, b, preferred_element_type=F32)


def _sigmoid(x):
    return 0.5 * jnp.tanh(0.5 * x) + 0.5


def _log_sigmoid(x):
    return jnp.minimum(x, 0.0) - jnp.log1p(jnp.exp(-jnp.abs(x)))


def _weave(*streams):
    live = list(streams)
    while live:
        for g in list(live):
            try:
                next(g)
            except StopIteration:
                live.remove(g)


class _AwaitedRef:
    def __init__(self, ref, copy):
        self._ref, self._copy = ref, copy

    def __getitem__(self, idx):
        if self._copy is not None:
            self._copy.wait()
            self._copy = None
        return self._ref[idx]


WEIGHT_ROW_SPLIT = 8


def _weight_cast_kernel(*refs):
    n_plain = (len(refs) - 3) // 2
    w_in_t_ref, plain_in = refs[0], refs[1:1 + n_plain]
    w_proj_ref, w_gate_ref, plain_out = refs[1 + n_plain], refs[2 + n_plain], refs[3 + n_plain:]
    for g in range(N_PROJ // LANES):
        w_proj_ref[:, g * LANES:(g + 1) * LANES] = w_in_t_ref[g * LANES:(g + 1) * LANES, :].T.astype(BF16)
    for g in range(2 * D_MODEL // LANES):
        w_gate_ref[:, g * LANES:(g + 1) * LANES] = w_in_t_ref[OFF_GA + g * LANES:OFF_GA + (g + 1) * LANES, :].T.astype(BF16)
    for src, dst in zip(plain_in, plain_out):
        dst[...] = src[...].astype(BF16)


def _weight_cast(w_in_t, plain):
    split = WEIGHT_ROW_SPLIT
    assert D_MODEL // split == LANES
    rows_blk = lambda w: pl.BlockSpec((w.shape[0] // split, w.shape[1]), lambda i: (i, 0))
    out_shapes = (jax.ShapeDtypeStruct((D_MODEL, N_PROJ), BF16), jax.ShapeDtypeStruct((D_MODEL, 2 * D_MODEL), BF16),
                  *(jax.ShapeDtypeStruct(w.shape, BF16) for w in plain))
    return pl.pallas_call(
        _weight_cast_kernel,
        grid=(split,),
        in_specs=[pl.BlockSpec((D_IN, LANES), lambda i: (0, i))] + [rows_blk(w) for w in plain],
        out_specs=tuple(rows_blk(s) for s in out_shapes),
        out_shape=out_shapes,
        compiler_params=pltpu.CompilerParams(dimension_semantics=("parallel",), vmem_limit_bytes=VMEM_LIMIT),
        name="weight_cast",
    )(w_in_t, *plain)


def _in_proj_stages(x, g_ref, w_ref, b_ref, cos_ref, sin_ref,
                    q_ref, k_ref, v_ref, kl_ref, vl_ref, mq_ref, mk_ref, mv_ref, og_ref, gif_ref, gt_ref,
                    *, feature_major):
    xn = _rms(x, g_ref[...]).astype(BF16)
    tm = x.shape[0]

    def proj(lo, hi):
        return _mm(xn, w_ref[:, lo:hi]) + b_ref[:, lo:hi]

    cos = cos_ref[...]
    sin = sin_ref[...]
    lane = lax.broadcasted_iota(jnp.int32, (1, LANES), 1)
    first_half = (lane % HEAD_DIM) < (HEAD_DIM // 2)

    def rope(z):
        partner = jnp.where(first_half, pltpu.roll(z, LANES - HEAD_DIM // 2, 1), pltpu.roll(z, HEAD_DIM // 2, 1))
        return z * cos + partner * sin

    zq = proj(OFF_AQ, OFF_AK)
    for c in range(ATTN_Q // LANES):
        sl = slice(c * LANES, (c + 1) * LANES)
        qc = rope(zq[:, sl]) * (HEAD_DIM ** -0.5)
        if feature_major:
            q_ref[sl, :] = qc.T.astype(q_ref.dtype)
        else:
            kv = (2 * c) // GQA_GROUP
            own = (lane // HEAD_DIM) == kv
            swapped = pltpu.roll(qc, HEAD_DIM, 1)
            for j, src in enumerate((qc, swapped) if kv == 0 else (swapped, qc)):
                hd = 2 * c + j
                q_ref[:, hd * LANES:(hd + 1) * LANES] = jnp.where(own, src, 0.0).astype(q_ref.dtype)
    yield
    zkv = proj(OFF_AK, OFF_MQ)
    k = rope(zkv[:, :ATTN_KV])
    v = zkv[:, ATTN_KV:]
    k_ref[...] = k.astype(k_ref.dtype)
    v_ref[...] = (v.T if feature_major else v).astype(v_ref.dtype)
    if feature_major:
        kl_ref[0] = k[tm - WINDOW:, :].T
        vl_ref[0] = v[tm - WINDOW:, :].T
    else:
        kl_ref[...] = k[tm - WINDOW:, :]
        vl_ref[...] = v[tm - WINDOW:, :]
    yield
    for z_off, dst in ((OFF_MQ, mq_ref), (OFF_MV, mv_ref)):
        z = proj(z_off, z_off + ML_W)
        if feature_major:
            for h in range(ML_HEADS):
                hs = slice(h * ML_DIM, (h + 1) * ML_DIM)
                dst[0, hs, :] = z[:, hs].T.astype(dst.dtype)
        else:
            dst[...] = z.astype(dst.dtype)
        yield
    mk_ref[...] = (proj(OFF_MK, OFF_MV) * (ML_DIM ** -0.5)).astype(mk_ref.dtype)
    yield
    og_ref[...] = _sigmoid(proj(OFF_MO, OFF_MI))
    yield
    zg = proj(OFF_MI, N_PROJ)
    gif = jnp.where(lane < ML_HEADS, zg, jnp.where(lane < 2 * ML_HEADS, _log_sigmoid(zg), 0.0))
    if feature_major:
        L = ML_CHUNK
        upper = (lax.broadcasted_iota(jnp.int32, (L, L), 0) <= lax.broadcasted_iota(jnp.int32, (L, L), 1)).astype(F32)
        row8 = lax.broadcasted_iota(jnp.int32, (8, 1), 0)
        fill = jnp.zeros((L - 16, L), F32)
        for c in range(tm // L):
            cs = slice(c * L, (c + 1) * L)
            g_rows = gif[cs, :].T[0:8, :]
            cum = jnp.dot(g_rows, upper, precision=lax.Precision.HIGHEST, preferred_element_type=F32)
            rows = jnp.where(row8 < ML_HEADS, g_rows, cum)
            gt_ref[0, :, cs] = rows
            diff = rows - pltpu.roll(rows, ML_HEADS, 0)
            gif_ref[cs, :] = jnp.concatenate([rows, diff, fill], axis=0).T
    else:
        gif_ref[...] = gif
        gt_ref[0] = jnp.zeros(gt_ref.shape[1:], F32)
    yield


def _in_proj_kernel(x_ref, *refs, feature_major):
    _weave(_in_proj_stages(x_ref[...], *refs, feature_major=feature_major))


def _in_proj_sample(x, g, w, b, cos, sin):
    rows = x.shape[0]
    assert rows == WINDOW
    full = lambda n, dt: (jax.ShapeDtypeStruct((rows, n), dt), pl.BlockSpec((rows, n), lambda i: (0, 0)))
    outs = (
        full(N_Q_HEADS * LANES, BF16), full(ATTN_KV, BF16), full(ATTN_KV, BF16), full(ATTN_KV, F32), full(ATTN_KV, F32),
        full(ML_W, F32), full(ML_W, F32), full(ML_W, F32), full(ML_W, F32), full(LANES, F32),
        (jax.ShapeDtypeStruct((1, 8, rows), F32), pl.BlockSpec((1, 8, rows), lambda i: (0, 0, 0))),
    )
    return pl.pallas_call(
        functools.partial(_in_proj_kernel, feature_major=False),
        grid=(1,),
        in_specs=[pl.BlockSpec(a.shape, lambda i: (0, 0)) for a in (x, g, w, b, cos, sin)],
        out_specs=tuple(o[1] for o in outs),
        out_shape=tuple(o[0] for o in outs),
        compiler_params=pltpu.CompilerParams(dimension_semantics=("arbitrary",), vmem_limit_bytes=VMEM_LIMIT),
        name="in_proj",
    )(x, g, w, b, cos, sin)


def _attn_stages(sink_ref, qt_ref, kp_ref, kc_ref, vtp_ref, vtc_ref, first_block, emit):
    L = WINDOW
    nsub = kc_ref.shape[0] // L
    key = lax.broadcasted_iota(jnp.int32, (2 * L, L), 0)
    t = lax.broadcasted_iota(jnp.int32, (2 * L, L), 1)
    is_prev = key < L
    allowed = (is_prev & (key >= t)) | (~is_prev & (key - L <= t))
    bias = jnp.where(allowed, 0.0, -jnp.inf)
    bias_first = jnp.where(is_prev & first_block, -jnp.inf, bias)
    zeros_q = jnp.zeros((HEAD_DIM, L), BF16)
    heads = range(N_Q_HEADS)
    for r in range(nsub):
        ts = slice(r * L, (r + 1) * L)
        k_prev = kp_ref[...] if r == 0 else kc_ref[(r - 1) * L:r * L, :]
        vt_prev = vtp_ref[...] if r == 0 else vtc_ref[:, (r - 1) * L:r * L]
        k2 = jnp.concatenate([k_prev, kc_ref[ts, :]], axis=0)
        vt2 = jnp.concatenate([vt_prev, vtc_ref[:, ts]], axis=1)
        b_r = bias_first if r == 0 else bias
        q_op = [jnp.concatenate([qt_ref[hd * HEAD_DIM:(hd + 1) * HEAD_DIM, ts], zeros_q][::1 if hd < GQA_GROUP else -1],
                                axis=0) for hd in heads]
        s = [_mm(k2, q_op[hd]) + b_r for hd in heads]
        yield
        m = [jnp.maximum(jnp.max(s[hd], axis=0, keepdims=True), sink_ref[hd]) for hd in heads]
        yield
        p = [jnp.exp(s[hd] - m[hd]) for hd in heads]
        yield
        den = [jnp.sum(p[hd], axis=0, keepdims=True) + jnp.exp(sink_ref[hd] - m[hd]) for hd in heads]
        o = [_mm(vt2[(hd // GQA_GROUP) * HEAD_DIM:(hd // GQA_GROUP + 1) * HEAD_DIM, :], p[hd].astype(BF16))
             * (1.0 / den[hd]) for hd in heads]
        yield
        emit(r, jnp.concatenate(o, axis=0).T)
        yield


ML_STATE_ROWS = ML_DIM + 8


def _mlstm_stages(chains, emit):
    L = ML_CHUNK
    s_idx = lax.broadcasted_iota(jnp.int32, (L, L), 0)
    t_idx = lax.broadcasted_iota(jnp.int32, (L, L), 1)
    causal = s_idx <= t_idx
    n = range(len(chains))
    dlog = [jnp.where(causal, ch["b_r"] + ch["u_c"], -jnp.inf) for ch in chains]
    yield
    qk = [_mm(ch["k"], ch["qt"]) for ch in chains]
    inter = [_mm(ch["state"].astype(BF16), ch["qt"]) for ch in chains]
    yield
    a = [ch["b_r"] + ch["m_prev"] for ch in chains]
    m_t = [jnp.maximum(a[i], jnp.max(dlog[i], axis=0, keepdims=True)) for i in n]
    yield
    sw = [qk[i] * jnp.exp(dlog[i] - m_t[i]) for i in n]
    yield
    sv = [_mm(chains[i]["vt"], sw[i].astype(BF16)) for i in n]
    yield
    hs = []
    for i in n:
        aw = jnp.exp(a[i] - m_t[i])
        num = aw * inter[i][:ML_DIM] + sv[i]
        den = aw * inter[i][ML_DIM:ML_DIM + 1] + jnp.sum(sw[i], axis=0, keepdims=True)
        hh = num * (1.0 / jnp.maximum(jnp.abs(den), jnp.exp(-m_t[i])))
        hs.append(hh.T * chains[i]["og"])
    yield
    pad = jnp.zeros((ML_STATE_ROWS - ML_DIM - 1, L), F32)
    new_state, new_m = [], []
    for i in n:
        ch = chains[i]
        b_end = ch["b_r"][:, L - 1:L]
        g = b_end - ch["b_r"] + ch["ig_r"]
        m_new = jnp.maximum(b_end + ch["m_prev"], jnp.max(g, axis=1, keepdims=True))
        decay = jnp.exp(b_end + ch["m_prev"] - m_new)
        w = jnp.exp(g - m_new)
        v_aug = jnp.concatenate([ch["vt"].astype(F32) * w, w, pad], axis=0).astype(BF16)
        new_state.append(decay * ch["state"] + _mm(v_aug, ch["k"]))
        new_m.append(m_new)
    emit(hs, new_state, new_m)
    yield


DENSE_COLS = 512


def _dense_stages(x, a, hm, p, gmix_ref, wg_ref, bg_ref, wau_ref, wmu_ref, wo_ref,
                  gmlp_ref, wff1_ref, wff2_ref, gple_ref, wpg_ref, wpp_ref, gfin_ref, emit):
    nc = DENSE_COLS
    pieces = lambda n: [slice(c, c + nc) for c in range(0, n, nc)]
    xn = _rms(x, gmix_ref[...]).astype(BF16)
    gates = []
    for cs in pieces(2 * D_MODEL):
        gates.append(_sigmoid(_mm(xn, wg_ref[:, cs]) + bg_ref[:, cs]))
        yield
    half = D_MODEL // nc
    mix = []
    for j, cs in enumerate(pieces(D_MODEL)):
        mix.append((gates[j] * _mm(a, wau_ref[:, cs]) + gates[half + j] * _mm(hm, wmu_ref[:, cs])).astype(BF16))
        yield
    mix = jnp.concatenate(mix, axis=1)
    h = []
    for cs in pieces(D_MODEL):
        h.append(x[:, cs] + _mm(mix, wo_ref[:, cs]))
        yield
    h = jnp.concatenate(h, axis=1)
    hn = _rms(h, gmlp_ref[...]).astype(BF16)
    u = []
    for cs in pieces(D_FF):
        u.append(jnp.square(jnp.maximum(_mm(hn, wff1_ref[:, cs]), 0.0)).astype(BF16))
        yield
    u = jnp.concatenate(u, axis=1)
    h2 = []
    for cs in pieces(D_MODEL):
        h2.append(h[:, cs] + _mm(u, wff2_ref[:, cs]))
        yield
    h = jnp.concatenate(h2, axis=1)
    hn = _rms(h, gple_ref[...]).astype(BF16)
    pb = p.astype(BF16)
    h3 = []
    for cs in pieces(D_MODEL):
        h3.append(h[:, cs] + _mm(pb, wpp_ref[:, cs]) * _sigmoid(_mm(hn, wpg_ref[:, cs])))
        yield
    emit(_rms(jnp.concatenate(h3, axis=1), gfin_ref[...]))
    yield


def _out_kernel(x_ref, a_ref, hm_ref, p_ref, *rest, matrix_idx):
    weights, y_ref = list(rest[:N_FUSED_WEIGHTS]), rest[N_FUSED_WEIGHTS]
    w_scr, dma_sem = rest[N_FUSED_WEIGHTS + 1:-1], rest[-1]
    copies = [pltpu.make_async_copy(weights[j], w_scr[n], dma_sem.at[n]) for n, j in enumerate(matrix_idx)]
    for c in copies:
        c.start()
    for n, j in enumerate(matrix_idx):
        weights[j] = _AwaitedRef(w_scr[n], copies[n])

    def emit(y):
        y_ref[...] = y

    _weave(_dense_stages(x_ref[...], a_ref[...], hm_ref[...], p_ref[...], *weights, emit=emit))


def _out_proj_sample(x, a, hm, p, weights):
    assert len(weights) == N_FUSED_WEIGHTS
    matrix_idx = tuple(j for j, w in enumerate(weights) if w.dtype == BF16)
    matrices = [weights[j] for j in matrix_idx]
    whole = lambda o: pl.BlockSpec(o.shape, lambda i: (0, 0), pipeline_mode=pl.Buffered(1))
    return pl.pallas_call(
        functools.partial(_out_kernel, matrix_idx=matrix_idx),
        grid=(1,),
        in_specs=[whole(o) for o in (x, a, hm, p)]
        + [pl.BlockSpec(memory_space=pl.ANY) if j in matrix_idx else whole(w) for j, w in enumerate(weights)],
        out_specs=pl.BlockSpec(x.shape, lambda i: (0, 0)),
        out_shape=jax.ShapeDtypeStruct(x.shape, F32),
        scratch_shapes=[*(pltpu.VMEM(w.shape, w.dtype) for w in matrices), pltpu.SemaphoreType.DMA((len(matrices),))],
        compiler_params=pltpu.CompilerParams(dimension_semantics=("arbitrary",), vmem_limit_bytes=VMEM_LIMIT),
        name="out_proj",
    )(x, a, hm, p, *weights)


N_FUSED_WEIGHTS = 13
N_PROJ_SLOTS = 9


def _prompt_kernel(sink_ref, x0_ref, cos_ref, sin_ref, x2_ref, p_ref, gmix_ref, wp_ref, bp_ref, *rest,
                   tiles_per_seq, matrix_idx):
    weights = list(rest[:N_FUSED_WEIGHTS])
    y_ref, kl_ref, vl_ref, cst_ref, nst_ref, mst_ref = rest[N_FUSED_WEIGHTS:N_FUSED_WEIGHTS + 6]
    scr = rest[N_FUSED_WEIGHTS + 6:]
    proj_scr = scr[:N_PROJ_SLOTS]
    kprev_scr, vtprev_scr, a_scr, h_scr, c_scr, m_scr = scr[N_PROJ_SLOTS:N_PROJ_SLOTS + 6]
    w_scr, dma_sem = scr[N_PROJ_SLOTS + 6:-1], scr[-1]
    w_copies = [pltpu.make_async_copy(weights[j], w_scr[n], dma_sem.at[n]) for n, j in enumerate(matrix_idx)]
    for n, j in enumerate(matrix_idx):
        weights[j] = w_scr[n]
    i = pl.program_id(0)
    n_tiles = pl.num_programs(0) - 2
    t1 = jnp.clip(i - 1, 0, n_tiles - 1)
    tile_in_seq = t1 % tiles_per_seq
    s0, s1 = i % 2, (i + 1) % 2
    L = ML_CHUNK
    tm = x0_ref.shape[0]

    @pl.when(i == 0)
    def _():
        for c in w_copies:
            c.start()
        for r in (*proj_scr, kprev_scr, vtprev_scr, a_scr, h_scr, c_scr, m_scr):
            r[...] = jnp.zeros_like(r)

    @pl.when(i == 2)
    def _():
        for c in w_copies:
            c.wait()

    @pl.when(tile_in_seq == 0)
    def _():
        c_scr[...] = jnp.zeros_like(c_scr)
        m_scr[...] = jnp.zeros_like(m_scr)

    def projection():
        outs = [r.at[s0] for r in proj_scr]
        yield from _in_proj_stages(x0_ref[...], gmix_ref, wp_ref, bp_ref, cos_ref, sin_ref,
                                   outs[0], outs[1], outs[2], kl_ref, vl_ref, *outs[3:], feature_major=True)

    def branches():
        qt_ref, kc_ref, vtc_ref, mqt_ref, mk_ref, mvt_ref, og_ref, gif_ref, gt_ref = [r.at[s1] for r in proj_scr]
        slot = s1

        def emit_attn(r, blk):
            a_scr[slot, r * WINDOW:(r + 1) * WINDOW, :] = blk.astype(a_scr.dtype)

        yield from _attn_stages(sink_ref, qt_ref, kprev_scr, kc_ref, vtprev_scr, vtc_ref, tile_in_seq == 0, emit_attn)
        kprev_scr[...] = kc_ref[tm - WINDOW:, :]
        vtprev_scr[...] = vtc_ref[:, tm - WINDOW:]
        hsl = lambda h: slice(h * ML_DIM, (h + 1) * ML_DIM)
        carry = dict(state=[c_scr[h] for h in range(ML_HEADS)], m=[m_scr[h:h + 1, :] for h in range(ML_HEADS)])
        for c in range(tm // L):
            cs = slice(c * L, (c + 1) * L)

            def emit_ml(hs, new_state, new_m, cs=cs):
                for h in range(ML_HEADS):
                    h_scr[slot, cs, hsl(h)] = hs[h].astype(h_scr.dtype)
                carry["state"], carry["m"] = new_state, new_m

            chains = [dict(qt=mqt_ref[0, hsl(h), cs], k=mk_ref[cs, hsl(h)], vt=mvt_ref[0, hsl(h), cs],
                           og=og_ref[cs, hsl(h)], ig_r=gt_ref[0, h:h + 1, cs],
                           b_r=gt_ref[0, ML_HEADS + h:ML_HEADS + h + 1, cs],
                           u_c=gif_ref[cs, 2 * ML_HEADS + h:2 * ML_HEADS + h + 1],
                           state=carry["state"][h], m_prev=carry["m"][h]) for h in range(ML_HEADS)]
            yield from _mlstm_stages(chains, emit_ml)
        for h in range(ML_HEADS):
            c_scr[h] = carry["state"][h]
            m_scr[h:h + 1, :] = carry["m"][h]

    def dense():
        def emit_y(y):
            y_ref[...] = y

        return _dense_stages(x2_ref[...], a_scr[s0], h_scr[s0], p_ref[...], *weights, emit=emit_y)

    @pl.when(i < 2)
    def _():
        _weave(branches(), projection())

    @pl.when((i >= 2) & (i < n_tiles))
    def _():
        _weave(branches(), dense(), projection())

    @pl.when(i >= n_tiles)
    def _():
        _weave(branches(), dense())

    @pl.when((tile_in_seq == tiles_per_seq - 1) & (i >= 1) & (i <= n_tiles))
    def _():
        nst_ref[...] = jnp.zeros_like(nst_ref)
        for h in range(ML_HEADS):
            final = c_scr[h]
            cst_ref[0, h] = final[:ML_DIM].T
            nst_ref[0, h:h + 1, :] = final[ML_DIM:ML_DIM + 1]
        mst_ref[0] = m_scr[...]


def _prompt_fused(sinks, x, p, cos, sin, g_mix, w_proj, b_proj, weights, batch, seq, tm):
    assert len(weights) == N_FUSED_WEIGHTS
    rows = batch * seq
    n_tiles = rows // tm
    tps = seq // tm
    t0 = lambda i: jnp.minimum(i, n_tiles - 1)
    t1 = lambda i: jnp.clip(i - 1, 0, n_tiles - 1)
    t2 = lambda i: jnp.maximum(i - 2, 0)
    const = lambda i: (0, 0)
    single = lambda a: pl.BlockSpec(a.shape, const, pipeline_mode=pl.Buffered(1))
    slot = lambda shape, dt: pltpu.VMEM((2,) + shape, dt)
    matrix_idx = tuple(j for j, w in enumerate(weights) if w.dtype == BF16)
    matrices = [weights[j] for j in matrix_idx]
    return pl.pallas_call(
        functools.partial(_prompt_kernel, tiles_per_seq=tps, matrix_idx=matrix_idx),
        grid=(n_tiles + 2,),
        in_specs=[
            pl.BlockSpec(memory_space=pltpu.SMEM),
            pl.BlockSpec((tm, D_MODEL), lambda i: (t0(i), 0)),
            pl.BlockSpec((tm, LANES), lambda i: (t0(i) % tps, 0)),
            pl.BlockSpec((tm, LANES), lambda i: (t0(i) % tps, 0)),
            pl.BlockSpec((tm, D_MODEL), lambda i: (t2(i), 0)),
            pl.BlockSpec((tm, D_PLE), lambda i: (t2(i), 0)),
            single(g_mix), single(w_proj), single(b_proj),
        ] + [pl.BlockSpec(memory_space=pl.ANY) if j in matrix_idx else single(w) for j, w in enumerate(weights)],
        out_specs=(
            pl.BlockSpec((tm, D_MODEL), lambda i: (t2(i), 0)),
            pl.BlockSpec((1, ATTN_KV, WINDOW), lambda i: (t0(i) // tps, 0, 0)),
            pl.BlockSpec((1, ATTN_KV, WINDOW), lambda i: (t0(i) // tps, 0, 0)),
            pl.BlockSpec((1, ML_HEADS, ML_DIM, ML_DIM), lambda i: (t1(i) // tps, 0, 0, 0)),
            pl.BlockSpec((1, 8, LANES), lambda i: (t1(i) // tps, 0, 0)),
            pl.BlockSpec((1, 8, LANES), lambda i: (t1(i) // tps, 0, 0)),
        ),
        out_shape=(
            jax.ShapeDtypeStruct((rows, D_MODEL), F32),
            jax.ShapeDtypeStruct((batch, ATTN_KV, WINDOW), F32),
            jax.ShapeDtypeStruct((batch, ATTN_KV, WINDOW), F32),
            jax.ShapeDtypeStruct((batch, ML_HEADS, ML_DIM, ML_DIM), F32),
            jax.ShapeDtypeStruct((batch, 8, LANES), F32),
            jax.ShapeDtypeStruct((batch, 8, LANES), F32),
        ),
        scratch_shapes=[
            slot((ATTN_Q, tm), BF16), slot((tm, ATTN_KV), BF16), slot((ATTN_KV, tm), BF16),
            slot((1, ML_W, tm), BF16), slot((tm, ML_W), BF16), slot((1, ML_W, tm), BF16),
            slot((tm, ML_W), F32), slot((tm, LANES), F32), slot((1, 8, tm), F32),
            pltpu.VMEM((WINDOW, ATTN_KV), BF16), pltpu.VMEM((ATTN_KV, WINDOW), BF16),
            slot((tm, ATTN_Q), BF16), slot((tm, ML_W), BF16),
            pltpu.VMEM((ML_HEADS, ML_STATE_ROWS, ML_DIM), F32), pltpu.VMEM((8, LANES), F32),
            *(pltpu.VMEM(w.shape, w.dtype) for w in matrices), pltpu.SemaphoreType.DMA((len(matrices),)),
        ],
        compiler_params=pltpu.CompilerParams(dimension_semantics=("arbitrary",), vmem_limit_bytes=VMEM_LIMIT),
        name="prompt_fused",
    )(sinks, x, cos, sin, x, p, g_mix, w_proj, b_proj, *weights)


def _attn_decode_kernel(sink_ref, q_ref, kn_ref, vn_ref, ck_ref, cv_ref, o_ref, ko_ref, vo_ref, o_scr, qm_scr):
    bt = q_ref.shape[0]
    for hd in range(N_Q_HEADS):
        qm_scr[:, hd, :] = q_ref[:, hd * LANES:(hd + 1) * LANES]
    qm = qm_scr[...]
    kn = kn_ref[...]
    vn = vn_ref[...]
    ck = ck_ref[...]
    cv = cv_ref[...]
    head = lax.broadcasted_iota(jnp.int32, (N_Q_HEADS, 1), 0)
    sink = jnp.zeros((N_Q_HEADS, 1), F32)
    for hd in range(N_Q_HEADS):
        sink = jnp.where(head == hd, sink_ref[hd], sink)
    s = jnp.einsum('bhc,bck->bhk', qm, ck.astype(BF16), preferred_element_type=F32)
    s_new = jnp.sum(qm.astype(F32) * kn[:, None, :], axis=2, keepdims=True)
    m = jnp.maximum(jnp.maximum(jnp.max(s, axis=2, keepdims=True), s_new), sink)
    p = jnp.exp(s - m)
    p_new = jnp.exp(s_new - m)
    den = jnp.sum(p, axis=2, keepdims=True) + p_new + jnp.exp(sink - m)
    o = jnp.einsum('bhk,bck->bhc', p.astype(BF16), cv.astype(BF16), preferred_element_type=F32)
    o_scr[...] = (o + p_new * vn[:, None, :]) / den
    low = lax.broadcasted_iota(jnp.int32, (1, LANES), 1) < HEAD_DIM
    for c in range(N_Q_HEADS // 2):
        even, odd = o_scr[:, 2 * c, :], o_scr[:, 2 * c + 1, :]
        if (2 * c) // GQA_GROUP == 0:
            odd = pltpu.roll(odd, HEAD_DIM, 1)
        else:
            even = pltpu.roll(even, HEAD_DIM, 1)
        o_ref[:, c * LANES:(c + 1) * LANES] = jnp.where(low, even, odd).astype(o_ref.dtype)
    pad = jnp.zeros((LANES - bt, ATTN_KV), F32)
    kn_t = jnp.concatenate([kn, pad], axis=0).T
    vn_t = jnp.concatenate([vn, pad], axis=0).T
    newest = lax.broadcasted_iota(jnp.int32, (ATTN_KV, WINDOW), 1) == WINDOW - 1
    for b in range(bt):
        ko_ref[b] = jnp.where(newest, kn_t[:, b:b + 1], pltpu.roll(ck[b], WINDOW - 1, 1))
        vo_ref[b] = jnp.where(newest, vn_t[:, b:b + 1], pltpu.roll(cv[b], WINDOW - 1, 1))


def _attn_decode(sinks, q_heads, k_new, v_new, cache_k, cache_v, bt):
    nbatch = q_heads.shape[0]
    b3 = lambda i: (i, 0, 0)
    b2 = lambda i: (i, 0)
    return pl.pallas_call(
        _attn_decode_kernel,
        grid=(nbatch // bt,),
        in_specs=[
            pl.BlockSpec(memory_space=pltpu.SMEM),
            pl.BlockSpec((bt, N_Q_HEADS * LANES), b2),
            pl.BlockSpec((bt, ATTN_KV), b2),
            pl.BlockSpec((bt, ATTN_KV), b2),
            pl.BlockSpec((bt, WINDOW, ATTN_KV), b3),
            pl.BlockSpec((bt, WINDOW, ATTN_KV), b3),
        ],
        out_specs=(
            pl.BlockSpec((bt, ATTN_Q), b2),
            pl.BlockSpec((bt, WINDOW, ATTN_KV), b3),
            pl.BlockSpec((bt, WINDOW, ATTN_KV), b3),
        ),
        out_shape=(
            jax.ShapeDtypeStruct((nbatch, ATTN_Q), BF16),
            jax.ShapeDtypeStruct((nbatch, WINDOW, ATTN_KV), F32),
            jax.ShapeDtypeStruct((nbatch, WINDOW, ATTN_KV), F32),
        ),
        scratch_shapes=[pltpu.VMEM((bt, N_Q_HEADS, ATTN_KV), F32), pltpu.VMEM((bt, N_Q_HEADS, ATTN_KV), BF16)],
        compiler_params=pltpu.CompilerParams(dimension_semantics=("parallel",)),
        name="attn_decode",
    )(sinks, q_heads, k_new, v_new, cache_k, cache_v)


def _mlstm_decode_kernel(q_ref, k_ref, v_ref, og_ref, gif_ref, c_ref, n_ref, m_ref, h_ref, co_ref, no_ref, mo_ref):
    bt = q_ref.shape[0]
    gif = gif_ref[...]
    ig = gif[:, 0:ML_HEADS]
    lf = gif[:, ML_HEADS:2 * ML_HEADS]
    m_prev = m_ref[...]
    a = lf + m_prev
    m_t = jnp.maximum(a, ig)
    aw = jnp.exp(a - m_t)
    e_i = jnp.exp(ig - m_t)
    floor = jnp.exp(-m_t)
    m_new = jnp.maximum(a, ig)
    decay = jnp.exp(a - m_new)
    w = jnp.exp(ig - m_new)
    mo_ref[...] = m_new
    pad = jnp.zeros((LANES - bt, ML_DIM), F32)
    lane = lax.broadcasted_iota(jnp.int32, (ML_DIM, LANES), 1)
    for h in range(ML_HEADS):
        hs = slice(h * ML_DIM, (h + 1) * ML_DIM)
        q = q_ref[:, hs]
        k = k_ref[:, hs]
        v = v_ref[:, hs]
        n = n_ref[:, h, :]
        qk = jnp.sum(q * k, axis=1, keepdims=True)
        qn = jnp.sum(q * n, axis=1, keepdims=True)
        sw = qk * e_i[:, h:h + 1]
        den = aw[:, h:h + 1] * qn + sw
        inv = 1.0 / jnp.maximum(jnp.abs(den), floor[:, h:h + 1])
        no_ref[:, h, :] = decay[:, h:h + 1] * n + w[:, h:h + 1] * k
        q_b = q.astype(BF16)
        wk_t = jnp.concatenate([w[:, h:h + 1] * k, pad], axis=0).T.astype(BF16)
        v_rows = jnp.concatenate([v, pad], axis=0).astype(BF16)
        nums = []
        for b in range(bt):
            C = c_ref[b, h]
            q_c = _mm(q_b, C.astype(BF16))[b:b + 1, :]
            nums.append(aw[b:b + 1, h:h + 1] * q_c)
            outer = _mm(jnp.where(lane == b, wk_t, jnp.zeros_like(wk_t)), v_rows)
            co_ref[b, h] = decay[b:b + 1, h:h + 1] * C + outer
        num = jnp.concatenate(nums, axis=0) + sw * v
        h_ref[:, hs] = (num * inv * og_ref[:, hs]).astype(h_ref.dtype)


def _mlstm_decode(mq, mk, mv, og, gif, state_c, state_n, state_m, bt):
    nbatch = mq.shape[0]
    b2 = lambda i: (i, 0)
    b3 = lambda i: (i, 0, 0)
    b4 = lambda i: (i, 0, 0, 0)
    return pl.pallas_call(
        _mlstm_decode_kernel,
        grid=(nbatch // bt,),
        in_specs=[
            pl.BlockSpec((bt, ML_W), b2),
            pl.BlockSpec((bt, ML_W), b2),
            pl.BlockSpec((bt, ML_W), b2),
            pl.BlockSpec((bt, ML_W), b2),
            pl.BlockSpec((bt, LANES), b2),
            pl.BlockSpec((bt, ML_HEADS, ML_DIM, ML_DIM), b4),
            pl.BlockSpec((bt, ML_HEADS, ML_DIM), b3),
            pl.BlockSpec((bt, ML_HEADS), b2),
        ],
        out_specs=(
            pl.BlockSpec((bt, ML_W), b2),
            pl.BlockSpec((bt, ML_HEADS, ML_DIM, ML_DIM), b4),
            pl.BlockSpec((bt, ML_HEADS, ML_DIM), b3),
            pl.BlockSpec((bt, ML_HEADS), b2),
        ),
        out_shape=(
            jax.ShapeDtypeStruct((nbatch, ML_W), BF16),
            jax.ShapeDtypeStruct((nbatch, ML_HEADS, ML_DIM, ML_DIM), F32),
            jax.ShapeDtypeStruct((nbatch, ML_HEADS, ML_DIM), F32),
            jax.ShapeDtypeStruct((nbatch, ML_HEADS), F32),
        ),
        compiler_params=pltpu.CompilerParams(dimension_semantics=("parallel",)),
        name="mlstm_decode",
    )(mq, mk, mv, og, gif, state_c, state_n, state_m)


def _rope_tables(pos):
    half = HEAD_DIM // 2
    inv = (1.0 / (ROPE_THETA ** (np.arange(half, dtype=np.float32) / half))).astype(np.float32)
    ang = pos.astype(np.float32)[:, None] * inv[None, :]
    cos = np.cos(ang)
    sin = np.sin(ang)
    cos_t = np.concatenate([cos, cos, cos, cos], axis=1)
    sin_t = np.concatenate([-sin, sin, -sin, sin], axis=1)
    return jnp.asarray(cos_t, F32), jnp.asarray(sin_t, F32)


def kernel(x_prompt, x_sample, cache_k, cache_v, state_C, state_n, state_m, p_prompt, p_sample, norm_mix, w_in, b_in, attn_sinks, w_attn_up, w_ml_up, w_o, norm_mlp, w_ff1, w_ff2, norm_ple, w_ple_gate, w_ple_proj, norm_final):
    assert w_in.shape[0] == 1, "single layer"
    batch, seq, _ = x_prompt.shape
    nbatch = x_sample.shape[0]
    assert x_sample.shape[1] == 1

    w_proj, w_gate, w_au, w_mu, w_out, w_f1, w_f2, w_pg, w_pp = _weight_cast(
        w_in[0].T, (w_attn_up[0], w_ml_up[0], w_o[0], w_ff1[0], w_ff2[0], w_ple_gate[0], w_ple_proj[0]))
    b_proj = b_in[:, :N_PROJ]
    b_gate = b_in[:, OFF_GA:]
    g_mix = norm_mix[0][None, :]
    out_weights = (
        g_mix, w_gate, b_gate, w_au, w_mu, w_out,
        norm_mlp[0][None, :], w_f1, w_f2,
        norm_ple[0][None, :], w_pg, w_pp, norm_final[None, :],
    )
    sinks = attn_sinks[0]
    to_fp = lambda c: c[0].transpose(0, 2, 3, 1).reshape(c.shape[1], ATTN_KV, WINDOW)
    from_fp = lambda c: c.reshape(c.shape[0], N_KV_HEADS, HEAD_DIM, WINDOW).transpose(0, 3, 1, 2)[None]

    cos_p, sin_p = _rope_tables(np.arange(seq, dtype=np.int32))
    xp = x_prompt.reshape(batch * seq, D_MODEL)
    y_prompt, k_last, v_last, cst, nst, mst = _prompt_fused(
        sinks, xp, p_prompt[0].reshape(batch * seq, D_PLE), cos_p, sin_p, g_mix, w_proj, b_proj, out_weights,
        batch, seq, tm=PROMPT_TILE)
    y_prompt = y_prompt.reshape(batch, seq, D_MODEL)
    k_prompt = from_fp(k_last)
    v_prompt = from_fp(v_last)
    c_prompt = cst[None]
    n_prompt = nst[:, :ML_HEADS, :][None]
    m_prompt = mst[:, :ML_HEADS, 0][None]

    cos_s, sin_s = _rope_tables(np.full((nbatch,), PAST_LEN, dtype=np.int32))
    xs = x_sample.reshape(nbatch, D_MODEL)
    qs, _, _, ks, vs, mqs, mks, mvs, ogs, gifs, _ = _in_proj_sample(xs, g_mix, w_proj, b_proj, cos_s, sin_s)
    a_s, k_s, v_s = _attn_decode(sinks, qs, ks, vs, to_fp(cache_k), to_fp(cache_v), bt=DECODE_ROWS)
    h_s, c_s, n_s, m_s = _mlstm_decode(mqs, mks, mvs, ogs, gifs, state_C[0], state_n[0], state_m[0], bt=DECODE_ROWS)
    y_sample = _out_proj_sample(xs, a_s, h_s, p_sample[0].reshape(nbatch, D_PLE), out_weights)
    y_sample = y_sample.reshape(nbatch, 1, D_MODEL)
    k_sample = from_fp(k_s)
    v_sample = from_fp(v_s)

    return (y_prompt, y_sample, k_prompt, v_prompt, c_prompt, n_prompt, m_prompt,
            k_sample, v_sample, c_s[None], n_s[None], m_s[None])
```

```python
import functools

import numpy as np
import jax
import jax.numpy as jnp
from jax import lax
from jax.experimental import pallas as pl
from jax.experimental.pallas import tpu as pltpu

F32 = jnp.float32
BF16 = jnp.bfloat16

D_MODEL = 1024
HEAD_DIM = 64
N_Q_HEADS = 8
N_KV_HEADS = 2
GQA_GROUP = N_Q_HEADS // N_KV_HEADS
WINDOW = 128
ROPE_THETA = 10000.0
ML_HEADS = 4
ML_DIM = 128
ML_CHUNK = 128
D_FF = 4 * D_MODEL
D_PLE = 256
EPS = 1e-6
PAST_LEN = 16384

ATTN_Q = N_Q_HEADS * HEAD_DIM
ATTN_KV = N_KV_HEADS * HEAD_DIM
ML_W = ML_HEADS * ML_DIM
LANES = 128

OFF_AQ = 0
OFF_AK = OFF_AQ + ATTN_Q
OFF_AV = OFF_AK + ATTN_KV
OFF_MQ = OFF_AV + ATTN_KV
OFF_MK = OFF_MQ + ML_W
OFF_MV = OFF_MK + ML_W
OFF_MO = OFF_MV + ML_W
OFF_MI = OFF_MO + ML_W
OFF_MF = OFF_MI + ML_HEADS
OFF_GA = OFF_MF + ML_HEADS
OFF_GM = OFF_GA + D_MODEL
D_IN = OFF_GM + D_MODEL
N_PROJ = OFF_MI + LANES

VMEM_LIMIT = 60 * 1024 * 1024
PROMPT_TILE = 256
DECODE_ROWS = 32

def _rms(x, g):
    r = lax.rsqrt(jnp.mean(x * x, axis=-1, keepdims=True) + EPS)
    return (x * r) * g


def _mm(a, b):
    return jnp.dot(a, b, preferred_element_type=F32)


def _log_sigmoid(x):
    return jnp.minimum(x, 0.0) - jnp.log1p(jnp.exp(-jnp.abs(x)))


def _weave(*streams):
    live = list(streams)
    while live:
        for g in list(live):
            try:
                next(g)
            except StopIteration:
                live.remove(g)


class _AwaitedRef:
    def __init__(self, ref, copy):
        self._ref, self._copy = ref, copy

    def __getitem__(self, idx):
        if self._copy is not None:
            self._copy.wait()
            self._copy = None
        return self._ref[idx]


WEIGHT_ROW_SPLIT = 8


def _weight_cast_kernel(*refs):
    n_plain = (len(refs) - 3) // 2
    w_in_t_ref, plain_in = refs[0], refs[1:1 + n_plain]
    w_proj_ref, w_gate_ref, plain_out = refs[1 + n_plain], refs[2 + n_plain], refs[3 + n_plain:]
    for g in range(N_PROJ // LANES):
        w_proj_ref[:, g * LANES:(g + 1) * LANES] = w_in_t_ref[g * LANES:(g + 1) * LANES, :].T.astype(BF16)
    for g in range(2 * D_MODEL // LANES):
        w_gate_ref[:, g * LANES:(g + 1) * LANES] = w_in_t_ref[OFF_GA + g * LANES:OFF_GA + (g + 1) * LANES, :].T.astype(BF16)
    for src, dst in zip(plain_in, plain_out):
        dst[...] = src[...].astype(BF16)


def _weight_cast(w_in_t, plain):
    split = WEIGHT_ROW_SPLIT
    assert D_MODEL // split == LANES
    rows_blk = lambda w: pl.BlockSpec((w.shape[0] // split, w.shape[1]), lambda i: (i, 0))
    out_shapes = (jax.ShapeDtypeStruct((D_MODEL, N_PROJ), BF16), jax.ShapeDtypeStruct((D_MODEL, 2 * D_MODEL), BF16),
                  *(jax.ShapeDtypeStruct(w.shape, BF16) for w in plain))
    return pl.pallas_call(
        _weight_cast_kernel,
        grid=(split,),
        in_specs=[pl.BlockSpec((D_IN, LANES), lambda i: (0, i))] + [rows_blk(w) for w in plain],
        out_specs=tuple(rows_blk(s) for s in out_shapes),
        out_shape=out_shapes,
        compiler_params=pltpu.CompilerParams(dimension_semantics=("parallel",), vmem_limit_bytes=VMEM_LIMIT),
        name="weight_cast",
    )(w_in_t, *plain)


def _in_proj_stages(x, g_ref, w_ref, b_ref, cos_ref, sin_ref,
                    q_ref, k_ref, v_ref, kl_ref, vl_ref, mq_ref, mk_ref, mv_ref, og_ref, gif_ref, gt_ref,
                    *, feature_major):
    xn = _rms(x, g_ref[...]).astype(BF16)
    tm = x.shape[0]

    def proj(lo, hi):
        return _mm(xn, w_ref[:, lo:hi]) + b_ref[:, lo:hi]

    cos = cos_ref[...]
    sin = sin_ref[...]
    lane = lax.broadcasted_iota(jnp.int32, (1, LANES), 1)
    first_half = (lane % HEAD_DIM) < (HEAD_DIM // 2)

    def rope(z):
        partner = jnp.where(first_half, pltpu.roll(z, LANES - HEAD_DIM // 2, 1), pltpu.roll(z, HEAD_DIM // 2, 1))
        return z * cos + partner * sin

    zq = proj(OFF_AQ, OFF_AK)
    for c in range(ATTN_Q // LANES):
        sl = slice(c * LANES, (c + 1) * LANES)
        qc = rope(zq[:, sl]) * (HEAD_DIM ** -0.5)
        if feature_major:
            q_ref[sl, :] = qc.T.astype(q_ref.dtype)
        else:
            kv = (2 * c) // GQA_GROUP
            own = (lane // HEAD_DIM) == kv
            swapped = pltpu.roll(qc, HEAD_DIM, 1)
            for j, src in enumerate((qc, swapped) if kv == 0 else (swapped, qc)):
                hd = 2 * c + j
                q_ref[:, hd * LANES:(hd + 1) * LANES] = jnp.where(own, src, 0.0).astype(q_ref.dtype)
    yield
    zkv = proj(OFF_AK, OFF_MQ)
    k = rope(zkv[:, :ATTN_KV])
    v = zkv[:, ATTN_KV:]
    k_ref[...] = k.astype(k_ref.dtype)
    v_ref[...] = (v.T if feature_major else v).astype(v_ref.dtype)
    if feature_major:
        kl_ref[0] = k[tm - WINDOW:, :].T
        vl_ref[0] = v[tm - WINDOW:, :].T
    else:
        kl_ref[...] = k[tm - WINDOW:, :]
        vl_ref[...] = v[tm - WINDOW:, :]
    yield
    for z_off, dst in ((OFF_MQ, mq_ref), (OFF_MV, mv_ref)):
        z = proj(z_off, z_off + ML_W)
        if feature_major:
            for h in range(ML_HEADS):
                hs = slice(h * ML_DIM, (h + 1) * ML_DIM)
                dst[0, hs, :] = z[:, hs].T.astype(dst.dtype)
        else:
            dst[...] = z.astype(dst.dtype)
        yield
    mk_ref[...] = (proj(OFF_MK, OFF_MV) * (ML_DIM ** -0.5)).astype(mk_ref.dtype)
    yield
    og_ref[...] = jax.nn.sigmoid(proj(OFF_MO, OFF_MI))
    yield
    zg = proj(OFF_MI, N_PROJ)
    gif = jnp.where(lane < ML_HEADS, zg, jnp.where(lane < 2 * ML_HEADS, _log_sigmoid(zg), 0.0))
    if feature_major:
        L = ML_CHUNK
        upper = (lax.broadcasted_iota(jnp.int32, (L, L), 0) <= lax.broadcasted_iota(jnp.int32, (L, L), 1)).astype(F32)
        row8 = lax.broadcasted_iota(jnp.int32, (8, 1), 0)
        fill = jnp.zeros((L - 16, L), F32)
        for c in range(tm // L):
            cs = slice(c * L, (c + 1) * L)
            g_rows = gif[cs, :].T[0:8, :]
            cum = jnp.dot(g_rows, upper, precision=lax.Precision.HIGHEST, preferred_element_type=F32)
            rows = jnp.where(row8 < ML_HEADS, g_rows, cum)
            gt_ref[0, :, cs] = rows
            diff = rows - pltpu.roll(rows, ML_HEADS, 0)
            gif_ref[cs, :] = jnp.concatenate([rows, diff, fill], axis=0).T
    else:
        gif_ref[...] = gif
        gt_ref[0] = jnp.zeros(gt_ref.shape[1:], F32)
    yield


def _in_proj_kernel(x_ref, *refs, feature_major):
    _weave(_in_proj_stages(x_ref[...], *refs, feature_major=feature_major))


def _in_proj_sample(x, g, w, b, cos, sin):
    rows = x.shape[0]
    assert rows == WINDOW
    full = lambda n, dt: (jax.ShapeDtypeStruct((rows, n), dt), pl.BlockSpec((rows, n), lambda i: (0, 0)))
    outs = (
        full(N_Q_HEADS * LANES, BF16), full(ATTN_KV, BF16), full(ATTN_KV, BF16), full(ATTN_KV, F32), full(ATTN_KV, F32),
        full(ML_W, F32), full(ML_W, F32), full(ML_W, F32), full(ML_W, F32), full(LANES, F32),
        (jax.ShapeDtypeStruct((1, 8, rows), F32), pl.BlockSpec((1, 8, rows), lambda i: (0, 0, 0))),
    )
    return pl.pallas_call(
        functools.partial(_in_proj_kernel, feature_major=False),
        grid=(1,),
        in_specs=[pl.BlockSpec(a.shape, lambda i: (0, 0)) for a in (x, g, w, b, cos, sin)],
        out_specs=tuple(o[1] for o in outs),
        out_shape=tuple(o[0] for o in outs),
        compiler_params=pltpu.CompilerParams(dimension_semantics=("arbitrary",), vmem_limit_bytes=VMEM_LIMIT),
        name="in_proj",
    )(x, g, w, b, cos, sin)


def _attn_stages(sink_ref, qt_ref, kp_ref, kc_ref, vtp_ref, vtc_ref, first_block, emit):
    L = WINDOW
    nsub = kc_ref.shape[0] // L
    key = lax.broadcasted_iota(jnp.int32, (2 * L, L), 0)
    t = lax.broadcasted_iota(jnp.int32, (2 * L, L), 1)
    is_prev = key < L
    allowed = (is_prev & (key >= t)) | (~is_prev & (key - L <= t))
    bias = jnp.where(allowed, 0.0, -jnp.inf)
    bias_first = jnp.where(is_prev & first_block, -jnp.inf, bias)
    zeros_q = jnp.zeros((HEAD_DIM, L), BF16)
    heads = range(N_Q_HEADS)
    for r in range(nsub):
        ts = slice(r * L, (r + 1) * L)
        k_prev = kp_ref[...] if r == 0 else kc_ref[(r - 1) * L:r * L, :]
        vt_prev = vtp_ref[...] if r == 0 else vtc_ref[:, (r - 1) * L:r * L]
        k2 = jnp.concatenate([k_prev, kc_ref[ts, :]], axis=0)
        vt2 = jnp.concatenate([vt_prev, vtc_ref[:, ts]], axis=1)
        b_r = bias_first if r == 0 else bias
        q_op = [jnp.concatenate([qt_ref[hd * HEAD_DIM:(hd + 1) * HEAD_DIM, ts], zeros_q][::1 if hd < GQA_GROUP else -1],
                                axis=0) for hd in heads]
        s = [_mm(k2, q_op[hd]) + b_r for hd in heads]
        yield
        m = [jnp.maximum(jnp.max(s[hd], axis=0, keepdims=True), sink_ref[hd]) for hd in heads]
        yield
        p = [jnp.exp(s[hd] - m[hd]) for hd in heads]
        yield
        den = [jnp.sum(p[hd], axis=0, keepdims=True) + jnp.exp(sink_ref[hd] - m[hd]) for hd in heads]
        o = [_mm(vt2[(hd // GQA_GROUP) * HEAD_DIM:(hd // GQA_GROUP + 1) * HEAD_DIM, :], p[hd].astype(BF16))
             * (1.0 / den[hd]) for hd in heads]
        yield
        emit(r, jnp.concatenate(o, axis=0).T)
        yield


ML_STATE_ROWS = ML_DIM + 8


def _mlstm_stages(chains, emit):
    L = ML_CHUNK
    s_idx = lax.broadcasted_iota(jnp.int32, (L, L), 0)
    t_idx = lax.broadcasted_iota(jnp.int32, (L, L), 1)
    causal = s_idx <= t_idx
    n = range(len(chains))
    dlog = [jnp.where(causal, ch["b_r"] + ch["u_c"], -jnp.inf) for ch in chains]
    yield
    qk = [_mm(ch["k"], ch["qt"]) for ch in chains]
    inter = [_mm(ch["state"].astype(BF16), ch["qt"]) for ch in chains]
    yield
    a = [ch["b_r"] + ch["m_prev"] for ch in chains]
    m_t = [jnp.maximum(a[i], jnp.max(dlog[i], axis=0, keepdims=True)) for i in n]
    yield
    sw = [qk[i] * jnp.exp(dlog[i] - m_t[i]) for i in n]
    yield
    sv = [_mm(chains[i]["vt"], sw[i].astype(BF16)) for i in n]
    yield
    hs = []
    for i in n:
        aw = jnp.exp(a[i] - m_t[i])
        num = aw * inter[i][:ML_DIM] + sv[i]
        den = aw * inter[i][ML_DIM:ML_DIM + 1] + jnp.sum(sw[i], axis=0, keepdims=True)
        hh = num * (1.0 / jnp.maximum(jnp.abs(den), jnp.exp(-m_t[i])))
        hs.append(hh.T * chains[i]["og"])
    yield
    pad = jnp.zeros((ML_STATE_ROWS - ML_DIM - 1, L), F32)
    new_state, new_m = [], []
    for i in n:
        ch = chains[i]
        b_end = ch["b_r"][:, L - 1:L]
        g = b_end - ch["b_r"] + ch["ig_r"]
        m_new = jnp.maximum(b_end + ch["m_prev"], jnp.max(g, axis=1, keepdims=True))
        decay = jnp.exp(b_end + ch["m_prev"] - m_new)
        w = jnp.exp(g - m_new)
        v_aug = jnp.concatenate([ch["vt"].astype(F32) * w, w, pad], axis=0).astype(BF16)
        new_state.append(decay * ch["state"] + _mm(v_aug, ch["k"]))
        new_m.append(m_new)
    emit(hs, new_state, new_m)
    yield


DENSE_COLS = 512


def _dense_stages(x_ref, a_ref, hm_ref, p_ref, gmix_ref, wg_ref, bg_ref, wau_ref, wmu_ref, wo_ref,
                  gmlp_ref, wff1_ref, wff2_ref, gple_ref, wpg_ref, wpp_ref, gfin_ref, emit):
    nc = DENSE_COLS
    pieces = lambda n: [slice(c, c + nc) for c in range(0, n, nc)]
    xn = _rms(x_ref[...], gmix_ref[...]).astype(BF16)
    gates = []
    for cs in pieces(2 * D_MODEL):
        gates.append(jax.nn.sigmoid(_mm(xn, wg_ref[:, cs]) + bg_ref[:, cs]))
        yield
    half = D_MODEL // nc
    mix = []
    for j, cs in enumerate(pieces(D_MODEL)):
        mix.append((gates[j] * _mm(a_ref[...], wau_ref[:, cs])
                    + gates[half + j] * _mm(hm_ref[...], wmu_ref[:, cs])).astype(BF16))
        yield
    mix = jnp.concatenate(mix, axis=1)
    h = []
    for cs in pieces(D_MODEL):
        h.append(x_ref[:, cs] + _mm(mix, wo_ref[:, cs]))
        yield
    h = jnp.concatenate(h, axis=1)
    hn = _rms(h, gmlp_ref[...]).astype(BF16)
    u = []
    for cs in pieces(D_FF):
        u.append(jnp.square(jnp.maximum(_mm(hn, wff1_ref[:, cs]), 0.0)).astype(BF16))
        yield
    u = jnp.concatenate(u, axis=1)
    h2 = []
    for cs in pieces(D_MODEL):
        h2.append(h[:, cs] + _mm(u, wff2_ref[:, cs]))
        yield
    h = jnp.concatenate(h2, axis=1)
    hn = _rms(h, gple_ref[...]).astype(BF16)
    pb = p_ref[...].astype(BF16)
    h3 = []
    for cs in pieces(D_MODEL):
        h3.append(h[:, cs] + _mm(pb, wpp_ref[:, cs]) * jax.nn.sigmoid(_mm(hn, wpg_ref[:, cs])))
        yield
    emit(_rms(jnp.concatenate(h3, axis=1), gfin_ref[...]))
    yield


def _out_kernel(x_ref, a_ref, hm_ref, p_ref, *rest, matrix_idx):
    weights, y_ref = list(rest[:N_FUSED_WEIGHTS]), rest[N_FUSED_WEIGHTS]
    w_scr, dma_sem = rest[N_FUSED_WEIGHTS + 1:-1], rest[-1]
    copies = [pltpu.make_async_copy(weights[j], w_scr[n], dma_sem.at[n]) for n, j in enumerate(matrix_idx)]
    for c in copies:
        c.start()
    for n, j in enumerate(matrix_idx):
        weights[j] = _AwaitedRef(w_scr[n], copies[n])

    def emit(y):
        y_ref[...] = y

    _weave(_dense_stages(x_ref, a_ref, hm_ref, p_ref, *weights, emit=emit))


def _out_proj_sample(x, a, hm, p, weights):
    assert len(weights) == N_FUSED_WEIGHTS
    matrix_idx = tuple(j for j, w in enumerate(weights) if w.dtype == BF16)
    matrices = [weights[j] for j in matrix_idx]
    whole = lambda o: pl.BlockSpec(o.shape, lambda i: (0, 0), pipeline_mode=pl.Buffered(1))
    return pl.pallas_call(
        functools.partial(_out_kernel, matrix_idx=matrix_idx),
        grid=(1,),
        in_specs=[whole(o) for o in (x, a, hm, p)]
        + [pl.BlockSpec(memory_space=pl.ANY) if j in matrix_idx else whole(w) for j, w in enumerate(weights)],
        out_specs=pl.BlockSpec(x.shape, lambda i: (0, 0)),
        out_shape=jax.ShapeDtypeStruct(x.shape, F32),
        scratch_shapes=[*(pltpu.VMEM(w.shape, w.dtype) for w in matrices), pltpu.SemaphoreType.DMA((len(matrices),))],
        compiler_params=pltpu.CompilerParams(dimension_semantics=("arbitrary",), vmem_limit_bytes=VMEM_LIMIT),
        name="out_proj",
    )(x, a, hm, p, *weights)


N_FUSED_WEIGHTS = 13
N_PROJ_SLOTS = 9


def _prompt_kernel(sink_ref, x0_ref, cos_ref, sin_ref, x2_ref, p_ref, gmix_ref, wp_ref, bp_ref, *rest,
                   tiles_per_seq, matrix_idx):
    weights = list(rest[:N_FUSED_WEIGHTS])
    y_ref, kl_ref, vl_ref, cst_ref, nst_ref, mst_ref = rest[N_FUSED_WEIGHTS:N_FUSED_WEIGHTS + 6]
    scr = rest[N_FUSED_WEIGHTS + 6:]
    proj_scr = scr[:N_PROJ_SLOTS]
    kprev_scr, vtprev_scr, a_scr, h_scr, c_scr, m_scr = scr[N_PROJ_SLOTS:N_PROJ_SLOTS + 6]
    w_scr, dma_sem = scr[N_PROJ_SLOTS + 6:-1], scr[-1]
    w_copies = [pltpu.make_async_copy(weights[j], w_scr[n], dma_sem.at[n]) for n, j in enumerate(matrix_idx)]
    for n, j in enumerate(matrix_idx):
        weights[j] = w_scr[n]
    i = pl.program_id(0)
    n_tiles = pl.num_programs(0) - 2
    t1 = jnp.clip(i - 1, 0, n_tiles - 1)
    tile_in_seq = t1 % tiles_per_seq
    s0, s1 = i % 2, (i + 1) % 2
    L = ML_CHUNK
    tm = x0_ref.shape[0]

    @pl.when(i == 0)
    def _():
        for c in w_copies:
            c.start()
        for r in (*proj_scr, kprev_scr, vtprev_scr, a_scr, h_scr, c_scr, m_scr):
            r[...] = jnp.zeros_like(r)

    @pl.when(i == 2)
    def _():
        for c in w_copies:
            c.wait()

    @pl.when(tile_in_seq == 0)
    def _():
        c_scr[...] = jnp.zeros_like(c_scr)
        m_scr[...] = jnp.zeros_like(m_scr)

    def projection():
        outs = [r.at[s0] for r in proj_scr]
        yield from _in_proj_stages(x0_ref[...], gmix_ref, wp_ref, bp_ref, cos_ref, sin_ref,
                                   outs[0], outs[1], outs[2], kl_ref, vl_ref, *outs[3:], feature_major=True)

    def branches():
        qt_ref, kc_ref, vtc_ref, mqt_ref, mk_ref, mvt_ref, og_ref, gif_ref, gt_ref = [r.at[s1] for r in proj_scr]
        slot = s1

        def emit_attn(r, blk):
            a_scr[slot, r * WINDOW:(r + 1) * WINDOW, :] = blk.astype(a_scr.dtype)

        yield from _attn_stages(sink_ref, qt_ref, kprev_scr, kc_ref, vtprev_scr, vtc_ref, tile_in_seq == 0, emit_attn)
        kprev_scr[...] = kc_ref[tm - WINDOW:, :]
        vtprev_scr[...] = vtc_ref[:, tm - WINDOW:]
        hsl = lambda h: slice(h * ML_DIM, (h + 1) * ML_DIM)
        carry = dict(state=[c_scr[h] for h in range(ML_HEADS)], m=[m_scr[h:h + 1, :] for h in range(ML_HEADS)])
        for c in range(tm // L):
            cs = slice(c * L, (c + 1) * L)

            def emit_ml(hs, new_state, new_m, cs=cs):
                for h in range(ML_HEADS):
                    h_scr[slot, cs, hsl(h)] = hs[h].astype(h_scr.dtype)
                carry["state"], carry["m"] = new_state, new_m

            chains = [dict(qt=mqt_ref[0, hsl(h), cs], k=mk_ref[cs, hsl(h)], vt=mvt_ref[0, hsl(h), cs],
                           og=og_ref[cs, hsl(h)], ig_r=gt_ref[0, h:h + 1, cs],
                           b_r=gt_ref[0, ML_HEADS + h:ML_HEADS + h + 1, cs],
                           u_c=gif_ref[cs, 2 * ML_HEADS + h:2 * ML_HEADS + h + 1],
                           state=carry["state"][h], m_prev=carry["m"][h]) for h in range(ML_HEADS)]
            yield from _mlstm_stages(chains, emit_ml)
        for h in range(ML_HEADS):
            c_scr[h] = carry["state"][h]
            m_scr[h:h + 1, :] = carry["m"][h]

    def dense():
        def emit_y(y):
            y_ref[...] = y

        return _dense_stages(x2_ref, a_scr.at[s0], h_scr.at[s0], p_ref, *weights, emit=emit_y)

    @pl.when(i < 2)
    def _():
        _weave(branches(), projection())

    @pl.when((i >= 2) & (i < n_tiles))
    def _():
        _weave(branches(), dense(), projection())

    @pl.when(i >= n_tiles)
    def _():
        _weave(branches(), dense())

    @pl.when((tile_in_seq == tiles_per_seq - 1) & (i >= 1) & (i <= n_tiles))
    def _():
        nst_ref[...] = jnp.zeros_like(nst_ref)
        for h in range(ML_HEADS):
            final = c_scr[h]
            cst_ref[0, h] = final[:ML_DIM].T
            nst_ref[0, h:h + 1, :] = final[ML_DIM:ML_DIM + 1]
        mst_ref[0] = m_scr[...]


def _prompt_fused(sinks, x, p, cos, sin, g_mix, w_proj, b_proj, weights, batch, seq, tm):
    assert len(weights) == N_FUSED_WEIGHTS
    rows = batch * seq
    n_tiles = rows // tm
    tps = seq // tm
    t0 = lambda i: jnp.minimum(i, n_tiles - 1)
    t1 = lambda i: jnp.clip(i - 1, 0, n_tiles - 1)
    t2 = lambda i: jnp.maximum(i - 2, 0)
    const = lambda i: (0, 0)
    single = lambda a: pl.BlockSpec(a.shape, const, pipeline_mode=pl.Buffered(1))
    slot = lambda shape, dt: pltpu.VMEM((2,) + shape, dt)
    matrix_idx = tuple(j for j, w in enumerate(weights) if w.dtype == BF16)
    matrices = [weights[j] for j in matrix_idx]
    return pl.pallas_call(
        functools.partial(_prompt_kernel, tiles_per_seq=tps, matrix_idx=matrix_idx),
        grid=(n_tiles + 2,),
        in_specs=[
            pl.BlockSpec(memory_space=pltpu.SMEM),
            pl.BlockSpec((tm, D_MODEL), lambda i: (t0(i), 0)),
            pl.BlockSpec((tm, LANES), lambda i: (t0(i) % tps, 0)),
            pl.BlockSpec((tm, LANES), lambda i: (t0(i) % tps, 0)),
            pl.BlockSpec((tm, D_MODEL), lambda i: (t2(i), 0)),
            pl.BlockSpec((tm, D_PLE), lambda i: (t2(i), 0)),
            single(g_mix), single(w_proj), single(b_proj),
        ] + [pl.BlockSpec(memory_space=pl.ANY) if j in matrix_idx else single(w) for j, w in enumerate(weights)],
        out_specs=(
            pl.BlockSpec((tm, D_MODEL), lambda i: (t2(i), 0)),
            pl.BlockSpec((1, ATTN_KV, WINDOW), lambda i: (t0(i) // tps, 0, 0)),
            pl.BlockSpec((1, ATTN_KV, WINDOW), lambda i: (t0(i) // tps, 0, 0)),
            pl.BlockSpec((1, ML_HEADS, ML_DIM, ML_DIM), lambda i: (t1(i) // tps, 0, 0, 0)),
            pl.BlockSpec((1, 8, LANES), lambda i: (t1(i) // tps, 0, 0)),
            pl.BlockSpec((1, 8, LANES), lambda i: (t1(i) // tps, 0, 0)),
        ),
        out_shape=(
            jax.ShapeDtypeStruct((rows, D_MODEL), F32),
            jax.ShapeDtypeStruct((batch, ATTN_KV, WINDOW), F32),
            jax.ShapeDtypeStruct((batch, ATTN_KV, WINDOW), F32),
            jax.ShapeDtypeStruct((batch, ML_HEADS, ML_DIM, ML_DIM), F32),
            jax.ShapeDtypeStruct((batch, 8, LANES), F32),
            jax.ShapeDtypeStruct((batch, 8, LANES), F32),
        ),
        scratch_shapes=[
            slot((ATTN_Q, tm), BF16), slot((tm, ATTN_KV), BF16), slot((ATTN_KV, tm), BF16),
            slot((1, ML_W, tm), BF16), slot((tm, ML_W), BF16), slot((1, ML_W, tm), BF16),
            slot((tm, ML_W), F32), slot((tm, LANES), F32), slot((1, 8, tm), F32),
            pltpu.VMEM((WINDOW, ATTN_KV), BF16), pltpu.VMEM((ATTN_KV, WINDOW), BF16),
            slot((tm, ATTN_Q), BF16), slot((tm, ML_W), BF16),
            pltpu.VMEM((ML_HEADS, ML_STATE_ROWS, ML_DIM), F32), pltpu.VMEM((8, LANES), F32),
            *(pltpu.VMEM(w.shape, w.dtype) for w in matrices), pltpu.SemaphoreType.DMA((len(matrices),)),
        ],
        compiler_params=pltpu.CompilerParams(dimension_semantics=("arbitrary",), vmem_limit_bytes=VMEM_LIMIT),
        name="prompt_fused",
    )(sinks, x, cos, sin, x, p, g_mix, w_proj, b_proj, *weights)


def _attn_decode_kernel(sink_ref, q_ref, kn_ref, vn_ref, ck_ref, cv_ref, o_ref, ko_ref, vo_ref, o_scr, qm_scr):
    bt = q_ref.shape[0]
    for hd in range(N_Q_HEADS):
        qm_scr[:, hd, :] = q_ref[:, hd * LANES:(hd + 1) * LANES]
    qm = qm_scr[...]
    kn = kn_ref[...]
    vn = vn_ref[...]
    ck = ck_ref[...]
    cv = cv_ref[...]
    head = lax.broadcasted_iota(jnp.int32, (N_Q_HEADS, 1), 0)
    sink = jnp.zeros((N_Q_HEADS, 1), F32)
    for hd in range(N_Q_HEADS):
        sink = jnp.where(head == hd, sink_ref[hd], sink)
    s = jnp.einsum('bhc,bck->bhk', qm, ck.astype(BF16), preferred_element_type=F32)
    s_new = jnp.sum(qm.astype(F32) * kn[:, None, :], axis=2, keepdims=True)
    m = jnp.maximum(jnp.maximum(jnp.max(s, axis=2, keepdims=True), s_new), sink)
    p = jnp.exp(s - m)
    p_new = jnp.exp(s_new - m)
    den = jnp.sum(p, axis=2, keepdims=True) + p_new + jnp.exp(sink - m)
    o = jnp.einsum('bhk,bck->bhc', p.astype(BF16), cv.astype(BF16), preferred_element_type=F32)
    o_scr[...] = (o + p_new * vn[:, None, :]) / den
    low = lax.broadcasted_iota(jnp.int32, (1, LANES), 1) < HEAD_DIM
    for c in range(N_Q_HEADS // 2):
        even, odd = o_scr[:, 2 * c, :], o_scr[:, 2 * c + 1, :]
        if (2 * c) // GQA_GROUP == 0:
            odd = pltpu.roll(odd, HEAD_DIM, 1)
        else:
            even = pltpu.roll(even, HEAD_DIM, 1)
        o_ref[:, c * LANES:(c + 1) * LANES] = jnp.where(low, even, odd).astype(o_ref.dtype)
    pad = jnp.zeros((LANES - bt, ATTN_KV), F32)
    kn_t = jnp.concatenate([kn, pad], axis=0).T
    vn_t = jnp.concatenate([vn, pad], axis=0).T
    newest = lax.broadcasted_iota(jnp.int32, (ATTN_KV, WINDOW), 1) == WINDOW - 1
    for b in range(bt):
        ko_ref[b] = jnp.where(newest, kn_t[:, b:b + 1], pltpu.roll(ck[b], WINDOW - 1, 1))
        vo_ref[b] = jnp.where(newest, vn_t[:, b:b + 1], pltpu.roll(cv[b], WINDOW - 1, 1))


def _attn_decode(sinks, q_heads, k_new, v_new, cache_k, cache_v, bt):
    nbatch = q_heads.shape[0]
    b3 = lambda i: (i, 0, 0)
    b2 = lambda i: (i, 0)
    return pl.pallas_call(
        _attn_decode_kernel,
        grid=(nbatch // bt,),
        in_specs=[
            pl.BlockSpec(memory_space=pltpu.SMEM),
            pl.BlockSpec((bt, N_Q_HEADS * LANES), b2),
            pl.BlockSpec((bt, ATTN_KV), b2),
            pl.BlockSpec((bt, ATTN_KV), b2),
            pl.BlockSpec((bt, WINDOW, ATTN_KV), b3),
            pl.BlockSpec((bt, WINDOW, ATTN_KV), b3),
        ],
        out_specs=(
            pl.BlockSpec((bt, ATTN_Q), b2),
            pl.BlockSpec((bt, WINDOW, ATTN_KV), b3),
            pl.BlockSpec((bt, WINDOW, ATTN_KV), b3),
        ),
        out_shape=(
            jax.ShapeDtypeStruct((nbatch, ATTN_Q), BF16),
            jax.ShapeDtypeStruct((nbatch, WINDOW, ATTN_KV), F32),
            jax.ShapeDtypeStruct((nbatch, WINDOW, ATTN_KV), F32),
        ),
        scratch_shapes=[pltpu.VMEM((bt, N_Q_HEADS, ATTN_KV), F32), pltpu.VMEM((bt, N_Q_HEADS, ATTN_KV), BF16)],
        compiler_params=pltpu.CompilerParams(dimension_semantics=("parallel",)),
        name="attn_decode",
    )(sinks, q_heads, k_new, v_new, cache_k, cache_v)


def _mlstm_decode_kernel(q_ref, k_ref, v_ref, og_ref, gif_ref, c_ref, n_ref, m_ref, h_ref, co_ref, no_ref, mo_ref):
    bt = q_ref.shape[0]
    gif = gif_ref[...]
    ig = gif[:, 0:ML_HEADS]
    lf = gif[:, ML_HEADS:2 * ML_HEADS]
    m_prev = m_ref[...]
    a = lf + m_prev
    m_t = jnp.maximum(a, ig)
    aw = jnp.exp(a - m_t)
    e_i = jnp.exp(ig - m_t)
    floor = jnp.exp(-m_t)
    m_new = jnp.maximum(a, ig)
    decay = jnp.exp(a - m_new)
    w = jnp.exp(ig - m_new)
    mo_ref[...] = m_new
    pad = jnp.zeros((LANES - bt, ML_DIM), F32)
    lane = lax.broadcasted_iota(jnp.int32, (ML_DIM, LANES), 1)
    for h in range(ML_HEADS):
        hs = slice(h * ML_DIM, (h + 1) * ML_DIM)
        q = q_ref[:, hs]
        k = k_ref[:, hs]
        v = v_ref[:, hs]
        n = n_ref[:, h, :]
        qk = jnp.sum(q * k, axis=1, keepdims=True)
        qn = jnp.sum(q * n, axis=1, keepdims=True)
        sw = qk * e_i[:, h:h + 1]
        den = aw[:, h:h + 1] * qn + sw
        inv = 1.0 / jnp.maximum(jnp.abs(den), floor[:, h:h + 1])
        no_ref[:, h, :] = decay[:, h:h + 1] * n + w[:, h:h + 1] * k
        q_b = q.astype(BF16)
        wk_t = jnp.concatenate([w[:, h:h + 1] * k, pad], axis=0).T.astype(BF16)
        v_rows = jnp.concatenate([v, pad], axis=0).astype(BF16)
        nums = []
        for b in range(bt):
            C = c_ref[b, h]
            q_c = _mm(q_b, C.astype(BF16))[b:b + 1, :]
            nums.append(aw[b:b + 1, h:h + 1] * q_c)
            outer = _mm(jnp.where(lane == b, wk_t, jnp.zeros_like(wk_t)), v_rows)
            co_ref[b, h] = decay[b:b + 1, h:h + 1] * C + outer
        num = jnp.concatenate(nums, axis=0) + sw * v
        h_ref[:, hs] = (num * inv * og_ref[:, hs]).astype(h_ref.dtype)


def _mlstm_decode(mq, mk, mv, og, gif, state_c, state_n, state_m, bt):
    nbatch = mq.shape[0]
    b2 = lambda i: (i, 0)
    b3 = lambda i: (i, 0, 0)
    b4 = lambda i: (i, 0, 0, 0)
    return pl.pallas_call(
        _mlstm_decode_kernel,
        grid=(nbatch // bt,),
        in_specs=[
            pl.BlockSpec((bt, ML_W), b2),
            pl.BlockSpec((bt, ML_W), b2),
            pl.BlockSpec((bt, ML_W), b2),
            pl.BlockSpec((bt, ML_W), b2),
            pl.BlockSpec((bt, LANES), b2),
            pl.BlockSpec((bt, ML_HEADS, ML_DIM, ML_DIM), b4),
            pl.BlockSpec((bt, ML_HEADS, ML_DIM), b3),
            pl.BlockSpec((bt, ML_HEADS), b2),
        ],
        out_specs=(
            pl.BlockSpec((bt, ML_W), b2),
            pl.BlockSpec((bt, ML_HEADS, ML_DIM, ML_DIM), b4),
            pl.BlockSpec((bt, ML_HEADS, ML_DIM), b3),
            pl.BlockSpec((bt, ML_HEADS), b2),
        ),
        out_shape=(
            jax.ShapeDtypeStruct((nbatch, ML_W), BF16),
            jax.ShapeDtypeStruct((nbatch, ML_HEADS, ML_DIM, ML_DIM), F32),
            jax.ShapeDtypeStruct((nbatch, ML_HEADS, ML_DIM), F32),
            jax.ShapeDtypeStruct((nbatch, ML_HEADS), F32),
        ),
        compiler_params=pltpu.CompilerParams(dimension_semantics=("parallel",)),
        name="mlstm_decode",
    )(mq, mk, mv, og, gif, state_c, state_n, state_m)


def _rope_tables(pos):
    half = HEAD_DIM // 2
    inv = (1.0 / (ROPE_THETA ** (np.arange(half, dtype=np.float32) / half))).astype(np.float32)
    ang = pos.astype(np.float32)[:, None] * inv[None, :]
    cos = np.cos(ang)
    sin = np.sin(ang)
    cos_t = np.concatenate([cos, cos, cos, cos], axis=1)
    sin_t = np.concatenate([-sin, sin, -sin, sin], axis=1)
    return jnp.asarray(cos_t, F32), jnp.asarray(sin_t, F32)


def kernel(x_prompt, x_sample, cache_k, cache_v, state_C, state_n, state_m, p_prompt, p_sample, norm_mix, w_in, b_in, attn_sinks, w_attn_up, w_ml_up, w_o, norm_mlp, w_ff1, w_ff2, norm_ple, w_ple_gate, w_ple_proj, norm_final):
    assert w_in.shape[0] == 1, "single layer"
    batch, seq, _ = x_prompt.shape
    nbatch = x_sample.shape[0]
    assert x_sample.shape[1] == 1

    w_proj, w_gate, w_au, w_mu, w_out, w_f1, w_f2, w_pg, w_pp = _weight_cast(
        w_in[0].T, (w_attn_up[0], w_ml_up[0], w_o[0], w_ff1[0], w_ff2[0], w_ple_gate[0], w_ple_proj[0]))
    b_proj = b_in[:, :N_PROJ]
    b_gate = b_in[:, OFF_GA:]
    g_mix = norm_mix[0][None, :]
    out_weights = (
        g_mix, w_gate, b_gate, w_au, w_mu, w_out,
        norm_mlp[0][None, :], w_f1, w_f2,
        norm_ple[0][None, :], w_pg, w_pp, norm_final[None, :],
    )
    sinks = attn_sinks[0]
    to_fp = lambda c: c[0].transpose(0, 2, 3, 1).reshape(c.shape[1], ATTN_KV, WINDOW)
    from_fp = lambda c: c.reshape(c.shape[0], N_KV_HEADS, HEAD_DIM, WINDOW).transpose(0, 3, 1, 2)[None]

    cos_p, sin_p = _rope_tables(np.arange(seq, dtype=np.int32))
    xp = x_prompt.reshape(batch * seq, D_MODEL)
    y_prompt, k_last, v_last, cst, nst, mst = _prompt_fused(
        sinks, xp, p_prompt[0].reshape(batch * seq, D_PLE), cos_p, sin_p, g_mix, w_proj, b_proj, out_weights,
        batch, seq, tm=PROMPT_TILE)
    y_prompt = y_prompt.reshape(batch, seq, D_MODEL)
    k_prompt = from_fp(k_last)
    v_prompt = from_fp(v_last)
    c_prompt = cst[None]
    n_prompt = nst[:, :ML_HEADS, :][None]
    m_prompt = mst[:, :ML_HEADS, 0][None]

    cos_s, sin_s = _rope_tables(np.full((nbatch,), PAST_LEN, dtype=np.int32))
    xs = x_sample.reshape(nbatch, D_MODEL)
    qs, _, _, ks, vs, mqs, mks, mvs, ogs, gifs, _ = _in_proj_sample(xs, g_mix, w_proj, b_proj, cos_s, sin_s)
    a_s, k_s, v_s = _attn_decode(sinks, qs, ks, vs, to_fp(cache_k), to_fp(cache_v), bt=DECODE_ROWS)
    h_s, c_s, n_s, m_s = _mlstm_decode(mqs, mks, mvs, ogs, gifs, state_C[0], state_n[0], state_m[0], bt=DECODE_ROWS)
    y_sample = _out_proj_sample(xs, a_s, h_s, p_sample[0].reshape(nbatch, D_PLE), out_weights)
    y_sample = y_sample.reshape(nbatch, 1, D_MODEL)
    k_sample = from_fp(k_s)
    v_sample = from_fp(v_s)

    return (y_prompt, y_sample, k_prompt, v_prompt, c_prompt, n_prompt, m_prompt,
            k_sample, v_sample, c_s[None], n_s[None], m_s[None])
```
